```python
import functools
import numpy as np
import jax
import jax.numpy as jnp
from jax import lax

D_MODEL = 1024
BATCH = 16
SEQ = 256
DEPTH = 2
DEC_BATCH = 2
DEC_SEQ = 4096
PAST_LEN = 256

GRID_W = 64
HEAD_DIM = 64
ML_HEADS = 4
ML_WIDTH = ML_HEADS * HEAD_DIM
ML_CHUNK = 128
SW_HEADS = 4
SW_KV_HEADS = 2
SW_GROUP = SW_HEADS // SW_KV_HEADS
SW_WINDOW = 128
SW_BLOCK = 128
RG_WIDTH = 256
RG_BLOCKS = 4
RG_BLOCK_DIM = RG_WIDTH // RG_BLOCKS
RG_CONV = 4
RG_C = 8.0
NA_HEADS = 4
NA_WIDTH = NA_HEADS * HEAD_DIM
NA_ROWS = 8
NA_COLS = 16
NA_QCOLS = 16
NA_KSPAN = 2 * NA_COLS
NA_RPB_R = 2 * NA_ROWS - 1
NA_RPB_C = 2 * NA_COLS - 1
N_BRANCH = 4
BRANCH_WIDTH = 256
ROPE_BASE = 10000.0
CTX_BLOCK = 128
D_FF = 2816
N_EXPERTS = 8
TOP_K = 2
D_FF_EXPERT = 2048
EPS = 1e-6
NEG = -1e30
IN_SIZES = (ML_WIDTH, ML_WIDTH, ML_WIDTH, ML_WIDTH, 2 * ML_HEADS, 2 * ML_HEADS, SW_HEADS * HEAD_DIM, SW_KV_HEADS * HEAD_DIM, SW_KV_HEADS * HEAD_DIM, RG_WIDTH, RG_WIDTH, NA_WIDTH, NA_WIDTH, NA_WIDTH)
D_IN = sum(IN_SIZES)

kernel_name = 'hybrid_diffusion_prefix_step'


def rms_norm(x, g):
    xf = x.astype(jnp.float32)
    y = xf * lax.rsqrt(jnp.mean(xf * xf, axis=-1, keepdims=True) + EPS)
    return (y * g.astype(jnp.float32)).astype(x.dtype)


def rope_2d(x):
    s_len = x.shape[1]
    t = jnp.arange(s_len)
    row, col = t // GRID_W, t % GRID_W
    nf = HEAD_DIM // 4
    freqs = ROPE_BASE ** (-jnp.arange(nf, dtype=jnp.float32) / nf)
    bshape = (1, s_len) + (1,) * (x.ndim - 3) + (nf,)

    def rot(xa, pos):
        ang = (pos.astype(jnp.float32)[:, None] * freqs).reshape(bshape)
        cos, sin = jnp.cos(ang), jnp.sin(ang)
        x1, x2 = xa[..., :nf], xa[..., nf:]
        return jnp.concatenate([x1 * cos - x2 * sin, x1 * sin + x2 * cos], axis=-1)

    xf = x.astype(jnp.float32)
    half = HEAD_DIM // 2
    out = jnp.concatenate([rot(xf[..., :half], row), rot(xf[..., half:], col)], axis=-1)
    return out.astype(x.dtype)


def softmax_sink(s, sink):
    if sink is None:
        return jax.nn.softmax(s, axis=-1)
    kv, grp = s.shape[1], s.shape[2]
    sk = jnp.broadcast_to(sink.astype(jnp.float32).reshape(1, kv, grp, 1, 1), s.shape[:-1] + (1,))
    return jax.nn.softmax(jnp.concatenate([s, sk], axis=-1), axis=-1)[..., :-1]


def dense_attention(q, k, v, sink):
    lq = q.shape[1]
    scale = HEAD_DIM ** -0.5

    def blk(n):
        qb = lax.dynamic_slice_in_dim(q, n * CTX_BLOCK, CTX_BLOCK, axis=1)
        s = jnp.einsum('bqkgd,bjkd->bkgqj', qb, k).astype(jnp.float32) * scale
        p = softmax_sink(s, sink).astype(v.dtype)
        return jnp.einsum('bkgqj,bjkd->bqkgd', p, v)

    o = lax.map(blk, jnp.arange(lq // CTX_BLOCK))
    return jnp.moveaxis(o, 0, 1).reshape(q.shape)


def window_attention(q, k, v, kc, vc, sink):
    s_len = q.shape[1]
    span = SW_BLOCK + 2 * SW_WINDOW
    scale = HEAD_DIM ** -0.5
    pad = ((0, 0), (SW_WINDOW, SW_WINDOW), (0, 0), (0, 0))
    kp, vp = jnp.pad(k, pad), jnp.pad(v, pad)
    kj = np.arange(span)[None, :] - SW_WINDOW
    band = np.abs(np.arange(SW_BLOCK)[:, None] - kj) <= SW_WINDOW

    def blk(n):
        start = n * SW_BLOCK
        qb = lax.dynamic_slice_in_dim(q, start, SW_BLOCK, axis=1)
        kb = lax.dynamic_slice_in_dim(kp, start, span, axis=1)
        vb = lax.dynamic_slice_in_dim(vp, start, span, axis=1)
        kpos = start + kj
        valid = band & (kpos >= 0) & (kpos < s_len)
        s_loc = jnp.einsum('bqkgd,bjkd->bkgqj', qb, kb).astype(jnp.float32) * scale
        s_loc = jnp.where(valid, s_loc, NEG)
        s_ctx = jnp.einsum('bqkgd,bjkd->bkgqj', qb, kc).astype(jnp.float32) * scale
        p = softmax_sink(jnp.concatenate([s_loc, s_ctx], axis=-1), sink).astype(v.dtype)
        return (jnp.einsum('bkgqj,bjkd->bqkgd', p[..., :span], vb)
                + jnp.einsum('bkgqj,bjkd->bqkgd', p[..., span:], vc))

    o = lax.map(blk, jnp.arange(s_len // SW_BLOCK))
    return jnp.moveaxis(o, 0, 1).reshape(q.shape)


def neighbourhood_attention(q, k, v, kc, vc, rpb):
    bsz, s_len, nh, hd = q.shape
    rows = s_len // GRID_W
    kr = min(NA_ROWS, rows)
    ncb = GRID_W // NA_QCOLS
    scale = HEAD_DIM ** -0.5
    rs = np.clip(np.arange(rows) - kr // 2, 0, rows - kr).astype(np.int32)
    cs = np.clip(np.arange(GRID_W) - NA_COLS // 2, 0, GRID_W - NA_COLS)
    ks = np.clip(np.arange(ncb) * NA_QCOLS - NA_COLS // 2, 0, GRID_W - NA_KSPAN)
    col_idx = (ks[:, None] + np.arange(NA_KSPAN)[None, :]).astype(np.int32)
    qcol = np.arange(ncb)[:, None] * NA_QCOLS + np.arange(NA_QCOLS)[None, :]
    kcol = col_idx[:, None, :]
    lo = cs[qcol][..., None]
    col_valid = (kcol >= lo) & (kcol < lo + NA_COLS)
    dc = jnp.asarray((np.clip(kcol - qcol[..., None], -(NA_COLS - 1), NA_COLS - 1) + NA_COLS - 1).astype(np.int32))
    dr = (rs[:, None] + np.arange(kr)[None, :] - np.arange(rows)[:, None] + NA_ROWS - 1).astype(np.int32)
    qg = q.reshape(bsz, rows, ncb, NA_QCOLS, nh, hd)
    kg = k.reshape(bsz, rows, GRID_W, nh, hd)
    vg = v.reshape(bsz, rows, GRID_W, nh, hd)
    n_loc = kr * NA_KSPAN

    def row_block(args):
        r, r0, dr_r = args
        qr = lax.dynamic_index_in_dim(qg, r, axis=1, keepdims=False)
        kb = lax.dynamic_slice_in_dim(kg, r0, kr, axis=1)[:, :, col_idx]
        vb = lax.dynamic_slice_in_dim(vg, r0, kr, axis=1)[:, :, col_idx]
        s_loc = jnp.einsum('bnqhd,binjhd->bhnqij', qr, kb).astype(jnp.float32) * scale
        bias = rpb[:, dr_r[None, None, :, None], dc[:, :, None, :]].astype(jnp.float32)
        s_loc = jnp.where(col_valid[:, :, None, :], s_loc + bias, NEG).reshape(bsz, nh, ncb, NA_QCOLS, n_loc)
        s_ctx = jnp.einsum('bnqhd,bchd->bhnqc', qr, kc).astype(jnp.float32) * scale
        p = jax.nn.softmax(jnp.concatenate([s_loc, s_ctx], axis=-1), axis=-1).astype(v.dtype)
        p_loc = p[..., :n_loc].reshape(bsz, nh, ncb, NA_QCOLS, kr, NA_KSPAN)
        o = (jnp.einsum('bhnqij,binjhd->bnqhd', p_loc, vb)
             + jnp.einsum('bhnqc,bchd->bnqhd', p[..., n_loc:], vc))
        return o.reshape(bsz, GRID_W, nh, hd)

    o = lax.map(row_block, (jnp.arange(rows), jnp.asarray(rs), jnp.asarray(dr)))
    return jnp.moveaxis(o, 0, 1).reshape(bsz, s_len, nh, hd)


def mlstm_scan(q, k, v, logi, logf, c0, n0, m0):
    bsz, l_len, nh, hd = q.shape
    nc = l_len // ML_CHUNK

    def chunks(t):
        t = t.reshape((bsz, nc, ML_CHUNK, nh) + t.shape[3:])
        return jnp.transpose(t, (1, 0, 3, 2) + tuple(range(4, t.ndim)))

    tril = jnp.tril(jnp.ones((ML_CHUNK, ML_CHUNK), dtype=bool))

    def step(carry, xs):
        cm, nv, m = carry
        qc, kc, vc, li, lf = xs
        b = jnp.cumsum(lf, axis=-1)
        dmat = jnp.where(tril, b[..., :, None] - b[..., None, :] + li[..., None, :], NEG)
        inter = b + m[..., None]
        mt = jnp.maximum(inter, jnp.max(dmat, axis=-1))
        w_intra = jnp.exp(dmat - mt[..., None])
        w_prev = jnp.exp(inter - mt)
        s = jnp.einsum('bhtd,bhsd->bhts', qc, kc) * w_intra
        num = w_prev[..., None] * jnp.einsum('bhtd,bhde->bhte', qc, cm) + jnp.einsum('bhts,bhse->bhte', s, vc)
        den = w_prev * jnp.einsum('bhtd,bhd->bht', qc, nv) + jnp.sum(s, axis=-1)
        h = num / jnp.maximum(jnp.abs(den), jnp.exp(-mt))[..., None]
        bl = b[..., -1]
        g = bl[..., None] - b + li
        m_new = jnp.maximum(bl + m, jnp.max(g, axis=-1))
        wk = jnp.exp(g - m_new[..., None])
        wp = jnp.exp(bl + m - m_new)
        c_new = wp[..., None, None] * cm + jnp.einsum('bhs,bhsd,bhse->bhde', wk, kc, vc)
        n_new = wp[..., None] * nv + jnp.einsum('bhs,bhsd->bhd', wk, kc)
        return (c_new, n_new, m_new), h

    (cf, nf, mf), hs = lax.scan(step, (c0, n0, m0), (chunks(q), chunks(k), chunks(v), chunks(logi), chunks(logf)))
    h = jnp.transpose(hs, (1, 0, 3, 2, 4)).reshape(bsz, l_len, nh, hd)
    return h, (cf, nf, mf)


def _lin_combine(left, right):
    a1, b1 = left
    a2, b2 = right
    return a1 * a2, a2 * b1 + b2


def rglru(xc, w_r, b_r, w_i, b_i, lam, h0, reverse):
    bsz, l_len, ch = xc.shape
    xb = xc.reshape(bsz, l_len, RG_BLOCKS, RG_BLOCK_DIM)
    r = jax.nn.sigmoid(jnp.einsum('blnc,nce->blne', xb, w_r).reshape(bsz, l_len, ch) + b_r)
    gi = jax.nn.sigmoid(jnp.einsum('blnc,nce->blne', xb, w_i).reshape(bsz, l_len, ch) + b_i)
    log_a = -RG_C * r * jax.nn.softplus(-lam)
    a = jnp.exp(log_a)
    u = jnp.sqrt(-jnp.expm1(2.0 * log_a)) * (gi * xc)
    if reverse:
        a, u = jnp.flip(a, axis=1), jnp.flip(u, axis=1)
    u = u.at[:, 0].add(a[:, 0] * h0)
    _, hs = lax.associative_scan(_lin_combine, (a, u), axis=1)
    h_last = hs[:, -1]
    if reverse:
        hs = jnp.flip(hs, axis=1)
    return hs, h_last


def centred_conv(x, w, b):
    left = (RG_CONV - 1) // 2
    xp = jnp.pad(x, ((0, 0), (left, RG_CONV - 1 - left), (0, 0)))
    l_len = x.shape[1]
    return sum(xp[:, j:j + l_len] * w[j] for j in range(RG_CONV)) + b


def swiglu(h, w1, w3, w2):
    return (jax.nn.silu(h @ w1) * (h @ w3)) @ w2


def moe_ffn(h, wr, br, w1, w3, w2):
    logits = (h @ wr).astype(jnp.float32) + br
    top_v, top_i = lax.top_k(logits, TOP_K)
    gates = jax.nn.softmax(top_v, axis=-1)
    combine = jnp.sum(jax.nn.one_hot(top_i, N_EXPERTS, dtype=jnp.float32) * gates[..., None], axis=-2)
    y = jnp.zeros_like(h)
    for e in range(N_EXPERTS):
        y = y + combine[..., e:e + 1].astype(h.dtype) * swiglu(h, w1[e], w3[e], w2[e])
    return y


def token_mixers(h, p, ctx):
    bsz, l_len, _ = h.shape
    f32 = jnp.float32
    latent = ctx is not None
    split_points = [int(s) for s in np.cumsum(IN_SIZES)[:-1]]
    (mq, mk, mv, mo, mi, mfg, sq, sk, sv, rx, ry, nq, nk, nv) = jnp.split(h @ p['w_in'], split_points, axis=-1)

    def heads(t, n):
        return t.reshape(bsz, l_len, n, HEAD_DIM)

    def flip(t):
        return jnp.flip(t, axis=1)

    q_a = heads(mq, ML_HEADS).astype(f32)
    k_a = heads(mk, ML_HEADS).astype(f32) * HEAD_DIM ** -0.5
    v_a = heads(mv, ML_HEADS).astype(f32)
    logi = mi.reshape(bsz, l_len, 2, ML_HEADS).astype(f32) + p['ml_b_i']
    logf = jax.nn.log_sigmoid(mfg.reshape(bsz, l_len, 2, ML_HEADS).astype(f32) + p['ml_b_f'])
    if latent:
        c0, n0, m0 = ctx['ml_C'].astype(f32), ctx['ml_n'].astype(f32), ctx['ml_m'].astype(f32)
    else:
        c0 = jnp.zeros((bsz, 2, ML_HEADS, HEAD_DIM, HEAD_DIM), f32)
        n0 = jnp.zeros((bsz, 2, ML_HEADS, HEAD_DIM), f32)
        m0 = jnp.zeros((bsz, 2, ML_HEADS), f32)
    h_f, (c_f, n_f, m_f) = mlstm_scan(q_a, k_a, v_a, logi[:, :, 0], logf[:, :, 0], c0[:, 0], n0[:, 0], m0[:, 0])
    h_b, (c_b, n_b, m_b) = mlstm_scan(flip(q_a), flip(k_a), flip(v_a), flip(logi[:, :, 1]), flip(logf[:, :, 1]), c0[:, 1], n0[:, 1], m0[:, 1])
    hm = rms_norm(h_f + flip(h_b), p['ml_hn']) * jax.nn.sigmoid(heads(mo, ML_HEADS).astype(f32))
    out_a = hm.reshape(bsz, l_len, ML_WIDTH).astype(h.dtype)

    q_b = rms_norm(heads(sq, SW_HEADS), p['sw_qn']).reshape(bsz, l_len, SW_KV_HEADS, SW_GROUP, HEAD_DIM)
    k_b = rms_norm(heads(sk, SW_KV_HEADS), p['sw_kn'])
    v_b = heads(sv, SW_KV_HEADS)
    if latent:
        o_b = window_attention(rope_2d(q_b), rope_2d(k_b), v_b, ctx['sw_k'], ctx['sw_v'], p['sw_sink'])
    else:
        o_b = dense_attention(q_b, k_b, v_b, p['sw_sink'])
    out_b = o_b.reshape(bsz, l_len, SW_HEADS * HEAD_DIM)

    xc = centred_conv(rx, p['rg_conv_w'], p['rg_conv_b']).astype(f32)
    h0 = ctx['rg_h'].astype(f32) if latent else jnp.zeros((bsz, 2, RG_WIDTH), f32)
    hs_f, hl_f = rglru(xc, p['rg_w_r'][0], p['rg_b_r'][0], p['rg_w_i'][0], p['rg_b_i'][0], p['rg_lam'][0], h0[:, 0], False)
    hs_b, hl_b = rglru(xc, p['rg_w_r'][1], p['rg_b_r'][1], p['rg_w_i'][1], p['rg_b_i'][1], p['rg_lam'][1], h0[:, 1], True)
    out_c = ((hs_f + hs_b) * jax.nn.gelu(ry.astype(f32))).astype(h.dtype)

    q_d = rms_norm(heads(nq, NA_HEADS), p['na_qn'])
    k_d = rms_norm(heads(nk, NA_HEADS), p['na_kn'])
    v_d = heads(nv, NA_HEADS)
    if latent:
        o_d = neighbourhood_attention(q_d, k_d, v_d, ctx['na_k'], ctx['na_v'], p['na_rpb'])
    else:
        o_d = dense_attention(q_d[:, :, :, None], k_d, v_d, None)
    out_d = o_d.reshape(bsz, l_len, NA_WIDTH)

    branches = jnp.stack([out_a, out_b, out_c, out_d], axis=2)
    if latent:
        return branches, None
    new_ctx = {'sw_k': k_b, 'sw_v': v_b, 'na_k': k_d, 'na_v': v_d,
               'ml_C': jnp.stack([c_f, c_b], axis=1), 'ml_n': jnp.stack([n_f, n_b], axis=1),
               'ml_m': jnp.stack([m_f, m_b], axis=1), 'rg_h': jnp.stack([hl_f, hl_b], axis=1)}
    return branches, new_ctx


def trunk_layer(x, mod, p, ffn, ctx):
    sh1, sc1, g1, sh2, sc2, g2 = jnp.split(mod, 6, axis=-1)
    h = rms_norm(x, p['norm1']) * (1 + sc1) + sh1
    branches, new_ctx = token_mixers(h, p, ctx)
    yb = jnp.einsum('blnw,nwd->blnd', branches, p['w_br'])
    gates = jax.nn.sigmoid(h @ p['w_mg'] + p['b_mg']).reshape(yb.shape)
    y = jnp.sum(gates * yb, axis=2) @ p['w_out']
    x = x + g1 * y
    h2 = rms_norm(x, p['norm2']) * (1 + sc2) + sh2
    x = x + g2 * ffn(h2)
    return x, new_ctx


def setup_inputs(seed: int = 0) -> dict:
    key = jax.random.key(seed)
    ks = iter(jax.random.split(key, 64))

    def nrm(shape, scale=1.0):
        return jax.random.normal(next(ks), shape, jnp.float32) * scale

    d = D_MODEL
    n_dense = (DEPTH + 1) // 2
    n_moe = DEPTH // 2
    u = jax.random.uniform(next(ks), (DEPTH, 2, RG_WIDTH), jnp.float32, minval=0.9, maxval=0.999)
    s = u ** (1.0 / RG_C)
    rg_lam = jnp.log(s) - jnp.log1p(-s)
    ml_b_f = jnp.linspace(3.0, 6.0, ML_HEADS)[None, None, :] + nrm((DEPTH, 2, ML_HEADS), 0.1)
    return {
        'x_prompt': nrm((BATCH, SEQ, d)),
        'x_sample': nrm((DEC_BATCH, DEC_SEQ, d)),
        'cache_swa_k': nrm((DEC_BATCH, DEPTH, PAST_LEN, SW_KV_HEADS, HEAD_DIM)),
        'cache_swa_v': nrm((DEC_BATCH, DEPTH, PAST_LEN, SW_KV_HEADS, HEAD_DIM)),
        'cache_na_k': nrm((DEC_BATCH, DEPTH, PAST_LEN, NA_HEADS, HEAD_DIM)),
        'cache_na_v': nrm((DEC_BATCH, DEPTH, PAST_LEN, NA_HEADS, HEAD_DIM)),
        'state_mlstm_C': nrm((DEC_BATCH, DEPTH, 2, ML_HEADS, HEAD_DIM, HEAD_DIM), 0.3),
        'state_mlstm_n': nrm((DEC_BATCH, DEPTH, 2, ML_HEADS, HEAD_DIM), 0.3),
        'state_mlstm_m': nrm((DEC_BATCH, DEPTH, 2, ML_HEADS), 0.5),
        'state_rglru_h': nrm((DEC_BATCH, DEPTH, 2, RG_WIDTH), 0.5),
        'c': nrm((DEC_BATCH, d)),
        'c_ctx': nrm((d,)),
        'norm1_g': 1.0 + nrm((DEPTH, d), 0.02),
        'norm2_g': 1.0 + nrm((DEPTH, d), 0.02),
        'w_ada': nrm((DEPTH, d, 6 * d), 0.5 * d ** -0.5),
        'b_ada': nrm((DEPTH, 6 * d), 0.02),
        'w_in': nrm((DEPTH, d, D_IN), d ** -0.5),
        'ml_b_i': nrm((DEPTH, 2, ML_HEADS), 0.1),
        'ml_b_f': ml_b_f,
        'ml_hn': 1.0 + nrm((DEPTH, HEAD_DIM), 0.02),
        'sw_qn': 1.0 + nrm((DEPTH, HEAD_DIM), 0.02),
        'sw_kn': 1.0 + nrm((DEPTH, HEAD_DIM), 0.02),
        'sw_sink': nrm((DEPTH, SW_HEADS), 0.5),
        'rg_conv_w': nrm((DEPTH, RG_CONV, RG_WIDTH), RG_CONV ** -0.5),
        'rg_conv_b': nrm((DEPTH, RG_WIDTH), 0.02),
        'rg_w_r': nrm((DEPTH, 2, RG_BLOCKS, RG_BLOCK_DIM, RG_BLOCK_DIM), RG_BLOCK_DIM ** -0.5),
        'rg_b_r': nrm((DEPTH, 2, RG_WIDTH), 0.02),
        'rg_w_i': nrm((DEPTH, 2, RG_BLOCKS, RG_BLOCK_DIM, RG_BLOCK_DIM), RG_BLOCK_DIM ** -0.5),
        'rg_b_i': nrm((DEPTH, 2, RG_WIDTH), 0.02),
        'rg_lam': rg_lam,
        'na_qn': 1.0 + nrm((DEPTH, HEAD_DIM), 0.02),
        'na_kn': 1.0 + nrm((DEPTH, HEAD_DIM), 0.02),
        'na_rpb': nrm((DEPTH, NA_HEADS, NA_RPB_R, NA_RPB_C), 0.1),
        'w_br': nrm((DEPTH, N_BRANCH, BRANCH_WIDTH, d), BRANCH_WIDTH ** -0.5),
        'w_mg': nrm((DEPTH, d, N_BRANCH * d), d ** -0.5),
        'b_mg': nrm((DEPTH, N_BRANCH * d), 0.02),
        'w_out': nrm((DEPTH, d, d), d ** -0.5),
        'ffn_w1': nrm((n_dense, d, D_FF), d ** -0.5),
        'ffn_w3': nrm((n_dense, d, D_FF), d ** -0.5),
        'ffn_w2': nrm((n_dense, D_FF, d), D_FF ** -0.5),
        'moe_wr': nrm((n_moe, d, N_EXPERTS), d ** -0.5),
        'moe_br': nrm((n_moe, N_EXPERTS), 0.01),
        'moe_w1': nrm((n_moe, N_EXPERTS, d, D_FF_EXPERT), d ** -0.5),
        'moe_w3': nrm((n_moe, N_EXPERTS, d, D_FF_EXPERT), d ** -0.5),
        'moe_w2': nrm((n_moe, N_EXPERTS, D_FF_EXPERT, d), D_FF_EXPERT ** -0.5),
    }


def reference(x_prompt, x_sample, cache_swa_k, cache_swa_v, cache_na_k, cache_na_v, state_mlstm_C, state_mlstm_n, state_mlstm_m, state_rglru_h, c, c_ctx, norm1_g, norm2_g, w_ada, b_ada, w_in, ml_b_i, ml_b_f, ml_hn, sw_qn, sw_kn, sw_sink, rg_conv_w, rg_conv_b, rg_w_r, rg_b_r, rg_w_i, rg_b_i, rg_lam, na_qn, na_kn, na_rpb, w_br, w_mg, b_mg, w_out, ffn_w1, ffn_w3, ffn_w2, moe_wr, moe_br, moe_w1, moe_w3, moe_w2):
    xp = x_prompt
    xs = x_sample
    ctx_out = []
    for l in range(DEPTH):
        p = {'norm1': norm1_g[l], 'norm2': norm2_g[l], 'w_in': w_in[l],
             'ml_b_i': ml_b_i[l], 'ml_b_f': ml_b_f[l], 'ml_hn': ml_hn[l],
             'sw_qn': sw_qn[l], 'sw_kn': sw_kn[l], 'sw_sink': sw_sink[l],
             'rg_conv_w': rg_conv_w[l], 'rg_conv_b': rg_conv_b[l], 'rg_w_r': rg_w_r[l], 'rg_b_r': rg_b_r[l],
             'rg_w_i': rg_w_i[l], 'rg_b_i': rg_b_i[l], 'rg_lam': rg_lam[l],
             'na_qn': na_qn[l], 'na_kn': na_kn[l], 'na_rpb': na_rpb[l],
             'w_br': w_br[l], 'w_mg': w_mg[l], 'b_mg': b_mg[l], 'w_out': w_out[l]}
        j = l // 2
        if l % 2 == 0:
            ffn = functools.partial(swiglu, w1=ffn_w1[j], w3=ffn_w3[j], w2=ffn_w2[j])
        else:
            ffn = functools.partial(moe_ffn, wr=moe_wr[j], br=moe_br[j], w1=moe_w1[j], w3=moe_w3[j], w2=moe_w2[j])
        mod_ctx = jax.nn.silu(c_ctx) @ w_ada[l] + b_ada[l]
        mod_lat = (jax.nn.silu(c) @ w_ada[l] + b_ada[l])[:, None, :]
        xp, new_ctx = trunk_layer(xp, mod_ctx, p, ffn, None)
        ctx_out.append(new_ctx)
        cached = {'sw_k': cache_swa_k[:, l], 'sw_v': cache_swa_v[:, l], 'na_k': cache_na_k[:, l], 'na_v': cache_na_v[:, l],
                  'ml_C': state_mlstm_C[:, l], 'ml_n': state_mlstm_n[:, l], 'ml_m': state_mlstm_m[:, l], 'rg_h': state_rglru_h[:, l]}
        xs, _ = trunk_layer(xs, mod_lat, p, ffn, cached)
    new_swa_k = jnp.stack([t['sw_k'] for t in ctx_out], axis=1)
    new_swa_v = jnp.stack([t['sw_v'] for t in ctx_out], axis=1)
    new_na_k = jnp.stack([t['na_k'] for t in ctx_out], axis=1)
    new_na_v = jnp.stack([t['na_v'] for t in ctx_out], axis=1)
    new_mlstm_C = jnp.stack([t['ml_C'] for t in ctx_out], axis=1)
    new_mlstm_n = jnp.stack([t['ml_n'] for t in ctx_out], axis=1)
    new_mlstm_m = jnp.stack([t['ml_m'] for t in ctx_out], axis=1)
    new_rglru_h = jnp.stack([t['rg_h'] for t in ctx_out], axis=1)
    return (xp, xs, new_swa_k, new_swa_v, new_na_k, new_na_v, new_mlstm_C, new_mlstm_n, new_mlstm_m, new_rglru_h)
```

```python
import functools

import numpy as np
import jax
import jax.numpy as jnp
from jax import lax
from jax.experimental import pallas as pl
from jax.experimental.pallas import tpu as pltpu

F32 = jnp.float32
BF16 = jnp.bfloat16

D_MODEL = 1024
BATCH = 16
SEQ = 256
DEPTH = 2
DEC_BATCH = 2
DEC_SEQ = 4096
PAST_LEN = 256
GRID_W = 64
HEAD_DIM = 64
ML_HEADS = 4
ML_CHUNK = 128
SW_HEADS = 4
SW_KV_HEADS = 2
SW_WINDOW = 128
RG_WIDTH = 256
RG_BLOCKS = 4
RG_CONV = 4
RG_C = 8.0
NA_HEADS = 4
NA_ROWS = 8
NA_COLS = 16
N_BRANCH = 4
ROPE_BASE = 10000.0
D_FF = 2816
N_EXPERTS = 8
D_FF_EXPERT = 2048
EPS = 1e-6
NEG = -1e30
SCALE = HEAD_DIM ** -0.5

SEG = 4096
N_SEG = 3
N_TOK = N_SEG * SEG
LANES = 128
VMEM_LIMIT = 56 * 1024 * 1024

C_MQ, C_MK, C_MV, C_MO = 0, 256, 512, 768
C_SQ, C_RX, C_RY, C_NQ, C_NK, C_NV = 1024, 1280, 1536, 1792, 2048, 2304
C_SK, C_SV, C_G = 2560, 2688, 2816
P_W = 2944


def _cparams(sem):
    return pltpu.CompilerParams(dimension_semantics=sem, vmem_limit_bytes=VMEM_LIMIT)


def _dot(a, b):
    return jnp.dot(a.astype(BF16), b.astype(BF16), preferred_element_type=F32)


def _dot_nt(a, b):
    return lax.dot_general(a.astype(BF16), b.astype(BF16), (((1,), (1,)), ((), ())),
                           preferred_element_type=F32)


def _dot_tn(a, b):
    return lax.dot_general(a.astype(BF16), b.astype(BF16), (((0,), (0,)), ((), ())),
                           preferred_element_type=F32)


def _split3(x):
    hi = x.astype(BF16)
    r1 = x - hi.astype(F32)
    mid = r1.astype(BF16)
    lo = (r1 - mid.astype(F32)).astype(BF16)
    return hi, mid, lo


def _dot_exact_rhs(a01, x):
    hi, mid, lo = _split3(x)
    d = lambda p: jnp.dot(a01, p, preferred_element_type=F32)
    return d(hi) + d(mid) + d(lo)


def _dot_exact_lhs(x, a01):
    hi, mid, lo = _split3(x)
    d = lambda p: jnp.dot(p, a01, preferred_element_type=F32)
    return d(hi) + d(mid) + d(lo)


def _rms(x, g):
    return x * lax.rsqrt(jnp.mean(x * x, axis=-1, keepdims=True) + EPS) * g


def _rms_head_pairs(x, g):
    lane = lax.broadcasted_iota(jnp.int32, x.shape, 1)
    left = lane < HEAD_DIM
    sq = x * x
    s0 = jnp.sum(jnp.where(left, sq, 0.0), axis=-1, keepdims=True)
    s1 = jnp.sum(jnp.where(left, 0.0, sq), axis=-1, keepdims=True)
    ms = jnp.where(left, s0, s1) * (1.0 / HEAD_DIM)
    return x * lax.rsqrt(ms + EPS) * g


MOD_TN = 1536
MOD_ROWS = 3


def _mod_kernel(ct_ref, w_ref, b_ref, o_ref):
    ct = ct_ref[...]
    st = ct * jax.nn.sigmoid(ct)
    w = w_ref[0]
    o_ref[...] = jnp.zeros(o_ref.shape, F32)
    for r in range(MOD_ROWS):
        o_ref[0, r:r + 1, :] = jnp.sum(w * st[:, r:r + 1], axis=0, keepdims=True) + b_ref[0]


def _mod_table(cvecs_t, w_ada, b_ada):
    n = 6 * D_MODEL
    return pl.pallas_call(
        _mod_kernel,
        grid=(DEPTH, n // MOD_TN),
        in_specs=[pl.BlockSpec((D_MODEL, 8), lambda l, j: (0, 0)),
                  pl.BlockSpec((1, D_MODEL, MOD_TN), lambda l, j: (l, 0, j)),
                  pl.BlockSpec((1, 1, MOD_TN), lambda l, j: (l, 0, j))],
        out_specs=pl.BlockSpec((1, 8, MOD_TN), lambda l, j: (l, 0, j)),
        out_shape=jax.ShapeDtypeStruct((DEPTH, 8, n), F32),
        compiler_params=_cparams(("arbitrary", "arbitrary")),
        name="adaln_mod",
    )(cvecs_t, w_ada, b_ada.reshape(DEPTH, 1, n))


IN_TM = 256


def _swap16(y):
    lane = lax.broadcasted_iota(jnp.int32, y.shape, 1)
    first = (lane % 32) < 16
    return jnp.where(first, pltpu.roll(y, LANES - 16, 1), pltpu.roll(y, 16, 1))


def _inproj_kernel(x_ref, mod_ref, g_ref, w_ref, qkg_ref, cos_ref, sin_ref, o_ref):
    seg = pl.program_id(0)
    mod = mod_ref[0]
    sh1 = mod[:, 0:D_MODEL]
    sc1 = mod[:, D_MODEL:2 * D_MODEL]
    h = _rms(x_ref[...], g_ref[...]) * (1.0 + sc1) + sh1
    r = jnp.dot(h.astype(BF16), w_ref[...], preferred_element_type=F32)
    o_ref[:, 0:C_SQ] = r[:, 0:C_SQ]
    o_ref[:, C_RX:C_NQ] = r[:, C_RX:C_NQ]
    o_ref[:, C_NV:C_SK] = r[:, C_NV:C_SK]
    o_ref[:, C_SV:P_W] = r[:, C_SV:P_W]
    cos = cos_ref[...]
    sin = sin_ref[...]
    latent = seg > 0

    def rope(y):
        return jnp.where(latent, y * cos + _swap16(y) * sin, y)

    for p in range(2):
        a = C_SQ + LANES * p
        o_ref[:, a:a + LANES] = rope(_rms_head_pairs(r[:, a:a + LANES], qkg_ref[0:1, :]))
    o_ref[:, C_SK:C_SK + LANES] = rope(_rms_head_pairs(r[:, C_SK:C_SK + LANES], qkg_ref[1:2, :]))
    for p in range(2):
        a = C_NQ + LANES * p
        o_ref[:, a:a + LANES] = _rms_head_pairs(r[:, a:a + LANES], qkg_ref[2:3, :])
        a = C_NK + LANES * p
        o_ref[:, a:a + LANES] = _rms_head_pairs(r[:, a:a + LANES], qkg_ref[3:4, :])


def _inproj(x_all, mod_l, norm1, w_in_p, qk_gains, cos_t, sin_t):
    nt = SEG // IN_TM
    return pl.pallas_call(
        _inproj_kernel,
        grid=(N_SEG, nt),
        in_specs=[pl.BlockSpec((IN_TM, D_MODEL), lambda s, i: (s * nt + i, 0)),
                  pl.BlockSpec((1, 1, 6 * D_MODEL), lambda s, i: (s, 0, 0)),
                  pl.BlockSpec((1, D_MODEL), lambda s, i: (0, 0)),
                  pl.BlockSpec((D_MODEL, P_W), lambda s, i: (0, 0)),
                  pl.BlockSpec((4, LANES), lambda s, i: (0, 0)),
                  pl.BlockSpec((IN_TM, LANES), lambda s, i: (i, 0)),
                  pl.BlockSpec((IN_TM, LANES), lambda s, i: (i, 0))],
        out_specs=pl.BlockSpec((IN_TM, P_W), lambda s, i: (s * nt + i, 0)),
        out_shape=jax.ShapeDtypeStruct((N_TOK, P_W), F32),
        compiler_params=_cparams(("arbitrary", "arbitrary")),
        name="inproj",
    )(x_all, mod_l, norm1, w_in_p, qk_gains, cos_t, sin_t)


def _mlstm_kernel(qf, kf, vf, gf, qb, kb, vb, gb, c0, n0, m0, bias,
                  hf, hb, co, no, mo, c_s, n_s, m_s, *, nc):
    c = pl.program_id(1)

    @pl.when(c == 0)
    def _():
        c_s[...] = c0[0]
        n_s[...] = n0[0]
        m_s[...] = m0[0]

    ch = ML_CHUNK
    r_io = lax.broadcasted_iota(jnp.int32, (ch, ch), 0)
    c_io = lax.broadcasted_iota(jnp.int32, (ch, ch), 1)
    lower = r_io >= c_io
    upper = r_io <= c_io
    for d, (q_ref, k_ref, v_ref, g_ref, h_ref) in enumerate(((qf, kf, vf, gf, hf), (qb, kb, vb, gb, hb))):
        mask = lower if d == 0 else upper
        tri = mask.astype(BF16)
        tri_t = (upper if d == 0 else lower).astype(BF16)
        g = g_ref[...] + bias[...]
        b_cols = _dot_exact_rhs(tri, jax.nn.log_sigmoid(g))
        gt = g.T
        li_rows = gt[0:8, :]
        b_rows = _dot_exact_lhs(jax.nn.log_sigmoid(gt[8:16, :]), tri_t)
        for hh in range(ML_HEADS):
            j = ML_HEADS * d + hh
            sl = slice(HEAD_DIM * hh, HEAD_DIM * (hh + 1))
            b_col = b_cols[:, 8 + j:9 + j]
            b_row = b_rows[j:j + 1, :]
            li_row = li_rows[j:j + 1, :]
            li_col = g[:, j:j + 1]
            m = m_s[j:j + 1, 0:1]
            bl = b_col[ch - 1:ch, :] if d == 0 else b_col[0:1, :]
            dmat = jnp.where(mask, b_col - b_row + li_row, NEG)
            inter = b_col + m
            mt = jnp.maximum(inter, jnp.max(dmat, axis=1, keepdims=True))
            w_intra = jnp.exp(dmat - mt)
            w_prev = jnp.exp(inter - mt)
            qh = q_ref[:, sl]
            kh = k_ref[:, sl] * SCALE
            vh = v_ref[:, sl]
            s = _dot_nt(qh, kh) * w_intra
            cj = c_s[j]
            nrow = n_s[j:j + 1, :]
            num = w_prev * _dot(qh, cj) + _dot(s, vh)
            den = w_prev * jnp.sum(qh * nrow, axis=1, keepdims=True) + jnp.sum(s, axis=1, keepdims=True)
            h_ref[:, sl] = num / jnp.maximum(jnp.abs(den), jnp.exp(-mt))
            g_col = bl - b_col + li_col
            m_new = jnp.maximum(bl + m, jnp.max(g_col, axis=0, keepdims=True))
            wk = jnp.exp(g_col - m_new)
            wp = jnp.exp(bl + m - m_new)
            kw = kh * wk
            c_s[j] = wp * cj + _dot_tn(kw, vh)
            n_s[j:j + 1, :] = wp * nrow + jnp.sum(kw, axis=0, keepdims=True)
            m_s[j:j + 1, :] = jnp.broadcast_to(m_new, (1, LANES))

    @pl.when(c == nc - 1)
    def _():
        co[0] = c_s[...]
        no[0] = n_s[...]
        mo[0] = m_s[...]


def _mlstm(proj, gate_bias, c0, n0, m0, *, bsz, seq, row0):
    nc = seq // ML_CHUNK
    base = row0 // ML_CHUNK
    nj = 2 * ML_HEADS
    fw = lambda col: (lambda b, c: (base + b * nc + c, col))
    bw = lambda col: (lambda b, c: (base + b * nc + nc - 1 - c, col))
    qkv = lambda f: [pl.BlockSpec((ML_CHUNK, 256), f(C_MQ // 256)),
                     pl.BlockSpec((ML_CHUNK, 256), f(C_MK // 256)),
                     pl.BlockSpec((ML_CHUNK, 256), f(C_MV // 256)),
                     pl.BlockSpec((ML_CHUNK, LANES), f(C_G // LANES))]
    st_specs = [pl.BlockSpec((1, nj, HEAD_DIM, HEAD_DIM), lambda b, c: (b, 0, 0, 0)),
                pl.BlockSpec((1, nj, HEAD_DIM), lambda b, c: (b, 0, 0)),
                pl.BlockSpec((1, nj, LANES), lambda b, c: (b, 0, 0))]
    return pl.pallas_call(
        functools.partial(_mlstm_kernel, nc=nc),
        grid=(bsz, nc),
        in_specs=qkv(fw) + qkv(bw) + st_specs + [pl.BlockSpec((1, LANES), lambda b, c: (0, 0))],
        out_specs=[pl.BlockSpec((ML_CHUNK, 256), lambda b, c: (b * nc + c, 0)),
                   pl.BlockSpec((ML_CHUNK, 256), lambda b, c: (b * nc + nc - 1 - c, 0))] + st_specs,
        out_shape=[jax.ShapeDtypeStruct((bsz * seq, 256), F32),
                   jax.ShapeDtypeStruct((bsz * seq, 256), F32),
                   jax.ShapeDtypeStruct((bsz, nj, HEAD_DIM, HEAD_DIM), F32),
                   jax.ShapeDtypeStruct((bsz, nj, HEAD_DIM), F32),
                   jax.ShapeDtypeStruct((bsz, nj, LANES), F32)],
        scratch_shapes=[pltpu.VMEM((nj, HEAD_DIM, HEAD_DIM), F32),
                        pltpu.VMEM((nj, HEAD_DIM), F32),
                        pltpu.VMEM((nj, LANES), F32)],
        compiler_params=_cparams(("arbitrary", "arbitrary")),
        name="mlstm",
    )(proj, proj, proj, proj, proj, proj, proj, proj, c0, n0, m0, gate_bias)


RG_TC = 256
RG_PAD = 8


def _rglru_kernel(rx, ry, cw, cb, wg, bg, lam, h0, oc, hl, xpad, af, ab, ub, *, seq):
    xpad[0:RG_PAD, :] = jnp.zeros((RG_PAD, RG_WIDTH), F32)
    xpad[seq + RG_PAD:seq + 2 * RG_PAD, :] = jnp.zeros((RG_PAD, RG_WIDTH), F32)
    xpad[RG_PAD:seq + RG_PAD, :] = rx[...]
    sp = jax.nn.softplus(-lam[...])
    left = (RG_CONV - 1) // 2
    for ci in range(seq // RG_TC):
        s0 = ci * RG_TC
        xc = None
        for j in range(RG_CONV):
            a = RG_PAD + s0 + j - left
            term = xpad[a:a + RG_TC, :] * cw[j:j + 1, :]
            xc = term if xc is None else xc + term
        xc = xc + cb[...]
        pre = _dot(xc, wg[...]) + bg[...]
        for d in range(2):
            o = 2 * RG_WIDTH * d
            r = jax.nn.sigmoid(pre[:, o:o + RG_WIDTH])
            gi = jax.nn.sigmoid(pre[:, o + RG_WIDTH:o + 2 * RG_WIDTH])
            log_a = -RG_C * r * sp[d:d + 1, :]
            a_val = jnp.exp(log_a)
            u_val = jnp.sqrt(-jnp.tanh(log_a) * (a_val * a_val + 1.0)) * (gi * xc)
            if d == 0:
                af[s0:s0 + RG_TC, :] = a_val
                oc[s0:s0 + RG_TC, :] = u_val
            else:
                ab[s0:s0 + RG_TC, :] = a_val
                ub[s0:s0 + RG_TC, :] = u_val

    def body(t, carry):
        h_f, h_b = carry
        h_f = af[pl.ds(t, 1), :] * h_f + oc[pl.ds(t, 1), :]
        oc[pl.ds(t, 1), :] = h_f
        tb = seq - 1 - t
        h_b = ab[pl.ds(tb, 1), :] * h_b + ub[pl.ds(tb, 1), :]
        ub[pl.ds(tb, 1), :] = h_b
        return h_f, h_b

    h_f, h_b = lax.fori_loop(0, seq, body, (h0[0, 0:1, :], h0[0, 1:2, :]), unroll=8)
    hl[0, 0:1, :] = h_f
    hl[0, 1:2, :] = h_b
    for ci in range(seq // RG_TC):
        sl = slice(ci * RG_TC, (ci + 1) * RG_TC)
        oc[sl, :] = (oc[sl, :] + ub[sl, :]) * jax.nn.gelu(ry[sl, :])


def _rglru(proj, cw, cb, wg, bg, lam, h0, *, bsz, seq, row0):
    base = row0 // seq
    full = lambda shape: pl.BlockSpec(shape, lambda b: tuple(0 for _ in shape))
    return pl.pallas_call(
        functools.partial(_rglru_kernel, seq=seq),
        grid=(bsz,),
        in_specs=[pl.BlockSpec((seq, RG_WIDTH), lambda b: (base + b, C_RX // RG_WIDTH)),
                  pl.BlockSpec((seq, RG_WIDTH), lambda b: (base + b, C_RY // RG_WIDTH)),
                  full((RG_CONV, RG_WIDTH)), full((1, RG_WIDTH)),
                  full((RG_WIDTH, 4 * RG_WIDTH)), full((1, 4 * RG_WIDTH)), full((2, RG_WIDTH)),
                  pl.BlockSpec((1, 2, RG_WIDTH), lambda b: (b, 0, 0))],
        out_specs=[pl.BlockSpec((seq, RG_WIDTH), lambda b: (b, 0)),
                   pl.BlockSpec((1, 2, RG_WIDTH), lambda b: (b, 0, 0))],
        out_shape=[jax.ShapeDtypeStruct((bsz * seq, RG_WIDTH), F32),
                   jax.ShapeDtypeStruct((bsz, 2, RG_WIDTH), F32)],
        scratch_shapes=[pltpu.VMEM((seq + 2 * RG_PAD, RG_WIDTH), F32),
                        pltpu.VMEM((seq, RG_WIDTH), F32),
                        pltpu.VMEM((seq, RG_WIDTH), F32),
                        pltpu.VMEM((seq, RG_WIDTH), F32)],
        compiler_params=_cparams(("arbitrary",)),
        name="rglru",
    )(proj, proj, cw, cb, wg, bg, lam, h0)


def _softmax_pv(scores, values, sink):
    m = functools.reduce(jnp.maximum, [jnp.max(s, axis=-1, keepdims=True) for s in scores])
    if sink is not None:
        m = jnp.maximum(m, sink)
    ps = [jnp.exp(s - m) for s in scores]
    den = functools.reduce(jnp.add, [jnp.sum(p, axis=-1, keepdims=True) for p in ps])
    if sink is not None:
        den = den + jnp.exp(sink - m)
    num = functools.reduce(jnp.add, [_dot(p, v) for p, v in zip(ps, values)])
    return num / den


def _ctx_attn_kernel(sink, sq, sk, sv, nq, nk, nv, ob, od):
    for h in range(SW_HEADS):
        j = h // (SW_HEADS // SW_KV_HEADS)
        sl = slice(HEAD_DIM * h, HEAD_DIM * (h + 1))
        kv = slice(HEAD_DIM * j, HEAD_DIM * (j + 1))
        s = _dot_nt(sq[:, sl], sk[:, kv]) * SCALE
        ob[:, sl] = _softmax_pv([s], [sv[:, kv]], sink[h])
    for h in range(NA_HEADS):
        sl = slice(HEAD_DIM * h, HEAD_DIM * (h + 1))
        s = _dot_nt(nq[:, sl], nk[:, sl]) * SCALE
        od[:, sl] = _softmax_pv([s], [nv[:, sl]], None)


def _ctx_attn(proj, sink):
    blk = lambda w, col: pl.BlockSpec((SEQ, w), lambda b: (b, col))
    return pl.pallas_call(
        _ctx_attn_kernel,
        grid=(BATCH,),
        in_specs=[pl.BlockSpec(memory_space=pltpu.SMEM),
                  blk(256, C_SQ // 256), blk(128, C_SK // 128), blk(128, C_SV // 128),
                  blk(256, C_NQ // 256), blk(256, C_NK // 256), blk(256, C_NV // 256)],
        out_specs=[pl.BlockSpec((SEQ, 256), lambda b: (b, 0)),
                   pl.BlockSpec((SEQ, 256), lambda b: (b, 0))],
        out_shape=[jax.ShapeDtypeStruct((SEG, 256), F32), jax.ShapeDtypeStruct((SEG, 256), F32)],
        compiler_params=_cparams(("arbitrary",)),
        name="ctx_attn",
    )(sink, proj, proj, proj, proj, proj, proj)


SW_QB = 128
SW_SPAN = SW_QB + 2 * SW_WINDOW


def _swa_kernel(sink, q, k, v, kc, vc, ob):
    n = pl.program_id(1)
    ws = jnp.clip((n - 1) * SW_QB, 0, DEC_SEQ - SW_SPAN)
    ws = pl.multiple_of(ws, SW_QB)
    qpos = n * SW_QB + lax.broadcasted_iota(jnp.int32, (SW_QB, SW_SPAN), 0)
    kpos = ws + lax.broadcasted_iota(jnp.int32, (SW_QB, SW_SPAN), 1)
    valid = jnp.abs(qpos - kpos) <= SW_WINDOW
    for h in range(SW_HEADS):
        j = h // (SW_HEADS // SW_KV_HEADS)
        sl = slice(HEAD_DIM * h, HEAD_DIM * (h + 1))
        kv = slice(HEAD_DIM * j, HEAD_DIM * (j + 1))
        qh = q[:, sl]
        s_loc = jnp.where(valid, _dot_nt(qh, k[pl.ds(ws, SW_SPAN), kv]) * SCALE, NEG)
        s_ctx = _dot_nt(qh, kc[0, :, kv]) * SCALE
        ob[:, sl] = _softmax_pv([s_loc, s_ctx], [v[pl.ds(ws, SW_SPAN), kv], vc[0, :, kv]], sink[h])


def _swa(proj, kc, vc, sink):
    nq = DEC_SEQ // SW_QB
    qbase = SEG // SW_QB
    return pl.pallas_call(
        _swa_kernel,
        grid=(DEC_BATCH, nq),
        in_specs=[pl.BlockSpec(memory_space=pltpu.SMEM),
                  pl.BlockSpec((SW_QB, 256), lambda b, n: (qbase + b * nq + n, C_SQ // 256)),
                  pl.BlockSpec((DEC_SEQ, 128), lambda b, n: (1 + b, C_SK // 128)),
                  pl.BlockSpec((DEC_SEQ, 128), lambda b, n: (1 + b, C_SV // 128)),
                  pl.BlockSpec((1, PAST_LEN, 128), lambda b, n: (b, 0, 0)),
                  pl.BlockSpec((1, PAST_LEN, 128), lambda b, n: (b, 0, 0))],
        out_specs=pl.BlockSpec((SW_QB, 256), lambda b, n: (b * nq + n, 0)),
        out_shape=jax.ShapeDtypeStruct((DEC_BATCH * DEC_SEQ, 256), F32),
        compiler_params=_cparams(("arbitrary", "arbitrary")),
        name="swa",
    )(sink, proj, proj, proj, kc, vc)


NA_RPB_R = 2 * NA_ROWS - 1
NA_RPB_C = 2 * NA_COLS - 1
NA_NKEY = NA_ROWS * GRID_W
GRID_ROWS = DEC_SEQ // GRID_W


def _na_bias_kernel(rpb, out):
    h = pl.program_id(0)
    qc = lax.broadcasted_iota(jnp.int32, (GRID_W, GRID_W), 0)
    kc = lax.broadcasted_iota(jnp.int32, (GRID_W, GRID_W), 1)
    dc = jnp.clip(kc - qc, -(NA_COLS - 1), NA_COLS - 1) + NA_COLS - 1
    lo = jnp.clip(qc - NA_COLS // 2, 0, GRID_W - NA_COLS)
    valid = (kc >= lo) & (kc < lo + NA_COLS)
    tiles = []
    for dr in range(NA_RPB_R):
        t = jnp.zeros((GRID_W, GRID_W), F32)
        for j in range(NA_RPB_C):
            t = jnp.where(dc == j, rpb[(h * NA_RPB_R + dr) * NA_RPB_C + j], t)
        tiles.append(jnp.where(valid, t, NEG))
    for w in range(NA_ROWS):
        for i in range(NA_ROWS):
            out[0, w, :, GRID_W * i:GRID_W * (i + 1)] = tiles[w + i]


def _na_bias(rpb_flat):
    return pl.pallas_call(
        _na_bias_kernel,
        grid=(NA_HEADS,),
        in_specs=[pl.BlockSpec(memory_space=pltpu.SMEM)],
        out_specs=pl.BlockSpec((1, NA_ROWS, GRID_W, NA_NKEY), lambda h: (h, 0, 0, 0)),
        out_shape=jax.ShapeDtypeStruct((NA_HEADS, NA_ROWS, GRID_W, NA_NKEY), F32),
        compiler_params=_cparams(("arbitrary",)),
        name="na_bias",
    )(rpb_flat)


def _na_kernel(q, k, v, kc, vc, bias, od):
    r = pl.program_id(1)
    r0 = jnp.clip(r - NA_ROWS // 2, 0, GRID_ROWS - NA_ROWS)
    k0 = pl.multiple_of(r0 * GRID_W, GRID_W)
    win = r0 - r + NA_ROWS - 1
    for h in range(NA_HEADS):
        sl = slice(HEAD_DIM * h, HEAD_DIM * (h + 1))
        qh = q[:, sl]
        s_loc = _dot_nt(qh, k[pl.ds(k0, NA_NKEY), sl]) * SCALE + bias[h, pl.ds(win, 1)][0]
        s_ctx = _dot_nt(qh, kc[0, :, sl]) * SCALE
        od[:, sl] = _softmax_pv([s_loc, s_ctx], [v[pl.ds(k0, NA_NKEY), sl], vc[0, :, sl]], None)


def _na(proj, kc, vc, bias):
    qbase = SEG // GRID_W
    return pl.pallas_call(
        _na_kernel,
        grid=(DEC_BATCH, GRID_ROWS),
        in_specs=[pl.BlockSpec((GRID_W, 256), lambda b, r: (qbase + b * GRID_ROWS + r, C_NQ // 256)),
                  pl.BlockSpec((DEC_SEQ, 256), lambda b, r: (1 + b, C_NK // 256)),
                  pl.BlockSpec((DEC_SEQ, 256), lambda b, r: (1 + b, C_NV // 256)),
                  pl.BlockSpec((1, PAST_LEN, 256), lambda b, r: (b, 0, 0)),
                  pl.BlockSpec((1, PAST_LEN, 256), lambda b, r: (b, 0, 0)),
                  pl.BlockSpec((NA_HEADS, NA_ROWS, GRID_W, NA_NKEY), lambda b, r: (0, 0, 0, 0))],
        out_specs=pl.BlockSpec((GRID_W, 256), lambda b, r: (b * GRID_ROWS + r, 0)),
        out_shape=jax.ShapeDtypeStruct((DEC_BATCH * DEC_SEQ, 256), F32),
        compiler_params=_cparams(("arbitrary", "arbitrary")),
        name="na",
    )(proj, proj, proj, kc, vc, bias)


MG_TM = 256


def _merge_kernel(x_ref, mod_ref, g1_ref, g2_ref, hf, hb, mo, ob, oc, od, hn, wmg, bmg, wbr, wout, *rest, moe):
    if moe:
        wr, br, x1_ref, h2_ref, comb_ref = rest
    else:
        x1_ref, h2_ref = rest
    mod = mod_ref[0]
    chunk = lambda i: mod[:, i * D_MODEL:(i + 1) * D_MODEL]
    sh1, sc1, gate1, sh2, sc2 = chunk(0), chunk(1), chunk(2), chunk(3), chunk(4)
    x = x_ref[...]
    h = (_rms(x, g1_ref[...]) * (1.0 + sc1) + sh1).astype(BF16)
    hsum = hf[...] + hb[...]
    out_a = jnp.concatenate(
        [_rms_head_pairs(hsum[:, LANES * p:LANES * (p + 1)], hn[...]) for p in range(2)], axis=-1)
    out_a = out_a * jax.nn.sigmoid(mo[...])
    acc = None
    for n, br_val in enumerate((out_a, ob[...], oc[...], od[...])):
        gate = jax.nn.sigmoid(jnp.dot(h, wmg[:, n * D_MODEL:(n + 1) * D_MODEL], preferred_element_type=F32)
                              + bmg[:, n * D_MODEL:(n + 1) * D_MODEL])
        term = gate * jnp.dot(br_val.astype(BF16), wbr[n], preferred_element_type=F32)
        acc = term if acc is None else acc + term
    y = jnp.dot(acc.astype(BF16), wout[...], preferred_element_type=F32)
    x1 = x + gate1 * y
    x1_ref[...] = x1
    h2 = _rms(x1, g2_ref[...]) * (1.0 + sc2) + sh2
    h2_ref[...] = h2.astype(BF16)
    if moe:
        logits = jnp.dot(h2, wr[...], precision=lax.Precision.HIGHEST, preferred_element_type=F32) + br[...]
        lane = lax.broadcasted_iota(jnp.int32, logits.shape, 1)
        logits = jnp.where(lane < N_EXPERTS, logits, -jnp.inf)
        v1 = jnp.max(logits, axis=-1, keepdims=True)
        i1 = jnp.min(jnp.where(logits == v1, lane, LANES), axis=-1, keepdims=True)
        rest_l = jnp.where(lane == i1, -jnp.inf, logits)
        v2 = jnp.max(rest_l, axis=-1, keepdims=True)
        i2 = jnp.min(jnp.where(rest_l == v2, lane, LANES), axis=-1, keepdims=True)
        e2 = jnp.exp(v2 - v1)
        den = 1.0 + e2
        comb_ref[...] = jnp.where(lane == i1, 1.0 / den, 0.0) + jnp.where(lane == i2, e2 / den, 0.0)


def _merge(x_all, mod_l, g1, g2, hf, hb, proj, ob, oc, od, hn, wmg, bmg, wbr, wout, router=None):
    nt = SEG // MG_TM
    moe = router is not None
    row = lambda w: pl.BlockSpec((MG_TM, w), lambda s, i: (s * nt + i, 0))
    full = lambda shape: pl.BlockSpec(shape, lambda s, i: tuple(0 for _ in shape))
    in_specs = [row(D_MODEL),
                pl.BlockSpec((1, 1, 6 * D_MODEL), lambda s, i: (s, 0, 0)),
                full((1, D_MODEL)), full((1, D_MODEL)),
                row(256), row(256),
                pl.BlockSpec((MG_TM, 256), lambda s, i: (s * nt + i, C_MO // 256)),
                row(256), row(256), row(256),
                full((1, LANES)), full((D_MODEL, N_BRANCH * D_MODEL)), full((1, N_BRANCH * D_MODEL)),
                full((N_BRANCH, 256, D_MODEL)), full((D_MODEL, D_MODEL))]
    args = [x_all, mod_l, g1, g2, hf, hb, proj, ob, oc, od, hn, wmg, bmg, wbr, wout]
    out_specs = [row(D_MODEL), row(D_MODEL)]
    out_shape = [jax.ShapeDtypeStruct((N_TOK, D_MODEL), F32), jax.ShapeDtypeStruct((N_TOK, D_MODEL), BF16)]
    if moe:
        in_specs += [full((D_MODEL, LANES)), full((1, LANES))]
        args += list(router)
        out_specs.append(row(LANES))
        out_shape.append(jax.ShapeDtypeStruct((N_TOK, LANES), F32))
    return pl.pallas_call(
        functools.partial(_merge_kernel, moe=moe),
        grid=(N_SEG, nt),
        in_specs=in_specs,
        out_specs=out_specs,
        out_shape=out_shape,
        compiler_params=_cparams(("arbitrary", "arbitrary")),
        name="merge",
    )(*args)


FF_TM = 512
FF_TF = 1408
MOE_TF = 1024


def _ffn_kernel(h2, w1, w3, w2, x1, mod_ref, out, acc):
    f = pl.program_id(1)

    @pl.when(f == 0)
    def _():
        acc[...] = jnp.zeros(acc.shape, F32)

    h = h2[...]
    a = jnp.dot(h, w1[...], preferred_element_type=F32)
    b = jnp.dot(h, w3[...], preferred_element_type=F32)
    act = (jax.nn.silu(a) * b).astype(BF16)
    acc[...] += jnp.dot(act, w2[...], preferred_element_type=F32)

    @pl.when(f == pl.num_programs(1) - 1)
    def _():
        gate2 = mod_ref[0][:, 5 * D_MODEL:6 * D_MODEL]
        out[...] = x1[...] + gate2 * acc[...]


def _ffn(h2, x1, mod_l, w1, w3, w2):
    nt = N_TOK // FF_TM
    per_seg = SEG // FF_TM
    return pl.pallas_call(
        _ffn_kernel,
        grid=(nt, D_FF // FF_TF),
        in_specs=[pl.BlockSpec((FF_TM, D_MODEL), lambda i, f: (i, 0)),
                  pl.BlockSpec((D_MODEL, FF_TF), lambda i, f: (0, f)),
                  pl.BlockSpec((D_MODEL, FF_TF), lambda i, f: (0, f)),
                  pl.BlockSpec((FF_TF, D_MODEL), lambda i, f: (f, 0)),
                  pl.BlockSpec((FF_TM, D_MODEL), lambda i, f: (i, 0)),
                  pl.BlockSpec((1, 1, 6 * D_MODEL), lambda i, f: (i // per_seg, 0, 0))],
        out_specs=pl.BlockSpec((FF_TM, D_MODEL), lambda i, f: (i, 0)),
        out_shape=jax.ShapeDtypeStruct((N_TOK, D_MODEL), F32),
        scratch_shapes=[pltpu.VMEM((FF_TM, D_MODEL), F32)],
        compiler_params=_cparams(("arbitrary", "arbitrary")),
        name="ffn",
    )(h2, w1, w3, w2, x1, mod_l)


def _moe_kernel(h2, comb, w1, w3, w2, x1, mod_ref, out, acc):
    e = pl.program_id(1)
    f = pl.program_id(2)

    @pl.when((e == 0) & (f == 0))
    def _():
        acc[...] = jnp.zeros(acc.shape, F32)

    h = h2[...]
    cv = comb[...]
    lane = lax.broadcasted_iota(jnp.int32, cv.shape, 1)
    weight = jnp.sum(jnp.where(lane == e, cv, 0.0), axis=-1, keepdims=True)
    a = jnp.dot(h, w1[0], preferred_element_type=F32)
    b = jnp.dot(h, w3[0], preferred_element_type=F32)
    act = (weight * (jax.nn.silu(a) * b)).astype(BF16)
    acc[...] += jnp.dot(act, w2[0], preferred_element_type=F32)

    @pl.when((e == pl.num_programs(1) - 1) & (f == pl.num_programs(2) - 1))
    def _():
        gate2 = mod_ref[0][:, 5 * D_MODEL:6 * D_MODEL]
        out[...] = x1[...] + gate2 * acc[...]


def _moe(h2, comb, x1, mod_l, w1, w3, w2):
    nt = N_TOK // FF_TM
    per_seg = SEG // FF_TM
    return pl.pallas_call(
        _moe_kernel,
        grid=(nt, N_EXPERTS, D_FF_EXPERT // MOE_TF),
        in_specs=[pl.BlockSpec((FF_TM, D_MODEL), lambda i, e, f: (i, 0)),
                  pl.BlockSpec((FF_TM, LANES), lambda i, e, f: (i, 0)),
                  pl.BlockSpec((1, D_MODEL, MOE_TF), lambda i, e, f: (e, 0, f)),
                  pl.BlockSpec((1, D_MODEL, MOE_TF), lambda i, e, f: (e, 0, f)),
                  pl.BlockSpec((1, MOE_TF, D_MODEL), lambda i, e, f: (e, f, 0)),
                  pl.BlockSpec((FF_TM, D_MODEL), lambda i, e, f: (i, 0)),
                  pl.BlockSpec((1, 1, 6 * D_MODEL), lambda i, e, f: (i // per_seg, 0, 0))],
        out_specs=pl.BlockSpec((FF_TM, D_MODEL), lambda i, e, f: (i, 0)),
        out_shape=jax.ShapeDtypeStruct((N_TOK, D_MODEL), F32),
        scratch_shapes=[pltpu.VMEM((FF_TM, D_MODEL), F32)],
        compiler_params=_cparams(("arbitrary", "arbitrary", "arbitrary")),
        name="moe",
    )(h2, comb, w1, w3, w2, x1, mod_l)


def _rope_tables():
    t = jnp.arange(DEC_SEQ)
    row, col = (t // GRID_W).astype(F32), (t % GRID_W).astype(F32)
    nf = HEAD_DIM // 4
    freqs = ROPE_BASE ** (-jnp.arange(nf, dtype=F32) / nf)
    lane = np.arange(LANES) % HEAD_DIM
    fidx = lane % nf
    use_col = (lane // (HEAD_DIM // 2)) == 1
    first = (lane % (HEAD_DIM // 2)) < nf
    pos = jnp.where(use_col[None, :], col[:, None], row[:, None])
    ang = pos * freqs[fidx][None, :]
    sin = jnp.sin(ang)
    return jnp.cos(ang), jnp.where(first[None, :], -sin, sin)


def _permute_w_in(w):
    sizes = (256, 256, 256, 256, 8, 8, 256, 128, 128, 256, 256, 256, 256, 256)
    offs = np.concatenate([[0], np.cumsum(sizes)])
    part = lambda i: w[:, offs[i]:offs[i + 1]]
    mq, mk, mv, mo, mi, mf, sq, sk, sv, rx, ry, nq, nk, nv = (part(i) for i in range(14))
    pad = jnp.zeros((w.shape[0], LANES - 16), w.dtype)
    return jnp.concatenate([mq, mk, mv, mo, sq, rx, ry, nq, nk, nv, sk, sv, mi, mf, pad], axis=1)


def _block_diag(w):
    bd = RG_WIDTH // RG_BLOCKS
    out = jnp.zeros((RG_WIDTH, RG_WIDTH), w.dtype)
    for n in range(RG_BLOCKS):
        out = out.at[n * bd:(n + 1) * bd, n * bd:(n + 1) * bd].set(w[n])
    return out


def _tile2(g):
    return jnp.concatenate([g, g]).reshape(1, LANES)


def kernel(x_prompt, x_sample, cache_swa_k, cache_swa_v, cache_na_k, cache_na_v, state_mlstm_C, state_mlstm_n, state_mlstm_m, state_rglru_h, c, c_ctx, norm1_g, norm2_g, w_ada, b_ada, w_in, ml_b_i, ml_b_f, ml_hn, sw_qn, sw_kn, sw_sink, rg_conv_w, rg_conv_b, rg_w_r, rg_b_r, rg_w_i, rg_b_i, rg_lam, na_qn, na_kn, na_rpb, w_br, w_mg, b_mg, w_out, ffn_w1, ffn_w3, ffn_w2, moe_wr, moe_br, moe_w1, moe_w3, moe_w2):
    x_all = jnp.concatenate([x_prompt.reshape(SEG, D_MODEL), x_sample.reshape(2 * SEG, D_MODEL)], axis=0)
    cvecs = jnp.concatenate([c_ctx[None, :], c, jnp.zeros((8 - 1 - DEC_BATCH, D_MODEL), F32)], axis=0)
    mod = _mod_table(cvecs.T, w_ada, b_ada)
    cos_t, sin_t = _rope_tables()
    nj = 2 * ML_HEADS
    zeros_state = (jnp.zeros((BATCH, nj, HEAD_DIM, HEAD_DIM), F32), jnp.zeros((BATCH, nj, HEAD_DIM), F32),
                   jnp.zeros((BATCH, nj, LANES), F32), jnp.zeros((BATCH, 2, RG_WIDTH), F32))
    ctx_out = []
    for l in range(DEPTH):
        mod_l = mod[l].reshape(8, 1, 6 * D_MODEL)
        qk_gains = jnp.stack([_tile2(sw_qn[l])[0], _tile2(sw_kn[l])[0], _tile2(na_qn[l])[0], _tile2(na_kn[l])[0]])
        proj = _inproj(x_all, mod_l, norm1_g[l].reshape(1, D_MODEL), _permute_w_in(w_in[l]).astype(BF16),
                       qk_gains, cos_t, sin_t)
        gate_bias = jnp.concatenate([ml_b_i[l].reshape(-1), ml_b_f[l].reshape(-1),
                                     jnp.zeros((LANES - 2 * nj,), F32)]).reshape(1, LANES)
        hf_c, hb_c, c_new, n_new, m_new = _mlstm(proj, gate_bias, zeros_state[0], zeros_state[1], zeros_state[2],
                                                 bsz=BATCH, seq=SEQ, row0=0)
        m0_lat = jnp.broadcast_to(state_mlstm_m[:, l].reshape(DEC_BATCH, nj, 1), (DEC_BATCH, nj, LANES))
        hf_l, hb_l, _, _, _ = _mlstm(proj, gate_bias,
                                     state_mlstm_C[:, l].reshape(DEC_BATCH, nj, HEAD_DIM, HEAD_DIM),
                                     state_mlstm_n[:, l].reshape(DEC_BATCH, nj, HEAD_DIM), m0_lat,
                                     bsz=DEC_BATCH, seq=DEC_SEQ, row0=SEG)
        wg = jnp.concatenate([_block_diag(rg_w_r[l, 0]), _block_diag(rg_w_i[l, 0]),
                              _block_diag(rg_w_r[l, 1]), _block_diag(rg_w_i[l, 1])], axis=1).astype(BF16)
        bg = jnp.concatenate([rg_b_r[l, 0], rg_b_i[l, 0], rg_b_r[l, 1], rg_b_i[l, 1]]).reshape(1, 4 * RG_WIDTH)
        rg_args = (rg_conv_w[l], rg_conv_b[l].reshape(1, RG_WIDTH), wg, bg, rg_lam[l])
        oc_c, hl_new = _rglru(proj, *rg_args, zeros_state[3], bsz=BATCH, seq=SEQ, row0=0)
        oc_l, _ = _rglru(proj, *rg_args, state_rglru_h[:, l], bsz=DEC_BATCH, seq=DEC_SEQ, row0=SEG)
        ob_c, od_c = _ctx_attn(proj, sw_sink[l])
        ob_l = _swa(proj, cache_swa_k[:, l].reshape(DEC_BATCH, PAST_LEN, 128),
                    cache_swa_v[:, l].reshape(DEC_BATCH, PAST_LEN, 128), sw_sink[l])
        od_l = _na(proj, cache_na_k[:, l].reshape(DEC_BATCH, PAST_LEN, 256),
                   cache_na_v[:, l].reshape(DEC_BATCH, PAST_LEN, 256), _na_bias(na_rpb[l].reshape(-1)))
        cat = lambda a, b: jnp.concatenate([a, b], axis=0)
        moe_layer = l % 2 == 1
        router = None
        if moe_layer:
            j = l // 2
            wr = jnp.concatenate([moe_wr[j], jnp.zeros((D_MODEL, LANES - N_EXPERTS), F32)], axis=1)
            brp = jnp.concatenate([moe_br[j], jnp.zeros((LANES - N_EXPERTS,), F32)]).reshape(1, LANES)
            router = (wr, brp)
        outs = _merge(x_all, mod_l, norm1_g[l].reshape(1, D_MODEL), norm2_g[l].reshape(1, D_MODEL),
                      cat(hf_c, hf_l), cat(hb_c, hb_l), proj, cat(ob_c, ob_l), cat(oc_c, oc_l), cat(od_c, od_l),
                      _tile2(ml_hn[l]), w_mg[l].astype(BF16), b_mg[l].reshape(1, -1), w_br[l].astype(BF16),
                      w_out[l].astype(BF16), router)
        if moe_layer:
            x1, h2, comb = outs
            x_all = _moe(h2, comb, x1, mod_l, moe_w1[j].astype(BF16), moe_w3[j].astype(BF16),
                         moe_w2[j].astype(BF16))
        else:
            x1, h2 = outs
            j = l // 2
            x_all = _ffn(h2, x1, mod_l, ffn_w1[j].astype(BF16), ffn_w3[j].astype(BF16), ffn_w2[j].astype(BF16))
        pc = proj[:SEG]
        ctx_out.append(dict(
            sw_k=pc[:, C_SK:C_SK + 128].reshape(BATCH, SEQ, SW_KV_HEADS, HEAD_DIM),
            sw_v=pc[:, C_SV:C_SV + 128].reshape(BATCH, SEQ, SW_KV_HEADS, HEAD_DIM),
            na_k=pc[:, C_NK:C_NK + 256].reshape(BATCH, SEQ, NA_HEADS, HEAD_DIM),
            na_v=pc[:, C_NV:C_NV + 256].reshape(BATCH, SEQ, NA_HEADS, HEAD_DIM),
            ml_C=c_new.reshape(BATCH, 2, ML_HEADS, HEAD_DIM, HEAD_DIM),
            ml_n=n_new.reshape(BATCH, 2, ML_HEADS, HEAD_DIM),
            ml_m=m_new[:, :, 0].reshape(BATCH, 2, ML_HEADS),
            rg_h=hl_new))
    stack = lambda name: jnp.stack([t[name] for t in ctx_out], axis=1)
    return (x_all[:SEG].reshape(BATCH, SEQ, D_MODEL), x_all[SEG:].reshape(DEC_BATCH, DEC_SEQ, D_MODEL),
            stack('sw_k'), stack('sw_v'), stack('na_k'), stack('na_v'),
            stack('ml_C'), stack('ml_n'), stack('ml_m'), stack('rg_h'))
```

```python
import functools

import numpy as np
import jax
import jax.numpy as jnp
from jax import lax
from jax.experimental import pallas as pl
from jax.experimental.pallas import tpu as pltpu

F32 = jnp.float32
BF16 = jnp.bfloat16

D_MODEL = 1024
BATCH = 16
SEQ = 256
DEPTH = 2
DEC_BATCH = 2
DEC_SEQ = 4096
PAST_LEN = 256
GRID_W = 64
HEAD_DIM = 64
ML_HEADS = 4
ML_CHUNK = 128
SW_HEADS = 4
SW_KV_HEADS = 2
SW_WINDOW = 128
RG_WIDTH = 256
RG_BLOCKS = 4
RG_CONV = 4
RG_C = 8.0
NA_HEADS = 4
NA_ROWS = 8
NA_COLS = 16
N_BRANCH = 4
ROPE_BASE = 10000.0
D_FF = 2816
N_EXPERTS = 8
D_FF_EXPERT = 2048
EPS = 1e-6
NEG = -1e30
SCALE = HEAD_DIM ** -0.5

SEG = 4096
N_SEG = 3
N_TOK = N_SEG * SEG
LANES = 128
VMEM_LIMIT = 56 * 1024 * 1024

C_MQ, C_MK, C_MV, C_MO = 0, 256, 512, 768
C_SQ, C_RX, C_RY, C_NQ, C_NK, C_NV = 1024, 1280, 1536, 1792, 2048, 2304
C_SK, C_SV, C_G = 2560, 2688, 2816
P_W = 2944


def _cparams(sem):
    return pltpu.CompilerParams(dimension_semantics=sem, vmem_limit_bytes=VMEM_LIMIT)


def _dot(a, b):
    return jnp.dot(a.astype(BF16), b.astype(BF16), preferred_element_type=F32)


def _dot_nt(a, b):
    return lax.dot_general(a.astype(BF16), b.astype(BF16), (((1,), (1,)), ((), ())),
                           preferred_element_type=F32)


def _dot_tn(a, b):
    return lax.dot_general(a.astype(BF16), b.astype(BF16), (((0,), (0,)), ((), ())),
                           preferred_element_type=F32)


def _split3(x):
    hi = x.astype(BF16)
    r1 = x - hi.astype(F32)
    mid = r1.astype(BF16)
    lo = (r1 - mid.astype(F32)).astype(BF16)
    return hi, mid, lo


def _dot_exact_rhs(a01, x):
    hi, mid, lo = _split3(x)
    d = lambda p: jnp.dot(a01, p, preferred_element_type=F32)
    return d(hi) + d(mid) + d(lo)


def _dot_exact_lhs(x, a01):
    hi, mid, lo = _split3(x)
    d = lambda p: jnp.dot(p, a01, preferred_element_type=F32)
    return d(hi) + d(mid) + d(lo)


def _rms(x, g):
    return x * lax.rsqrt(jnp.mean(x * x, axis=-1, keepdims=True) + EPS) * g


def _rms_head_pairs(x, g):
    lane = lax.broadcasted_iota(jnp.int32, x.shape, 1)
    left = lane < HEAD_DIM
    sq = x * x
    s0 = jnp.sum(jnp.where(left, sq, 0.0), axis=-1, keepdims=True)
    s1 = jnp.sum(jnp.where(left, 0.0, sq), axis=-1, keepdims=True)
    ms = jnp.where(left, s0, s1) * (1.0 / HEAD_DIM)
    return x * lax.rsqrt(ms + EPS) * g


MOD_TN = 1536
MOD_ROWS = 3


def _mod_kernel(ct_ref, w_ref, b_ref, o_ref):
    ct = ct_ref[...]
    st = ct * jax.nn.sigmoid(ct)
    w = w_ref[0]
    o_ref[...] = jnp.zeros(o_ref.shape, F32)
    for r in range(MOD_ROWS):
        o_ref[0, r:r + 1, :] = jnp.sum(w * st[:, r:r + 1], axis=0, keepdims=True) + b_ref[0]


def _mod_table(cvecs_t, w_ada, b_ada):
    n = 6 * D_MODEL
    return pl.pallas_call(
        _mod_kernel,
        grid=(DEPTH, n // MOD_TN),
        in_specs=[pl.BlockSpec((D_MODEL, 8), lambda l, j: (0, 0)),
                  pl.BlockSpec((1, D_MODEL, MOD_TN), lambda l, j: (l, 0, j)),
                  pl.BlockSpec((1, 1, MOD_TN), lambda l, j: (l, 0, j))],
        out_specs=pl.BlockSpec((1, 8, MOD_TN), lambda l, j: (l, 0, j)),
        out_shape=jax.ShapeDtypeStruct((DEPTH, 8, n), F32),
        compiler_params=_cparams(("arbitrary", "arbitrary")),
        name="adaln_mod",
    )(cvecs_t, w_ada, b_ada.reshape(DEPTH, 1, n))


IN_TM = 256


def _swap16(y):
    lane = lax.broadcasted_iota(jnp.int32, y.shape, 1)
    first = (lane % 32) < 16
    return jnp.where(first, pltpu.roll(y, LANES - 16, 1), pltpu.roll(y, 16, 1))


def _inproj_kernel(x_ref, mod_ref, g_ref, w_ref, qkg_ref, cos_ref, sin_ref, o_ref):
    seg = pl.program_id(0)
    mod = mod_ref[0]
    sh1 = mod[:, 0:D_MODEL]
    sc1 = mod[:, D_MODEL:2 * D_MODEL]
    h = _rms(x_ref[...], g_ref[...]) * (1.0 + sc1) + sh1
    r = jnp.dot(h.astype(BF16), w_ref[...], preferred_element_type=F32)
    o_ref[:, 0:C_SQ] = r[:, 0:C_SQ]
    o_ref[:, C_RX:C_NQ] = r[:, C_RX:C_NQ]
    o_ref[:, C_NV:C_SK] = r[:, C_NV:C_SK]
    o_ref[:, C_SV:P_W] = r[:, C_SV:P_W]
    cos = cos_ref[...]
    sin = sin_ref[...]
    latent = seg > 0

    def rope(y):
        return jnp.where(latent, y * cos + _swap16(y) * sin, y)

    for p in range(2):
        a = C_SQ + LANES * p
        o_ref[:, a:a + LANES] = rope(_rms_head_pairs(r[:, a:a + LANES], qkg_ref[0:1, :]))
    o_ref[:, C_SK:C_SK + LANES] = rope(_rms_head_pairs(r[:, C_SK:C_SK + LANES], qkg_ref[1:2, :]))
    for p in range(2):
        a = C_NQ + LANES * p
        o_ref[:, a:a + LANES] = _rms_head_pairs(r[:, a:a + LANES], qkg_ref[2:3, :])
        a = C_NK + LANES * p
        o_ref[:, a:a + LANES] = _rms_head_pairs(r[:, a:a + LANES], qkg_ref[3:4, :])


def _inproj(x_all, mod_l, norm1, w_in_p, qk_gains, cos_t, sin_t):
    nt = SEG // IN_TM
    return pl.pallas_call(
        _inproj_kernel,
        grid=(N_SEG, nt),
        in_specs=[pl.BlockSpec((IN_TM, D_MODEL), lambda s, i: (s * nt + i, 0)),
                  pl.BlockSpec((1, 1, 6 * D_MODEL), lambda s, i: (s, 0, 0)),
                  pl.BlockSpec((1, D_MODEL), lambda s, i: (0, 0)),
                  pl.BlockSpec((D_MODEL, P_W), lambda s, i: (0, 0)),
                  pl.BlockSpec((4, LANES), lambda s, i: (0, 0)),
                  pl.BlockSpec((IN_TM, LANES), lambda s, i: (i, 0)),
                  pl.BlockSpec((IN_TM, LANES), lambda s, i: (i, 0))],
        out_specs=pl.BlockSpec((IN_TM, P_W), lambda s, i: (s * nt + i, 0)),
        out_shape=jax.ShapeDtypeStruct((N_TOK, P_W), F32),
        compiler_params=_cparams(("arbitrary", "arbitrary")),
        name="inproj",
    )(x_all, mod_l, norm1, w_in_p, qk_gains, cos_t, sin_t)


def _mlstm_direction(d, q_ref, k_ref, v_ref, g_ref, bias, cbd, nbd, m_s):
    ch = ML_CHUNK
    r_io = lax.broadcasted_iota(jnp.int32, (ch, ch), 0)
    c_io = lax.broadcasted_iota(jnp.int32, (ch, ch), 1)
    lower = r_io >= c_io
    upper = r_io <= c_io
    mask = lower if d == 0 else upper
    tri = mask.astype(BF16)
    tri_t = (upper if d == 0 else lower).astype(BF16)
    left = c_io < HEAD_DIM
    top = r_io < HEAD_DIM
    blockdiag = top == left
    g = g_ref[...] + bias[...]
    b_cols = _dot_exact_rhs(tri, jax.nn.log_sigmoid(g))
    b3 = jnp.concatenate(_split3(b_cols), axis=1)
    gt = g.T
    li_rows = gt[0:8, :]
    b_rows = _dot_exact_lhs(jax.nn.log_sigmoid(gt[8:16, :]), tri_t)
    a_rows = li_rows - b_rows
    bl = b_rows[:, ch - 1:ch] if d == 0 else b_rows[:, 0:1]
    m_old = m_s[:, 0:1]
    g_rows = bl - b_rows + li_rows
    m_new = jnp.maximum(bl + m_old, jnp.max(g_rows, axis=1, keepdims=True))
    wk_rows = jnp.exp(g_rows - m_new)
    wp = jnp.exp(bl + m_old - m_new)
    sel_r = lax.broadcasted_iota(jnp.int32, (3 * LANES, LANES), 0) % LANES
    sel_left = lax.broadcasted_iota(jnp.int32, (3 * LANES, LANES), 1) < HEAD_DIM
    ones_blk = jnp.ones((ch, LANES), BF16)
    left2 = lax.broadcasted_iota(jnp.int32, (ch, 2 * LANES), 1) % LANES < HEAD_DIM
    h_out, c_out, n_out = [], [], []
    for p in range(ML_HEADS // 2):
        lanes = slice(LANES * p, LANES * (p + 1))
        j0 = ML_HEADS * d + 2 * p
        q2 = q_ref[:, lanes]
        k2t = (k_ref[:, lanes] * SCALE).T.astype(BF16)
        v2e = jnp.concatenate([v_ref[:, lanes].astype(BF16), ones_blk], axis=1)
        cb = cbd[2 * d + p]
        nb = nbd[2 * d + p]
        q2b = q2.astype(BF16)
        q_lo = (q2 - q2b.astype(F32)).astype(BF16)
        nb_hi = nb.astype(BF16)
        nb_lo = (nb - nb_hi.astype(F32)).astype(BF16)
        qc = jnp.dot(q2b, cb.astype(BF16), preferred_element_type=F32)
        qn = jnp.dot(jnp.concatenate([q2b, q_lo, q2b], axis=1), jnp.concatenate([nb_hi, nb_hi, nb_lo], axis=0),
                     preferred_element_type=F32)
        sel = (sel_r == jnp.where(sel_left, 8 + j0, 9 + j0)).astype(BF16)
        b_pair = jnp.dot(b3, sel, preferred_element_type=F32)
        cbs, sves = [], []
        for i in range(2):
            j = j0 + i
            half = left if i == 0 else jnp.logical_not(left)
            a_mat = jnp.where(mask, a_rows[j:j + 1, :], NEG)
            cvec = jnp.maximum(m_old[j:j + 1, :], jnp.max(a_mat, axis=1, keepdims=True))
            cbro = jnp.broadcast_to(cvec, (ch, ch))
            s = jnp.dot(jnp.where(half, q2b, 0), k2t, preferred_element_type=F32) * jnp.exp(a_mat - cbro)
            s_hi = s.astype(BF16)
            s_lo = (s - s_hi.astype(F32)).astype(BF16)
            sve = jnp.dot(s_hi, v2e, preferred_element_type=F32)
            rs_lo = jnp.dot(s_lo, ones_blk, preferred_element_type=F32)
            sves.append(jnp.concatenate([sve[:, 0:LANES], sve[:, LANES:2 * LANES] + rs_lo], axis=1))
            cbs.append(cbro)
        c_pair = jnp.where(left, cbs[0], cbs[1])
        w_prev = jnp.exp(jnp.where(left, m_old[j0:j0 + 1, :], m_old[j0 + 1:j0 + 2, :]) - c_pair)
        sve = jnp.where(left2, sves[0], sves[1])
        num = w_prev * qc + sve[:, 0:LANES]
        den = w_prev * qn + sve[:, LANES:2 * LANES]
        h_out.append(num / jnp.maximum(jnp.abs(den), jnp.exp(-(c_pair + b_pair))))
        kwt = k2t * jnp.where(top, wk_rows[j0:j0 + 1, :], wk_rows[j0 + 1:j0 + 2, :])
        kwt_hi = kwt.astype(BF16)
        kwt_lo = (kwt - kwt_hi.astype(F32)).astype(BF16)
        kve = jnp.dot(kwt_hi, v2e, preferred_element_type=F32)
        kn = kve[:, LANES:2 * LANES] + jnp.dot(kwt_lo, ones_blk, preferred_element_type=F32)
        wp_pair = jnp.where(top, wp[j0:j0 + 1, :], wp[j0 + 1:j0 + 2, :])
        c_out.append(wp_pair * cb + jnp.where(blockdiag, kve[:, 0:LANES], 0.0))
        n_out.append(wp_pair * nb + jnp.where(blockdiag, kn, 0.0))
    return h_out, c_out, n_out, m_new


def _mlstm_kernel(qf, kf, vf, gf, qb, kb, vb, gb, c0, n0, m0, bias,
                  hf, hb, co, no, mo, cbd, nbd, m_s, *, nc):
    c = pl.program_id(1)

    @pl.when(c == 0)
    def _():
        cbd[...] = c0[0]
        nbd[...] = n0[0]
        m_s[...] = m0[0]

    res = [_mlstm_direction(d, *refs, bias, cbd, nbd, m_s)
           for d, refs in enumerate(((qf, kf, vf, gf), (qb, kb, vb, gb)))]
    for d, (h_ref, (h_out, c_out, n_out, m_new)) in enumerate(zip((hf, hb), res)):
        for p in range(ML_HEADS // 2):
            h_ref[:, LANES * p:LANES * (p + 1)] = h_out[p]
            cbd[2 * d + p] = c_out[p]
            nbd[2 * d + p] = n_out[p]
        rows = slice(ML_HEADS * d, ML_HEADS * (d + 1))
        m_s[rows, :] = jnp.broadcast_to(m_new[rows, :], (ML_HEADS, LANES))

    @pl.when(c == nc - 1)
    def _():
        co[0] = cbd[...]
        no[0] = nbd[...]
        mo[0] = m_s[...]


def _mlstm_pack_state(c0, n0, m0):
    bsz = c0.shape[0]
    hd = HEAD_DIM
    cbd = jnp.zeros((bsz, ML_HEADS, LANES, LANES), F32)
    cbd = cbd.at[:, :, :hd, :hd].set(c0[:, 0::2]).at[:, :, hd:, hd:].set(c0[:, 1::2])
    nbd = jnp.zeros((bsz, ML_HEADS, LANES, LANES), F32)
    rep = lambda t: jnp.broadcast_to(t[..., None], t.shape + (hd,))
    nbd = nbd.at[:, :, :hd, :hd].set(rep(n0[:, 0::2])).at[:, :, hd:, hd:].set(rep(n0[:, 1::2]))
    return cbd, nbd, jnp.broadcast_to(m0[..., None], m0.shape + (LANES,))


def _mlstm_unpack_state(cbd, nbd, mrow):
    hd = HEAD_DIM
    bsz = cbd.shape[0]
    c = jnp.stack([cbd[:, :, :hd, :hd], cbd[:, :, hd:, hd:]], axis=2).reshape(bsz, 2 * ML_HEADS, hd, hd)
    n = jnp.stack([nbd[:, :, :hd, 0], nbd[:, :, hd:, hd]], axis=2).reshape(bsz, 2 * ML_HEADS, hd)
    return c, n, mrow[:, :, 0]


def _mlstm(proj, gate_bias, c0, n0, m0, *, bsz, seq, row0):
    nc = seq // ML_CHUNK
    base = row0 // ML_CHUNK
    nj = 2 * ML_HEADS
    fw = lambda col: (lambda b, c: (base + b * nc + c, col))
    bw = lambda col: (lambda b, c: (base + b * nc + nc - 1 - c, col))
    qkv = lambda f: [pl.BlockSpec((ML_CHUNK, 256), f(C_MQ // 256)),
                     pl.BlockSpec((ML_CHUNK, 256), f(C_MK // 256)),
                     pl.BlockSpec((ML_CHUNK, 256), f(C_MV // 256)),
                     pl.BlockSpec((ML_CHUNK, LANES), f(C_G // LANES))]
    st_specs = [pl.BlockSpec((1, nj // 2, LANES, LANES), lambda b, c: (b, 0, 0, 0)),
                pl.BlockSpec((1, nj // 2, LANES, LANES), lambda b, c: (b, 0, 0, 0)),
                pl.BlockSpec((1, nj, LANES), lambda b, c: (b, 0, 0))]
    st_shapes = [jax.ShapeDtypeStruct((bsz, nj // 2, LANES, LANES), F32),
                 jax.ShapeDtypeStruct((bsz, nj // 2, LANES, LANES), F32),
                 jax.ShapeDtypeStruct((bsz, nj, LANES), F32)]
    return pl.pallas_call(
        functools.partial(_mlstm_kernel, nc=nc),
        grid=(bsz, nc),
        in_specs=qkv(fw) + qkv(bw) + st_specs + [pl.BlockSpec((1, LANES), lambda b, c: (0, 0))],
        out_specs=[pl.BlockSpec((ML_CHUNK, 256), lambda b, c: (b * nc + c, 0)),
                   pl.BlockSpec((ML_CHUNK, 256), lambda b, c: (b * nc + nc - 1 - c, 0))] + st_specs,
        out_shape=[jax.ShapeDtypeStruct((bsz * seq, 256), F32),
                   jax.ShapeDtypeStruct((bsz * seq, 256), F32)] + st_shapes,
        scratch_shapes=[pltpu.VMEM((nj // 2, LANES, LANES), F32),
                        pltpu.VMEM((nj // 2, LANES, LANES), F32),
                        pltpu.VMEM((nj, LANES), F32)],
        compiler_params=_cparams(("arbitrary", "arbitrary")),
        name="mlstm",
    )(proj, proj, proj, proj, proj, proj, proj, proj, c0, n0, m0, gate_bias)


RG_TC = 256
RG_PAD = 8


def _rglru_kernel(rx, ry, cw, cb, wg, bg, lam, h0, oc, hl, xpad, af, ab, ub, *, seq):
    xpad[0:RG_PAD, :] = jnp.zeros((RG_PAD, RG_WIDTH), F32)
    xpad[seq + RG_PAD:seq + 2 * RG_PAD, :] = jnp.zeros((RG_PAD, RG_WIDTH), F32)
    xpad[RG_PAD:seq + RG_PAD, :] = rx[...]
    sp = jax.nn.softplus(-lam[...])
    left = (RG_CONV - 1) // 2
    for ci in range(seq // RG_TC):
        s0 = ci * RG_TC
        xc = None
        for j in range(RG_CONV):
            a = RG_PAD + s0 + j - left
            term = xpad[a:a + RG_TC, :] * cw[j:j + 1, :]
            xc = term if xc is None else xc + term
        xc = xc + cb[...]
        pre = _dot(xc, wg[...]) + bg[...]
        for d in range(2):
            o = 2 * RG_WIDTH * d
            r = jax.nn.sigmoid(pre[:, o:o + RG_WIDTH])
            gi = jax.nn.sigmoid(pre[:, o + RG_WIDTH:o + 2 * RG_WIDTH])
            log_a = -RG_C * r * sp[d:d + 1, :]
            a_val = jnp.exp(log_a)
            u_val = jnp.sqrt(-jnp.tanh(log_a) * (a_val * a_val + 1.0)) * (gi * xc)
            if d == 0:
                af[s0:s0 + RG_TC, :] = a_val
                oc[s0:s0 + RG_TC, :] = u_val
            else:
                ab[s0:s0 + RG_TC, :] = a_val
                ub[s0:s0 + RG_TC, :] = u_val

    def body(t, carry):
        h_f, h_b = carry
        h_f = af[pl.ds(t, 1), :] * h_f + oc[pl.ds(t, 1), :]
        oc[pl.ds(t, 1), :] = h_f
        tb = seq - 1 - t
        h_b = ab[pl.ds(tb, 1), :] * h_b + ub[pl.ds(tb, 1), :]
        ub[pl.ds(tb, 1), :] = h_b
        return h_f, h_b

    h_f, h_b = lax.fori_loop(0, seq, body, (h0[0, 0:1, :], h0[0, 1:2, :]), unroll=8)
    hl[0, 0:1, :] = h_f
    hl[0, 1:2, :] = h_b
    for ci in range(seq // RG_TC):
        sl = slice(ci * RG_TC, (ci + 1) * RG_TC)
        oc[sl, :] = (oc[sl, :] + ub[sl, :]) * jax.nn.gelu(ry[sl, :])


def _rglru(proj, cw, cb, wg, bg, lam, h0, *, bsz, seq, row0):
    base = row0 // seq
    full = lambda shape: pl.BlockSpec(shape, lambda b: tuple(0 for _ in shape))
    return pl.pallas_call(
        functools.partial(_rglru_kernel, seq=seq),
        grid=(bsz,),
        in_specs=[pl.BlockSpec((seq, RG_WIDTH), lambda b: (base + b, C_RX // RG_WIDTH)),
                  pl.BlockSpec((seq, RG_WIDTH), lambda b: (base + b, C_RY // RG_WIDTH)),
                  full((RG_CONV, RG_WIDTH)), full((1, RG_WIDTH)),
                  full((RG_WIDTH, 4 * RG_WIDTH)), full((1, 4 * RG_WIDTH)), full((2, RG_WIDTH)),
                  pl.BlockSpec((1, 2, RG_WIDTH), lambda b: (b, 0, 0))],
        out_specs=[pl.BlockSpec((seq, RG_WIDTH), lambda b: (b, 0)),
                   pl.BlockSpec((1, 2, RG_WIDTH), lambda b: (b, 0, 0))],
        out_shape=[jax.ShapeDtypeStruct((bsz * seq, RG_WIDTH), F32),
                   jax.ShapeDtypeStruct((bsz, 2, RG_WIDTH), F32)],
        scratch_shapes=[pltpu.VMEM((seq + 2 * RG_PAD, RG_WIDTH), F32),
                        pltpu.VMEM((seq, RG_WIDTH), F32),
                        pltpu.VMEM((seq, RG_WIDTH), F32),
                        pltpu.VMEM((seq, RG_WIDTH), F32)],
        compiler_params=_cparams(("arbitrary",)),
        name="rglru",
    )(proj, proj, cw, cb, wg, bg, lam, h0)


def _softmax_pv(scores, values, sink):
    m = functools.reduce(jnp.maximum, [jnp.max(s, axis=-1, keepdims=True) for s in scores])
    if sink is not None:
        m = jnp.maximum(m, sink)
    ps = [jnp.exp(s - m) for s in scores]
    den = functools.reduce(jnp.add, [jnp.sum(p, axis=-1, keepdims=True) for p in ps])
    if sink is not None:
        den = den + jnp.exp(sink - m)
    num = functools.reduce(jnp.add, [_dot(p, v) for p, v in zip(ps, values)])
    return num / den


def _ctx_attn_kernel(sink, sq, sk, sv, nq, nk, nv, ob, od):
    for h in range(SW_HEADS):
        j = h // (SW_HEADS // SW_KV_HEADS)
        sl = slice(HEAD_DIM * h, HEAD_DIM * (h + 1))
        kv = slice(HEAD_DIM * j, HEAD_DIM * (j + 1))
        s = _dot_nt(sq[:, sl], sk[:, kv]) * SCALE
        ob[:, sl] = _softmax_pv([s], [sv[:, kv]], sink[h])
    for h in range(NA_HEADS):
        sl = slice(HEAD_DIM * h, HEAD_DIM * (h + 1))
        s = _dot_nt(nq[:, sl], nk[:, sl]) * SCALE
        od[:, sl] = _softmax_pv([s], [nv[:, sl]], None)


def _ctx_attn(proj, sink):
    blk = lambda w, col: pl.BlockSpec((SEQ, w), lambda b: (b, col))
    return pl.pallas_call(
        _ctx_attn_kernel,
        grid=(BATCH,),
        in_specs=[pl.BlockSpec(memory_space=pltpu.SMEM),
                  blk(256, C_SQ // 256), blk(128, C_SK // 128), blk(128, C_SV // 128),
                  blk(256, C_NQ // 256), blk(256, C_NK // 256), blk(256, C_NV // 256)],
        out_specs=[pl.BlockSpec((SEQ, 256), lambda b: (b, 0)),
                   pl.BlockSpec((SEQ, 256), lambda b: (b, 0))],
        out_shape=[jax.ShapeDtypeStruct((SEG, 256), F32), jax.ShapeDtypeStruct((SEG, 256), F32)],
        compiler_params=_cparams(("arbitrary",)),
        name="ctx_attn",
    )(sink, proj, proj, proj, proj, proj, proj)


SW_QB = 128
SW_SPAN = SW_QB + 2 * SW_WINDOW


def _swa_kernel(sink, q, k, v, kc, vc, ob):
    n = pl.program_id(1)
    ws = jnp.clip((n - 1) * SW_QB, 0, DEC_SEQ - SW_SPAN)
    ws = pl.multiple_of(ws, SW_QB)
    qpos = n * SW_QB + lax.broadcasted_iota(jnp.int32, (SW_QB, SW_SPAN), 0)
    kpos = ws + lax.broadcasted_iota(jnp.int32, (SW_QB, SW_SPAN), 1)
    valid = jnp.abs(qpos - kpos) <= SW_WINDOW
    for h in range(SW_HEADS):
        j = h // (SW_HEADS // SW_KV_HEADS)
        sl = slice(HEAD_DIM * h, HEAD_DIM * (h + 1))
        kv = slice(HEAD_DIM * j, HEAD_DIM * (j + 1))
        qh = q[:, sl]
        s_loc = jnp.where(valid, _dot_nt(qh, k[pl.ds(ws, SW_SPAN), kv]) * SCALE, NEG)
        s_ctx = _dot_nt(qh, kc[0, :, kv]) * SCALE
        ob[:, sl] = _softmax_pv([s_loc, s_ctx], [v[pl.ds(ws, SW_SPAN), kv], vc[0, :, kv]], sink[h])


def _swa(proj, kc, vc, sink):
    nq = DEC_SEQ // SW_QB
    qbase = SEG // SW_QB
    return pl.pallas_call(
        _swa_kernel,
        grid=(DEC_BATCH, nq),
        in_specs=[pl.BlockSpec(memory_space=pltpu.SMEM),
                  pl.BlockSpec((SW_QB, 256), lambda b, n: (qbase + b * nq + n, C_SQ // 256)),
                  pl.BlockSpec((DEC_SEQ, 128), lambda b, n: (1 + b, C_SK // 128)),
                  pl.BlockSpec((DEC_SEQ, 128), lambda b, n: (1 + b, C_SV // 128)),
                  pl.BlockSpec((1, PAST_LEN, 128), lambda b, n: (b, 0, 0)),
                  pl.BlockSpec((1, PAST_LEN, 128), lambda b, n: (b, 0, 0))],
        out_specs=pl.BlockSpec((SW_QB, 256), lambda b, n: (b * nq + n, 0)),
        out_shape=jax.ShapeDtypeStruct((DEC_BATCH * DEC_SEQ, 256), F32),
        compiler_params=_cparams(("arbitrary", "arbitrary")),
        name="swa",
    )(sink, proj, proj, proj, kc, vc)


NA_RPB_R = 2 * NA_ROWS - 1
NA_RPB_C = 2 * NA_COLS - 1
NA_NKEY = NA_ROWS * GRID_W
GRID_ROWS = DEC_SEQ // GRID_W


def _na_bias_kernel(rpb, out):
    h = pl.program_id(0)
    qc = lax.broadcasted_iota(jnp.int32, (GRID_W, GRID_W), 0)
    kc = lax.broadcasted_iota(jnp.int32, (GRID_W, GRID_W), 1)
    dc = jnp.clip(kc - qc, -(NA_COLS - 1), NA_COLS - 1) + NA_COLS - 1
    lo = jnp.clip(qc - NA_COLS // 2, 0, GRID_W - NA_COLS)
    valid = (kc >= lo) & (kc < lo + NA_COLS)
    tiles = []
    for dr in range(NA_RPB_R):
        t = jnp.zeros((GRID_W, GRID_W), F32)
        for j in range(NA_RPB_C):
            t = jnp.where(dc == j, rpb[(h * NA_RPB_R + dr) * NA_RPB_C + j], t)
        tiles.append(jnp.where(valid, t, NEG))
    for w in range(NA_ROWS):
        for i in range(NA_ROWS):
            out[0, w, :, GRID_W * i:GRID_W * (i + 1)] = tiles[w + i]


def _na_bias(rpb_flat):
    return pl.pallas_call(
        _na_bias_kernel,
        grid=(NA_HEADS,),
        in_specs=[pl.BlockSpec(memory_space=pltpu.SMEM)],
        out_specs=pl.BlockSpec((1, NA_ROWS, GRID_W, NA_NKEY), lambda h: (h, 0, 0, 0)),
        out_shape=jax.ShapeDtypeStruct((NA_HEADS, NA_ROWS, GRID_W, NA_NKEY), F32),
        compiler_params=_cparams(("arbitrary",)),
        name="na_bias",
    )(rpb_flat)


def _na_kernel(q, k, v, kc, vc, bias, od):
    r = pl.program_id(1)
    r0 = jnp.clip(r - NA_ROWS // 2, 0, GRID_ROWS - NA_ROWS)
    k0 = pl.multiple_of(r0 * GRID_W, GRID_W)
    win = r0 - r + NA_ROWS - 1
    for h in range(NA_HEADS):
        sl = slice(HEAD_DIM * h, HEAD_DIM * (h + 1))
        qh = q[:, sl]
        s_loc = _dot_nt(qh, k[pl.ds(k0, NA_NKEY), sl]) * SCALE + bias[h, pl.ds(win, 1)][0]
        s_ctx = _dot_nt(qh, kc[0, :, sl]) * SCALE
        od[:, sl] = _softmax_pv([s_loc, s_ctx], [v[pl.ds(k0, NA_NKEY), sl], vc[0, :, sl]], None)


def _na(proj, kc, vc, bias):
    qbase = SEG // GRID_W
    return pl.pallas_call(
        _na_kernel,
        grid=(DEC_BATCH, GRID_ROWS),
        in_specs=[pl.BlockSpec((GRID_W, 256), lambda b, r: (qbase + b * GRID_ROWS + r, C_NQ // 256)),
                  pl.BlockSpec((DEC_SEQ, 256), lambda b, r: (1 + b, C_NK // 256)),
                  pl.BlockSpec((DEC_SEQ, 256), lambda b, r: (1 + b, C_NV // 256)),
                  pl.BlockSpec((1, PAST_LEN, 256), lambda b, r: (b, 0, 0)),
                  pl.BlockSpec((1, PAST_LEN, 256), lambda b, r: (b, 0, 0)),
                  pl.BlockSpec((NA_HEADS, NA_ROWS, GRID_W, NA_NKEY), lambda b, r: (0, 0, 0, 0))],
        out_specs=pl.BlockSpec((GRID_W, 256), lambda b, r: (b * GRID_ROWS + r, 0)),
        out_shape=jax.ShapeDtypeStruct((DEC_BATCH * DEC_SEQ, 256), F32),
        compiler_params=_cparams(("arbitrary", "arbitrary")),
        name="na",
    )(proj, proj, proj, kc, vc, bias)


MG_TM = 256


def _merge_kernel(x_ref, mod_ref, g1_ref, g2_ref, hf_c, hf_l, hb_c, hb_l, mo, ob_c, ob_l, oc_c, oc_l, od_c, od_l,
                  hn, wmg, bmg, wbr, wout, *rest, moe):
    if moe:
        wrt, br, x1_ref, h2_ref, route_ref = rest
    else:
        x1_ref, h2_ref = rest
    ctx = pl.program_id(0) == 0
    pick = lambda c_ref, l_ref: jnp.where(ctx, c_ref[...], l_ref[...])
    mod = mod_ref[0]
    chunk = lambda i: mod[:, i * D_MODEL:(i + 1) * D_MODEL]
    sh1, sc1, gate1, sh2, sc2 = chunk(0), chunk(1), chunk(2), chunk(3), chunk(4)
    x = x_ref[...]
    h = (_rms(x, g1_ref[...]) * (1.0 + sc1) + sh1).astype(BF16)
    hsum = pick(hf_c, hf_l) + pick(hb_c, hb_l)
    out_a = jnp.concatenate(
        [_rms_head_pairs(hsum[:, LANES * p:LANES * (p + 1)], hn[...]) for p in range(2)], axis=-1)
    out_a = out_a * jax.nn.sigmoid(mo[...])
    acc = None
    for n, br_val in enumerate((out_a, pick(ob_c, ob_l), pick(oc_c, oc_l), pick(od_c, od_l))):
        gate = jax.nn.sigmoid(jnp.dot(h, wmg[:, n * D_MODEL:(n + 1) * D_MODEL], preferred_element_type=F32)
                              + bmg[:, n * D_MODEL:(n + 1) * D_MODEL])
        term = gate * jnp.dot(br_val.astype(BF16), wbr[n], preferred_element_type=F32)
        acc = term if acc is None else acc + term
    y = jnp.dot(acc.astype(BF16), wout[...], preferred_element_type=F32)
    x1 = x + gate1 * y
    x1_ref[...] = x1
    h2 = _rms(x1, g2_ref[...]) * (1.0 + sc2) + sh2
    h2_ref[...] = h2.astype(h2_ref.dtype)
    if moe:
        logit = [jnp.sum(h2 * wrt[e:e + 1, :], axis=-1, keepdims=True) + br[e] for e in range(N_EXPERTS)]
        v1, i1 = logit[0], jnp.zeros(logit[0].shape, jnp.int32)
        for e in range(1, N_EXPERTS):
            better = logit[e] > v1
            v1 = jnp.where(better, logit[e], v1)
            i1 = jnp.where(better, e, i1)
        v2, i2 = jnp.full(v1.shape, -jnp.inf, F32), jnp.zeros(v1.shape, jnp.int32)
        for e in range(N_EXPERTS):
            better = (i1 != e) & (logit[e] > v2)
            v2 = jnp.where(better, logit[e], v2)
            i2 = jnp.where(better, e, i2)
        e2 = jnp.exp(v2 - v1)
        den = 1.0 + e2
        lane = lax.broadcasted_iota(jnp.int32, route_ref.shape, 1)
        route = jnp.where(lane == 0, 1.0 / den, 0.0) + jnp.where(lane == 1, e2 / den, 0.0)
        route = route + jnp.where(lane == 2, i1.astype(F32), 0.0) + jnp.where(lane == 3, i2.astype(F32), 0.0)
        route_ref[...] = route


def _merge(x_all, mod_l, g1, g2, hf, hb, proj, ob, oc, od, hn, wmg, bmg, wbr, wout, router=None):
    nt = SEG // MG_TM
    moe = router is not None
    row = lambda w: pl.BlockSpec((MG_TM, w), lambda s, i: (s * nt + i, 0))
    ctx_blk = pl.BlockSpec((MG_TM, 256), lambda s, i: (jnp.minimum(s * nt + i, nt - 1), 0))
    lat_blk = pl.BlockSpec((MG_TM, 256), lambda s, i: (jnp.maximum(s * nt + i - nt, 0), 0))
    full = lambda shape: pl.BlockSpec(shape, lambda s, i: tuple(0 for _ in shape))
    in_specs = [row(D_MODEL),
                pl.BlockSpec((1, 1, 6 * D_MODEL), lambda s, i: (s, 0, 0)),
                full((1, D_MODEL)), full((1, D_MODEL)),
                ctx_blk, lat_blk, ctx_blk, lat_blk,
                pl.BlockSpec((MG_TM, 256), lambda s, i: (s * nt + i, C_MO // 256)),
                ctx_blk, lat_blk, ctx_blk, lat_blk, ctx_blk, lat_blk,
                full((1, LANES)), full((D_MODEL, N_BRANCH * D_MODEL)), full((1, N_BRANCH * D_MODEL)),
                full((N_BRANCH, 256, D_MODEL)), full((D_MODEL, D_MODEL))]
    args = [x_all, mod_l, g1, g2, *hf, *hb, proj, *ob, *oc, *od, hn, wmg, bmg, wbr, wout]
    out_specs = [row(D_MODEL), row(D_MODEL)]
    out_shape = [jax.ShapeDtypeStruct((N_TOK, D_MODEL), F32),
                 jax.ShapeDtypeStruct((N_TOK, D_MODEL), F32 if moe else BF16)]
    if moe:
        in_specs += [full((N_EXPERTS, D_MODEL)), pl.BlockSpec(memory_space=pltpu.SMEM)]
        args += list(router)
        out_specs.append(row(LANES))
        out_shape.append(jax.ShapeDtypeStruct((N_TOK, LANES), F32))
    return pl.pallas_call(
        functools.partial(_merge_kernel, moe=moe),
        grid=(N_SEG, nt),
        in_specs=in_specs,
        out_specs=out_specs,
        out_shape=out_shape,
        compiler_params=_cparams(("arbitrary", "arbitrary")),
        name="merge",
    )(*args)


FF_TM = 512
FF_TF = 1408


def _ffn_kernel(h2, w1, w3, w2, x1, mod_ref, out, acc):
    f = pl.program_id(1)

    @pl.when(f == 0)
    def _():
        acc[...] = jnp.zeros(acc.shape, F32)

    h = h2[...]
    a = jnp.dot(h, w1[...], preferred_element_type=F32)
    b = jnp.dot(h, w3[...], preferred_element_type=F32)
    act = (jax.nn.silu(a) * b).astype(BF16)
    acc[...] += jnp.dot(act, w2[...], preferred_element_type=F32)

    @pl.when(f == pl.num_programs(1) - 1)
    def _():
        gate2 = mod_ref[0][:, 5 * D_MODEL:6 * D_MODEL]
        out[...] = x1[...] + gate2 * acc[...]


def _ffn(h2, x1, mod_l, w1, w3, w2):
    nt = N_TOK // FF_TM
    per_seg = SEG // FF_TM
    return pl.pallas_call(
        _ffn_kernel,
        grid=(nt, D_FF // FF_TF),
        in_specs=[pl.BlockSpec((FF_TM, D_MODEL), lambda i, f: (i, 0)),
                  pl.BlockSpec((D_MODEL, FF_TF), lambda i, f: (0, f)),
                  pl.BlockSpec((D_MODEL, FF_TF), lambda i, f: (0, f)),
                  pl.BlockSpec((FF_TF, D_MODEL), lambda i, f: (f, 0)),
                  pl.BlockSpec((FF_TM, D_MODEL), lambda i, f: (i, 0)),
                  pl.BlockSpec((1, 1, 6 * D_MODEL), lambda i, f: (i // per_seg, 0, 0))],
        out_specs=pl.BlockSpec((FF_TM, D_MODEL), lambda i, f: (i, 0)),
        out_shape=jax.ShapeDtypeStruct((N_TOK, D_MODEL), F32),
        scratch_shapes=[pltpu.VMEM((FF_TM, D_MODEL), F32)],
        compiler_params=_cparams(("arbitrary", "arbitrary")),
        name="ffn",
    )(h2, w1, w3, w2, x1, mod_l)


MOE_TM = 256
MOE_SLOTS = 2 * N_TOK
MOE_ROWS = MOE_SLOTS + N_EXPERTS * MOE_TM
MOE_TILES = MOE_ROWS // MOE_TM
MOE_UNROLL = 8


def _moe_group_kernel(texp, nused, slots, h2_hbm, w1, w3, w2, y_hbm, xs, ys, sem_in, sem_out):
    del texp
    i = pl.program_id(0)
    n_used = nused[0]
    buf = i % 2

    def gather_copy(tile, b, r):
        s = slots[tile * MOE_TM + r]
        tok = jnp.maximum(s, 0) // 2
        return s >= 0, pltpu.make_async_copy(h2_hbm.at[pl.ds(tok, 1)], xs.at[b, pl.ds(r, 1)], sem_in.at[b])

    def scatter_copy(tile, b, r):
        s = slots[tile * MOE_TM + r]
        dst = y_hbm.at[pl.ds(jnp.maximum(s, 0), 1)]
        return s >= 0, pltpu.make_async_copy(ys.at[b, pl.ds(r, 1)], dst, sem_out.at[b])

    def for_rows(make_copy, tile, b, fn):
        def body(r, carry):
            valid, copy = make_copy(tile, b, r)

            @pl.when(valid)
            def _():
                fn(copy)
            return carry
        lax.fori_loop(0, MOE_TM, body, 0, unroll=MOE_UNROLL)

    start = lambda copy: copy.start()
    wait = lambda copy: copy.wait()

    @pl.when(i == 0)
    def _():
        xs[...] = jnp.zeros(xs.shape, F32)
        for_rows(gather_copy, 0, 0, start)

    @pl.when(i + 1 < n_used)
    def _():
        for_rows(gather_copy, i + 1, 1 - buf, start)

    @pl.when(i < n_used)
    def _():
        for_rows(gather_copy, i, buf, wait)

        @pl.when(i >= 2)
        def _():
            for_rows(scatter_copy, i - 2, buf, wait)

        x = xs[buf].astype(BF16)
        a = jnp.dot(x, w1[0], preferred_element_type=F32)
        b = jnp.dot(x, w3[0], preferred_element_type=F32)
        act = (jax.nn.silu(a) * b).astype(BF16)
        ys[buf] = jnp.dot(act, w2[0], preferred_element_type=F32)
        for_rows(scatter_copy, i, buf, start)

    @pl.when(i == MOE_TILES - 1)
    def _():
        for_rows(scatter_copy, n_used - 2, n_used % 2, wait)
        for_rows(scatter_copy, n_used - 1, (n_used - 1) % 2, wait)


def _moe_group(tile_expert, n_used, slots, h2, w1, w3, w2):
    wspec = lambda shape: pl.BlockSpec((1,) + shape, lambda i, texp, nused, slots: (texp[i], 0, 0))
    grid_spec = pltpu.PrefetchScalarGridSpec(
        num_scalar_prefetch=3,
        grid=(MOE_TILES,),
        in_specs=[pl.BlockSpec(memory_space=pl.ANY),
                  wspec((D_MODEL, D_FF_EXPERT)), wspec((D_MODEL, D_FF_EXPERT)), wspec((D_FF_EXPERT, D_MODEL))],
        out_specs=pl.BlockSpec(memory_space=pl.ANY),
        scratch_shapes=[pltpu.VMEM((2, MOE_TM, D_MODEL), F32), pltpu.VMEM((2, MOE_TM, D_MODEL), F32),
                        pltpu.SemaphoreType.DMA((2,)), pltpu.SemaphoreType.DMA((2,))])
    return pl.pallas_call(
        _moe_group_kernel,
        grid_spec=grid_spec,
        out_shape=jax.ShapeDtypeStruct((MOE_SLOTS, D_MODEL), F32),
        compiler_params=_cparams(("arbitrary",)),
        name="moe_group",
    )(tile_expert, n_used, slots, h2, w1, w3, w2)


def _moe_plan(expert_ids):
    e_flat = expert_ids.reshape(-1)
    order = jnp.argsort(e_flat, stable=True).astype(jnp.int32)
    counts = jnp.sum((e_flat[:, None] == jnp.arange(N_EXPERTS)[None, :]).astype(jnp.int32), axis=0)
    padded = (counts + MOE_TM - 1) // MOE_TM * MOE_TM
    pend = jnp.cumsum(padded)
    pstart = pend - padded
    ustart = jnp.cumsum(counts) - counts
    n_used = pend[-1] // MOE_TM
    tiles = jnp.arange(MOE_TILES, dtype=jnp.int32)
    last_used = jnp.minimum(tiles, n_used - 1)
    tile_expert = jnp.sum((last_used[:, None] * MOE_TM >= pend[None, :]).astype(jnp.int32), axis=1)
    rows = jnp.arange(MOE_ROWS, dtype=jnp.int32)
    e_r = tile_expert[rows // MOE_TM]
    off = rows - pstart[e_r]
    valid = (off < counts[e_r]) & (rows // MOE_TM < n_used)
    src = order[jnp.clip(ustart[e_r] + off, 0, MOE_SLOTS - 1)]
    return tile_expert.astype(jnp.int32), n_used.reshape(1).astype(jnp.int32), jnp.where(valid, src, -1)


def _moe_combine_kernel(x1, y, route, mod_ref, out):
    gate2 = mod_ref[0][:, 5 * D_MODEL:6 * D_MODEL]
    r = route[...]
    mix = r[:, 0:1] * y[:, 0:D_MODEL] + r[:, 1:2] * y[:, D_MODEL:2 * D_MODEL]
    out[...] = x1[...] + gate2 * mix


def _moe_combine(x1, y_slots, route, mod_l):
    nt = N_TOK // FF_TM
    per_seg = SEG // FF_TM
    return pl.pallas_call(
        _moe_combine_kernel,
        grid=(nt,),
        in_specs=[pl.BlockSpec((FF_TM, D_MODEL), lambda i: (i, 0)),
                  pl.BlockSpec((FF_TM, 2 * D_MODEL), lambda i: (i, 0)),
                  pl.BlockSpec((FF_TM, LANES), lambda i: (i, 0)),
                  pl.BlockSpec((1, 1, 6 * D_MODEL), lambda i: (i // per_seg, 0, 0))],
        out_specs=pl.BlockSpec((FF_TM, D_MODEL), lambda i: (i, 0)),
        out_shape=jax.ShapeDtypeStruct((N_TOK, D_MODEL), F32),
        compiler_params=_cparams(("arbitrary",)),
        name="moe_combine",
    )(x1, y_slots.reshape(N_TOK, 2 * D_MODEL), route, mod_l)


def _rope_tables():
    t = jnp.arange(DEC_SEQ)
    row, col = (t // GRID_W).astype(F32), (t % GRID_W).astype(F32)
    nf = HEAD_DIM // 4
    freqs = ROPE_BASE ** (-jnp.arange(nf, dtype=F32) / nf)
    lane = np.arange(LANES) % HEAD_DIM
    fidx = lane % nf
    use_col = (lane // (HEAD_DIM // 2)) == 1
    first = (lane % (HEAD_DIM // 2)) < nf
    pos = jnp.where(use_col[None, :], col[:, None], row[:, None])
    ang = pos * freqs[fidx][None, :]
    sin = jnp.sin(ang)
    return jnp.cos(ang), jnp.where(first[None, :], -sin, sin)


def _permute_w_in(w):
    sizes = (256, 256, 256, 256, 8, 8, 256, 128, 128, 256, 256, 256, 256, 256)
    offs = np.concatenate([[0], np.cumsum(sizes)])
    part = lambda i: w[:, offs[i]:offs[i + 1]]
    mq, mk, mv, mo, mi, mf, sq, sk, sv, rx, ry, nq, nk, nv = (part(i) for i in range(14))
    pad = jnp.zeros((w.shape[0], LANES - 16), w.dtype)
    return jnp.concatenate([mq, mk, mv, mo, sq, rx, ry, nq, nk, nv, sk, sv, mi, mf, pad], axis=1)


def _block_diag(w):
    bd = RG_WIDTH // RG_BLOCKS
    out = jnp.zeros((RG_WIDTH, RG_WIDTH), w.dtype)
    for n in range(RG_BLOCKS):
        out = out.at[n * bd:(n + 1) * bd, n * bd:(n + 1) * bd].set(w[n])
    return out


def _tile2(g):
    return jnp.concatenate([g, g]).reshape(1, LANES)


def kernel(x_prompt, x_sample, cache_swa_k, cache_swa_v, cache_na_k, cache_na_v, state_mlstm_C, state_mlstm_n, state_mlstm_m, state_rglru_h, c, c_ctx, norm1_g, norm2_g, w_ada, b_ada, w_in, ml_b_i, ml_b_f, ml_hn, sw_qn, sw_kn, sw_sink, rg_conv_w, rg_conv_b, rg_w_r, rg_b_r, rg_w_i, rg_b_i, rg_lam, na_qn, na_kn, na_rpb, w_br, w_mg, b_mg, w_out, ffn_w1, ffn_w3, ffn_w2, moe_wr, moe_br, moe_w1, moe_w3, moe_w2):
    x_all = jnp.concatenate([x_prompt.reshape(SEG, D_MODEL), x_sample.reshape(2 * SEG, D_MODEL)], axis=0)
    cvecs = jnp.concatenate([c_ctx[None, :], c, jnp.zeros((8 - 1 - DEC_BATCH, D_MODEL), F32)], axis=0)
    mod = _mod_table(cvecs.T, w_ada, b_ada)
    cos_t, sin_t = _rope_tables()
    nj = 2 * ML_HEADS
    zeros_state = (jnp.zeros((BATCH, nj // 2, LANES, LANES), F32), jnp.zeros((BATCH, nj // 2, LANES, LANES), F32),
                   jnp.zeros((BATCH, nj, LANES), F32), jnp.zeros((BATCH, 2, RG_WIDTH), F32))
    ctx_out = []
    for l in range(DEPTH):
        mod_l = mod[l].reshape(8, 1, 6 * D_MODEL)
        qk_gains = jnp.stack([_tile2(sw_qn[l])[0], _tile2(sw_kn[l])[0], _tile2(na_qn[l])[0], _tile2(na_kn[l])[0]])
        proj = _inproj(x_all, mod_l, norm1_g[l].reshape(1, D_MODEL), _permute_w_in(w_in[l]).astype(BF16),
                       qk_gains, cos_t, sin_t)
        gate_bias = jnp.concatenate([ml_b_i[l].reshape(-1), ml_b_f[l].reshape(-1),
                                     jnp.zeros((LANES - 2 * nj,), F32)]).reshape(1, LANES)
        hf_c, hb_c, *st_new = _mlstm(proj, gate_bias, *zeros_state[:3], bsz=BATCH, seq=SEQ, row0=0)
        c_new, n_new, m_new = _mlstm_unpack_state(*st_new)
        st_lat = _mlstm_pack_state(state_mlstm_C[:, l].reshape(DEC_BATCH, nj, HEAD_DIM, HEAD_DIM),
                                   state_mlstm_n[:, l].reshape(DEC_BATCH, nj, HEAD_DIM),
                                   state_mlstm_m[:, l].reshape(DEC_BATCH, nj))
        hf_l, hb_l, _, _, _ = _mlstm(proj, gate_bias, *st_lat, bsz=DEC_BATCH, seq=DEC_SEQ, row0=SEG)
        wg = jnp.concatenate([_block_diag(rg_w_r[l, 0]), _block_diag(rg_w_i[l, 0]),
                              _block_diag(rg_w_r[l, 1]), _block_diag(rg_w_i[l, 1])], axis=1).astype(BF16)
        bg = jnp.concatenate([rg_b_r[l, 0], rg_b_i[l, 0], rg_b_r[l, 1], rg_b_i[l, 1]]).reshape(1, 4 * RG_WIDTH)
        rg_args = (rg_conv_w[l], rg_conv_b[l].reshape(1, RG_WIDTH), wg, bg, rg_lam[l])
        oc_c, hl_new = _rglru(proj, *rg_args, zeros_state[3], bsz=BATCH, seq=SEQ, row0=0)
        oc_l, _ = _rglru(proj, *rg_args, state_rglru_h[:, l], bsz=DEC_BATCH, seq=DEC_SEQ, row0=SEG)
        ob_c, od_c = _ctx_attn(proj, sw_sink[l])
        ob_l = _swa(proj, cache_swa_k[:, l].reshape(DEC_BATCH, PAST_LEN, 128),
                    cache_swa_v[:, l].reshape(DEC_BATCH, PAST_LEN, 128), sw_sink[l])
        od_l = _na(proj, cache_na_k[:, l].reshape(DEC_BATCH, PAST_LEN, 256),
                   cache_na_v[:, l].reshape(DEC_BATCH, PAST_LEN, 256), _na_bias(na_rpb[l].reshape(-1)))
        moe_layer = l % 2 == 1
        j = l // 2
        router = (moe_wr[j].T, moe_br[j]) if moe_layer else None
        outs = _merge(x_all, mod_l, norm1_g[l].reshape(1, D_MODEL), norm2_g[l].reshape(1, D_MODEL),
                      (hf_c, hf_l), (hb_c, hb_l), proj, (ob_c, ob_l), (oc_c, oc_l), (od_c, od_l),
                      _tile2(ml_hn[l]), w_mg[l].astype(BF16), b_mg[l].reshape(1, -1), w_br[l].astype(BF16),
                      w_out[l].astype(BF16), router)
        if moe_layer:
            x1, h2, route = outs
            plan = _moe_plan(route[:, 2:4].astype(jnp.int32))
            y_slots = _moe_group(*plan, h2, moe_w1[j].astype(BF16), moe_w3[j].astype(BF16), moe_w2[j].astype(BF16))
            x_all = _moe_combine(x1, y_slots, route, mod_l)
        else:
            x1, h2 = outs
            x_all = _ffn(h2, x1, mod_l, ffn_w1[j].astype(BF16), ffn_w3[j].astype(BF16), ffn_w2[j].astype(BF16))
        pc = proj[:SEG]
        ctx_out.append(dict(
            sw_k=pc[:, C_SK:C_SK + 128].reshape(BATCH, SEQ, SW_KV_HEADS, HEAD_DIM),
            sw_v=pc[:, C_SV:C_SV + 128].reshape(BATCH, SEQ, SW_KV_HEADS, HEAD_DIM),
            na_k=pc[:, C_NK:C_NK + 256].reshape(BATCH, SEQ, NA_HEADS, HEAD_DIM),
            na_v=pc[:, C_NV:C_NV + 256].reshape(BATCH, SEQ, NA_HEADS, HEAD_DIM),
            ml_C=c_new.reshape(BATCH, 2, ML_HEADS, HEAD_DIM, HEAD_DIM),
            ml_n=n_new.reshape(BATCH, 2, ML_HEADS, HEAD_DIM),
            ml_m=m_new.reshape(BATCH, 2, ML_HEADS),
            rg_h=hl_new))
    stack = lambda name: jnp.stack([t[name] for t in ctx_out], axis=1)
    return (x_all[:SEG].reshape(BATCH, SEQ, D_MODEL), x_all[SEG:].reshape(DEC_BATCH, DEC_SEQ, D_MODEL),
            stack('sw_k'), stack('sw_v'), stack('na_k'), stack('na_v'),
            stack('ml_C'), stack('ml_n'), stack('ml_m'), stack('rg_h'))
```

```python
import functools

import numpy as np
import jax
import jax.numpy as jnp
from jax import lax
from jax.experimental import pallas as pl
from jax.experimental.pallas import tpu as pltpu

F32 = jnp.float32
BF16 = jnp.bfloat16

D_MODEL = 1024
BATCH = 16
SEQ = 256
DEPTH = 2
DEC_BATCH = 2
DEC_SEQ = 4096
PAST_LEN = 256
GRID_W = 64
HEAD_DIM = 64
ML_HEADS = 4
ML_CHUNK = 128
SW_HEADS = 4
SW_KV_HEADS = 2
SW_WINDOW = 128
RG_WIDTH = 256
RG_BLOCKS = 4
RG_CONV = 4
RG_C = 8.0
NA_HEADS = 4
NA_ROWS = 8
NA_COLS = 16
N_BRANCH = 4
ROPE_BASE = 10000.0
D_FF = 2816
N_EXPERTS = 8
D_FF_EXPERT = 2048
EPS = 1e-6
NEG = -1e30
SCALE = HEAD_DIM ** -0.5

SEG = 4096
N_SEG = 3
N_TOK = N_SEG * SEG
LANES = 128
VMEM_LIMIT = 56 * 1024 * 1024

C_MQ, C_MK, C_MV, C_MO = 0, 256, 512, 768
C_SQ, C_RX, C_RY, C_NQ, C_NK, C_NV = 1024, 1280, 1536, 1792, 2048, 2304
C_SK, C_SV, C_G = 2560, 2688, 2816
P_W = 2944


def _cparams(sem):
    return pltpu.CompilerParams(dimension_semantics=sem, vmem_limit_bytes=VMEM_LIMIT)


def _dot(a, b):
    return jnp.dot(a.astype(BF16), b.astype(BF16), preferred_element_type=F32)


def _dot_nt(a, b):
    return lax.dot_general(a.astype(BF16), b.astype(BF16), (((1,), (1,)), ((), ())),
                           preferred_element_type=F32)


def _dot_tn(a, b):
    return lax.dot_general(a.astype(BF16), b.astype(BF16), (((0,), (0,)), ((), ())),
                           preferred_element_type=F32)


def _split3(x):
    hi = x.astype(BF16)
    r1 = x - hi.astype(F32)
    mid = r1.astype(BF16)
    lo = (r1 - mid.astype(F32)).astype(BF16)
    return hi, mid, lo


def _dot_exact_rhs(a01, x):
    hi, mid, lo = _split3(x)
    d = lambda p: jnp.dot(a01, p, preferred_element_type=F32)
    return d(hi) + d(mid) + d(lo)


def _dot_exact_lhs(x, a01):
    hi, mid, lo = _split3(x)
    d = lambda p: jnp.dot(p, a01, preferred_element_type=F32)
    return d(hi) + d(mid) + d(lo)


def _rms(x, g):
    return x * lax.rsqrt(jnp.mean(x * x, axis=-1, keepdims=True) + EPS) * g


def _rms_head_pairs(x, g):
    lane = lax.broadcasted_iota(jnp.int32, x.shape, 1)
    left = lane < HEAD_DIM
    sq = x * x
    s0 = jnp.sum(jnp.where(left, sq, 0.0), axis=-1, keepdims=True)
    s1 = jnp.sum(jnp.where(left, 0.0, sq), axis=-1, keepdims=True)
    ms = jnp.where(left, s0, s1) * (1.0 / HEAD_DIM)
    return x * lax.rsqrt(ms + EPS) * g


MOD_TN = 1536
MOD_ROWS = 3


def _mod_kernel(ct_ref, w_ref, b_ref, o_ref):
    ct = ct_ref[...]
    st = ct * jax.nn.sigmoid(ct)
    w = w_ref[0]
    o_ref[...] = jnp.zeros(o_ref.shape, F32)
    for r in range(MOD_ROWS):
        o_ref[0, r:r + 1, :] = jnp.sum(w * st[:, r:r + 1], axis=0, keepdims=True) + b_ref[0]


def _mod_table(cvecs_t, w_ada, b_ada):
    n = 6 * D_MODEL
    return pl.pallas_call(
        _mod_kernel,
        grid=(DEPTH, n // MOD_TN),
        in_specs=[pl.BlockSpec((D_MODEL, 8), lambda l, j: (0, 0)),
                  pl.BlockSpec((1, D_MODEL, MOD_TN), lambda l, j: (l, 0, j)),
                  pl.BlockSpec((1, 1, MOD_TN), lambda l, j: (l, 0, j))],
        out_specs=pl.BlockSpec((1, 8, MOD_TN), lambda l, j: (l, 0, j)),
        out_shape=jax.ShapeDtypeStruct((DEPTH, 8, n), F32),
        compiler_params=_cparams(("arbitrary", "arbitrary")),
        name="adaln_mod",
    )(cvecs_t, w_ada, b_ada.reshape(DEPTH, 1, n))


IN_TM = 256


def _swap16(y):
    lane = lax.broadcasted_iota(jnp.int32, y.shape, 1)
    first = (lane % 32) < 16
    return jnp.where(first, pltpu.roll(y, LANES - 16, 1), pltpu.roll(y, 16, 1))


def _inproj_kernel(x_ref, mod_ref, g_ref, w_ref, qkg_ref, cos_ref, sin_ref, o_ref):
    seg = pl.program_id(0)
    mod = mod_ref[0]
    sh1 = mod[:, 0:D_MODEL]
    sc1 = mod[:, D_MODEL:2 * D_MODEL]
    h = _rms(x_ref[...], g_ref[...]) * (1.0 + sc1) + sh1
    r = jnp.dot(h.astype(BF16), w_ref[...], preferred_element_type=F32)
    o_ref[:, 0:C_SQ] = r[:, 0:C_SQ]
    o_ref[:, C_RX:C_NQ] = r[:, C_RX:C_NQ]
    o_ref[:, C_NV:C_SK] = r[:, C_NV:C_SK]
    o_ref[:, C_SV:P_W] = r[:, C_SV:P_W]
    cos = cos_ref[...]
    sin = sin_ref[...]
    latent = seg > 0

    def rope(y):
        return jnp.where(latent, y * cos + _swap16(y) * sin, y)

    for p in range(2):
        a = C_SQ + LANES * p
        o_ref[:, a:a + LANES] = rope(_rms_head_pairs(r[:, a:a + LANES], qkg_ref[0:1, :]))
    o_ref[:, C_SK:C_SK + LANES] = rope(_rms_head_pairs(r[:, C_SK:C_SK + LANES], qkg_ref[1:2, :]))
    for p in range(2):
        a = C_NQ + LANES * p
        o_ref[:, a:a + LANES] = _rms_head_pairs(r[:, a:a + LANES], qkg_ref[2:3, :])
        a = C_NK + LANES * p
        o_ref[:, a:a + LANES] = _rms_head_pairs(r[:, a:a + LANES], qkg_ref[3:4, :])


def _inproj(x_all, mod_l, norm1, w_in_p, qk_gains, cos_t, sin_t):
    nt = SEG // IN_TM
    return pl.pallas_call(
        _inproj_kernel,
        grid=(N_SEG, nt),
        in_specs=[pl.BlockSpec((IN_TM, D_MODEL), lambda s, i: (s * nt + i, 0)),
                  pl.BlockSpec((1, 1, 6 * D_MODEL), lambda s, i: (s, 0, 0)),
                  pl.BlockSpec((1, D_MODEL), lambda s, i: (0, 0)),
                  pl.BlockSpec((D_MODEL, P_W), lambda s, i: (0, 0)),
                  pl.BlockSpec((4, LANES), lambda s, i: (0, 0)),
                  pl.BlockSpec((IN_TM, LANES), lambda s, i: (i, 0)),
                  pl.BlockSpec((IN_TM, LANES), lambda s, i: (i, 0))],
        out_specs=pl.BlockSpec((IN_TM, P_W), lambda s, i: (s * nt + i, 0)),
        out_shape=jax.ShapeDtypeStruct((N_TOK, P_W), F32),
        compiler_params=_cparams(("arbitrary", "arbitrary")),
        name="inproj",
    )(x_all, mod_l, norm1, w_in_p, qk_gains, cos_t, sin_t)


def _mlstm_direction(d, q_ref, k_ref, v_ref, g_ref, bias, cbd, nbd, m_s):
    ch = ML_CHUNK
    r_io = lax.broadcasted_iota(jnp.int32, (ch, ch), 0)
    c_io = lax.broadcasted_iota(jnp.int32, (ch, ch), 1)
    lower = r_io >= c_io
    upper = r_io <= c_io
    mask = lower if d == 0 else upper
    tri = mask.astype(BF16)
    tri_t = (upper if d == 0 else lower).astype(BF16)
    left = c_io < HEAD_DIM
    top = r_io < HEAD_DIM
    blockdiag = top == left
    g = g_ref[...] + bias[...]
    b_cols = _dot_exact_rhs(tri, jax.nn.log_sigmoid(g))
    b3 = jnp.concatenate(_split3(b_cols), axis=1)
    gt = g.T
    li_rows = gt[0:8, :]
    b_rows = _dot_exact_lhs(jax.nn.log_sigmoid(gt[8:16, :]), tri_t)
    a_rows = li_rows - b_rows
    bl = b_rows[:, ch - 1:ch] if d == 0 else b_rows[:, 0:1]
    m_old = m_s[:, 0:1]
    g_rows = bl - b_rows + li_rows
    m_new = jnp.maximum(bl + m_old, jnp.max(g_rows, axis=1, keepdims=True))
    wk_rows = jnp.exp(g_rows - m_new)
    wp = jnp.exp(bl + m_old - m_new)
    sel_r = lax.broadcasted_iota(jnp.int32, (3 * LANES, LANES), 0) % LANES
    sel_left = lax.broadcasted_iota(jnp.int32, (3 * LANES, LANES), 1) < HEAD_DIM
    ones_blk = jnp.ones((ch, LANES), BF16)
    left2 = lax.broadcasted_iota(jnp.int32, (ch, 2 * LANES), 1) % LANES < HEAD_DIM
    h_out, c_out, n_out = [], [], []
    for p in range(ML_HEADS // 2):
        lanes = slice(LANES * p, LANES * (p + 1))
        j0 = ML_HEADS * d + 2 * p
        q2 = q_ref[:, lanes]
        k2t = (k_ref[:, lanes] * SCALE).T.astype(BF16)
        v2e = jnp.concatenate([v_ref[:, lanes].astype(BF16), ones_blk], axis=1)
        cb = cbd[2 * d + p]
        nb = nbd[2 * d + p]
        q2b = q2.astype(BF16)
        q_lo = (q2 - q2b.astype(F32)).astype(BF16)
        nb_hi = nb.astype(BF16)
        nb_lo = (nb - nb_hi.astype(F32)).astype(BF16)
        qc = jnp.dot(q2b, cb.astype(BF16), preferred_element_type=F32)
        qn = jnp.dot(jnp.concatenate([q2b, q_lo, q2b], axis=1), jnp.concatenate([nb_hi, nb_hi, nb_lo], axis=0),
                     preferred_element_type=F32)
        sel = (sel_r == jnp.where(sel_left, 8 + j0, 9 + j0)).astype(BF16)
        b_pair = jnp.dot(b3, sel, preferred_element_type=F32)
        cbs, sves = [], []
        for i in range(2):
            j = j0 + i
            half = left if i == 0 else jnp.logical_not(left)
            a_mat = jnp.where(mask, a_rows[j:j + 1, :], NEG)
            cvec = jnp.maximum(m_old[j:j + 1, :], jnp.max(a_mat, axis=1, keepdims=True))
            cbro = jnp.broadcast_to(cvec, (ch, ch))
            s = jnp.dot(jnp.where(half, q2b, 0), k2t, preferred_element_type=F32) * jnp.exp(a_mat - cbro)
            s_hi = s.astype(BF16)
            s_lo = (s - s_hi.astype(F32)).astype(BF16)
            sve = jnp.dot(s_hi, v2e, preferred_element_type=F32)
            rs_lo = jnp.dot(s_lo, ones_blk, preferred_element_type=F32)
            sves.append(jnp.concatenate([sve[:, 0:LANES], sve[:, LANES:2 * LANES] + rs_lo], axis=1))
            cbs.append(cbro)
        c_pair = jnp.where(left, cbs[0], cbs[1])
        w_prev = jnp.exp(jnp.where(left, m_old[j0:j0 + 1, :], m_old[j0 + 1:j0 + 2, :]) - c_pair)
        sve = jnp.where(left2, sves[0], sves[1])
        num = w_prev * qc + sve[:, 0:LANES]
        den = w_prev * qn + sve[:, LANES:2 * LANES]
        h_out.append(num / jnp.maximum(jnp.abs(den), jnp.exp(-(c_pair + b_pair))))
        kwt = k2t * jnp.where(top, wk_rows[j0:j0 + 1, :], wk_rows[j0 + 1:j0 + 2, :])
        kwt_hi = kwt.astype(BF16)
        kwt_lo = (kwt - kwt_hi.astype(F32)).astype(BF16)
        kve = jnp.dot(kwt_hi, v2e, preferred_element_type=F32)
        kn = kve[:, LANES:2 * LANES] + jnp.dot(kwt_lo, ones_blk, preferred_element_type=F32)
        wp_pair = jnp.where(top, wp[j0:j0 + 1, :], wp[j0 + 1:j0 + 2, :])
        c_out.append(wp_pair * cb + jnp.where(blockdiag, kve[:, 0:LANES], 0.0))
        n_out.append(wp_pair * nb + jnp.where(blockdiag, kn, 0.0))
    return h_out, c_out, n_out, m_new


def _mlstm_kernel(qf, kf, vf, gf, qb, kb, vb, gb, c0, n0, m0, bias,
                  hf, hb, co, no, mo, cbd, nbd, m_s, *, nc):
    c = pl.program_id(1)

    @pl.when(c == 0)
    def _():
        cbd[...] = c0[0]
        nbd[...] = n0[0]
        m_s[...] = m0[0]

    res = [_mlstm_direction(d, *refs, bias, cbd, nbd, m_s)
           for d, refs in enumerate(((qf, kf, vf, gf), (qb, kb, vb, gb)))]
    for d, (h_ref, (h_out, c_out, n_out, m_new)) in enumerate(zip((hf, hb), res)):
        for p in range(ML_HEADS // 2):
            h_ref[:, LANES * p:LANES * (p + 1)] = h_out[p]
            cbd[2 * d + p] = c_out[p]
            nbd[2 * d + p] = n_out[p]
        rows = slice(ML_HEADS * d, ML_HEADS * (d + 1))
        m_s[rows, :] = jnp.broadcast_to(m_new[rows, :], (ML_HEADS, LANES))

    @pl.when(c == nc - 1)
    def _():
        co[0] = cbd[...]
        no[0] = nbd[...]
        mo[0] = m_s[...]


def _mlstm_pack_state(c0, n0, m0):
    bsz = c0.shape[0]
    hd = HEAD_DIM
    cbd = jnp.zeros((bsz, ML_HEADS, LANES, LANES), F32)
    cbd = cbd.at[:, :, :hd, :hd].set(c0[:, 0::2]).at[:, :, hd:, hd:].set(c0[:, 1::2])
    nbd = jnp.zeros((bsz, ML_HEADS, LANES, LANES), F32)
    rep = lambda t: jnp.broadcast_to(t[..., None], t.shape + (hd,))
    nbd = nbd.at[:, :, :hd, :hd].set(rep(n0[:, 0::2])).at[:, :, hd:, hd:].set(rep(n0[:, 1::2]))
    return cbd, nbd, jnp.broadcast_to(m0[..., None], m0.shape + (LANES,))


def _mlstm_unpack_state(cbd, nbd, mrow):
    hd = HEAD_DIM
    bsz = cbd.shape[0]
    c = jnp.stack([cbd[:, :, :hd, :hd], cbd[:, :, hd:, hd:]], axis=2).reshape(bsz, 2 * ML_HEADS, hd, hd)
    n = jnp.stack([nbd[:, :, :hd, 0], nbd[:, :, hd:, hd]], axis=2).reshape(bsz, 2 * ML_HEADS, hd)
    return c, n, mrow[:, :, 0]


def _mlstm(proj, gate_bias, c0, n0, m0, *, bsz, seq, row0):
    nc = seq // ML_CHUNK
    base = row0 // ML_CHUNK
    nj = 2 * ML_HEADS
    fw = lambda col: (lambda b, c: (base + b * nc + c, col))
    bw = lambda col: (lambda b, c: (base + b * nc + nc - 1 - c, col))
    qkv = lambda f: [pl.BlockSpec((ML_CHUNK, 256), f(C_MQ // 256)),
                     pl.BlockSpec((ML_CHUNK, 256), f(C_MK // 256)),
                     pl.BlockSpec((ML_CHUNK, 256), f(C_MV // 256)),
                     pl.BlockSpec((ML_CHUNK, LANES), f(C_G // LANES))]
    st_specs = [pl.BlockSpec((1, nj // 2, LANES, LANES), lambda b, c: (b, 0, 0, 0)),
                pl.BlockSpec((1, nj // 2, LANES, LANES), lambda b, c: (b, 0, 0, 0)),
                pl.BlockSpec((1, nj, LANES), lambda b, c: (b, 0, 0))]
    st_shapes = [jax.ShapeDtypeStruct((bsz, nj // 2, LANES, LANES), F32),
                 jax.ShapeDtypeStruct((bsz, nj // 2, LANES, LANES), F32),
                 jax.ShapeDtypeStruct((bsz, nj, LANES), F32)]
    return pl.pallas_call(
        functools.partial(_mlstm_kernel, nc=nc),
        grid=(bsz, nc),
        in_specs=qkv(fw) + qkv(bw) + st_specs + [pl.BlockSpec((1, LANES), lambda b, c: (0, 0))],
        out_specs=[pl.BlockSpec((ML_CHUNK, 256), lambda b, c: (b * nc + c, 0)),
                   pl.BlockSpec((ML_CHUNK, 256), lambda b, c: (b * nc + nc - 1 - c, 0))] + st_specs,
        out_shape=[jax.ShapeDtypeStruct((bsz * seq, 256), F32),
                   jax.ShapeDtypeStruct((bsz * seq, 256), F32)] + st_shapes,
        scratch_shapes=[pltpu.VMEM((nj // 2, LANES, LANES), F32),
                        pltpu.VMEM((nj // 2, LANES, LANES), F32),
                        pltpu.VMEM((nj, LANES), F32)],
        compiler_params=_cparams(("arbitrary", "arbitrary")),
        name="mlstm",
    )(proj, proj, proj, proj, proj, proj, proj, proj, c0, n0, m0, gate_bias)


RG_TC = 256
RG_PAD = 8


def _rglru_kernel(rx, ry, cw, cb, wg, bg, lam, h0, oc, hl, xpad, af, ab, ub, *, seq):
    xpad[0:RG_PAD, :] = jnp.zeros((RG_PAD, RG_WIDTH), F32)
    xpad[seq + RG_PAD:seq + 2 * RG_PAD, :] = jnp.zeros((RG_PAD, RG_WIDTH), F32)
    xpad[RG_PAD:seq + RG_PAD, :] = rx[...]
    sp = jax.nn.softplus(-lam[...])
    left = (RG_CONV - 1) // 2
    for ci in range(seq // RG_TC):
        s0 = ci * RG_TC
        xc = None
        for j in range(RG_CONV):
            a = RG_PAD + s0 + j - left
            term = xpad[a:a + RG_TC, :] * cw[j:j + 1, :]
            xc = term if xc is None else xc + term
        xc = xc + cb[...]
        pre = _dot(xc, wg[...]) + bg[...]
        for d in range(2):
            o = 2 * RG_WIDTH * d
            r = jax.nn.sigmoid(pre[:, o:o + RG_WIDTH])
            gi = jax.nn.sigmoid(pre[:, o + RG_WIDTH:o + 2 * RG_WIDTH])
            log_a = -RG_C * r * sp[d:d + 1, :]
            a_val = jnp.exp(log_a)
            u_val = jnp.sqrt(-jnp.tanh(log_a) * (a_val * a_val + 1.0)) * (gi * xc)
            if d == 0:
                af[s0:s0 + RG_TC, :] = a_val
                oc[s0:s0 + RG_TC, :] = u_val
            else:
                ab[s0:s0 + RG_TC, :] = a_val
                ub[s0:s0 + RG_TC, :] = u_val

    def body(t, carry):
        h_f, h_b = carry
        h_f = af[pl.ds(t, 1), :] * h_f + oc[pl.ds(t, 1), :]
        oc[pl.ds(t, 1), :] = h_f
        tb = seq - 1 - t
        h_b = ab[pl.ds(tb, 1), :] * h_b + ub[pl.ds(tb, 1), :]
        ub[pl.ds(tb, 1), :] = h_b
        return h_f, h_b

    h_f, h_b = lax.fori_loop(0, seq, body, (h0[0, 0:1, :], h0[0, 1:2, :]), unroll=8)
    hl[0, 0:1, :] = h_f
    hl[0, 1:2, :] = h_b
    for ci in range(seq // RG_TC):
        sl = slice(ci * RG_TC, (ci + 1) * RG_TC)
        oc[sl, :] = (oc[sl, :] + ub[sl, :]) * jax.nn.gelu(ry[sl, :])


def _rglru(proj, cw, cb, wg, bg, lam, h0, *, bsz, seq, row0):
    base = row0 // seq
    full = lambda shape: pl.BlockSpec(shape, lambda b: tuple(0 for _ in shape))
    return pl.pallas_call(
        functools.partial(_rglru_kernel, seq=seq),
        grid=(bsz,),
        in_specs=[pl.BlockSpec((seq, RG_WIDTH), lambda b: (base + b, C_RX // RG_WIDTH)),
                  pl.BlockSpec((seq, RG_WIDTH), lambda b: (base + b, C_RY // RG_WIDTH)),
                  full((RG_CONV, RG_WIDTH)), full((1, RG_WIDTH)),
                  full((RG_WIDTH, 4 * RG_WIDTH)), full((1, 4 * RG_WIDTH)), full((2, RG_WIDTH)),
                  pl.BlockSpec((1, 2, RG_WIDTH), lambda b: (b, 0, 0))],
        out_specs=[pl.BlockSpec((seq, RG_WIDTH), lambda b: (b, 0)),
                   pl.BlockSpec((1, 2, RG_WIDTH), lambda b: (b, 0, 0))],
        out_shape=[jax.ShapeDtypeStruct((bsz * seq, RG_WIDTH), F32),
                   jax.ShapeDtypeStruct((bsz, 2, RG_WIDTH), F32)],
        scratch_shapes=[pltpu.VMEM((seq + 2 * RG_PAD, RG_WIDTH), F32),
                        pltpu.VMEM((seq, RG_WIDTH), F32),
                        pltpu.VMEM((seq, RG_WIDTH), F32),
                        pltpu.VMEM((seq, RG_WIDTH), F32)],
        compiler_params=_cparams(("arbitrary",)),
        name="rglru",
    )(proj, proj, cw, cb, wg, bg, lam, h0)


def _softmax_pv(scores, values, sink):
    m = functools.reduce(jnp.maximum, [jnp.max(s, axis=-1, keepdims=True) for s in scores])
    if sink is not None:
        m = jnp.maximum(m, sink)
    ps = [jnp.exp(s - m) for s in scores]
    den = functools.reduce(jnp.add, [jnp.sum(p, axis=-1, keepdims=True) for p in ps])
    if sink is not None:
        den = den + jnp.exp(sink - m)
    num = functools.reduce(jnp.add, [_dot(p, v) for p, v in zip(ps, values)])
    return num / den


def _ctx_attn_kernel(sink, sq, sk, sv, nq, nk, nv, ob, od):
    for h in range(SW_HEADS):
        j = h // (SW_HEADS // SW_KV_HEADS)
        sl = slice(HEAD_DIM * h, HEAD_DIM * (h + 1))
        kv = slice(HEAD_DIM * j, HEAD_DIM * (j + 1))
        s = _dot_nt(sq[:, sl], sk[:, kv]) * SCALE
        ob[:, sl] = _softmax_pv([s], [sv[:, kv]], sink[h])
    for h in range(NA_HEADS):
        sl = slice(HEAD_DIM * h, HEAD_DIM * (h + 1))
        s = _dot_nt(nq[:, sl], nk[:, sl]) * SCALE
        od[:, sl] = _softmax_pv([s], [nv[:, sl]], None)


def _ctx_attn(proj, sink):
    blk = lambda w, col: pl.BlockSpec((SEQ, w), lambda b: (b, col))
    return pl.pallas_call(
        _ctx_attn_kernel,
        grid=(BATCH,),
        in_specs=[pl.BlockSpec(memory_space=pltpu.SMEM),
                  blk(256, C_SQ // 256), blk(128, C_SK // 128), blk(128, C_SV // 128),
                  blk(256, C_NQ // 256), blk(256, C_NK // 256), blk(256, C_NV // 256)],
        out_specs=[pl.BlockSpec((SEQ, 256), lambda b: (b, 0)),
                   pl.BlockSpec((SEQ, 256), lambda b: (b, 0))],
        out_shape=[jax.ShapeDtypeStruct((SEG, 256), F32), jax.ShapeDtypeStruct((SEG, 256), F32)],
        compiler_params=_cparams(("arbitrary",)),
        name="ctx_attn",
    )(sink, proj, proj, proj, proj, proj, proj)


SW_QB = 128
SW_SPAN = SW_QB + 2 * SW_WINDOW


def _swa_kernel(sink, q, k, v, kc, vc, ob):
    n = pl.program_id(1)
    ws = jnp.clip((n - 1) * SW_QB, 0, DEC_SEQ - SW_SPAN)
    ws = pl.multiple_of(ws, SW_QB)
    qpos = n * SW_QB + lax.broadcasted_iota(jnp.int32, (SW_QB, SW_SPAN), 0)
    kpos = ws + lax.broadcasted_iota(jnp.int32, (SW_QB, SW_SPAN), 1)
    valid = jnp.abs(qpos - kpos) <= SW_WINDOW
    for h in range(SW_HEADS):
        j = h // (SW_HEADS // SW_KV_HEADS)
        sl = slice(HEAD_DIM * h, HEAD_DIM * (h + 1))
        kv = slice(HEAD_DIM * j, HEAD_DIM * (j + 1))
        qh = q[:, sl]
        s_loc = jnp.where(valid, _dot_nt(qh, k[pl.ds(ws, SW_SPAN), kv]) * SCALE, NEG)
        s_ctx = _dot_nt(qh, kc[0, :, kv]) * SCALE
        ob[:, sl] = _softmax_pv([s_loc, s_ctx], [v[pl.ds(ws, SW_SPAN), kv], vc[0, :, kv]], sink[h])


def _swa(proj, kc, vc, sink):
    nq = DEC_SEQ // SW_QB
    qbase = SEG // SW_QB
    return pl.pallas_call(
        _swa_kernel,
        grid=(DEC_BATCH, nq),
        in_specs=[pl.BlockSpec(memory_space=pltpu.SMEM),
                  pl.BlockSpec((SW_QB, 256), lambda b, n: (qbase + b * nq + n, C_SQ // 256)),
                  pl.BlockSpec((DEC_SEQ, 128), lambda b, n: (1 + b, C_SK // 128)),
                  pl.BlockSpec((DEC_SEQ, 128), lambda b, n: (1 + b, C_SV // 128)),
                  pl.BlockSpec((1, PAST_LEN, 128), lambda b, n: (b, 0, 0)),
                  pl.BlockSpec((1, PAST_LEN, 128), lambda b, n: (b, 0, 0))],
        out_specs=pl.BlockSpec((SW_QB, 256), lambda b, n: (b * nq + n, 0)),
        out_shape=jax.ShapeDtypeStruct((DEC_BATCH * DEC_SEQ, 256), F32),
        compiler_params=_cparams(("arbitrary", "arbitrary")),
        name="swa",
    )(sink, proj, proj, proj, kc, vc)


NA_RPB_R = 2 * NA_ROWS - 1
NA_RPB_C = 2 * NA_COLS - 1
NA_NKEY = NA_ROWS * GRID_W
GRID_ROWS = DEC_SEQ // GRID_W


def _na_bias_kernel(rpb, out):
    h = pl.program_id(0)
    qc = lax.broadcasted_iota(jnp.int32, (GRID_W, GRID_W), 0)
    kc = lax.broadcasted_iota(jnp.int32, (GRID_W, GRID_W), 1)
    dc = jnp.clip(kc - qc, -(NA_COLS - 1), NA_COLS - 1) + NA_COLS - 1
    lo = jnp.clip(qc - NA_COLS // 2, 0, GRID_W - NA_COLS)
    valid = (kc >= lo) & (kc < lo + NA_COLS)
    tiles = []
    for dr in range(NA_RPB_R):
        t = jnp.zeros((GRID_W, GRID_W), F32)
        for j in range(NA_RPB_C):
            t = jnp.where(dc == j, rpb[(h * NA_RPB_R + dr) * NA_RPB_C + j], t)
        tiles.append(jnp.where(valid, t, NEG))
    for w in range(NA_ROWS):
        for i in range(NA_ROWS):
            out[0, w, :, GRID_W * i:GRID_W * (i + 1)] = tiles[w + i]


def _na_bias(rpb_flat):
    return pl.pallas_call(
        _na_bias_kernel,
        grid=(NA_HEADS,),
        in_specs=[pl.BlockSpec(memory_space=pltpu.SMEM)],
        out_specs=pl.BlockSpec((1, NA_ROWS, GRID_W, NA_NKEY), lambda h: (h, 0, 0, 0)),
        out_shape=jax.ShapeDtypeStruct((NA_HEADS, NA_ROWS, GRID_W, NA_NKEY), F32),
        compiler_params=_cparams(("arbitrary",)),
        name="na_bias",
    )(rpb_flat)


def _na_kernel(q, k, v, kc, vc, bias, od):
    r = pl.program_id(1)
    r0 = jnp.clip(r - NA_ROWS // 2, 0, GRID_ROWS - NA_ROWS)
    k0 = pl.multiple_of(r0 * GRID_W, GRID_W)
    win = r0 - r + NA_ROWS - 1
    for h in range(NA_HEADS):
        sl = slice(HEAD_DIM * h, HEAD_DIM * (h + 1))
        qh = q[:, sl]
        s_loc = _dot_nt(qh, k[pl.ds(k0, NA_NKEY), sl]) * SCALE + bias[h, pl.ds(win, 1)][0]
        s_ctx = _dot_nt(qh, kc[0, :, sl]) * SCALE
        od[:, sl] = _softmax_pv([s_loc, s_ctx], [v[pl.ds(k0, NA_NKEY), sl], vc[0, :, sl]], None)


def _na(proj, kc, vc, bias):
    qbase = SEG // GRID_W
    return pl.pallas_call(
        _na_kernel,
        grid=(DEC_BATCH, GRID_ROWS),
        in_specs=[pl.BlockSpec((GRID_W, 256), lambda b, r: (qbase + b * GRID_ROWS + r, C_NQ // 256)),
                  pl.BlockSpec((DEC_SEQ, 256), lambda b, r: (1 + b, C_NK // 256)),
                  pl.BlockSpec((DEC_SEQ, 256), lambda b, r: (1 + b, C_NV // 256)),
                  pl.BlockSpec((1, PAST_LEN, 256), lambda b, r: (b, 0, 0)),
                  pl.BlockSpec((1, PAST_LEN, 256), lambda b, r: (b, 0, 0)),
                  pl.BlockSpec((NA_HEADS, NA_ROWS, GRID_W, NA_NKEY), lambda b, r: (0, 0, 0, 0))],
        out_specs=pl.BlockSpec((GRID_W, 256), lambda b, r: (b * GRID_ROWS + r, 0)),
        out_shape=jax.ShapeDtypeStruct((DEC_BATCH * DEC_SEQ, 256), F32),
        compiler_params=_cparams(("arbitrary", "arbitrary")),
        name="na",
    )(proj, proj, proj, kc, vc, bias)


MG_TM = 256


def _merge_kernel(x_ref, mod_ref, g1_ref, g2_ref, hf_c, hf_l, hb_c, hb_l, mo, ob_c, ob_l, oc_c, oc_l, od_c, od_l,
                  hn, wmg, bmg, wbr, wout, *rest, moe):
    if moe:
        wrt, br, x1_ref, h2_ref, route_ref = rest
    else:
        x1_ref, h2_ref = rest
    ctx = pl.program_id(0) == 0
    pick = lambda c_ref, l_ref: jnp.where(ctx, c_ref[...], l_ref[...])
    mod = mod_ref[0]
    chunk = lambda i: mod[:, i * D_MODEL:(i + 1) * D_MODEL]
    sh1, sc1, gate1, sh2, sc2 = chunk(0), chunk(1), chunk(2), chunk(3), chunk(4)
    x = x_ref[...]
    h = (_rms(x, g1_ref[...]) * (1.0 + sc1) + sh1).astype(BF16)
    hsum = pick(hf_c, hf_l) + pick(hb_c, hb_l)
    out_a = jnp.concatenate(
        [_rms_head_pairs(hsum[:, LANES * p:LANES * (p + 1)], hn[...]) for p in range(2)], axis=-1)
    out_a = out_a * jax.nn.sigmoid(mo[...])
    acc = None
    for n, br_val in enumerate((out_a, pick(ob_c, ob_l), pick(oc_c, oc_l), pick(od_c, od_l))):
        gate = jax.nn.sigmoid(jnp.dot(h, wmg[:, n * D_MODEL:(n + 1) * D_MODEL], preferred_element_type=F32)
                              + bmg[:, n * D_MODEL:(n + 1) * D_MODEL])
        term = gate * jnp.dot(br_val.astype(BF16), wbr[n], preferred_element_type=F32)
        acc = term if acc is None else acc + term
    y = jnp.dot(acc.astype(BF16), wout[...], preferred_element_type=F32)
    x1 = x + gate1 * y
    x1_ref[...] = x1
    h2 = _rms(x1, g2_ref[...]) * (1.0 + sc2) + sh2
    h2_ref[...] = h2.astype(h2_ref.dtype)
    if moe:
        logit = [jnp.sum(h2 * wrt[e:e + 1, :], axis=-1, keepdims=True) + br[e] for e in range(N_EXPERTS)]
        v1, i1 = logit[0], jnp.zeros(logit[0].shape, jnp.int32)
        for e in range(1, N_EXPERTS):
            better = logit[e] > v1
            v1 = jnp.where(better, logit[e], v1)
            i1 = jnp.where(better, e, i1)
        v2, i2 = jnp.full(v1.shape, -jnp.inf, F32), jnp.zeros(v1.shape, jnp.int32)
        for e in range(N_EXPERTS):
            better = (i1 != e) & (logit[e] > v2)
            v2 = jnp.where(better, logit[e], v2)
            i2 = jnp.where(better, e, i2)
        e2 = jnp.exp(v2 - v1)
        den = 1.0 + e2
        lane = lax.broadcasted_iota(jnp.int32, route_ref.shape, 1)
        route = jnp.where(lane == 0, 1.0 / den, 0.0) + jnp.where(lane == 1, e2 / den, 0.0)
        route = route + jnp.where(lane == 2, i1.astype(F32), 0.0) + jnp.where(lane == 3, i2.astype(F32), 0.0)
        route_ref[...] = route


def _merge(x_all, mod_l, g1, g2, hf, hb, proj, ob, oc, od, hn, wmg, bmg, wbr, wout, router=None):
    nt = SEG // MG_TM
    moe = router is not None
    row = lambda w: pl.BlockSpec((MG_TM, w), lambda s, i: (s * nt + i, 0))
    ctx_blk = pl.BlockSpec((MG_TM, 256), lambda s, i: (jnp.minimum(s * nt + i, nt - 1), 0))
    lat_blk = pl.BlockSpec((MG_TM, 256), lambda s, i: (jnp.maximum(s * nt + i - nt, 0), 0))
    full = lambda shape: pl.BlockSpec(shape, lambda s, i: tuple(0 for _ in shape))
    in_specs = [row(D_MODEL),
                pl.BlockSpec((1, 1, 6 * D_MODEL), lambda s, i: (s, 0, 0)),
                full((1, D_MODEL)), full((1, D_MODEL)),
                ctx_blk, lat_blk, ctx_blk, lat_blk,
                pl.BlockSpec((MG_TM, 256), lambda s, i: (s * nt + i, C_MO // 256)),
                ctx_blk, lat_blk, ctx_blk, lat_blk, ctx_blk, lat_blk,
                full((1, LANES)), full((D_MODEL, N_BRANCH * D_MODEL)), full((1, N_BRANCH * D_MODEL)),
                full((N_BRANCH, 256, D_MODEL)), full((D_MODEL, D_MODEL))]
    args = [x_all, mod_l, g1, g2, *hf, *hb, proj, *ob, *oc, *od, hn, wmg, bmg, wbr, wout]
    out_specs = [row(D_MODEL), row(D_MODEL)]
    out_shape = [jax.ShapeDtypeStruct((N_TOK, D_MODEL), F32),
                 jax.ShapeDtypeStruct((N_TOK, D_MODEL), F32 if moe else BF16)]
    if moe:
        in_specs += [full((N_EXPERTS, D_MODEL)), pl.BlockSpec(memory_space=pltpu.SMEM)]
        args += list(router)
        out_specs.append(row(LANES))
        out_shape.append(jax.ShapeDtypeStruct((N_TOK, LANES), F32))
    return pl.pallas_call(
        functools.partial(_merge_kernel, moe=moe),
        grid=(N_SEG, nt),
        in_specs=in_specs,
        out_specs=out_specs,
        out_shape=out_shape,
        compiler_params=_cparams(("arbitrary", "arbitrary")),
        name="merge",
    )(*args)


FF_TM = 512
FF_TF = 1408


def _ffn_kernel(h2, w1, w3, w2, x1, mod_ref, out, acc):
    f = pl.program_id(1)

    @pl.when(f == 0)
    def _():
        acc[...] = jnp.zeros(acc.shape, F32)

    h = h2[...]
    a = jnp.dot(h, w1[...], preferred_element_type=F32)
    b = jnp.dot(h, w3[...], preferred_element_type=F32)
    act = (jax.nn.silu(a) * b).astype(BF16)
    acc[...] += jnp.dot(act, w2[...], preferred_element_type=F32)

    @pl.when(f == pl.num_programs(1) - 1)
    def _():
        gate2 = mod_ref[0][:, 5 * D_MODEL:6 * D_MODEL]
        out[...] = x1[...] + gate2 * acc[...]


def _ffn(h2, x1, mod_l, w1, w3, w2):
    nt = N_TOK // FF_TM
    per_seg = SEG // FF_TM
    return pl.pallas_call(
        _ffn_kernel,
        grid=(nt, D_FF // FF_TF),
        in_specs=[pl.BlockSpec((FF_TM, D_MODEL), lambda i, f: (i, 0)),
                  pl.BlockSpec((D_MODEL, FF_TF), lambda i, f: (0, f)),
                  pl.BlockSpec((D_MODEL, FF_TF), lambda i, f: (0, f)),
                  pl.BlockSpec((FF_TF, D_MODEL), lambda i, f: (f, 0)),
                  pl.BlockSpec((FF_TM, D_MODEL), lambda i, f: (i, 0)),
                  pl.BlockSpec((1, 1, 6 * D_MODEL), lambda i, f: (i // per_seg, 0, 0))],
        out_specs=pl.BlockSpec((FF_TM, D_MODEL), lambda i, f: (i, 0)),
        out_shape=jax.ShapeDtypeStruct((N_TOK, D_MODEL), F32),
        scratch_shapes=[pltpu.VMEM((FF_TM, D_MODEL), F32)],
        compiler_params=_cparams(("arbitrary", "arbitrary")),
        name="ffn",
    )(h2, w1, w3, w2, x1, mod_l)


MOE_TM = 256
MOE_SLOTS = 2 * N_TOK
MOE_TILES = MOE_SLOTS // MOE_TM + N_EXPERTS
MOE_STEPS = MOE_TILES + 2
MOE_DUMP = 2 * MOE_TM
MOE_UNROLL = 8


def _moe_group_kernel(texp, nused, src_tok, dst_row, h2_hbm, w1, w3, w2, y_hbm, xs, ys, sem_in, sem_out):
    del texp
    i = pl.program_id(0)
    n_used = nused[0]
    buf = i % 2

    def gather_copy(tile, b, r):
        tok = src_tok[(tile + 1) * MOE_TM + r]
        return pltpu.make_async_copy(h2_hbm.at[pl.ds(tok, 1)], xs.at[b, pl.ds(r, 1)], sem_in.at[b])

    def scatter_copy(tile, b, r):
        dst = dst_row[(tile + 1) * MOE_TM + r]
        return pltpu.make_async_copy(ys.at[b, pl.ds(r, 1)], y_hbm.at[pl.ds(dst, 1)], sem_out.at[b])

    def start_rows_loop(make_copy, tile, b):
        def body(r, carry):
            make_copy(tile, b, r).start()
            return carry
        lax.fori_loop(0, MOE_TM, body, 0, unroll=MOE_UNROLL)

    def wait_tile(b, gather):
        if gather:
            pltpu.make_async_copy(h2_hbm.at[pl.ds(0, MOE_TM)], xs.at[b], sem_in.at[b]).wait()
        else:
            pltpu.make_async_copy(ys.at[b], y_hbm.at[pl.ds(0, MOE_TM)], sem_out.at[b]).wait()

    @pl.when(i == 0)
    def _():
        xs[...] = jnp.zeros(xs.shape, F32)
        ys[...] = jnp.zeros(ys.shape, F32)
        for b in range(2):
            fill = pltpu.make_async_copy(ys.at[b], y_hbm.at[pl.ds(MOE_SLOTS + b * MOE_TM, MOE_TM)], sem_out.at[b])
            fill.start()
            fill.wait()
        start_rows_loop(gather_copy, 0, 0)

    @pl.when(i <= n_used)
    def _():
        wait_tile(buf, True)

    @pl.when((i >= 1) & (i <= n_used + 1))
    def _():
        wait_tile(buf, False)

    @pl.when(i < n_used)
    def _():
        x = xs[buf].astype(BF16)
        for r in range(MOE_TM):
            gather_copy(i + 1, 1 - buf, r).start()
            scatter_copy(i - 1, 1 - buf, r).start()
        a = jnp.dot(x, w1[0], preferred_element_type=F32)
        b = jnp.dot(x, w3[0], preferred_element_type=F32)
        act = (jax.nn.silu(a) * b).astype(BF16)
        ys[buf] = jnp.dot(act, w2[0], preferred_element_type=F32)

    @pl.when(i == n_used)
    def _():
        start_rows_loop(scatter_copy, i - 1, 1 - buf)


def _moe_group(tile_expert, n_used, src_tok, dst_row, h2, w1, w3, w2):
    wspec = lambda shape: pl.BlockSpec((1,) + shape, lambda i, texp, *_: (texp[jnp.minimum(i, MOE_TILES - 1)], 0, 0))
    grid_spec = pltpu.PrefetchScalarGridSpec(
        num_scalar_prefetch=4,
        grid=(MOE_STEPS,),
        in_specs=[pl.BlockSpec(memory_space=pl.ANY),
                  wspec((D_MODEL, D_FF_EXPERT)), wspec((D_MODEL, D_FF_EXPERT)), wspec((D_FF_EXPERT, D_MODEL))],
        out_specs=pl.BlockSpec(memory_space=pl.ANY),
        scratch_shapes=[pltpu.VMEM((2, MOE_TM, D_MODEL), F32), pltpu.VMEM((2, MOE_TM, D_MODEL), F32),
                        pltpu.SemaphoreType.DMA((2,)), pltpu.SemaphoreType.DMA((2,))])
    return pl.pallas_call(
        _moe_group_kernel,
        grid_spec=grid_spec,
        out_shape=jax.ShapeDtypeStruct((MOE_SLOTS + MOE_DUMP, D_MODEL), F32),
        compiler_params=_cparams(("arbitrary",)),
        name="moe_group",
    )(tile_expert, n_used, src_tok, dst_row, h2, w1, w3, w2)


def _moe_plan(expert_ids):
    e_flat = expert_ids.reshape(-1)
    order = jnp.argsort(e_flat, stable=True).astype(jnp.int32)
    counts = jnp.sum((e_flat[:, None] == jnp.arange(N_EXPERTS)[None, :]).astype(jnp.int32), axis=0)
    padded = (counts + MOE_TM - 1) // MOE_TM * MOE_TM
    pend = jnp.cumsum(padded)
    pstart = pend - padded
    ustart = jnp.cumsum(counts) - counts
    n_used = pend[-1] // MOE_TM
    tiles = jnp.arange(MOE_TILES, dtype=jnp.int32)
    last_used = jnp.minimum(tiles, n_used - 1)
    tile_expert = jnp.sum((last_used[:, None] * MOE_TM >= pend[None, :]).astype(jnp.int32), axis=1)
    t = jnp.arange(-1, MOE_TILES + 1, dtype=jnp.int32)[:, None]
    r = jnp.arange(MOE_TM, dtype=jnp.int32)[None, :]
    e_t = tile_expert[jnp.clip(t, 0, MOE_TILES - 1)]
    off = t * MOE_TM + r - pstart[e_t]
    valid = (t >= 0) & (t < n_used) & (off < counts[e_t])
    slot = order[jnp.clip(ustart[e_t] + off, 0, MOE_SLOTS - 1)]
    src_tok = jnp.where(valid, slot // 2, 0)
    dst_row = jnp.where(valid, slot, MOE_SLOTS + (t % 2) * MOE_TM + r)
    return (tile_expert.astype(jnp.int32), n_used.reshape(1).astype(jnp.int32),
            src_tok.reshape(-1).astype(jnp.int32), dst_row.reshape(-1).astype(jnp.int32))


def _moe_combine_kernel(x1, y, route, mod_ref, out):
    gate2 = mod_ref[0][:, 5 * D_MODEL:6 * D_MODEL]
    r = route[...]
    mix = r[:, 0:1] * y[:, 0:D_MODEL] + r[:, 1:2] * y[:, D_MODEL:2 * D_MODEL]
    out[...] = x1[...] + gate2 * mix


def _moe_combine(x1, y_slots, route, mod_l):
    nt = N_TOK // FF_TM
    per_seg = SEG // FF_TM
    return pl.pallas_call(
        _moe_combine_kernel,
        grid=(nt,),
        in_specs=[pl.BlockSpec((FF_TM, D_MODEL), lambda i: (i, 0)),
                  pl.BlockSpec((FF_TM, 2 * D_MODEL), lambda i: (i, 0)),
                  pl.BlockSpec((FF_TM, LANES), lambda i: (i, 0)),
                  pl.BlockSpec((1, 1, 6 * D_MODEL), lambda i: (i // per_seg, 0, 0))],
        out_specs=pl.BlockSpec((FF_TM, D_MODEL), lambda i: (i, 0)),
        out_shape=jax.ShapeDtypeStruct((N_TOK, D_MODEL), F32),
        compiler_params=_cparams(("arbitrary",)),
        name="moe_combine",
    )(x1, y_slots.reshape(N_TOK + MOE_DUMP // 2, 2 * D_MODEL), route, mod_l)


def _rope_tables():
    t = jnp.arange(DEC_SEQ)
    row, col = (t // GRID_W).astype(F32), (t % GRID_W).astype(F32)
    nf = HEAD_DIM // 4
    freqs = ROPE_BASE ** (-jnp.arange(nf, dtype=F32) / nf)
    lane = np.arange(LANES) % HEAD_DIM
    fidx = lane % nf
    use_col = (lane // (HEAD_DIM // 2)) == 1
    first = (lane % (HEAD_DIM // 2)) < nf
    pos = jnp.where(use_col[None, :], col[:, None], row[:, None])
    ang = pos * freqs[fidx][None, :]
    sin = jnp.sin(ang)
    return jnp.cos(ang), jnp.where(first[None, :], -sin, sin)


def _permute_w_in(w):
    sizes = (256, 256, 256, 256, 8, 8, 256, 128, 128, 256, 256, 256, 256, 256)
    offs = np.concatenate([[0], np.cumsum(sizes)])
    part = lambda i: w[:, offs[i]:offs[i + 1]]
    mq, mk, mv, mo, mi, mf, sq, sk, sv, rx, ry, nq, nk, nv = (part(i) for i in range(14))
    pad = jnp.zeros((w.shape[0], LANES - 16), w.dtype)
    return jnp.concatenate([mq, mk, mv, mo, sq, rx, ry, nq, nk, nv, sk, sv, mi, mf, pad], axis=1)


def _block_diag(w):
    bd = RG_WIDTH // RG_BLOCKS
    out = jnp.zeros((RG_WIDTH, RG_WIDTH), w.dtype)
    for n in range(RG_BLOCKS):
        out = out.at[n * bd:(n + 1) * bd, n * bd:(n + 1) * bd].set(w[n])
    return out


def _tile2(g):
    return jnp.concatenate([g, g]).reshape(1, LANES)


def kernel(x_prompt, x_sample, cache_swa_k, cache_swa_v, cache_na_k, cache_na_v, state_mlstm_C, state_mlstm_n, state_mlstm_m, state_rglru_h, c, c_ctx, norm1_g, norm2_g, w_ada, b_ada, w_in, ml_b_i, ml_b_f, ml_hn, sw_qn, sw_kn, sw_sink, rg_conv_w, rg_conv_b, rg_w_r, rg_b_r, rg_w_i, rg_b_i, rg_lam, na_qn, na_kn, na_rpb, w_br, w_mg, b_mg, w_out, ffn_w1, ffn_w3, ffn_w2, moe_wr, moe_br, moe_w1, moe_w3, moe_w2):
    x_all = jnp.concatenate([x_prompt.reshape(SEG, D_MODEL), x_sample.reshape(2 * SEG, D_MODEL)], axis=0)
    cvecs = jnp.concatenate([c_ctx[None, :], c, jnp.zeros((8 - 1 - DEC_BATCH, D_MODEL), F32)], axis=0)
    mod = _mod_table(cvecs.T, w_ada, b_ada)
    cos_t, sin_t = _rope_tables()
    nj = 2 * ML_HEADS
    zeros_state = (jnp.zeros((BATCH, nj // 2, LANES, LANES), F32), jnp.zeros((BATCH, nj // 2, LANES, LANES), F32),
                   jnp.zeros((BATCH, nj, LANES), F32), jnp.zeros((BATCH, 2, RG_WIDTH), F32))
    ctx_out = []
    for l in range(DEPTH):
        mod_l = mod[l].reshape(8, 1, 6 * D_MODEL)
        qk_gains = jnp.stack([_tile2(sw_qn[l])[0], _tile2(sw_kn[l])[0], _tile2(na_qn[l])[0], _tile2(na_kn[l])[0]])
        proj = _inproj(x_all, mod_l, norm1_g[l].reshape(1, D_MODEL), _permute_w_in(w_in[l]).astype(BF16),
                       qk_gains, cos_t, sin_t)
        gate_bias = jnp.concatenate([ml_b_i[l].reshape(-1), ml_b_f[l].reshape(-1),
                                     jnp.zeros((LANES - 2 * nj,), F32)]).reshape(1, LANES)
        hf_c, hb_c, *st_new = _mlstm(proj, gate_bias, *zeros_state[:3], bsz=BATCH, seq=SEQ, row0=0)
        c_new, n_new, m_new = _mlstm_unpack_state(*st_new)
        st_lat = _mlstm_pack_state(state_mlstm_C[:, l].reshape(DEC_BATCH, nj, HEAD_DIM, HEAD_DIM),
                                   state_mlstm_n[:, l].reshape(DEC_BATCH, nj, HEAD_DIM),
                                   state_mlstm_m[:, l].reshape(DEC_BATCH, nj))
        hf_l, hb_l, _, _, _ = _mlstm(proj, gate_bias, *st_lat, bsz=DEC_BATCH, seq=DEC_SEQ, row0=SEG)
        wg = jnp.concatenate([_block_diag(rg_w_r[l, 0]), _block_diag(rg_w_i[l, 0]),
                              _block_diag(rg_w_r[l, 1]), _block_diag(rg_w_i[l, 1])], axis=1).astype(BF16)
        bg = jnp.concatenate([rg_b_r[l, 0], rg_b_i[l, 0], rg_b_r[l, 1], rg_b_i[l, 1]]).reshape(1, 4 * RG_WIDTH)
        rg_args = (rg_conv_w[l], rg_conv_b[l].reshape(1, RG_WIDTH), wg, bg, rg_lam[l])
        oc_c, hl_new = _rglru(proj, *rg_args, zeros_state[3], bsz=BATCH, seq=SEQ, row0=0)
        oc_l, _ = _rglru(proj, *rg_args, state_rglru_h[:, l], bsz=DEC_BATCH, seq=DEC_SEQ, row0=SEG)
        ob_c, od_c = _ctx_attn(proj, sw_sink[l])
        ob_l = _swa(proj, cache_swa_k[:, l].reshape(DEC_BATCH, PAST_LEN, 128),
                    cache_swa_v[:, l].reshape(DEC_BATCH, PAST_LEN, 128), sw_sink[l])
        od_l = _na(proj, cache_na_k[:, l].reshape(DEC_BATCH, PAST_LEN, 256),
                   cache_na_v[:, l].reshape(DEC_BATCH, PAST_LEN, 256), _na_bias(na_rpb[l].reshape(-1)))
        moe_layer = l % 2 == 1
        j = l // 2
        router = (moe_wr[j].T, moe_br[j]) if moe_layer else None
        outs = _merge(x_all, mod_l, norm1_g[l].reshape(1, D_MODEL), norm2_g[l].reshape(1, D_MODEL),
                      (hf_c, hf_l), (hb_c, hb_l), proj, (ob_c, ob_l), (oc_c, oc_l), (od_c, od_l),
                      _tile2(ml_hn[l]), w_mg[l].astype(BF16), b_mg[l].reshape(1, -1), w_br[l].astype(BF16),
                      w_out[l].astype(BF16), router)
        if moe_layer:
            x1, h2, route = outs
            plan = _moe_plan(route[:, 2:4].astype(jnp.int32))
            y_slots = _moe_group(*plan, h2, moe_w1[j].astype(BF16), moe_w3[j].astype(BF16), moe_w2[j].astype(BF16))
            x_all = _moe_combine(x1, y_slots, route, mod_l)
        else:
            x1, h2 = outs
            x_all = _ffn(h2, x1, mod_l, ffn_w1[j].astype(BF16), ffn_w3[j].astype(BF16), ffn_w2[j].astype(BF16))
        pc = proj[:SEG]
        ctx_out.append(dict(
            sw_k=pc[:, C_SK:C_SK + 128].reshape(BATCH, SEQ, SW_KV_HEADS, HEAD_DIM),
            sw_v=pc[:, C_SV:C_SV + 128].reshape(BATCH, SEQ, SW_KV_HEADS, HEAD_DIM),
            na_k=pc[:, C_NK:C_NK + 256].reshape(BATCH, SEQ, NA_HEADS, HEAD_DIM),
            na_v=pc[:, C_NV:C_NV + 256].reshape(BATCH, SEQ, NA_HEADS, HEAD_DIM),
            ml_C=c_new.reshape(BATCH, 2, ML_HEADS, HEAD_DIM, HEAD_DIM),
            ml_n=n_new.reshape(BATCH, 2, ML_HEADS, HEAD_DIM),
            ml_m=m_new.reshape(BATCH, 2, ML_HEADS),
            rg_h=hl_new))
    stack = lambda name: jnp.stack([t[name] for t in ctx_out], axis=1)
    return (x_all[:SEG].reshape(BATCH, SEQ, D_MODEL), x_all[SEG:].reshape(DEC_BATCH, DEC_SEQ, D_MODEL),
            stack('sw_k'), stack('sw_v'), stack('na_k'), stack('na_v'),
            stack('ml_C'), stack('ml_n'), stack('ml_m'), stack('rg_h'))
```

```python
import functools

import numpy as np
import jax
import jax.numpy as jnp
from jax import lax
from jax.experimental import pallas as pl
from jax.experimental.pallas import tpu as pltpu

F32 = jnp.float32
BF16 = jnp.bfloat16

D_MODEL = 1024
BATCH = 16
SEQ = 256
DEPTH = 2
DEC_BATCH = 2
DEC_SEQ = 4096
PAST_LEN = 256
GRID_W = 64
HEAD_DIM = 64
ML_HEADS = 4
ML_CHUNK = 128
SW_HEADS = 4
SW_KV_HEADS = 2
SW_WINDOW = 128
RG_WIDTH = 256
RG_BLOCKS = 4
RG_CONV = 4
RG_C = 8.0
NA_HEADS = 4
NA_ROWS = 8
NA_COLS = 16
N_BRANCH = 4
ROPE_BASE = 10000.0
D_FF = 2816
N_EXPERTS = 8
D_FF_EXPERT = 2048
EPS = 1e-6
NEG = -1e30
SCALE = HEAD_DIM ** -0.5

SEG = 4096
N_SEG = 3
N_TOK = N_SEG * SEG
LANES = 128
VMEM_LIMIT = 56 * 1024 * 1024

C_MQ, C_MK, C_MV, C_MO = 0, 256, 512, 768
C_SQ, C_RX, C_RY, C_NQ, C_NK, C_NV = 1024, 1280, 1536, 1792, 2048, 2304
C_SK, C_SV, C_G = 2560, 2688, 2816
P_W = 2944


def _cparams(sem):
    return pltpu.CompilerParams(dimension_semantics=sem, vmem_limit_bytes=VMEM_LIMIT)


def _dot(a, b):
    return jnp.dot(a.astype(BF16), b.astype(BF16), preferred_element_type=F32)


def _dot_nt(a, b):
    return lax.dot_general(a.astype(BF16), b.astype(BF16), (((1,), (1,)), ((), ())),
                           preferred_element_type=F32)


def _dot_tn(a, b):
    return lax.dot_general(a.astype(BF16), b.astype(BF16), (((0,), (0,)), ((), ())),
                           preferred_element_type=F32)


def _split3(x):
    hi = x.astype(BF16)
    r1 = x - hi.astype(F32)
    mid = r1.astype(BF16)
    lo = (r1 - mid.astype(F32)).astype(BF16)
    return hi, mid, lo


def _dot_exact_rhs(a01, x):
    hi, mid, lo = _split3(x)
    d = lambda p: jnp.dot(a01, p, preferred_element_type=F32)
    return d(hi) + d(mid) + d(lo)


def _dot_exact_lhs(x, a01):
    hi, mid, lo = _split3(x)
    d = lambda p: jnp.dot(p, a01, preferred_element_type=F32)
    return d(hi) + d(mid) + d(lo)


def _sigmoid(x):
    return 0.5 * jnp.tanh(0.5 * x) + 0.5


def _rms(x, g):
    return x * lax.rsqrt(jnp.mean(x * x, axis=-1, keepdims=True) + EPS) * g


def _rms_head_pairs(x, g):
    lane = lax.broadcasted_iota(jnp.int32, x.shape, 1)
    left = lane < HEAD_DIM
    sq = x * x
    s0 = jnp.sum(jnp.where(left, sq, 0.0), axis=-1, keepdims=True)
    s1 = jnp.sum(jnp.where(left, 0.0, sq), axis=-1, keepdims=True)
    ms = jnp.where(left, s0, s1) * (1.0 / HEAD_DIM)
    return x * lax.rsqrt(ms + EPS) * g


MOD_TN = 1536
MOD_ROWS = 3


def _mod_kernel(ct_ref, w_ref, b_ref, o_ref):
    ct = ct_ref[...]
    st = ct * jax.nn.sigmoid(ct)
    w = w_ref[0]
    o_ref[...] = jnp.zeros(o_ref.shape, F32)
    for r in range(MOD_ROWS):
        o_ref[0, r:r + 1, :] = jnp.sum(w * st[:, r:r + 1], axis=0, keepdims=True) + b_ref[0]


def _mod_table(cvecs_t, w_ada, b_ada):
    n = 6 * D_MODEL
    return pl.pallas_call(
        _mod_kernel,
        grid=(DEPTH, n // MOD_TN),
        in_specs=[pl.BlockSpec((D_MODEL, 8), lambda l, j: (0, 0)),
                  pl.BlockSpec((1, D_MODEL, MOD_TN), lambda l, j: (l, 0, j)),
                  pl.BlockSpec((1, 1, MOD_TN), lambda l, j: (l, 0, j))],
        out_specs=pl.BlockSpec((1, 8, MOD_TN), lambda l, j: (l, 0, j)),
        out_shape=jax.ShapeDtypeStruct((DEPTH, 8, n), F32),
        compiler_params=_cparams(("arbitrary", "arbitrary")),
        name="adaln_mod",
    )(cvecs_t, w_ada, b_ada.reshape(DEPTH, 1, n))


IN_TM = 256


def _swap16(y):
    lane = lax.broadcasted_iota(jnp.int32, y.shape, 1)
    first = (lane % 32) < 16
    return jnp.where(first, pltpu.roll(y, LANES - 16, 1), pltpu.roll(y, 16, 1))


def _seg_pair_specs(tm, width, lat_row0=0):
    nt = SEG // tm
    lat_off = lat_row0 // tm
    return (pl.BlockSpec((tm, width), lambda s, i: (jnp.minimum(s * nt + i, nt - 1), 0)),
            pl.BlockSpec((tm, width), lambda s, i: (lat_off + jnp.maximum(s * nt + i - nt, 0), 0)))


def _pick(c_ref, l_ref):
    return jnp.where(pl.program_id(0) == 0, c_ref[...], l_ref[...])


def _x_pair(x):
    return (x, 0) if isinstance(x, tuple) else ((x, x), SEG)


def _inproj_kernel(xc_ref, xl_ref, mod_ref, g_ref, w_ref, qkg_ref, cos_ref, sin_ref, o_ref):
    seg = pl.program_id(0)
    mod = mod_ref[0]
    sh1 = mod[:, 0:D_MODEL]
    sc1 = mod[:, D_MODEL:2 * D_MODEL]
    h = _rms(_pick(xc_ref, xl_ref), g_ref[...]) * (1.0 + sc1) + sh1
    r = jnp.dot(h.astype(BF16), w_ref[...], preferred_element_type=F32)
    o_ref[:, 0:C_SQ] = r[:, 0:C_SQ]
    o_ref[:, C_RX:C_NQ] = r[:, C_RX:C_NQ]
    o_ref[:, C_NV:C_SK] = r[:, C_NV:C_SK]
    o_ref[:, C_SV:P_W] = r[:, C_SV:P_W]
    cos = cos_ref[...]
    sin = sin_ref[...]
    latent = seg > 0

    def rope(y):
        return jnp.where(latent, y * cos + _swap16(y) * sin, y)

    for p in range(2):
        a = C_SQ + LANES * p
        o_ref[:, a:a + LANES] = rope(_rms_head_pairs(r[:, a:a + LANES], qkg_ref[0:1, :]))
    o_ref[:, C_SK:C_SK + LANES] = rope(_rms_head_pairs(r[:, C_SK:C_SK + LANES], qkg_ref[1:2, :]))
    for p in range(2):
        a = C_NQ + LANES * p
        o_ref[:, a:a + LANES] = _rms_head_pairs(r[:, a:a + LANES], qkg_ref[2:3, :])
        a = C_NK + LANES * p
        o_ref[:, a:a + LANES] = _rms_head_pairs(r[:, a:a + LANES], qkg_ref[3:4, :])


def _inproj(x, mod_l, norm1, w_in_p, qk_gains, cos_t, sin_t):
    nt = SEG // IN_TM
    x_pair, lat_row0 = _x_pair(x)
    return pl.pallas_call(
        _inproj_kernel,
        grid=(N_SEG, nt),
        in_specs=[*_seg_pair_specs(IN_TM, D_MODEL, lat_row0),
                  pl.BlockSpec((1, 1, 6 * D_MODEL), lambda s, i: (s, 0, 0)),
                  pl.BlockSpec((1, D_MODEL), lambda s, i: (0, 0)),
                  pl.BlockSpec((D_MODEL, P_W), lambda s, i: (0, 0)),
                  pl.BlockSpec((4, LANES), lambda s, i: (0, 0)),
                  pl.BlockSpec((IN_TM, LANES), lambda s, i: (i, 0)),
                  pl.BlockSpec((IN_TM, LANES), lambda s, i: (i, 0))],
        out_specs=pl.BlockSpec((IN_TM, P_W), lambda s, i: (s * nt + i, 0)),
        out_shape=jax.ShapeDtypeStruct((N_TOK, P_W), F32),
        compiler_params=_cparams(("arbitrary", "arbitrary")),
        name="inproj",
    )(*x_pair, mod_l, norm1, w_in_p, qk_gains, cos_t, sin_t)


def _mlstm_direction(d, q_ref, k_ref, v_ref, g_ref, bias, cbd, nbd, m_s):
    ch = ML_CHUNK
    r_io = lax.broadcasted_iota(jnp.int32, (ch, ch), 0)
    c_io = lax.broadcasted_iota(jnp.int32, (ch, ch), 1)
    lower = r_io >= c_io
    upper = r_io <= c_io
    mask = lower if d == 0 else upper
    tri = mask.astype(BF16)
    tri_t = (upper if d == 0 else lower).astype(BF16)
    left = c_io < HEAD_DIM
    top = r_io < HEAD_DIM
    blockdiag = top == left
    g = g_ref[...] + bias[...]
    b_cols = _dot_exact_rhs(tri, jax.nn.log_sigmoid(g))
    b3 = jnp.concatenate(_split3(b_cols), axis=1)
    gt = g.T
    li_rows = gt[0:8, :]
    b_rows = _dot_exact_lhs(jax.nn.log_sigmoid(gt[8:16, :]), tri_t)
    a_rows = li_rows - b_rows
    bl = b_rows[:, ch - 1:ch] if d == 0 else b_rows[:, 0:1]
    m_old = m_s[:, 0:1]
    g_rows = bl - b_rows + li_rows
    m_new = jnp.maximum(bl + m_old, jnp.max(g_rows, axis=1, keepdims=True))
    wk_rows = jnp.exp(g_rows - m_new)
    wp = jnp.exp(bl + m_old - m_new)
    sel_r = lax.broadcasted_iota(jnp.int32, (3 * LANES, LANES), 0) % LANES
    sel_left = lax.broadcasted_iota(jnp.int32, (3 * LANES, LANES), 1) < HEAD_DIM
    ones_blk = jnp.ones((ch, LANES), BF16)
    left2 = lax.broadcasted_iota(jnp.int32, (ch, 2 * LANES), 1) % LANES < HEAD_DIM
    h_out, c_out, n_out = [], [], []
    for p in range(ML_HEADS // 2):
        lanes = slice(LANES * p, LANES * (p + 1))
        j0 = ML_HEADS * d + 2 * p
        q2 = q_ref[:, lanes]
        k2t = (k_ref[:, lanes] * SCALE).T.astype(BF16)
        v2e = jnp.concatenate([v_ref[:, lanes].astype(BF16), ones_blk], axis=1)
        cb = cbd[2 * d + p]
        nb = nbd[2 * d + p]
        q2b = q2.astype(BF16)
        q_lo = (q2 - q2b.astype(F32)).astype(BF16)
        nb_hi = nb.astype(BF16)
        nb_lo = (nb - nb_hi.astype(F32)).astype(BF16)
        qc = jnp.dot(q2b, cb.astype(BF16), preferred_element_type=F32)
        qn = jnp.dot(jnp.concatenate([q2b, q_lo, q2b], axis=1), jnp.concatenate([nb_hi, nb_hi, nb_lo], axis=0),
                     preferred_element_type=F32)
        sel = (sel_r == jnp.where(sel_left, 8 + j0, 9 + j0)).astype(BF16)
        b_pair = jnp.dot(b3, sel, preferred_element_type=F32)
        cbs, sves = [], []
        for i in range(2):
            j = j0 + i
            half = left if i == 0 else jnp.logical_not(left)
            a_mat = jnp.where(mask, a_rows[j:j + 1, :], NEG)
            cvec = jnp.maximum(m_old[j:j + 1, :], jnp.max(a_mat, axis=1, keepdims=True))
            cbro = jnp.broadcast_to(cvec, (ch, ch))
            s = jnp.dot(jnp.where(half, q2b, 0), k2t, preferred_element_type=F32) * jnp.exp(a_mat - cbro)
            s_hi = s.astype(BF16)
            s_lo = (s - s_hi.astype(F32)).astype(BF16)
            sve = jnp.dot(s_hi, v2e, preferred_element_type=F32)
            rs_lo = jnp.dot(s_lo, ones_blk, preferred_element_type=F32)
            sves.append(jnp.concatenate([sve[:, 0:LANES], sve[:, LANES:2 * LANES] + rs_lo], axis=1))
            cbs.append(cbro)
        c_pair = jnp.where(left, cbs[0], cbs[1])
        w_prev = jnp.exp(jnp.where(left, m_old[j0:j0 + 1, :], m_old[j0 + 1:j0 + 2, :]) - c_pair)
        sve = jnp.where(left2, sves[0], sves[1])
        num = w_prev * qc + sve[:, 0:LANES]
        den = w_prev * qn + sve[:, LANES:2 * LANES]
        h_out.append(num / jnp.maximum(jnp.abs(den), jnp.exp(-(c_pair + b_pair))))
        kwt = k2t * jnp.where(top, wk_rows[j0:j0 + 1, :], wk_rows[j0 + 1:j0 + 2, :])
        kwt_hi = kwt.astype(BF16)
        kwt_lo = (kwt - kwt_hi.astype(F32)).astype(BF16)
        kve = jnp.dot(kwt_hi, v2e, preferred_element_type=F32)
        kn = kve[:, LANES:2 * LANES] + jnp.dot(kwt_lo, ones_blk, preferred_element_type=F32)
        wp_pair = jnp.where(top, wp[j0:j0 + 1, :], wp[j0 + 1:j0 + 2, :])
        c_out.append(wp_pair * cb + jnp.where(blockdiag, kve[:, 0:LANES], 0.0))
        n_out.append(wp_pair * nb + jnp.where(blockdiag, kn, 0.0))
    return h_out, c_out, n_out, m_new


def _mlstm_kernel(qf, kf, vf, gf, qb, kb, vb, gb, c0, n0, m0, bias,
                  hf, hb, co, no, mo, cbd, nbd, m_s, *, nc):
    c = pl.program_id(1)

    @pl.when(c == 0)
    def _():
        cbd[...] = c0[0]
        nbd[...] = n0[0]
        m_s[...] = m0[0]

    res = [_mlstm_direction(d, *refs, bias, cbd, nbd, m_s)
           for d, refs in enumerate(((qf, kf, vf, gf), (qb, kb, vb, gb)))]
    for d, (h_ref, (h_out, c_out, n_out, m_new)) in enumerate(zip((hf, hb), res)):
        for p in range(ML_HEADS // 2):
            h_ref[:, LANES * p:LANES * (p + 1)] = h_out[p]
            cbd[2 * d + p] = c_out[p]
            nbd[2 * d + p] = n_out[p]
        rows = slice(ML_HEADS * d, ML_HEADS * (d + 1))
        m_s[rows, :] = jnp.broadcast_to(m_new[rows, :], (ML_HEADS, LANES))

    @pl.when(c == nc - 1)
    def _():
        co[0] = cbd[...]
        no[0] = nbd[...]
        mo[0] = m_s[...]


def _mlstm_pack_state(c0, n0, m0):
    bsz = c0.shape[0]
    hd = HEAD_DIM
    cbd = jnp.zeros((bsz, ML_HEADS, LANES, LANES), F32)
    cbd = cbd.at[:, :, :hd, :hd].set(c0[:, 0::2]).at[:, :, hd:, hd:].set(c0[:, 1::2])
    nbd = jnp.zeros((bsz, ML_HEADS, LANES, LANES), F32)
    rep = lambda t: jnp.broadcast_to(t[..., None], t.shape + (hd,))
    nbd = nbd.at[:, :, :hd, :hd].set(rep(n0[:, 0::2])).at[:, :, hd:, hd:].set(rep(n0[:, 1::2]))
    return cbd, nbd, jnp.broadcast_to(m0[..., None], m0.shape + (LANES,))


def _mlstm_unpack_state(cbd, nbd, mrow):
    hd = HEAD_DIM
    bsz = cbd.shape[0]
    c = jnp.stack([cbd[:, :, :hd, :hd], cbd[:, :, hd:, hd:]], axis=2).reshape(bsz, 2 * ML_HEADS, hd, hd)
    n = jnp.stack([nbd[:, :, :hd, 0], nbd[:, :, hd:, hd]], axis=2).reshape(bsz, 2 * ML_HEADS, hd)
    return c, n, mrow[:, :, 0]


def _mlstm(proj, gate_bias, c0, n0, m0, *, bsz, seq, row0):
    nc = seq // ML_CHUNK
    base = row0 // ML_CHUNK
    nj = 2 * ML_HEADS
    fw = lambda col: (lambda b, c: (base + b * nc + c, col))
    bw = lambda col: (lambda b, c: (base + b * nc + nc - 1 - c, col))
    qkv = lambda f: [pl.BlockSpec((ML_CHUNK, 256), f(C_MQ // 256)),
                     pl.BlockSpec((ML_CHUNK, 256), f(C_MK // 256)),
                     pl.BlockSpec((ML_CHUNK, 256), f(C_MV // 256)),
                     pl.BlockSpec((ML_CHUNK, LANES), f(C_G // LANES))]
    st_specs = [pl.BlockSpec((1, nj // 2, LANES, LANES), lambda b, c: (b, 0, 0, 0)),
                pl.BlockSpec((1, nj // 2, LANES, LANES), lambda b, c: (b, 0, 0, 0)),
                pl.BlockSpec((1, nj, LANES), lambda b, c: (b, 0, 0))]
    st_shapes = [jax.ShapeDtypeStruct((bsz, nj // 2, LANES, LANES), F32),
                 jax.ShapeDtypeStruct((bsz, nj // 2, LANES, LANES), F32),
                 jax.ShapeDtypeStruct((bsz, nj, LANES), F32)]
    return pl.pallas_call(
        functools.partial(_mlstm_kernel, nc=nc),
        grid=(bsz, nc),
        in_specs=qkv(fw) + qkv(bw) + st_specs + [pl.BlockSpec((1, LANES), lambda b, c: (0, 0))],
        out_specs=[pl.BlockSpec((ML_CHUNK, 256), lambda b, c: (b * nc + c, 0)),
                   pl.BlockSpec((ML_CHUNK, 256), lambda b, c: (b * nc + nc - 1 - c, 0))] + st_specs,
        out_shape=[jax.ShapeDtypeStruct((bsz * seq, 256), F32),
                   jax.ShapeDtypeStruct((bsz * seq, 256), F32)] + st_shapes,
        scratch_shapes=[pltpu.VMEM((nj // 2, LANES, LANES), F32),
                        pltpu.VMEM((nj // 2, LANES, LANES), F32),
                        pltpu.VMEM((nj, LANES), F32)],
        compiler_params=_cparams(("arbitrary", "arbitrary")),
        name="mlstm",
    )(proj, proj, proj, proj, proj, proj, proj, proj, c0, n0, m0, gate_bias)


RG_TC = 256
RG_PAD = 8


def _rglru_kernel(rx, ry, cw, cb, wg, bg, lam, h0, oc, hl, xpad, af, ab, ub, *, seq):
    xpad[0:RG_PAD, :] = jnp.zeros((RG_PAD, RG_WIDTH), F32)
    xpad[seq + RG_PAD:seq + 2 * RG_PAD, :] = jnp.zeros((RG_PAD, RG_WIDTH), F32)
    xpad[RG_PAD:seq + RG_PAD, :] = rx[...]
    sp = jax.nn.softplus(-lam[...])
    left = (RG_CONV - 1) // 2
    for ci in range(seq // RG_TC):
        s0 = ci * RG_TC
        xc = None
        for j in range(RG_CONV):
            a = RG_PAD + s0 + j - left
            term = xpad[a:a + RG_TC, :] * cw[j:j + 1, :]
            xc = term if xc is None else xc + term
        xc = xc + cb[...]
        pre = _dot(xc, wg[...]) + bg[...]
        for d in range(2):
            o = 2 * RG_WIDTH * d
            r = _sigmoid(pre[:, o:o + RG_WIDTH])
            gi = _sigmoid(pre[:, o + RG_WIDTH:o + 2 * RG_WIDTH])
            log_a = -RG_C * r * sp[d:d + 1, :]
            a_val = jnp.exp(log_a)
            u_val = jnp.sqrt(-jnp.tanh(log_a) * (a_val * a_val + 1.0)) * (gi * xc)
            if d == 0:
                af[s0:s0 + RG_TC, :] = a_val
                oc[s0:s0 + RG_TC, :] = u_val
            else:
                ab[s0:s0 + RG_TC, :] = a_val
                ub[s0:s0 + RG_TC, :] = u_val

    def body(t, carry):
        h_f, h_b = carry
        h_f = af[pl.ds(t, 1), :] * h_f + oc[pl.ds(t, 1), :]
        oc[pl.ds(t, 1), :] = h_f
        tb = seq - 1 - t
        h_b = ab[pl.ds(tb, 1), :] * h_b + ub[pl.ds(tb, 1), :]
        ub[pl.ds(tb, 1), :] = h_b
        return h_f, h_b

    h_f, h_b = lax.fori_loop(0, seq, body, (h0[0, 0:1, :], h0[0, 1:2, :]), unroll=8)
    hl[0, 0:1, :] = h_f
    hl[0, 1:2, :] = h_b
    for ci in range(seq // RG_TC):
        sl = slice(ci * RG_TC, (ci + 1) * RG_TC)
        oc[sl, :] = (oc[sl, :] + ub[sl, :]) * jax.nn.gelu(ry[sl, :])


def _rglru(proj, cw, cb, wg, bg, lam, h0, *, bsz, seq, row0):
    base = row0 // seq
    full = lambda shape: pl.BlockSpec(shape, lambda b: tuple(0 for _ in shape))
    return pl.pallas_call(
        functools.partial(_rglru_kernel, seq=seq),
        grid=(bsz,),
        in_specs=[pl.BlockSpec((seq, RG_WIDTH), lambda b: (base + b, C_RX // RG_WIDTH)),
                  pl.BlockSpec((seq, RG_WIDTH), lambda b: (base + b, C_RY // RG_WIDTH)),
                  full((RG_CONV, RG_WIDTH)), full((1, RG_WIDTH)),
                  full((RG_WIDTH, 4 * RG_WIDTH)), full((1, 4 * RG_WIDTH)), full((2, RG_WIDTH)),
                  pl.BlockSpec((1, 2, RG_WIDTH), lambda b: (b, 0, 0))],
        out_specs=[pl.BlockSpec((seq, RG_WIDTH), lambda b: (b, 0)),
                   pl.BlockSpec((1, 2, RG_WIDTH), lambda b: (b, 0, 0))],
        out_shape=[jax.ShapeDtypeStruct((bsz * seq, RG_WIDTH), F32),
                   jax.ShapeDtypeStruct((bsz, 2, RG_WIDTH), F32)],
        scratch_shapes=[pltpu.VMEM((seq + 2 * RG_PAD, RG_WIDTH), F32),
                        pltpu.VMEM((seq, RG_WIDTH), F32),
                        pltpu.VMEM((seq, RG_WIDTH), F32),
                        pltpu.VMEM((seq, RG_WIDTH), F32)],
        compiler_params=_cparams(("arbitrary",)),
        name="rglru",
    )(proj, proj, cw, cb, wg, bg, lam, h0)


def _softmax_pv(scores, values, sink):
    m = functools.reduce(jnp.maximum, [jnp.max(s, axis=-1, keepdims=True) for s in scores])
    if sink is not None:
        m = jnp.maximum(m, sink)
    ps = [jnp.exp(s - m) for s in scores]
    den = functools.reduce(jnp.add, [jnp.sum(p, axis=-1, keepdims=True) for p in ps])
    if sink is not None:
        den = den + jnp.exp(sink - m)
    num = functools.reduce(jnp.add, [_dot(p, v) for p, v in zip(ps, values)])
    return num / den


def _ctx_attn_kernel(sink, sq, sk, sv, nq, nk, nv, ob, od):
    for h in range(SW_HEADS):
        j = h // (SW_HEADS // SW_KV_HEADS)
        sl = slice(HEAD_DIM * h, HEAD_DIM * (h + 1))
        kv = slice(HEAD_DIM * j, HEAD_DIM * (j + 1))
        s = _dot_nt(sq[:, sl], sk[:, kv]) * SCALE
        ob[:, sl] = _softmax_pv([s], [sv[:, kv]], sink[h])
    for h in range(NA_HEADS):
        sl = slice(HEAD_DIM * h, HEAD_DIM * (h + 1))
        s = _dot_nt(nq[:, sl], nk[:, sl]) * SCALE
        od[:, sl] = _softmax_pv([s], [nv[:, sl]], None)


def _ctx_attn(proj, sink):
    blk = lambda w, col: pl.BlockSpec((SEQ, w), lambda b: (b, col))
    return pl.pallas_call(
        _ctx_attn_kernel,
        grid=(BATCH,),
        in_specs=[pl.BlockSpec(memory_space=pltpu.SMEM),
                  blk(256, C_SQ // 256), blk(128, C_SK // 128), blk(128, C_SV // 128),
                  blk(256, C_NQ // 256), blk(256, C_NK // 256), blk(256, C_NV // 256)],
        out_specs=[pl.BlockSpec((SEQ, 256), lambda b: (b, 0)),
                   pl.BlockSpec((SEQ, 256), lambda b: (b, 0))],
        out_shape=[jax.ShapeDtypeStruct((SEG, 256), F32), jax.ShapeDtypeStruct((SEG, 256), F32)],
        compiler_params=_cparams(("arbitrary",)),
        name="ctx_attn",
    )(sink, proj, proj, proj, proj, proj, proj)


SW_QB = 128
SW_SPAN = SW_QB + 2 * SW_WINDOW


def _swa_kernel(sink, q, k, v, kc, vc, ob):
    n = pl.program_id(1)
    ws = jnp.clip((n - 1) * SW_QB, 0, DEC_SEQ - SW_SPAN)
    ws = pl.multiple_of(ws, SW_QB)
    qpos = n * SW_QB + lax.broadcasted_iota(jnp.int32, (SW_QB, SW_SPAN), 0)
    kpos = ws + lax.broadcasted_iota(jnp.int32, (SW_QB, SW_SPAN), 1)
    valid = jnp.abs(qpos - kpos) <= SW_WINDOW
    for h in range(SW_HEADS):
        j = h // (SW_HEADS // SW_KV_HEADS)
        sl = slice(HEAD_DIM * h, HEAD_DIM * (h + 1))
        kv = slice(HEAD_DIM * j, HEAD_DIM * (j + 1))
        qh = q[:, sl]
        s_loc = jnp.where(valid, _dot_nt(qh, k[pl.ds(ws, SW_SPAN), kv]) * SCALE, NEG)
        s_ctx = _dot_nt(qh, kc[0, :, kv]) * SCALE
        ob[:, sl] = _softmax_pv([s_loc, s_ctx], [v[pl.ds(ws, SW_SPAN), kv], vc[0, :, kv]], sink[h])


def _swa(proj, kc, vc, sink):
    nq = DEC_SEQ // SW_QB
    qbase = SEG // SW_QB
    return pl.pallas_call(
        _swa_kernel,
        grid=(DEC_BATCH, nq),
        in_specs=[pl.BlockSpec(memory_space=pltpu.SMEM),
                  pl.BlockSpec((SW_QB, 256), lambda b, n: (qbase + b * nq + n, C_SQ // 256)),
                  pl.BlockSpec((DEC_SEQ, 128), lambda b, n: (1 + b, C_SK // 128)),
                  pl.BlockSpec((DEC_SEQ, 128), lambda b, n: (1 + b, C_SV // 128)),
                  pl.BlockSpec((1, PAST_LEN, 128), lambda b, n: (b, 0, 0)),
                  pl.BlockSpec((1, PAST_LEN, 128), lambda b, n: (b, 0, 0))],
        out_specs=pl.BlockSpec((SW_QB, 256), lambda b, n: (b * nq + n, 0)),
        out_shape=jax.ShapeDtypeStruct((DEC_BATCH * DEC_SEQ, 256), F32),
        compiler_params=_cparams(("arbitrary", "arbitrary")),
        name="swa",
    )(sink, proj, proj, proj, kc, vc)


NA_RPB_R = 2 * NA_ROWS - 1
NA_RPB_C = 2 * NA_COLS - 1
NA_NKEY = NA_ROWS * GRID_W
GRID_ROWS = DEC_SEQ // GRID_W


def _na_bias_kernel(rpb, out):
    h = pl.program_id(0)
    qc = lax.broadcasted_iota(jnp.int32, (GRID_W, GRID_W), 0)
    kc = lax.broadcasted_iota(jnp.int32, (GRID_W, GRID_W), 1)
    dc = jnp.clip(kc - qc, -(NA_COLS - 1), NA_COLS - 1) + NA_COLS - 1
    lo = jnp.clip(qc - NA_COLS // 2, 0, GRID_W - NA_COLS)
    valid = (kc >= lo) & (kc < lo + NA_COLS)
    tiles = []
    for dr in range(NA_RPB_R):
        t = jnp.zeros((GRID_W, GRID_W), F32)
        for j in range(NA_RPB_C):
            t = jnp.where(dc == j, rpb[(h * NA_RPB_R + dr) * NA_RPB_C + j], t)
        tiles.append(jnp.where(valid, t, NEG))
    for w in range(NA_ROWS):
        for i in range(NA_ROWS):
            out[0, w, :, GRID_W * i:GRID_W * (i + 1)] = tiles[w + i]


def _na_bias(rpb_flat):
    return pl.pallas_call(
        _na_bias_kernel,
        grid=(NA_HEADS,),
        in_specs=[pl.BlockSpec(memory_space=pltpu.SMEM)],
        out_specs=pl.BlockSpec((1, NA_ROWS, GRID_W, NA_NKEY), lambda h: (h, 0, 0, 0)),
        out_shape=jax.ShapeDtypeStruct((NA_HEADS, NA_ROWS, GRID_W, NA_NKEY), F32),
        compiler_params=_cparams(("arbitrary",)),
        name="na_bias",
    )(rpb_flat)


def _na_kernel(q, k, v, kc, vc, bias, od):
    r = pl.program_id(1)
    r0 = jnp.clip(r - NA_ROWS // 2, 0, GRID_ROWS - NA_ROWS)
    k0 = pl.multiple_of(r0 * GRID_W, GRID_W)
    win = r0 - r + NA_ROWS - 1
    for h in range(NA_HEADS):
        sl = slice(HEAD_DIM * h, HEAD_DIM * (h + 1))
        qh = q[:, sl]
        s_loc = _dot_nt(qh, k[pl.ds(k0, NA_NKEY), sl]) * SCALE + bias[h, pl.ds(win, 1)][0]
        s_ctx = _dot_nt(qh, kc[0, :, sl]) * SCALE
        od[:, sl] = _softmax_pv([s_loc, s_ctx], [v[pl.ds(k0, NA_NKEY), sl], vc[0, :, sl]], None)


def _na(proj, kc, vc, bias):
    qbase = SEG // GRID_W
    return pl.pallas_call(
        _na_kernel,
        grid=(DEC_BATCH, GRID_ROWS),
        in_specs=[pl.BlockSpec((GRID_W, 256), lambda b, r: (qbase + b * GRID_ROWS + r, C_NQ // 256)),
                  pl.BlockSpec((DEC_SEQ, 256), lambda b, r: (1 + b, C_NK // 256)),
                  pl.BlockSpec((DEC_SEQ, 256), lambda b, r: (1 + b, C_NV // 256)),
                  pl.BlockSpec((1, PAST_LEN, 256), lambda b, r: (b, 0, 0)),
                  pl.BlockSpec((1, PAST_LEN, 256), lambda b, r: (b, 0, 0)),
                  pl.BlockSpec((NA_HEADS, NA_ROWS, GRID_W, NA_NKEY), lambda b, r: (0, 0, 0, 0))],
        out_specs=pl.BlockSpec((GRID_W, 256), lambda b, r: (b * GRID_ROWS + r, 0)),
        out_shape=jax.ShapeDtypeStruct((DEC_BATCH * DEC_SEQ, 256), F32),
        compiler_params=_cparams(("arbitrary", "arbitrary")),
        name="na",
    )(proj, proj, proj, kc, vc, bias)


MG_TM = 256


def _merge_kernel(xc_ref, xl_ref, mod_ref, g1_ref, g2_ref, hf_c, hf_l, hb_c, hb_l, mo, ob_c, ob_l, oc_c, oc_l,
                  od_c, od_l, hn, wmg, bmg, wbr, wout, *rest, moe):
    if moe:
        wrt, br, x1_ref, h2_ref, route_ref = rest
    else:
        x1_ref, h2_ref = rest
    pick = _pick
    mod = mod_ref[0]
    chunk = lambda i: mod[:, i * D_MODEL:(i + 1) * D_MODEL]
    sh1, sc1, gate1, sh2, sc2 = chunk(0), chunk(1), chunk(2), chunk(3), chunk(4)
    x = pick(xc_ref, xl_ref)
    h = (_rms(x, g1_ref[...]) * (1.0 + sc1) + sh1).astype(BF16)
    hsum = pick(hf_c, hf_l) + pick(hb_c, hb_l)
    out_a = jnp.concatenate(
        [_rms_head_pairs(hsum[:, LANES * p:LANES * (p + 1)], hn[...]) for p in range(2)], axis=-1)
    out_a = out_a * jax.nn.sigmoid(mo[...])
    acc = None
    for n, br_val in enumerate((out_a, pick(ob_c, ob_l), pick(oc_c, oc_l), pick(od_c, od_l))):
        gate = jax.nn.sigmoid(jnp.dot(h, wmg[:, n * D_MODEL:(n + 1) * D_MODEL], preferred_element_type=F32)
                              + bmg[:, n * D_MODEL:(n + 1) * D_MODEL])
        term = gate * jnp.dot(br_val.astype(BF16), wbr[n], preferred_element_type=F32)
        acc = term if acc is None else acc + term
    y = jnp.dot(acc.astype(BF16), wout[...], preferred_element_type=F32)
    x1 = x + gate1 * y
    x1_ref[...] = x1
    h2 = _rms(x1, g2_ref[...]) * (1.0 + sc2) + sh2
    h2_ref[...] = h2.astype(h2_ref.dtype)
    if moe:
        logit = [jnp.sum(h2 * wrt[e:e + 1, :], axis=-1, keepdims=True) + br[e] for e in range(N_EXPERTS)]
        v1, i1 = logit[0], jnp.zeros(logit[0].shape, jnp.int32)
        for e in range(1, N_EXPERTS):
            better = logit[e] > v1
            v1 = jnp.where(better, logit[e], v1)
            i1 = jnp.where(better, e, i1)
        v2, i2 = jnp.full(v1.shape, -jnp.inf, F32), jnp.zeros(v1.shape, jnp.int32)
        for e in range(N_EXPERTS):
            better = (i1 != e) & (logit[e] > v2)
            v2 = jnp.where(better, logit[e], v2)
            i2 = jnp.where(better, e, i2)
        e2 = jnp.exp(v2 - v1)
        den = 1.0 + e2
        lane = lax.broadcasted_iota(jnp.int32, route_ref.shape, 1)
        route = jnp.where(lane == 0, 1.0 / den, 0.0) + jnp.where(lane == 1, e2 / den, 0.0)
        route = route + jnp.where(lane == 2, i1.astype(F32), 0.0) + jnp.where(lane == 3, i2.astype(F32), 0.0)
        route_ref[...] = route


def _merge(x, mod_l, g1, g2, hf, hb, proj, ob, oc, od, hn, wmg, bmg, wbr, wout, router=None):
    nt = SEG // MG_TM
    moe = router is not None
    x_pair, lat_row0 = _x_pair(x)
    row = lambda w: pl.BlockSpec((MG_TM, w), lambda s, i: (s * nt + i, 0))
    ctx_blk, lat_blk = _seg_pair_specs(MG_TM, 256)
    full = lambda shape: pl.BlockSpec(shape, lambda s, i: tuple(0 for _ in shape))
    in_specs = [*_seg_pair_specs(MG_TM, D_MODEL, lat_row0),
                pl.BlockSpec((1, 1, 6 * D_MODEL), lambda s, i: (s, 0, 0)),
                full((1, D_MODEL)), full((1, D_MODEL)),
                ctx_blk, lat_blk, ctx_blk, lat_blk,
                pl.BlockSpec((MG_TM, 256), lambda s, i: (s * nt + i, C_MO // 256)),
                ctx_blk, lat_blk, ctx_blk, lat_blk, ctx_blk, lat_blk,
                full((1, LANES)), full((D_MODEL, N_BRANCH * D_MODEL)), full((1, N_BRANCH * D_MODEL)),
                full((N_BRANCH, 256, D_MODEL)), full((D_MODEL, D_MODEL))]
    args = [*x_pair, mod_l, g1, g2, *hf, *hb, proj, *ob, *oc, *od, hn, wmg, bmg, wbr, wout]
    out_specs = [row(D_MODEL), row(D_MODEL)]
    out_shape = [jax.ShapeDtypeStruct((N_TOK, D_MODEL), F32),
                 jax.ShapeDtypeStruct((N_TOK, D_MODEL), F32 if moe else BF16)]
    if moe:
        in_specs += [full((N_EXPERTS, D_MODEL)), pl.BlockSpec(memory_space=pltpu.SMEM)]
        args += list(router)
        out_specs.append(row(LANES))
        out_shape.append(jax.ShapeDtypeStruct((N_TOK, LANES), F32))
    return pl.pallas_call(
        functools.partial(_merge_kernel, moe=moe),
        grid=(N_SEG, nt),
        in_specs=in_specs,
        out_specs=out_specs,
        out_shape=out_shape,
        compiler_params=_cparams(("arbitrary", "arbitrary")),
        name="merge",
    )(*args)


FF_TM = 512
FF_TF = 1408


def _ffn_kernel(h2, w1, w3, w2, x1, mod_ref, out, acc):
    f = pl.program_id(1)

    @pl.when(f == 0)
    def _():
        acc[...] = jnp.zeros(acc.shape, F32)

    h = h2[...]
    a = jnp.dot(h, w1[...], preferred_element_type=F32)
    b = jnp.dot(h, w3[...], preferred_element_type=F32)
    act = (jax.nn.silu(a) * b).astype(BF16)
    acc[...] += jnp.dot(act, w2[...], preferred_element_type=F32)

    @pl.when(f == pl.num_programs(1) - 1)
    def _():
        gate2 = mod_ref[0][:, 5 * D_MODEL:6 * D_MODEL]
        out[...] = x1[...] + gate2 * acc[...]


def _ffn(h2, x1, mod_l, w1, w3, w2):
    nt = N_TOK // FF_TM
    per_seg = SEG // FF_TM
    return pl.pallas_call(
        _ffn_kernel,
        grid=(nt, D_FF // FF_TF),
        in_specs=[pl.BlockSpec((FF_TM, D_MODEL), lambda i, f: (i, 0)),
                  pl.BlockSpec((D_MODEL, FF_TF), lambda i, f: (0, f)),
                  pl.BlockSpec((D_MODEL, FF_TF), lambda i, f: (0, f)),
                  pl.BlockSpec((FF_TF, D_MODEL), lambda i, f: (f, 0)),
                  pl.BlockSpec((FF_TM, D_MODEL), lambda i, f: (i, 0)),
                  pl.BlockSpec((1, 1, 6 * D_MODEL), lambda i, f: (i // per_seg, 0, 0))],
        out_specs=pl.BlockSpec((FF_TM, D_MODEL), lambda i, f: (i, 0)),
        out_shape=jax.ShapeDtypeStruct((N_TOK, D_MODEL), F32),
        scratch_shapes=[pltpu.VMEM((FF_TM, D_MODEL), F32)],
        compiler_params=_cparams(("arbitrary", "arbitrary")),
        name="ffn",
    )(h2, w1, w3, w2, x1, mod_l)


MOE_TM = 256
MOE_SLOTS = 2 * N_TOK
MOE_TILES = MOE_SLOTS // MOE_TM + N_EXPERTS
MOE_NBUF = 3
MOE_STEPS = MOE_TILES + MOE_NBUF
MOE_DUMP = MOE_NBUF * MOE_TM
MOE_LEAD = 1
MOE_PLAN_TILES = MOE_LEAD + MOE_TILES + 2
MOE_FCHUNKS = 1
MOE_UNROLL = 8


def _moe_group_kernel(texp, nused, src_tok, dst_row, h2_hbm, w1, w3, w2, y_hbm, xs, ys, sem_in, sem_out):
    del texp
    i = pl.program_id(0)
    n_used = nused[0]
    buf = i % MOE_NBUF
    buf_next = (i + 2) % MOE_NBUF

    def gather_copy(tile, b, r):
        tok = src_tok[(tile + MOE_LEAD) * MOE_TM + r]
        return pltpu.make_async_copy(h2_hbm.at[pl.ds(tok, 1)], xs.at[b, pl.ds(r, 1)], sem_in.at[b])

    def scatter_copy(tile, b, r):
        dst = dst_row[(tile + MOE_LEAD) * MOE_TM + r]
        return pltpu.make_async_copy(ys.at[b, pl.ds(r, 1)], y_hbm.at[pl.ds(dst, 1)], sem_out.at[b])

    def start_rows_loop(make_copy, tile, b):
        def body(r, carry):
            make_copy(tile, b, r).start()
            return carry
        lax.fori_loop(0, MOE_TM, body, 0, unroll=MOE_UNROLL)

    def wait_tile(b, gather):
        if gather:
            pltpu.make_async_copy(h2_hbm.at[pl.ds(0, MOE_TM)], xs.at[b], sem_in.at[b]).wait()
        else:
            pltpu.make_async_copy(ys.at[b], y_hbm.at[pl.ds(0, MOE_TM)], sem_out.at[b]).wait()

    @pl.when(i == 0)
    def _():
        xs[...] = jnp.zeros(xs.shape, F32)
        ys[...] = jnp.zeros(ys.shape, F32)
        for b in range(MOE_NBUF):
            fill = pltpu.make_async_copy(ys.at[b], y_hbm.at[pl.ds(MOE_SLOTS + b * MOE_TM, MOE_TM)], sem_out.at[b])
            fill.start()
            fill.wait()
        start_rows_loop(gather_copy, 0, 0)
        start_rows_loop(gather_copy, 1, 1)

    @pl.when(i <= n_used + 1)
    def _():
        wait_tile(buf, True)

    @pl.when((i >= 2) & (i <= n_used + 2))
    def _():
        wait_tile(buf, False)

    @pl.when(i < n_used)
    def _():
        x = xs[buf].astype(BF16)
        fc = D_FF_EXPERT // MOE_FCHUNKS
        rc = MOE_TM // MOE_FCHUNKS
        y = None
        for c in range(MOE_FCHUNKS):
            for r in range(c * rc, (c + 1) * rc):
                gather_copy(i + 2, buf_next, r).start()
                scatter_copy(i - 1, buf_next, r).start()
            a = jnp.dot(x, w1[0, :, c * fc:(c + 1) * fc], preferred_element_type=F32)
            b = jnp.dot(x, w3[0, :, c * fc:(c + 1) * fc], preferred_element_type=F32)
            act = (jax.nn.silu(a) * b).astype(BF16)
            part = jnp.dot(act, w2[0, c * fc:(c + 1) * fc, :], preferred_element_type=F32)
            y = part if y is None else y + part
        ys[buf] = y

    @pl.when(i == n_used)
    def _():
        start_rows_loop(scatter_copy, i - 1, buf_next)


def _moe_group(tile_expert, n_used, src_tok, dst_row, h2, w1, w3, w2):
    wspec = lambda shape: pl.BlockSpec((1,) + shape, lambda i, texp, *_: (texp[jnp.minimum(i, MOE_TILES - 1)], 0, 0))
    grid_spec = pltpu.PrefetchScalarGridSpec(
        num_scalar_prefetch=4,
        grid=(MOE_STEPS,),
        in_specs=[pl.BlockSpec(memory_space=pl.ANY),
                  wspec((D_MODEL, D_FF_EXPERT)), wspec((D_MODEL, D_FF_EXPERT)), wspec((D_FF_EXPERT, D_MODEL))],
        out_specs=pl.BlockSpec(memory_space=pl.ANY),
        scratch_shapes=[pltpu.VMEM((MOE_NBUF, MOE_TM, D_MODEL), F32), pltpu.VMEM((MOE_NBUF, MOE_TM, D_MODEL), F32),
                        pltpu.SemaphoreType.DMA((MOE_NBUF,)), pltpu.SemaphoreType.DMA((MOE_NBUF,))])
    return pl.pallas_call(
        _moe_group_kernel,
        grid_spec=grid_spec,
        out_shape=jax.ShapeDtypeStruct((MOE_SLOTS + MOE_DUMP, D_MODEL), F32),
        compiler_params=_cparams(("arbitrary",)),
        name="moe_group",
    )(tile_expert, n_used, src_tok, dst_row, h2, w1, w3, w2)


def _moe_plan(expert_ids):
    e_flat = expert_ids.T.reshape(-1)
    order = jnp.argsort(e_flat, stable=True).astype(jnp.int32)
    counts = jnp.sum((e_flat[:, None] == jnp.arange(N_EXPERTS)[None, :]).astype(jnp.int32), axis=0)
    padded = (counts + MOE_TM - 1) // MOE_TM * MOE_TM
    pend = jnp.cumsum(padded)
    pstart = pend - padded
    ustart = jnp.cumsum(counts) - counts
    n_used = pend[-1] // MOE_TM
    tiles = jnp.arange(MOE_TILES, dtype=jnp.int32)
    last_used = jnp.minimum(tiles, n_used - 1)
    tile_expert = jnp.sum((last_used[:, None] * MOE_TM >= pend[None, :]).astype(jnp.int32), axis=1)
    t = jnp.arange(-MOE_LEAD, MOE_PLAN_TILES - MOE_LEAD, dtype=jnp.int32)[:, None]
    r = jnp.arange(MOE_TM, dtype=jnp.int32)[None, :]
    e_t = tile_expert[jnp.clip(t, 0, MOE_TILES - 1)]
    off = t * MOE_TM + r - pstart[e_t]
    valid = (t >= 0) & (t < n_used) & (off < counts[e_t])
    slot = order[jnp.clip(ustart[e_t] + off, 0, MOE_SLOTS - 1)]
    src_tok = jnp.where(valid, slot % N_TOK, 0)
    dst_row = jnp.where(valid, slot, MOE_SLOTS + (t % MOE_NBUF) * MOE_TM + r)
    return (tile_expert.astype(jnp.int32), n_used.reshape(1).astype(jnp.int32),
            src_tok.reshape(-1).astype(jnp.int32), dst_row.reshape(-1).astype(jnp.int32))


def _moe_combine_kernel(x1, y0, y1, route, mod_ref, out_c, out_l):
    gate2 = mod_ref[0][:, 5 * D_MODEL:6 * D_MODEL]
    r = route[...]
    val = x1[...] + gate2 * (r[:, 0:1] * y0[...] + r[:, 1:2] * y1[...])
    is_ctx = pl.program_id(0) < SEG // FF_TM

    @pl.when(is_ctx)
    def _():
        out_c[...] = val

    @pl.when(jnp.logical_not(is_ctx))
    def _():
        out_l[...] = val


def _moe_combine(x1, y_slots, route, mod_l):
    nt = N_TOK // FF_TM
    per_seg = SEG // FF_TM
    return pl.pallas_call(
        _moe_combine_kernel,
        grid=(nt,),
        in_specs=[pl.BlockSpec((FF_TM, D_MODEL), lambda i: (i, 0)),
                  pl.BlockSpec((FF_TM, D_MODEL), lambda i: (i, 0)),
                  pl.BlockSpec((FF_TM, D_MODEL), lambda i: (nt + i, 0)),
                  pl.BlockSpec((FF_TM, LANES), lambda i: (i, 0)),
                  pl.BlockSpec((1, 1, 6 * D_MODEL), lambda i: (i // per_seg, 0, 0))],
        out_specs=[pl.BlockSpec((FF_TM, D_MODEL), lambda i: (jnp.minimum(i, per_seg - 1), 0)),
                   pl.BlockSpec((FF_TM, D_MODEL), lambda i: (jnp.maximum(i - per_seg, 0), 0))],
        out_shape=[jax.ShapeDtypeStruct((SEG, D_MODEL), F32), jax.ShapeDtypeStruct((N_TOK - SEG, D_MODEL), F32)],
        compiler_params=_cparams(("arbitrary",)),
        name="moe_combine",
    )(x1, y_slots, y_slots, route, mod_l)


def _rope_tables():
    t = jnp.arange(DEC_SEQ)
    row, col = (t // GRID_W).astype(F32), (t % GRID_W).astype(F32)
    nf = HEAD_DIM // 4
    freqs = ROPE_BASE ** (-jnp.arange(nf, dtype=F32) / nf)
    lane = np.arange(LANES) % HEAD_DIM
    fidx = lane % nf
    use_col = (lane // (HEAD_DIM // 2)) == 1
    first = (lane % (HEAD_DIM // 2)) < nf
    pos = jnp.where(use_col[None, :], col[:, None], row[:, None])
    ang = pos * freqs[fidx][None, :]
    sin = jnp.sin(ang)
    return jnp.cos(ang), jnp.where(first[None, :], -sin, sin)


def _permute_w_in(w):
    sizes = (256, 256, 256, 256, 8, 8, 256, 128, 128, 256, 256, 256, 256, 256)
    offs = np.concatenate([[0], np.cumsum(sizes)])
    part = lambda i: w[:, offs[i]:offs[i + 1]]
    mq, mk, mv, mo, mi, mf, sq, sk, sv, rx, ry, nq, nk, nv = (part(i) for i in range(14))
    pad = jnp.zeros((w.shape[0], LANES - 16), w.dtype)
    return jnp.concatenate([mq, mk, mv, mo, sq, rx, ry, nq, nk, nv, sk, sv, mi, mf, pad], axis=1)


def _block_diag(w):
    bd = RG_WIDTH // RG_BLOCKS
    out = jnp.zeros((RG_WIDTH, RG_WIDTH), w.dtype)
    for n in range(RG_BLOCKS):
        out = out.at[n * bd:(n + 1) * bd, n * bd:(n + 1) * bd].set(w[n])
    return out


def _tile2(g):
    return jnp.concatenate([g, g]).reshape(1, LANES)


def kernel(x_prompt, x_sample, cache_swa_k, cache_swa_v, cache_na_k, cache_na_v, state_mlstm_C, state_mlstm_n, state_mlstm_m, state_rglru_h, c, c_ctx, norm1_g, norm2_g, w_ada, b_ada, w_in, ml_b_i, ml_b_f, ml_hn, sw_qn, sw_kn, sw_sink, rg_conv_w, rg_conv_b, rg_w_r, rg_b_r, rg_w_i, rg_b_i, rg_lam, na_qn, na_kn, na_rpb, w_br, w_mg, b_mg, w_out, ffn_w1, ffn_w3, ffn_w2, moe_wr, moe_br, moe_w1, moe_w3, moe_w2):
    assert DEPTH % 2 == 0
    x_all = (x_prompt.reshape(SEG, D_MODEL), x_sample.reshape(N_TOK - SEG, D_MODEL))
    cvecs = jnp.concatenate([c_ctx[None, :], c, jnp.zeros((8 - 1 - DEC_BATCH, D_MODEL), F32)], axis=0)
    mod = _mod_table(cvecs.T, w_ada, b_ada)
    cos_t, sin_t = _rope_tables()
    nj = 2 * ML_HEADS
    zeros_state = (jnp.zeros((BATCH, nj // 2, LANES, LANES), F32), jnp.zeros((BATCH, nj // 2, LANES, LANES), F32),
                   jnp.zeros((BATCH, nj, LANES), F32), jnp.zeros((BATCH, 2, RG_WIDTH), F32))
    ctx_out = []
    for l in range(DEPTH):
        mod_l = mod[l].reshape(8, 1, 6 * D_MODEL)
        qk_gains = jnp.stack([_tile2(sw_qn[l])[0], _tile2(sw_kn[l])[0], _tile2(na_qn[l])[0], _tile2(na_kn[l])[0]])
        proj = _inproj(x_all, mod_l, norm1_g[l].reshape(1, D_MODEL), _permute_w_in(w_in[l]).astype(BF16),
                       qk_gains, cos_t, sin_t)
        gate_bias = jnp.concatenate([ml_b_i[l].reshape(-1), ml_b_f[l].reshape(-1),
                                     jnp.zeros((LANES - 2 * nj,), F32)]).reshape(1, LANES)
        hf_c, hb_c, *st_new = _mlstm(proj, gate_bias, *zeros_state[:3], bsz=BATCH, seq=SEQ, row0=0)
        c_new, n_new, m_new = _mlstm_unpack_state(*st_new)
        st_lat = _mlstm_pack_state(state_mlstm_C[:, l].reshape(DEC_BATCH, nj, HEAD_DIM, HEAD_DIM),
                                   state_mlstm_n[:, l].reshape(DEC_BATCH, nj, HEAD_DIM),
                                   state_mlstm_m[:, l].reshape(DEC_BATCH, nj))
        hf_l, hb_l, _, _, _ = _mlstm(proj, gate_bias, *st_lat, bsz=DEC_BATCH, seq=DEC_SEQ, row0=SEG)
        wg = jnp.concatenate([_block_diag(rg_w_r[l, 0]), _block_diag(rg_w_i[l, 0]),
                              _block_diag(rg_w_r[l, 1]), _block_diag(rg_w_i[l, 1])], axis=1).astype(BF16)
        bg = jnp.concatenate([rg_b_r[l, 0], rg_b_i[l, 0], rg_b_r[l, 1], rg_b_i[l, 1]]).reshape(1, 4 * RG_WIDTH)
        rg_args = (rg_conv_w[l], rg_conv_b[l].reshape(1, RG_WIDTH), wg, bg, rg_lam[l])
        oc_c, hl_new = _rglru(proj, *rg_args, zeros_state[3], bsz=BATCH, seq=SEQ, row0=0)
        oc_l, _ = _rglru(proj, *rg_args, state_rglru_h[:, l], bsz=DEC_BATCH, seq=DEC_SEQ, row0=SEG)
        ob_c, od_c = _ctx_attn(proj, sw_sink[l])
        ob_l = _swa(proj, cache_swa_k[:, l].reshape(DEC_BATCH, PAST_LEN, 128),
                    cache_swa_v[:, l].reshape(DEC_BATCH, PAST_LEN, 128), sw_sink[l])
        od_l = _na(proj, cache_na_k[:, l].reshape(DEC_BATCH, PAST_LEN, 256),
                   cache_na_v[:, l].reshape(DEC_BATCH, PAST_LEN, 256), _na_bias(na_rpb[l].reshape(-1)))
        moe_layer = l % 2 == 1
        j = l // 2
        router = (moe_wr[j].T, moe_br[j]) if moe_layer else None
        outs = _merge(x_all, mod_l, norm1_g[l].reshape(1, D_MODEL), norm2_g[l].reshape(1, D_MODEL),
                      (hf_c, hf_l), (hb_c, hb_l), proj, (ob_c, ob_l), (oc_c, oc_l), (od_c, od_l),
                      _tile2(ml_hn[l]), w_mg[l].astype(BF16), b_mg[l].reshape(1, -1), w_br[l].astype(BF16),
                      w_out[l].astype(BF16), router)
        if moe_layer:
            x1, h2, route = outs
            plan = _moe_plan(route[:, 2:4].astype(jnp.int32))
            y_slots = _moe_group(*plan, h2, moe_w1[j].astype(BF16), moe_w3[j].astype(BF16), moe_w2[j].astype(BF16))
            x_all = tuple(_moe_combine(x1, y_slots, route, mod_l))
        else:
            x1, h2 = outs
            x_all = _ffn(h2, x1, mod_l, ffn_w1[j].astype(BF16), ffn_w3[j].astype(BF16), ffn_w2[j].astype(BF16))
        pc = proj[:SEG]
        ctx_out.append(dict(
            sw_k=pc[:, C_SK:C_SK + 128].reshape(BATCH, SEQ, SW_KV_HEADS, HEAD_DIM),
            sw_v=pc[:, C_SV:C_SV + 128].reshape(BATCH, SEQ, SW_KV_HEADS, HEAD_DIM),
            na_k=pc[:, C_NK:C_NK + 256].reshape(BATCH, SEQ, NA_HEADS, HEAD_DIM),
            na_v=pc[:, C_NV:C_NV + 256].reshape(BATCH, SEQ, NA_HEADS, HEAD_DIM),
            ml_C=c_new.reshape(BATCH, 2, ML_HEADS, HEAD_DIM, HEAD_DIM),
            ml_n=n_new.reshape(BATCH, 2, ML_HEADS, HEAD_DIM),
            ml_m=m_new.reshape(BATCH, 2, ML_HEADS),
            rg_h=hl_new))
    stack = lambda name: jnp.stack([t[name] for t in ctx_out], axis=1)
    return (x_all[0].reshape(BATCH, SEQ, D_MODEL), x_all[1].reshape(DEC_BATCH, DEC_SEQ, D_MODEL),
            stack('sw_k'), stack('sw_v'), stack('na_k'), stack('na_v'),
            stack('ml_C'), stack('ml_n'), stack('ml_m'), stack('rg_h'))
```

```python
import functools

import numpy as np
import jax
import jax.numpy as jnp
from jax import lax
from jax.experimental import pallas as pl
from jax.experimental.pallas import tpu as pltpu

F32 = jnp.float32
BF16 = jnp.bfloat16

D_MODEL = 1024
BATCH = 16
SEQ = 256
DEPTH = 2
DEC_BATCH = 2
DEC_SEQ = 4096
PAST_LEN = 256
GRID_W = 64
HEAD_DIM = 64
ML_HEADS = 4
ML_CHUNK = 128
SW_HEADS = 4
SW_KV_HEADS = 2
SW_WINDOW = 128
RG_WIDTH = 256
RG_BLOCKS = 4
RG_CONV = 4
RG_C = 8.0
NA_HEADS = 4
NA_ROWS = 8
NA_COLS = 16
N_BRANCH = 4
ROPE_BASE = 10000.0
D_FF = 2816
N_EXPERTS = 8
D_FF_EXPERT = 2048
EPS = 1e-6
NEG = -1e30
SCALE = HEAD_DIM ** -0.5

SEG = 4096
N_SEG = 3
N_TOK = N_SEG * SEG
LANES = 128
VMEM_LIMIT = 56 * 1024 * 1024

C_MQ, C_MK, C_MV, C_MO = 0, 256, 512, 768
C_SQ, C_RX, C_RY, C_NQ, C_NK, C_NV = 1024, 1280, 1536, 1792, 2048, 2304
C_SK, C_SV, C_G = 2560, 2688, 2816
P_W = 2944


def _cparams(sem):
    return pltpu.CompilerParams(dimension_semantics=sem, vmem_limit_bytes=VMEM_LIMIT)


def _dot(a, b):
    return jnp.dot(a.astype(BF16), b.astype(BF16), preferred_element_type=F32)


def _dot_nt(a, b):
    return lax.dot_general(a.astype(BF16), b.astype(BF16), (((1,), (1,)), ((), ())),
                           preferred_element_type=F32)


def _dot_tn(a, b):
    return lax.dot_general(a.astype(BF16), b.astype(BF16), (((0,), (0,)), ((), ())),
                           preferred_element_type=F32)


def _split3(x):
    hi = x.astype(BF16)
    r1 = x - hi.astype(F32)
    mid = r1.astype(BF16)
    lo = (r1 - mid.astype(F32)).astype(BF16)
    return hi, mid, lo


def _dot_exact_rhs(a01, x):
    hi, mid, lo = _split3(x)
    d = lambda p: jnp.dot(a01, p, preferred_element_type=F32)
    return d(hi) + d(mid) + d(lo)


def _dot_exact_lhs(x, a01):
    hi, mid, lo = _split3(x)
    d = lambda p: jnp.dot(p, a01, preferred_element_type=F32)
    return d(hi) + d(mid) + d(lo)


def _sigmoid(x):
    return 0.5 * jnp.tanh(0.5 * x) + 0.5


def _rms(x, g):
    return x * lax.rsqrt(jnp.mean(x * x, axis=-1, keepdims=True) + EPS) * g


def _rms_head_pairs(x, g):
    lane = lax.broadcasted_iota(jnp.int32, x.shape, 1)
    left = lane < HEAD_DIM
    sq = x * x
    s0 = jnp.sum(jnp.where(left, sq, 0.0), axis=-1, keepdims=True)
    s1 = jnp.sum(jnp.where(left, 0.0, sq), axis=-1, keepdims=True)
    ms = jnp.where(left, s0, s1) * (1.0 / HEAD_DIM)
    return x * lax.rsqrt(ms + EPS) * g


MOD_TN = 1536
MOD_ROWS = 3


def _mod_kernel(ct_ref, w_ref, b_ref, o_ref):
    ct = ct_ref[...]
    st = ct * jax.nn.sigmoid(ct)
    w = w_ref[0]
    o_ref[...] = jnp.zeros(o_ref.shape, F32)
    for r in range(MOD_ROWS):
        o_ref[0, r:r + 1, :] = jnp.sum(w * st[:, r:r + 1], axis=0, keepdims=True) + b_ref[0]


def _mod_table(cvecs_t, w_ada, b_ada):
    n = 6 * D_MODEL
    return pl.pallas_call(
        _mod_kernel,
        grid=(DEPTH, n // MOD_TN),
        in_specs=[pl.BlockSpec((D_MODEL, 8), lambda l, j: (0, 0)),
                  pl.BlockSpec((1, D_MODEL, MOD_TN), lambda l, j: (l, 0, j)),
                  pl.BlockSpec((1, 1, MOD_TN), lambda l, j: (l, 0, j))],
        out_specs=pl.BlockSpec((1, 8, MOD_TN), lambda l, j: (l, 0, j)),
        out_shape=jax.ShapeDtypeStruct((DEPTH, 8, n), F32),
        compiler_params=_cparams(("arbitrary", "arbitrary")),
        name="adaln_mod",
    )(cvecs_t, w_ada, b_ada.reshape(DEPTH, 1, n))


IN_TM = 256


def _swap16(y):
    lane = lax.broadcasted_iota(jnp.int32, y.shape, 1)
    first = (lane % 32) < 16
    return jnp.where(first, pltpu.roll(y, LANES - 16, 1), pltpu.roll(y, 16, 1))


def _seg_pair_specs(tm, width, lat_row0=0):
    nt = SEG // tm
    lat_off = lat_row0 // tm
    return (pl.BlockSpec((tm, width), lambda s, i: (jnp.minimum(s * nt + i, nt - 1), 0)),
            pl.BlockSpec((tm, width), lambda s, i: (lat_off + jnp.maximum(s * nt + i - nt, 0), 0)))


def _pick(c_ref, l_ref):
    return jnp.where(pl.program_id(0) == 0, c_ref[...], l_ref[...])


def _x_pair(x):
    return (x, 0) if isinstance(x, tuple) else ((x, x), SEG)


def _inproj_kernel(xc_ref, xl_ref, mod_ref, g_ref, w_ref, qkg_ref, cos_ref, sin_ref, o_ref):
    seg = pl.program_id(0)
    mod = mod_ref[0]
    sh1 = mod[:, 0:D_MODEL]
    sc1 = mod[:, D_MODEL:2 * D_MODEL]
    h = _rms(_pick(xc_ref, xl_ref), g_ref[...]) * (1.0 + sc1) + sh1
    r = jnp.dot(h.astype(BF16), w_ref[...], preferred_element_type=F32)
    o_ref[:, 0:C_SQ] = r[:, 0:C_SQ]
    o_ref[:, C_RX:C_NQ] = r[:, C_RX:C_NQ]
    o_ref[:, C_NV:C_SK] = r[:, C_NV:C_SK]
    o_ref[:, C_SV:P_W] = r[:, C_SV:P_W]
    cos = cos_ref[...]
    sin = sin_ref[...]
    latent = seg > 0

    def rope(y):
        return jnp.where(latent, y * cos + _swap16(y) * sin, y)

    for p in range(2):
        a = C_SQ + LANES * p
        o_ref[:, a:a + LANES] = rope(_rms_head_pairs(r[:, a:a + LANES], qkg_ref[0:1, :]))
    o_ref[:, C_SK:C_SK + LANES] = rope(_rms_head_pairs(r[:, C_SK:C_SK + LANES], qkg_ref[1:2, :]))
    for p in range(2):
        a = C_NQ + LANES * p
        o_ref[:, a:a + LANES] = _rms_head_pairs(r[:, a:a + LANES], qkg_ref[2:3, :])
        a = C_NK + LANES * p
        o_ref[:, a:a + LANES] = _rms_head_pairs(r[:, a:a + LANES], qkg_ref[3:4, :])


def _inproj(x, mod_l, norm1, w_in_p, qk_gains, cos_t, sin_t):
    nt = SEG // IN_TM
    x_pair, lat_row0 = _x_pair(x)
    return pl.pallas_call(
        _inproj_kernel,
        grid=(N_SEG, nt),
        in_specs=[*_seg_pair_specs(IN_TM, D_MODEL, lat_row0),
                  pl.BlockSpec((1, 1, 6 * D_MODEL), lambda s, i: (s, 0, 0)),
                  pl.BlockSpec((1, D_MODEL), lambda s, i: (0, 0)),
                  pl.BlockSpec((D_MODEL, P_W), lambda s, i: (0, 0)),
                  pl.BlockSpec((4, LANES), lambda s, i: (0, 0)),
                  pl.BlockSpec((IN_TM, LANES), lambda s, i: (i, 0)),
                  pl.BlockSpec((IN_TM, LANES), lambda s, i: (i, 0))],
        out_specs=pl.BlockSpec((IN_TM, P_W), lambda s, i: (s * nt + i, 0)),
        out_shape=jax.ShapeDtypeStruct((N_TOK, P_W), F32),
        compiler_params=_cparams(("arbitrary", "arbitrary")),
        name="inproj",
    )(*x_pair, mod_l, norm1, w_in_p, qk_gains, cos_t, sin_t)


def _mlstm_direction(d, q_ref, k_ref, v_ref, g_ref, bias, cbd, nbd, m_s):
    ch = ML_CHUNK
    r_io = lax.broadcasted_iota(jnp.int32, (ch, ch), 0)
    c_io = lax.broadcasted_iota(jnp.int32, (ch, ch), 1)
    lower = r_io >= c_io
    upper = r_io <= c_io
    mask = lower if d == 0 else upper
    tri = mask.astype(BF16)
    tri_t = (upper if d == 0 else lower).astype(BF16)
    left = c_io < HEAD_DIM
    top = r_io < HEAD_DIM
    blockdiag = top == left
    g = g_ref[...] + bias[...]
    b_cols = _dot_exact_rhs(tri, jax.nn.log_sigmoid(g))
    b3 = jnp.concatenate(_split3(b_cols), axis=1)
    gt = g.T
    li_rows = gt[0:8, :]
    b_rows = _dot_exact_lhs(jax.nn.log_sigmoid(gt[8:16, :]), tri_t)
    a_rows = li_rows - b_rows
    bl = b_rows[:, ch - 1:ch] if d == 0 else b_rows[:, 0:1]
    m_old = m_s[:, 0:1]
    g_rows = bl - b_rows + li_rows
    m_new = jnp.maximum(bl + m_old, jnp.max(g_rows, axis=1, keepdims=True))
    wk_rows = jnp.exp(g_rows - m_new)
    wp = jnp.exp(bl + m_old - m_new)
    sel_r = lax.broadcasted_iota(jnp.int32, (3 * LANES, LANES), 0) % LANES
    sel_left = lax.broadcasted_iota(jnp.int32, (3 * LANES, LANES), 1) < HEAD_DIM
    ones_blk = jnp.ones((ch, LANES), BF16)
    left2 = lax.broadcasted_iota(jnp.int32, (ch, 2 * LANES), 1) % LANES < HEAD_DIM
    h_out, c_out, n_out = [], [], []
    for p in range(ML_HEADS // 2):
        lanes = slice(LANES * p, LANES * (p + 1))
        j0 = ML_HEADS * d + 2 * p
        q2 = q_ref[:, lanes]
        k2t = (k_ref[:, lanes] * SCALE).T.astype(BF16)
        v2e = jnp.concatenate([v_ref[:, lanes].astype(BF16), ones_blk], axis=1)
        cb = cbd[2 * d + p]
        nb = nbd[2 * d + p]
        q2b = q2.astype(BF16)
        q_lo = (q2 - q2b.astype(F32)).astype(BF16)
        nb_hi = nb.astype(BF16)
        nb_lo = (nb - nb_hi.astype(F32)).astype(BF16)
        qc = jnp.dot(q2b, cb.astype(BF16), preferred_element_type=F32)
        qn = jnp.dot(jnp.concatenate([q2b, q_lo, q2b], axis=1), jnp.concatenate([nb_hi, nb_hi, nb_lo], axis=0),
                     preferred_element_type=F32)
        sel = (sel_r == jnp.where(sel_left, 8 + j0, 9 + j0)).astype(BF16)
        b_pair = jnp.dot(b3, sel, preferred_element_type=F32)
        cbs, sves = [], []
        for i in range(2):
            j = j0 + i
            half = left if i == 0 else jnp.logical_not(left)
            a_mat = jnp.where(mask, a_rows[j:j + 1, :], NEG)
            cvec = jnp.maximum(m_old[j:j + 1, :], jnp.max(a_mat, axis=1, keepdims=True))
            cbro = jnp.broadcast_to(cvec, (ch, ch))
            s = jnp.dot(jnp.where(half, q2b, 0), k2t, preferred_element_type=F32) * jnp.exp(a_mat - cbro)
            s_hi = s.astype(BF16)
            s_lo = (s - s_hi.astype(F32)).astype(BF16)
            sve = jnp.dot(s_hi, v2e, preferred_element_type=F32)
            rs_lo = jnp.dot(s_lo, ones_blk, preferred_element_type=F32)
            sves.append(jnp.concatenate([sve[:, 0:LANES], sve[:, LANES:2 * LANES] + rs_lo], axis=1))
            cbs.append(cbro)
        c_pair = jnp.where(left, cbs[0], cbs[1])
        w_prev = jnp.exp(jnp.where(left, m_old[j0:j0 + 1, :], m_old[j0 + 1:j0 + 2, :]) - c_pair)
        sve = jnp.where(left2, sves[0], sves[1])
        num = w_prev * qc + sve[:, 0:LANES]
        den = w_prev * qn + sve[:, LANES:2 * LANES]
        h_out.append(num / jnp.maximum(jnp.abs(den), jnp.exp(-(c_pair + b_pair))))
        kwt = k2t * jnp.where(top, wk_rows[j0:j0 + 1, :], wk_rows[j0 + 1:j0 + 2, :])
        kwt_hi = kwt.astype(BF16)
        kwt_lo = (kwt - kwt_hi.astype(F32)).astype(BF16)
        kve = jnp.dot(kwt_hi, v2e, preferred_element_type=F32)
        kn = kve[:, LANES:2 * LANES] + jnp.dot(kwt_lo, ones_blk, preferred_element_type=F32)
        wp_pair = jnp.where(top, wp[j0:j0 + 1, :], wp[j0 + 1:j0 + 2, :])
        c_out.append(wp_pair * cb + jnp.where(blockdiag, kve[:, 0:LANES], 0.0))
        n_out.append(wp_pair * nb + jnp.where(blockdiag, kn, 0.0))
    return h_out, c_out, n_out, m_new


def _mlstm_kernel(qf, kf, vf, gf, qb, kb, vb, gb, c0, n0, m0, bias,
                  hf, hb, co, no, mo, cbd, nbd, m_s, *, nc):
    c = pl.program_id(1)

    @pl.when(c == 0)
    def _():
        cbd[...] = c0[0]
        nbd[...] = n0[0]
        m_s[...] = m0[0]

    res = [_mlstm_direction(d, *refs, bias, cbd, nbd, m_s)
           for d, refs in enumerate(((qf, kf, vf, gf), (qb, kb, vb, gb)))]
    for d, (h_ref, (h_out, c_out, n_out, m_new)) in enumerate(zip((hf, hb), res)):
        for p in range(ML_HEADS // 2):
            h_ref[:, LANES * p:LANES * (p + 1)] = h_out[p]
            cbd[2 * d + p] = c_out[p]
            nbd[2 * d + p] = n_out[p]
        rows = slice(ML_HEADS * d, ML_HEADS * (d + 1))
        m_s[rows, :] = jnp.broadcast_to(m_new[rows, :], (ML_HEADS, LANES))

    @pl.when(c == nc - 1)
    def _():
        co[0] = cbd[...]
        no[0] = nbd[...]
        mo[0] = m_s[...]


def _mlstm_pack_state(c0, n0, m0):
    bsz = c0.shape[0]
    hd = HEAD_DIM
    eye = jnp.eye(2, dtype=F32)[None, None, :, None, :, None]
    cbd = c0.reshape(bsz, ML_HEADS, 2, hd, 1, hd) * eye
    nbd = jnp.broadcast_to(n0.reshape(bsz, ML_HEADS, 2, hd, 1, 1) * eye, cbd.shape)
    to_mat = lambda t: t.reshape(bsz, ML_HEADS, LANES, LANES)
    return to_mat(cbd), to_mat(nbd), jnp.broadcast_to(m0[..., None], m0.shape + (LANES,))


def _mlstm_unpack_state(cbd, nbd, mrow):
    hd = HEAD_DIM
    bsz = cbd.shape[0]
    c = jnp.stack([cbd[:, :, :hd, :hd], cbd[:, :, hd:, hd:]], axis=2).reshape(bsz, 2 * ML_HEADS, hd, hd)
    n = jnp.stack([nbd[:, :, :hd, 0], nbd[:, :, hd:, hd]], axis=2).reshape(bsz, 2 * ML_HEADS, hd)
    return c, n, mrow[:, :, 0]


def _mlstm(proj, gate_bias, c0, n0, m0, *, bsz, seq, row0):
    nc = seq // ML_CHUNK
    base = row0 // ML_CHUNK
    nj = 2 * ML_HEADS
    fw = lambda col: (lambda b, c: (base + b * nc + c, col))
    bw = lambda col: (lambda b, c: (base + b * nc + nc - 1 - c, col))
    qkv = lambda f: [pl.BlockSpec((ML_CHUNK, 256), f(C_MQ // 256)),
                     pl.BlockSpec((ML_CHUNK, 256), f(C_MK // 256)),
                     pl.BlockSpec((ML_CHUNK, 256), f(C_MV // 256)),
                     pl.BlockSpec((ML_CHUNK, LANES), f(C_G // LANES))]
    st_specs = [pl.BlockSpec((1, nj // 2, LANES, LANES), lambda b, c: (b, 0, 0, 0)),
                pl.BlockSpec((1, nj // 2, LANES, LANES), lambda b, c: (b, 0, 0, 0)),
                pl.BlockSpec((1, nj, LANES), lambda b, c: (b, 0, 0))]
    st_shapes = [jax.ShapeDtypeStruct((bsz, nj // 2, LANES, LANES), F32),
                 jax.ShapeDtypeStruct((bsz, nj // 2, LANES, LANES), F32),
                 jax.ShapeDtypeStruct((bsz, nj, LANES), F32)]
    return pl.pallas_call(
        functools.partial(_mlstm_kernel, nc=nc),
        grid=(bsz, nc),
        in_specs=qkv(fw) + qkv(bw) + st_specs + [pl.BlockSpec((1, LANES), lambda b, c: (0, 0))],
        out_specs=[pl.BlockSpec((ML_CHUNK, 256), lambda b, c: (b * nc + c, 0)),
                   pl.BlockSpec((ML_CHUNK, 256), lambda b, c: (b * nc + nc - 1 - c, 0))] + st_specs,
        out_shape=[jax.ShapeDtypeStruct((bsz * seq, 256), F32),
                   jax.ShapeDtypeStruct((bsz * seq, 256), F32)] + st_shapes,
        scratch_shapes=[pltpu.VMEM((nj // 2, LANES, LANES), F32),
                        pltpu.VMEM((nj // 2, LANES, LANES), F32),
                        pltpu.VMEM((nj, LANES), F32)],
        compiler_params=_cparams(("arbitrary", "arbitrary")),
        name="mlstm",
    )(proj, proj, proj, proj, proj, proj, proj, proj, c0, n0, m0, gate_bias)


RG_TC = 256
RG_PAD = 8
RG_NSEG = 8
RG_SKEW = 4


def _rglru_kernel(rx, ry, cw, cb, wg, bg, lam, h0, oc, hl, xpad, af, ab, uf, ub, *, seg_len, chained):
    rows = RG_NSEG * seg_len
    seq_len = rows if chained else seg_len
    halves = RG_WIDTH // LANES
    pitch = seg_len + RG_SKEW
    buf_row = lambda t: (t // seg_len) * pitch + t % seg_len
    xpad[0:RG_PAD, :] = jnp.zeros((RG_PAD, RG_WIDTH), F32)
    xpad[rows + RG_PAD:rows + 2 * RG_PAD, :] = jnp.zeros((RG_PAD, RG_WIDTH), F32)
    xpad[RG_PAD:rows + RG_PAD, :] = rx[...]
    sp = jax.nn.softplus(-lam[...])
    left = (RG_CONV - 1) // 2
    for ci in range(rows // RG_TC):
        s0 = ci * RG_TC
        pos = (s0 + lax.broadcasted_iota(jnp.int32, (RG_TC, RG_WIDTH), 0)) % seq_len
        xc = None
        for j in range(RG_CONV):
            a = RG_PAD + s0 + j - left
            term = xpad[a:a + RG_TC, :] * cw[j:j + 1, :]
            if not chained and j != left:
                term = jnp.where((pos + (j - left) >= 0) & (pos + (j - left) < seq_len), term, 0.0)
            xc = term if xc is None else xc + term
        xc = xc + cb[...]
        pre = _dot(xc, wg[...]) + bg[...]
        for d, (a_ref, u_ref) in enumerate(((af, uf), (ab, ub))):
            o = 2 * RG_WIDTH * d
            r = _sigmoid(pre[:, o:o + RG_WIDTH])
            gi = _sigmoid(pre[:, o + RG_WIDTH:o + 2 * RG_WIDTH])
            log_a = -RG_C * r * sp[d:d + 1, :]
            a_val = jnp.exp(log_a)
            u_val = jnp.sqrt(-jnp.tanh(log_a) * (a_val * a_val + 1.0)) * (gi * xc)
            for hv in range(halves):
                dst = slice(buf_row(s0), buf_row(s0) + RG_TC)
                a_ref[hv, dst, :] = a_val[:, LANES * hv:LANES * (hv + 1)]
                u_ref[hv, dst, :] = u_val[:, LANES * hv:LANES * (hv + 1)]

    def body(s, carry):
        out = []
        for d, (a_ref, u_ref) in enumerate(((af, uf), (ab, ub))):
            step_rows = pl.ds(s if d == 0 else seg_len - 1 - s, RG_NSEG, stride=pitch)
            for hv in range(halves):
                h_loc, prod = carry[2 * (halves * d + hv)], carry[2 * (halves * d + hv) + 1]
                a = a_ref[hv, step_rows, :]
                h_loc = a * h_loc + u_ref[hv, step_rows, :]
                prod = a * prod
                u_ref[hv, step_rows, :] = h_loc
                a_ref[hv, step_rows, :] = prod
                out += [h_loc, prod]
        return tuple(out)

    zero = jnp.zeros((RG_NSEG, LANES), F32)
    one = jnp.ones((RG_NSEG, LANES), F32)
    ends = lax.fori_loop(0, seg_len, body, (zero, one) * (2 * halves), unroll=8)
    seg = lax.broadcasted_iota(jnp.int32, (RG_NSEG, LANES), 0)
    for d, (a_ref, u_ref) in enumerate(((af, uf), (ab, ub))):
        for hv in range(halves):
            lanes = slice(LANES * hv, LANES * (hv + 1))
            h_end, p_end = ends[2 * (halves * d + hv)], ends[2 * (halves * d + hv) + 1]
            h_in = h0[0, d][:, lanes]
            if chained:
                c = h_in[0:1, :]
                h_in = zero
                for k in (range(RG_NSEG) if d == 0 else reversed(range(RG_NSEG))):
                    h_in = jnp.where(seg == k, c, h_in)
                    c = h_end[k:k + 1, :] + p_end[k:k + 1, :] * c
            hl[0, d, :, lanes] = h_end + p_end * h_in
            for k in range(RG_NSEG):
                for ci in range(seg_len // RG_TC):
                    sl = slice(k * pitch + ci * RG_TC, k * pitch + (ci + 1) * RG_TC)
                    u_ref[hv, sl, :] = u_ref[hv, sl, :] + a_ref[hv, sl, :] * h_in[k:k + 1, :]
    for ci in range(rows // RG_TC):
        sl = slice(ci * RG_TC, (ci + 1) * RG_TC)
        for hv in range(halves):
            lanes = slice(LANES * hv, LANES * (hv + 1))
            src = slice(buf_row(ci * RG_TC), buf_row(ci * RG_TC) + RG_TC)
            oc[sl, lanes] = (uf[hv, src, :] + ub[hv, src, :]) * jax.nn.gelu(ry[sl, lanes])


def _rglru(proj, cw, cb, wg, bg, lam, h0, *, nblk, seg_len, chained, row0):
    rows = RG_NSEG * seg_len
    base = row0 // rows
    full = lambda shape: pl.BlockSpec(shape, lambda b: tuple(0 for _ in shape))
    st_spec = pl.BlockSpec((1, 2, RG_NSEG, RG_WIDTH), lambda b: (b, 0, 0, 0))
    return pl.pallas_call(
        functools.partial(_rglru_kernel, seg_len=seg_len, chained=chained),
        grid=(nblk,),
        in_specs=[pl.BlockSpec((rows, RG_WIDTH), lambda b: (base + b, C_RX // RG_WIDTH)),
                  pl.BlockSpec((rows, RG_WIDTH), lambda b: (base + b, C_RY // RG_WIDTH)),
                  full((RG_CONV, RG_WIDTH)), full((1, RG_WIDTH)),
                  full((RG_WIDTH, 4 * RG_WIDTH)), full((1, 4 * RG_WIDTH)), full((2, RG_WIDTH)),
                  st_spec],
        out_specs=[pl.BlockSpec((rows, RG_WIDTH), lambda b: (b, 0)), st_spec],
        out_shape=[jax.ShapeDtypeStruct((nblk * rows, RG_WIDTH), F32),
                   jax.ShapeDtypeStruct((nblk, 2, RG_NSEG, RG_WIDTH), F32)],
        scratch_shapes=[pltpu.VMEM((rows + 2 * RG_PAD, RG_WIDTH), F32)]
        + [pltpu.VMEM((RG_WIDTH // LANES, RG_NSEG * (seg_len + RG_SKEW), LANES), F32) for _ in range(4)],
        compiler_params=_cparams(("arbitrary",)),
        name="rglru",
    )(proj, proj, cw, cb, wg, bg, lam, h0)


def _softmax_pv(scores, values, sink):
    m = functools.reduce(jnp.maximum, [jnp.max(s, axis=-1, keepdims=True) for s in scores])
    if sink is not None:
        m = jnp.maximum(m, sink)
    ps = [jnp.exp(s - m) for s in scores]
    den = functools.reduce(jnp.add, [jnp.sum(p, axis=-1, keepdims=True) for p in ps])
    if sink is not None:
        den = den + jnp.exp(sink - m)
    num = functools.reduce(jnp.add, [_dot(p, v) for p, v in zip(ps, values)])
    return num / den


def _ctx_attn_kernel(sink, sq, sk, sv, nq, nk, nv, ob, od):
    for h in range(SW_HEADS):
        j = h // (SW_HEADS // SW_KV_HEADS)
        sl = slice(HEAD_DIM * h, HEAD_DIM * (h + 1))
        kv = slice(HEAD_DIM * j, HEAD_DIM * (j + 1))
        s = _dot_nt(sq[:, sl], sk[:, kv]) * SCALE
        ob[:, sl] = _softmax_pv([s], [sv[:, kv]], sink[h])
    for h in range(NA_HEADS):
        sl = slice(HEAD_DIM * h, HEAD_DIM * (h + 1))
        s = _dot_nt(nq[:, sl], nk[:, sl]) * SCALE
        od[:, sl] = _softmax_pv([s], [nv[:, sl]], None)


def _ctx_attn(proj, sink):
    blk = lambda w, col: pl.BlockSpec((SEQ, w), lambda b: (b, col))
    return pl.pallas_call(
        _ctx_attn_kernel,
        grid=(BATCH,),
        in_specs=[pl.BlockSpec(memory_space=pltpu.SMEM),
                  blk(256, C_SQ // 256), blk(128, C_SK // 128), blk(128, C_SV // 128),
                  blk(256, C_NQ // 256), blk(256, C_NK // 256), blk(256, C_NV // 256)],
        out_specs=[pl.BlockSpec((SEQ, 256), lambda b: (b, 0)),
                   pl.BlockSpec((SEQ, 256), lambda b: (b, 0))],
        out_shape=[jax.ShapeDtypeStruct((SEG, 256), F32), jax.ShapeDtypeStruct((SEG, 256), F32)],
        compiler_params=_cparams(("arbitrary",)),
        name="ctx_attn",
    )(sink, proj, proj, proj, proj, proj, proj)


SW_QB = 128
SW_SPAN = SW_QB + 2 * SW_WINDOW


def _swa_kernel(sink, q, k, v, kc, vc, ob):
    assert SW_HEADS == 4 and SW_KV_HEADS == 2
    n = pl.program_id(1)
    ws = jnp.clip((n - 1) * SW_QB, 0, DEC_SEQ - SW_SPAN)
    ws = pl.multiple_of(ws, SW_QB)
    row = lax.broadcasted_iota(jnp.int32, (2 * SW_QB, SW_SPAN), 0)
    qpos = n * SW_QB + row % SW_QB
    kpos = ws + lax.broadcasted_iota(jnp.int32, (2 * SW_QB, SW_SPAN), 1)
    valid = jnp.abs(qpos - kpos) <= SW_WINDOW
    left = lax.broadcasted_iota(jnp.int32, (SW_QB, LANES), 1) < HEAD_DIM
    first = lax.broadcasted_iota(jnp.int32, (2 * SW_QB, 1), 0) < SW_QB
    k2 = k[pl.ds(ws, SW_SPAN), :].astype(BF16)
    v2 = v[pl.ds(ws, SW_SPAN), :].astype(BF16)
    kc2 = kc[0].astype(BF16)
    vc2 = vc[0].astype(BF16)
    for p in range(SW_KV_HEADS):
        lanes = slice(LANES * p, LANES * (p + 1))
        q2 = q[:, lanes]
        q_swapped = pltpu.roll(q2, HEAD_DIM, 1)
        kv_half = left if p == 0 else jnp.logical_not(left)
        qs = jnp.concatenate([jnp.where(kv_half, q2 if i == p else q_swapped, 0.0) for i in range(2)], axis=0)
        s_loc = jnp.where(valid, _dot_nt(qs, k2) * SCALE, NEG)
        s_ctx = _dot_nt(qs, kc2) * SCALE
        sink_col = jnp.where(first, sink[2 * p], sink[2 * p + 1])
        res = _softmax_pv([s_loc, s_ctx], [v2, vc2], sink_col)
        halves = [res[0:SW_QB], res[SW_QB:2 * SW_QB]]
        placed = [halves[i] if i == p else pltpu.roll(halves[i], HEAD_DIM, 1) for i in range(2)]
        ob[:, lanes] = jnp.where(left, placed[0], placed[1])


def _swa(proj, kc, vc, sink):
    nq = DEC_SEQ // SW_QB
    qbase = SEG // SW_QB
    return pl.pallas_call(
        _swa_kernel,
        grid=(DEC_BATCH, nq),
        in_specs=[pl.BlockSpec(memory_space=pltpu.SMEM),
                  pl.BlockSpec((SW_QB, 256), lambda b, n: (qbase + b * nq + n, C_SQ // 256)),
                  pl.BlockSpec((DEC_SEQ, 128), lambda b, n: (1 + b, C_SK // 128)),
                  pl.BlockSpec((DEC_SEQ, 128), lambda b, n: (1 + b, C_SV // 128)),
                  pl.BlockSpec((1, PAST_LEN, 128), lambda b, n: (b, 0, 0)),
                  pl.BlockSpec((1, PAST_LEN, 128), lambda b, n: (b, 0, 0))],
        out_specs=pl.BlockSpec((SW_QB, 256), lambda b, n: (b * nq + n, 0)),
        out_shape=jax.ShapeDtypeStruct((DEC_BATCH * DEC_SEQ, 256), F32),
        compiler_params=_cparams(("arbitrary", "arbitrary")),
        name="swa",
    )(sink, proj, proj, proj, kc, vc)


NA_RPB_R = 2 * NA_ROWS - 1
NA_RPB_C = 2 * NA_COLS - 1
GRID_ROWS = DEC_SEQ // GRID_W
NA_RB = 4
NA_UW = 12
NA_NQ = NA_RB * GRID_W
NA_NKEY = NA_UW * GRID_W
NA_NBLK = GRID_ROWS // NA_RB
NA_CASES = ((0, 0), (NA_RB, 0), (GRID_ROWS - NA_RB, GRID_ROWS - NA_UW))


def _na_row_start(qrow):
    return min(max(qrow - NA_ROWS // 2, 0), GRID_ROWS - NA_ROWS)


def _na_bias_kernel(rpb, out):
    h = pl.program_id(0)
    qc = lax.broadcasted_iota(jnp.int32, (GRID_W, GRID_W), 0)
    kc = lax.broadcasted_iota(jnp.int32, (GRID_W, GRID_W), 1)
    dc = jnp.clip(kc - qc, -(NA_COLS - 1), NA_COLS - 1) + NA_COLS - 1
    lo = jnp.clip(qc - NA_COLS // 2, 0, GRID_W - NA_COLS)
    valid = (kc >= lo) & (kc < lo + NA_COLS)
    tiles = []
    for dr in range(NA_RPB_R):
        t = jnp.zeros((GRID_W, GRID_W), F32)
        for j in range(NA_RPB_C):
            t = jnp.where(dc == j, rpb[(h * NA_RPB_R + dr) * NA_RPB_C + j], t)
        tiles.append(jnp.where(valid, t, NEG))
    outside = jnp.full((GRID_W, GRID_W), NEG, F32)
    for case, (q0, k0) in enumerate(NA_CASES):
        for a in range(NA_RB):
            r0 = _na_row_start(q0 + a)
            for i in range(NA_UW):
                inside = r0 <= k0 + i < r0 + NA_ROWS
                tile = tiles[k0 + i - (q0 + a) + NA_ROWS - 1] if inside else outside
                out[0, case, GRID_W * a:GRID_W * (a + 1), GRID_W * i:GRID_W * (i + 1)] = tile


def _na_bias(rpb_flat):
    shape = (NA_HEADS, len(NA_CASES), NA_NQ, NA_NKEY)
    return pl.pallas_call(
        _na_bias_kernel,
        grid=(NA_HEADS,),
        in_specs=[pl.BlockSpec(memory_space=pltpu.SMEM)],
        out_specs=pl.BlockSpec((1,) + shape[1:], lambda h: (h, 0, 0, 0)),
        out_shape=jax.ShapeDtypeStruct(shape, F32),
        compiler_params=_cparams(("arbitrary",)),
        name="na_bias",
    )(rpb_flat)


def _na_kernel(q, k, v, kc, vc, bias, od):
    blk = pl.program_id(1)
    case = jnp.where(blk == 0, 0, jnp.where(blk == NA_NBLK - 1, 2, 1))
    u0 = jnp.clip(blk * NA_RB - NA_ROWS // 2, 0, GRID_ROWS - NA_UW)
    k0 = pl.multiple_of(u0 * GRID_W, GRID_W)
    left = lax.broadcasted_iota(jnp.int32, (NA_NQ, LANES), 1) < HEAD_DIM
    for p in range(NA_HEADS // 2):
        lanes = slice(LANES * p, LANES * (p + 1))
        q2 = q[:, lanes]
        k2 = k[pl.ds(k0, NA_NKEY), lanes].astype(BF16)
        v2 = v[pl.ds(k0, NA_NKEY), lanes].astype(BF16)
        kc2 = kc[0, :, lanes].astype(BF16)
        vc2 = vc[0, :, lanes].astype(BF16)
        res = []
        for i in range(2):
            qm = jnp.where(left if i == 0 else jnp.logical_not(left), q2, 0.0)
            s_loc = _dot_nt(qm, k2) * SCALE + bias[2 * p + i, pl.ds(case, 1)][0]
            s_ctx = _dot_nt(qm, kc2) * SCALE
            res.append(_softmax_pv([s_loc, s_ctx], [v2, vc2], None))
        od[:, lanes] = jnp.where(left, res[0], res[1])


def _na(proj, kc, vc, bias):
    qbase = SEG // NA_NQ
    return pl.pallas_call(
        _na_kernel,
        grid=(DEC_BATCH, NA_NBLK),
        in_specs=[pl.BlockSpec((NA_NQ, 256), lambda b, r: (qbase + b * NA_NBLK + r, C_NQ // 256)),
                  pl.BlockSpec((DEC_SEQ, 256), lambda b, r: (1 + b, C_NK // 256)),
                  pl.BlockSpec((DEC_SEQ, 256), lambda b, r: (1 + b, C_NV // 256)),
                  pl.BlockSpec((1, PAST_LEN, 256), lambda b, r: (b, 0, 0)),
                  pl.BlockSpec((1, PAST_LEN, 256), lambda b, r: (b, 0, 0)),
                  pl.BlockSpec((NA_HEADS, len(NA_CASES), NA_NQ, NA_NKEY), lambda b, r: (0, 0, 0, 0))],
        out_specs=pl.BlockSpec((NA_NQ, 256), lambda b, r: (b * NA_NBLK + r, 0)),
        out_shape=jax.ShapeDtypeStruct((DEC_BATCH * DEC_SEQ, 256), F32),
        compiler_params=_cparams(("arbitrary", "arbitrary")),
        name="na",
    )(proj, proj, proj, kc, vc, bias)


MG_TM = 256


def _merge_kernel(xc_ref, xl_ref, mod_ref, g1_ref, g2_ref, hf_c, hf_l, hb_c, hb_l, mo, ob_c, ob_l, oc_c, oc_l,
                  od_c, od_l, hn, wmg, bmg, wbr, wout, *rest, moe):
    if moe:
        wrt, br, x1_ref, h2_ref, route_ref = rest
    else:
        x1_ref, h2_ref = rest
    pick = _pick
    mod = mod_ref[0]
    chunk = lambda i: mod[:, i * D_MODEL:(i + 1) * D_MODEL]
    sh1, sc1, gate1, sh2, sc2 = chunk(0), chunk(1), chunk(2), chunk(3), chunk(4)
    x = pick(xc_ref, xl_ref)
    h = (_rms(x, g1_ref[...]) * (1.0 + sc1) + sh1).astype(BF16)
    hsum = pick(hf_c, hf_l) + pick(hb_c, hb_l)
    out_a = jnp.concatenate(
        [_rms_head_pairs(hsum[:, LANES * p:LANES * (p + 1)], hn[...]) for p in range(2)], axis=-1)
    out_a = out_a * jax.nn.sigmoid(mo[...])
    acc = None
    for n, br_val in enumerate((out_a, pick(ob_c, ob_l), pick(oc_c, oc_l), pick(od_c, od_l))):
        gate = jax.nn.sigmoid(jnp.dot(h, wmg[:, n * D_MODEL:(n + 1) * D_MODEL], preferred_element_type=F32)
                              + bmg[:, n * D_MODEL:(n + 1) * D_MODEL])
        term = gate * jnp.dot(br_val.astype(BF16), wbr[n], preferred_element_type=F32)
        acc = term if acc is None else acc + term
    y = jnp.dot(acc.astype(BF16), wout[...], preferred_element_type=F32)
    x1 = x + gate1 * y
    x1_ref[...] = x1
    h2 = _rms(x1, g2_ref[...]) * (1.0 + sc2) + sh2
    h2_ref[...] = h2.astype(h2_ref.dtype)
    if moe:
        logit = [jnp.sum(h2 * wrt[e:e + 1, :], axis=-1, keepdims=True) + br[e] for e in range(N_EXPERTS)]
        v1, i1 = logit[0], jnp.zeros(logit[0].shape, jnp.int32)
        for e in range(1, N_EXPERTS):
            better = logit[e] > v1
            v1 = jnp.where(better, logit[e], v1)
            i1 = jnp.where(better, e, i1)
        v2, i2 = jnp.full(v1.shape, -jnp.inf, F32), jnp.zeros(v1.shape, jnp.int32)
        for e in range(N_EXPERTS):
            better = (i1 != e) & (logit[e] > v2)
            v2 = jnp.where(better, logit[e], v2)
            i2 = jnp.where(better, e, i2)
        e2 = jnp.exp(v2 - v1)
        den = 1.0 + e2
        lane = lax.broadcasted_iota(jnp.int32, route_ref.shape, 1)
        route = jnp.where(lane == 0, 1.0 / den, 0.0) + jnp.where(lane == 1, e2 / den, 0.0)
        route = route + jnp.where(lane == 2, i1.astype(F32), 0.0) + jnp.where(lane == 3, i2.astype(F32), 0.0)
        route_ref[...] = route


def _merge(x, mod_l, g1, g2, hf, hb, proj, ob, oc, od, hn, wmg, bmg, wbr, wout, router=None):
    nt = SEG // MG_TM
    moe = router is not None
    x_pair, lat_row0 = _x_pair(x)
    row = lambda w: pl.BlockSpec((MG_TM, w), lambda s, i: (s * nt + i, 0))
    ctx_blk, lat_blk = _seg_pair_specs(MG_TM, 256)
    full = lambda shape: pl.BlockSpec(shape, lambda s, i: tuple(0 for _ in shape))
    in_specs = [*_seg_pair_specs(MG_TM, D_MODEL, lat_row0),
                pl.BlockSpec((1, 1, 6 * D_MODEL), lambda s, i: (s, 0, 0)),
                full((1, D_MODEL)), full((1, D_MODEL)),
                ctx_blk, lat_blk, ctx_blk, lat_blk,
                pl.BlockSpec((MG_TM, 256), lambda s, i: (s * nt + i, C_MO // 256)),
                ctx_blk, lat_blk, ctx_blk, lat_blk, ctx_blk, lat_blk,
                full((1, LANES)), full((D_MODEL, N_BRANCH * D_MODEL)), full((1, N_BRANCH * D_MODEL)),
                full((N_BRANCH, 256, D_MODEL)), full((D_MODEL, D_MODEL))]
    args = [*x_pair, mod_l, g1, g2, *hf, *hb, proj, *ob, *oc, *od, hn, wmg, bmg, wbr, wout]
    out_specs = [row(D_MODEL), row(D_MODEL)]
    out_shape = [jax.ShapeDtypeStruct((N_TOK, D_MODEL), F32),
                 jax.ShapeDtypeStruct((N_TOK, D_MODEL), F32 if moe else BF16)]
    if moe:
        in_specs += [full((N_EXPERTS, D_MODEL)), pl.BlockSpec(memory_space=pltpu.SMEM)]
        args += list(router)
        out_specs.append(row(LANES))
        out_shape.append(jax.ShapeDtypeStruct((N_TOK, LANES), F32))
    return pl.pallas_call(
        functools.partial(_merge_kernel, moe=moe),
        grid=(N_SEG, nt),
        in_specs=in_specs,
        out_specs=out_specs,
        out_shape=out_shape,
        compiler_params=_cparams(("arbitrary", "arbitrary")),
        name="merge",
    )(*args)


FF_TM = 512
FF_TF = 1408


def _ffn_kernel(h2, w1, w3, w2, x1, mod_ref, out, acc):
    f = pl.program_id(1)

    @pl.when(f == 0)
    def _():
        acc[...] = jnp.zeros(acc.shape, F32)

    h = h2[...]
    a = jnp.dot(h, w1[...], preferred_element_type=F32)
    b = jnp.dot(h, w3[...], preferred_element_type=F32)
    act = (jax.nn.silu(a) * b).astype(BF16)
    acc[...] += jnp.dot(act, w2[...], preferred_element_type=F32)

    @pl.when(f == pl.num_programs(1) - 1)
    def _():
        gate2 = mod_ref[0][:, 5 * D_MODEL:6 * D_MODEL]
        out[...] = x1[...] + gate2 * acc[...]


def _ffn(h2, x1, mod_l, w1, w3, w2):
    nt = N_TOK // FF_TM
    per_seg = SEG // FF_TM
    return pl.pallas_call(
        _ffn_kernel,
        grid=(nt, D_FF // FF_TF),
        in_specs=[pl.BlockSpec((FF_TM, D_MODEL), lambda i, f: (i, 0)),
                  pl.BlockSpec((D_MODEL, FF_TF), lambda i, f: (0, f)),
                  pl.BlockSpec((D_MODEL, FF_TF), lambda i, f: (0, f)),
                  pl.BlockSpec((FF_TF, D_MODEL), lambda i, f: (f, 0)),
                  pl.BlockSpec((FF_TM, D_MODEL), lambda i, f: (i, 0)),
                  pl.BlockSpec((1, 1, 6 * D_MODEL), lambda i, f: (i // per_seg, 0, 0))],
        out_specs=pl.BlockSpec((FF_TM, D_MODEL), lambda i, f: (i, 0)),
        out_shape=jax.ShapeDtypeStruct((N_TOK, D_MODEL), F32),
        scratch_shapes=[pltpu.VMEM((FF_TM, D_MODEL), F32)],
        compiler_params=_cparams(("arbitrary", "arbitrary")),
        name="ffn",
    )(h2, w1, w3, w2, x1, mod_l)


MOE_TM = 256
MOE_SLOTS = 2 * N_TOK
MOE_TILES = MOE_SLOTS // MOE_TM + N_EXPERTS
MOE_NBUF = 3
MOE_STEPS = MOE_TILES + MOE_NBUF
MOE_DUMP = MOE_NBUF * MOE_TM
MOE_LEAD = 1
MOE_PLAN_TILES = MOE_LEAD + MOE_TILES + 2
MOE_FCHUNKS = 1
MOE_UNROLL = 8


def _moe_group_kernel(texp, nused, src_tok, dst_row, h2_hbm, w1, w3, w2, y_hbm, xs, ys, sem_in, sem_out):
    del texp
    i = pl.program_id(0)
    n_used = nused[0]
    buf = i % MOE_NBUF
    buf_next = (i + 2) % MOE_NBUF

    def gather_copy(tile, b, r):
        tok = src_tok[(tile + MOE_LEAD) * MOE_TM + r]
        return pltpu.make_async_copy(h2_hbm.at[pl.ds(tok, 1)], xs.at[b, pl.ds(r, 1)], sem_in.at[b])

    def scatter_copy(tile, b, r):
        dst = dst_row[(tile + MOE_LEAD) * MOE_TM + r]
        return pltpu.make_async_copy(ys.at[b, pl.ds(r, 1)], y_hbm.at[pl.ds(dst, 1)], sem_out.at[b])

    def start_rows_loop(make_copy, tile, b):
        def body(r, carry):
            make_copy(tile, b, r).start()
            return carry
        lax.fori_loop(0, MOE_TM, body, 0, unroll=MOE_UNROLL)

    def wait_tile(b, gather):
        if gather:
            pltpu.make_async_copy(h2_hbm.at[pl.ds(0, MOE_TM)], xs.at[b], sem_in.at[b]).wait()
        else:
            pltpu.make_async_copy(ys.at[b], y_hbm.at[pl.ds(0, MOE_TM)], sem_out.at[b]).wait()

    @pl.when(i == 0)
    def _():
        xs[...] = jnp.zeros(xs.shape, F32)
        ys[...] = jnp.zeros(ys.shape, F32)
        for b in range(MOE_NBUF):
            fill = pltpu.make_async_copy(ys.at[b], y_hbm.at[pl.ds(MOE_SLOTS + b * MOE_TM, MOE_TM)], sem_out.at[b])
            fill.start()
            fill.wait()
        start_rows_loop(gather_copy, 0, 0)
        start_rows_loop(gather_copy, 1, 1)

    @pl.when(i <= n_used + 1)
    def _():
        wait_tile(buf, True)

    @pl.when((i >= 2) & (i <= n_used + 2))
    def _():
        wait_tile(buf, False)

    @pl.when(i < n_used)
    def _():
        x = xs[buf].astype(BF16)
        fc = D_FF_EXPERT // MOE_FCHUNKS
        rc = MOE_TM // MOE_FCHUNKS
        y = None
        for c in range(MOE_FCHUNKS):
            for r in range(c * rc, (c + 1) * rc):
                gather_copy(i + 2, buf_next, r).start()
                scatter_copy(i - 1, buf_next, r).start()
            a = jnp.dot(x, w1[0, :, c * fc:(c + 1) * fc], preferred_element_type=F32)
            b = jnp.dot(x, w3[0, :, c * fc:(c + 1) * fc], preferred_element_type=F32)
            act = (jax.nn.silu(a) * b).astype(BF16)
            part = jnp.dot(act, w2[0, c * fc:(c + 1) * fc, :], preferred_element_type=F32)
            y = part if y is None else y + part
        ys[buf] = y

    @pl.when(i == n_used)
    def _():
        start_rows_loop(scatter_copy, i - 1, buf_next)


def _moe_group(tile_expert, n_used, src_tok, dst_row, h2, w1, w3, w2):
    wspec = lambda shape: pl.BlockSpec((1,) + shape, lambda i, texp, *_: (texp[jnp.minimum(i, MOE_TILES - 1)], 0, 0))
    grid_spec = pltpu.PrefetchScalarGridSpec(
        num_scalar_prefetch=4,
        grid=(MOE_STEPS,),
        in_specs=[pl.BlockSpec(memory_space=pl.ANY),
                  wspec((D_MODEL, D_FF_EXPERT)), wspec((D_MODEL, D_FF_EXPERT)), wspec((D_FF_EXPERT, D_MODEL))],
        out_specs=pl.BlockSpec(memory_space=pl.ANY),
        scratch_shapes=[pltpu.VMEM((MOE_NBUF, MOE_TM, D_MODEL), F32), pltpu.VMEM((MOE_NBUF, MOE_TM, D_MODEL), F32),
                        pltpu.SemaphoreType.DMA((MOE_NBUF,)), pltpu.SemaphoreType.DMA((MOE_NBUF,))])
    return pl.pallas_call(
        _moe_group_kernel,
        grid_spec=grid_spec,
        out_shape=jax.ShapeDtypeStruct((MOE_SLOTS + MOE_DUMP, D_MODEL), F32),
        compiler_params=_cparams(("arbitrary",)),
        name="moe_group",
    )(tile_expert, n_used, src_tok, dst_row, h2, w1, w3, w2)


def _moe_plan(expert_ids):
    e_flat = expert_ids.T.reshape(-1)
    order = jnp.argsort(e_flat, stable=True).astype(jnp.int32)
    counts = jnp.sum((e_flat[:, None] == jnp.arange(N_EXPERTS)[None, :]).astype(jnp.int32), axis=0)
    padded = (counts + MOE_TM - 1) // MOE_TM * MOE_TM
    pend = jnp.cumsum(padded)
    pstart = pend - padded
    ustart = jnp.cumsum(counts) - counts
    n_used = pend[-1] // MOE_TM
    tiles = jnp.arange(MOE_TILES, dtype=jnp.int32)
    last_used = jnp.minimum(tiles, n_used - 1)
    tile_expert = jnp.sum((last_used[:, None] * MOE_TM >= pend[None, :]).astype(jnp.int32), axis=1)
    t = jnp.arange(-MOE_LEAD, MOE_PLAN_TILES - MOE_LEAD, dtype=jnp.int32)[:, None]
    r = jnp.arange(MOE_TM, dtype=jnp.int32)[None, :]
    e_t = tile_expert[jnp.clip(t, 0, MOE_TILES - 1)]
    off = t * MOE_TM + r - pstart[e_t]
    valid = (t >= 0) & (t < n_used) & (off < counts[e_t])
    slot = order[jnp.clip(ustart[e_t] + off, 0, MOE_SLOTS - 1)]
    src_tok = jnp.where(valid, slot % N_TOK, 0)
    dst_row = jnp.where(valid, slot, MOE_SLOTS + (t % MOE_NBUF) * MOE_TM + r)
    return (tile_expert.astype(jnp.int32), n_used.reshape(1).astype(jnp.int32),
            src_tok.reshape(-1).astype(jnp.int32), dst_row.reshape(-1).astype(jnp.int32))


def _moe_combine_kernel(x1, y0, y1, route, mod_ref, out_c, out_l):
    gate2 = mod_ref[0][:, 5 * D_MODEL:6 * D_MODEL]
    r = route[...]
    val = x1[...] + gate2 * (r[:, 0:1] * y0[...] + r[:, 1:2] * y1[...])
    is_ctx = pl.program_id(0) < SEG // FF_TM

    @pl.when(is_ctx)
    def _():
        out_c[...] = val

    @pl.when(jnp.logical_not(is_ctx))
    def _():
        out_l[...] = val


def _moe_combine(x1, y_slots, route, mod_l):
    nt = N_TOK // FF_TM
    per_seg = SEG // FF_TM
    return pl.pallas_call(
        _moe_combine_kernel,
        grid=(nt,),
        in_specs=[pl.BlockSpec((FF_TM, D_MODEL), lambda i: (i, 0)),
                  pl.BlockSpec((FF_TM, D_MODEL), lambda i: (i, 0)),
                  pl.BlockSpec((FF_TM, D_MODEL), lambda i: (nt + i, 0)),
                  pl.BlockSpec((FF_TM, LANES), lambda i: (i, 0)),
                  pl.BlockSpec((1, 1, 6 * D_MODEL), lambda i: (i // per_seg, 0, 0))],
        out_specs=[pl.BlockSpec((FF_TM, D_MODEL), lambda i: (jnp.minimum(i, per_seg - 1), 0)),
                   pl.BlockSpec((FF_TM, D_MODEL), lambda i: (jnp.maximum(i - per_seg, 0), 0))],
        out_shape=[jax.ShapeDtypeStruct((SEG, D_MODEL), F32), jax.ShapeDtypeStruct((N_TOK - SEG, D_MODEL), F32)],
        compiler_params=_cparams(("arbitrary",)),
        name="moe_combine",
    )(x1, y_slots, y_slots, route, mod_l)


def _rope_tables():
    t = jnp.arange(DEC_SEQ)
    row, col = (t // GRID_W).astype(F32), (t % GRID_W).astype(F32)
    nf = HEAD_DIM // 4
    freqs = ROPE_BASE ** (-jnp.arange(nf, dtype=F32) / nf)
    lane = np.arange(LANES) % HEAD_DIM
    fidx = lane % nf
    use_col = (lane // (HEAD_DIM // 2)) == 1
    first = (lane % (HEAD_DIM // 2)) < nf
    pos = jnp.where(use_col[None, :], col[:, None], row[:, None])
    ang = pos * freqs[fidx][None, :]
    sin = jnp.sin(ang)
    return jnp.cos(ang), jnp.where(first[None, :], -sin, sin)


def _permute_w_in(w):
    sizes = (256, 256, 256, 256, 8, 8, 256, 128, 128, 256, 256, 256, 256, 256)
    offs = np.concatenate([[0], np.cumsum(sizes)])
    part = lambda i: w[:, offs[i]:offs[i + 1]]
    mq, mk, mv, mo, mi, mf, sq, sk, sv, rx, ry, nq, nk, nv = (part(i) for i in range(14))
    pad = jnp.zeros((w.shape[0], LANES - 16), w.dtype)
    return jnp.concatenate([mq, mk, mv, mo, sq, rx, ry, nq, nk, nv, sk, sv, mi, mf, pad], axis=1)


def _block_diag(w):
    eye = jnp.eye(RG_BLOCKS, dtype=w.dtype)
    return (w[:, :, None, :] * eye[:, None, :, None]).reshape(RG_WIDTH, RG_WIDTH)


def _tile2(g):
    return jnp.concatenate([g, g]).reshape(1, LANES)


def kernel(x_prompt, x_sample, cache_swa_k, cache_swa_v, cache_na_k, cache_na_v, state_mlstm_C, state_mlstm_n, state_mlstm_m, state_rglru_h, c, c_ctx, norm1_g, norm2_g, w_ada, b_ada, w_in, ml_b_i, ml_b_f, ml_hn, sw_qn, sw_kn, sw_sink, rg_conv_w, rg_conv_b, rg_w_r, rg_b_r, rg_w_i, rg_b_i, rg_lam, na_qn, na_kn, na_rpb, w_br, w_mg, b_mg, w_out, ffn_w1, ffn_w3, ffn_w2, moe_wr, moe_br, moe_w1, moe_w3, moe_w2):
    assert DEPTH % 2 == 0
    x_all = (x_prompt.reshape(SEG, D_MODEL), x_sample.reshape(N_TOK - SEG, D_MODEL))
    cvecs = jnp.concatenate([c_ctx[None, :], c, jnp.zeros((8 - 1 - DEC_BATCH, D_MODEL), F32)], axis=0)
    mod = _mod_table(cvecs.T, w_ada, b_ada)
    cos_t, sin_t = _rope_tables()
    nj = 2 * ML_HEADS
    zeros_state = (jnp.zeros((BATCH, nj // 2, LANES, LANES), F32), jnp.zeros((BATCH, nj // 2, LANES, LANES), F32),
                   jnp.zeros((BATCH, nj, LANES), F32), jnp.zeros((BATCH // RG_NSEG, 2, RG_NSEG, RG_WIDTH), F32))
    ctx_out = []
    for l in range(DEPTH):
        mod_l = mod[l].reshape(8, 1, 6 * D_MODEL)
        qk_gains = jnp.stack([_tile2(sw_qn[l])[0], _tile2(sw_kn[l])[0], _tile2(na_qn[l])[0], _tile2(na_kn[l])[0]])
        proj = _inproj(x_all, mod_l, norm1_g[l].reshape(1, D_MODEL), _permute_w_in(w_in[l]).astype(BF16),
                       qk_gains, cos_t, sin_t)
        gate_bias = jnp.concatenate([ml_b_i[l].reshape(-1), ml_b_f[l].reshape(-1),
                                     jnp.zeros((LANES - 2 * nj,), F32)]).reshape(1, LANES)
        hf_c, hb_c, *st_new = _mlstm(proj, gate_bias, *zeros_state[:3], bsz=BATCH, seq=SEQ, row0=0)
        c_new, n_new, m_new = _mlstm_unpack_state(*st_new)
        st_lat = _mlstm_pack_state(state_mlstm_C[:, l].reshape(DEC_BATCH, nj, HEAD_DIM, HEAD_DIM),
                                   state_mlstm_n[:, l].reshape(DEC_BATCH, nj, HEAD_DIM),
                                   state_mlstm_m[:, l].reshape(DEC_BATCH, nj))
        hf_l, hb_l, _, _, _ = _mlstm(proj, gate_bias, *st_lat, bsz=DEC_BATCH, seq=DEC_SEQ, row0=SEG)
        wg = jnp.concatenate([_block_diag(rg_w_r[l, 0]), _block_diag(rg_w_i[l, 0]),
                              _block_diag(rg_w_r[l, 1]), _block_diag(rg_w_i[l, 1])], axis=1).astype(BF16)
        bg = jnp.concatenate([rg_b_r[l, 0], rg_b_i[l, 0], rg_b_r[l, 1], rg_b_i[l, 1]]).reshape(1, 4 * RG_WIDTH)
        rg_args = (rg_conv_w[l], rg_conv_b[l].reshape(1, RG_WIDTH), wg, bg, rg_lam[l])
        oc_c, hl_c = _rglru(proj, *rg_args, zeros_state[3], nblk=BATCH // RG_NSEG, seg_len=SEQ, chained=False, row0=0)
        hl_new = jnp.transpose(hl_c, (0, 2, 1, 3)).reshape(BATCH, 2, RG_WIDTH)
        h0_lat = jnp.broadcast_to(state_rglru_h[:, l][:, :, None, :], (DEC_BATCH, 2, RG_NSEG, RG_WIDTH))
        oc_l, _ = _rglru(proj, *rg_args, h0_lat, nblk=DEC_BATCH, seg_len=DEC_SEQ // RG_NSEG, chained=True, row0=SEG)
        ob_c, od_c = _ctx_attn(proj, sw_sink[l])
        ob_l = _swa(proj, cache_swa_k[:, l].reshape(DEC_BATCH, PAST_LEN, 128),
                    cache_swa_v[:, l].reshape(DEC_BATCH, PAST_LEN, 128), sw_sink[l])
        od_l = _na(proj, cache_na_k[:, l].reshape(DEC_BATCH, PAST_LEN, 256),
                   cache_na_v[:, l].reshape(DEC_BATCH, PAST_LEN, 256), _na_bias(na_rpb[l].reshape(-1)))
        moe_layer = l % 2 == 1
        j = l // 2
        router = (moe_wr[j].T, moe_br[j]) if moe_layer else None
        outs = _merge(x_all, mod_l, norm1_g[l].reshape(1, D_MODEL), norm2_g[l].reshape(1, D_MODEL),
                      (hf_c, hf_l), (hb_c, hb_l), proj, (ob_c, ob_l), (oc_c, oc_l), (od_c, od_l),
                      _tile2(ml_hn[l]), w_mg[l].astype(BF16), b_mg[l].reshape(1, -1), w_br[l].astype(BF16),
                      w_out[l].astype(BF16), router)
        if moe_layer:
            x1, h2, route = outs
            plan = _moe_plan(route[:, 2:4].astype(jnp.int32))
            y_slots = _moe_group(*plan, h2, moe_w1[j].astype(BF16), moe_w3[j].astype(BF16), moe_w2[j].astype(BF16))
            x_all = tuple(_moe_combine(x1, y_slots, route, mod_l))
        else:
            x1, h2 = outs
            x_all = _ffn(h2, x1, mod_l, ffn_w1[j].astype(BF16), ffn_w3[j].astype(BF16), ffn_w2[j].astype(BF16))
        pc = proj[:SEG]
        ctx_out.append(dict(
            sw_k=pc[:, C_SK:C_SK + 128].reshape(BATCH, SEQ, SW_KV_HEADS, HEAD_DIM),
            sw_v=pc[:, C_SV:C_SV + 128].reshape(BATCH, SEQ, SW_KV_HEADS, HEAD_DIM),
            na_k=pc[:, C_NK:C_NK + 256].reshape(BATCH, SEQ, NA_HEADS, HEAD_DIM),
            na_v=pc[:, C_NV:C_NV + 256].reshape(BATCH, SEQ, NA_HEADS, HEAD_DIM),
            ml_C=c_new.reshape(BATCH, 2, ML_HEADS, HEAD_DIM, HEAD_DIM),
            ml_n=n_new.reshape(BATCH, 2, ML_HEADS, HEAD_DIM),
            ml_m=m_new.reshape(BATCH, 2, ML_HEADS),
            rg_h=hl_new))
    stack = lambda name: jnp.stack([t[name] for t in ctx_out], axis=1)
    return (x_all[0].reshape(BATCH, SEQ, D_MODEL), x_all[1].reshape(DEC_BATCH, DEC_SEQ, D_MODEL),
            stack('sw_k'), stack('sw_v'), stack('na_k'), stack('na_v'),
            stack('ml_C'), stack('ml_n'), stack('ml_m'), stack('rg_h'))
```

```python
import functools

import numpy as np
import jax
import jax.numpy as jnp
from jax import lax
from jax.experimental import pallas as pl
from jax.experimental.pallas import tpu as pltpu

F32 = jnp.float32
BF16 = jnp.bfloat16

D_MODEL = 1024
BATCH = 16
SEQ = 256
DEPTH = 2
DEC_BATCH = 2
DEC_SEQ = 4096
PAST_LEN = 256
GRID_W = 64
HEAD_DIM = 64
ML_HEADS = 4
ML_CHUNK = 128
SW_HEADS = 4
SW_KV_HEADS = 2
SW_WINDOW = 128
RG_WIDTH = 256
RG_BLOCKS = 4
RG_CONV = 4
RG_C = 8.0
NA_HEADS = 4
NA_ROWS = 8
NA_COLS = 16
N_BRANCH = 4
ROPE_BASE = 10000.0
D_FF = 2816
N_EXPERTS = 8
D_FF_EXPERT = 2048
EPS = 1e-6
NEG = -1e30
SCALE = HEAD_DIM ** -0.5

SEG = 4096
N_SEG = 3
N_TOK = N_SEG * SEG
LANES = 128
VMEM_LIMIT = 56 * 1024 * 1024

C_MQ, C_MK, C_MV, C_MO = 0, 256, 512, 768
C_SQ, C_RX, C_RY, C_NQ, C_NK, C_NV = 1024, 1280, 1536, 1792, 2048, 2304
C_SK, C_SV, C_G = 2560, 2688, 2816
P_W = 2944


def _cparams(sem):
    return pltpu.CompilerParams(dimension_semantics=sem, vmem_limit_bytes=VMEM_LIMIT)


def _dot(a, b):
    return jnp.dot(a.astype(BF16), b.astype(BF16), preferred_element_type=F32)


def _dot_nt(a, b):
    return lax.dot_general(a.astype(BF16), b.astype(BF16), (((1,), (1,)), ((), ())),
                           preferred_element_type=F32)


def _dot_tn(a, b):
    return lax.dot_general(a.astype(BF16), b.astype(BF16), (((0,), (0,)), ((), ())),
                           preferred_element_type=F32)


def _split3(x):
    hi = x.astype(BF16)
    r1 = x - hi.astype(F32)
    mid = r1.astype(BF16)
    lo = (r1 - mid.astype(F32)).astype(BF16)
    return hi, mid, lo


def _dot_exact_rhs(a01, x):
    hi, mid, lo = _split3(x)
    d = lambda p: jnp.dot(a01, p, preferred_element_type=F32)
    return d(hi) + d(mid) + d(lo)


def _dot_exact_lhs(x, a01):
    hi, mid, lo = _split3(x)
    d = lambda p: jnp.dot(p, a01, preferred_element_type=F32)
    return d(hi) + d(mid) + d(lo)


def _sigmoid(x):
    return 0.5 * jnp.tanh(0.5 * x) + 0.5


def _rms(x, g):
    return x * lax.rsqrt(jnp.mean(x * x, axis=-1, keepdims=True) + EPS) * g


def _rms_head_pairs(x, g):
    lane = lax.broadcasted_iota(jnp.int32, x.shape, 1)
    left = lane < HEAD_DIM
    sq = x * x
    s0 = jnp.sum(jnp.where(left, sq, 0.0), axis=-1, keepdims=True)
    s1 = jnp.sum(jnp.where(left, 0.0, sq), axis=-1, keepdims=True)
    ms = jnp.where(left, s0, s1) * (1.0 / HEAD_DIM)
    return x * lax.rsqrt(ms + EPS) * g


MOD_TN = 1536
MOD_ROWS = 3


def _mod_kernel(ct_ref, w_ref, b_ref, o_ref):
    ct = ct_ref[...]
    st = ct * jax.nn.sigmoid(ct)
    w = w_ref[0]
    o_ref[...] = jnp.zeros(o_ref.shape, F32)
    for r in range(MOD_ROWS):
        o_ref[0, r:r + 1, :] = jnp.sum(w * st[:, r:r + 1], axis=0, keepdims=True) + b_ref[0]


def _mod_table(cvecs_t, w_ada, b_ada):
    n = 6 * D_MODEL
    return pl.pallas_call(
        _mod_kernel,
        grid=(DEPTH, n // MOD_TN),
        in_specs=[pl.BlockSpec((D_MODEL, 8), lambda l, j: (0, 0)),
                  pl.BlockSpec((1, D_MODEL, MOD_TN), lambda l, j: (l, 0, j)),
                  pl.BlockSpec((1, 1, MOD_TN), lambda l, j: (l, 0, j))],
        out_specs=pl.BlockSpec((1, 8, MOD_TN), lambda l, j: (l, 0, j)),
        out_shape=jax.ShapeDtypeStruct((DEPTH, 8, n), F32),
        compiler_params=_cparams(("arbitrary", "arbitrary")),
        name="adaln_mod",
    )(cvecs_t, w_ada, b_ada.reshape(DEPTH, 1, n))


IN_TM = 256


def _swap16(y):
    lane = lax.broadcasted_iota(jnp.int32, y.shape, 1)
    first = (lane % 32) < 16
    return jnp.where(first, pltpu.roll(y, LANES - 16, 1), pltpu.roll(y, 16, 1))


def _seg_pair_specs(tm, width, lat_row0=0):
    nt = SEG // tm
    lat_off = lat_row0 // tm
    return (pl.BlockSpec((tm, width), lambda s, i: (jnp.minimum(s * nt + i, nt - 1), 0)),
            pl.BlockSpec((tm, width), lambda s, i: (lat_off + jnp.maximum(s * nt + i - nt, 0), 0)))


def _pick(c_ref, l_ref):
    return jnp.where(pl.program_id(0) == 0, c_ref[...], l_ref[...])


def _x_pair(x):
    return (x, 0) if isinstance(x, tuple) else ((x, x), SEG)


def _inproj_kernel(xc_ref, xl_ref, mod_ref, g_ref, w_ref, qkg_ref, cos_ref, sin_ref, o_ref):
    seg = pl.program_id(0)
    mod = mod_ref[0]
    sh1 = mod[:, 0:D_MODEL]
    sc1 = mod[:, D_MODEL:2 * D_MODEL]
    h = _rms(_pick(xc_ref, xl_ref), g_ref[...]) * (1.0 + sc1) + sh1
    r = jnp.dot(h.astype(BF16), w_ref[...], preferred_element_type=F32)
    o_ref[:, 0:C_SQ] = r[:, 0:C_SQ]
    o_ref[:, C_RX:C_NQ] = r[:, C_RX:C_NQ]
    o_ref[:, C_NV:C_SK] = r[:, C_NV:C_SK]
    o_ref[:, C_SV:P_W] = r[:, C_SV:P_W]
    cos = cos_ref[...]
    sin = sin_ref[...]
    latent = seg > 0

    def rope(y):
        return jnp.where(latent, y * cos + _swap16(y) * sin, y)

    for p in range(2):
        a = C_SQ + LANES * p
        o_ref[:, a:a + LANES] = rope(_rms_head_pairs(r[:, a:a + LANES], qkg_ref[0:1, :]))
    o_ref[:, C_SK:C_SK + LANES] = rope(_rms_head_pairs(r[:, C_SK:C_SK + LANES], qkg_ref[1:2, :]))
    for p in range(2):
        a = C_NQ + LANES * p
        o_ref[:, a:a + LANES] = _rms_head_pairs(r[:, a:a + LANES], qkg_ref[2:3, :])
        a = C_NK + LANES * p
        o_ref[:, a:a + LANES] = _rms_head_pairs(r[:, a:a + LANES], qkg_ref[3:4, :])


def _inproj(x, mod_l, norm1, w_in_p, qk_gains, cos_t, sin_t):
    nt = SEG // IN_TM
    x_pair, lat_row0 = _x_pair(x)
    return pl.pallas_call(
        _inproj_kernel,
        grid=(N_SEG, nt),
        in_specs=[*_seg_pair_specs(IN_TM, D_MODEL, lat_row0),
                  pl.BlockSpec((1, 1, 6 * D_MODEL), lambda s, i: (s, 0, 0)),
                  pl.BlockSpec((1, D_MODEL), lambda s, i: (0, 0)),
                  pl.BlockSpec((D_MODEL, P_W), lambda s, i: (0, 0)),
                  pl.BlockSpec((4, LANES), lambda s, i: (0, 0)),
                  pl.BlockSpec((IN_TM, LANES), lambda s, i: (i, 0)),
                  pl.BlockSpec((IN_TM, LANES), lambda s, i: (i, 0))],
        out_specs=pl.BlockSpec((IN_TM, P_W), lambda s, i: (s * nt + i, 0)),
        out_shape=jax.ShapeDtypeStruct((N_TOK, P_W), F32),
        compiler_params=_cparams(("arbitrary", "arbitrary")),
        name="inproj",
    )(*x_pair, mod_l, norm1, w_in_p, qk_gains, cos_t, sin_t)


def _mlstm_direction(d, q_ref, k_ref, v_ref, g_ref, bias, tri_ref, sel_ref, cbd, nbd, m_s):
    ch = ML_CHUNK
    r_io = lax.broadcasted_iota(jnp.int32, (ch, ch), 0)
    c_io = lax.broadcasted_iota(jnp.int32, (ch, ch), 1)
    lower = r_io >= c_io
    upper = r_io <= c_io
    mask = lower if d == 0 else upper
    tri = tri_ref[d]
    tri_t = tri_ref[1 - d]
    left = c_io < HEAD_DIM
    top = r_io < HEAD_DIM
    blockdiag = top == left
    g = g_ref[...] + bias[...]
    b_cols = _dot_exact_rhs(tri, jax.nn.log_sigmoid(g))
    b3 = jnp.concatenate(_split3(b_cols), axis=1)
    gt = g.T
    li_rows = gt[0:8, :]
    b_rows = _dot_exact_lhs(jax.nn.log_sigmoid(gt[8:16, :]), tri_t)
    a_rows = li_rows - b_rows
    bl = b_rows[:, ch - 1:ch] if d == 0 else b_rows[:, 0:1]
    m_old = m_s[:, 0:1]
    g_rows = bl - b_rows + li_rows
    m_new = jnp.maximum(bl + m_old, jnp.max(g_rows, axis=1, keepdims=True))
    wk_rows = jnp.exp(g_rows - m_new)
    wp = jnp.exp(bl + m_old - m_new)
    ones_blk = jnp.ones((ch, LANES), BF16)
    left2 = lax.broadcasted_iota(jnp.int32, (ch, 2 * LANES), 1) % LANES < HEAD_DIM
    h_out, c_out, n_out = [], [], []
    for p in range(ML_HEADS // 2):
        lanes = slice(LANES * p, LANES * (p + 1))
        j0 = ML_HEADS * d + 2 * p
        q2 = q_ref[:, lanes]
        k2t = (k_ref[:, lanes] * SCALE).T.astype(BF16)
        v2e = jnp.concatenate([v_ref[:, lanes].astype(BF16), ones_blk], axis=1)
        cb = cbd[2 * d + p]
        nb = nbd[2 * d + p]
        q2b = q2.astype(BF16)
        q_lo = (q2 - q2b.astype(F32)).astype(BF16)
        nb_hi = nb.astype(BF16)
        nb_lo = (nb - nb_hi.astype(F32)).astype(BF16)
        qc = jnp.dot(q2b, cb.astype(BF16), preferred_element_type=F32)
        qn = jnp.dot(jnp.concatenate([q2b, q_lo, q2b], axis=1), jnp.concatenate([nb_hi, nb_hi, nb_lo], axis=0),
                     preferred_element_type=F32)
        b_pair = jnp.dot(b3, sel_ref[2 * d + p], preferred_element_type=F32)
        cbs, sves = [], []
        for i in range(2):
            j = j0 + i
            half = left if i == 0 else jnp.logical_not(left)
            a_mat = jnp.where(mask, a_rows[j:j + 1, :], NEG)
            cvec = jnp.maximum(m_old[j:j + 1, :], jnp.max(a_mat, axis=1, keepdims=True))
            cbro = jnp.broadcast_to(cvec, (ch, ch))
            s = jnp.dot(jnp.where(half, q2b, 0), k2t, preferred_element_type=F32) * jnp.exp(a_mat - cbro)
            s_hi = s.astype(BF16)
            s_lo = (s - s_hi.astype(F32)).astype(BF16)
            sve = jnp.dot(s_hi, v2e, preferred_element_type=F32)
            rs_lo = jnp.dot(s_lo, ones_blk, preferred_element_type=F32)
            sves.append(jnp.concatenate([sve[:, 0:LANES], sve[:, LANES:2 * LANES] + rs_lo], axis=1))
            cbs.append(cbro)
        c_pair = jnp.where(left, cbs[0], cbs[1])
        w_prev = jnp.exp(jnp.where(left, m_old[j0:j0 + 1, :], m_old[j0 + 1:j0 + 2, :]) - c_pair)
        sve = jnp.where(left2, sves[0], sves[1])
        num = w_prev * qc + sve[:, 0:LANES]
        den = w_prev * qn + sve[:, LANES:2 * LANES]
        h_out.append(num / jnp.maximum(jnp.abs(den), jnp.exp(-(c_pair + b_pair))))
        kwt = k2t * jnp.where(top, wk_rows[j0:j0 + 1, :], wk_rows[j0 + 1:j0 + 2, :])
        kwt_hi = kwt.astype(BF16)
        kwt_lo = (kwt - kwt_hi.astype(F32)).astype(BF16)
        kve = jnp.dot(kwt_hi, v2e, preferred_element_type=F32)
        kn = kve[:, LANES:2 * LANES] + jnp.dot(kwt_lo, ones_blk, preferred_element_type=F32)
        wp_pair = jnp.where(top, wp[j0:j0 + 1, :], wp[j0 + 1:j0 + 2, :])
        c_out.append(wp_pair * cb + jnp.where(blockdiag, kve[:, 0:LANES], 0.0))
        n_out.append(wp_pair * nb + jnp.where(blockdiag, kn, 0.0))
    return h_out, c_out, n_out, m_new


def _mlstm_kernel(qf, kf, vf, gf, qb, kb, vb, gb, c0, n0, m0, bias, tri_ref, sel_ref,
                  hf, hb, co, no, mo, cbd, nbd, m_s, *, nc):
    c = pl.program_id(1)

    @pl.when(c == 0)
    def _():
        cbd[...] = c0[0]
        nbd[...] = n0[0]
        m_s[...] = m0[0]

    res = [_mlstm_direction(d, *refs, bias, tri_ref, sel_ref, cbd, nbd, m_s)
           for d, refs in enumerate(((qf, kf, vf, gf), (qb, kb, vb, gb)))]
    for d, (h_ref, (h_out, c_out, n_out, m_new)) in enumerate(zip((hf, hb), res)):
        for p in range(ML_HEADS // 2):
            h_ref[:, LANES * p:LANES * (p + 1)] = h_out[p]
            cbd[2 * d + p] = c_out[p]
            nbd[2 * d + p] = n_out[p]
        rows = slice(ML_HEADS * d, ML_HEADS * (d + 1))
        m_s[rows, :] = jnp.broadcast_to(m_new[rows, :], (ML_HEADS, LANES))

    @pl.when(c == nc - 1)
    def _():
        co[0] = cbd[...]
        no[0] = nbd[...]
        mo[0] = m_s[...]


def _mlstm_pack_state(c0, n0, m0):
    bsz = c0.shape[0]
    hd = HEAD_DIM
    eye = jnp.eye(2, dtype=F32)[None, None, :, None, :, None]
    cbd = c0.reshape(bsz, ML_HEADS, 2, hd, 1, hd) * eye
    nbd = jnp.broadcast_to(n0.reshape(bsz, ML_HEADS, 2, hd, 1, 1) * eye, cbd.shape)
    to_mat = lambda t: t.reshape(bsz, ML_HEADS, LANES, LANES)
    return to_mat(cbd), to_mat(nbd), jnp.broadcast_to(m0[..., None], m0.shape + (LANES,))


def _mlstm_unpack_state(cbd, nbd, mrow):
    hd = HEAD_DIM
    bsz = cbd.shape[0]
    c = jnp.stack([cbd[:, :, :hd, :hd], cbd[:, :, hd:, hd:]], axis=2).reshape(bsz, 2 * ML_HEADS, hd, hd)
    n = jnp.stack([nbd[:, :, :hd, 0], nbd[:, :, hd:, hd]], axis=2).reshape(bsz, 2 * ML_HEADS, hd)
    return c, n, mrow[:, :, 0]


def _mlstm_tables():
    r = np.arange(ML_CHUNK)
    lower = (r[:, None] >= r[None, :]).astype(np.float32)
    row = np.arange(3 * LANES)[:, None] % LANES
    lane_left = np.arange(LANES)[None, :] < HEAD_DIM
    sel = [row == np.where(lane_left, 8 + ML_HEADS * d + 2 * p, 9 + ML_HEADS * d + 2 * p)
           for d in range(2) for p in range(ML_HEADS // 2)]
    return (jnp.asarray(np.stack([lower, lower.T]), dtype=BF16),
            jnp.asarray(np.stack(sel).astype(np.float32), dtype=BF16))


def _mlstm(proj, gate_bias, c0, n0, m0, *, bsz, seq, row0):
    nc = seq // ML_CHUNK
    tri, sel = _mlstm_tables()
    base = row0 // ML_CHUNK
    nj = 2 * ML_HEADS
    fw = lambda col: (lambda b, c: (base + b * nc + c, col))
    bw = lambda col: (lambda b, c: (base + b * nc + nc - 1 - c, col))
    qkv = lambda f: [pl.BlockSpec((ML_CHUNK, 256), f(C_MQ // 256)),
                     pl.BlockSpec((ML_CHUNK, 256), f(C_MK // 256)),
                     pl.BlockSpec((ML_CHUNK, 256), f(C_MV // 256)),
                     pl.BlockSpec((ML_CHUNK, LANES), f(C_G // LANES))]
    st_specs = [pl.BlockSpec((1, nj // 2, LANES, LANES), lambda b, c: (b, 0, 0, 0)),
                pl.BlockSpec((1, nj // 2, LANES, LANES), lambda b, c: (b, 0, 0, 0)),
                pl.BlockSpec((1, nj, LANES), lambda b, c: (b, 0, 0))]
    st_shapes = [jax.ShapeDtypeStruct((bsz, nj // 2, LANES, LANES), F32),
                 jax.ShapeDtypeStruct((bsz, nj // 2, LANES, LANES), F32),
                 jax.ShapeDtypeStruct((bsz, nj, LANES), F32)]
    return pl.pallas_call(
        functools.partial(_mlstm_kernel, nc=nc),
        grid=(bsz, nc),
        in_specs=qkv(fw) + qkv(bw) + st_specs + [pl.BlockSpec((1, LANES), lambda b, c: (0, 0)),
                                                 pl.BlockSpec(tri.shape, lambda b, c: (0, 0, 0)),
                                                 pl.BlockSpec(sel.shape, lambda b, c: (0, 0, 0))],
        out_specs=[pl.BlockSpec((ML_CHUNK, 256), lambda b, c: (b * nc + c, 0)),
                   pl.BlockSpec((ML_CHUNK, 256), lambda b, c: (b * nc + nc - 1 - c, 0))] + st_specs,
        out_shape=[jax.ShapeDtypeStruct((bsz * seq, 256), F32),
                   jax.ShapeDtypeStruct((bsz * seq, 256), F32)] + st_shapes,
        scratch_shapes=[pltpu.VMEM((nj // 2, LANES, LANES), F32),
                        pltpu.VMEM((nj // 2, LANES, LANES), F32),
                        pltpu.VMEM((nj, LANES), F32)],
        compiler_params=_cparams(("arbitrary", "arbitrary")),
        name="mlstm",
    )(proj, proj, proj, proj, proj, proj, proj, proj, c0, n0, m0, gate_bias, tri, sel)


RG_TC = 256
RG_PAD = 8
RG_NSEG = 8
RG_SKEW = 4


def _rglru_kernel(rx, ry, cw, cb, wg, bg, lam, h0, oc, hl, xpad, af, ab, uf, ub, *, seg_len, chained):
    rows = RG_NSEG * seg_len
    seq_len = rows if chained else seg_len
    halves = RG_WIDTH // LANES
    pitch = seg_len + RG_SKEW
    buf_row = lambda t: (t // seg_len) * pitch + t % seg_len
    xpad[0:RG_PAD, :] = jnp.zeros((RG_PAD, RG_WIDTH), F32)
    xpad[rows + RG_PAD:rows + 2 * RG_PAD, :] = jnp.zeros((RG_PAD, RG_WIDTH), F32)
    xpad[RG_PAD:rows + RG_PAD, :] = rx[...]
    sp = jax.nn.softplus(-lam[...])
    left = (RG_CONV - 1) // 2
    for ci in range(rows // RG_TC):
        s0 = ci * RG_TC
        pos = (s0 + lax.broadcasted_iota(jnp.int32, (RG_TC, RG_WIDTH), 0)) % seq_len
        xc = None
        for j in range(RG_CONV):
            a = RG_PAD + s0 + j - left
            term = xpad[a:a + RG_TC, :] * cw[j:j + 1, :]
            if not chained and j != left:
                term = jnp.where((pos + (j - left) >= 0) & (pos + (j - left) < seq_len), term, 0.0)
            xc = term if xc is None else xc + term
        xc = xc + cb[...]
        pre = _dot(xc, wg[...]) + bg[...]
        for d, (a_ref, u_ref) in enumerate(((af, uf), (ab, ub))):
            o = 2 * RG_WIDTH * d
            r = _sigmoid(pre[:, o:o + RG_WIDTH])
            gi = _sigmoid(pre[:, o + RG_WIDTH:o + 2 * RG_WIDTH])
            log_a = -RG_C * r * sp[d:d + 1, :]
            a_val = jnp.exp(log_a)
            u_val = jnp.sqrt(-jnp.tanh(log_a) * (a_val * a_val + 1.0)) * (gi * xc)
            for hv in range(halves):
                dst = slice(buf_row(s0), buf_row(s0) + RG_TC)
                a_ref[hv, dst, :] = a_val[:, LANES * hv:LANES * (hv + 1)]
                u_ref[hv, dst, :] = u_val[:, LANES * hv:LANES * (hv + 1)]

    def body(s, carry):
        out = []
        for d, (a_ref, u_ref) in enumerate(((af, uf), (ab, ub))):
            step_rows = pl.ds(s if d == 0 else seg_len - 1 - s, RG_NSEG, stride=pitch)
            for hv in range(halves):
                h_loc, prod = carry[2 * (halves * d + hv)], carry[2 * (halves * d + hv) + 1]
                a = a_ref[hv, step_rows, :]
                h_loc = a * h_loc + u_ref[hv, step_rows, :]
                prod = a * prod
                u_ref[hv, step_rows, :] = h_loc
                a_ref[hv, step_rows, :] = prod
                out += [h_loc, prod]
        return tuple(out)

    zero = jnp.zeros((RG_NSEG, LANES), F32)
    one = jnp.ones((RG_NSEG, LANES), F32)
    ends = lax.fori_loop(0, seg_len, body, (zero, one) * (2 * halves), unroll=8)
    seg = lax.broadcasted_iota(jnp.int32, (RG_NSEG, LANES), 0)
    for d, (a_ref, u_ref) in enumerate(((af, uf), (ab, ub))):
        for hv in range(halves):
            lanes = slice(LANES * hv, LANES * (hv + 1))
            h_end, p_end = ends[2 * (halves * d + hv)], ends[2 * (halves * d + hv) + 1]
            h_in = h0[0, d][:, lanes]
            if chained:
                c = h_in[0:1, :]
                h_in = zero
                for k in (range(RG_NSEG) if d == 0 else reversed(range(RG_NSEG))):
                    h_in = jnp.where(seg == k, c, h_in)
                    c = h_end[k:k + 1, :] + p_end[k:k + 1, :] * c
            hl[0, d, :, lanes] = h_end + p_end * h_in
            for k in range(RG_NSEG):
                for ci in range(seg_len // RG_TC):
                    sl = slice(k * pitch + ci * RG_TC, k * pitch + (ci + 1) * RG_TC)
                    u_ref[hv, sl, :] = u_ref[hv, sl, :] + a_ref[hv, sl, :] * h_in[k:k + 1, :]
    for ci in range(rows // RG_TC):
        sl = slice(ci * RG_TC, (ci + 1) * RG_TC)
        for hv in range(halves):
            lanes = slice(LANES * hv, LANES * (hv + 1))
            src = slice(buf_row(ci * RG_TC), buf_row(ci * RG_TC) + RG_TC)
            oc[sl, lanes] = (uf[hv, src, :] + ub[hv, src, :]) * jax.nn.gelu(ry[sl, lanes])


def _rglru(proj, cw, cb, wg, bg, lam, h0, *, nblk, seg_len, chained, row0):
    rows = RG_NSEG * seg_len
    base = row0 // rows
    full = lambda shape: pl.BlockSpec(shape, lambda b: tuple(0 for _ in shape))
    st_spec = pl.BlockSpec((1, 2, RG_NSEG, RG_WIDTH), lambda b: (b, 0, 0, 0))
    return pl.pallas_call(
        functools.partial(_rglru_kernel, seg_len=seg_len, chained=chained),
        grid=(nblk,),
        in_specs=[pl.BlockSpec((rows, RG_WIDTH), lambda b: (base + b, C_RX // RG_WIDTH)),
                  pl.BlockSpec((rows, RG_WIDTH), lambda b: (base + b, C_RY // RG_WIDTH)),
                  full((RG_CONV, RG_WIDTH)), full((1, RG_WIDTH)),
                  full((RG_WIDTH, 4 * RG_WIDTH)), full((1, 4 * RG_WIDTH)), full((2, RG_WIDTH)),
                  st_spec],
        out_specs=[pl.BlockSpec((rows, RG_WIDTH), lambda b: (b, 0)), st_spec],
        out_shape=[jax.ShapeDtypeStruct((nblk * rows, RG_WIDTH), F32),
                   jax.ShapeDtypeStruct((nblk, 2, RG_NSEG, RG_WIDTH), F32)],
        scratch_shapes=[pltpu.VMEM((rows + 2 * RG_PAD, RG_WIDTH), F32)]
        + [pltpu.VMEM((RG_WIDTH // LANES, RG_NSEG * (seg_len + RG_SKEW), LANES), F32) for _ in range(4)],
        compiler_params=_cparams(("arbitrary",)),
        name="rglru",
    )(proj, proj, cw, cb, wg, bg, lam, h0)


def _softmax_pv(scores, values, sink):
    m = functools.reduce(jnp.maximum, [jnp.max(s, axis=-1, keepdims=True) for s in scores])
    if sink is not None:
        m = jnp.maximum(m, sink)
    ps = [jnp.exp(s - m) for s in scores]
    den = functools.reduce(jnp.add, [jnp.sum(p, axis=-1, keepdims=True) for p in ps])
    if sink is not None:
        den = den + jnp.exp(sink - m)
    num = functools.reduce(jnp.add, [_dot(p, v) for p, v in zip(ps, values)])
    return num / den


def _ctx_attn_kernel(sink, sq, sk, sv, nq, nk, nv, ob, od):
    assert SW_HEADS == 4 and SW_KV_HEADS == 2
    left = lax.broadcasted_iota(jnp.int32, (SEQ, LANES), 1) < HEAD_DIM
    first = lax.broadcasted_iota(jnp.int32, (2 * SEQ, 1), 0) < SEQ
    k2 = sk[...].astype(BF16)
    v2 = sv[...].astype(BF16)
    for p in range(SW_KV_HEADS):
        lanes = slice(LANES * p, LANES * (p + 1))
        q2 = sq[:, lanes]
        q_swapped = pltpu.roll(q2, HEAD_DIM, 1)
        kv_half = left if p == 0 else jnp.logical_not(left)
        qs = jnp.concatenate([jnp.where(kv_half, q2 if i == p else q_swapped, 0.0) for i in range(2)], axis=0)
        sink_col = jnp.where(first, sink[2 * p], sink[2 * p + 1])
        res = _softmax_pv([_dot_nt(qs, k2) * SCALE], [v2], sink_col)
        halves = [res[0:SEQ], res[SEQ:2 * SEQ]]
        placed = [halves[i] if i == p else pltpu.roll(halves[i], HEAD_DIM, 1) for i in range(2)]
        ob[:, lanes] = jnp.where(left, placed[0], placed[1])
    for p in range(NA_HEADS // 2):
        lanes = slice(LANES * p, LANES * (p + 1))
        q2 = nq[:, lanes]
        k2 = nk[:, lanes].astype(BF16)
        v2 = nv[:, lanes].astype(BF16)
        res = [_softmax_pv([_dot_nt(jnp.where(left if i == 0 else jnp.logical_not(left), q2, 0.0), k2) * SCALE],
                           [v2], None) for i in range(2)]
        od[:, lanes] = jnp.where(left, res[0], res[1])


def _ctx_attn(proj, sink):
    blk = lambda w, col: pl.BlockSpec((SEQ, w), lambda b: (b, col))
    return pl.pallas_call(
        _ctx_attn_kernel,
        grid=(BATCH,),
        in_specs=[pl.BlockSpec(memory_space=pltpu.SMEM),
                  blk(256, C_SQ // 256), blk(128, C_SK // 128), blk(128, C_SV // 128),
                  blk(256, C_NQ // 256), blk(256, C_NK // 256), blk(256, C_NV // 256)],
        out_specs=[pl.BlockSpec((SEQ, 256), lambda b: (b, 0)),
                   pl.BlockSpec((SEQ, 256), lambda b: (b, 0))],
        out_shape=[jax.ShapeDtypeStruct((SEG, 256), F32), jax.ShapeDtypeStruct((SEG, 256), F32)],
        compiler_params=_cparams(("arbitrary",)),
        name="ctx_attn",
    )(sink, proj, proj, proj, proj, proj, proj)


SW_QB = 128
SW_SPAN = SW_QB + 2 * SW_WINDOW


def _swa_kernel(sink, q, k, v, kc, vc, ob):
    assert SW_HEADS == 4 and SW_KV_HEADS == 2
    n = pl.program_id(1)
    ws = jnp.clip((n - 1) * SW_QB, 0, DEC_SEQ - SW_SPAN)
    ws = pl.multiple_of(ws, SW_QB)
    row = lax.broadcasted_iota(jnp.int32, (2 * SW_QB, SW_SPAN), 0)
    qpos = n * SW_QB + row % SW_QB
    kpos = ws + lax.broadcasted_iota(jnp.int32, (2 * SW_QB, SW_SPAN), 1)
    valid = jnp.abs(qpos - kpos) <= SW_WINDOW
    left = lax.broadcasted_iota(jnp.int32, (SW_QB, LANES), 1) < HEAD_DIM
    first = lax.broadcasted_iota(jnp.int32, (2 * SW_QB, 1), 0) < SW_QB
    k2 = k[pl.ds(ws, SW_SPAN), :].astype(BF16)
    v2 = v[pl.ds(ws, SW_SPAN), :].astype(BF16)
    kc2 = kc[0].astype(BF16)
    vc2 = vc[0].astype(BF16)
    for p in range(SW_KV_HEADS):
        lanes = slice(LANES * p, LANES * (p + 1))
        q2 = q[:, lanes]
        q_swapped = pltpu.roll(q2, HEAD_DIM, 1)
        kv_half = left if p == 0 else jnp.logical_not(left)
        qs = jnp.concatenate([jnp.where(kv_half, q2 if i == p else q_swapped, 0.0) for i in range(2)], axis=0)
        s_loc = jnp.where(valid, _dot_nt(qs, k2) * SCALE, NEG)
        s_ctx = _dot_nt(qs, kc2) * SCALE
        sink_col = jnp.where(first, sink[2 * p], sink[2 * p + 1])
        res = _softmax_pv([s_loc, s_ctx], [v2, vc2], sink_col)
        halves = [res[0:SW_QB], res[SW_QB:2 * SW_QB]]
        placed = [halves[i] if i == p else pltpu.roll(halves[i], HEAD_DIM, 1) for i in range(2)]
        ob[:, lanes] = jnp.where(left, placed[0], placed[1])


def _swa(proj, kc, vc, sink):
    nq = DEC_SEQ // SW_QB
    qbase = SEG // SW_QB
    return pl.pallas_call(
        _swa_kernel,
        grid=(DEC_BATCH, nq),
        in_specs=[pl.BlockSpec(memory_space=pltpu.SMEM),
                  pl.BlockSpec((SW_QB, 256), lambda b, n: (qbase + b * nq + n, C_SQ // 256)),
                  pl.BlockSpec((DEC_SEQ, 128), lambda b, n: (1 + b, C_SK // 128)),
                  pl.BlockSpec((DEC_SEQ, 128), lambda b, n: (1 + b, C_SV // 128)),
                  pl.BlockSpec((1, PAST_LEN, 128), lambda b, n: (b, 0, 0)),
                  pl.BlockSpec((1, PAST_LEN, 128), lambda b, n: (b, 0, 0))],
        out_specs=pl.BlockSpec((SW_QB, 256), lambda b, n: (b * nq + n, 0)),
        out_shape=jax.ShapeDtypeStruct((DEC_BATCH * DEC_SEQ, 256), F32),
        compiler_params=_cparams(("arbitrary", "arbitrary")),
        name="swa",
    )(sink, proj, proj, proj, kc, vc)


NA_RPB_R = 2 * NA_ROWS - 1
NA_RPB_C = 2 * NA_COLS - 1
GRID_ROWS = DEC_SEQ // GRID_W
NA_RB = 4
NA_UW = 12
NA_NQ = NA_RB * GRID_W
NA_NKEY = NA_UW * GRID_W
NA_NBLK = GRID_ROWS // NA_RB
NA_CASES = ((0, 0), (NA_RB, 0), (GRID_ROWS - NA_RB, GRID_ROWS - NA_UW))


def _na_row_start(qrow):
    return min(max(qrow - NA_ROWS // 2, 0), GRID_ROWS - NA_ROWS)


def _na_bias_kernel(rpb, out):
    h = pl.program_id(0)
    qc = lax.broadcasted_iota(jnp.int32, (GRID_W, GRID_W), 0)
    kc = lax.broadcasted_iota(jnp.int32, (GRID_W, GRID_W), 1)
    dc = jnp.clip(kc - qc, -(NA_COLS - 1), NA_COLS - 1) + NA_COLS - 1
    lo = jnp.clip(qc - NA_COLS // 2, 0, GRID_W - NA_COLS)
    valid = (kc >= lo) & (kc < lo + NA_COLS)
    tiles = []
    for dr in range(NA_RPB_R):
        t = jnp.zeros((GRID_W, GRID_W), F32)
        for j in range(NA_RPB_C):
            t = jnp.where(dc == j, rpb[(h * NA_RPB_R + dr) * NA_RPB_C + j], t)
        tiles.append(jnp.where(valid, t, NEG))
    outside = jnp.full((GRID_W, GRID_W), NEG, F32)
    for case, (q0, k0) in enumerate(NA_CASES):
        for a in range(NA_RB):
            r0 = _na_row_start(q0 + a)
            for i in range(NA_UW):
                inside = r0 <= k0 + i < r0 + NA_ROWS
                tile = tiles[k0 + i - (q0 + a) + NA_ROWS - 1] if inside else outside
                out[0, case, GRID_W * a:GRID_W * (a + 1), GRID_W * i:GRID_W * (i + 1)] = tile


def _na_bias(rpb_flat):
    shape = (NA_HEADS, len(NA_CASES), NA_NQ, NA_NKEY)
    return pl.pallas_call(
        _na_bias_kernel,
        grid=(NA_HEADS,),
        in_specs=[pl.BlockSpec(memory_space=pltpu.SMEM)],
        out_specs=pl.BlockSpec((1,) + shape[1:], lambda h: (h, 0, 0, 0)),
        out_shape=jax.ShapeDtypeStruct(shape, F32),
        compiler_params=_cparams(("arbitrary",)),
        name="na_bias",
    )(rpb_flat)


def _na_kernel(q, k, v, kc, vc, bias, od):
    blk = pl.program_id(1)
    case = jnp.where(blk == 0, 0, jnp.where(blk == NA_NBLK - 1, 2, 1))
    u0 = jnp.clip(blk * NA_RB - NA_ROWS // 2, 0, GRID_ROWS - NA_UW)
    k0 = pl.multiple_of(u0 * GRID_W, GRID_W)
    left = lax.broadcasted_iota(jnp.int32, (NA_NQ, LANES), 1) < HEAD_DIM
    for p in range(NA_HEADS // 2):
        lanes = slice(LANES * p, LANES * (p + 1))
        q2 = q[:, lanes]
        k2 = k[pl.ds(k0, NA_NKEY), lanes].astype(BF16)
        v2 = v[pl.ds(k0, NA_NKEY), lanes].astype(BF16)
        kc2 = kc[0, :, lanes].astype(BF16)
        vc2 = vc[0, :, lanes].astype(BF16)
        res = []
        for i in range(2):
            qm = jnp.where(left if i == 0 else jnp.logical_not(left), q2, 0.0)
            s_loc = _dot_nt(qm, k2) * SCALE + bias[2 * p + i, pl.ds(case, 1)][0]
            s_ctx = _dot_nt(qm, kc2) * SCALE
            res.append(_softmax_pv([s_loc, s_ctx], [v2, vc2], None))
        od[:, lanes] = jnp.where(left, res[0], res[1])


def _na(proj, kc, vc, bias):
    qbase = SEG // NA_NQ
    return pl.pallas_call(
        _na_kernel,
        grid=(DEC_BATCH, NA_NBLK),
        in_specs=[pl.BlockSpec((NA_NQ, 256), lambda b, r: (qbase + b * NA_NBLK + r, C_NQ // 256)),
                  pl.BlockSpec((DEC_SEQ, 256), lambda b, r: (1 + b, C_NK // 256)),
                  pl.BlockSpec((DEC_SEQ, 256), lambda b, r: (1 + b, C_NV // 256)),
                  pl.BlockSpec((1, PAST_LEN, 256), lambda b, r: (b, 0, 0)),
                  pl.BlockSpec((1, PAST_LEN, 256), lambda b, r: (b, 0, 0)),
                  pl.BlockSpec((NA_HEADS, len(NA_CASES), NA_NQ, NA_NKEY), lambda b, r: (0, 0, 0, 0))],
        out_specs=pl.BlockSpec((NA_NQ, 256), lambda b, r: (b * NA_NBLK + r, 0)),
        out_shape=jax.ShapeDtypeStruct((DEC_BATCH * DEC_SEQ, 256), F32),
        compiler_params=_cparams(("arbitrary", "arbitrary")),
        name="na",
    )(proj, proj, proj, kc, vc, bias)


MG_TM = 256


def _merge_kernel(xc_ref, xl_ref, mod_ref, g1_ref, g2_ref, hf_c, hf_l, hb_c, hb_l, mo, ob_c, ob_l, oc_c, oc_l,
                  od_c, od_l, hn, wmg, bmg, wbr, wout, *rest, moe):
    if moe:
        wrt, br, x1_ref, h2_ref, route_ref = rest
    else:
        x1_ref, h2_ref = rest
    pick = _pick
    mod = mod_ref[0]
    chunk = lambda i: mod[:, i * D_MODEL:(i + 1) * D_MODEL]
    sh1, sc1, gate1, sh2, sc2 = chunk(0), chunk(1), chunk(2), chunk(3), chunk(4)
    x = pick(xc_ref, xl_ref)
    h = (_rms(x, g1_ref[...]) * (1.0 + sc1) + sh1).astype(BF16)
    hsum = pick(hf_c, hf_l) + pick(hb_c, hb_l)
    out_a = jnp.concatenate(
        [_rms_head_pairs(hsum[:, LANES * p:LANES * (p + 1)], hn[...]) for p in range(2)], axis=-1)
    out_a = out_a * jax.nn.sigmoid(mo[...])
    acc = None
    for n, br_val in enumerate((out_a, pick(ob_c, ob_l), pick(oc_c, oc_l), pick(od_c, od_l))):
        gate = jax.nn.sigmoid(jnp.dot(h, wmg[:, n * D_MODEL:(n + 1) * D_MODEL], preferred_element_type=F32)
                              + bmg[:, n * D_MODEL:(n + 1) * D_MODEL])
        term = gate * jnp.dot(br_val.astype(BF16), wbr[n], preferred_element_type=F32)
        acc = term if acc is None else acc + term
    y = jnp.dot(acc.astype(BF16), wout[...], preferred_element_type=F32)
    x1 = x + gate1 * y
    x1_ref[...] = x1
    h2 = _rms(x1, g2_ref[...]) * (1.0 + sc2) + sh2
    h2_ref[...] = h2.astype(h2_ref.dtype)
    if moe:
        logit = [jnp.sum(h2 * wrt[e:e + 1, :], axis=-1, keepdims=True) + br[e] for e in range(N_EXPERTS)]
        v1, i1 = logit[0], jnp.zeros(logit[0].shape, jnp.int32)
        for e in range(1, N_EXPERTS):
            better = logit[e] > v1
            v1 = jnp.where(better, logit[e], v1)
            i1 = jnp.where(better, e, i1)
        v2, i2 = jnp.full(v1.shape, -jnp.inf, F32), jnp.zeros(v1.shape, jnp.int32)
        for e in range(N_EXPERTS):
            better = (i1 != e) & (logit[e] > v2)
            v2 = jnp.where(better, logit[e], v2)
            i2 = jnp.where(better, e, i2)
        e2 = jnp.exp(v2 - v1)
        den = 1.0 + e2
        lane = lax.broadcasted_iota(jnp.int32, route_ref.shape, 1)
        route = jnp.where(lane == 0, 1.0 / den, 0.0) + jnp.where(lane == 1, e2 / den, 0.0)
        route = route + jnp.where(lane == 2, i1.astype(F32), 0.0) + jnp.where(lane == 3, i2.astype(F32), 0.0)
        route_ref[...] = route


def _merge(x, mod_l, g1, g2, hf, hb, proj, ob, oc, od, hn, wmg, bmg, wbr, wout, router=None):
    nt = SEG // MG_TM
    moe = router is not None
    x_pair, lat_row0 = _x_pair(x)
    row = lambda w: pl.BlockSpec((MG_TM, w), lambda s, i: (s * nt + i, 0))
    ctx_blk, lat_blk = _seg_pair_specs(MG_TM, 256)
    full = lambda shape: pl.BlockSpec(shape, lambda s, i: tuple(0 for _ in shape))
    in_specs = [*_seg_pair_specs(MG_TM, D_MODEL, lat_row0),
                pl.BlockSpec((1, 1, 6 * D_MODEL), lambda s, i: (s, 0, 0)),
                full((1, D_MODEL)), full((1, D_MODEL)),
                ctx_blk, lat_blk, ctx_blk, lat_blk,
                pl.BlockSpec((MG_TM, 256), lambda s, i: (s * nt + i, C_MO // 256)),
                ctx_blk, lat_blk, ctx_blk, lat_blk, ctx_blk, lat_blk,
                full((1, LANES)), full((D_MODEL, N_BRANCH * D_MODEL)), full((1, N_BRANCH * D_MODEL)),
                full((N_BRANCH, 256, D_MODEL)), full((D_MODEL, D_MODEL))]
    args = [*x_pair, mod_l, g1, g2, *hf, *hb, proj, *ob, *oc, *od, hn, wmg, bmg, wbr, wout]
    out_specs = [row(D_MODEL), row(D_MODEL)]
    out_shape = [jax.ShapeDtypeStruct((N_TOK, D_MODEL), F32),
                 jax.ShapeDtypeStruct((N_TOK, D_MODEL), F32 if moe else BF16)]
    if moe:
        in_specs += [full((N_EXPERTS, D_MODEL)), pl.BlockSpec(memory_space=pltpu.SMEM)]
        args += list(router)
        out_specs.append(row(LANES))
        out_shape.append(jax.ShapeDtypeStruct((N_TOK, LANES), F32))
    return pl.pallas_call(
        functools.partial(_merge_kernel, moe=moe),
        grid=(N_SEG, nt),
        in_specs=in_specs,
        out_specs=out_specs,
        out_shape=out_shape,
        compiler_params=_cparams(("arbitrary", "arbitrary")),
        name="merge",
    )(*args)


FF_TM = 512
FF_TF = 1408


def _ffn_kernel(h2, w1, w3, w2, x1, mod_ref, out, acc):
    f = pl.program_id(1)

    @pl.when(f == 0)
    def _():
        acc[...] = jnp.zeros(acc.shape, F32)

    h = h2[...]
    a = jnp.dot(h, w1[...], preferred_element_type=F32)
    b = jnp.dot(h, w3[...], preferred_element_type=F32)
    act = (jax.nn.silu(a) * b).astype(BF16)
    acc[...] += jnp.dot(act, w2[...], preferred_element_type=F32)

    @pl.when(f == pl.num_programs(1) - 1)
    def _():
        gate2 = mod_ref[0][:, 5 * D_MODEL:6 * D_MODEL]
        out[...] = x1[...] + gate2 * acc[...]


def _ffn(h2, x1, mod_l, w1, w3, w2):
    nt = N_TOK // FF_TM
    per_seg = SEG // FF_TM
    return pl.pallas_call(
        _ffn_kernel,
        grid=(nt, D_FF // FF_TF),
        in_specs=[pl.BlockSpec((FF_TM, D_MODEL), lambda i, f: (i, 0)),
                  pl.BlockSpec((D_MODEL, FF_TF), lambda i, f: (0, f)),
                  pl.BlockSpec((D_MODEL, FF_TF), lambda i, f: (0, f)),
                  pl.BlockSpec((FF_TF, D_MODEL), lambda i, f: (f, 0)),
                  pl.BlockSpec((FF_TM, D_MODEL), lambda i, f: (i, 0)),
                  pl.BlockSpec((1, 1, 6 * D_MODEL), lambda i, f: (i // per_seg, 0, 0))],
        out_specs=pl.BlockSpec((FF_TM, D_MODEL), lambda i, f: (i, 0)),
        out_shape=jax.ShapeDtypeStruct((N_TOK, D_MODEL), F32),
        scratch_shapes=[pltpu.VMEM((FF_TM, D_MODEL), F32)],
        compiler_params=_cparams(("arbitrary", "arbitrary")),
        name="ffn",
    )(h2, w1, w3, w2, x1, mod_l)


MOE_TM = 256
MOE_SLOTS = 2 * N_TOK
MOE_TILES = MOE_SLOTS // MOE_TM + N_EXPERTS
MOE_NBUF = 3
MOE_STEPS = MOE_TILES + MOE_NBUF
MOE_DUMP = MOE_NBUF * MOE_TM
MOE_LEAD = 1
MOE_PLAN_TILES = MOE_LEAD + MOE_TILES + 2
MOE_FCHUNKS = 1
MOE_UNROLL = 8


def _moe_group_kernel(texp, nused, src_tok, dst_row, h2_hbm, w1, w3, w2, y_hbm, xs, ys, sem_in, sem_out):
    del texp
    i = pl.program_id(0)
    n_used = nused[0]
    buf = i % MOE_NBUF
    buf_next = (i + 2) % MOE_NBUF

    def gather_copy(tile, b, r):
        tok = src_tok[(tile + MOE_LEAD) * MOE_TM + r]
        return pltpu.make_async_copy(h2_hbm.at[pl.ds(tok, 1)], xs.at[b, pl.ds(r, 1)], sem_in.at[b])

    def scatter_copy(tile, b, r):
        dst = dst_row[(tile + MOE_LEAD) * MOE_TM + r]
        return pltpu.make_async_copy(ys.at[b, pl.ds(r, 1)], y_hbm.at[pl.ds(dst, 1)], sem_out.at[b])

    def start_rows_loop(make_copy, tile, b):
        def body(r, carry):
            make_copy(tile, b, r).start()
            return carry
        lax.fori_loop(0, MOE_TM, body, 0, unroll=MOE_UNROLL)

    def wait_tile(b, gather):
        if gather:
            pltpu.make_async_copy(h2_hbm.at[pl.ds(0, MOE_TM)], xs.at[b], sem_in.at[b]).wait()
        else:
            pltpu.make_async_copy(ys.at[b], y_hbm.at[pl.ds(0, MOE_TM)], sem_out.at[b]).wait()

    @pl.when(i == 0)
    def _():
        xs[...] = jnp.zeros(xs.shape, F32)
        ys[...] = jnp.zeros(ys.shape, F32)
        for b in range(MOE_NBUF):
            fill = pltpu.make_async_copy(ys.at[b], y_hbm.at[pl.ds(MOE_SLOTS + b * MOE_TM, MOE_TM)], sem_out.at[b])
            fill.start()
            fill.wait()
        start_rows_loop(gather_copy, 0, 0)
        start_rows_loop(gather_copy, 1, 1)

    @pl.when(i <= n_used + 1)
    def _():
        wait_tile(buf, True)

    @pl.when((i >= 2) & (i <= n_used + 2))
    def _():
        wait_tile(buf, False)

    @pl.when(i < n_used)
    def _():
        x = xs[buf].astype(BF16)
        fc = D_FF_EXPERT // MOE_FCHUNKS
        rc = MOE_TM // MOE_FCHUNKS
        y = None
        for c in range(MOE_FCHUNKS):
            for r in range(c * rc, (c + 1) * rc):
                gather_copy(i + 2, buf_next, r).start()
                scatter_copy(i - 1, buf_next, r).start()
            a = jnp.dot(x, w1[0, :, c * fc:(c + 1) * fc], preferred_element_type=F32)
            b = jnp.dot(x, w3[0, :, c * fc:(c + 1) * fc], preferred_element_type=F32)
            act = (jax.nn.silu(a) * b).astype(BF16)
            part = jnp.dot(act, w2[0, c * fc:(c + 1) * fc, :], preferred_element_type=F32)
            y = part if y is None else y + part
        ys[buf] = y

    @pl.when(i == n_used)
    def _():
        start_rows_loop(scatter_copy, i - 1, buf_next)


def _moe_group(tile_expert, n_used, src_tok, dst_row, h2, w1, w3, w2):
    wspec = lambda shape: pl.BlockSpec((1,) + shape, lambda i, texp, *_: (texp[jnp.minimum(i, MOE_TILES - 1)], 0, 0))
    grid_spec = pltpu.PrefetchScalarGridSpec(
        num_scalar_prefetch=4,
        grid=(MOE_STEPS,),
        in_specs=[pl.BlockSpec(memory_space=pl.ANY),
                  wspec((D_MODEL, D_FF_EXPERT)), wspec((D_MODEL, D_FF_EXPERT)), wspec((D_FF_EXPERT, D_MODEL))],
        out_specs=pl.BlockSpec(memory_space=pl.ANY),
        scratch_shapes=[pltpu.VMEM((MOE_NBUF, MOE_TM, D_MODEL), F32), pltpu.VMEM((MOE_NBUF, MOE_TM, D_MODEL), F32),
                        pltpu.SemaphoreType.DMA((MOE_NBUF,)), pltpu.SemaphoreType.DMA((MOE_NBUF,))])
    return pl.pallas_call(
        _moe_group_kernel,
        grid_spec=grid_spec,
        out_shape=jax.ShapeDtypeStruct((MOE_SLOTS + MOE_DUMP, D_MODEL), F32),
        compiler_params=_cparams(("arbitrary",)),
        name="moe_group",
    )(tile_expert, n_used, src_tok, dst_row, h2, w1, w3, w2)


def _moe_plan(expert_ids):
    e_flat = expert_ids.T.reshape(-1)
    order = jnp.argsort(e_flat, stable=True).astype(jnp.int32)
    counts = jnp.sum((e_flat[:, None] == jnp.arange(N_EXPERTS)[None, :]).astype(jnp.int32), axis=0)
    padded = (counts + MOE_TM - 1) // MOE_TM * MOE_TM
    pend = jnp.cumsum(padded)
    pstart = pend - padded
    ustart = jnp.cumsum(counts) - counts
    n_used = pend[-1] // MOE_TM
    tiles = jnp.arange(MOE_TILES, dtype=jnp.int32)
    last_used = jnp.minimum(tiles, n_used - 1)
    tile_expert = jnp.sum((last_used[:, None] * MOE_TM >= pend[None, :]).astype(jnp.int32), axis=1)
    t = jnp.arange(-MOE_LEAD, MOE_PLAN_TILES - MOE_LEAD, dtype=jnp.int32)[:, None]
    r = jnp.arange(MOE_TM, dtype=jnp.int32)[None, :]
    e_t = tile_expert[jnp.clip(t, 0, MOE_TILES - 1)]
    off = t * MOE_TM + r - pstart[e_t]
    valid = (t >= 0) & (t < n_used) & (off < counts[e_t])
    slot = order[jnp.clip(ustart[e_t] + off, 0, MOE_SLOTS - 1)]
    src_tok = jnp.where(valid, slot % N_TOK, 0)
    dst_row = jnp.where(valid, slot, MOE_SLOTS + (t % MOE_NBUF) * MOE_TM + r)
    return (tile_expert.astype(jnp.int32), n_used.reshape(1).astype(jnp.int32),
            src_tok.reshape(-1).astype(jnp.int32), dst_row.reshape(-1).astype(jnp.int32))


def _moe_combine_kernel(x1, y0, y1, route, mod_ref, out_c, out_l):
    gate2 = mod_ref[0][:, 5 * D_MODEL:6 * D_MODEL]
    r = route[...]
    val = x1[...] + gate2 * (r[:, 0:1] * y0[...] + r[:, 1:2] * y1[...])
    is_ctx = pl.program_id(0) < SEG // FF_TM

    @pl.when(is_ctx)
    def _():
        out_c[...] = val

    @pl.when(jnp.logical_not(is_ctx))
    def _():
        out_l[...] = val


def _moe_combine(x1, y_slots, route, mod_l):
    nt = N_TOK // FF_TM
    per_seg = SEG // FF_TM
    return pl.pallas_call(
        _moe_combine_kernel,
        grid=(nt,),
        in_specs=[pl.BlockSpec((FF_TM, D_MODEL), lambda i: (i, 0)),
                  pl.BlockSpec((FF_TM, D_MODEL), lambda i: (i, 0)),
                  pl.BlockSpec((FF_TM, D_MODEL), lambda i: (nt + i, 0)),
                  pl.BlockSpec((FF_TM, LANES), lambda i: (i, 0)),
                  pl.BlockSpec((1, 1, 6 * D_MODEL), lambda i: (i // per_seg, 0, 0))],
        out_specs=[pl.BlockSpec((FF_TM, D_MODEL), lambda i: (jnp.minimum(i, per_seg - 1), 0)),
                   pl.BlockSpec((FF_TM, D_MODEL), lambda i: (jnp.maximum(i - per_seg, 0), 0))],
        out_shape=[jax.ShapeDtypeStruct((SEG, D_MODEL), F32), jax.ShapeDtypeStruct((N_TOK - SEG, D_MODEL), F32)],
        compiler_params=_cparams(("arbitrary",)),
        name="moe_combine",
    )(x1, y_slots, y_slots, route, mod_l)


def _rope_tables():
    t = np.arange(DEC_SEQ)
    row, col = (t // GRID_W).astype(np.float32), (t % GRID_W).astype(np.float32)
    nf = HEAD_DIM // 4
    freqs = np.float32(ROPE_BASE) ** (-np.arange(nf, dtype=np.float32) / np.float32(nf))
    lane = np.arange(LANES) % HEAD_DIM
    fidx = lane % nf
    use_col = (lane // (HEAD_DIM // 2)) == 1
    first = (lane % (HEAD_DIM // 2)) < nf
    pos = np.where(use_col[None, :], col[:, None], row[:, None])
    ang = (pos * freqs[fidx][None, :]).astype(np.float32).astype(np.float64)
    sin = np.sin(ang)
    return (jnp.asarray(np.cos(ang), dtype=F32), jnp.asarray(np.where(first[None, :], -sin, sin), dtype=F32))


def _permute_w_in(w):
    sizes = (256, 256, 256, 256, 8, 8, 256, 128, 128, 256, 256, 256, 256, 256)
    offs = np.concatenate([[0], np.cumsum(sizes)])
    part = lambda i: w[:, offs[i]:offs[i + 1]]
    mq, mk, mv, mo, mi, mf, sq, sk, sv, rx, ry, nq, nk, nv = (part(i) for i in range(14))
    pad = jnp.zeros((w.shape[0], LANES - 16), w.dtype)
    return jnp.concatenate([mq, mk, mv, mo, sq, rx, ry, nq, nk, nv, sk, sv, mi, mf, pad], axis=1)


def _block_diag(w):
    eye = jnp.eye(RG_BLOCKS, dtype=w.dtype)
    return (w[:, :, None, :] * eye[:, None, :, None]).reshape(RG_WIDTH, RG_WIDTH)


def _tile2(g):
    return jnp.concatenate([g, g]).reshape(1, LANES)


def kernel(x_prompt, x_sample, cache_swa_k, cache_swa_v, cache_na_k, cache_na_v, state_mlstm_C, state_mlstm_n, state_mlstm_m, state_rglru_h, c, c_ctx, norm1_g, norm2_g, w_ada, b_ada, w_in, ml_b_i, ml_b_f, ml_hn, sw_qn, sw_kn, sw_sink, rg_conv_w, rg_conv_b, rg_w_r, rg_b_r, rg_w_i, rg_b_i, rg_lam, na_qn, na_kn, na_rpb, w_br, w_mg, b_mg, w_out, ffn_w1, ffn_w3, ffn_w2, moe_wr, moe_br, moe_w1, moe_w3, moe_w2):
    assert DEPTH % 2 == 0
    x_all = (x_prompt.reshape(SEG, D_MODEL), x_sample.reshape(N_TOK - SEG, D_MODEL))
    cvecs = jnp.concatenate([c_ctx[None, :], c, jnp.zeros((8 - 1 - DEC_BATCH, D_MODEL), F32)], axis=0)
    mod = _mod_table(cvecs.T, w_ada, b_ada)
    cos_t, sin_t = _rope_tables()
    nj = 2 * ML_HEADS
    zeros_state = (jnp.zeros((BATCH, nj // 2, LANES, LANES), F32), jnp.zeros((BATCH, nj // 2, LANES, LANES), F32),
                   jnp.zeros((BATCH, nj, LANES), F32), jnp.zeros((BATCH // RG_NSEG, 2, RG_NSEG, RG_WIDTH), F32))
    ctx_out = []
    for l in range(DEPTH):
        mod_l = mod[l].reshape(8, 1, 6 * D_MODEL)
        qk_gains = jnp.stack([_tile2(sw_qn[l])[0], _tile2(sw_kn[l])[0], _tile2(na_qn[l])[0], _tile2(na_kn[l])[0]])
        proj = _inproj(x_all, mod_l, norm1_g[l].reshape(1, D_MODEL), _permute_w_in(w_in[l]).astype(BF16),
                       qk_gains, cos_t, sin_t)
        gate_bias = jnp.concatenate([ml_b_i[l].reshape(-1), ml_b_f[l].reshape(-1),
                                     jnp.zeros((LANES - 2 * nj,), F32)]).reshape(1, LANES)
        hf_c, hb_c, *st_new = _mlstm(proj, gate_bias, *zeros_state[:3], bsz=BATCH, seq=SEQ, row0=0)
        c_new, n_new, m_new = _mlstm_unpack_state(*st_new)
        st_lat = _mlstm_pack_state(state_mlstm_C[:, l].reshape(DEC_BATCH, nj, HEAD_DIM, HEAD_DIM),
                                   state_mlstm_n[:, l].reshape(DEC_BATCH, nj, HEAD_DIM),
                                   state_mlstm_m[:, l].reshape(DEC_BATCH, nj))
        hf_l, hb_l, _, _, _ = _mlstm(proj, gate_bias, *st_lat, bsz=DEC_BATCH, seq=DEC_SEQ, row0=SEG)
        wg = jnp.concatenate([_block_diag(rg_w_r[l, 0]), _block_diag(rg_w_i[l, 0]),
                              _block_diag(rg_w_r[l, 1]), _block_diag(rg_w_i[l, 1])], axis=1).astype(BF16)
        bg = jnp.concatenate([rg_b_r[l, 0], rg_b_i[l, 0], rg_b_r[l, 1], rg_b_i[l, 1]]).reshape(1, 4 * RG_WIDTH)
        rg_args = (rg_conv_w[l], rg_conv_b[l].reshape(1, RG_WIDTH), wg, bg, rg_lam[l])
        oc_c, hl_c = _rglru(proj, *rg_args, zeros_state[3], nblk=BATCH // RG_NSEG, seg_len=SEQ, chained=False, row0=0)
        hl_new = jnp.transpose(hl_c, (0, 2, 1, 3)).reshape(BATCH, 2, RG_WIDTH)
        h0_lat = jnp.broadcast_to(state_rglru_h[:, l][:, :, None, :], (DEC_BATCH, 2, RG_NSEG, RG_WIDTH))
        oc_l, _ = _rglru(proj, *rg_args, h0_lat, nblk=DEC_BATCH, seg_len=DEC_SEQ // RG_NSEG, chained=True, row0=SEG)
        ob_c, od_c = _ctx_attn(proj, sw_sink[l])
        ob_l = _swa(proj, cache_swa_k[:, l].reshape(DEC_BATCH, PAST_LEN, 128),
                    cache_swa_v[:, l].reshape(DEC_BATCH, PAST_LEN, 128), sw_sink[l])
        od_l = _na(proj, cache_na_k[:, l].reshape(DEC_BATCH, PAST_LEN, 256),
                   cache_na_v[:, l].reshape(DEC_BATCH, PAST_LEN, 256), _na_bias(na_rpb[l].reshape(-1)))
        moe_layer = l % 2 == 1
        j = l // 2
        router = (moe_wr[j].T, moe_br[j]) if moe_layer else None
        outs = _merge(x_all, mod_l, norm1_g[l].reshape(1, D_MODEL), norm2_g[l].reshape(1, D_MODEL),
                      (hf_c, hf_l), (hb_c, hb_l), proj, (ob_c, ob_l), (oc_c, oc_l), (od_c, od_l),
                      _tile2(ml_hn[l]), w_mg[l].astype(BF16), b_mg[l].reshape(1, -1), w_br[l].astype(BF16),
                      w_out[l].astype(BF16), router)
        if moe_layer:
            x1, h2, route = outs
            plan = _moe_plan(route[:, 2:4].astype(jnp.int32))
            y_slots = _moe_group(*plan, h2, moe_w1[j].astype(BF16), moe_w3[j].astype(BF16), moe_w2[j].astype(BF16))
            x_all = tuple(_moe_combine(x1, y_slots, route, mod_l))
        else:
            x1, h2 = outs
            x_all = _ffn(h2, x1, mod_l, ffn_w1[j].astype(BF16), ffn_w3[j].astype(BF16), ffn_w2[j].astype(BF16))
        pc = proj[:SEG]
        ctx_out.append(dict(
            sw_k=pc[:, C_SK:C_SK + 128].reshape(BATCH, SEQ, SW_KV_HEADS, HEAD_DIM),
            sw_v=pc[:, C_SV:C_SV + 128].reshape(BATCH, SEQ, SW_KV_HEADS, HEAD_DIM),
            na_k=pc[:, C_NK:C_NK + 256].reshape(BATCH, SEQ, NA_HEADS, HEAD_DIM),
            na_v=pc[:, C_NV:C_NV + 256].reshape(BATCH, SEQ, NA_HEADS, HEAD_DIM),
            ml_C=c_new.reshape(BATCH, 2, ML_HEADS, HEAD_DIM, HEAD_DIM),
            ml_n=n_new.reshape(BATCH, 2, ML_HEADS, HEAD_DIM),
            ml_m=m_new.reshape(BATCH, 2, ML_HEADS),
            rg_h=hl_new))
    stack = lambda name: jnp.stack([t[name] for t in ctx_out], axis=1)
    return (x_all[0].reshape(BATCH, SEQ, D_MODEL), x_all[1].reshape(DEC_BATCH, DEC_SEQ, D_MODEL),
            stack('sw_k'), stack('sw_v'), stack('na_k'), stack('na_v'),
            stack('ml_C'), stack('ml_n'), stack('ml_m'), stack('rg_h'))
```

```python
import functools

import numpy as np
import jax
import jax.numpy as jnp
from jax import lax
from jax.experimental import pallas as pl
from jax.experimental.pallas import tpu as pltpu

F32 = jnp.float32
BF16 = jnp.bfloat16

D_MODEL = 1024
BATCH = 16
SEQ = 256
DEPTH = 2
DEC_BATCH = 2
DEC_SEQ = 4096
PAST_LEN = 256
GRID_W = 64
HEAD_DIM = 64
ML_HEADS = 4
ML_CHUNK = 128
ML_SUB = 2
SW_HEADS = 4
SW_KV_HEADS = 2
SW_WINDOW = 128
RG_WIDTH = 256
RG_BLOCKS = 4
RG_CONV = 4
RG_C = 8.0
NA_HEADS = 4
NA_ROWS = 8
NA_COLS = 16
N_BRANCH = 4
ROPE_BASE = 10000.0
D_FF = 2816
N_EXPERTS = 8
D_FF_EXPERT = 2048
EPS = 1e-6
NEG = -1e30
SCALE = HEAD_DIM ** -0.5

SEG = 4096
N_SEG = 3
N_TOK = N_SEG * SEG
LANES = 128
VMEM_LIMIT = 56 * 1024 * 1024

C_MQ, C_MK, C_MV, C_MO = 0, 256, 512, 768
C_SQ, C_RX, C_RY, C_NQ, C_NK, C_NV = 1024, 1280, 1536, 1792, 2048, 2304
C_SK, C_SV, C_G = 2560, 2688, 2816
P_W = 2944


def _cparams(sem):
    return pltpu.CompilerParams(dimension_semantics=sem, vmem_limit_bytes=VMEM_LIMIT)


def _dot(a, b):
    return jnp.dot(a.astype(BF16), b.astype(BF16), preferred_element_type=F32)


def _dot_nt(a, b):
    return lax.dot_general(a.astype(BF16), b.astype(BF16), (((1,), (1,)), ((), ())),
                           preferred_element_type=F32)


def _dot_tn(a, b):
    return lax.dot_general(a.astype(BF16), b.astype(BF16), (((0,), (0,)), ((), ())),
                           preferred_element_type=F32)


def _split3(x):
    hi = x.astype(BF16)
    r1 = x - hi.astype(F32)
    mid = r1.astype(BF16)
    lo = (r1 - mid.astype(F32)).astype(BF16)
    return hi, mid, lo


def _dot_exact_rhs(a01, x):
    hi, mid, lo = _split3(x)
    d = lambda p: jnp.dot(a01, p, preferred_element_type=F32)
    return d(hi) + d(mid) + d(lo)


def _dot_exact_lhs(x, a01):
    hi, mid, lo = _split3(x)
    d = lambda p: jnp.dot(p, a01, preferred_element_type=F32)
    return d(hi) + d(mid) + d(lo)


def _sigmoid(x):
    return 0.5 * jnp.tanh(0.5 * x) + 0.5


def _rms(x, g):
    return x * lax.rsqrt(jnp.mean(x * x, axis=-1, keepdims=True) + EPS) * g


def _rms_head_pairs(x, g):
    lane = lax.broadcasted_iota(jnp.int32, x.shape, 1)
    left = lane < HEAD_DIM
    sq = x * x
    s0 = jnp.sum(jnp.where(left, sq, 0.0), axis=-1, keepdims=True)
    s1 = jnp.sum(jnp.where(left, 0.0, sq), axis=-1, keepdims=True)
    ms = jnp.where(left, s0, s1) * (1.0 / HEAD_DIM)
    return x * lax.rsqrt(ms + EPS) * g


MOD_TN = 1536
MOD_ROWS = 3


def _mod_kernel(ct_ref, w_ref, b_ref, o_ref):
    ct = ct_ref[...]
    st = ct * jax.nn.sigmoid(ct)
    w = w_ref[0]
    o_ref[...] = jnp.zeros(o_ref.shape, F32)
    for r in range(MOD_ROWS):
        o_ref[0, r:r + 1, :] = jnp.sum(w * st[:, r:r + 1], axis=0, keepdims=True) + b_ref[0]


def _mod_table(cvecs_t, w_ada, b_ada):
    n = 6 * D_MODEL
    return pl.pallas_call(
        _mod_kernel,
        grid=(DEPTH, n // MOD_TN),
        in_specs=[pl.BlockSpec((D_MODEL, 8), lambda l, j: (0, 0)),
                  pl.BlockSpec((1, D_MODEL, MOD_TN), lambda l, j: (l, 0, j)),
                  pl.BlockSpec((1, 1, MOD_TN), lambda l, j: (l, 0, j))],
        out_specs=pl.BlockSpec((1, 8, MOD_TN), lambda l, j: (l, 0, j)),
        out_shape=jax.ShapeDtypeStruct((DEPTH, 8, n), F32),
        compiler_params=_cparams(("arbitrary", "arbitrary")),
        name="adaln_mod",
    )(cvecs_t, w_ada, b_ada.reshape(DEPTH, 1, n))


IN_TM = 512


def _swap16(y):
    lane = lax.broadcasted_iota(jnp.int32, y.shape, 1)
    first = (lane % 32) < 16
    return jnp.where(first, pltpu.roll(y, LANES - 16, 1), pltpu.roll(y, 16, 1))


def _seg_pair_specs(tm, width, lat_row0=0):
    nt = SEG // tm
    lat_off = lat_row0 // tm
    return (pl.BlockSpec((tm, width), lambda s, i: (jnp.minimum(s * nt + i, nt - 1), 0)),
            pl.BlockSpec((tm, width), lambda s, i: (lat_off + jnp.maximum(s * nt + i - nt, 0), 0)))


def _pick(c_ref, l_ref):
    return jnp.where(pl.program_id(0) == 0, c_ref[...], l_ref[...])


def _x_pair(x):
    return (x, 0) if isinstance(x, tuple) else ((x, x), SEG)


def _inproj_kernel(xc_ref, xl_ref, mod_ref, g_ref, w_ref, qkg_ref, cos_ref, sin_ref, o_ref):
    seg = pl.program_id(0)
    mod = mod_ref[0]
    sh1 = mod[:, 0:D_MODEL]
    sc1 = mod[:, D_MODEL:2 * D_MODEL]
    h = _rms(_pick(xc_ref, xl_ref), g_ref[...]) * (1.0 + sc1) + sh1
    r = jnp.dot(h.astype(BF16), w_ref[...], preferred_element_type=F32)
    o_ref[:, 0:C_SQ] = r[:, 0:C_SQ]
    o_ref[:, C_RX:C_NQ] = r[:, C_RX:C_NQ]
    o_ref[:, C_NV:C_SK] = r[:, C_NV:C_SK]
    o_ref[:, C_SV:P_W] = r[:, C_SV:P_W]
    cos = cos_ref[...]
    sin = sin_ref[...]
    latent = seg > 0

    def rope(y):
        return jnp.where(latent, y * cos + _swap16(y) * sin, y)

    for p in range(2):
        a = C_SQ + LANES * p
        o_ref[:, a:a + LANES] = rope(_rms_head_pairs(r[:, a:a + LANES], qkg_ref[0:1, :]))
    o_ref[:, C_SK:C_SK + LANES] = rope(_rms_head_pairs(r[:, C_SK:C_SK + LANES], qkg_ref[1:2, :]))
    for p in range(2):
        a = C_NQ + LANES * p
        o_ref[:, a:a + LANES] = _rms_head_pairs(r[:, a:a + LANES], qkg_ref[2:3, :])
        a = C_NK + LANES * p
        o_ref[:, a:a + LANES] = _rms_head_pairs(r[:, a:a + LANES], qkg_ref[3:4, :])


def _inproj(x, mod_l, norm1, w_in_p, qk_gains, cos_t, sin_t):
    nt = SEG // IN_TM
    x_pair, lat_row0 = _x_pair(x)
    return pl.pallas_call(
        _inproj_kernel,
        grid=(N_SEG, nt),
        in_specs=[*_seg_pair_specs(IN_TM, D_MODEL, lat_row0),
                  pl.BlockSpec((1, 1, 6 * D_MODEL), lambda s, i: (s, 0, 0)),
                  pl.BlockSpec((1, D_MODEL), lambda s, i: (0, 0)),
                  pl.BlockSpec((D_MODEL, P_W), lambda s, i: (0, 0)),
                  pl.BlockSpec((4, LANES), lambda s, i: (0, 0)),
                  pl.BlockSpec((IN_TM, LANES), lambda s, i: (i, 0)),
                  pl.BlockSpec((IN_TM, LANES), lambda s, i: (i, 0))],
        out_specs=pl.BlockSpec((IN_TM, P_W), lambda s, i: (s * nt + i, 0)),
        out_shape=jax.ShapeDtypeStruct((N_TOK, P_W), F32),
        compiler_params=_cparams(("arbitrary", "arbitrary")),
        name="inproj",
    )(*x_pair, mod_l, norm1, w_in_p, qk_gains, cos_t, sin_t)


def _mlstm_direction(d, rows, q_ref, k_ref, v_ref, g_ref, bias, tri_ref, sel_ref, state):
    ch = ML_CHUNK
    r_io = lax.broadcasted_iota(jnp.int32, (ch, ch), 0)
    c_io = lax.broadcasted_iota(jnp.int32, (ch, ch), 1)
    lower = r_io >= c_io
    upper = r_io <= c_io
    mask = lower if d == 0 else upper
    tri = tri_ref[d]
    tri_t = tri_ref[1 - d]
    left = c_io < HEAD_DIM
    top = r_io < HEAD_DIM
    blockdiag = top == left
    cbs_in, nbs_in, m_old = state
    g = g_ref[rows, :] + bias[...]
    b_cols = _dot_exact_rhs(tri, jax.nn.log_sigmoid(g))
    b3 = jnp.concatenate(_split3(b_cols), axis=1)
    gt = g.T
    li_rows = gt[0:8, :]
    b_rows = _dot_exact_lhs(jax.nn.log_sigmoid(gt[8:16, :]), tri_t)
    a_rows = li_rows - b_rows
    bl = b_rows[:, ch - 1:ch] if d == 0 else b_rows[:, 0:1]
    g_rows = bl - b_rows + li_rows
    m_new = jnp.maximum(bl + m_old, jnp.max(g_rows, axis=1, keepdims=True))
    wk_rows = jnp.exp(g_rows - m_new)
    wp = jnp.exp(bl + m_old - m_new)
    ones_blk = jnp.ones((ch, LANES), BF16)
    left2 = lax.broadcasted_iota(jnp.int32, (ch, 2 * LANES), 1) % LANES < HEAD_DIM
    h_out, c_out, n_out = [], [], []
    for p in range(ML_HEADS // 2):
        lanes = slice(LANES * p, LANES * (p + 1))
        j0 = ML_HEADS * d + 2 * p
        q2 = q_ref[rows, lanes]
        k2t = (k_ref[rows, lanes] * SCALE).T.astype(BF16)
        v2e = jnp.concatenate([v_ref[rows, lanes].astype(BF16), ones_blk], axis=1)
        cb = cbs_in[p]
        nb = nbs_in[p]
        q2b = q2.astype(BF16)
        q_lo = (q2 - q2b.astype(F32)).astype(BF16)
        nb_hi = nb.astype(BF16)
        nb_lo = (nb - nb_hi.astype(F32)).astype(BF16)
        qc = jnp.dot(q2b, cb.astype(BF16), preferred_element_type=F32)
        qn = jnp.dot(jnp.concatenate([q2b, q_lo, q2b], axis=1), jnp.concatenate([nb_hi, nb_hi, nb_lo], axis=0),
                     preferred_element_type=F32)
        b_pair = jnp.dot(b3, sel_ref[2 * d + p], preferred_element_type=F32)
        cbs, sves = [], []
        for i in range(2):
            j = j0 + i
            half = left if i == 0 else jnp.logical_not(left)
            a_mat = jnp.where(mask, a_rows[j:j + 1, :], NEG)
            cvec = jnp.maximum(m_old[j:j + 1, :], jnp.max(a_mat, axis=1, keepdims=True))
            cbro = jnp.broadcast_to(cvec, (ch, ch))
            s = jnp.dot(jnp.where(half, q2b, 0), k2t, preferred_element_type=F32) * jnp.exp(a_mat - cbro)
            s_hi = s.astype(BF16)
            s_lo = (s - s_hi.astype(F32)).astype(BF16)
            sve = jnp.dot(s_hi, v2e, preferred_element_type=F32)
            rs_lo = jnp.dot(s_lo, ones_blk, preferred_element_type=F32)
            sves.append(jnp.concatenate([sve[:, 0:LANES], sve[:, LANES:2 * LANES] + rs_lo], axis=1))
            cbs.append(cbro)
        c_pair = jnp.where(left, cbs[0], cbs[1])
        w_prev = jnp.exp(jnp.where(left, m_old[j0:j0 + 1, :], m_old[j0 + 1:j0 + 2, :]) - c_pair)
        sve = jnp.where(left2, sves[0], sves[1])
        num = w_prev * qc + sve[:, 0:LANES]
        den = w_prev * qn + sve[:, LANES:2 * LANES]
        h_out.append(num / jnp.maximum(jnp.abs(den), jnp.exp(-(c_pair + b_pair))))
        kwt = k2t * jnp.where(top, wk_rows[j0:j0 + 1, :], wk_rows[j0 + 1:j0 + 2, :])
        kwt_hi = kwt.astype(BF16)
        kwt_lo = (kwt - kwt_hi.astype(F32)).astype(BF16)
        kve = jnp.dot(kwt_hi, v2e, preferred_element_type=F32)
        kn = kve[:, LANES:2 * LANES] + jnp.dot(kwt_lo, ones_blk, preferred_element_type=F32)
        wp_pair = jnp.where(top, wp[j0:j0 + 1, :], wp[j0 + 1:j0 + 2, :])
        c_out.append(wp_pair * cb + jnp.where(blockdiag, kve[:, 0:LANES], 0.0))
        n_out.append(wp_pair * nb + jnp.where(blockdiag, kn, 0.0))
    return h_out, (c_out, n_out, m_new)


def _mlstm_kernel(qf, kf, vf, gf, qb, kb, vb, gb, c0, n0, m0, bias, tri_ref, sel_ref,
                  hf, hb, co, no, mo, cbd, nbd, m_s, *, nc):
    c = pl.program_id(1)

    @pl.when(c == 0)
    def _():
        cbd[...] = c0[0]
        nbd[...] = n0[0]
        m_s[...] = m0[0]

    npair = ML_HEADS // 2
    for d, (refs, h_ref) in enumerate((((qf, kf, vf, gf), hf), ((qb, kb, vb, gb), hb))):
        state = ([cbd[npair * d + p] for p in range(npair)], [nbd[npair * d + p] for p in range(npair)],
                 m_s[:, 0:1])
        subs = range(ML_SUB) if d == 0 else reversed(range(ML_SUB))
        for sub in subs:
            rows = slice(sub * ML_CHUNK, (sub + 1) * ML_CHUNK)
            h_out, state = _mlstm_direction(d, rows, *refs, bias, tri_ref, sel_ref, state)
            for p in range(npair):
                h_ref[rows, LANES * p:LANES * (p + 1)] = h_out[p]
        c_out, n_out, m_new = state
        for p in range(npair):
            cbd[npair * d + p] = c_out[p]
            nbd[npair * d + p] = n_out[p]
        heads = slice(ML_HEADS * d, ML_HEADS * (d + 1))
        m_s[heads, :] = jnp.broadcast_to(m_new[heads, :], (ML_HEADS, LANES))

    @pl.when(c == nc - 1)
    def _():
        co[0] = cbd[...]
        no[0] = nbd[...]
        mo[0] = m_s[...]


def _mlstm_pack_state(c0, n0, m0):
    bsz = c0.shape[0]
    hd = HEAD_DIM
    eye = jnp.eye(2, dtype=F32)[None, None, :, None, :, None]
    cbd = c0.reshape(bsz, ML_HEADS, 2, hd, 1, hd) * eye
    nbd = jnp.broadcast_to(n0.reshape(bsz, ML_HEADS, 2, hd, 1, 1) * eye, cbd.shape)
    to_mat = lambda t: t.reshape(bsz, ML_HEADS, LANES, LANES)
    return to_mat(cbd), to_mat(nbd), jnp.broadcast_to(m0[..., None], m0.shape + (LANES,))


def _mlstm_unpack_state(cbd, nbd, mrow):
    hd = HEAD_DIM
    bsz = cbd.shape[0]
    c = jnp.stack([cbd[:, :, :hd, :hd], cbd[:, :, hd:, hd:]], axis=2).reshape(bsz, 2 * ML_HEADS, hd, hd)
    n = jnp.stack([nbd[:, :, :hd, 0], nbd[:, :, hd:, hd]], axis=2).reshape(bsz, 2 * ML_HEADS, hd)
    return c, n, mrow[:, :, 0]


def _mlstm_tables():
    r = np.arange(ML_CHUNK)
    lower = (r[:, None] >= r[None, :]).astype(np.float32)
    row = np.arange(3 * LANES)[:, None] % LANES
    lane_left = np.arange(LANES)[None, :] < HEAD_DIM
    sel = [row == np.where(lane_left, 8 + ML_HEADS * d + 2 * p, 9 + ML_HEADS * d + 2 * p)
           for d in range(2) for p in range(ML_HEADS // 2)]
    return (jnp.asarray(np.stack([lower, lower.T]), dtype=BF16),
            jnp.asarray(np.stack(sel).astype(np.float32), dtype=BF16))


def _mlstm(proj, gate_bias, c0, n0, m0, *, bsz, seq, row0):
    blk = ML_SUB * ML_CHUNK
    nc = seq // blk
    tri, sel = _mlstm_tables()
    base = row0 // blk
    nj = 2 * ML_HEADS
    fw = lambda col: (lambda b, c: (base + b * nc + c, col))
    bw = lambda col: (lambda b, c: (base + b * nc + nc - 1 - c, col))
    qkv = lambda f: [pl.BlockSpec((blk, 256), f(C_MQ // 256)),
                     pl.BlockSpec((blk, 256), f(C_MK // 256)),
                     pl.BlockSpec((blk, 256), f(C_MV // 256)),
                     pl.BlockSpec((blk, LANES), f(C_G // LANES))]
    st_specs = [pl.BlockSpec((1, nj // 2, LANES, LANES), lambda b, c: (b, 0, 0, 0)),
                pl.BlockSpec((1, nj // 2, LANES, LANES), lambda b, c: (b, 0, 0, 0)),
                pl.BlockSpec((1, nj, LANES), lambda b, c: (b, 0, 0))]
    st_shapes = [jax.ShapeDtypeStruct((bsz, nj // 2, LANES, LANES), F32),
                 jax.ShapeDtypeStruct((bsz, nj // 2, LANES, LANES), F32),
                 jax.ShapeDtypeStruct((bsz, nj, LANES), F32)]
    return pl.pallas_call(
        functools.partial(_mlstm_kernel, nc=nc),
        grid=(bsz, nc),
        in_specs=qkv(fw) + qkv(bw) + st_specs + [pl.BlockSpec((1, LANES), lambda b, c: (0, 0)),
                                                 pl.BlockSpec(tri.shape, lambda b, c: (0, 0, 0)),
                                                 pl.BlockSpec(sel.shape, lambda b, c: (0, 0, 0))],
        out_specs=[pl.BlockSpec((blk, 256), lambda b, c: (b * nc + c, 0)),
                   pl.BlockSpec((blk, 256), lambda b, c: (b * nc + nc - 1 - c, 0))] + st_specs,
        out_shape=[jax.ShapeDtypeStruct((bsz * seq, 256), F32),
                   jax.ShapeDtypeStruct((bsz * seq, 256), F32)] + st_shapes,
        scratch_shapes=[pltpu.VMEM((nj // 2, LANES, LANES), F32),
                        pltpu.VMEM((nj // 2, LANES, LANES), F32),
                        pltpu.VMEM((nj, LANES), F32)],
        compiler_params=_cparams(("arbitrary", "arbitrary")),
        name="mlstm",
    )(proj, proj, proj, proj, proj, proj, proj, proj, c0, n0, m0, gate_bias, tri, sel)


RG_TC = 256
RG_PAD = 8
RG_NSEG = 8
RG_SKEW = 4


def _rglru_kernel(rx, ry, cw, cb, wg, bg, lam, h0, oc, hl, xpad, af, ab, uf, ub, *, seg_len, chained):
    rows = RG_NSEG * seg_len
    seq_len = rows if chained else seg_len
    halves = RG_WIDTH // LANES
    pitch = seg_len + RG_SKEW
    buf_row = lambda t: (t // seg_len) * pitch + t % seg_len
    xpad[0:RG_PAD, :] = jnp.zeros((RG_PAD, RG_WIDTH), F32)
    xpad[rows + RG_PAD:rows + 2 * RG_PAD, :] = jnp.zeros((RG_PAD, RG_WIDTH), F32)
    xpad[RG_PAD:rows + RG_PAD, :] = rx[...]
    sp = jax.nn.softplus(-lam[...])
    left = (RG_CONV - 1) // 2
    for ci in range(rows // RG_TC):
        s0 = ci * RG_TC
        pos = (s0 + lax.broadcasted_iota(jnp.int32, (RG_TC, RG_WIDTH), 0)) % seq_len
        xc = None
        for j in range(RG_CONV):
            a = RG_PAD + s0 + j - left
            term = xpad[a:a + RG_TC, :] * cw[j:j + 1, :]
            if not chained and j != left:
                term = jnp.where((pos + (j - left) >= 0) & (pos + (j - left) < seq_len), term, 0.0)
            xc = term if xc is None else xc + term
        xc = xc + cb[...]
        pre = _dot(xc, wg[...]) + bg[...]
        for d, (a_ref, u_ref) in enumerate(((af, uf), (ab, ub))):
            o = 2 * RG_WIDTH * d
            r = _sigmoid(pre[:, o:o + RG_WIDTH])
            gi = _sigmoid(pre[:, o + RG_WIDTH:o + 2 * RG_WIDTH])
            log_a = -RG_C * r * sp[d:d + 1, :]
            a_val = jnp.exp(log_a)
            u_val = jnp.sqrt(-jnp.tanh(log_a) * (a_val * a_val + 1.0)) * (gi * xc)
            for hv in range(halves):
                dst = slice(buf_row(s0), buf_row(s0) + RG_TC)
                a_ref[hv, dst, :] = a_val[:, LANES * hv:LANES * (hv + 1)]
                u_ref[hv, dst, :] = u_val[:, LANES * hv:LANES * (hv + 1)]

    def body(s, carry):
        out = []
        for d, (a_ref, u_ref) in enumerate(((af, uf), (ab, ub))):
            step_rows = pl.ds(s if d == 0 else seg_len - 1 - s, RG_NSEG, stride=pitch)
            for hv in range(halves):
                h_loc, prod = carry[2 * (halves * d + hv)], carry[2 * (halves * d + hv) + 1]
                a = a_ref[hv, step_rows, :]
                h_loc = a * h_loc + u_ref[hv, step_rows, :]
                prod = a * prod
                u_ref[hv, step_rows, :] = h_loc
                a_ref[hv, step_rows, :] = prod
                out += [h_loc, prod]
        return tuple(out)

    zero = jnp.zeros((RG_NSEG, LANES), F32)
    one = jnp.ones((RG_NSEG, LANES), F32)
    ends = lax.fori_loop(0, seg_len, body, (zero, one) * (2 * halves), unroll=8)
    seg = lax.broadcasted_iota(jnp.int32, (RG_NSEG, LANES), 0)
    for d, (a_ref, u_ref) in enumerate(((af, uf), (ab, ub))):
        for hv in range(halves):
            lanes = slice(LANES * hv, LANES * (hv + 1))
            h_end, p_end = ends[2 * (halves * d + hv)], ends[2 * (halves * d + hv) + 1]
            h_in = h0[0, d][:, lanes]
            if chained:
                c = h_in[0:1, :]
                h_in = zero
                for k in (range(RG_NSEG) if d == 0 else reversed(range(RG_NSEG))):
                    h_in = jnp.where(seg == k, c, h_in)
                    c = h_end[k:k + 1, :] + p_end[k:k + 1, :] * c
            hl[0, d, :, lanes] = h_end + p_end * h_in
            for k in range(RG_NSEG):
                for ci in range(seg_len // RG_TC):
                    sl = slice(k * pitch + ci * RG_TC, k * pitch + (ci + 1) * RG_TC)
                    u_ref[hv, sl, :] = u_ref[hv, sl, :] + a_ref[hv, sl, :] * h_in[k:k + 1, :]
    for ci in range(rows // RG_TC):
        sl = slice(ci * RG_TC, (ci + 1) * RG_TC)
        for hv in range(halves):
            lanes = slice(LANES * hv, LANES * (hv + 1))
            src = slice(buf_row(ci * RG_TC), buf_row(ci * RG_TC) + RG_TC)
            oc[sl, lanes] = (uf[hv, src, :] + ub[hv, src, :]) * jax.nn.gelu(ry[sl, lanes])


def _rglru(proj, cw, cb, wg, bg, lam, h0, *, nblk, seg_len, chained, row0):
    rows = RG_NSEG * seg_len
    base = row0 // rows
    full = lambda shape: pl.BlockSpec(shape, lambda b: tuple(0 for _ in shape))
    st_spec = pl.BlockSpec((1, 2, RG_NSEG, RG_WIDTH), lambda b: (b, 0, 0, 0))
    return pl.pallas_call(
        functools.partial(_rglru_kernel, seg_len=seg_len, chained=chained),
        grid=(nblk,),
        in_specs=[pl.BlockSpec((rows, RG_WIDTH), lambda b: (base + b, C_RX // RG_WIDTH)),
                  pl.BlockSpec((rows, RG_WIDTH), lambda b: (base + b, C_RY // RG_WIDTH)),
                  full((RG_CONV, RG_WIDTH)), full((1, RG_WIDTH)),
                  full((RG_WIDTH, 4 * RG_WIDTH)), full((1, 4 * RG_WIDTH)), full((2, RG_WIDTH)),
                  st_spec],
        out_specs=[pl.BlockSpec((rows, RG_WIDTH), lambda b: (b, 0)), st_spec],
        out_shape=[jax.ShapeDtypeStruct((nblk * rows, RG_WIDTH), F32),
                   jax.ShapeDtypeStruct((nblk, 2, RG_NSEG, RG_WIDTH), F32)],
        scratch_shapes=[pltpu.VMEM((rows + 2 * RG_PAD, RG_WIDTH), F32)]
        + [pltpu.VMEM((RG_WIDTH // LANES, RG_NSEG * (seg_len + RG_SKEW), LANES), F32) for _ in range(4)],
        compiler_params=_cparams(("arbitrary",)),
        name="rglru",
    )(proj, proj, cw, cb, wg, bg, lam, h0)


def _softmax_pv(scores, values, sink):
    m = functools.reduce(jnp.maximum, [jnp.max(s, axis=-1, keepdims=True) for s in scores])
    if sink is not None:
        m = jnp.maximum(m, sink)
    ps = [jnp.exp(s - m) for s in scores]
    den = functools.reduce(jnp.add, [jnp.sum(p, axis=-1, keepdims=True) for p in ps])
    if sink is not None:
        den = den + jnp.exp(sink - m)
    num = functools.reduce(jnp.add, [_dot(p, v) for p, v in zip(ps, values)])
    return num / den


def _ctx_attn_kernel(sink, sq, sk, sv, nq, nk, nv, ob, od):
    assert SW_HEADS == 4 and SW_KV_HEADS == 2
    left = lax.broadcasted_iota(jnp.int32, (SEQ, LANES), 1) < HEAD_DIM
    first = lax.broadcasted_iota(jnp.int32, (2 * SEQ, 1), 0) < SEQ
    k2 = sk[...].astype(BF16)
    v2 = sv[...].astype(BF16)
    for p in range(SW_KV_HEADS):
        lanes = slice(LANES * p, LANES * (p + 1))
        q2 = sq[:, lanes]
        q_swapped = pltpu.roll(q2, HEAD_DIM, 1)
        kv_half = left if p == 0 else jnp.logical_not(left)
        qs = jnp.concatenate([jnp.where(kv_half, q2 if i == p else q_swapped, 0.0) for i in range(2)], axis=0)
        sink_col = jnp.where(first, sink[2 * p], sink[2 * p + 1])
        res = _softmax_pv([_dot_nt(qs, k2) * SCALE], [v2], sink_col)
        halves = [res[0:SEQ], res[SEQ:2 * SEQ]]
        placed = [halves[i] if i == p else pltpu.roll(halves[i], HEAD_DIM, 1) for i in range(2)]
        ob[:, lanes] = jnp.where(left, placed[0], placed[1])
    for p in range(NA_HEADS // 2):
        lanes = slice(LANES * p, LANES * (p + 1))
        q2 = nq[:, lanes]
        k2 = nk[:, lanes].astype(BF16)
        v2 = nv[:, lanes].astype(BF16)
        res = [_softmax_pv([_dot_nt(jnp.where(left if i == 0 else jnp.logical_not(left), q2, 0.0), k2) * SCALE],
                           [v2], None) for i in range(2)]
        od[:, lanes] = jnp.where(left, res[0], res[1])


def _ctx_attn(proj, sink):
    blk = lambda w, col: pl.BlockSpec((SEQ, w), lambda b: (b, col))
    return pl.pallas_call(
        _ctx_attn_kernel,
        grid=(BATCH,),
        in_specs=[pl.BlockSpec(memory_space=pltpu.SMEM),
                  blk(256, C_SQ // 256), blk(128, C_SK // 128), blk(128, C_SV // 128),
                  blk(256, C_NQ // 256), blk(256, C_NK // 256), blk(256, C_NV // 256)],
        out_specs=[pl.BlockSpec((SEQ, 256), lambda b: (b, 0)),
                   pl.BlockSpec((SEQ, 256), lambda b: (b, 0))],
        out_shape=[jax.ShapeDtypeStruct((SEG, 256), F32), jax.ShapeDtypeStruct((SEG, 256), F32)],
        compiler_params=_cparams(("arbitrary",)),
        name="ctx_attn",
    )(sink, proj, proj, proj, proj, proj, proj)


SW_QB = 128
SW_SPAN = SW_QB + 2 * SW_WINDOW


def _swa_kernel(sink, q, k, v, kc, vc, ob):
    assert SW_HEADS == 4 and SW_KV_HEADS == 2
    n = pl.program_id(1)
    ws = jnp.clip((n - 1) * SW_QB, 0, DEC_SEQ - SW_SPAN)
    ws = pl.multiple_of(ws, SW_QB)
    row = lax.broadcasted_iota(jnp.int32, (2 * SW_QB, SW_SPAN), 0)
    qpos = n * SW_QB + row % SW_QB
    kpos = ws + lax.broadcasted_iota(jnp.int32, (2 * SW_QB, SW_SPAN), 1)
    valid = jnp.abs(qpos - kpos) <= SW_WINDOW
    left = lax.broadcasted_iota(jnp.int32, (SW_QB, LANES), 1) < HEAD_DIM
    first = lax.broadcasted_iota(jnp.int32, (2 * SW_QB, 1), 0) < SW_QB
    k2 = k[pl.ds(ws, SW_SPAN), :].astype(BF16)
    v2 = v[pl.ds(ws, SW_SPAN), :].astype(BF16)
    kc2 = kc[0].astype(BF16)
    vc2 = vc[0].astype(BF16)
    for p in range(SW_KV_HEADS):
        lanes = slice(LANES * p, LANES * (p + 1))
        q2 = q[:, lanes]
        q_swapped = pltpu.roll(q2, HEAD_DIM, 1)
        kv_half = left if p == 0 else jnp.logical_not(left)
        qs = jnp.concatenate([jnp.where(kv_half, q2 if i == p else q_swapped, 0.0) for i in range(2)], axis=0)
        s_loc = jnp.where(valid, _dot_nt(qs, k2) * SCALE, NEG)
        s_ctx = _dot_nt(qs, kc2) * SCALE
        sink_col = jnp.where(first, sink[2 * p], sink[2 * p + 1])
        res = _softmax_pv([s_loc, s_ctx], [v2, vc2], sink_col)
        halves = [res[0:SW_QB], res[SW_QB:2 * SW_QB]]
        placed = [halves[i] if i == p else pltpu.roll(halves[i], HEAD_DIM, 1) for i in range(2)]
        ob[:, lanes] = jnp.where(left, placed[0], placed[1])


def _swa(proj, kc, vc, sink):
    nq = DEC_SEQ // SW_QB
    qbase = SEG // SW_QB
    return pl.pallas_call(
        _swa_kernel,
        grid=(DEC_BATCH, nq),
        in_specs=[pl.BlockSpec(memory_space=pltpu.SMEM),
                  pl.BlockSpec((SW_QB, 256), lambda b, n: (qbase + b * nq + n, C_SQ // 256)),
                  pl.BlockSpec((DEC_SEQ, 128), lambda b, n: (1 + b, C_SK // 128)),
                  pl.BlockSpec((DEC_SEQ, 128), lambda b, n: (1 + b, C_SV // 128)),
                  pl.BlockSpec((1, PAST_LEN, 128), lambda b, n: (b, 0, 0)),
                  pl.BlockSpec((1, PAST_LEN, 128), lambda b, n: (b, 0, 0))],
        out_specs=pl.BlockSpec((SW_QB, 256), lambda b, n: (b * nq + n, 0)),
        out_shape=jax.ShapeDtypeStruct((DEC_BATCH * DEC_SEQ, 256), F32),
        compiler_params=_cparams(("arbitrary", "arbitrary")),
        name="swa",
    )(sink, proj, proj, proj, kc, vc)


NA_RPB_R = 2 * NA_ROWS - 1
NA_RPB_C = 2 * NA_COLS - 1
GRID_ROWS = DEC_SEQ // GRID_W
NA_RB = 4
NA_UW = 12
NA_NQ = NA_RB * GRID_W
NA_NKEY = NA_UW * GRID_W
NA_NBLK = GRID_ROWS // NA_RB
NA_CASES = ((0, 0), (NA_RB, 0), (GRID_ROWS - NA_RB, GRID_ROWS - NA_UW))


def _na_row_start(qrow):
    return min(max(qrow - NA_ROWS // 2, 0), GRID_ROWS - NA_ROWS)


def _na_bias_kernel(rpb, out):
    h = pl.program_id(0)
    qc = lax.broadcasted_iota(jnp.int32, (GRID_W, GRID_W), 0)
    kc = lax.broadcasted_iota(jnp.int32, (GRID_W, GRID_W), 1)
    dc = jnp.clip(kc - qc, -(NA_COLS - 1), NA_COLS - 1) + NA_COLS - 1
    lo = jnp.clip(qc - NA_COLS // 2, 0, GRID_W - NA_COLS)
    valid = (kc >= lo) & (kc < lo + NA_COLS)
    tiles = []
    for dr in range(NA_RPB_R):
        t = jnp.zeros((GRID_W, GRID_W), F32)
        for j in range(NA_RPB_C):
            t = jnp.where(dc == j, rpb[(h * NA_RPB_R + dr) * NA_RPB_C + j], t)
        tiles.append(jnp.where(valid, t, NEG))
    outside = jnp.full((GRID_W, GRID_W), NEG, F32)
    for case, (q0, k0) in enumerate(NA_CASES):
        for a in range(NA_RB):
            r0 = _na_row_start(q0 + a)
            for i in range(NA_UW):
                inside = r0 <= k0 + i < r0 + NA_ROWS
                tile = tiles[k0 + i - (q0 + a) + NA_ROWS - 1] if inside else outside
                out[0, case, GRID_W * a:GRID_W * (a + 1), GRID_W * i:GRID_W * (i + 1)] = tile


def _na_bias(rpb_flat):
    shape = (NA_HEADS, len(NA_CASES), NA_NQ, NA_NKEY)
    return pl.pallas_call(
        _na_bias_kernel,
        grid=(NA_HEADS,),
        in_specs=[pl.BlockSpec(memory_space=pltpu.SMEM)],
        out_specs=pl.BlockSpec((1,) + shape[1:], lambda h: (h, 0, 0, 0)),
        out_shape=jax.ShapeDtypeStruct(shape, F32),
        compiler_params=_cparams(("arbitrary",)),
        name="na_bias",
    )(rpb_flat)


def _na_kernel(q, k, v, kc, vc, bias, od):
    blk = pl.program_id(1)
    case = jnp.where(blk == 0, 0, jnp.where(blk == NA_NBLK - 1, 2, 1))
    u0 = jnp.clip(blk * NA_RB - NA_ROWS // 2, 0, GRID_ROWS - NA_UW)
    k0 = pl.multiple_of(u0 * GRID_W, GRID_W)
    left = lax.broadcasted_iota(jnp.int32, (NA_NQ, LANES), 1) < HEAD_DIM
    for p in range(NA_HEADS // 2):
        lanes = slice(LANES * p, LANES * (p + 1))
        q2 = q[:, lanes]
        k2 = k[pl.ds(k0, NA_NKEY), lanes].astype(BF16)
        v2 = v[pl.ds(k0, NA_NKEY), lanes].astype(BF16)
        kc2 = kc[0, :, lanes].astype(BF16)
        vc2 = vc[0, :, lanes].astype(BF16)
        res = []
        for i in range(2):
            qm = jnp.where(left if i == 0 else jnp.logical_not(left), q2, 0.0)
            s_loc = _dot_nt(qm, k2) * SCALE + bias[2 * p + i, pl.ds(case, 1)][0]
            s_ctx = _dot_nt(qm, kc2) * SCALE
            res.append(_softmax_pv([s_loc, s_ctx], [v2, vc2], None))
        od[:, lanes] = jnp.where(left, res[0], res[1])


def _na(proj, kc, vc, bias):
    qbase = SEG // NA_NQ
    return pl.pallas_call(
        _na_kernel,
        grid=(DEC_BATCH, NA_NBLK),
        in_specs=[pl.BlockSpec((NA_NQ, 256), lambda b, r: (qbase + b * NA_NBLK + r, C_NQ // 256)),
                  pl.BlockSpec((DEC_SEQ, 256), lambda b, r: (1 + b, C_NK // 256)),
                  pl.BlockSpec((DEC_SEQ, 256), lambda b, r: (1 + b, C_NV // 256)),
                  pl.BlockSpec((1, PAST_LEN, 256), lambda b, r: (b, 0, 0)),
                  pl.BlockSpec((1, PAST_LEN, 256), lambda b, r: (b, 0, 0)),
                  pl.BlockSpec((NA_HEADS, len(NA_CASES), NA_NQ, NA_NKEY), lambda b, r: (0, 0, 0, 0))],
        out_specs=pl.BlockSpec((NA_NQ, 256), lambda b, r: (b * NA_NBLK + r, 0)),
        out_shape=jax.ShapeDtypeStruct((DEC_BATCH * DEC_SEQ, 256), F32),
        compiler_params=_cparams(("arbitrary", "arbitrary")),
        name="na",
    )(proj, proj, proj, kc, vc, bias)


MG_TM = 256


def _merge_kernel(xc_ref, xl_ref, mod_ref, g1_ref, g2_ref, hf_c, hf_l, hb_c, hb_l, mo, ob_c, ob_l, oc_c, oc_l,
                  od_c, od_l, hn, wmg, bmg, wbr, wout, *rest, moe):
    if moe:
        wrt, br, x1_ref, h2_ref, route_ref = rest
    else:
        x1_ref, h2_ref = rest
    pick = _pick
    mod = mod_ref[0]
    chunk = lambda i: mod[:, i * D_MODEL:(i + 1) * D_MODEL]
    sh1, sc1, gate1, sh2, sc2 = chunk(0), chunk(1), chunk(2), chunk(3), chunk(4)
    x = pick(xc_ref, xl_ref)
    h = (_rms(x, g1_ref[...]) * (1.0 + sc1) + sh1).astype(BF16)
    hsum = pick(hf_c, hf_l) + pick(hb_c, hb_l)
    out_a = jnp.concatenate(
        [_rms_head_pairs(hsum[:, LANES * p:LANES * (p + 1)], hn[...]) for p in range(2)], axis=-1)
    out_a = out_a * jax.nn.sigmoid(mo[...])
    acc = None
    for n, br_val in enumerate((out_a, pick(ob_c, ob_l), pick(oc_c, oc_l), pick(od_c, od_l))):
        gate = jax.nn.sigmoid(jnp.dot(h, wmg[:, n * D_MODEL:(n + 1) * D_MODEL], preferred_element_type=F32)
                              + bmg[:, n * D_MODEL:(n + 1) * D_MODEL])
        term = gate * jnp.dot(br_val.astype(BF16), wbr[n], preferred_element_type=F32)
        acc = term if acc is None else acc + term
    y = jnp.dot(acc.astype(BF16), wout[...], preferred_element_type=F32)
    x1 = x + gate1 * y
    x1_ref[...] = x1
    h2 = _rms(x1, g2_ref[...]) * (1.0 + sc2) + sh2
    h2_ref[...] = h2.astype(h2_ref.dtype)
    if moe:
        logit = [jnp.sum(h2 * wrt[e:e + 1, :], axis=-1, keepdims=True) + br[e] for e in range(N_EXPERTS)]
        v1, i1 = logit[0], jnp.zeros(logit[0].shape, jnp.int32)
        for e in range(1, N_EXPERTS):
            better = logit[e] > v1
            v1 = jnp.where(better, logit[e], v1)
            i1 = jnp.where(better, e, i1)
        v2, i2 = jnp.full(v1.shape, -jnp.inf, F32), jnp.zeros(v1.shape, jnp.int32)
        for e in range(N_EXPERTS):
            better = (i1 != e) & (logit[e] > v2)
            v2 = jnp.where(better, logit[e], v2)
            i2 = jnp.where(better, e, i2)
        e2 = jnp.exp(v2 - v1)
        den = 1.0 + e2
        lane = lax.broadcasted_iota(jnp.int32, route_ref.shape, 1)
        route = jnp.where(lane == 0, 1.0 / den, 0.0) + jnp.where(lane == 1, e2 / den, 0.0)
        route = route + jnp.where(lane == 2, i1.astype(F32), 0.0) + jnp.where(lane == 3, i2.astype(F32), 0.0)
        route_ref[...] = route


def _merge(x, mod_l, g1, g2, hf, hb, proj, ob, oc, od, hn, wmg, bmg, wbr, wout, router=None):
    nt = SEG // MG_TM
    moe = router is not None
    x_pair, lat_row0 = _x_pair(x)
    row = lambda w: pl.BlockSpec((MG_TM, w), lambda s, i: (s * nt + i, 0))
    ctx_blk, lat_blk = _seg_pair_specs(MG_TM, 256)
    full = lambda shape: pl.BlockSpec(shape, lambda s, i: tuple(0 for _ in shape))
    in_specs = [*_seg_pair_specs(MG_TM, D_MODEL, lat_row0),
                pl.BlockSpec((1, 1, 6 * D_MODEL), lambda s, i: (s, 0, 0)),
                full((1, D_MODEL)), full((1, D_MODEL)),
                ctx_blk, lat_blk, ctx_blk, lat_blk,
                pl.BlockSpec((MG_TM, 256), lambda s, i: (s * nt + i, C_MO // 256)),
                ctx_blk, lat_blk, ctx_blk, lat_blk, ctx_blk, lat_blk,
                full((1, LANES)), full((D_MODEL, N_BRANCH * D_MODEL)), full((1, N_BRANCH * D_MODEL)),
                full((N_BRANCH, 256, D_MODEL)), full((D_MODEL, D_MODEL))]
    args = [*x_pair, mod_l, g1, g2, *hf, *hb, proj, *ob, *oc, *od, hn, wmg, bmg, wbr, wout]
    out_specs = [row(D_MODEL), row(D_MODEL)]
    out_shape = [jax.ShapeDtypeStruct((N_TOK, D_MODEL), F32),
                 jax.ShapeDtypeStruct((N_TOK, D_MODEL), F32 if moe else BF16)]
    if moe:
        in_specs += [full((N_EXPERTS, D_MODEL)), pl.BlockSpec(memory_space=pltpu.SMEM)]
        args += list(router)
        out_specs.append(row(LANES))
        out_shape.append(jax.ShapeDtypeStruct((N_TOK, LANES), F32))
    return pl.pallas_call(
        functools.partial(_merge_kernel, moe=moe),
        grid=(N_SEG, nt),
        in_specs=in_specs,
        out_specs=out_specs,
        out_shape=out_shape,
        compiler_params=_cparams(("arbitrary", "arbitrary")),
        name="merge",
    )(*args)


FF_TM = 512
FF_TF = 1408


def _ffn_kernel(h2, w1, w3, w2, x1, mod_ref, out, acc):
    f = pl.program_id(1)

    @pl.when(f == 0)
    def _():
        acc[...] = jnp.zeros(acc.shape, F32)

    h = h2[...]
    a = jnp.dot(h, w1[...], preferred_element_type=F32)
    b = jnp.dot(h, w3[...], preferred_element_type=F32)
    act = (jax.nn.silu(a) * b).astype(BF16)
    acc[...] += jnp.dot(act, w2[...], preferred_element_type=F32)

    @pl.when(f == pl.num_programs(1) - 1)
    def _():
        gate2 = mod_ref[0][:, 5 * D_MODEL:6 * D_MODEL]
        out[...] = x1[...] + gate2 * acc[...]


def _ffn(h2, x1, mod_l, w1, w3, w2):
    nt = N_TOK // FF_TM
    per_seg = SEG // FF_TM
    return pl.pallas_call(
        _ffn_kernel,
        grid=(nt, D_FF // FF_TF),
        in_specs=[pl.BlockSpec((FF_TM, D_MODEL), lambda i, f: (i, 0)),
                  pl.BlockSpec((D_MODEL, FF_TF), lambda i, f: (0, f)),
                  pl.BlockSpec((D_MODEL, FF_TF), lambda i, f: (0, f)),
                  pl.BlockSpec((FF_TF, D_MODEL), lambda i, f: (f, 0)),
                  pl.BlockSpec((FF_TM, D_MODEL), lambda i, f: (i, 0)),
                  pl.BlockSpec((1, 1, 6 * D_MODEL), lambda i, f: (i // per_seg, 0, 0))],
        out_specs=pl.BlockSpec((FF_TM, D_MODEL), lambda i, f: (i, 0)),
        out_shape=jax.ShapeDtypeStruct((N_TOK, D_MODEL), F32),
        scratch_shapes=[pltpu.VMEM((FF_TM, D_MODEL), F32)],
        compiler_params=_cparams(("arbitrary", "arbitrary")),
        name="ffn",
    )(h2, w1, w3, w2, x1, mod_l)


MOE_TM = 256
MOE_SLOTS = 2 * N_TOK
MOE_TILES = MOE_SLOTS // MOE_TM + N_EXPERTS
MOE_NBUF = 3
MOE_STEPS = MOE_TILES + MOE_NBUF
MOE_DUMP = MOE_NBUF * MOE_TM
MOE_LEAD = 1
MOE_PLAN_TILES = MOE_LEAD + MOE_TILES + 2
MOE_FCHUNKS = 1
MOE_UNROLL = 8


def _moe_group_kernel(texp, nused, src_tok, dst_row, h2_hbm, w1, w3, w2, y_hbm, xs, ys, sem_in, sem_out):
    del texp
    i = pl.program_id(0)
    n_used = nused[0]
    buf = i % MOE_NBUF
    buf_next = (i + 2) % MOE_NBUF

    def gather_copy(tile, b, r):
        tok = src_tok[(tile + MOE_LEAD) * MOE_TM + r]
        return pltpu.make_async_copy(h2_hbm.at[pl.ds(tok, 1)], xs.at[b, pl.ds(r, 1)], sem_in.at[b])

    def scatter_copy(tile, b, r):
        dst = dst_row[(tile + MOE_LEAD) * MOE_TM + r]
        return pltpu.make_async_copy(ys.at[b, pl.ds(r, 1)], y_hbm.at[pl.ds(dst, 1)], sem_out.at[b])

    def start_rows_loop(make_copy, tile, b):
        def body(r, carry):
            make_copy(tile, b, r).start()
            return carry
        lax.fori_loop(0, MOE_TM, body, 0, unroll=MOE_UNROLL)

    def wait_tile(b, gather):
        if gather:
            pltpu.make_async_copy(h2_hbm.at[pl.ds(0, MOE_TM)], xs.at[b], sem_in.at[b]).wait()
        else:
            pltpu.make_async_copy(ys.at[b], y_hbm.at[pl.ds(0, MOE_TM)], sem_out.at[b]).wait()

    @pl.when(i == 0)
    def _():
        xs[...] = jnp.zeros(xs.shape, F32)
        ys[...] = jnp.zeros(ys.shape, F32)
        for b in range(MOE_NBUF):
            fill = pltpu.make_async_copy(ys.at[b], y_hbm.at[pl.ds(MOE_SLOTS + b * MOE_TM, MOE_TM)], sem_out.at[b])
            fill.start()
            fill.wait()
        start_rows_loop(gather_copy, 0, 0)
        start_rows_loop(gather_copy, 1, 1)

    @pl.when(i <= n_used + 1)
    def _():
        wait_tile(buf, True)

    @pl.when((i >= 2) & (i <= n_used + 2))
    def _():
        wait_tile(buf, False)

    @pl.when(i < n_used)
    def _():
        x = xs[buf].astype(BF16)
        fc = D_FF_EXPERT // MOE_FCHUNKS
        rc = MOE_TM // MOE_FCHUNKS
        y = None
        for c in range(MOE_FCHUNKS):
            for r in range(c * rc, (c + 1) * rc):
                gather_copy(i + 2, buf_next, r).start()
                scatter_copy(i - 1, buf_next, r).start()
            a = jnp.dot(x, w1[0, :, c * fc:(c + 1) * fc], preferred_element_type=F32)
            b = jnp.dot(x, w3[0, :, c * fc:(c + 1) * fc], preferred_element_type=F32)
            act = (jax.nn.silu(a) * b).astype(BF16)
            part = jnp.dot(act, w2[0, c * fc:(c + 1) * fc, :], preferred_element_type=F32)
            y = part if y is None else y + part
        ys[buf] = y

    @pl.when(i == n_used)
    def _():
        start_rows_loop(scatter_copy, i - 1, buf_next)


def _moe_group(tile_expert, n_used, src_tok, dst_row, h2, w1, w3, w2):
    wspec = lambda shape: pl.BlockSpec((1,) + shape, lambda i, texp, *_: (texp[jnp.minimum(i, MOE_TILES - 1)], 0, 0))
    grid_spec = pltpu.PrefetchScalarGridSpec(
        num_scalar_prefetch=4,
        grid=(MOE_STEPS,),
        in_specs=[pl.BlockSpec(memory_space=pl.ANY),
                  wspec((D_MODEL, D_FF_EXPERT)), wspec((D_MODEL, D_FF_EXPERT)), wspec((D_FF_EXPERT, D_MODEL))],
        out_specs=pl.BlockSpec(memory_space=pl.ANY),
        scratch_shapes=[pltpu.VMEM((MOE_NBUF, MOE_TM, D_MODEL), F32), pltpu.VMEM((MOE_NBUF, MOE_TM, D_MODEL), F32),
                        pltpu.SemaphoreType.DMA((MOE_NBUF,)), pltpu.SemaphoreType.DMA((MOE_NBUF,))])
    return pl.pallas_call(
        _moe_group_kernel,
        grid_spec=grid_spec,
        out_shape=jax.ShapeDtypeStruct((MOE_SLOTS + MOE_DUMP, D_MODEL), F32),
        compiler_params=_cparams(("arbitrary",)),
        name="moe_group",
    )(tile_expert, n_used, src_tok, dst_row, h2, w1, w3, w2)


def _moe_plan(expert_ids):
    e_flat = expert_ids.T.reshape(-1)
    order = jnp.argsort(e_flat, stable=True).astype(jnp.int32)
    counts = jnp.sum((e_flat[:, None] == jnp.arange(N_EXPERTS)[None, :]).astype(jnp.int32), axis=0)
    padded = (counts + MOE_TM - 1) // MOE_TM * MOE_TM
    pend = jnp.cumsum(padded)
    pstart = pend - padded
    ustart = jnp.cumsum(counts) - counts
    n_used = pend[-1] // MOE_TM
    tiles = jnp.arange(MOE_TILES, dtype=jnp.int32)
    last_used = jnp.minimum(tiles, n_used - 1)
    tile_expert = jnp.sum((last_used[:, None] * MOE_TM >= pend[None, :]).astype(jnp.int32), axis=1)
    t = jnp.arange(-MOE_LEAD, MOE_PLAN_TILES - MOE_LEAD, dtype=jnp.int32)[:, None]
    r = jnp.arange(MOE_TM, dtype=jnp.int32)[None, :]
    e_t = tile_expert[jnp.clip(t, 0, MOE_TILES - 1)]
    off = t * MOE_TM + r - pstart[e_t]
    valid = (t >= 0) & (t < n_used) & (off < counts[e_t])
    slot = order[jnp.clip(ustart[e_t] + off, 0, MOE_SLOTS - 1)]
    src_tok = jnp.where(valid, slot % N_TOK, 0)
    dst_row = jnp.where(valid, slot, MOE_SLOTS + (t % MOE_NBUF) * MOE_TM + r)
    return (tile_expert.astype(jnp.int32), n_used.reshape(1).astype(jnp.int32),
            src_tok.reshape(-1).astype(jnp.int32), dst_row.reshape(-1).astype(jnp.int32))


def _moe_combine_kernel(x1, y0, y1, route, mod_ref, out_c, out_l):
    gate2 = mod_ref[0][:, 5 * D_MODEL:6 * D_MODEL]
    r = route[...]
    val = x1[...] + gate2 * (r[:, 0:1] * y0[...] + r[:, 1:2] * y1[...])
    is_ctx = pl.program_id(0) < SEG // FF_TM

    @pl.when(is_ctx)
    def _():
        out_c[...] = val

    @pl.when(jnp.logical_not(is_ctx))
    def _():
        out_l[...] = val


def _moe_combine(x1, y_slots, route, mod_l):
    nt = N_TOK // FF_TM
    per_seg = SEG // FF_TM
    return pl.pallas_call(
        _moe_combine_kernel,
        grid=(nt,),
        in_specs=[pl.BlockSpec((FF_TM, D_MODEL), lambda i: (i, 0)),
                  pl.BlockSpec((FF_TM, D_MODEL), lambda i: (i, 0)),
                  pl.BlockSpec((FF_TM, D_MODEL), lambda i: (nt + i, 0)),
                  pl.BlockSpec((FF_TM, LANES), lambda i: (i, 0)),
                  pl.BlockSpec((1, 1, 6 * D_MODEL), lambda i: (i // per_seg, 0, 0))],
        out_specs=[pl.BlockSpec((FF_TM, D_MODEL), lambda i: (jnp.minimum(i, per_seg - 1), 0)),
                   pl.BlockSpec((FF_TM, D_MODEL), lambda i: (jnp.maximum(i - per_seg, 0), 0))],
        out_shape=[jax.ShapeDtypeStruct((SEG, D_MODEL), F32), jax.ShapeDtypeStruct((N_TOK - SEG, D_MODEL), F32)],
        compiler_params=_cparams(("arbitrary",)),
        name="moe_combine",
    )(x1, y_slots, y_slots, route, mod_l)


def _rope_tables():
    t = np.arange(DEC_SEQ)
    row, col = (t // GRID_W).astype(np.float32), (t % GRID_W).astype(np.float32)
    nf = HEAD_DIM // 4
    freqs = np.float32(ROPE_BASE) ** (-np.arange(nf, dtype=np.float32) / np.float32(nf))
    lane = np.arange(LANES) % HEAD_DIM
    fidx = lane % nf
    use_col = (lane // (HEAD_DIM // 2)) == 1
    first = (lane % (HEAD_DIM // 2)) < nf
    pos = np.where(use_col[None, :], col[:, None], row[:, None])
    ang = (pos * freqs[fidx][None, :]).astype(np.float32).astype(np.float64)
    sin = np.sin(ang)
    return (jnp.asarray(np.cos(ang), dtype=F32), jnp.asarray(np.where(first[None, :], -sin, sin), dtype=F32))


def _permute_w_in(w):
    sizes = (256, 256, 256, 256, 8, 8, 256, 128, 128, 256, 256, 256, 256, 256)
    offs = np.concatenate([[0], np.cumsum(sizes)])
    part = lambda i: w[:, offs[i]:offs[i + 1]]
    mq, mk, mv, mo, mi, mf, sq, sk, sv, rx, ry, nq, nk, nv = (part(i) for i in range(14))
    pad = jnp.zeros((w.shape[0], LANES - 16), w.dtype)
    return jnp.concatenate([mq, mk, mv, mo, sq, rx, ry, nq, nk, nv, sk, sv, mi, mf, pad], axis=1)


def _block_diag(w):
    eye = jnp.eye(RG_BLOCKS, dtype=w.dtype)
    return (w[:, :, None, :] * eye[:, None, :, None]).reshape(RG_WIDTH, RG_WIDTH)


def _tile2(g):
    return jnp.concatenate([g, g]).reshape(1, LANES)


def kernel(x_prompt, x_sample, cache_swa_k, cache_swa_v, cache_na_k, cache_na_v, state_mlstm_C, state_mlstm_n, state_mlstm_m, state_rglru_h, c, c_ctx, norm1_g, norm2_g, w_ada, b_ada, w_in, ml_b_i, ml_b_f, ml_hn, sw_qn, sw_kn, sw_sink, rg_conv_w, rg_conv_b, rg_w_r, rg_b_r, rg_w_i, rg_b_i, rg_lam, na_qn, na_kn, na_rpb, w_br, w_mg, b_mg, w_out, ffn_w1, ffn_w3, ffn_w2, moe_wr, moe_br, moe_w1, moe_w3, moe_w2):
    assert DEPTH % 2 == 0
    x_all = (x_prompt.reshape(SEG, D_MODEL), x_sample.reshape(N_TOK - SEG, D_MODEL))
    cvecs = jnp.concatenate([c_ctx[None, :], c, jnp.zeros((8 - 1 - DEC_BATCH, D_MODEL), F32)], axis=0)
    mod = _mod_table(cvecs.T, w_ada, b_ada)
    cos_t, sin_t = _rope_tables()
    nj = 2 * ML_HEADS
    zeros_state = (jnp.zeros((BATCH, nj // 2, LANES, LANES), F32), jnp.zeros((BATCH, nj // 2, LANES, LANES), F32),
                   jnp.zeros((BATCH, nj, LANES), F32), jnp.zeros((BATCH // RG_NSEG, 2, RG_NSEG, RG_WIDTH), F32))
    ctx_out = []
    for l in range(DEPTH):
        mod_l = mod[l].reshape(8, 1, 6 * D_MODEL)
        qk_gains = jnp.stack([_tile2(sw_qn[l])[0], _tile2(sw_kn[l])[0], _tile2(na_qn[l])[0], _tile2(na_kn[l])[0]])
        proj = _inproj(x_all, mod_l, norm1_g[l].reshape(1, D_MODEL), _permute_w_in(w_in[l]).astype(BF16),
                       qk_gains, cos_t, sin_t)
        gate_bias = jnp.concatenate([ml_b_i[l].reshape(-1), ml_b_f[l].reshape(-1),
                                     jnp.zeros((LANES - 2 * nj,), F32)]).reshape(1, LANES)
        hf_c, hb_c, *st_new = _mlstm(proj, gate_bias, *zeros_state[:3], bsz=BATCH, seq=SEQ, row0=0)
        c_new, n_new, m_new = _mlstm_unpack_state(*st_new)
        st_lat = _mlstm_pack_state(state_mlstm_C[:, l].reshape(DEC_BATCH, nj, HEAD_DIM, HEAD_DIM),
                                   state_mlstm_n[:, l].reshape(DEC_BATCH, nj, HEAD_DIM),
                                   state_mlstm_m[:, l].reshape(DEC_BATCH, nj))
        hf_l, hb_l, _, _, _ = _mlstm(proj, gate_bias, *st_lat, bsz=DEC_BATCH, seq=DEC_SEQ, row0=SEG)
        wg = jnp.concatenate([_block_diag(rg_w_r[l, 0]), _block_diag(rg_w_i[l, 0]),
                              _block_diag(rg_w_r[l, 1]), _block_diag(rg_w_i[l, 1])], axis=1).astype(BF16)
        bg = jnp.concatenate([rg_b_r[l, 0], rg_b_i[l, 0], rg_b_r[l, 1], rg_b_i[l, 1]]).reshape(1, 4 * RG_WIDTH)
        rg_args = (rg_conv_w[l], rg_conv_b[l].reshape(1, RG_WIDTH), wg, bg, rg_lam[l])
        oc_c, hl_c = _rglru(proj, *rg_args, zeros_state[3], nblk=BATCH // RG_NSEG, seg_len=SEQ, chained=False, row0=0)
        hl_new = jnp.transpose(hl_c, (0, 2, 1, 3)).reshape(BATCH, 2, RG_WIDTH)
        h0_lat = jnp.broadcast_to(state_rglru_h[:, l][:, :, None, :], (DEC_BATCH, 2, RG_NSEG, RG_WIDTH))
        oc_l, _ = _rglru(proj, *rg_args, h0_lat, nblk=DEC_BATCH, seg_len=DEC_SEQ // RG_NSEG, chained=True, row0=SEG)
        ob_c, od_c = _ctx_attn(proj, sw_sink[l])
        ob_l = _swa(proj, cache_swa_k[:, l].reshape(DEC_BATCH, PAST_LEN, 128),
                    cache_swa_v[:, l].reshape(DEC_BATCH, PAST_LEN, 128), sw_sink[l])
        od_l = _na(proj, cache_na_k[:, l].reshape(DEC_BATCH, PAST_LEN, 256),
                   cache_na_v[:, l].reshape(DEC_BATCH, PAST_LEN, 256), _na_bias(na_rpb[l].reshape(-1)))
        moe_layer = l % 2 == 1
        j = l // 2
        router = (moe_wr[j].T, moe_br[j]) if moe_layer else None
        outs = _merge(x_all, mod_l, norm1_g[l].reshape(1, D_MODEL), norm2_g[l].reshape(1, D_MODEL),
                      (hf_c, hf_l), (hb_c, hb_l), proj, (ob_c, ob_l), (oc_c, oc_l), (od_c, od_l),
                      _tile2(ml_hn[l]), w_mg[l].astype(BF16), b_mg[l].reshape(1, -1), w_br[l].astype(BF16),
                      w_out[l].astype(BF16), router)
        if moe_layer:
            x1, h2, route = outs
            plan = _moe_plan(route[:, 2:4].astype(jnp.int32))
            y_slots = _moe_group(*plan, h2, moe_w1[j].astype(BF16), moe_w3[j].astype(BF16), moe_w2[j].astype(BF16))
            x_all = tuple(_moe_combine(x1, y_slots, route, mod_l))
        else:
            x1, h2 = outs
            x_all = _ffn(h2, x1, mod_l, ffn_w1[j].astype(BF16), ffn_w3[j].astype(BF16), ffn_w2[j].astype(BF16))
        pc = proj[:SEG]
        ctx_out.append(dict(
            sw_k=pc[:, C_SK:C_SK + 128].reshape(BATCH, SEQ, SW_KV_HEADS, HEAD_DIM),
            sw_v=pc[:, C_SV:C_SV + 128].reshape(BATCH, SEQ, SW_KV_HEADS, HEAD_DIM),
            na_k=pc[:, C_NK:C_NK + 256].reshape(BATCH, SEQ, NA_HEADS, HEAD_DIM),
            na_v=pc[:, C_NV:C_NV + 256].reshape(BATCH, SEQ, NA_HEADS, HEAD_DIM),
            ml_C=c_new.reshape(BATCH, 2, ML_HEADS, HEAD_DIM, HEAD_DIM),
            ml_n=n_new.reshape(BATCH, 2, ML_HEADS, HEAD_DIM),
            ml_m=m_new.reshape(BATCH, 2, ML_HEADS),
            rg_h=hl_new))
    stack = lambda name: jnp.stack([t[name] for t in ctx_out], axis=1)
    return (x_all[0].reshape(BATCH, SEQ, D_MODEL), x_all[1].reshape(DEC_BATCH, DEC_SEQ, D_MODEL),
            stack('sw_k'), stack('sw_v'), stack('na_k'), stack('na_v'),
            stack('ml_C'), stack('ml_n'), stack('ml_m'), stack('rg_h'))
```

```python
import functools

import numpy as np
import jax
import jax.numpy as jnp
from jax import lax
from jax.experimental import pallas as pl
from jax.experimental.pallas import tpu as pltpu

F32 = jnp.float32
BF16 = jnp.bfloat16

D_MODEL = 1024
BATCH = 16
SEQ = 256
DEPTH = 2
DEC_BATCH = 2
DEC_SEQ = 4096
PAST_LEN = 256
GRID_W = 64
HEAD_DIM = 64
ML_HEADS = 4
ML_CHUNK = 128
ML_SUB = 2
SW_HEADS = 4
SW_KV_HEADS = 2
SW_WINDOW = 128
RG_WIDTH = 256
RG_BLOCKS = 4
RG_CONV = 4
RG_C = 8.0
NA_HEADS = 4
NA_ROWS = 8
NA_COLS = 16
N_BRANCH = 4
ROPE_BASE = 10000.0
D_FF = 2816
N_EXPERTS = 8
D_FF_EXPERT = 2048
EPS = 1e-6
NEG = -1e30
SCALE = HEAD_DIM ** -0.5

SEG = 4096
N_SEG = 3
N_TOK = N_SEG * SEG
LANES = 128
VMEM_LIMIT = 56 * 1024 * 1024

C_MQ, C_MK, C_MV, C_MO = 0, 256, 512, 768
C_SQ, C_RX, C_RY, C_NQ, C_NK, C_NV = 1024, 1280, 1536, 1792, 2048, 2304
C_SK, C_SV, C_G = 2560, 2688, 2816
P_W = 2944


def _cparams(sem):
    return pltpu.CompilerParams(dimension_semantics=sem, vmem_limit_bytes=VMEM_LIMIT)


def _dot(a, b):
    return jnp.dot(a.astype(BF16), b.astype(BF16), preferred_element_type=F32)


def _dot_nt(a, b):
    return lax.dot_general(a.astype(BF16), b.astype(BF16), (((1,), (1,)), ((), ())),
                           preferred_element_type=F32)


def _dot_tn(a, b):
    return lax.dot_general(a.astype(BF16), b.astype(BF16), (((0,), (0,)), ((), ())),
                           preferred_element_type=F32)


def _split3(x):
    hi = x.astype(BF16)
    r1 = x - hi.astype(F32)
    mid = r1.astype(BF16)
    lo = (r1 - mid.astype(F32)).astype(BF16)
    return hi, mid, lo


def _dot_exact_rhs(a01, x):
    hi, mid, lo = _split3(x)
    d = lambda p: jnp.dot(a01, p, preferred_element_type=F32)
    return d(hi) + d(mid) + d(lo)


def _dot_exact_lhs(x, a01):
    hi, mid, lo = _split3(x)
    d = lambda p: jnp.dot(p, a01, preferred_element_type=F32)
    return d(hi) + d(mid) + d(lo)


def _sigmoid(x):
    return 0.5 * jnp.tanh(0.5 * x) + 0.5


def _rms(x, g):
    return x * lax.rsqrt(jnp.mean(x * x, axis=-1, keepdims=True) + EPS) * g


def _rms_head_pairs(x, g):
    lane = lax.broadcasted_iota(jnp.int32, x.shape, 1)
    left = lane < HEAD_DIM
    sq = x * x
    s0 = jnp.sum(jnp.where(left, sq, 0.0), axis=-1, keepdims=True)
    s1 = jnp.sum(jnp.where(left, 0.0, sq), axis=-1, keepdims=True)
    ms = jnp.where(left, s0, s1) * (1.0 / HEAD_DIM)
    return x * lax.rsqrt(ms + EPS) * g


MOD_TN = 1536
MOD_ROWS = 3


def _mod_kernel(ct_ref, w_ref, b_ref, o_ref):
    ct = ct_ref[...]
    st = ct * jax.nn.sigmoid(ct)
    w = w_ref[0]
    o_ref[...] = jnp.zeros(o_ref.shape, F32)
    for r in range(MOD_ROWS):
        o_ref[0, r:r + 1, :] = jnp.sum(w * st[:, r:r + 1], axis=0, keepdims=True) + b_ref[0]


def _mod_table(cvecs_t, w_ada, b_ada):
    n = 6 * D_MODEL
    return pl.pallas_call(
        _mod_kernel,
        grid=(DEPTH, n // MOD_TN),
        in_specs=[pl.BlockSpec((D_MODEL, 8), lambda l, j: (0, 0)),
                  pl.BlockSpec((1, D_MODEL, MOD_TN), lambda l, j: (l, 0, j)),
                  pl.BlockSpec((1, 1, MOD_TN), lambda l, j: (l, 0, j))],
        out_specs=pl.BlockSpec((1, 8, MOD_TN), lambda l, j: (l, 0, j)),
        out_shape=jax.ShapeDtypeStruct((DEPTH, 8, n), F32),
        compiler_params=_cparams(("arbitrary", "arbitrary")),
        name="adaln_mod",
    )(cvecs_t, w_ada, b_ada.reshape(DEPTH, 1, n))


IN_TM = 512


def _swap16(y):
    lane = lax.broadcasted_iota(jnp.int32, y.shape, 1)
    first = (lane % 32) < 16
    return jnp.where(first, pltpu.roll(y, LANES - 16, 1), pltpu.roll(y, 16, 1))


def _seg_pair_specs(tm, width, lat_row0=0):
    nt = SEG // tm
    lat_off = lat_row0 // tm
    return (pl.BlockSpec((tm, width), lambda s, i: (jnp.minimum(s * nt + i, nt - 1), 0)),
            pl.BlockSpec((tm, width), lambda s, i: (lat_off + jnp.maximum(s * nt + i - nt, 0), 0)))


TOK_TILE = D_MODEL // LANES


def _store_token_tiles(ref, x):
    for s in range(TOK_TILE):
        ref[pl.ds(s, x.shape[0], stride=TOK_TILE), :] = x[:, LANES * s:LANES * (s + 1)]


def _load_token_tiles(ref, n_tok):
    return jnp.concatenate([ref[pl.ds(s, n_tok, stride=TOK_TILE), :] for s in range(TOK_TILE)], axis=1)


def _pick(c_ref, l_ref):
    return jnp.where(pl.program_id(0) == 0, c_ref[...], l_ref[...])


def _x_pair(x):
    return (x, 0) if isinstance(x, tuple) else ((x, x), SEG)


def _inproj_kernel(xc_ref, xl_ref, mod_ref, g_ref, w_ref, qkg_ref, cos_ref, sin_ref, o_ref):
    seg = pl.program_id(0)
    mod = mod_ref[0]
    sh1 = mod[:, 0:D_MODEL]
    sc1 = mod[:, D_MODEL:2 * D_MODEL]
    h = _rms(_pick(xc_ref, xl_ref), g_ref[...]) * (1.0 + sc1) + sh1
    r = jnp.dot(h.astype(BF16), w_ref[...], preferred_element_type=F32)
    o_ref[:, 0:C_SQ] = r[:, 0:C_SQ]
    o_ref[:, C_RX:C_NQ] = r[:, C_RX:C_NQ]
    o_ref[:, C_NV:C_SK] = r[:, C_NV:C_SK]
    o_ref[:, C_SV:P_W] = r[:, C_SV:P_W]
    cos = cos_ref[...]
    sin = sin_ref[...]
    latent = seg > 0

    def rope(y):
        return jnp.where(latent, y * cos + _swap16(y) * sin, y)

    for p in range(2):
        a = C_SQ + LANES * p
        o_ref[:, a:a + LANES] = rope(_rms_head_pairs(r[:, a:a + LANES], qkg_ref[0:1, :]))
    o_ref[:, C_SK:C_SK + LANES] = rope(_rms_head_pairs(r[:, C_SK:C_SK + LANES], qkg_ref[1:2, :]))
    for p in range(2):
        a = C_NQ + LANES * p
        o_ref[:, a:a + LANES] = _rms_head_pairs(r[:, a:a + LANES], qkg_ref[2:3, :])
        a = C_NK + LANES * p
        o_ref[:, a:a + LANES] = _rms_head_pairs(r[:, a:a + LANES], qkg_ref[3:4, :])


def _inproj(x, mod_l, norm1, w_in_p, qk_gains, cos_t, sin_t):
    nt = SEG // IN_TM
    x_pair, lat_row0 = _x_pair(x)
    return pl.pallas_call(
        _inproj_kernel,
        grid=(N_SEG, nt),
        in_specs=[*_seg_pair_specs(IN_TM, D_MODEL, lat_row0),
                  pl.BlockSpec((1, 1, 6 * D_MODEL), lambda s, i: (s, 0, 0)),
                  pl.BlockSpec((1, D_MODEL), lambda s, i: (0, 0)),
                  pl.BlockSpec((D_MODEL, P_W), lambda s, i: (0, 0)),
                  pl.BlockSpec((4, LANES), lambda s, i: (0, 0)),
                  pl.BlockSpec((IN_TM, LANES), lambda s, i: (i, 0)),
                  pl.BlockSpec((IN_TM, LANES), lambda s, i: (i, 0))],
        out_specs=pl.BlockSpec((IN_TM, P_W), lambda s, i: (s * nt + i, 0)),
        out_shape=jax.ShapeDtypeStruct((N_TOK, P_W), F32),
        compiler_params=_cparams(("arbitrary", "arbitrary")),
        name="inproj",
    )(*x_pair, mod_l, norm1, w_in_p, qk_gains, cos_t, sin_t)


def _mlstm_direction(d, rows, q_ref, k_ref, v_ref, g_ref, bias, tri_ref, sel_ref, state):
    ch = ML_CHUNK
    r_io = lax.broadcasted_iota(jnp.int32, (ch, ch), 0)
    c_io = lax.broadcasted_iota(jnp.int32, (ch, ch), 1)
    lower = r_io >= c_io
    upper = r_io <= c_io
    mask = lower if d == 0 else upper
    tri = tri_ref[d]
    tri_t = tri_ref[1 - d]
    left = c_io < HEAD_DIM
    top = r_io < HEAD_DIM
    blockdiag = top == left
    cbs_in, nbs_in, m_old = state
    g = g_ref[rows, :] + bias[...]
    b_cols = _dot_exact_rhs(tri, jax.nn.log_sigmoid(g))
    b3 = jnp.concatenate(_split3(b_cols), axis=1)
    gt = g.T
    li_rows = gt[0:8, :]
    b_rows = _dot_exact_lhs(jax.nn.log_sigmoid(gt[8:16, :]), tri_t)
    a_rows = li_rows - b_rows
    bl = b_rows[:, ch - 1:ch] if d == 0 else b_rows[:, 0:1]
    g_rows = bl - b_rows + li_rows
    m_new = jnp.maximum(bl + m_old, jnp.max(g_rows, axis=1, keepdims=True))
    wk_rows = jnp.exp(g_rows - m_new)
    wp = jnp.exp(bl + m_old - m_new)
    ones_blk = jnp.ones((ch, LANES), BF16)
    left2 = lax.broadcasted_iota(jnp.int32, (ch, 2 * LANES), 1) % LANES < HEAD_DIM
    h_out, c_out, n_out = [], [], []
    for p in range(ML_HEADS // 2):
        lanes = slice(LANES * p, LANES * (p + 1))
        j0 = ML_HEADS * d + 2 * p
        q2 = q_ref[rows, lanes]
        k2t = (k_ref[rows, lanes] * SCALE).T.astype(BF16)
        v2e = jnp.concatenate([v_ref[rows, lanes].astype(BF16), ones_blk], axis=1)
        cb = cbs_in[p]
        nb = nbs_in[p]
        q2b = q2.astype(BF16)
        q_lo = (q2 - q2b.astype(F32)).astype(BF16)
        nb_hi = nb.astype(BF16)
        nb_lo = (nb - nb_hi.astype(F32)).astype(BF16)
        qc = jnp.dot(q2b, cb.astype(BF16), preferred_element_type=F32)
        qn = jnp.dot(jnp.concatenate([q2b, q_lo, q2b], axis=1), jnp.concatenate([nb_hi, nb_hi, nb_lo], axis=0),
                     preferred_element_type=F32)
        b_pair = jnp.dot(b3, sel_ref[2 * d + p], preferred_element_type=F32)
        cbs, sves = [], []
        for i in range(2):
            j = j0 + i
            half = left if i == 0 else jnp.logical_not(left)
            a_mat = jnp.where(mask, a_rows[j:j + 1, :], NEG)
            cvec = jnp.maximum(m_old[j:j + 1, :], jnp.max(a_mat, axis=1, keepdims=True))
            cbro = jnp.broadcast_to(cvec, (ch, ch))
            s = jnp.dot(jnp.where(half, q2b, 0), k2t, preferred_element_type=F32) * jnp.exp(a_mat - cbro)
            s_hi = s.astype(BF16)
            s_lo = (s - s_hi.astype(F32)).astype(BF16)
            sve = jnp.dot(s_hi, v2e, preferred_element_type=F32)
            rs_lo = jnp.dot(s_lo, ones_blk, preferred_element_type=F32)
            sves.append(jnp.concatenate([sve[:, 0:LANES], sve[:, LANES:2 * LANES] + rs_lo], axis=1))
            cbs.append(cbro)
        c_pair = jnp.where(left, cbs[0], cbs[1])
        w_prev = jnp.exp(jnp.where(left, m_old[j0:j0 + 1, :], m_old[j0 + 1:j0 + 2, :]) - c_pair)
        sve = jnp.where(left2, sves[0], sves[1])
        num = w_prev * qc + sve[:, 0:LANES]
        den = w_prev * qn + sve[:, LANES:2 * LANES]
        h_out.append(num / jnp.maximum(jnp.abs(den), jnp.exp(-(c_pair + b_pair))))
        kwt = k2t * jnp.where(top, wk_rows[j0:j0 + 1, :], wk_rows[j0 + 1:j0 + 2, :])
        kwt_hi = kwt.astype(BF16)
        kwt_lo = (kwt - kwt_hi.astype(F32)).astype(BF16)
        kve = jnp.dot(kwt_hi, v2e, preferred_element_type=F32)
        kn = kve[:, LANES:2 * LANES] + jnp.dot(kwt_lo, ones_blk, preferred_element_type=F32)
        wp_pair = jnp.where(top, wp[j0:j0 + 1, :], wp[j0 + 1:j0 + 2, :])
        c_out.append(wp_pair * cb + jnp.where(blockdiag, kve[:, 0:LANES], 0.0))
        n_out.append(wp_pair * nb + jnp.where(blockdiag, kn, 0.0))
    return h_out, (c_out, n_out, m_new)


def _mlstm_kernel(qf, kf, vf, gf, qb, kb, vb, gb, c0, n0, m0, bias, tri_ref, sel_ref,
                  hf, hb, co, no, mo, cbd, nbd, m_s, *, nc):
    c = pl.program_id(1)

    @pl.when(c == 0)
    def _():
        cbd[...] = c0[0]
        nbd[...] = n0[0]
        m_s[...] = m0[0]

    npair = ML_HEADS // 2
    for d, (refs, h_ref) in enumerate((((qf, kf, vf, gf), hf), ((qb, kb, vb, gb), hb))):
        state = ([cbd[npair * d + p] for p in range(npair)], [nbd[npair * d + p] for p in range(npair)],
                 m_s[:, 0:1])
        subs = range(ML_SUB) if d == 0 else reversed(range(ML_SUB))
        for sub in subs:
            rows = slice(sub * ML_CHUNK, (sub + 1) * ML_CHUNK)
            h_out, state = _mlstm_direction(d, rows, *refs, bias, tri_ref, sel_ref, state)
            for p in range(npair):
                h_ref[rows, LANES * p:LANES * (p + 1)] = h_out[p]
        c_out, n_out, m_new = state
        for p in range(npair):
            cbd[npair * d + p] = c_out[p]
            nbd[npair * d + p] = n_out[p]
        heads = slice(ML_HEADS * d, ML_HEADS * (d + 1))
        m_s[heads, :] = jnp.broadcast_to(m_new[heads, :], (ML_HEADS, LANES))

    @pl.when(c == nc - 1)
    def _():
        co[0] = cbd[...]
        no[0] = nbd[...]
        mo[0] = m_s[...]


def _mlstm_pack_state(c0, n0, m0):
    bsz = c0.shape[0]
    hd = HEAD_DIM
    eye = jnp.eye(2, dtype=F32)[None, None, :, None, :, None]
    cbd = c0.reshape(bsz, ML_HEADS, 2, hd, 1, hd) * eye
    nbd = jnp.broadcast_to(n0.reshape(bsz, ML_HEADS, 2, hd, 1, 1) * eye, cbd.shape)
    to_mat = lambda t: t.reshape(bsz, ML_HEADS, LANES, LANES)
    return to_mat(cbd), to_mat(nbd), jnp.broadcast_to(m0[..., None], m0.shape + (LANES,))


def _mlstm_unpack_state(cbd, nbd, mrow):
    hd = HEAD_DIM
    bsz = cbd.shape[0]
    c = jnp.stack([cbd[:, :, :hd, :hd], cbd[:, :, hd:, hd:]], axis=2).reshape(bsz, 2 * ML_HEADS, hd, hd)
    n = jnp.stack([nbd[:, :, :hd, 0], nbd[:, :, hd:, hd]], axis=2).reshape(bsz, 2 * ML_HEADS, hd)
    return c, n, mrow[:, :, 0]


def _mlstm_tables():
    r = np.arange(ML_CHUNK)
    lower = (r[:, None] >= r[None, :]).astype(np.float32)
    row = np.arange(3 * LANES)[:, None] % LANES
    lane_left = np.arange(LANES)[None, :] < HEAD_DIM
    sel = [row == np.where(lane_left, 8 + ML_HEADS * d + 2 * p, 9 + ML_HEADS * d + 2 * p)
           for d in range(2) for p in range(ML_HEADS // 2)]
    return (jnp.asarray(np.stack([lower, lower.T]), dtype=BF16),
            jnp.asarray(np.stack(sel).astype(np.float32), dtype=BF16))


def _mlstm(proj, gate_bias, c0, n0, m0, *, bsz, seq, row0):
    blk = ML_SUB * ML_CHUNK
    nc = seq // blk
    tri, sel = _mlstm_tables()
    base = row0 // blk
    nj = 2 * ML_HEADS
    fw = lambda col: (lambda b, c: (base + b * nc + c, col))
    bw = lambda col: (lambda b, c: (base + b * nc + nc - 1 - c, col))
    qkv = lambda f: [pl.BlockSpec((blk, 256), f(C_MQ // 256)),
                     pl.BlockSpec((blk, 256), f(C_MK // 256)),
                     pl.BlockSpec((blk, 256), f(C_MV // 256)),
                     pl.BlockSpec((blk, LANES), f(C_G // LANES))]
    st_specs = [pl.BlockSpec((1, nj // 2, LANES, LANES), lambda b, c: (b, 0, 0, 0)),
                pl.BlockSpec((1, nj // 2, LANES, LANES), lambda b, c: (b, 0, 0, 0)),
                pl.BlockSpec((1, nj, LANES), lambda b, c: (b, 0, 0))]
    st_shapes = [jax.ShapeDtypeStruct((bsz, nj // 2, LANES, LANES), F32),
                 jax.ShapeDtypeStruct((bsz, nj // 2, LANES, LANES), F32),
                 jax.ShapeDtypeStruct((bsz, nj, LANES), F32)]
    return pl.pallas_call(
        functools.partial(_mlstm_kernel, nc=nc),
        grid=(bsz, nc),
        in_specs=qkv(fw) + qkv(bw) + st_specs + [pl.BlockSpec((1, LANES), lambda b, c: (0, 0)),
                                                 pl.BlockSpec(tri.shape, lambda b, c: (0, 0, 0)),
                                                 pl.BlockSpec(sel.shape, lambda b, c: (0, 0, 0))],
        out_specs=[pl.BlockSpec((blk, 256), lambda b, c: (b * nc + c, 0)),
                   pl.BlockSpec((blk, 256), lambda b, c: (b * nc + nc - 1 - c, 0))] + st_specs,
        out_shape=[jax.ShapeDtypeStruct((bsz * seq, 256), F32),
                   jax.ShapeDtypeStruct((bsz * seq, 256), F32)] + st_shapes,
        scratch_shapes=[pltpu.VMEM((nj // 2, LANES, LANES), F32),
                        pltpu.VMEM((nj // 2, LANES, LANES), F32),
                        pltpu.VMEM((nj, LANES), F32)],
        compiler_params=_cparams(("arbitrary", "arbitrary")),
        name="mlstm",
    )(proj, proj, proj, proj, proj, proj, proj, proj, c0, n0, m0, gate_bias, tri, sel)


RG_TC = 256
RG_PAD = 8
RG_NSEG = 8
RG_SKEW = 4


def _rglru_kernel(rx, ry, cw, cb, wg, bg, lam, h0, oc, hl, xpad, af, ab, uf, ub, *, seg_len, chained):
    rows = RG_NSEG * seg_len
    seq_len = rows if chained else seg_len
    halves = RG_WIDTH // LANES
    pitch = seg_len + RG_SKEW
    buf_row = lambda t: (t // seg_len) * pitch + t % seg_len
    xpad[0:RG_PAD, :] = jnp.zeros((RG_PAD, RG_WIDTH), F32)
    xpad[rows + RG_PAD:rows + 2 * RG_PAD, :] = jnp.zeros((RG_PAD, RG_WIDTH), F32)
    xpad[RG_PAD:rows + RG_PAD, :] = rx[...]
    sp = jax.nn.softplus(-lam[...])
    left = (RG_CONV - 1) // 2
    for ci in range(rows // RG_TC):
        s0 = ci * RG_TC
        pos = (s0 + lax.broadcasted_iota(jnp.int32, (RG_TC, RG_WIDTH), 0)) % seq_len
        xc = None
        for j in range(RG_CONV):
            a = RG_PAD + s0 + j - left
            term = xpad[a:a + RG_TC, :] * cw[j:j + 1, :]
            if not chained and j != left:
                term = jnp.where((pos + (j - left) >= 0) & (pos + (j - left) < seq_len), term, 0.0)
            xc = term if xc is None else xc + term
        xc = xc + cb[...]
        pre = _dot(xc, wg[...]) + bg[...]
        for d, (a_ref, u_ref) in enumerate(((af, uf), (ab, ub))):
            o = 2 * RG_WIDTH * d
            r = _sigmoid(pre[:, o:o + RG_WIDTH])
            gi = _sigmoid(pre[:, o + RG_WIDTH:o + 2 * RG_WIDTH])
            log_a = -RG_C * r * sp[d:d + 1, :]
            a_val = jnp.exp(log_a)
            u_val = jnp.sqrt(-jnp.tanh(log_a) * (a_val * a_val + 1.0)) * (gi * xc)
            for hv in range(halves):
                dst = slice(buf_row(s0), buf_row(s0) + RG_TC)
                a_ref[hv, dst, :] = a_val[:, LANES * hv:LANES * (hv + 1)]
                u_ref[hv, dst, :] = u_val[:, LANES * hv:LANES * (hv + 1)]

    def body(s, carry):
        out = []
        for d, (a_ref, u_ref) in enumerate(((af, uf), (ab, ub))):
            step_rows = pl.ds(s if d == 0 else seg_len - 1 - s, RG_NSEG, stride=pitch)
            for hv in range(halves):
                h_loc, prod = carry[2 * (halves * d + hv)], carry[2 * (halves * d + hv) + 1]
                a = a_ref[hv, step_rows, :]
                h_loc = a * h_loc + u_ref[hv, step_rows, :]
                prod = a * prod
                u_ref[hv, step_rows, :] = h_loc
                a_ref[hv, step_rows, :] = prod
                out += [h_loc, prod]
        return tuple(out)

    zero = jnp.zeros((RG_NSEG, LANES), F32)
    one = jnp.ones((RG_NSEG, LANES), F32)
    ends = lax.fori_loop(0, seg_len, body, (zero, one) * (2 * halves), unroll=8)
    seg = lax.broadcasted_iota(jnp.int32, (RG_NSEG, LANES), 0)
    for d, (a_ref, u_ref) in enumerate(((af, uf), (ab, ub))):
        for hv in range(halves):
            lanes = slice(LANES * hv, LANES * (hv + 1))
            h_end, p_end = ends[2 * (halves * d + hv)], ends[2 * (halves * d + hv) + 1]
            h_in = h0[0, d][:, lanes]
            if chained:
                c = h_in[0:1, :]
                h_in = zero
                for k in (range(RG_NSEG) if d == 0 else reversed(range(RG_NSEG))):
                    h_in = jnp.where(seg == k, c, h_in)
                    c = h_end[k:k + 1, :] + p_end[k:k + 1, :] * c
            hl[0, d, :, lanes] = h_end + p_end * h_in
            for k in range(RG_NSEG):
                for ci in range(seg_len // RG_TC):
                    sl = slice(k * pitch + ci * RG_TC, k * pitch + (ci + 1) * RG_TC)
                    u_ref[hv, sl, :] = u_ref[hv, sl, :] + a_ref[hv, sl, :] * h_in[k:k + 1, :]
    for ci in range(rows // RG_TC):
        sl = slice(ci * RG_TC, (ci + 1) * RG_TC)
        for hv in range(halves):
            lanes = slice(LANES * hv, LANES * (hv + 1))
            src = slice(buf_row(ci * RG_TC), buf_row(ci * RG_TC) + RG_TC)
            oc[sl, lanes] = (uf[hv, src, :] + ub[hv, src, :]) * jax.nn.gelu(ry[sl, lanes])


def _rglru(proj, cw, cb, wg, bg, lam, h0, *, nblk, seg_len, chained, row0):
    rows = RG_NSEG * seg_len
    base = row0 // rows
    full = lambda shape: pl.BlockSpec(shape, lambda b: tuple(0 for _ in shape))
    st_spec = pl.BlockSpec((1, 2, RG_NSEG, RG_WIDTH), lambda b: (b, 0, 0, 0))
    return pl.pallas_call(
        functools.partial(_rglru_kernel, seg_len=seg_len, chained=chained),
        grid=(nblk,),
        in_specs=[pl.BlockSpec((rows, RG_WIDTH), lambda b: (base + b, C_RX // RG_WIDTH)),
                  pl.BlockSpec((rows, RG_WIDTH), lambda b: (base + b, C_RY // RG_WIDTH)),
                  full((RG_CONV, RG_WIDTH)), full((1, RG_WIDTH)),
                  full((RG_WIDTH, 4 * RG_WIDTH)), full((1, 4 * RG_WIDTH)), full((2, RG_WIDTH)),
                  st_spec],
        out_specs=[pl.BlockSpec((rows, RG_WIDTH), lambda b: (b, 0)), st_spec],
        out_shape=[jax.ShapeDtypeStruct((nblk * rows, RG_WIDTH), F32),
                   jax.ShapeDtypeStruct((nblk, 2, RG_NSEG, RG_WIDTH), F32)],
        scratch_shapes=[pltpu.VMEM((rows + 2 * RG_PAD, RG_WIDTH), F32)]
        + [pltpu.VMEM((RG_WIDTH // LANES, RG_NSEG * (seg_len + RG_SKEW), LANES), F32) for _ in range(4)],
        compiler_params=_cparams(("arbitrary",)),
        name="rglru",
    )(proj, proj, cw, cb, wg, bg, lam, h0)


def _softmax_pv(scores, values, sink):
    m = functools.reduce(jnp.maximum, [jnp.max(s, axis=-1, keepdims=True) for s in scores])
    if sink is not None:
        m = jnp.maximum(m, sink)
    ps = [jnp.exp(s - m) for s in scores]
    den = functools.reduce(jnp.add, [jnp.sum(p, axis=-1, keepdims=True) for p in ps])
    if sink is not None:
        den = den + jnp.exp(sink - m)
    num = functools.reduce(jnp.add, [_dot(p, v) for p, v in zip(ps, values)])
    return num / den


def _ctx_attn_kernel(sink, sq, sk, sv, nq, nk, nv, ob, od):
    assert SW_HEADS == 4 and SW_KV_HEADS == 2
    left = lax.broadcasted_iota(jnp.int32, (SEQ, LANES), 1) < HEAD_DIM
    first = lax.broadcasted_iota(jnp.int32, (2 * SEQ, 1), 0) < SEQ
    k2 = sk[...].astype(BF16)
    v2 = sv[...].astype(BF16)
    for p in range(SW_KV_HEADS):
        lanes = slice(LANES * p, LANES * (p + 1))
        q2 = sq[:, lanes]
        q_swapped = pltpu.roll(q2, HEAD_DIM, 1)
        kv_half = left if p == 0 else jnp.logical_not(left)
        qs = jnp.concatenate([jnp.where(kv_half, q2 if i == p else q_swapped, 0.0) for i in range(2)], axis=0)
        sink_col = jnp.where(first, sink[2 * p], sink[2 * p + 1])
        res = _softmax_pv([_dot_nt(qs, k2) * SCALE], [v2], sink_col)
        halves = [res[0:SEQ], res[SEQ:2 * SEQ]]
        placed = [halves[i] if i == p else pltpu.roll(halves[i], HEAD_DIM, 1) for i in range(2)]
        ob[:, lanes] = jnp.where(left, placed[0], placed[1])
    for p in range(NA_HEADS // 2):
        lanes = slice(LANES * p, LANES * (p + 1))
        q2 = nq[:, lanes]
        k2 = nk[:, lanes].astype(BF16)
        v2 = nv[:, lanes].astype(BF16)
        res = [_softmax_pv([_dot_nt(jnp.where(left if i == 0 else jnp.logical_not(left), q2, 0.0), k2) * SCALE],
                           [v2], None) for i in range(2)]
        od[:, lanes] = jnp.where(left, res[0], res[1])


def _ctx_attn(proj, sink):
    blk = lambda w, col: pl.BlockSpec((SEQ, w), lambda b: (b, col))
    return pl.pallas_call(
        _ctx_attn_kernel,
        grid=(BATCH,),
        in_specs=[pl.BlockSpec(memory_space=pltpu.SMEM),
                  blk(256, C_SQ // 256), blk(128, C_SK // 128), blk(128, C_SV // 128),
                  blk(256, C_NQ // 256), blk(256, C_NK // 256), blk(256, C_NV // 256)],
        out_specs=[pl.BlockSpec((SEQ, 256), lambda b: (b, 0)),
                   pl.BlockSpec((SEQ, 256), lambda b: (b, 0))],
        out_shape=[jax.ShapeDtypeStruct((SEG, 256), F32), jax.ShapeDtypeStruct((SEG, 256), F32)],
        compiler_params=_cparams(("arbitrary",)),
        name="ctx_attn",
    )(sink, proj, proj, proj, proj, proj, proj)


SW_QB = 128
SW_SPAN = SW_QB + 2 * SW_WINDOW


def _swa_kernel(sink, q, k, v, kc, vc, ob):
    assert SW_HEADS == 4 and SW_KV_HEADS == 2
    n = pl.program_id(1)
    ws = jnp.clip((n - 1) * SW_QB, 0, DEC_SEQ - SW_SPAN)
    ws = pl.multiple_of(ws, SW_QB)
    row = lax.broadcasted_iota(jnp.int32, (2 * SW_QB, SW_SPAN), 0)
    qpos = n * SW_QB + row % SW_QB
    kpos = ws + lax.broadcasted_iota(jnp.int32, (2 * SW_QB, SW_SPAN), 1)
    valid = jnp.abs(qpos - kpos) <= SW_WINDOW
    left = lax.broadcasted_iota(jnp.int32, (SW_QB, LANES), 1) < HEAD_DIM
    first = lax.broadcasted_iota(jnp.int32, (2 * SW_QB, 1), 0) < SW_QB
    k2 = k[pl.ds(ws, SW_SPAN), :].astype(BF16)
    v2 = v[pl.ds(ws, SW_SPAN), :].astype(BF16)
    kc2 = kc[0].astype(BF16)
    vc2 = vc[0].astype(BF16)
    for p in range(SW_KV_HEADS):
        lanes = slice(LANES * p, LANES * (p + 1))
        q2 = q[:, lanes]
        q_swapped = pltpu.roll(q2, HEAD_DIM, 1)
        kv_half = left if p == 0 else jnp.logical_not(left)
        qs = jnp.concatenate([jnp.where(kv_half, q2 if i == p else q_swapped, 0.0) for i in range(2)], axis=0)
        s_loc = jnp.where(valid, _dot_nt(qs, k2) * SCALE, NEG)
        s_ctx = _dot_nt(qs, kc2) * SCALE
        sink_col = jnp.where(first, sink[2 * p], sink[2 * p + 1])
        res = _softmax_pv([s_loc, s_ctx], [v2, vc2], sink_col)
        halves = [res[0:SW_QB], res[SW_QB:2 * SW_QB]]
        placed = [halves[i] if i == p else pltpu.roll(halves[i], HEAD_DIM, 1) for i in range(2)]
        ob[:, lanes] = jnp.where(left, placed[0], placed[1])


def _swa(proj, kc, vc, sink):
    nq = DEC_SEQ // SW_QB
    qbase = SEG // SW_QB
    return pl.pallas_call(
        _swa_kernel,
        grid=(DEC_BATCH, nq),
        in_specs=[pl.BlockSpec(memory_space=pltpu.SMEM),
                  pl.BlockSpec((SW_QB, 256), lambda b, n: (qbase + b * nq + n, C_SQ // 256)),
                  pl.BlockSpec((DEC_SEQ, 128), lambda b, n: (1 + b, C_SK // 128)),
                  pl.BlockSpec((DEC_SEQ, 128), lambda b, n: (1 + b, C_SV // 128)),
                  pl.BlockSpec((1, PAST_LEN, 128), lambda b, n: (b, 0, 0)),
                  pl.BlockSpec((1, PAST_LEN, 128), lambda b, n: (b, 0, 0))],
        out_specs=pl.BlockSpec((SW_QB, 256), lambda b, n: (b * nq + n, 0)),
        out_shape=jax.ShapeDtypeStruct((DEC_BATCH * DEC_SEQ, 256), F32),
        compiler_params=_cparams(("arbitrary", "arbitrary")),
        name="swa",
    )(sink, proj, proj, proj, kc, vc)


NA_RPB_R = 2 * NA_ROWS - 1
NA_RPB_C = 2 * NA_COLS - 1
GRID_ROWS = DEC_SEQ // GRID_W
NA_RB = 4
NA_UW = 12
NA_NQ = NA_RB * GRID_W
NA_NKEY = NA_UW * GRID_W
NA_NBLK = GRID_ROWS // NA_RB
NA_CASES = ((0, 0), (NA_RB, 0), (GRID_ROWS - NA_RB, GRID_ROWS - NA_UW))


def _na_row_start(qrow):
    return min(max(qrow - NA_ROWS // 2, 0), GRID_ROWS - NA_ROWS)


def _na_bias_kernel(rpb, out):
    h = pl.program_id(0)
    qc = lax.broadcasted_iota(jnp.int32, (GRID_W, GRID_W), 0)
    kc = lax.broadcasted_iota(jnp.int32, (GRID_W, GRID_W), 1)
    dc = jnp.clip(kc - qc, -(NA_COLS - 1), NA_COLS - 1) + NA_COLS - 1
    lo = jnp.clip(qc - NA_COLS // 2, 0, GRID_W - NA_COLS)
    valid = (kc >= lo) & (kc < lo + NA_COLS)
    tiles = []
    for dr in range(NA_RPB_R):
        t = jnp.zeros((GRID_W, GRID_W), F32)
        for j in range(NA_RPB_C):
            t = jnp.where(dc == j, rpb[(h * NA_RPB_R + dr) * NA_RPB_C + j], t)
        tiles.append(jnp.where(valid, t, NEG))
    outside = jnp.full((GRID_W, GRID_W), NEG, F32)
    for case, (q0, k0) in enumerate(NA_CASES):
        for a in range(NA_RB):
            r0 = _na_row_start(q0 + a)
            for i in range(NA_UW):
                inside = r0 <= k0 + i < r0 + NA_ROWS
                tile = tiles[k0 + i - (q0 + a) + NA_ROWS - 1] if inside else outside
                out[0, case, GRID_W * a:GRID_W * (a + 1), GRID_W * i:GRID_W * (i + 1)] = tile


def _na_bias(rpb_flat):
    shape = (NA_HEADS, len(NA_CASES), NA_NQ, NA_NKEY)
    return pl.pallas_call(
        _na_bias_kernel,
        grid=(NA_HEADS,),
        in_specs=[pl.BlockSpec(memory_space=pltpu.SMEM)],
        out_specs=pl.BlockSpec((1,) + shape[1:], lambda h: (h, 0, 0, 0)),
        out_shape=jax.ShapeDtypeStruct(shape, F32),
        compiler_params=_cparams(("arbitrary",)),
        name="na_bias",
    )(rpb_flat)


def _na_kernel(q, k, v, kc, vc, bias, od):
    blk = pl.program_id(1)
    case = jnp.where(blk == 0, 0, jnp.where(blk == NA_NBLK - 1, 2, 1))
    u0 = jnp.clip(blk * NA_RB - NA_ROWS // 2, 0, GRID_ROWS - NA_UW)
    k0 = pl.multiple_of(u0 * GRID_W, GRID_W)
    left = lax.broadcasted_iota(jnp.int32, (NA_NQ, LANES), 1) < HEAD_DIM
    for p in range(NA_HEADS // 2):
        lanes = slice(LANES * p, LANES * (p + 1))
        q2 = q[:, lanes]
        k2 = k[pl.ds(k0, NA_NKEY), lanes].astype(BF16)
        v2 = v[pl.ds(k0, NA_NKEY), lanes].astype(BF16)
        kc2 = kc[0, :, lanes].astype(BF16)
        vc2 = vc[0, :, lanes].astype(BF16)
        res = []
        for i in range(2):
            qm = jnp.where(left if i == 0 else jnp.logical_not(left), q2, 0.0)
            s_loc = _dot_nt(qm, k2) * SCALE + bias[2 * p + i, pl.ds(case, 1)][0]
            s_ctx = _dot_nt(qm, kc2) * SCALE
            res.append(_softmax_pv([s_loc, s_ctx], [v2, vc2], None))
        od[:, lanes] = jnp.where(left, res[0], res[1])


def _na(proj, kc, vc, bias):
    qbase = SEG // NA_NQ
    return pl.pallas_call(
        _na_kernel,
        grid=(DEC_BATCH, NA_NBLK),
        in_specs=[pl.BlockSpec((NA_NQ, 256), lambda b, r: (qbase + b * NA_NBLK + r, C_NQ // 256)),
                  pl.BlockSpec((DEC_SEQ, 256), lambda b, r: (1 + b, C_NK // 256)),
                  pl.BlockSpec((DEC_SEQ, 256), lambda b, r: (1 + b, C_NV // 256)),
                  pl.BlockSpec((1, PAST_LEN, 256), lambda b, r: (b, 0, 0)),
                  pl.BlockSpec((1, PAST_LEN, 256), lambda b, r: (b, 0, 0)),
                  pl.BlockSpec((NA_HEADS, len(NA_CASES), NA_NQ, NA_NKEY), lambda b, r: (0, 0, 0, 0))],
        out_specs=pl.BlockSpec((NA_NQ, 256), lambda b, r: (b * NA_NBLK + r, 0)),
        out_shape=jax.ShapeDtypeStruct((DEC_BATCH * DEC_SEQ, 256), F32),
        compiler_params=_cparams(("arbitrary", "arbitrary")),
        name="na",
    )(proj, proj, proj, kc, vc, bias)


MG_TM = 256


def _merge_kernel(xc_ref, xl_ref, mod_ref, g1_ref, g2_ref, hf_c, hf_l, hb_c, hb_l, mo, ob_c, ob_l, oc_c, oc_l,
                  od_c, od_l, hn, wmg, bmg, wbr, wout, *rest, moe):
    if moe:
        wrt, br, x1_ref, h2_ref, route_ref = rest
    else:
        x1_ref, h2_ref = rest
    pick = _pick
    mod = mod_ref[0]
    chunk = lambda i: mod[:, i * D_MODEL:(i + 1) * D_MODEL]
    sh1, sc1, gate1, sh2, sc2 = chunk(0), chunk(1), chunk(2), chunk(3), chunk(4)
    x = pick(xc_ref, xl_ref)
    h = (_rms(x, g1_ref[...]) * (1.0 + sc1) + sh1).astype(BF16)
    hsum = pick(hf_c, hf_l) + pick(hb_c, hb_l)
    out_a = jnp.concatenate(
        [_rms_head_pairs(hsum[:, LANES * p:LANES * (p + 1)], hn[...]) for p in range(2)], axis=-1)
    out_a = out_a * jax.nn.sigmoid(mo[...])
    acc = None
    for n, br_val in enumerate((out_a, pick(ob_c, ob_l), pick(oc_c, oc_l), pick(od_c, od_l))):
        gate = jax.nn.sigmoid(jnp.dot(h, wmg[:, n * D_MODEL:(n + 1) * D_MODEL], preferred_element_type=F32)
                              + bmg[:, n * D_MODEL:(n + 1) * D_MODEL])
        term = gate * jnp.dot(br_val.astype(BF16), wbr[n], preferred_element_type=F32)
        acc = term if acc is None else acc + term
    y = jnp.dot(acc.astype(BF16), wout[...], preferred_element_type=F32)
    x1 = x + gate1 * y
    x1_ref[...] = x1
    h2 = _rms(x1, g2_ref[...]) * (1.0 + sc2) + sh2
    if moe:
        _store_token_tiles(h2_ref, h2)
    else:
        h2_ref[...] = h2.astype(BF16)
    if moe:
        logit = [jnp.sum(h2 * wrt[e:e + 1, :], axis=-1, keepdims=True) + br[e] for e in range(N_EXPERTS)]
        v1, i1 = logit[0], jnp.zeros(logit[0].shape, jnp.int32)
        for e in range(1, N_EXPERTS):
            better = logit[e] > v1
            v1 = jnp.where(better, logit[e], v1)
            i1 = jnp.where(better, e, i1)
        v2, i2 = jnp.full(v1.shape, -jnp.inf, F32), jnp.zeros(v1.shape, jnp.int32)
        for e in range(N_EXPERTS):
            better = (i1 != e) & (logit[e] > v2)
            v2 = jnp.where(better, logit[e], v2)
            i2 = jnp.where(better, e, i2)
        e2 = jnp.exp(v2 - v1)
        den = 1.0 + e2
        lane = lax.broadcasted_iota(jnp.int32, route_ref.shape, 1)
        route = jnp.where(lane == 0, 1.0 / den, 0.0) + jnp.where(lane == 1, e2 / den, 0.0)
        route = route + jnp.where(lane == 2, i1.astype(F32), 0.0) + jnp.where(lane == 3, i2.astype(F32), 0.0)
        route_ref[...] = route


def _merge(x, mod_l, g1, g2, hf, hb, proj, ob, oc, od, hn, wmg, bmg, wbr, wout, router=None):
    nt = SEG // MG_TM
    moe = router is not None
    x_pair, lat_row0 = _x_pair(x)
    row = lambda w: pl.BlockSpec((MG_TM, w), lambda s, i: (s * nt + i, 0))
    ctx_blk, lat_blk = _seg_pair_specs(MG_TM, 256)
    full = lambda shape: pl.BlockSpec(shape, lambda s, i: tuple(0 for _ in shape))
    in_specs = [*_seg_pair_specs(MG_TM, D_MODEL, lat_row0),
                pl.BlockSpec((1, 1, 6 * D_MODEL), lambda s, i: (s, 0, 0)),
                full((1, D_MODEL)), full((1, D_MODEL)),
                ctx_blk, lat_blk, ctx_blk, lat_blk,
                pl.BlockSpec((MG_TM, 256), lambda s, i: (s * nt + i, C_MO // 256)),
                ctx_blk, lat_blk, ctx_blk, lat_blk, ctx_blk, lat_blk,
                full((1, LANES)), full((D_MODEL, N_BRANCH * D_MODEL)), full((1, N_BRANCH * D_MODEL)),
                full((N_BRANCH, 256, D_MODEL)), full((D_MODEL, D_MODEL))]
    args = [*x_pair, mod_l, g1, g2, *hf, *hb, proj, *ob, *oc, *od, hn, wmg, bmg, wbr, wout]
    if moe:
        h2_spec = pl.BlockSpec((MG_TM * TOK_TILE, LANES), lambda s, i: (s * nt + i, 0))
        h2_shape = jax.ShapeDtypeStruct((N_TOK * TOK_TILE, LANES), F32)
    else:
        h2_spec, h2_shape = row(D_MODEL), jax.ShapeDtypeStruct((N_TOK, D_MODEL), BF16)
    out_specs = [row(D_MODEL), h2_spec]
    out_shape = [jax.ShapeDtypeStruct((N_TOK, D_MODEL), F32), h2_shape]
    if moe:
        in_specs += [full((N_EXPERTS, D_MODEL)), pl.BlockSpec(memory_space=pltpu.SMEM)]
        args += list(router)
        out_specs.append(row(LANES))
        out_shape.append(jax.ShapeDtypeStruct((N_TOK, LANES), F32))
    return pl.pallas_call(
        functools.partial(_merge_kernel, moe=moe),
        grid=(N_SEG, nt),
        in_specs=in_specs,
        out_specs=out_specs,
        out_shape=out_shape,
        compiler_params=_cparams(("arbitrary", "arbitrary")),
        name="merge",
    )(*args)


FF_TM = 512
FF_TF = 1408


def _ffn_kernel(h2, w1, w3, w2, x1, mod_ref, out, acc):
    f = pl.program_id(1)

    @pl.when(f == 0)
    def _():
        acc[...] = jnp.zeros(acc.shape, F32)

    h = h2[...]
    a = jnp.dot(h, w1[...], preferred_element_type=F32)
    b = jnp.dot(h, w3[...], preferred_element_type=F32)
    act = (jax.nn.silu(a) * b).astype(BF16)
    acc[...] += jnp.dot(act, w2[...], preferred_element_type=F32)

    @pl.when(f == pl.num_programs(1) - 1)
    def _():
        gate2 = mod_ref[0][:, 5 * D_MODEL:6 * D_MODEL]
        out[...] = x1[...] + gate2 * acc[...]


def _ffn(h2, x1, mod_l, w1, w3, w2):
    nt = N_TOK // FF_TM
    per_seg = SEG // FF_TM
    return pl.pallas_call(
        _ffn_kernel,
        grid=(nt, D_FF // FF_TF),
        in_specs=[pl.BlockSpec((FF_TM, D_MODEL), lambda i, f: (i, 0)),
                  pl.BlockSpec((D_MODEL, FF_TF), lambda i, f: (0, f)),
                  pl.BlockSpec((D_MODEL, FF_TF), lambda i, f: (0, f)),
                  pl.BlockSpec((FF_TF, D_MODEL), lambda i, f: (f, 0)),
                  pl.BlockSpec((FF_TM, D_MODEL), lambda i, f: (i, 0)),
                  pl.BlockSpec((1, 1, 6 * D_MODEL), lambda i, f: (i // per_seg, 0, 0))],
        out_specs=pl.BlockSpec((FF_TM, D_MODEL), lambda i, f: (i, 0)),
        out_shape=jax.ShapeDtypeStruct((N_TOK, D_MODEL), F32),
        scratch_shapes=[pltpu.VMEM((FF_TM, D_MODEL), F32)],
        compiler_params=_cparams(("arbitrary", "arbitrary")),
        name="ffn",
    )(h2, w1, w3, w2, x1, mod_l)


MOE_TM = 256
MOE_SLOTS = 2 * N_TOK
MOE_TILES = MOE_SLOTS // MOE_TM + N_EXPERTS
MOE_NBUF = 3
MOE_STEPS = MOE_TILES + MOE_NBUF
MOE_DUMP = MOE_NBUF * MOE_TM
MOE_LEAD = 1
MOE_PLAN_TILES = MOE_LEAD + MOE_TILES + 2
MOE_FCHUNKS = 1
MOE_UNROLL = 8


def _moe_group_kernel(texp, nused, src_tok, dst_row, h2_hbm, w1, w3, w2, y_hbm, xs, ys, sem_in, sem_out):
    del texp
    i = pl.program_id(0)
    n_used = nused[0]
    buf = i % MOE_NBUF
    buf_next = (i + 2) % MOE_NBUF

    def tile_rows(t):
        start = t * TOK_TILE
        return pl.ds(start if isinstance(start, int) else pl.multiple_of(start, TOK_TILE), TOK_TILE)

    def gather_copy(tile, b, r):
        tok = src_tok[(tile + MOE_LEAD) * MOE_TM + r]
        return pltpu.make_async_copy(h2_hbm.at[tile_rows(tok)], xs.at[b, tile_rows(r)], sem_in.at[b])

    def scatter_copy(tile, b, r):
        dst = dst_row[(tile + MOE_LEAD) * MOE_TM + r]
        return pltpu.make_async_copy(ys.at[b, tile_rows(r)], y_hbm.at[tile_rows(dst)], sem_out.at[b])

    def start_rows_loop(make_copy, tile, b):
        def body(r, carry):
            make_copy(tile, b, r).start()
            return carry
        lax.fori_loop(0, MOE_TM, body, 0, unroll=MOE_UNROLL)

    def wait_tile(b, gather):
        if gather:
            pltpu.make_async_copy(h2_hbm.at[pl.ds(0, MOE_TM * TOK_TILE)], xs.at[b], sem_in.at[b]).wait()
        else:
            pltpu.make_async_copy(ys.at[b], y_hbm.at[pl.ds(0, MOE_TM * TOK_TILE)], sem_out.at[b]).wait()

    @pl.when(i == 0)
    def _():
        xs[...] = jnp.zeros(xs.shape, F32)
        ys[...] = jnp.zeros(ys.shape, F32)
        for b in range(MOE_NBUF):
            fill = pltpu.make_async_copy(ys.at[b], y_hbm.at[pl.ds((MOE_SLOTS + b * MOE_TM) * TOK_TILE, MOE_TM * TOK_TILE)],
                                         sem_out.at[b])
            fill.start()
            fill.wait()
        start_rows_loop(gather_copy, 0, 0)
        start_rows_loop(gather_copy, 1, 1)

    @pl.when(i <= n_used + 1)
    def _():
        wait_tile(buf, True)

    @pl.when((i >= 2) & (i <= n_used + 2))
    def _():
        wait_tile(buf, False)

    @pl.when(i < n_used)
    def _():
        x = _load_token_tiles(xs.at[buf], MOE_TM).astype(BF16)
        fc = D_FF_EXPERT // MOE_FCHUNKS
        rc = MOE_TM // MOE_FCHUNKS
        y = None
        for c in range(MOE_FCHUNKS):
            for r in range(c * rc, (c + 1) * rc):
                gather_copy(i + 2, buf_next, r).start()
                scatter_copy(i - 1, buf_next, r).start()
            a = jnp.dot(x, w1[0, :, c * fc:(c + 1) * fc], preferred_element_type=F32)
            b = jnp.dot(x, w3[0, :, c * fc:(c + 1) * fc], preferred_element_type=F32)
            act = (jax.nn.silu(a) * b).astype(BF16)
            part = jnp.dot(act, w2[0, c * fc:(c + 1) * fc, :], preferred_element_type=F32)
            y = part if y is None else y + part
        _store_token_tiles(ys.at[buf], y)

    @pl.when(i == n_used)
    def _():
        start_rows_loop(scatter_copy, i - 1, buf_next)


def _moe_group(tile_expert, n_used, src_tok, dst_row, h2, w1, w3, w2):
    wspec = lambda shape: pl.BlockSpec((1,) + shape, lambda i, texp, *_: (texp[jnp.minimum(i, MOE_TILES - 1)], 0, 0))
    grid_spec = pltpu.PrefetchScalarGridSpec(
        num_scalar_prefetch=4,
        grid=(MOE_STEPS,),
        in_specs=[pl.BlockSpec(memory_space=pl.ANY),
                  wspec((D_MODEL, D_FF_EXPERT)), wspec((D_MODEL, D_FF_EXPERT)), wspec((D_FF_EXPERT, D_MODEL))],
        out_specs=pl.BlockSpec(memory_space=pl.ANY),
        scratch_shapes=[pltpu.VMEM((MOE_NBUF, MOE_TM * TOK_TILE, LANES), F32),
                        pltpu.VMEM((MOE_NBUF, MOE_TM * TOK_TILE, LANES), F32),
                        pltpu.SemaphoreType.DMA((MOE_NBUF,)), pltpu.SemaphoreType.DMA((MOE_NBUF,))])
    return pl.pallas_call(
        _moe_group_kernel,
        grid_spec=grid_spec,
        out_shape=jax.ShapeDtypeStruct(((MOE_SLOTS + MOE_DUMP) * TOK_TILE, LANES), F32),
        compiler_params=_cparams(("arbitrary",)),
        name="moe_group",
    )(tile_expert, n_used, src_tok, dst_row, h2, w1, w3, w2)


def _moe_plan(expert_ids):
    e_flat = expert_ids.T.reshape(-1)
    order = jnp.argsort(e_flat, stable=True).astype(jnp.int32)
    counts = jnp.sum((e_flat[:, None] == jnp.arange(N_EXPERTS)[None, :]).astype(jnp.int32), axis=0)
    padded = (counts + MOE_TM - 1) // MOE_TM * MOE_TM
    pend = jnp.cumsum(padded)
    pstart = pend - padded
    ustart = jnp.cumsum(counts) - counts
    n_used = pend[-1] // MOE_TM
    tiles = jnp.arange(MOE_TILES, dtype=jnp.int32)
    last_used = jnp.minimum(tiles, n_used - 1)
    tile_expert = jnp.sum((last_used[:, None] * MOE_TM >= pend[None, :]).astype(jnp.int32), axis=1)
    t = jnp.arange(-MOE_LEAD, MOE_PLAN_TILES - MOE_LEAD, dtype=jnp.int32)[:, None]
    r = jnp.arange(MOE_TM, dtype=jnp.int32)[None, :]
    e_t = tile_expert[jnp.clip(t, 0, MOE_TILES - 1)]
    off = t * MOE_TM + r - pstart[e_t]
    valid = (t >= 0) & (t < n_used) & (off < counts[e_t])
    slot = order[jnp.clip(ustart[e_t] + off, 0, MOE_SLOTS - 1)]
    src_tok = jnp.where(valid, slot % N_TOK, 0)
    dst_row = jnp.where(valid, slot, MOE_SLOTS + (t % MOE_NBUF) * MOE_TM + r)
    return (tile_expert.astype(jnp.int32), n_used.reshape(1).astype(jnp.int32),
            src_tok.reshape(-1).astype(jnp.int32), dst_row.reshape(-1).astype(jnp.int32))


def _moe_combine_kernel(x1, y0, y1, route, mod_ref, out_c, out_l):
    gate2 = mod_ref[0][:, 5 * D_MODEL:6 * D_MODEL]
    r = route[...]
    val = x1[...] + gate2 * (r[:, 0:1] * _load_token_tiles(y0, FF_TM) + r[:, 1:2] * _load_token_tiles(y1, FF_TM))
    is_ctx = pl.program_id(0) < SEG // FF_TM

    @pl.when(is_ctx)
    def _():
        out_c[...] = val

    @pl.when(jnp.logical_not(is_ctx))
    def _():
        out_l[...] = val


def _moe_combine(x1, y_slots, route, mod_l):
    nt = N_TOK // FF_TM
    per_seg = SEG // FF_TM
    return pl.pallas_call(
        _moe_combine_kernel,
        grid=(nt,),
        in_specs=[pl.BlockSpec((FF_TM, D_MODEL), lambda i: (i, 0)),
                  pl.BlockSpec((FF_TM * TOK_TILE, LANES), lambda i: (i, 0)),
                  pl.BlockSpec((FF_TM * TOK_TILE, LANES), lambda i: (nt + i, 0)),
                  pl.BlockSpec((FF_TM, LANES), lambda i: (i, 0)),
                  pl.BlockSpec((1, 1, 6 * D_MODEL), lambda i: (i // per_seg, 0, 0))],
        out_specs=[pl.BlockSpec((FF_TM, D_MODEL), lambda i: (jnp.minimum(i, per_seg - 1), 0)),
                   pl.BlockSpec((FF_TM, D_MODEL), lambda i: (jnp.maximum(i - per_seg, 0), 0))],
        out_shape=[jax.ShapeDtypeStruct((SEG, D_MODEL), F32), jax.ShapeDtypeStruct((N_TOK - SEG, D_MODEL), F32)],
        compiler_params=_cparams(("arbitrary",)),
        name="moe_combine",
    )(x1, y_slots, y_slots, route, mod_l)


def _rope_tables():
    t = np.arange(DEC_SEQ)
    row, col = (t // GRID_W).astype(np.float32), (t % GRID_W).astype(np.float32)
    nf = HEAD_DIM // 4
    freqs = np.float32(ROPE_BASE) ** (-np.arange(nf, dtype=np.float32) / np.float32(nf))
    lane = np.arange(LANES) % HEAD_DIM
    fidx = lane % nf
    use_col = (lane // (HEAD_DIM // 2)) == 1
    first = (lane % (HEAD_DIM // 2)) < nf
    pos = np.where(use_col[None, :], col[:, None], row[:, None])
    ang = (pos * freqs[fidx][None, :]).astype(np.float32).astype(np.float64)
    sin = np.sin(ang)
    return (jnp.asarray(np.cos(ang), dtype=F32), jnp.asarray(np.where(first[None, :], -sin, sin), dtype=F32))


def _permute_w_in(w):
    sizes = (256, 256, 256, 256, 8, 8, 256, 128, 128, 256, 256, 256, 256, 256)
    offs = np.concatenate([[0], np.cumsum(sizes)])
    part = lambda i: w[:, offs[i]:offs[i + 1]]
    mq, mk, mv, mo, mi, mf, sq, sk, sv, rx, ry, nq, nk, nv = (part(i) for i in range(14))
    pad = jnp.zeros((w.shape[0], LANES - 16), w.dtype)
    return jnp.concatenate([mq, mk, mv, mo, sq, rx, ry, nq, nk, nv, sk, sv, mi, mf, pad], axis=1)


def _block_diag(w):
    eye = jnp.eye(RG_BLOCKS, dtype=w.dtype)
    return (w[:, :, None, :] * eye[:, None, :, None]).reshape(RG_WIDTH, RG_WIDTH)


def _tile2(g):
    return jnp.concatenate([g, g]).reshape(1, LANES)


def kernel(x_prompt, x_sample, cache_swa_k, cache_swa_v, cache_na_k, cache_na_v, state_mlstm_C, state_mlstm_n, state_mlstm_m, state_rglru_h, c, c_ctx, norm1_g, norm2_g, w_ada, b_ada, w_in, ml_b_i, ml_b_f, ml_hn, sw_qn, sw_kn, sw_sink, rg_conv_w, rg_conv_b, rg_w_r, rg_b_r, rg_w_i, rg_b_i, rg_lam, na_qn, na_kn, na_rpb, w_br, w_mg, b_mg, w_out, ffn_w1, ffn_w3, ffn_w2, moe_wr, moe_br, moe_w1, moe_w3, moe_w2):
    assert DEPTH % 2 == 0
    x_all = (x_prompt.reshape(SEG, D_MODEL), x_sample.reshape(N_TOK - SEG, D_MODEL))
    cvecs = jnp.concatenate([c_ctx[None, :], c, jnp.zeros((8 - 1 - DEC_BATCH, D_MODEL), F32)], axis=0)
    mod = _mod_table(cvecs.T, w_ada, b_ada)
    cos_t, sin_t = _rope_tables()
    nj = 2 * ML_HEADS
    zeros_state = (jnp.zeros((BATCH, nj // 2, LANES, LANES), F32), jnp.zeros((BATCH, nj // 2, LANES, LANES), F32),
                   jnp.zeros((BATCH, nj, LANES), F32), jnp.zeros((BATCH // RG_NSEG, 2, RG_NSEG, RG_WIDTH), F32))
    ctx_out = []
    for l in range(DEPTH):
        mod_l = mod[l].reshape(8, 1, 6 * D_MODEL)
        qk_gains = jnp.stack([_tile2(sw_qn[l])[0], _tile2(sw_kn[l])[0], _tile2(na_qn[l])[0], _tile2(na_kn[l])[0]])
        proj = _inproj(x_all, mod_l, norm1_g[l].reshape(1, D_MODEL), _permute_w_in(w_in[l]).astype(BF16),
                       qk_gains, cos_t, sin_t)
        gate_bias = jnp.concatenate([ml_b_i[l].reshape(-1), ml_b_f[l].reshape(-1),
                                     jnp.zeros((LANES - 2 * nj,), F32)]).reshape(1, LANES)
        hf_c, hb_c, *st_new = _mlstm(proj, gate_bias, *zeros_state[:3], bsz=BATCH, seq=SEQ, row0=0)
        c_new, n_new, m_new = _mlstm_unpack_state(*st_new)
        st_lat = _mlstm_pack_state(state_mlstm_C[:, l].reshape(DEC_BATCH, nj, HEAD_DIM, HEAD_DIM),
                                   state_mlstm_n[:, l].reshape(DEC_BATCH, nj, HEAD_DIM),
                                   state_mlstm_m[:, l].reshape(DEC_BATCH, nj))
        hf_l, hb_l, _, _, _ = _mlstm(proj, gate_bias, *st_lat, bsz=DEC_BATCH, seq=DEC_SEQ, row0=SEG)
        wg = jnp.concatenate([_block_diag(rg_w_r[l, 0]), _block_diag(rg_w_i[l, 0]),
                              _block_diag(rg_w_r[l, 1]), _block_diag(rg_w_i[l, 1])], axis=1).astype(BF16)
        bg = jnp.concatenate([rg_b_r[l, 0], rg_b_i[l, 0], rg_b_r[l, 1], rg_b_i[l, 1]]).reshape(1, 4 * RG_WIDTH)
        rg_args = (rg_conv_w[l], rg_conv_b[l].reshape(1, RG_WIDTH), wg, bg, rg_lam[l])
        oc_c, hl_c = _rglru(proj, *rg_args, zeros_state[3], nblk=BATCH // RG_NSEG, seg_len=SEQ, chained=False, row0=0)
        hl_new = jnp.transpose(hl_c, (0, 2, 1, 3)).reshape(BATCH, 2, RG_WIDTH)
        h0_lat = jnp.broadcast_to(state_rglru_h[:, l][:, :, None, :], (DEC_BATCH, 2, RG_NSEG, RG_WIDTH))
        oc_l, _ = _rglru(proj, *rg_args, h0_lat, nblk=DEC_BATCH, seg_len=DEC_SEQ // RG_NSEG, chained=True, row0=SEG)
        ob_c, od_c = _ctx_attn(proj, sw_sink[l])
        ob_l = _swa(proj, cache_swa_k[:, l].reshape(DEC_BATCH, PAST_LEN, 128),
                    cache_swa_v[:, l].reshape(DEC_BATCH, PAST_LEN, 128), sw_sink[l])
        od_l = _na(proj, cache_na_k[:, l].reshape(DEC_BATCH, PAST_LEN, 256),
                   cache_na_v[:, l].reshape(DEC_BATCH, PAST_LEN, 256), _na_bias(na_rpb[l].reshape(-1)))
        moe_layer = l % 2 == 1
        j = l // 2
        router = (moe_wr[j].T, moe_br[j]) if moe_layer else None
        outs = _merge(x_all, mod_l, norm1_g[l].reshape(1, D_MODEL), norm2_g[l].reshape(1, D_MODEL),
                      (hf_c, hf_l), (hb_c, hb_l), proj, (ob_c, ob_l), (oc_c, oc_l), (od_c, od_l),
                      _tile2(ml_hn[l]), w_mg[l].astype(BF16), b_mg[l].reshape(1, -1), w_br[l].astype(BF16),
                      w_out[l].astype(BF16), router)
        if moe_layer:
            x1, h2, route = outs
            plan = _moe_plan(route[:, 2:4].astype(jnp.int32))
            y_slots = _moe_group(*plan, h2, moe_w1[j].astype(BF16), moe_w3[j].astype(BF16), moe_w2[j].astype(BF16))
            x_all = tuple(_moe_combine(x1, y_slots, route, mod_l))
        else:
            x1, h2 = outs
            x_all = _ffn(h2, x1, mod_l, ffn_w1[j].astype(BF16), ffn_w3[j].astype(BF16), ffn_w2[j].astype(BF16))
        pc = proj[:SEG]
        ctx_out.append(dict(
            sw_k=pc[:, C_SK:C_SK + 128].reshape(BATCH, SEQ, SW_KV_HEADS, HEAD_DIM),
            sw_v=pc[:, C_SV:C_SV + 128].reshape(BATCH, SEQ, SW_KV_HEADS, HEAD_DIM),
            na_k=pc[:, C_NK:C_NK + 256].reshape(BATCH, SEQ, NA_HEADS, HEAD_DIM),
            na_v=pc[:, C_NV:C_NV + 256].reshape(BATCH, SEQ, NA_HEADS, HEAD_DIM),
            ml_C=c_new.reshape(BATCH, 2, ML_HEADS, HEAD_DIM, HEAD_DIM),
            ml_n=n_new.reshape(BATCH, 2, ML_HEADS, HEAD_DIM),
            ml_m=m_new.reshape(BATCH, 2, ML_HEADS),
            rg_h=hl_new))
    stack = lambda name: jnp.stack([t[name] for t in ctx_out], axis=1)
    return (x_all[0].reshape(BATCH, SEQ, D_MODEL), x_all[1].reshape(DEC_BATCH, DEC_SEQ, D_MODEL),
            stack('sw_k'), stack('sw_v'), stack('na_k'), stack('na_v'),
            stack('ml_C'), stack('ml_n'), stack('ml_m'), stack('rg_h'))
```

```python
import functools

import numpy as np
import jax
import jax.numpy as jnp
from jax import lax
from jax.experimental import pallas as pl
from jax.experimental.pallas import tpu as pltpu

F32 = jnp.float32
BF16 = jnp.bfloat16

D_MODEL = 1024
BATCH = 16
SEQ = 256
DEPTH = 2
DEC_BATCH = 2
DEC_SEQ = 4096
PAST_LEN = 256
GRID_W = 64
HEAD_DIM = 64
ML_HEADS = 4
ML_CHUNK = 128
ML_SUB = 2
SW_HEADS = 4
SW_KV_HEADS = 2
SW_WINDOW = 128
RG_WIDTH = 256
RG_BLOCKS = 4
RG_CONV = 4
RG_C = 8.0
NA_HEADS = 4
NA_ROWS = 8
NA_COLS = 16
N_BRANCH = 4
ROPE_BASE = 10000.0
D_FF = 2816
N_EXPERTS = 8
D_FF_EXPERT = 2048
EPS = 1e-6
NEG = -1e30
SCALE = HEAD_DIM ** -0.5

SEG = 4096
N_SEG = 3
N_TOK = N_SEG * SEG
LANES = 128
VMEM_LIMIT = 56 * 1024 * 1024

C_MQ, C_MK, C_MV, C_MO = 0, 256, 512, 768
C_SQ, C_RX, C_RY, C_NQ, C_NK, C_NV = 1024, 1280, 1536, 1792, 2048, 2304
C_SK, C_SV, C_G = 2560, 2688, 2816
P_W = 2944


def _cparams(sem):
    return pltpu.CompilerParams(dimension_semantics=sem, vmem_limit_bytes=VMEM_LIMIT)


def _dot(a, b):
    return jnp.dot(a.astype(BF16), b.astype(BF16), preferred_element_type=F32)


def _dot_nt(a, b):
    return lax.dot_general(a.astype(BF16), b.astype(BF16), (((1,), (1,)), ((), ())),
                           preferred_element_type=F32)


def _dot_tn(a, b):
    return lax.dot_general(a.astype(BF16), b.astype(BF16), (((0,), (0,)), ((), ())),
                           preferred_element_type=F32)


def _split3(x):
    hi = x.astype(BF16)
    r1 = x - hi.astype(F32)
    mid = r1.astype(BF16)
    lo = (r1 - mid.astype(F32)).astype(BF16)
    return hi, mid, lo


def _dot_exact_rhs(a01, x):
    hi, mid, lo = _split3(x)
    d = lambda p: jnp.dot(a01, p, preferred_element_type=F32)
    return d(hi) + d(mid) + d(lo)


def _dot_exact_lhs(x, a01):
    hi, mid, lo = _split3(x)
    d = lambda p: jnp.dot(p, a01, preferred_element_type=F32)
    return d(hi) + d(mid) + d(lo)


def _sigmoid(x):
    return 0.5 * jnp.tanh(0.5 * x) + 0.5


def _rms(x, g):
    return x * lax.rsqrt(jnp.mean(x * x, axis=-1, keepdims=True) + EPS) * g


def _rms_head_pairs(x, g):
    lane = lax.broadcasted_iota(jnp.int32, x.shape, 1)
    left = lane < HEAD_DIM
    sq = x * x
    s0 = jnp.sum(jnp.where(left, sq, 0.0), axis=-1, keepdims=True)
    s1 = jnp.sum(jnp.where(left, 0.0, sq), axis=-1, keepdims=True)
    ms = jnp.where(left, s0, s1) * (1.0 / HEAD_DIM)
    return x * lax.rsqrt(ms + EPS) * g


MOD_TN = 1536
MOD_ROWS = 3


def _mod_kernel(ct_ref, w_ref, b_ref, o_ref):
    ct = ct_ref[...]
    st = ct * jax.nn.sigmoid(ct)
    w = w_ref[0]
    o_ref[...] = jnp.zeros(o_ref.shape, F32)
    for r in range(MOD_ROWS):
        o_ref[0, r:r + 1, :] = jnp.sum(w * st[:, r:r + 1], axis=0, keepdims=True) + b_ref[0]


def _mod_table(cvecs_t, w_ada, b_ada):
    n = 6 * D_MODEL
    return pl.pallas_call(
        _mod_kernel,
        grid=(DEPTH, n // MOD_TN),
        in_specs=[pl.BlockSpec((D_MODEL, 8), lambda l, j: (0, 0)),
                  pl.BlockSpec((1, D_MODEL, MOD_TN), lambda l, j: (l, 0, j)),
                  pl.BlockSpec((1, 1, MOD_TN), lambda l, j: (l, 0, j))],
        out_specs=pl.BlockSpec((1, 8, MOD_TN), lambda l, j: (l, 0, j)),
        out_shape=jax.ShapeDtypeStruct((DEPTH, 8, n), F32),
        compiler_params=_cparams(("arbitrary", "arbitrary")),
        name="adaln_mod",
    )(cvecs_t, w_ada, b_ada.reshape(DEPTH, 1, n))


IN_TM = 512


def _swap16(y):
    lane = lax.broadcasted_iota(jnp.int32, y.shape, 1)
    first = (lane % 32) < 16
    return jnp.where(first, pltpu.roll(y, LANES - 16, 1), pltpu.roll(y, 16, 1))


def _seg_pair_specs(tm, width, lat_row0=0):
    nt = SEG // tm
    lat_off = lat_row0 // tm
    return (pl.BlockSpec((tm, width), lambda s, i: (jnp.minimum(s * nt + i, nt - 1), 0)),
            pl.BlockSpec((tm, width), lambda s, i: (lat_off + jnp.maximum(s * nt + i - nt, 0), 0)))


TOK_TILE = D_MODEL // LANES


def _store_token_tiles(ref, x):
    for s in range(TOK_TILE):
        ref[pl.ds(s, x.shape[0], stride=TOK_TILE), :] = x[:, LANES * s:LANES * (s + 1)]


def _load_token_tiles(ref, n_tok):
    return jnp.concatenate([ref[pl.ds(s, n_tok, stride=TOK_TILE), :] for s in range(TOK_TILE)], axis=1)


def _pick(c_ref, l_ref):
    return jnp.where(pl.program_id(0) == 0, c_ref[...], l_ref[...])


def _x_pair(x):
    return (x, 0) if isinstance(x, tuple) else ((x, x), SEG)


def _inproj_kernel(xc_ref, xl_ref, mod_ref, g_ref, w_ref, qkg_ref, cos_ref, sin_ref, o_ref):
    seg = pl.program_id(0)
    mod = mod_ref[0]
    sh1 = mod[:, 0:D_MODEL]
    sc1 = mod[:, D_MODEL:2 * D_MODEL]
    h = _rms(_pick(xc_ref, xl_ref), g_ref[...]) * (1.0 + sc1) + sh1
    r = jnp.dot(h.astype(BF16), w_ref[...], preferred_element_type=F32)
    o_ref[:, 0:C_SQ] = r[:, 0:C_SQ]
    o_ref[:, C_RX:C_NQ] = r[:, C_RX:C_NQ]
    o_ref[:, C_NV:C_SK] = r[:, C_NV:C_SK]
    o_ref[:, C_SV:P_W] = r[:, C_SV:P_W]
    cos = cos_ref[...]
    sin = sin_ref[...]
    latent = seg > 0

    def rope(y):
        return jnp.where(latent, y * cos + _swap16(y) * sin, y)

    for p in range(2):
        a = C_SQ + LANES * p
        o_ref[:, a:a + LANES] = rope(_rms_head_pairs(r[:, a:a + LANES], qkg_ref[0:1, :]))
    o_ref[:, C_SK:C_SK + LANES] = rope(_rms_head_pairs(r[:, C_SK:C_SK + LANES], qkg_ref[1:2, :]))
    for p in range(2):
        a = C_NQ + LANES * p
        o_ref[:, a:a + LANES] = _rms_head_pairs(r[:, a:a + LANES], qkg_ref[2:3, :])
        a = C_NK + LANES * p
        o_ref[:, a:a + LANES] = _rms_head_pairs(r[:, a:a + LANES], qkg_ref[3:4, :])


def _inproj(x, mod_l, norm1, w_in_p, qk_gains, cos_t, sin_t):
    nt = SEG // IN_TM
    x_pair, lat_row0 = _x_pair(x)
    return pl.pallas_call(
        _inproj_kernel,
        grid=(N_SEG, nt),
        in_specs=[*_seg_pair_specs(IN_TM, D_MODEL, lat_row0),
                  pl.BlockSpec((1, 1, 6 * D_MODEL), lambda s, i: (s, 0, 0)),
                  pl.BlockSpec((1, D_MODEL), lambda s, i: (0, 0)),
                  pl.BlockSpec((D_MODEL, P_W), lambda s, i: (0, 0)),
                  pl.BlockSpec((4, LANES), lambda s, i: (0, 0)),
                  pl.BlockSpec((IN_TM, LANES), lambda s, i: (i, 0)),
                  pl.BlockSpec((IN_TM, LANES), lambda s, i: (i, 0))],
        out_specs=pl.BlockSpec((IN_TM, P_W), lambda s, i: (s * nt + i, 0)),
        out_shape=jax.ShapeDtypeStruct((N_TOK, P_W), F32),
        compiler_params=_cparams(("arbitrary", "arbitrary")),
        name="inproj",
    )(*x_pair, mod_l, norm1, w_in_p, qk_gains, cos_t, sin_t)


def _mlstm_direction(d, rows, q_ref, k_ref, v_ref, g_ref, bias, tri_ref, sel_ref, state):
    ch = ML_CHUNK
    r_io = lax.broadcasted_iota(jnp.int32, (ch, ch), 0)
    c_io = lax.broadcasted_iota(jnp.int32, (ch, ch), 1)
    lower = r_io >= c_io
    upper = r_io <= c_io
    mask = lower if d == 0 else upper
    tri = tri_ref[d]
    tri_t = tri_ref[1 - d]
    left = c_io < HEAD_DIM
    top = r_io < HEAD_DIM
    blockdiag = top == left
    cbs_in, nbs_in, m_old = state
    g = g_ref[rows, :] + bias[...]
    b_cols = _dot_exact_rhs(tri, jax.nn.log_sigmoid(g))
    b3 = jnp.concatenate(_split3(b_cols), axis=1)
    gt = g.T
    li_rows = gt[0:8, :]
    b_rows = _dot_exact_lhs(jax.nn.log_sigmoid(gt[8:16, :]), tri_t)
    a_rows = li_rows - b_rows
    bl = b_rows[:, ch - 1:ch] if d == 0 else b_rows[:, 0:1]
    g_rows = bl - b_rows + li_rows
    m_new = jnp.maximum(bl + m_old, jnp.max(g_rows, axis=1, keepdims=True))
    wk_rows = jnp.exp(g_rows - m_new)
    wp = jnp.exp(bl + m_old - m_new)
    ones_blk = jnp.ones((ch, LANES), BF16)
    left2 = lax.broadcasted_iota(jnp.int32, (ch, 2 * LANES), 1) % LANES < HEAD_DIM
    h_out, c_out, n_out = [], [], []
    for p in range(ML_HEADS // 2):
        lanes = slice(LANES * p, LANES * (p + 1))
        j0 = ML_HEADS * d + 2 * p
        q2 = q_ref[rows, lanes]
        k2t = (k_ref[rows, lanes] * SCALE).T.astype(BF16)
        v2e = jnp.concatenate([v_ref[rows, lanes].astype(BF16), ones_blk], axis=1)
        cb = cbs_in[p]
        nb = nbs_in[p]
        q2b = q2.astype(BF16)
        q_lo = (q2 - q2b.astype(F32)).astype(BF16)
        nb_hi = nb.astype(BF16)
        nb_lo = (nb - nb_hi.astype(F32)).astype(BF16)
        qc = jnp.dot(q2b, cb.astype(BF16), preferred_element_type=F32)
        qn = jnp.dot(jnp.concatenate([q2b, q_lo, q2b], axis=1), jnp.concatenate([nb_hi, nb_hi, nb_lo], axis=0),
                     preferred_element_type=F32)
        b_pair = jnp.dot(b3, sel_ref[2 * d + p], preferred_element_type=F32)
        cbs, sves = [], []
        for i in range(2):
            j = j0 + i
            half = left if i == 0 else jnp.logical_not(left)
            a_mat = jnp.where(mask, a_rows[j:j + 1, :], NEG)
            cvec = jnp.maximum(m_old[j:j + 1, :], jnp.max(a_mat, axis=1, keepdims=True))
            cbro = jnp.broadcast_to(cvec, (ch, ch))
            s = jnp.dot(jnp.where(half, q2b, 0), k2t, preferred_element_type=F32) * jnp.exp(a_mat - cbro)
            s_hi = s.astype(BF16)
            s_lo = (s - s_hi.astype(F32)).astype(BF16)
            sve = jnp.dot(s_hi, v2e, preferred_element_type=F32)
            rs_lo = jnp.dot(s_lo, ones_blk, preferred_element_type=F32)
            sves.append(jnp.concatenate([sve[:, 0:LANES], sve[:, LANES:2 * LANES] + rs_lo], axis=1))
            cbs.append(cbro)
        c_pair = jnp.where(left, cbs[0], cbs[1])
        w_prev = jnp.exp(jnp.where(left, m_old[j0:j0 + 1, :], m_old[j0 + 1:j0 + 2, :]) - c_pair)
        sve = jnp.where(left2, sves[0], sves[1])
        num = w_prev * qc + sve[:, 0:LANES]
        den = w_prev * qn + sve[:, LANES:2 * LANES]
        h_out.append(num / jnp.maximum(jnp.abs(den), jnp.exp(-(c_pair + b_pair))))
        kwt = k2t * jnp.where(top, wk_rows[j0:j0 + 1, :], wk_rows[j0 + 1:j0 + 2, :])
        kwt_hi = kwt.astype(BF16)
        kwt_lo = (kwt - kwt_hi.astype(F32)).astype(BF16)
        kve = jnp.dot(kwt_hi, v2e, preferred_element_type=F32)
        kn = kve[:, LANES:2 * LANES] + jnp.dot(kwt_lo, ones_blk, preferred_element_type=F32)
        wp_pair = jnp.where(top, wp[j0:j0 + 1, :], wp[j0 + 1:j0 + 2, :])
        c_out.append(wp_pair * cb + jnp.where(blockdiag, kve[:, 0:LANES], 0.0))
        n_out.append(wp_pair * nb + jnp.where(blockdiag, kn, 0.0))
    return h_out, (c_out, n_out, m_new)


def _mlstm_kernel(qf, kf, vf, gf, qb, kb, vb, gb, c0, n0, m0, bias, tri_ref, sel_ref,
                  hf, hb, co, no, mo, cbd, nbd, m_s, *, nc):
    c = pl.program_id(1)

    @pl.when(c == 0)
    def _():
        cbd[...] = c0[0]
        nbd[...] = n0[0]
        m_s[...] = m0[0]

    npair = ML_HEADS // 2
    for d, (refs, h_ref) in enumerate((((qf, kf, vf, gf), hf), ((qb, kb, vb, gb), hb))):
        state = ([cbd[npair * d + p] for p in range(npair)], [nbd[npair * d + p] for p in range(npair)],
                 m_s[:, 0:1])
        subs = range(ML_SUB) if d == 0 else reversed(range(ML_SUB))
        for sub in subs:
            rows = slice(sub * ML_CHUNK, (sub + 1) * ML_CHUNK)
            h_out, state = _mlstm_direction(d, rows, *refs, bias, tri_ref, sel_ref, state)
            for p in range(npair):
                h_ref[rows, LANES * p:LANES * (p + 1)] = h_out[p]
        c_out, n_out, m_new = state
        for p in range(npair):
            cbd[npair * d + p] = c_out[p]
            nbd[npair * d + p] = n_out[p]
        heads = slice(ML_HEADS * d, ML_HEADS * (d + 1))
        m_s[heads, :] = jnp.broadcast_to(m_new[heads, :], (ML_HEADS, LANES))

    @pl.when(c == nc - 1)
    def _():
        co[0] = cbd[...]
        no[0] = nbd[...]
        mo[0] = m_s[...]


def _mlstm_pack_state(c0, n0, m0):
    bsz = c0.shape[0]
    hd = HEAD_DIM
    eye = jnp.eye(2, dtype=F32)[None, None, :, None, :, None]
    cbd = c0.reshape(bsz, ML_HEADS, 2, hd, 1, hd) * eye
    nbd = jnp.broadcast_to(n0.reshape(bsz, ML_HEADS, 2, hd, 1, 1) * eye, cbd.shape)
    to_mat = lambda t: t.reshape(bsz, ML_HEADS, LANES, LANES)
    return to_mat(cbd), to_mat(nbd), jnp.broadcast_to(m0[..., None], m0.shape + (LANES,))


def _mlstm_unpack_state(cbd, nbd, mrow):
    hd = HEAD_DIM
    bsz = cbd.shape[0]
    c = jnp.stack([cbd[:, :, :hd, :hd], cbd[:, :, hd:, hd:]], axis=2).reshape(bsz, 2 * ML_HEADS, hd, hd)
    n = jnp.stack([nbd[:, :, :hd, 0], nbd[:, :, hd:, hd]], axis=2).reshape(bsz, 2 * ML_HEADS, hd)
    return c, n, mrow[:, :, 0]


def _mlstm_tables():
    r = np.arange(ML_CHUNK)
    lower = (r[:, None] >= r[None, :]).astype(np.float32)
    row = np.arange(3 * LANES)[:, None] % LANES
    lane_left = np.arange(LANES)[None, :] < HEAD_DIM
    sel = [row == np.where(lane_left, 8 + ML_HEADS * d + 2 * p, 9 + ML_HEADS * d + 2 * p)
           for d in range(2) for p in range(ML_HEADS // 2)]
    return (jnp.asarray(np.stack([lower, lower.T]), dtype=BF16),
            jnp.asarray(np.stack(sel).astype(np.float32), dtype=BF16))


def _mlstm(proj, gate_bias, c0, n0, m0, *, bsz, seq, row0):
    blk = ML_SUB * ML_CHUNK
    nc = seq // blk
    tri, sel = _mlstm_tables()
    base = row0 // blk
    nj = 2 * ML_HEADS
    fw = lambda col: (lambda b, c: (base + b * nc + c, col))
    bw = lambda col: (lambda b, c: (base + b * nc + nc - 1 - c, col))
    qkv = lambda f: [pl.BlockSpec((blk, 256), f(C_MQ // 256)),
                     pl.BlockSpec((blk, 256), f(C_MK // 256)),
                     pl.BlockSpec((blk, 256), f(C_MV // 256)),
                     pl.BlockSpec((blk, LANES), f(C_G // LANES))]
    st_specs = [pl.BlockSpec((1, nj // 2, LANES, LANES), lambda b, c: (b, 0, 0, 0)),
                pl.BlockSpec((1, nj // 2, LANES, LANES), lambda b, c: (b, 0, 0, 0)),
                pl.BlockSpec((1, nj, LANES), lambda b, c: (b, 0, 0))]
    st_shapes = [jax.ShapeDtypeStruct((bsz, nj // 2, LANES, LANES), F32),
                 jax.ShapeDtypeStruct((bsz, nj // 2, LANES, LANES), F32),
                 jax.ShapeDtypeStruct((bsz, nj, LANES), F32)]
    return pl.pallas_call(
        functools.partial(_mlstm_kernel, nc=nc),
        grid=(bsz, nc),
        in_specs=qkv(fw) + qkv(bw) + st_specs + [pl.BlockSpec((1, LANES), lambda b, c: (0, 0)),
                                                 pl.BlockSpec(tri.shape, lambda b, c: (0, 0, 0)),
                                                 pl.BlockSpec(sel.shape, lambda b, c: (0, 0, 0))],
        out_specs=[pl.BlockSpec((blk, 256), lambda b, c: (b * nc + c, 0)),
                   pl.BlockSpec((blk, 256), lambda b, c: (b * nc + nc - 1 - c, 0))] + st_specs,
        out_shape=[jax.ShapeDtypeStruct((bsz * seq, 256), F32),
                   jax.ShapeDtypeStruct((bsz * seq, 256), F32)] + st_shapes,
        scratch_shapes=[pltpu.VMEM((nj // 2, LANES, LANES), F32),
                        pltpu.VMEM((nj // 2, LANES, LANES), F32),
                        pltpu.VMEM((nj, LANES), F32)],
        compiler_params=_cparams(("arbitrary", "arbitrary")),
        name="mlstm",
    )(proj, proj, proj, proj, proj, proj, proj, proj, c0, n0, m0, gate_bias, tri, sel)


RG_TC = 256
RG_PAD = 8
RG_NSEG = 8
RG_SKEW = 4


def _rglru_kernel(rx, ry, cw, cb, wg, bg, lam, h0, oc, hl, xpad, af, ab, uf, ub, *, seg_len, chained):
    rows = RG_NSEG * seg_len
    seq_len = rows if chained else seg_len
    halves = RG_WIDTH // LANES
    pitch = seg_len + RG_SKEW
    buf_row = lambda t: (t // seg_len) * pitch + t % seg_len
    xpad[0:RG_PAD, :] = jnp.zeros((RG_PAD, RG_WIDTH), F32)
    xpad[rows + RG_PAD:rows + 2 * RG_PAD, :] = jnp.zeros((RG_PAD, RG_WIDTH), F32)
    xpad[RG_PAD:rows + RG_PAD, :] = rx[...]
    sp = jax.nn.softplus(-lam[...])
    left = (RG_CONV - 1) // 2
    for ci in range(rows // RG_TC):
        s0 = ci * RG_TC
        pos = (s0 + lax.broadcasted_iota(jnp.int32, (RG_TC, RG_WIDTH), 0)) % seq_len
        xc = None
        for j in range(RG_CONV):
            a = RG_PAD + s0 + j - left
            term = xpad[a:a + RG_TC, :] * cw[j:j + 1, :]
            if not chained and j != left:
                term = jnp.where((pos + (j - left) >= 0) & (pos + (j - left) < seq_len), term, 0.0)
            xc = term if xc is None else xc + term
        xc = xc + cb[...]
        pre = _dot(xc, wg[...]) + bg[...]
        for d, (a_ref, u_ref) in enumerate(((af, uf), (ab, ub))):
            o = 2 * RG_WIDTH * d
            r = _sigmoid(pre[:, o:o + RG_WIDTH])
            gi = _sigmoid(pre[:, o + RG_WIDTH:o + 2 * RG_WIDTH])
            log_a = -RG_C * r * sp[d:d + 1, :]
            a_val = jnp.exp(log_a)
            u_val = jnp.sqrt(-jnp.tanh(log_a) * (a_val * a_val + 1.0)) * (gi * xc)
            for hv in range(halves):
                dst = slice(buf_row(s0), buf_row(s0) + RG_TC)
                a_ref[hv, dst, :] = a_val[:, LANES * hv:LANES * (hv + 1)]
                u_ref[hv, dst, :] = u_val[:, LANES * hv:LANES * (hv + 1)]

    def body(s, carry):
        out = []
        for d, (a_ref, u_ref) in enumerate(((af, uf), (ab, ub))):
            step_rows = pl.ds(s if d == 0 else seg_len - 1 - s, RG_NSEG, stride=pitch)
            for hv in range(halves):
                h_loc, prod = carry[2 * (halves * d + hv)], carry[2 * (halves * d + hv) + 1]
                a = a_ref[hv, step_rows, :]
                h_loc = a * h_loc + u_ref[hv, step_rows, :]
                prod = a * prod
                u_ref[hv, step_rows, :] = h_loc
                a_ref[hv, step_rows, :] = prod
                out += [h_loc, prod]
        return tuple(out)

    zero = jnp.zeros((RG_NSEG, LANES), F32)
    one = jnp.ones((RG_NSEG, LANES), F32)
    ends = lax.fori_loop(0, seg_len, body, (zero, one) * (2 * halves), unroll=8)
    seg = lax.broadcasted_iota(jnp.int32, (RG_NSEG, LANES), 0)
    for d, (a_ref, u_ref) in enumerate(((af, uf), (ab, ub))):
        for hv in range(halves):
            lanes = slice(LANES * hv, LANES * (hv + 1))
            h_end, p_end = ends[2 * (halves * d + hv)], ends[2 * (halves * d + hv) + 1]
            h_in = h0[0, d][:, lanes]
            if chained:
                c = h_in[0:1, :]
                h_in = zero
                for k in (range(RG_NSEG) if d == 0 else reversed(range(RG_NSEG))):
                    h_in = jnp.where(seg == k, c, h_in)
                    c = h_end[k:k + 1, :] + p_end[k:k + 1, :] * c
            hl[0, d, :, lanes] = h_end + p_end * h_in
            for k in range(RG_NSEG):
                for ci in range(seg_len // RG_TC):
                    sl = slice(k * pitch + ci * RG_TC, k * pitch + (ci + 1) * RG_TC)
                    u_ref[hv, sl, :] = u_ref[hv, sl, :] + a_ref[hv, sl, :] * h_in[k:k + 1, :]
    for ci in range(rows // RG_TC):
        sl = slice(ci * RG_TC, (ci + 1) * RG_TC)
        for hv in range(halves):
            lanes = slice(LANES * hv, LANES * (hv + 1))
            src = slice(buf_row(ci * RG_TC), buf_row(ci * RG_TC) + RG_TC)
            oc[sl, lanes] = (uf[hv, src, :] + ub[hv, src, :]) * jax.nn.gelu(ry[sl, lanes])


def _rglru(proj, cw, cb, wg, bg, lam, h0, *, nblk, seg_len, chained, row0):
    rows = RG_NSEG * seg_len
    base = row0 // rows
    full = lambda shape: pl.BlockSpec(shape, lambda b: tuple(0 for _ in shape))
    st_spec = pl.BlockSpec((1, 2, RG_NSEG, RG_WIDTH), lambda b: (b, 0, 0, 0))
    return pl.pallas_call(
        functools.partial(_rglru_kernel, seg_len=seg_len, chained=chained),
        grid=(nblk,),
        in_specs=[pl.BlockSpec((rows, RG_WIDTH), lambda b: (base + b, C_RX // RG_WIDTH)),
                  pl.BlockSpec((rows, RG_WIDTH), lambda b: (base + b, C_RY // RG_WIDTH)),
                  full((RG_CONV, RG_WIDTH)), full((1, RG_WIDTH)),
                  full((RG_WIDTH, 4 * RG_WIDTH)), full((1, 4 * RG_WIDTH)), full((2, RG_WIDTH)),
                  st_spec],
        out_specs=[pl.BlockSpec((rows, RG_WIDTH), lambda b: (b, 0)), st_spec],
        out_shape=[jax.ShapeDtypeStruct((nblk * rows, RG_WIDTH), F32),
                   jax.ShapeDtypeStruct((nblk, 2, RG_NSEG, RG_WIDTH), F32)],
        scratch_shapes=[pltpu.VMEM((rows + 2 * RG_PAD, RG_WIDTH), F32)]
        + [pltpu.VMEM((RG_WIDTH // LANES, RG_NSEG * (seg_len + RG_SKEW), LANES), F32) for _ in range(4)],
        compiler_params=_cparams(("arbitrary",)),
        name="rglru",
    )(proj, proj, cw, cb, wg, bg, lam, h0)


def _softmax_pv(scores, values, sink):
    m = functools.reduce(jnp.maximum, [jnp.max(s, axis=-1, keepdims=True) for s in scores])
    if sink is not None:
        m = jnp.maximum(m, sink)
    ps = [jnp.exp(s - m) for s in scores]
    den = functools.reduce(jnp.add, [jnp.sum(p, axis=-1, keepdims=True) for p in ps])
    if sink is not None:
        den = den + jnp.exp(sink - m)
    num = functools.reduce(jnp.add, [_dot(p, v) for p, v in zip(ps, values)])
    return num / den


def _ctx_attn_kernel(sink, sq, sk, sv, nq, nk, nv, ob, od):
    assert SW_HEADS == 4 and SW_KV_HEADS == 2
    left = lax.broadcasted_iota(jnp.int32, (SEQ, LANES), 1) < HEAD_DIM
    first = lax.broadcasted_iota(jnp.int32, (2 * SEQ, 1), 0) < SEQ
    k2 = sk[...].astype(BF16)
    v2 = sv[...].astype(BF16)
    for p in range(SW_KV_HEADS):
        lanes = slice(LANES * p, LANES * (p + 1))
        q2 = sq[:, lanes]
        q_swapped = pltpu.roll(q2, HEAD_DIM, 1)
        kv_half = left if p == 0 else jnp.logical_not(left)
        qs = jnp.concatenate([jnp.where(kv_half, q2 if i == p else q_swapped, 0.0) for i in range(2)], axis=0)
        sink_col = jnp.where(first, sink[2 * p], sink[2 * p + 1])
        res = _softmax_pv([_dot_nt(qs, k2) * SCALE], [v2], sink_col)
        halves = [res[0:SEQ], res[SEQ:2 * SEQ]]
        placed = [halves[i] if i == p else pltpu.roll(halves[i], HEAD_DIM, 1) for i in range(2)]
        ob[:, lanes] = jnp.where(left, placed[0], placed[1])
    for p in range(NA_HEADS // 2):
        lanes = slice(LANES * p, LANES * (p + 1))
        q2 = nq[:, lanes]
        k2 = nk[:, lanes].astype(BF16)
        v2 = nv[:, lanes].astype(BF16)
        res = [_softmax_pv([_dot_nt(jnp.where(left if i == 0 else jnp.logical_not(left), q2, 0.0), k2) * SCALE],
                           [v2], None) for i in range(2)]
        od[:, lanes] = jnp.where(left, res[0], res[1])


def _ctx_attn(proj, sink):
    blk = lambda w, col: pl.BlockSpec((SEQ, w), lambda b: (b, col))
    return pl.pallas_call(
        _ctx_attn_kernel,
        grid=(BATCH,),
        in_specs=[pl.BlockSpec(memory_space=pltpu.SMEM),
                  blk(256, C_SQ // 256), blk(128, C_SK // 128), blk(128, C_SV // 128),
                  blk(256, C_NQ // 256), blk(256, C_NK // 256), blk(256, C_NV // 256)],
        out_specs=[pl.BlockSpec((SEQ, 256), lambda b: (b, 0)),
                   pl.BlockSpec((SEQ, 256), lambda b: (b, 0))],
        out_shape=[jax.ShapeDtypeStruct((SEG, 256), F32), jax.ShapeDtypeStruct((SEG, 256), F32)],
        compiler_params=_cparams(("arbitrary",)),
        name="ctx_attn",
    )(sink, proj, proj, proj, proj, proj, proj)


SW_QB = 128
SW_SPAN = SW_QB + 2 * SW_WINDOW


def _swa_kernel(sink, q, k, v, kc, vc, ob):
    assert SW_HEADS == 4 and SW_KV_HEADS == 2
    n = pl.program_id(1)
    ws = jnp.clip((n - 1) * SW_QB, 0, DEC_SEQ - SW_SPAN)
    ws = pl.multiple_of(ws, SW_QB)
    row = lax.broadcasted_iota(jnp.int32, (2 * SW_QB, SW_SPAN), 0)
    qpos = n * SW_QB + row % SW_QB
    kpos = ws + lax.broadcasted_iota(jnp.int32, (2 * SW_QB, SW_SPAN), 1)
    valid = jnp.abs(qpos - kpos) <= SW_WINDOW
    left = lax.broadcasted_iota(jnp.int32, (SW_QB, LANES), 1) < HEAD_DIM
    first = lax.broadcasted_iota(jnp.int32, (2 * SW_QB, 1), 0) < SW_QB
    k2 = k[pl.ds(ws, SW_SPAN), :].astype(BF16)
    v2 = v[pl.ds(ws, SW_SPAN), :].astype(BF16)
    kc2 = kc[0].astype(BF16)
    vc2 = vc[0].astype(BF16)
    for p in range(SW_KV_HEADS):
        lanes = slice(LANES * p, LANES * (p + 1))
        q2 = q[:, lanes]
        q_swapped = pltpu.roll(q2, HEAD_DIM, 1)
        kv_half = left if p == 0 else jnp.logical_not(left)
        qs = jnp.concatenate([jnp.where(kv_half, q2 if i == p else q_swapped, 0.0) for i in range(2)], axis=0)
        s_loc = jnp.where(valid, _dot_nt(qs, k2) * SCALE, NEG)
        s_ctx = _dot_nt(qs, kc2) * SCALE
        sink_col = jnp.where(first, sink[2 * p], sink[2 * p + 1])
        res = _softmax_pv([s_loc, s_ctx], [v2, vc2], sink_col)
        halves = [res[0:SW_QB], res[SW_QB:2 * SW_QB]]
        placed = [halves[i] if i == p else pltpu.roll(halves[i], HEAD_DIM, 1) for i in range(2)]
        ob[:, lanes] = jnp.where(left, placed[0], placed[1])


def _swa(proj, kc, vc, sink):
    nq = DEC_SEQ // SW_QB
    qbase = SEG // SW_QB
    return pl.pallas_call(
        _swa_kernel,
        grid=(DEC_BATCH, nq),
        in_specs=[pl.BlockSpec(memory_space=pltpu.SMEM),
                  pl.BlockSpec((SW_QB, 256), lambda b, n: (qbase + b * nq + n, C_SQ // 256)),
                  pl.BlockSpec((DEC_SEQ, 128), lambda b, n: (1 + b, C_SK // 128)),
                  pl.BlockSpec((DEC_SEQ, 128), lambda b, n: (1 + b, C_SV // 128)),
                  pl.BlockSpec((1, PAST_LEN, 128), lambda b, n: (b, 0, 0)),
                  pl.BlockSpec((1, PAST_LEN, 128), lambda b, n: (b, 0, 0))],
        out_specs=pl.BlockSpec((SW_QB, 256), lambda b, n: (b * nq + n, 0)),
        out_shape=jax.ShapeDtypeStruct((DEC_BATCH * DEC_SEQ, 256), F32),
        compiler_params=_cparams(("arbitrary", "arbitrary")),
        name="swa",
    )(sink, proj, proj, proj, kc, vc)


NA_RPB_R = 2 * NA_ROWS - 1
NA_RPB_C = 2 * NA_COLS - 1
GRID_ROWS = DEC_SEQ // GRID_W
NA_RB = 4
NA_UW = 12
NA_NQ = NA_RB * GRID_W
NA_NKEY = NA_UW * GRID_W
NA_NBLK = GRID_ROWS // NA_RB
NA_CASES = ((0, 0), (NA_RB, 0), (GRID_ROWS - NA_RB, GRID_ROWS - NA_UW))


def _na_row_start(qrow):
    return min(max(qrow - NA_ROWS // 2, 0), GRID_ROWS - NA_ROWS)


def _na_bias_kernel(rpb, out):
    h = pl.program_id(0)
    qc = lax.broadcasted_iota(jnp.int32, (GRID_W, GRID_W), 0)
    kc = lax.broadcasted_iota(jnp.int32, (GRID_W, GRID_W), 1)
    dc = jnp.clip(kc - qc, -(NA_COLS - 1), NA_COLS - 1) + NA_COLS - 1
    lo = jnp.clip(qc - NA_COLS // 2, 0, GRID_W - NA_COLS)
    valid = (kc >= lo) & (kc < lo + NA_COLS)
    tiles = []
    for dr in range(NA_RPB_R):
        t = jnp.zeros((GRID_W, GRID_W), F32)
        for j in range(NA_RPB_C):
            t = jnp.where(dc == j, rpb[(h * NA_RPB_R + dr) * NA_RPB_C + j], t)
        tiles.append(jnp.where(valid, t, NEG))
    outside = jnp.full((GRID_W, GRID_W), NEG, F32)
    for case, (q0, k0) in enumerate(NA_CASES):
        for a in range(NA_RB):
            r0 = _na_row_start(q0 + a)
            for i in range(NA_UW):
                inside = r0 <= k0 + i < r0 + NA_ROWS
                tile = tiles[k0 + i - (q0 + a) + NA_ROWS - 1] if inside else outside
                out[0, case, GRID_W * a:GRID_W * (a + 1), GRID_W * i:GRID_W * (i + 1)] = tile


def _na_bias(rpb_flat):
    shape = (NA_HEADS, len(NA_CASES), NA_NQ, NA_NKEY)
    return pl.pallas_call(
        _na_bias_kernel,
        grid=(NA_HEADS,),
        in_specs=[pl.BlockSpec(memory_space=pltpu.SMEM)],
        out_specs=pl.BlockSpec((1,) + shape[1:], lambda h: (h, 0, 0, 0)),
        out_shape=jax.ShapeDtypeStruct(shape, F32),
        compiler_params=_cparams(("arbitrary",)),
        name="na_bias",
    )(rpb_flat)


def _na_kernel(q, k, v, kc, vc, bias, od):
    blk = pl.program_id(1)
    case = jnp.where(blk == 0, 0, jnp.where(blk == NA_NBLK - 1, 2, 1))
    u0 = jnp.clip(blk * NA_RB - NA_ROWS // 2, 0, GRID_ROWS - NA_UW)
    k0 = pl.multiple_of(u0 * GRID_W, GRID_W)
    left = lax.broadcasted_iota(jnp.int32, (NA_NQ, LANES), 1) < HEAD_DIM
    for p in range(NA_HEADS // 2):
        lanes = slice(LANES * p, LANES * (p + 1))
        q2 = q[:, lanes]
        k2 = k[pl.ds(k0, NA_NKEY), lanes].astype(BF16)
        v2 = v[pl.ds(k0, NA_NKEY), lanes].astype(BF16)
        kc2 = kc[0, :, lanes].astype(BF16)
        vc2 = vc[0, :, lanes].astype(BF16)
        res = []
        for i in range(2):
            qm = jnp.where(left if i == 0 else jnp.logical_not(left), q2, 0.0)
            s_loc = _dot_nt(qm, k2) * SCALE + bias[2 * p + i, pl.ds(case, 1)][0]
            s_ctx = _dot_nt(qm, kc2) * SCALE
            res.append(_softmax_pv([s_loc, s_ctx], [v2, vc2], None))
        od[:, lanes] = jnp.where(left, res[0], res[1])


def _na(proj, kc, vc, bias):
    qbase = SEG // NA_NQ
    return pl.pallas_call(
        _na_kernel,
        grid=(DEC_BATCH, NA_NBLK),
        in_specs=[pl.BlockSpec((NA_NQ, 256), lambda b, r: (qbase + b * NA_NBLK + r, C_NQ // 256)),
                  pl.BlockSpec((DEC_SEQ, 256), lambda b, r: (1 + b, C_NK // 256)),
                  pl.BlockSpec((DEC_SEQ, 256), lambda b, r: (1 + b, C_NV // 256)),
                  pl.BlockSpec((1, PAST_LEN, 256), lambda b, r: (b, 0, 0)),
                  pl.BlockSpec((1, PAST_LEN, 256), lambda b, r: (b, 0, 0)),
                  pl.BlockSpec((NA_HEADS, len(NA_CASES), NA_NQ, NA_NKEY), lambda b, r: (0, 0, 0, 0))],
        out_specs=pl.BlockSpec((NA_NQ, 256), lambda b, r: (b * NA_NBLK + r, 0)),
        out_shape=jax.ShapeDtypeStruct((DEC_BATCH * DEC_SEQ, 256), F32),
        compiler_params=_cparams(("arbitrary", "arbitrary")),
        name="na",
    )(proj, proj, proj, kc, vc, bias)


MG_ROWS = 256
MG_SUB = 2
MG_TM = MG_SUB * MG_ROWS


def _merge_kernel(xc_ref, xl_ref, mod_ref, g1_ref, g2_ref, hf_c, hf_l, hb_c, hb_l, mo, ob_c, ob_l, oc_c, oc_l,
                  od_c, od_l, hn, wmg, bmg, wbr, wout, *rest, moe):
    if moe:
        wrt, br, x1_ref, h2_ref, route_ref = rest
    else:
        x1_ref, h2_ref = rest
    ctx = pl.program_id(0) == 0
    mod = mod_ref[0]
    chunk = lambda i: mod[:, i * D_MODEL:(i + 1) * D_MODEL]
    sh1, sc1, gate1, sh2, sc2 = chunk(0), chunk(1), chunk(2), chunk(3), chunk(4)

    def row_group(rows):
        pick = lambda c_ref, l_ref: jnp.where(ctx, c_ref[rows, :], l_ref[rows, :])
        x = pick(xc_ref, xl_ref)
        h = (_rms(x, g1_ref[...]) * (1.0 + sc1) + sh1).astype(BF16)
        hsum = pick(hf_c, hf_l) + pick(hb_c, hb_l)
        out_a = jnp.concatenate(
            [_rms_head_pairs(hsum[:, LANES * p:LANES * (p + 1)], hn[...]) for p in range(2)], axis=-1)
        out_a = out_a * jax.nn.sigmoid(mo[rows, :])
        acc = None
        for n, br_val in enumerate((out_a, pick(ob_c, ob_l), pick(oc_c, oc_l), pick(od_c, od_l))):
            gate = jax.nn.sigmoid(jnp.dot(h, wmg[:, n * D_MODEL:(n + 1) * D_MODEL], preferred_element_type=F32)
                                  + bmg[:, n * D_MODEL:(n + 1) * D_MODEL])
            term = gate * jnp.dot(br_val.astype(BF16), wbr[n], preferred_element_type=F32)
            acc = term if acc is None else acc + term
        y = jnp.dot(acc.astype(BF16), wout[...], preferred_element_type=F32)
        x1 = x + gate1 * y
        x1_ref[rows, :] = x1
        h2 = _rms(x1, g2_ref[...]) * (1.0 + sc2) + sh2
        if moe:
            _store_token_tiles(h2_ref.at[pl.ds(rows.start * TOK_TILE, MG_ROWS * TOK_TILE)], h2)
        else:
            h2_ref[rows, :] = h2.astype(BF16)
        if moe:
            logit = [jnp.sum(h2 * wrt[e:e + 1, :], axis=-1, keepdims=True) + br[e] for e in range(N_EXPERTS)]
            v1, i1 = logit[0], jnp.zeros(logit[0].shape, jnp.int32)
            for e in range(1, N_EXPERTS):
                better = logit[e] > v1
                v1 = jnp.where(better, logit[e], v1)
                i1 = jnp.where(better, e, i1)
            v2, i2 = jnp.full(v1.shape, -jnp.inf, F32), jnp.zeros(v1.shape, jnp.int32)
            for e in range(N_EXPERTS):
                better = (i1 != e) & (logit[e] > v2)
                v2 = jnp.where(better, logit[e], v2)
                i2 = jnp.where(better, e, i2)
            e2 = jnp.exp(v2 - v1)
            den = 1.0 + e2
            lane = lax.broadcasted_iota(jnp.int32, (MG_ROWS, LANES), 1)
            route = jnp.where(lane == 0, 1.0 / den, 0.0) + jnp.where(lane == 1, e2 / den, 0.0)
            route = route + jnp.where(lane == 2, i1.astype(F32), 0.0) + jnp.where(lane == 3, i2.astype(F32), 0.0)
            route_ref[rows, :] = route

    for sub in range(MG_SUB):
        row_group(slice(sub * MG_ROWS, (sub + 1) * MG_ROWS))


def _merge(x, mod_l, g1, g2, hf, hb, proj, ob, oc, od, hn, wmg, bmg, wbr, wout, router=None):
    nt = SEG // MG_TM
    moe = router is not None
    x_pair, lat_row0 = _x_pair(x)
    row = lambda w: pl.BlockSpec((MG_TM, w), lambda s, i: (s * nt + i, 0))
    ctx_blk, lat_blk = _seg_pair_specs(MG_TM, 256)
    full = lambda shape: pl.BlockSpec(shape, lambda s, i: tuple(0 for _ in shape), pipeline_mode=pl.Buffered(1))
    in_specs = [*_seg_pair_specs(MG_TM, D_MODEL, lat_row0),
                pl.BlockSpec((1, 1, 6 * D_MODEL), lambda s, i: (s, 0, 0)),
                full((1, D_MODEL)), full((1, D_MODEL)),
                ctx_blk, lat_blk, ctx_blk, lat_blk,
                pl.BlockSpec((MG_TM, 256), lambda s, i: (s * nt + i, C_MO // 256)),
                ctx_blk, lat_blk, ctx_blk, lat_blk, ctx_blk, lat_blk,
                full((1, LANES)), full((D_MODEL, N_BRANCH * D_MODEL)), full((1, N_BRANCH * D_MODEL)),
                full((N_BRANCH, 256, D_MODEL)), full((D_MODEL, D_MODEL))]
    args = [*x_pair, mod_l, g1, g2, *hf, *hb, proj, *ob, *oc, *od, hn, wmg, bmg, wbr, wout]
    if moe:
        h2_spec = pl.BlockSpec((MG_TM * TOK_TILE, LANES), lambda s, i: (s * nt + i, 0))
        h2_shape = jax.ShapeDtypeStruct((N_TOK * TOK_TILE, LANES), F32)
    else:
        h2_spec, h2_shape = row(D_MODEL), jax.ShapeDtypeStruct((N_TOK, D_MODEL), BF16)
    out_specs = [row(D_MODEL), h2_spec]
    out_shape = [jax.ShapeDtypeStruct((N_TOK, D_MODEL), F32), h2_shape]
    if moe:
        in_specs += [full((N_EXPERTS, D_MODEL)), pl.BlockSpec(memory_space=pltpu.SMEM)]
        args += list(router)
        out_specs.append(row(LANES))
        out_shape.append(jax.ShapeDtypeStruct((N_TOK, LANES), F32))
    return pl.pallas_call(
        functools.partial(_merge_kernel, moe=moe),
        grid=(N_SEG, nt),
        in_specs=in_specs,
        out_specs=out_specs,
        out_shape=out_shape,
        compiler_params=_cparams(("arbitrary", "arbitrary")),
        name="merge",
    )(*args)


FF_TM = 512
FF_TF = 1408


def _ffn_kernel(h2, w1, w3, w2, x1, mod_ref, out, acc):
    f = pl.program_id(1)

    @pl.when(f == 0)
    def _():
        acc[...] = jnp.zeros(acc.shape, F32)

    h = h2[...]
    a = jnp.dot(h, w1[...], preferred_element_type=F32)
    b = jnp.dot(h, w3[...], preferred_element_type=F32)
    act = (jax.nn.silu(a) * b).astype(BF16)
    acc[...] += jnp.dot(act, w2[...], preferred_element_type=F32)

    @pl.when(f == pl.num_programs(1) - 1)
    def _():
        gate2 = mod_ref[0][:, 5 * D_MODEL:6 * D_MODEL]
        out[...] = x1[...] + gate2 * acc[...]


def _ffn(h2, x1, mod_l, w1, w3, w2):
    nt = N_TOK // FF_TM
    per_seg = SEG // FF_TM
    return pl.pallas_call(
        _ffn_kernel,
        grid=(nt, D_FF // FF_TF),
        in_specs=[pl.BlockSpec((FF_TM, D_MODEL), lambda i, f: (i, 0)),
                  pl.BlockSpec((D_MODEL, FF_TF), lambda i, f: (0, f)),
                  pl.BlockSpec((D_MODEL, FF_TF), lambda i, f: (0, f)),
                  pl.BlockSpec((FF_TF, D_MODEL), lambda i, f: (f, 0)),
                  pl.BlockSpec((FF_TM, D_MODEL), lambda i, f: (i, 0)),
                  pl.BlockSpec((1, 1, 6 * D_MODEL), lambda i, f: (i // per_seg, 0, 0))],
        out_specs=pl.BlockSpec((FF_TM, D_MODEL), lambda i, f: (i, 0)),
        out_shape=jax.ShapeDtypeStruct((N_TOK, D_MODEL), F32),
        scratch_shapes=[pltpu.VMEM((FF_TM, D_MODEL), F32)],
        compiler_params=_cparams(("arbitrary", "arbitrary")),
        name="ffn",
    )(h2, w1, w3, w2, x1, mod_l)


MOE_TM = 256
MOE_SLOTS = 2 * N_TOK
MOE_TILES = MOE_SLOTS // MOE_TM + N_EXPERTS
MOE_NBUF = 3
MOE_STEPS = MOE_TILES + MOE_NBUF
MOE_DUMP = MOE_NBUF * MOE_TM
MOE_LEAD = 1
MOE_PLAN_TILES = MOE_LEAD + MOE_TILES + 2
MOE_FCHUNKS = 1
MOE_UNROLL = 8


def _moe_group_kernel(texp, nused, src_tok, dst_row, h2_hbm, w1, w3, w2, y_hbm, xs, ys, sem_in, sem_out):
    del texp
    i = pl.program_id(0)
    n_used = nused[0]
    buf = i % MOE_NBUF
    buf_next = (i + 2) % MOE_NBUF

    def tile_rows(t):
        start = t * TOK_TILE
        return pl.ds(start if isinstance(start, int) else pl.multiple_of(start, TOK_TILE), TOK_TILE)

    def gather_copy(tile, b, r):
        tok = src_tok[(tile + MOE_LEAD) * MOE_TM + r]
        return pltpu.make_async_copy(h2_hbm.at[tile_rows(tok)], xs.at[b, tile_rows(r)], sem_in.at[b])

    def scatter_copy(tile, b, r):
        dst = dst_row[(tile + MOE_LEAD) * MOE_TM + r]
        return pltpu.make_async_copy(ys.at[b, tile_rows(r)], y_hbm.at[tile_rows(dst)], sem_out.at[b])

    def start_rows_loop(make_copy, tile, b):
        def body(r, carry):
            make_copy(tile, b, r).start()
            return carry
        lax.fori_loop(0, MOE_TM, body, 0, unroll=MOE_UNROLL)

    def wait_tile(b, gather):
        if gather:
            pltpu.make_async_copy(h2_hbm.at[pl.ds(0, MOE_TM * TOK_TILE)], xs.at[b], sem_in.at[b]).wait()
        else:
            pltpu.make_async_copy(ys.at[b], y_hbm.at[pl.ds(0, MOE_TM * TOK_TILE)], sem_out.at[b]).wait()

    @pl.when(i == 0)
    def _():
        xs[...] = jnp.zeros(xs.shape, F32)
        ys[...] = jnp.zeros(ys.shape, F32)
        for b in range(MOE_NBUF):
            fill = pltpu.make_async_copy(ys.at[b], y_hbm.at[pl.ds((MOE_SLOTS + b * MOE_TM) * TOK_TILE, MOE_TM * TOK_TILE)],
                                         sem_out.at[b])
            fill.start()
            fill.wait()
        start_rows_loop(gather_copy, 0, 0)
        start_rows_loop(gather_copy, 1, 1)

    @pl.when(i <= n_used + 1)
    def _():
        wait_tile(buf, True)

    @pl.when((i >= 2) & (i <= n_used + 2))
    def _():
        wait_tile(buf, False)

    @pl.when(i < n_used)
    def _():
        x = _load_token_tiles(xs.at[buf], MOE_TM).astype(BF16)
        fc = D_FF_EXPERT // MOE_FCHUNKS
        rc = MOE_TM // MOE_FCHUNKS
        y = None
        for c in range(MOE_FCHUNKS):
            for r in range(c * rc, (c + 1) * rc):
                gather_copy(i + 2, buf_next, r).start()
                scatter_copy(i - 1, buf_next, r).start()
            a = jnp.dot(x, w1[0, :, c * fc:(c + 1) * fc], preferred_element_type=F32)
            b = jnp.dot(x, w3[0, :, c * fc:(c + 1) * fc], preferred_element_type=F32)
            act = (jax.nn.silu(a) * b).astype(BF16)
            part = jnp.dot(act, w2[0, c * fc:(c + 1) * fc, :], preferred_element_type=F32)
            y = part if y is None else y + part
        _store_token_tiles(ys.at[buf], y)

    @pl.when(i == n_used)
    def _():
        start_rows_loop(scatter_copy, i - 1, buf_next)


def _moe_group(tile_expert, n_used, src_tok, dst_row, h2, w1, w3, w2):
    wspec = lambda shape: pl.BlockSpec((1,) + shape, lambda i, texp, *_: (texp[jnp.minimum(i, MOE_TILES - 1)], 0, 0))
    grid_spec = pltpu.PrefetchScalarGridSpec(
        num_scalar_prefetch=4,
        grid=(MOE_STEPS,),
        in_specs=[pl.BlockSpec(memory_space=pl.ANY),
                  wspec((D_MODEL, D_FF_EXPERT)), wspec((D_MODEL, D_FF_EXPERT)), wspec((D_FF_EXPERT, D_MODEL))],
        out_specs=pl.BlockSpec(memory_space=pl.ANY),
        scratch_shapes=[pltpu.VMEM((MOE_NBUF, MOE_TM * TOK_TILE, LANES), F32),
                        pltpu.VMEM((MOE_NBUF, MOE_TM * TOK_TILE, LANES), F32),
                        pltpu.SemaphoreType.DMA((MOE_NBUF,)), pltpu.SemaphoreType.DMA((MOE_NBUF,))])
    return pl.pallas_call(
        _moe_group_kernel,
        grid_spec=grid_spec,
        out_shape=jax.ShapeDtypeStruct(((MOE_SLOTS + MOE_DUMP) * TOK_TILE, LANES), F32),
        compiler_params=_cparams(("arbitrary",)),
        name="moe_group",
    )(tile_expert, n_used, src_tok, dst_row, h2, w1, w3, w2)


def _moe_plan(expert_ids):
    e_flat = expert_ids.T.reshape(-1)
    order = jnp.argsort(e_flat, stable=True).astype(jnp.int32)
    counts = jnp.sum((e_flat[:, None] == jnp.arange(N_EXPERTS)[None, :]).astype(jnp.int32), axis=0)
    padded = (counts + MOE_TM - 1) // MOE_TM * MOE_TM
    pend = jnp.cumsum(padded)
    pstart = pend - padded
    ustart = jnp.cumsum(counts) - counts
    n_used = pend[-1] // MOE_TM
    tiles = jnp.arange(MOE_TILES, dtype=jnp.int32)
    last_used = jnp.minimum(tiles, n_used - 1)
    tile_expert = jnp.sum((last_used[:, None] * MOE_TM >= pend[None, :]).astype(jnp.int32), axis=1)
    t = jnp.arange(-MOE_LEAD, MOE_PLAN_TILES - MOE_LEAD, dtype=jnp.int32)[:, None]
    r = jnp.arange(MOE_TM, dtype=jnp.int32)[None, :]
    e_t = tile_expert[jnp.clip(t, 0, MOE_TILES - 1)]
    off = t * MOE_TM + r - pstart[e_t]
    valid = (t >= 0) & (t < n_used) & (off < counts[e_t])
    slot = order[jnp.clip(ustart[e_t] + off, 0, MOE_SLOTS - 1)]
    src_tok = jnp.where(valid, slot % N_TOK, 0)
    dst_row = jnp.where(valid, slot, MOE_SLOTS + (t % MOE_NBUF) * MOE_TM + r)
    return (tile_expert.astype(jnp.int32), n_used.reshape(1).astype(jnp.int32),
            src_tok.reshape(-1).astype(jnp.int32), dst_row.reshape(-1).astype(jnp.int32))


def _moe_combine_kernel(x1, y0, y1, route, mod_ref, out_c, out_l):
    gate2 = mod_ref[0][:, 5 * D_MODEL:6 * D_MODEL]
    r = route[...]
    val = x1[...] + gate2 * (r[:, 0:1] * _load_token_tiles(y0, FF_TM) + r[:, 1:2] * _load_token_tiles(y1, FF_TM))
    is_ctx = pl.program_id(0) < SEG // FF_TM

    @pl.when(is_ctx)
    def _():
        out_c[...] = val

    @pl.when(jnp.logical_not(is_ctx))
    def _():
        out_l[...] = val


def _moe_combine(x1, y_slots, route, mod_l):
    nt = N_TOK // FF_TM
    per_seg = SEG // FF_TM
    return pl.pallas_call(
        _moe_combine_kernel,
        grid=(nt,),
        in_specs=[pl.BlockSpec((FF_TM, D_MODEL), lambda i: (i, 0)),
                  pl.BlockSpec((FF_TM * TOK_TILE, LANES), lambda i: (i, 0)),
                  pl.BlockSpec((FF_TM * TOK_TILE, LANES), lambda i: (nt + i, 0)),
                  pl.BlockSpec((FF_TM, LANES), lambda i: (i, 0)),
                  pl.BlockSpec((1, 1, 6 * D_MODEL), lambda i: (i // per_seg, 0, 0))],
        out_specs=[pl.BlockSpec((FF_TM, D_MODEL), lambda i: (jnp.minimum(i, per_seg - 1), 0)),
                   pl.BlockSpec((FF_TM, D_MODEL), lambda i: (jnp.maximum(i - per_seg, 0), 0))],
        out_shape=[jax.ShapeDtypeStruct((SEG, D_MODEL), F32), jax.ShapeDtypeStruct((N_TOK - SEG, D_MODEL), F32)],
        compiler_params=_cparams(("arbitrary",)),
        name="moe_combine",
    )(x1, y_slots, y_slots, route, mod_l)


def _rope_tables():
    t = np.arange(DEC_SEQ)
    row, col = (t // GRID_W).astype(np.float32), (t % GRID_W).astype(np.float32)
    nf = HEAD_DIM // 4
    freqs = np.float32(ROPE_BASE) ** (-np.arange(nf, dtype=np.float32) / np.float32(nf))
    lane = np.arange(LANES) % HEAD_DIM
    fidx = lane % nf
    use_col = (lane // (HEAD_DIM // 2)) == 1
    first = (lane % (HEAD_DIM // 2)) < nf
    pos = np.where(use_col[None, :], col[:, None], row[:, None])
    ang = (pos * freqs[fidx][None, :]).astype(np.float32).astype(np.float64)
    sin = np.sin(ang)
    return (jnp.asarray(np.cos(ang), dtype=F32), jnp.asarray(np.where(first[None, :], -sin, sin), dtype=F32))


def _permute_w_in(w):
    sizes = (256, 256, 256, 256, 8, 8, 256, 128, 128, 256, 256, 256, 256, 256)
    offs = np.concatenate([[0], np.cumsum(sizes)])
    part = lambda i: w[:, offs[i]:offs[i + 1]]
    mq, mk, mv, mo, mi, mf, sq, sk, sv, rx, ry, nq, nk, nv = (part(i) for i in range(14))
    pad = jnp.zeros((w.shape[0], LANES - 16), w.dtype)
    return jnp.concatenate([mq, mk, mv, mo, sq, rx, ry, nq, nk, nv, sk, sv, mi, mf, pad], axis=1)


def _block_diag(w):
    eye = jnp.eye(RG_BLOCKS, dtype=w.dtype)
    return (w[:, :, None, :] * eye[:, None, :, None]).reshape(RG_WIDTH, RG_WIDTH)


def _tile2(g):
    return jnp.concatenate([g, g]).reshape(1, LANES)


def kernel(x_prompt, x_sample, cache_swa_k, cache_swa_v, cache_na_k, cache_na_v, state_mlstm_C, state_mlstm_n, state_mlstm_m, state_rglru_h, c, c_ctx, norm1_g, norm2_g, w_ada, b_ada, w_in, ml_b_i, ml_b_f, ml_hn, sw_qn, sw_kn, sw_sink, rg_conv_w, rg_conv_b, rg_w_r, rg_b_r, rg_w_i, rg_b_i, rg_lam, na_qn, na_kn, na_rpb, w_br, w_mg, b_mg, w_out, ffn_w1, ffn_w3, ffn_w2, moe_wr, moe_br, moe_w1, moe_w3, moe_w2):
    assert DEPTH % 2 == 0
    x_all = (x_prompt.reshape(SEG, D_MODEL), x_sample.reshape(N_TOK - SEG, D_MODEL))
    cvecs = jnp.concatenate([c_ctx[None, :], c, jnp.zeros((8 - 1 - DEC_BATCH, D_MODEL), F32)], axis=0)
    mod = _mod_table(cvecs.T, w_ada, b_ada)
    cos_t, sin_t = _rope_tables()
    nj = 2 * ML_HEADS
    zeros_state = (jnp.zeros((BATCH, nj // 2, LANES, LANES), F32), jnp.zeros((BATCH, nj // 2, LANES, LANES), F32),
                   jnp.zeros((BATCH, nj, LANES), F32), jnp.zeros((BATCH // RG_NSEG, 2, RG_NSEG, RG_WIDTH), F32))
    ctx_out = []
    for l in range(DEPTH):
        mod_l = mod[l].reshape(8, 1, 6 * D_MODEL)
        qk_gains = jnp.stack([_tile2(sw_qn[l])[0], _tile2(sw_kn[l])[0], _tile2(na_qn[l])[0], _tile2(na_kn[l])[0]])
        proj = _inproj(x_all, mod_l, norm1_g[l].reshape(1, D_MODEL), _permute_w_in(w_in[l]).astype(BF16),
                       qk_gains, cos_t, sin_t)
        gate_bias = jnp.concatenate([ml_b_i[l].reshape(-1), ml_b_f[l].reshape(-1),
                                     jnp.zeros((LANES - 2 * nj,), F32)]).reshape(1, LANES)
        hf_c, hb_c, *st_new = _mlstm(proj, gate_bias, *zeros_state[:3], bsz=BATCH, seq=SEQ, row0=0)
        c_new, n_new, m_new = _mlstm_unpack_state(*st_new)
        st_lat = _mlstm_pack_state(state_mlstm_C[:, l].reshape(DEC_BATCH, nj, HEAD_DIM, HEAD_DIM),
                                   state_mlstm_n[:, l].reshape(DEC_BATCH, nj, HEAD_DIM),
                                   state_mlstm_m[:, l].reshape(DEC_BATCH, nj))
        hf_l, hb_l, _, _, _ = _mlstm(proj, gate_bias, *st_lat, bsz=DEC_BATCH, seq=DEC_SEQ, row0=SEG)
        wg = jnp.concatenate([_block_diag(rg_w_r[l, 0]), _block_diag(rg_w_i[l, 0]),
                              _block_diag(rg_w_r[l, 1]), _block_diag(rg_w_i[l, 1])], axis=1).astype(BF16)
        bg = jnp.concatenate([rg_b_r[l, 0], rg_b_i[l, 0], rg_b_r[l, 1], rg_b_i[l, 1]]).reshape(1, 4 * RG_WIDTH)
        rg_args = (rg_conv_w[l], rg_conv_b[l].reshape(1, RG_WIDTH), wg, bg, rg_lam[l])
        oc_c, hl_c = _rglru(proj, *rg_args, zeros_state[3], nblk=BATCH // RG_NSEG, seg_len=SEQ, chained=False, row0=0)
        hl_new = jnp.transpose(hl_c, (0, 2, 1, 3)).reshape(BATCH, 2, RG_WIDTH)
        h0_lat = jnp.broadcast_to(state_rglru_h[:, l][:, :, None, :], (DEC_BATCH, 2, RG_NSEG, RG_WIDTH))
        oc_l, _ = _rglru(proj, *rg_args, h0_lat, nblk=DEC_BATCH, seg_len=DEC_SEQ // RG_NSEG, chained=True, row0=SEG)
        ob_c, od_c = _ctx_attn(proj, sw_sink[l])
        ob_l = _swa(proj, cache_swa_k[:, l].reshape(DEC_BATCH, PAST_LEN, 128),
                    cache_swa_v[:, l].reshape(DEC_BATCH, PAST_LEN, 128), sw_sink[l])
        od_l = _na(proj, cache_na_k[:, l].reshape(DEC_BATCH, PAST_LEN, 256),
                   cache_na_v[:, l].reshape(DEC_BATCH, PAST_LEN, 256), _na_bias(na_rpb[l].reshape(-1)))
        moe_layer = l % 2 == 1
        j = l // 2
        router = (moe_wr[j].T, moe_br[j]) if moe_layer else None
        outs = _merge(x_all, mod_l, norm1_g[l].reshape(1, D_MODEL), norm2_g[l].reshape(1, D_MODEL),
                      (hf_c, hf_l), (hb_c, hb_l), proj, (ob_c, ob_l), (oc_c, oc_l), (od_c, od_l),
                      _tile2(ml_hn[l]), w_mg[l].astype(BF16), b_mg[l].reshape(1, -1), w_br[l].astype(BF16),
                      w_out[l].astype(BF16), router)
        if moe_layer:
            x1, h2, route = outs
            plan = _moe_plan(route[:, 2:4].astype(jnp.int32))
            y_slots = _moe_group(*plan, h2, moe_w1[j].astype(BF16), moe_w3[j].astype(BF16), moe_w2[j].astype(BF16))
            x_all = tuple(_moe_combine(x1, y_slots, route, mod_l))
        else:
            x1, h2 = outs
            x_all = _ffn(h2, x1, mod_l, ffn_w1[j].astype(BF16), ffn_w3[j].astype(BF16), ffn_w2[j].astype(BF16))
        pc = proj[:SEG]
        ctx_out.append(dict(
            sw_k=pc[:, C_SK:C_SK + 128].reshape(BATCH, SEQ, SW_KV_HEADS, HEAD_DIM),
            sw_v=pc[:, C_SV:C_SV + 128].reshape(BATCH, SEQ, SW_KV_HEADS, HEAD_DIM),
            na_k=pc[:, C_NK:C_NK + 256].reshape(BATCH, SEQ, NA_HEADS, HEAD_DIM),
            na_v=pc[:, C_NV:C_NV + 256].reshape(BATCH, SEQ, NA_HEADS, HEAD_DIM),
            ml_C=c_new.reshape(BATCH, 2, ML_HEADS, HEAD_DIM, HEAD_DIM),
            ml_n=n_new.reshape(BATCH, 2, ML_HEADS, HEAD_DIM),
            ml_m=m_new.reshape(BATCH, 2, ML_HEADS),
            rg_h=hl_new))
    stack = lambda name: jnp.stack([t[name] for t in ctx_out], axis=1)
    return (x_all[0].reshape(BATCH, SEQ, D_MODEL), x_all[1].reshape(DEC_BATCH, DEC_SEQ, D_MODEL),
            stack('sw_k'), stack('sw_v'), stack('na_k'), stack('na_v'),
            stack('ml_C'), stack('ml_n'), stack('ml_m'), stack('rg_h'))
```

```python
import functools

import numpy as np
import jax
import jax.numpy as jnp
from jax import lax
from jax.experimental import pallas as pl
from jax.experimental.pallas import tpu as pltpu

F32 = jnp.float32
BF16 = jnp.bfloat16

D_MODEL = 1024
BATCH = 16
SEQ = 256
DEPTH = 2
DEC_BATCH = 2
DEC_SEQ = 4096
PAST_LEN = 256
GRID_W = 64
HEAD_DIM = 64
ML_HEADS = 4
ML_CHUNK = 128
ML_SUB = 2
SW_HEADS = 4
SW_KV_HEADS = 2
SW_WINDOW = 128
RG_WIDTH = 256
RG_BLOCKS = 4
RG_CONV = 4
RG_C = 8.0
NA_HEADS = 4
NA_ROWS = 8
NA_COLS = 16
N_BRANCH = 4
ROPE_BASE = 10000.0
D_FF = 2816
N_EXPERTS = 8
D_FF_EXPERT = 2048
EPS = 1e-6
NEG = -1e30
SCALE = HEAD_DIM ** -0.5

SEG = 4096
N_SEG = 3
N_TOK = N_SEG * SEG
LANES = 128
VMEM_LIMIT = 56 * 1024 * 1024

C_MQ, C_MK, C_MV, C_MO = 0, 256, 512, 768
C_SQ, C_RX, C_RY, C_NQ, C_NK, C_NV = 1024, 1280, 1536, 1792, 2048, 2304
C_SK, C_SV, C_G = 2560, 2688, 2816
P_W = 2944


def _cparams(sem):
    return pltpu.CompilerParams(dimension_semantics=sem, vmem_limit_bytes=VMEM_LIMIT)


def _dot(a, b):
    return jnp.dot(a.astype(BF16), b.astype(BF16), preferred_element_type=F32)


def _dot_nt(a, b):
    return lax.dot_general(a.astype(BF16), b.astype(BF16), (((1,), (1,)), ((), ())),
                           preferred_element_type=F32)


def _dot_tn(a, b):
    return lax.dot_general(a.astype(BF16), b.astype(BF16), (((0,), (0,)), ((), ())),
                           preferred_element_type=F32)


def _split3(x):
    hi = x.astype(BF16)
    r1 = x - hi.astype(F32)
    mid = r1.astype(BF16)
    lo = (r1 - mid.astype(F32)).astype(BF16)
    return hi, mid, lo


def _dot_exact_rhs(a01, x):
    hi, mid, lo = _split3(x)
    d = lambda p: jnp.dot(a01, p, preferred_element_type=F32)
    return d(hi) + d(mid) + d(lo)


def _dot_exact_lhs(x, a01):
    hi, mid, lo = _split3(x)
    d = lambda p: jnp.dot(p, a01, preferred_element_type=F32)
    return d(hi) + d(mid) + d(lo)


def _sigmoid(x):
    return 0.5 * jnp.tanh(0.5 * x) + 0.5


def _rms(x, g):
    return x * lax.rsqrt(jnp.mean(x * x, axis=-1, keepdims=True) + EPS) * g


def _rms_head_pairs(x, g):
    lane = lax.broadcasted_iota(jnp.int32, x.shape, 1)
    left = lane < HEAD_DIM
    sq = x * x
    s0 = jnp.sum(jnp.where(left, sq, 0.0), axis=-1, keepdims=True)
    s1 = jnp.sum(jnp.where(left, 0.0, sq), axis=-1, keepdims=True)
    ms = jnp.where(left, s0, s1) * (1.0 / HEAD_DIM)
    return x * lax.rsqrt(ms + EPS) * g


MOD_TN = 1536
MOD_ROWS = 3


def _mod_kernel(ct_ref, w_ref, b_ref, o_ref):
    ct = ct_ref[...]
    st = ct * jax.nn.sigmoid(ct)
    w = w_ref[0]
    o_ref[...] = jnp.zeros(o_ref.shape, F32)
    for r in range(MOD_ROWS):
        o_ref[0, r:r + 1, :] = jnp.sum(w * st[:, r:r + 1], axis=0, keepdims=True) + b_ref[0]


def _mod_table(cvecs_t, w_ada, b_ada):
    n = 6 * D_MODEL
    return pl.pallas_call(
        _mod_kernel,
        grid=(DEPTH, n // MOD_TN),
        in_specs=[pl.BlockSpec((D_MODEL, 8), lambda l, j: (0, 0)),
                  pl.BlockSpec((1, D_MODEL, MOD_TN), lambda l, j: (l, 0, j)),
                  pl.BlockSpec((1, 1, MOD_TN), lambda l, j: (l, 0, j))],
        out_specs=pl.BlockSpec((1, 8, MOD_TN), lambda l, j: (l, 0, j)),
        out_shape=jax.ShapeDtypeStruct((DEPTH, 8, n), F32),
        compiler_params=_cparams(("arbitrary", "arbitrary")),
        name="adaln_mod",
    )(cvecs_t, w_ada, b_ada.reshape(DEPTH, 1, n))


IN_TM = 512


def _swap16(y):
    lane = lax.broadcasted_iota(jnp.int32, y.shape, 1)
    first = (lane % 32) < 16
    return jnp.where(first, pltpu.roll(y, LANES - 16, 1), pltpu.roll(y, 16, 1))


def _seg_pair_specs(tm, width, lat_row0=0):
    nt = SEG // tm
    lat_off = lat_row0 // tm
    return (pl.BlockSpec((tm, width), lambda s, i: (jnp.minimum(s * nt + i, nt - 1), 0)),
            pl.BlockSpec((tm, width), lambda s, i: (lat_off + jnp.maximum(s * nt + i - nt, 0), 0)))


TOK_TILE = D_MODEL // LANES


def _store_token_tiles(ref, x):
    for s in range(TOK_TILE):
        ref[pl.ds(s, x.shape[0], stride=TOK_TILE), :] = x[:, LANES * s:LANES * (s + 1)]


def _load_token_tiles(ref, n_tok):
    return jnp.concatenate([ref[pl.ds(s, n_tok, stride=TOK_TILE), :] for s in range(TOK_TILE)], axis=1)


def _pick(c_ref, l_ref):
    return jnp.where(pl.program_id(0) == 0, c_ref[...], l_ref[...])


def _x_pair(x):
    return (x, 0) if isinstance(x, tuple) else ((x, x), SEG)


def _inproj_kernel(xc_ref, xl_ref, mod_ref, g_ref, w_ref, qkg_ref, cos_ref, sin_ref, o_ref):
    seg = pl.program_id(0)
    mod = mod_ref[0]
    sh1 = mod[:, 0:D_MODEL]
    sc1 = mod[:, D_MODEL:2 * D_MODEL]
    h = _rms(_pick(xc_ref, xl_ref), g_ref[...]) * (1.0 + sc1) + sh1
    r = jnp.dot(h.astype(BF16), w_ref[...], preferred_element_type=F32)
    o_ref[:, 0:C_SQ] = r[:, 0:C_SQ]
    o_ref[:, C_RX:C_NQ] = r[:, C_RX:C_NQ]
    o_ref[:, C_NV:C_SK] = r[:, C_NV:C_SK]
    o_ref[:, C_SV:P_W] = r[:, C_SV:P_W]
    cos = cos_ref[...]
    sin = sin_ref[...]
    latent = seg > 0

    def rope(y):
        return jnp.where(latent, y * cos + _swap16(y) * sin, y)

    for p in range(2):
        a = C_SQ + LANES * p
        o_ref[:, a:a + LANES] = rope(_rms_head_pairs(r[:, a:a + LANES], qkg_ref[0:1, :]))
    o_ref[:, C_SK:C_SK + LANES] = rope(_rms_head_pairs(r[:, C_SK:C_SK + LANES], qkg_ref[1:2, :]))
    for p in range(2):
        a = C_NQ + LANES * p
        o_ref[:, a:a + LANES] = _rms_head_pairs(r[:, a:a + LANES], qkg_ref[2:3, :])
        a = C_NK + LANES * p
        o_ref[:, a:a + LANES] = _rms_head_pairs(r[:, a:a + LANES], qkg_ref[3:4, :])


def _inproj(x, mod_l, norm1, w_in_p, qk_gains, cos_t, sin_t):
    nt = SEG // IN_TM
    x_pair, lat_row0 = _x_pair(x)
    return pl.pallas_call(
        _inproj_kernel,
        grid=(N_SEG, nt),
        in_specs=[*_seg_pair_specs(IN_TM, D_MODEL, lat_row0),
                  pl.BlockSpec((1, 1, 6 * D_MODEL), lambda s, i: (s, 0, 0)),
                  pl.BlockSpec((1, D_MODEL), lambda s, i: (0, 0)),
                  pl.BlockSpec((D_MODEL, P_W), lambda s, i: (0, 0)),
                  pl.BlockSpec((4, LANES), lambda s, i: (0, 0)),
                  pl.BlockSpec((IN_TM, LANES), lambda s, i: (i, 0)),
                  pl.BlockSpec((IN_TM, LANES), lambda s, i: (i, 0))],
        out_specs=pl.BlockSpec((IN_TM, P_W), lambda s, i: (s * nt + i, 0)),
        out_shape=jax.ShapeDtypeStruct((N_TOK, P_W), F32),
        compiler_params=_cparams(("arbitrary", "arbitrary")),
        name="inproj",
    )(*x_pair, mod_l, norm1, w_in_p, qk_gains, cos_t, sin_t)


def _mlstm_direction(d, rows, q_ref, k_ref, v_ref, g_ref, bias, tri_ref, sel_ref, state):
    ch = ML_CHUNK
    r_io = lax.broadcasted_iota(jnp.int32, (ch, ch), 0)
    c_io = lax.broadcasted_iota(jnp.int32, (ch, ch), 1)
    lower = r_io >= c_io
    upper = r_io <= c_io
    mask = lower if d == 0 else upper
    tri = tri_ref[d]
    tri_t = tri_ref[1 - d]
    left = c_io < HEAD_DIM
    top = r_io < HEAD_DIM
    blockdiag = top == left
    cbs_in, nbs_in, m_old = state
    g = g_ref[rows, :] + bias[...]
    b_cols = _dot_exact_rhs(tri, jax.nn.log_sigmoid(g))
    b3 = jnp.concatenate(_split3(b_cols), axis=1)
    gt = g.T
    li_rows = gt[0:8, :]
    b_rows = _dot_exact_lhs(jax.nn.log_sigmoid(gt[8:16, :]), tri_t)
    a_rows = li_rows - b_rows
    bl = b_rows[:, ch - 1:ch] if d == 0 else b_rows[:, 0:1]
    g_rows = bl - b_rows + li_rows
    m_new = jnp.maximum(bl + m_old, jnp.max(g_rows, axis=1, keepdims=True))
    wk_rows = jnp.exp(g_rows - m_new)
    wp = jnp.exp(bl + m_old - m_new)
    ones_blk = jnp.ones((ch, LANES), BF16)
    left2 = lax.broadcasted_iota(jnp.int32, (ch, 2 * LANES), 1) % LANES < HEAD_DIM
    h_out, c_out, n_out = [], [], []
    for p in range(ML_HEADS // 2):
        lanes = slice(LANES * p, LANES * (p + 1))
        j0 = ML_HEADS * d + 2 * p
        q2 = q_ref[rows, lanes]
        k2t = (k_ref[rows, lanes] * SCALE).T.astype(BF16)
        v2e = jnp.concatenate([v_ref[rows, lanes].astype(BF16), ones_blk], axis=1)
        cb = cbs_in[p]
        nb = nbs_in[p]
        q2b = q2.astype(BF16)
        q_lo = (q2 - q2b.astype(F32)).astype(BF16)
        nb_hi = nb.astype(BF16)
        nb_lo = (nb - nb_hi.astype(F32)).astype(BF16)
        qc = jnp.dot(q2b, cb.astype(BF16), preferred_element_type=F32)
        qn = jnp.dot(jnp.concatenate([q2b, q_lo, q2b], axis=1), jnp.concatenate([nb_hi, nb_hi, nb_lo], axis=0),
                     preferred_element_type=F32)
        b_pair = jnp.dot(b3, sel_ref[2 * d + p], preferred_element_type=F32)
        cbs, sves = [], []
        for i in range(2):
            j = j0 + i
            half = left if i == 0 else jnp.logical_not(left)
            a_mat = jnp.where(mask, a_rows[j:j + 1, :], NEG)
            cvec = jnp.maximum(m_old[j:j + 1, :], jnp.max(a_mat, axis=1, keepdims=True))
            cbro = jnp.broadcast_to(cvec, (ch, ch))
            s = jnp.dot(jnp.where(half, q2b, 0), k2t, preferred_element_type=F32) * jnp.exp(a_mat - cbro)
            s_hi = s.astype(BF16)
            s_lo = (s - s_hi.astype(F32)).astype(BF16)
            sve = jnp.dot(s_hi, v2e, preferred_element_type=F32)
            rs_lo = jnp.dot(s_lo, ones_blk, preferred_element_type=F32)
            sves.append(jnp.concatenate([sve[:, 0:LANES], sve[:, LANES:2 * LANES] + rs_lo], axis=1))
            cbs.append(cbro)
        c_pair = jnp.where(left, cbs[0], cbs[1])
        w_prev = jnp.exp(jnp.where(left, m_old[j0:j0 + 1, :], m_old[j0 + 1:j0 + 2, :]) - c_pair)
        sve = jnp.where(left2, sves[0], sves[1])
        num = w_prev * qc + sve[:, 0:LANES]
        den = w_prev * qn + sve[:, LANES:2 * LANES]
        h_out.append(num / jnp.maximum(jnp.abs(den), jnp.exp(-(c_pair + b_pair))))
        kwt = k2t * jnp.where(top, wk_rows[j0:j0 + 1, :], wk_rows[j0 + 1:j0 + 2, :])
        kwt_hi = kwt.astype(BF16)
        kwt_lo = (kwt - kwt_hi.astype(F32)).astype(BF16)
        kve = jnp.dot(kwt_hi, v2e, preferred_element_type=F32)
        kn = kve[:, LANES:2 * LANES] + jnp.dot(kwt_lo, ones_blk, preferred_element_type=F32)
        wp_pair = jnp.where(top, wp[j0:j0 + 1, :], wp[j0 + 1:j0 + 2, :])
        c_out.append(wp_pair * cb + jnp.where(blockdiag, kve[:, 0:LANES], 0.0))
        n_out.append(wp_pair * nb + jnp.where(blockdiag, kn, 0.0))
    return h_out, (c_out, n_out, m_new)


def _mlstm_kernel(qf, kf, vf, gf, qb, kb, vb, gb, c0, n0, m0, bias, tri_ref, sel_ref,
                  hf, hb, co, no, mo, cbd, nbd, m_s, *, nc):
    c = pl.program_id(1)

    @pl.when(c == 0)
    def _():
        cbd[...] = c0[0]
        nbd[...] = n0[0]
        m_s[...] = m0[0]

    npair = ML_HEADS // 2
    for d, (refs, h_ref) in enumerate((((qf, kf, vf, gf), hf), ((qb, kb, vb, gb), hb))):
        state = ([cbd[npair * d + p] for p in range(npair)], [nbd[npair * d + p] for p in range(npair)],
                 m_s[:, 0:1])
        subs = range(ML_SUB) if d == 0 else reversed(range(ML_SUB))
        for sub in subs:
            rows = slice(sub * ML_CHUNK, (sub + 1) * ML_CHUNK)
            h_out, state = _mlstm_direction(d, rows, *refs, bias, tri_ref, sel_ref, state)
            for p in range(npair):
                h_ref[rows, LANES * p:LANES * (p + 1)] = h_out[p]
        c_out, n_out, m_new = state
        for p in range(npair):
            cbd[npair * d + p] = c_out[p]
            nbd[npair * d + p] = n_out[p]
        heads = slice(ML_HEADS * d, ML_HEADS * (d + 1))
        m_s[heads, :] = jnp.broadcast_to(m_new[heads, :], (ML_HEADS, LANES))

    @pl.when(c == nc - 1)
    def _():
        co[0] = cbd[...]
        no[0] = nbd[...]
        mo[0] = m_s[...]


def _mlstm_pack_state(c0, n0, m0):
    bsz = c0.shape[0]
    hd = HEAD_DIM
    eye = jnp.eye(2, dtype=F32)[None, None, :, None, :, None]
    cbd = c0.reshape(bsz, ML_HEADS, 2, hd, 1, hd) * eye
    nbd = jnp.broadcast_to(n0.reshape(bsz, ML_HEADS, 2, hd, 1, 1) * eye, cbd.shape)
    to_mat = lambda t: t.reshape(bsz, ML_HEADS, LANES, LANES)
    return to_mat(cbd), to_mat(nbd), jnp.broadcast_to(m0[..., None], m0.shape + (LANES,))


def _mlstm_unpack_state(cbd, nbd, mrow):
    hd = HEAD_DIM
    bsz = cbd.shape[0]
    c = jnp.stack([cbd[:, :, :hd, :hd], cbd[:, :, hd:, hd:]], axis=2).reshape(bsz, 2 * ML_HEADS, hd, hd)
    n = jnp.stack([nbd[:, :, :hd, 0], nbd[:, :, hd:, hd]], axis=2).reshape(bsz, 2 * ML_HEADS, hd)
    return c, n, mrow[:, :, 0]


def _mlstm_tables():
    r = np.arange(ML_CHUNK)
    lower = (r[:, None] >= r[None, :]).astype(np.float32)
    row = np.arange(3 * LANES)[:, None] % LANES
    lane_left = np.arange(LANES)[None, :] < HEAD_DIM
    sel = [row == np.where(lane_left, 8 + ML_HEADS * d + 2 * p, 9 + ML_HEADS * d + 2 * p)
           for d in range(2) for p in range(ML_HEADS // 2)]
    return (jnp.asarray(np.stack([lower, lower.T]), dtype=BF16),
            jnp.asarray(np.stack(sel).astype(np.float32), dtype=BF16))


def _mlstm(proj, gate_bias, c0, n0, m0, *, bsz, seq, row0):
    blk = ML_SUB * ML_CHUNK
    nc = seq // blk
    tri, sel = _mlstm_tables()
    base = row0 // blk
    nj = 2 * ML_HEADS
    fw = lambda col: (lambda b, c: (base + b * nc + c, col))
    bw = lambda col: (lambda b, c: (base + b * nc + nc - 1 - c, col))
    qkv = lambda f: [pl.BlockSpec((blk, 256), f(C_MQ // 256)),
                     pl.BlockSpec((blk, 256), f(C_MK // 256)),
                     pl.BlockSpec((blk, 256), f(C_MV // 256)),
                     pl.BlockSpec((blk, LANES), f(C_G // LANES))]
    st_specs = [pl.BlockSpec((1, nj // 2, LANES, LANES), lambda b, c: (b, 0, 0, 0)),
                pl.BlockSpec((1, nj // 2, LANES, LANES), lambda b, c: (b, 0, 0, 0)),
                pl.BlockSpec((1, nj, LANES), lambda b, c: (b, 0, 0))]
    st_shapes = [jax.ShapeDtypeStruct((bsz, nj // 2, LANES, LANES), F32),
                 jax.ShapeDtypeStruct((bsz, nj // 2, LANES, LANES), F32),
                 jax.ShapeDtypeStruct((bsz, nj, LANES), F32)]
    return pl.pallas_call(
        functools.partial(_mlstm_kernel, nc=nc),
        grid=(bsz, nc),
        in_specs=qkv(fw) + qkv(bw) + st_specs + [pl.BlockSpec((1, LANES), lambda b, c: (0, 0)),
                                                 pl.BlockSpec(tri.shape, lambda b, c: (0, 0, 0)),
                                                 pl.BlockSpec(sel.shape, lambda b, c: (0, 0, 0))],
        out_specs=[pl.BlockSpec((blk, 256), lambda b, c: (b * nc + c, 0)),
                   pl.BlockSpec((blk, 256), lambda b, c: (b * nc + nc - 1 - c, 0))] + st_specs,
        out_shape=[jax.ShapeDtypeStruct((bsz * seq, 256), F32),
                   jax.ShapeDtypeStruct((bsz * seq, 256), F32)] + st_shapes,
        scratch_shapes=[pltpu.VMEM((nj // 2, LANES, LANES), F32),
                        pltpu.VMEM((nj // 2, LANES, LANES), F32),
                        pltpu.VMEM((nj, LANES), F32)],
        compiler_params=_cparams(("arbitrary", "arbitrary")),
        name="mlstm",
    )(proj, proj, proj, proj, proj, proj, proj, proj, c0, n0, m0, gate_bias, tri, sel)


RG_TC = 256
RG_PAD = 8
RG_NSEG = 8
RG_SKEW = 4


def _rglru_kernel(rx, ry, cw, cb, wg, bg, lam, h0, oc, hl, xpad, af, ab, uf, ub, *, seg_len, chained):
    rows = RG_NSEG * seg_len
    seq_len = rows if chained else seg_len
    halves = RG_WIDTH // LANES
    pitch = seg_len + RG_SKEW
    buf_row = lambda t: (t // seg_len) * pitch + t % seg_len
    xpad[0:RG_PAD, :] = jnp.zeros((RG_PAD, RG_WIDTH), F32)
    xpad[rows + RG_PAD:rows + 2 * RG_PAD, :] = jnp.zeros((RG_PAD, RG_WIDTH), F32)
    xpad[RG_PAD:rows + RG_PAD, :] = rx[...]
    sp = jax.nn.softplus(-lam[...])
    left = (RG_CONV - 1) // 2
    for ci in range(rows // RG_TC):
        s0 = ci * RG_TC
        pos = (s0 + lax.broadcasted_iota(jnp.int32, (RG_TC, RG_WIDTH), 0)) % seq_len
        xc = None
        for j in range(RG_CONV):
            a = RG_PAD + s0 + j - left
            term = xpad[a:a + RG_TC, :] * cw[j:j + 1, :]
            if not chained and j != left:
                term = jnp.where((pos + (j - left) >= 0) & (pos + (j - left) < seq_len), term, 0.0)
            xc = term if xc is None else xc + term
        xc = xc + cb[...]
        pre = _dot(xc, wg[...]) + bg[...]
        for d, (a_ref, u_ref) in enumerate(((af, uf), (ab, ub))):
            o = 2 * RG_WIDTH * d
            r = _sigmoid(pre[:, o:o + RG_WIDTH])
            gi = _sigmoid(pre[:, o + RG_WIDTH:o + 2 * RG_WIDTH])
            log_a = -RG_C * r * sp[d:d + 1, :]
            a_val = jnp.exp(log_a)
            u_val = jnp.sqrt(-jnp.tanh(log_a) * (a_val * a_val + 1.0)) * (gi * xc)
            for hv in range(halves):
                dst = slice(buf_row(s0), buf_row(s0) + RG_TC)
                a_ref[hv, dst, :] = a_val[:, LANES * hv:LANES * (hv + 1)]
                u_ref[hv, dst, :] = u_val[:, LANES * hv:LANES * (hv + 1)]

    def body(s, carry):
        out = []
        for d, (a_ref, u_ref) in enumerate(((af, uf), (ab, ub))):
            step_rows = pl.ds(s if d == 0 else seg_len - 1 - s, RG_NSEG, stride=pitch)
            for hv in range(halves):
                h_loc, prod = carry[2 * (halves * d + hv)], carry[2 * (halves * d + hv) + 1]
                a = a_ref[hv, step_rows, :]
                h_loc = a * h_loc + u_ref[hv, step_rows, :]
                prod = a * prod
                u_ref[hv, step_rows, :] = h_loc
                a_ref[hv, step_rows, :] = prod
                out += [h_loc, prod]
        return tuple(out)

    zero = jnp.zeros((RG_NSEG, LANES), F32)
    one = jnp.ones((RG_NSEG, LANES), F32)
    ends = lax.fori_loop(0, seg_len, body, (zero, one) * (2 * halves), unroll=8)
    seg = lax.broadcasted_iota(jnp.int32, (RG_NSEG, LANES), 0)
    for d, (a_ref, u_ref) in enumerate(((af, uf), (ab, ub))):
        for hv in range(halves):
            lanes = slice(LANES * hv, LANES * (hv + 1))
            h_end, p_end = ends[2 * (halves * d + hv)], ends[2 * (halves * d + hv) + 1]
            h_in = h0[0, d][:, lanes]
            if chained:
                c = h_in[0:1, :]
                h_in = zero
                for k in (range(RG_NSEG) if d == 0 else reversed(range(RG_NSEG))):
                    h_in = jnp.where(seg == k, c, h_in)
                    c = h_end[k:k + 1, :] + p_end[k:k + 1, :] * c
            hl[0, d, :, lanes] = h_end + p_end * h_in
            for k in range(RG_NSEG):
                for ci in range(seg_len // RG_TC):
                    sl = slice(k * pitch + ci * RG_TC, k * pitch + (ci + 1) * RG_TC)
                    u_ref[hv, sl, :] = u_ref[hv, sl, :] + a_ref[hv, sl, :] * h_in[k:k + 1, :]
    for ci in range(rows // RG_TC):
        sl = slice(ci * RG_TC, (ci + 1) * RG_TC)
        for hv in range(halves):
            lanes = slice(LANES * hv, LANES * (hv + 1))
            src = slice(buf_row(ci * RG_TC), buf_row(ci * RG_TC) + RG_TC)
            oc[sl, lanes] = (uf[hv, src, :] + ub[hv, src, :]) * jax.nn.gelu(ry[sl, lanes])


def _rglru(proj, cw, cb, wg, bg, lam, h0, *, nblk, seg_len, chained, row0):
    rows = RG_NSEG * seg_len
    base = row0 // rows
    full = lambda shape: pl.BlockSpec(shape, lambda b: tuple(0 for _ in shape))
    st_spec = pl.BlockSpec((1, 2, RG_NSEG, RG_WIDTH), lambda b: (b, 0, 0, 0))
    return pl.pallas_call(
        functools.partial(_rglru_kernel, seg_len=seg_len, chained=chained),
        grid=(nblk,),
        in_specs=[pl.BlockSpec((rows, RG_WIDTH), lambda b: (base + b, C_RX // RG_WIDTH)),
                  pl.BlockSpec((rows, RG_WIDTH), lambda b: (base + b, C_RY // RG_WIDTH)),
                  full((RG_CONV, RG_WIDTH)), full((1, RG_WIDTH)),
                  full((RG_WIDTH, 4 * RG_WIDTH)), full((1, 4 * RG_WIDTH)), full((2, RG_WIDTH)),
                  st_spec],
        out_specs=[pl.BlockSpec((rows, RG_WIDTH), lambda b: (b, 0)), st_spec],
        out_shape=[jax.ShapeDtypeStruct((nblk * rows, RG_WIDTH), F32),
                   jax.ShapeDtypeStruct((nblk, 2, RG_NSEG, RG_WIDTH), F32)],
        scratch_shapes=[pltpu.VMEM((rows + 2 * RG_PAD, RG_WIDTH), F32)]
        + [pltpu.VMEM((RG_WIDTH // LANES, RG_NSEG * (seg_len + RG_SKEW), LANES), F32) for _ in range(4)],
        compiler_params=_cparams(("arbitrary",)),
        name="rglru",
    )(proj, proj, cw, cb, wg, bg, lam, h0)


def _softmax_pv(scores, values, sink):
    m = functools.reduce(jnp.maximum, [jnp.max(s, axis=-1, keepdims=True) for s in scores])
    if sink is not None:
        m = jnp.maximum(m, sink)
    ps = [jnp.exp(s - m) for s in scores]
    den = functools.reduce(jnp.add, [jnp.sum(p, axis=-1, keepdims=True) for p in ps])
    if sink is not None:
        den = den + jnp.exp(sink - m)
    num = functools.reduce(jnp.add, [_dot(p, v) for p, v in zip(ps, values)])
    return num / den


def _ctx_attn_kernel(sink, sq, sk, sv, nq, nk, nv, ob, od):
    assert SW_HEADS == 4 and SW_KV_HEADS == 2
    left = lax.broadcasted_iota(jnp.int32, (SEQ, LANES), 1) < HEAD_DIM
    first = lax.broadcasted_iota(jnp.int32, (2 * SEQ, 1), 0) < SEQ
    k2 = sk[...].astype(BF16)
    v2 = sv[...].astype(BF16)
    for p in range(SW_KV_HEADS):
        lanes = slice(LANES * p, LANES * (p + 1))
        q2 = sq[:, lanes]
        q_swapped = pltpu.roll(q2, HEAD_DIM, 1)
        kv_half = left if p == 0 else jnp.logical_not(left)
        qs = jnp.concatenate([jnp.where(kv_half, q2 if i == p else q_swapped, 0.0) for i in range(2)], axis=0)
        sink_col = jnp.where(first, sink[2 * p], sink[2 * p + 1])
        res = _softmax_pv([_dot_nt(qs, k2) * SCALE], [v2], sink_col)
        halves = [res[0:SEQ], res[SEQ:2 * SEQ]]
        placed = [halves[i] if i == p else pltpu.roll(halves[i], HEAD_DIM, 1) for i in range(2)]
        ob[:, lanes] = jnp.where(left, placed[0], placed[1])
    for p in range(NA_HEADS // 2):
        lanes = slice(LANES * p, LANES * (p + 1))
        q2 = nq[:, lanes]
        k2 = nk[:, lanes].astype(BF16)
        v2 = nv[:, lanes].astype(BF16)
        res = [_softmax_pv([_dot_nt(jnp.where(left if i == 0 else jnp.logical_not(left), q2, 0.0), k2) * SCALE],
                           [v2], None) for i in range(2)]
        od[:, lanes] = jnp.where(left, res[0], res[1])


def _ctx_attn(proj, sink):
    blk = lambda w, col: pl.BlockSpec((SEQ, w), lambda b: (b, col))
    return pl.pallas_call(
        _ctx_attn_kernel,
        grid=(BATCH,),
        in_specs=[pl.BlockSpec(memory_space=pltpu.SMEM),
                  blk(256, C_SQ // 256), blk(128, C_SK // 128), blk(128, C_SV // 128),
                  blk(256, C_NQ // 256), blk(256, C_NK // 256), blk(256, C_NV // 256)],
        out_specs=[pl.BlockSpec((SEQ, 256), lambda b: (b, 0)),
                   pl.BlockSpec((SEQ, 256), lambda b: (b, 0))],
        out_shape=[jax.ShapeDtypeStruct((SEG, 256), F32), jax.ShapeDtypeStruct((SEG, 256), F32)],
        compiler_params=_cparams(("arbitrary",)),
        name="ctx_attn",
    )(sink, proj, proj, proj, proj, proj, proj)


SW_QB = 128
SW_SPAN = SW_QB + 2 * SW_WINDOW


def _swa_kernel(sink, q, k, v, kc, vc, ob):
    assert SW_HEADS == 4 and SW_KV_HEADS == 2
    n = pl.program_id(1)
    ws = jnp.clip((n - 1) * SW_QB, 0, DEC_SEQ - SW_SPAN)
    ws = pl.multiple_of(ws, SW_QB)
    row = lax.broadcasted_iota(jnp.int32, (2 * SW_QB, SW_SPAN), 0)
    qpos = n * SW_QB + row % SW_QB
    kpos = ws + lax.broadcasted_iota(jnp.int32, (2 * SW_QB, SW_SPAN), 1)
    valid = jnp.abs(qpos - kpos) <= SW_WINDOW
    left = lax.broadcasted_iota(jnp.int32, (SW_QB, LANES), 1) < HEAD_DIM
    first = lax.broadcasted_iota(jnp.int32, (2 * SW_QB, 1), 0) < SW_QB
    k2 = k[pl.ds(ws, SW_SPAN), :].astype(BF16)
    v2 = v[pl.ds(ws, SW_SPAN), :].astype(BF16)
    kc2 = kc[0].astype(BF16)
    vc2 = vc[0].astype(BF16)
    for p in range(SW_KV_HEADS):
        lanes = slice(LANES * p, LANES * (p + 1))
        q2 = q[:, lanes]
        q_swapped = pltpu.roll(q2, HEAD_DIM, 1)
        kv_half = left if p == 0 else jnp.logical_not(left)
        qs = jnp.concatenate([jnp.where(kv_half, q2 if i == p else q_swapped, 0.0) for i in range(2)], axis=0)
        s_loc = jnp.where(valid, _dot_nt(qs, k2) * SCALE, NEG)
        s_ctx = _dot_nt(qs, kc2) * SCALE
        sink_col = jnp.where(first, sink[2 * p], sink[2 * p + 1])
        res = _softmax_pv([s_loc, s_ctx], [v2, vc2], sink_col)
        halves = [res[0:SW_QB], res[SW_QB:2 * SW_QB]]
        placed = [halves[i] if i == p else pltpu.roll(halves[i], HEAD_DIM, 1) for i in range(2)]
        ob[:, lanes] = jnp.where(left, placed[0], placed[1])


def _swa(proj, kc, vc, sink):
    nq = DEC_SEQ // SW_QB
    qbase = SEG // SW_QB
    return pl.pallas_call(
        _swa_kernel,
        grid=(DEC_BATCH, nq),
        in_specs=[pl.BlockSpec(memory_space=pltpu.SMEM),
                  pl.BlockSpec((SW_QB, 256), lambda b, n: (qbase + b * nq + n, C_SQ // 256)),
                  pl.BlockSpec((DEC_SEQ, 128), lambda b, n: (1 + b, C_SK // 128)),
                  pl.BlockSpec((DEC_SEQ, 128), lambda b, n: (1 + b, C_SV // 128)),
                  pl.BlockSpec((1, PAST_LEN, 128), lambda b, n: (b, 0, 0)),
                  pl.BlockSpec((1, PAST_LEN, 128), lambda b, n: (b, 0, 0))],
        out_specs=pl.BlockSpec((SW_QB, 256), lambda b, n: (b * nq + n, 0)),
        out_shape=jax.ShapeDtypeStruct((DEC_BATCH * DEC_SEQ, 256), F32),
        compiler_params=_cparams(("arbitrary", "arbitrary")),
        name="swa",
    )(sink, proj, proj, proj, kc, vc)


NA_RPB_R = 2 * NA_ROWS - 1
NA_RPB_C = 2 * NA_COLS - 1
GRID_ROWS = DEC_SEQ // GRID_W
NA_RB = 4
NA_UW = 12
NA_NQ = NA_RB * GRID_W
NA_NKEY = NA_UW * GRID_W
NA_NBLK = GRID_ROWS // NA_RB
NA_CASES = ((0, 0), (NA_RB, 0), (GRID_ROWS - NA_RB, GRID_ROWS - NA_UW))


def _na_row_start(qrow):
    return min(max(qrow - NA_ROWS // 2, 0), GRID_ROWS - NA_ROWS)


def _na_bias_kernel(rpb, out):
    h = pl.program_id(0)
    qc = lax.broadcasted_iota(jnp.int32, (GRID_W, GRID_W), 0)
    kc = lax.broadcasted_iota(jnp.int32, (GRID_W, GRID_W), 1)
    dc = jnp.clip(kc - qc, -(NA_COLS - 1), NA_COLS - 1) + NA_COLS - 1
    lo = jnp.clip(qc - NA_COLS // 2, 0, GRID_W - NA_COLS)
    valid = (kc >= lo) & (kc < lo + NA_COLS)
    tiles = []
    for dr in range(NA_RPB_R):
        t = jnp.zeros((GRID_W, GRID_W), F32)
        for j in range(NA_RPB_C):
            t = jnp.where(dc == j, rpb[(h * NA_RPB_R + dr) * NA_RPB_C + j], t)
        tiles.append(jnp.where(valid, t, NEG))
    outside = jnp.full((GRID_W, GRID_W), NEG, F32)
    for case, (q0, k0) in enumerate(NA_CASES):
        for a in range(NA_RB):
            r0 = _na_row_start(q0 + a)
            for i in range(NA_UW):
                inside = r0 <= k0 + i < r0 + NA_ROWS
                tile = tiles[k0 + i - (q0 + a) + NA_ROWS - 1] if inside else outside
                out[0, case, GRID_W * a:GRID_W * (a + 1), GRID_W * i:GRID_W * (i + 1)] = tile


def _na_bias(rpb_flat):
    shape = (NA_HEADS, len(NA_CASES), NA_NQ, NA_NKEY)
    return pl.pallas_call(
        _na_bias_kernel,
        grid=(NA_HEADS,),
        in_specs=[pl.BlockSpec(memory_space=pltpu.SMEM)],
        out_specs=pl.BlockSpec((1,) + shape[1:], lambda h: (h, 0, 0, 0)),
        out_shape=jax.ShapeDtypeStruct(shape, F32),
        compiler_params=_cparams(("arbitrary",)),
        name="na_bias",
    )(rpb_flat)


def _na_kernel(q, k, v, kc, vc, bias, od):
    blk = pl.program_id(1)
    case = jnp.where(blk == 0, 0, jnp.where(blk == NA_NBLK - 1, 2, 1))
    u0 = jnp.clip(blk * NA_RB - NA_ROWS // 2, 0, GRID_ROWS - NA_UW)
    k0 = pl.multiple_of(u0 * GRID_W, GRID_W)
    left = lax.broadcasted_iota(jnp.int32, (NA_NQ, LANES), 1) < HEAD_DIM
    for p in range(NA_HEADS // 2):
        lanes = slice(LANES * p, LANES * (p + 1))
        q2 = q[:, lanes]
        k2 = k[pl.ds(k0, NA_NKEY), lanes].astype(BF16)
        v2 = v[pl.ds(k0, NA_NKEY), lanes].astype(BF16)
        kc2 = kc[0, :, lanes].astype(BF16)
        vc2 = vc[0, :, lanes].astype(BF16)
        res = []
        for i in range(2):
            qm = jnp.where(left if i == 0 else jnp.logical_not(left), q2, 0.0)
            s_loc = _dot_nt(qm, k2) * SCALE + bias[2 * p + i, pl.ds(case, 1)][0]
            s_ctx = _dot_nt(qm, kc2) * SCALE
            res.append(_softmax_pv([s_loc, s_ctx], [v2, vc2], None))
        od[:, lanes] = jnp.where(left, res[0], res[1])


def _na(proj, kc, vc, bias):
    qbase = SEG // NA_NQ
    return pl.pallas_call(
        _na_kernel,
        grid=(DEC_BATCH, NA_NBLK),
        in_specs=[pl.BlockSpec((NA_NQ, 256), lambda b, r: (qbase + b * NA_NBLK + r, C_NQ // 256)),
                  pl.BlockSpec((DEC_SEQ, 256), lambda b, r: (1 + b, C_NK // 256)),
                  pl.BlockSpec((DEC_SEQ, 256), lambda b, r: (1 + b, C_NV // 256)),
                  pl.BlockSpec((1, PAST_LEN, 256), lambda b, r: (b, 0, 0)),
                  pl.BlockSpec((1, PAST_LEN, 256), lambda b, r: (b, 0, 0)),
                  pl.BlockSpec((NA_HEADS, len(NA_CASES), NA_NQ, NA_NKEY), lambda b, r: (0, 0, 0, 0))],
        out_specs=pl.BlockSpec((NA_NQ, 256), lambda b, r: (b * NA_NBLK + r, 0)),
        out_shape=jax.ShapeDtypeStruct((DEC_BATCH * DEC_SEQ, 256), F32),
        compiler_params=_cparams(("arbitrary", "arbitrary")),
        name="na",
    )(proj, proj, proj, kc, vc, bias)


MG_ROWS = 256
MG_SUB = 2
MG_TM = MG_SUB * MG_ROWS


def _merge_kernel(xc_ref, xl_ref, mod_ref, g1_ref, g2_ref, hf_c, hf_l, hb_c, hb_l, mo, ob_c, ob_l, oc_c, oc_l,
                  od_c, od_l, hn, wmg, bmg, wbr, wout, *rest, moe):
    if moe:
        wrt, br, x1_ref, h2_ref, route_ref = rest
    else:
        x1_ref, h2_ref = rest
    ctx = pl.program_id(0) == 0
    mod = mod_ref[0]
    chunk = lambda i: mod[:, i * D_MODEL:(i + 1) * D_MODEL]
    sh1, sc1, gate1, sh2, sc2 = chunk(0), chunk(1), chunk(2), chunk(3), chunk(4)

    def row_group(rows):
        pick = lambda c_ref, l_ref: jnp.where(ctx, c_ref[rows, :], l_ref[rows, :])
        x = pick(xc_ref, xl_ref)
        h = (_rms(x, g1_ref[...]) * (1.0 + sc1) + sh1).astype(BF16)
        hsum = pick(hf_c, hf_l) + pick(hb_c, hb_l)
        out_a = jnp.concatenate(
            [_rms_head_pairs(hsum[:, LANES * p:LANES * (p + 1)], hn[...]) for p in range(2)], axis=-1)
        out_a = out_a * jax.nn.sigmoid(mo[rows, :])
        acc = None
        for n, br_val in enumerate((out_a, pick(ob_c, ob_l), pick(oc_c, oc_l), pick(od_c, od_l))):
            gate = jax.nn.sigmoid(jnp.dot(h, wmg[:, n * D_MODEL:(n + 1) * D_MODEL], preferred_element_type=F32)
                                  + bmg[:, n * D_MODEL:(n + 1) * D_MODEL])
            term = gate * jnp.dot(br_val.astype(BF16), wbr[n], preferred_element_type=F32)
            acc = term if acc is None else acc + term
        y = jnp.dot(acc.astype(BF16), wout[...], preferred_element_type=F32)
        x1 = x + gate1 * y
        x1_ref[rows, :] = x1
        h2 = _rms(x1, g2_ref[...]) * (1.0 + sc2) + sh2
        if moe:
            _store_token_tiles(h2_ref.at[pl.ds(rows.start * TOK_TILE, MG_ROWS * TOK_TILE)], h2)
        else:
            h2_ref[rows, :] = h2.astype(BF16)
        if moe:
            logit = [jnp.sum(h2 * wrt[e:e + 1, :], axis=-1, keepdims=True) + br[e] for e in range(N_EXPERTS)]
            v1, i1 = logit[0], jnp.zeros(logit[0].shape, jnp.int32)
            for e in range(1, N_EXPERTS):
                better = logit[e] > v1
                v1 = jnp.where(better, logit[e], v1)
                i1 = jnp.where(better, e, i1)
            v2, i2 = jnp.full(v1.shape, -jnp.inf, F32), jnp.zeros(v1.shape, jnp.int32)
            for e in range(N_EXPERTS):
                better = (i1 != e) & (logit[e] > v2)
                v2 = jnp.where(better, logit[e], v2)
                i2 = jnp.where(better, e, i2)
            e2 = jnp.exp(v2 - v1)
            den = 1.0 + e2
            lane = lax.broadcasted_iota(jnp.int32, (MG_ROWS, LANES), 1)
            route = jnp.where(lane == 0, 1.0 / den, 0.0) + jnp.where(lane == 1, e2 / den, 0.0)
            route = route + jnp.where(lane == 2, i1.astype(F32), 0.0) + jnp.where(lane == 3, i2.astype(F32), 0.0)
            route_ref[rows, :] = route

    for sub in range(MG_SUB):
        row_group(slice(sub * MG_ROWS, (sub + 1) * MG_ROWS))


def _merge(x, mod_l, g1, g2, hf, hb, proj, ob, oc, od, hn, wmg, bmg, wbr, wout, router=None):
    nt = SEG // MG_TM
    moe = router is not None
    x_pair, lat_row0 = _x_pair(x)
    row = lambda w: pl.BlockSpec((MG_TM, w), lambda s, i: (s * nt + i, 0))
    ctx_blk, lat_blk = _seg_pair_specs(MG_TM, 256)
    full = lambda shape: pl.BlockSpec(shape, lambda s, i: tuple(0 for _ in shape), pipeline_mode=pl.Buffered(1))
    in_specs = [*_seg_pair_specs(MG_TM, D_MODEL, lat_row0),
                pl.BlockSpec((1, 1, 6 * D_MODEL), lambda s, i: (s, 0, 0)),
                full((1, D_MODEL)), full((1, D_MODEL)),
                ctx_blk, lat_blk, ctx_blk, lat_blk,
                pl.BlockSpec((MG_TM, 256), lambda s, i: (s * nt + i, C_MO // 256)),
                ctx_blk, lat_blk, ctx_blk, lat_blk, ctx_blk, lat_blk,
                full((1, LANES)), full((D_MODEL, N_BRANCH * D_MODEL)), full((1, N_BRANCH * D_MODEL)),
                full((N_BRANCH, 256, D_MODEL)), full((D_MODEL, D_MODEL))]
    args = [*x_pair, mod_l, g1, g2, *hf, *hb, proj, *ob, *oc, *od, hn, wmg, bmg, wbr, wout]
    if moe:
        h2_spec = pl.BlockSpec((MG_TM * TOK_TILE, LANES), lambda s, i: (s * nt + i, 0))
        h2_shape = jax.ShapeDtypeStruct((N_TOK * TOK_TILE, LANES), F32)
    else:
        h2_spec, h2_shape = row(D_MODEL), jax.ShapeDtypeStruct((N_TOK, D_MODEL), BF16)
    out_specs = [row(D_MODEL), h2_spec]
    out_shape = [jax.ShapeDtypeStruct((N_TOK, D_MODEL), F32), h2_shape]
    if moe:
        in_specs += [full((N_EXPERTS, D_MODEL)), pl.BlockSpec(memory_space=pltpu.SMEM)]
        args += list(router)
        out_specs.append(row(LANES))
        out_shape.append(jax.ShapeDtypeStruct((N_TOK, LANES), F32))
    return pl.pallas_call(
        functools.partial(_merge_kernel, moe=moe),
        grid=(N_SEG, nt),
        in_specs=in_specs,
        out_specs=out_specs,
        out_shape=out_shape,
        compiler_params=_cparams(("arbitrary", "arbitrary")),
        name="merge",
    )(*args)


FF_TM = 512


def _ffn_kernel(h2, w1, w3, w2, x1, mod_ref, out):
    h = h2[...]
    a = jnp.dot(h, w1[...], preferred_element_type=F32)
    b = jnp.dot(h, w3[...], preferred_element_type=F32)
    act = (jax.nn.silu(a) * b).astype(BF16)
    gate2 = mod_ref[0][:, 5 * D_MODEL:6 * D_MODEL]
    out[...] = x1[...] + gate2 * jnp.dot(act, w2[...], preferred_element_type=F32)


def _ffn(h2, x1, mod_l, w1, w3, w2):
    nt = N_TOK // FF_TM
    per_seg = SEG // FF_TM
    resident = lambda shape: pl.BlockSpec(shape, lambda i: (0, 0), pipeline_mode=pl.Buffered(1))
    return pl.pallas_call(
        _ffn_kernel,
        grid=(nt,),
        in_specs=[pl.BlockSpec((FF_TM, D_MODEL), lambda i: (i, 0)),
                  resident((D_MODEL, D_FF)), resident((D_MODEL, D_FF)), resident((D_FF, D_MODEL)),
                  pl.BlockSpec((FF_TM, D_MODEL), lambda i: (i, 0)),
                  pl.BlockSpec((1, 1, 6 * D_MODEL), lambda i: (i // per_seg, 0, 0))],
        out_specs=pl.BlockSpec((FF_TM, D_MODEL), lambda i: (i, 0)),
        out_shape=jax.ShapeDtypeStruct((N_TOK, D_MODEL), F32),
        compiler_params=_cparams(("arbitrary",)),
        name="ffn",
    )(h2, w1, w3, w2, x1, mod_l)


MOE_TM = 256
MOE_SLOTS = 2 * N_TOK
MOE_TILES = MOE_SLOTS // MOE_TM + N_EXPERTS
MOE_NBUF = 3
MOE_STEPS = MOE_TILES + MOE_NBUF
MOE_DUMP = MOE_NBUF * MOE_TM
MOE_LEAD = 1
MOE_PLAN_TILES = MOE_LEAD + MOE_TILES + 2
MOE_FCHUNKS = 1
MOE_UNROLL = 8


def _moe_group_kernel(texp, nused, src_tok, dst_row, h2_hbm, w1, w3, w2, y_hbm, xs, ys, sem_in, sem_out):
    del texp
    i = pl.program_id(0)
    n_used = nused[0]
    buf = i % MOE_NBUF
    buf_next = (i + 2) % MOE_NBUF

    def tile_rows(t):
        start = t * TOK_TILE
        return pl.ds(start if isinstance(start, int) else pl.multiple_of(start, TOK_TILE), TOK_TILE)

    def gather_copy(tile, b, r):
        tok = src_tok[(tile + MOE_LEAD) * MOE_TM + r]
        return pltpu.make_async_copy(h2_hbm.at[tile_rows(tok)], xs.at[b, tile_rows(r)], sem_in.at[b])

    def scatter_copy(tile, b, r):
        dst = dst_row[(tile + MOE_LEAD) * MOE_TM + r]
        return pltpu.make_async_copy(ys.at[b, tile_rows(r)], y_hbm.at[tile_rows(dst)], sem_out.at[b])

    def start_rows_loop(make_copy, tile, b):
        def body(r, carry):
            make_copy(tile, b, r).start()
            return carry
        lax.fori_loop(0, MOE_TM, body, 0, unroll=MOE_UNROLL)

    def wait_tile(b, gather):
        if gather:
            pltpu.make_async_copy(h2_hbm.at[pl.ds(0, MOE_TM * TOK_TILE)], xs.at[b], sem_in.at[b]).wait()
        else:
            pltpu.make_async_copy(ys.at[b], y_hbm.at[pl.ds(0, MOE_TM * TOK_TILE)], sem_out.at[b]).wait()

    @pl.when(i == 0)
    def _():
        xs[...] = jnp.zeros(xs.shape, F32)
        ys[...] = jnp.zeros(ys.shape, F32)
        for b in range(MOE_NBUF):
            fill = pltpu.make_async_copy(ys.at[b], y_hbm.at[pl.ds((MOE_SLOTS + b * MOE_TM) * TOK_TILE, MOE_TM * TOK_TILE)],
                                         sem_out.at[b])
            fill.start()
            fill.wait()
        start_rows_loop(gather_copy, 0, 0)
        start_rows_loop(gather_copy, 1, 1)

    @pl.when(i <= n_used + 1)
    def _():
        wait_tile(buf, True)

    @pl.when((i >= 2) & (i <= n_used + 2))
    def _():
        wait_tile(buf, False)

    @pl.when(i < n_used)
    def _():
        x = _load_token_tiles(xs.at[buf], MOE_TM).astype(BF16)
        fc = D_FF_EXPERT // MOE_FCHUNKS
        rc = MOE_TM // MOE_FCHUNKS
        y = None
        for c in range(MOE_FCHUNKS):
            for r in range(c * rc, (c + 1) * rc):
                gather_copy(i + 2, buf_next, r).start()
                scatter_copy(i - 1, buf_next, r).start()
            a = jnp.dot(x, w1[0, :, c * fc:(c + 1) * fc], preferred_element_type=F32)
            b = jnp.dot(x, w3[0, :, c * fc:(c + 1) * fc], preferred_element_type=F32)
            act = (jax.nn.silu(a) * b).astype(BF16)
            part = jnp.dot(act, w2[0, c * fc:(c + 1) * fc, :], preferred_element_type=F32)
            y = part if y is None else y + part
        _store_token_tiles(ys.at[buf], y)

    @pl.when(i == n_used)
    def _():
        start_rows_loop(scatter_copy, i - 1, buf_next)


def _moe_group(tile_expert, n_used, src_tok, dst_row, h2, w1, w3, w2):
    wspec = lambda shape: pl.BlockSpec((1,) + shape, lambda i, texp, *_: (texp[jnp.minimum(i, MOE_TILES - 1)], 0, 0))
    grid_spec = pltpu.PrefetchScalarGridSpec(
        num_scalar_prefetch=4,
        grid=(MOE_STEPS,),
        in_specs=[pl.BlockSpec(memory_space=pl.ANY),
                  wspec((D_MODEL, D_FF_EXPERT)), wspec((D_MODEL, D_FF_EXPERT)), wspec((D_FF_EXPERT, D_MODEL))],
        out_specs=pl.BlockSpec(memory_space=pl.ANY),
        scratch_shapes=[pltpu.VMEM((MOE_NBUF, MOE_TM * TOK_TILE, LANES), F32),
                        pltpu.VMEM((MOE_NBUF, MOE_TM * TOK_TILE, LANES), F32),
                        pltpu.SemaphoreType.DMA((MOE_NBUF,)), pltpu.SemaphoreType.DMA((MOE_NBUF,))])
    return pl.pallas_call(
        _moe_group_kernel,
        grid_spec=grid_spec,
        out_shape=jax.ShapeDtypeStruct(((MOE_SLOTS + MOE_DUMP) * TOK_TILE, LANES), F32),
        compiler_params=_cparams(("arbitrary",)),
        name="moe_group",
    )(tile_expert, n_used, src_tok, dst_row, h2, w1, w3, w2)


def _moe_plan(expert_ids):
    e_flat = expert_ids.T.reshape(-1)
    order = jnp.argsort(e_flat, stable=True).astype(jnp.int32)
    counts = jnp.sum((e_flat[:, None] == jnp.arange(N_EXPERTS)[None, :]).astype(jnp.int32), axis=0)
    padded = (counts + MOE_TM - 1) // MOE_TM * MOE_TM
    pend = jnp.cumsum(padded)
    pstart = pend - padded
    ustart = jnp.cumsum(counts) - counts
    n_used = pend[-1] // MOE_TM
    tiles = jnp.arange(MOE_TILES, dtype=jnp.int32)
    last_used = jnp.minimum(tiles, n_used - 1)
    tile_expert = jnp.sum((last_used[:, None] * MOE_TM >= pend[None, :]).astype(jnp.int32), axis=1)
    t = jnp.arange(-MOE_LEAD, MOE_PLAN_TILES - MOE_LEAD, dtype=jnp.int32)[:, None]
    r = jnp.arange(MOE_TM, dtype=jnp.int32)[None, :]
    e_t = tile_expert[jnp.clip(t, 0, MOE_TILES - 1)]
    off = t * MOE_TM + r - pstart[e_t]
    valid = (t >= 0) & (t < n_used) & (off < counts[e_t])
    slot = order[jnp.clip(ustart[e_t] + off, 0, MOE_SLOTS - 1)]
    src_tok = jnp.where(valid, slot % N_TOK, 0)
    dst_row = jnp.where(valid, slot, MOE_SLOTS + (t % MOE_NBUF) * MOE_TM + r)
    return (tile_expert.astype(jnp.int32), n_used.reshape(1).astype(jnp.int32),
            src_tok.reshape(-1).astype(jnp.int32), dst_row.reshape(-1).astype(jnp.int32))


def _moe_combine_kernel(x1, y0, y1, route, mod_ref, out_c, out_l):
    gate2 = mod_ref[0][:, 5 * D_MODEL:6 * D_MODEL]
    r = route[...]
    val = x1[...] + gate2 * (r[:, 0:1] * _load_token_tiles(y0, FF_TM) + r[:, 1:2] * _load_token_tiles(y1, FF_TM))
    is_ctx = pl.program_id(0) < SEG // FF_TM

    @pl.when(is_ctx)
    def _():
        out_c[...] = val

    @pl.when(jnp.logical_not(is_ctx))
    def _():
        out_l[...] = val


def _moe_combine(x1, y_slots, route, mod_l):
    nt = N_TOK // FF_TM
    per_seg = SEG // FF_TM
    return pl.pallas_call(
        _moe_combine_kernel,
        grid=(nt,),
        in_specs=[pl.BlockSpec((FF_TM, D_MODEL), lambda i: (i, 0)),
                  pl.BlockSpec((FF_TM * TOK_TILE, LANES), lambda i: (i, 0)),
                  pl.BlockSpec((FF_TM * TOK_TILE, LANES), lambda i: (nt + i, 0)),
                  pl.BlockSpec((FF_TM, LANES), lambda i: (i, 0)),
                  pl.BlockSpec((1, 1, 6 * D_MODEL), lambda i: (i // per_seg, 0, 0))],
        out_specs=[pl.BlockSpec((FF_TM, D_MODEL), lambda i: (jnp.minimum(i, per_seg - 1), 0)),
                   pl.BlockSpec((FF_TM, D_MODEL), lambda i: (jnp.maximum(i - per_seg, 0), 0))],
        out_shape=[jax.ShapeDtypeStruct((SEG, D_MODEL), F32), jax.ShapeDtypeStruct((N_TOK - SEG, D_MODEL), F32)],
        compiler_params=_cparams(("arbitrary",)),
        name="moe_combine",
    )(x1, y_slots, y_slots, route, mod_l)


def _rope_tables():
    t = np.arange(DEC_SEQ)
    row, col = (t // GRID_W).astype(np.float32), (t % GRID_W).astype(np.float32)
    nf = HEAD_DIM // 4
    freqs = np.float32(ROPE_BASE) ** (-np.arange(nf, dtype=np.float32) / np.float32(nf))
    lane = np.arange(LANES) % HEAD_DIM
    fidx = lane % nf
    use_col = (lane // (HEAD_DIM // 2)) == 1
    first = (lane % (HEAD_DIM // 2)) < nf
    pos = np.where(use_col[None, :], col[:, None], row[:, None])
    ang = (pos * freqs[fidx][None, :]).astype(np.float32).astype(np.float64)
    sin = np.sin(ang)
    return (jnp.asarray(np.cos(ang), dtype=F32), jnp.asarray(np.where(first[None, :], -sin, sin), dtype=F32))


def _permute_w_in(w):
    sizes = (256, 256, 256, 256, 8, 8, 256, 128, 128, 256, 256, 256, 256, 256)
    offs = np.concatenate([[0], np.cumsum(sizes)])
    part = lambda i: w[:, offs[i]:offs[i + 1]]
    mq, mk, mv, mo, mi, mf, sq, sk, sv, rx, ry, nq, nk, nv = (part(i) for i in range(14))
    pad = jnp.zeros((w.shape[0], LANES - 16), w.dtype)
    return jnp.concatenate([mq, mk, mv, mo, sq, rx, ry, nq, nk, nv, sk, sv, mi, mf, pad], axis=1)


def _block_diag(w):
    eye = jnp.eye(RG_BLOCKS, dtype=w.dtype)
    return (w[:, :, None, :] * eye[:, None, :, None]).reshape(RG_WIDTH, RG_WIDTH)


def _tile2(g):
    return jnp.concatenate([g, g]).reshape(1, LANES)


def kernel(x_prompt, x_sample, cache_swa_k, cache_swa_v, cache_na_k, cache_na_v, state_mlstm_C, state_mlstm_n, state_mlstm_m, state_rglru_h, c, c_ctx, norm1_g, norm2_g, w_ada, b_ada, w_in, ml_b_i, ml_b_f, ml_hn, sw_qn, sw_kn, sw_sink, rg_conv_w, rg_conv_b, rg_w_r, rg_b_r, rg_w_i, rg_b_i, rg_lam, na_qn, na_kn, na_rpb, w_br, w_mg, b_mg, w_out, ffn_w1, ffn_w3, ffn_w2, moe_wr, moe_br, moe_w1, moe_w3, moe_w2):
    assert DEPTH % 2 == 0
    x_all = (x_prompt.reshape(SEG, D_MODEL), x_sample.reshape(N_TOK - SEG, D_MODEL))
    cvecs = jnp.concatenate([c_ctx[None, :], c, jnp.zeros((8 - 1 - DEC_BATCH, D_MODEL), F32)], axis=0)
    mod = _mod_table(cvecs.T, w_ada, b_ada)
    cos_t, sin_t = _rope_tables()
    nj = 2 * ML_HEADS
    zeros_state = (jnp.zeros((BATCH, nj // 2, LANES, LANES), F32), jnp.zeros((BATCH, nj // 2, LANES, LANES), F32),
                   jnp.zeros((BATCH, nj, LANES), F32), jnp.zeros((BATCH // RG_NSEG, 2, RG_NSEG, RG_WIDTH), F32))
    ctx_out = []
    for l in range(DEPTH):
        mod_l = mod[l].reshape(8, 1, 6 * D_MODEL)
        qk_gains = jnp.stack([_tile2(sw_qn[l])[0], _tile2(sw_kn[l])[0], _tile2(na_qn[l])[0], _tile2(na_kn[l])[0]])
        proj = _inproj(x_all, mod_l, norm1_g[l].reshape(1, D_MODEL), _permute_w_in(w_in[l]).astype(BF16),
                       qk_gains, cos_t, sin_t)
        gate_bias = jnp.concatenate([ml_b_i[l].reshape(-1), ml_b_f[l].reshape(-1),
                                     jnp.zeros((LANES - 2 * nj,), F32)]).reshape(1, LANES)
        hf_c, hb_c, *st_new = _mlstm(proj, gate_bias, *zeros_state[:3], bsz=BATCH, seq=SEQ, row0=0)
        c_new, n_new, m_new = _mlstm_unpack_state(*st_new)
        st_lat = _mlstm_pack_state(state_mlstm_C[:, l].reshape(DEC_BATCH, nj, HEAD_DIM, HEAD_DIM),
                                   state_mlstm_n[:, l].reshape(DEC_BATCH, nj, HEAD_DIM),
                                   state_mlstm_m[:, l].reshape(DEC_BATCH, nj))
        hf_l, hb_l, _, _, _ = _mlstm(proj, gate_bias, *st_lat, bsz=DEC_BATCH, seq=DEC_SEQ, row0=SEG)
        wg = jnp.concatenate([_block_diag(rg_w_r[l, 0]), _block_diag(rg_w_i[l, 0]),
                              _block_diag(rg_w_r[l, 1]), _block_diag(rg_w_i[l, 1])], axis=1).astype(BF16)
        bg = jnp.concatenate([rg_b_r[l, 0], rg_b_i[l, 0], rg_b_r[l, 1], rg_b_i[l, 1]]).reshape(1, 4 * RG_WIDTH)
        rg_args = (rg_conv_w[l], rg_conv_b[l].reshape(1, RG_WIDTH), wg, bg, rg_lam[l])
        oc_c, hl_c = _rglru(proj, *rg_args, zeros_state[3], nblk=BATCH // RG_NSEG, seg_len=SEQ, chained=False, row0=0)
        hl_new = jnp.transpose(hl_c, (0, 2, 1, 3)).reshape(BATCH, 2, RG_WIDTH)
        h0_lat = jnp.broadcast_to(state_rglru_h[:, l][:, :, None, :], (DEC_BATCH, 2, RG_NSEG, RG_WIDTH))
        oc_l, _ = _rglru(proj, *rg_args, h0_lat, nblk=DEC_BATCH, seg_len=DEC_SEQ // RG_NSEG, chained=True, row0=SEG)
        ob_c, od_c = _ctx_attn(proj, sw_sink[l])
        ob_l = _swa(proj, cache_swa_k[:, l].reshape(DEC_BATCH, PAST_LEN, 128),
                    cache_swa_v[:, l].reshape(DEC_BATCH, PAST_LEN, 128), sw_sink[l])
        od_l = _na(proj, cache_na_k[:, l].reshape(DEC_BATCH, PAST_LEN, 256),
                   cache_na_v[:, l].reshape(DEC_BATCH, PAST_LEN, 256), _na_bias(na_rpb[l].reshape(-1)))
        moe_layer = l % 2 == 1
        j = l // 2
        router = (moe_wr[j].T, moe_br[j]) if moe_layer else None
        outs = _merge(x_all, mod_l, norm1_g[l].reshape(1, D_MODEL), norm2_g[l].reshape(1, D_MODEL),
                      (hf_c, hf_l), (hb_c, hb_l), proj, (ob_c, ob_l), (oc_c, oc_l), (od_c, od_l),
                      _tile2(ml_hn[l]), w_mg[l].astype(BF16), b_mg[l].reshape(1, -1), w_br[l].astype(BF16),
                      w_out[l].astype(BF16), router)
        if moe_layer:
            x1, h2, route = outs
            plan = _moe_plan(route[:, 2:4].astype(jnp.int32))
            y_slots = _moe_group(*plan, h2, moe_w1[j].astype(BF16), moe_w3[j].astype(BF16), moe_w2[j].astype(BF16))
            x_all = tuple(_moe_combine(x1, y_slots, route, mod_l))
        else:
            x1, h2 = outs
            x_all = _ffn(h2, x1, mod_l, ffn_w1[j].astype(BF16), ffn_w3[j].astype(BF16), ffn_w2[j].astype(BF16))
        pc = proj[:SEG]
        ctx_out.append(dict(
            sw_k=pc[:, C_SK:C_SK + 128].reshape(BATCH, SEQ, SW_KV_HEADS, HEAD_DIM),
            sw_v=pc[:, C_SV:C_SV + 128].reshape(BATCH, SEQ, SW_KV_HEADS, HEAD_DIM),
            na_k=pc[:, C_NK:C_NK + 256].reshape(BATCH, SEQ, NA_HEADS, HEAD_DIM),
            na_v=pc[:, C_NV:C_NV + 256].reshape(BATCH, SEQ, NA_HEADS, HEAD_DIM),
            ml_C=c_new.reshape(BATCH, 2, ML_HEADS, HEAD_DIM, HEAD_DIM),
            ml_n=n_new.reshape(BATCH, 2, ML_HEADS, HEAD_DIM),
            ml_m=m_new.reshape(BATCH, 2, ML_HEADS),
            rg_h=hl_new))
    stack = lambda name: jnp.stack([t[name] for t in ctx_out], axis=1)
    return (x_all[0].reshape(BATCH, SEQ, D_MODEL), x_all[1].reshape(DEC_BATCH, DEC_SEQ, D_MODEL),
            stack('sw_k'), stack('sw_v'), stack('na_k'), stack('na_v'),
            stack('ml_C'), stack('ml_n'), stack('ml_m'), stack('rg_h'))
```

```python
import functools

import numpy as np
import jax
import jax.numpy as jnp
from jax import lax
from jax.experimental import pallas as pl
from jax.experimental.pallas import tpu as pltpu

F32 = jnp.float32
BF16 = jnp.bfloat16

D_MODEL = 1024
BATCH = 16
SEQ = 256
DEPTH = 2
DEC_BATCH = 2
DEC_SEQ = 4096
PAST_LEN = 256
GRID_W = 64
HEAD_DIM = 64
ML_HEADS = 4
ML_CHUNK = 128
ML_SUB = 2
SW_HEADS = 4
SW_KV_HEADS = 2
SW_WINDOW = 128
RG_WIDTH = 256
RG_BLOCKS = 4
RG_CONV = 4
RG_C = 8.0
NA_HEADS = 4
NA_ROWS = 8
NA_COLS = 16
N_BRANCH = 4
ROPE_BASE = 10000.0
D_FF = 2816
N_EXPERTS = 8
D_FF_EXPERT = 2048
EPS = 1e-6
NEG = -1e30
SCALE = HEAD_DIM ** -0.5

SEG = 4096
N_SEG = 3
N_TOK = N_SEG * SEG
LANES = 128
VMEM_LIMIT = 60 * 1024 * 1024

C_MQ, C_MK, C_MV, C_MO = 0, 256, 512, 768
C_SQ, C_RX, C_RY, C_NQ, C_NK, C_NV = 1024, 1280, 1536, 1792, 2048, 2304
C_SK, C_SV, C_G = 2560, 2688, 2816
P_W = 2944


def _cparams(sem):
    return pltpu.CompilerParams(dimension_semantics=sem, vmem_limit_bytes=VMEM_LIMIT)


def _dot(a, b):
    return jnp.dot(a.astype(BF16), b.astype(BF16), preferred_element_type=F32)


def _dot_nt(a, b):
    return lax.dot_general(a.astype(BF16), b.astype(BF16), (((1,), (1,)), ((), ())),
                           preferred_element_type=F32)


def _dot_tn(a, b):
    return lax.dot_general(a.astype(BF16), b.astype(BF16), (((0,), (0,)), ((), ())),
                           preferred_element_type=F32)


def _split3(x):
    hi = x.astype(BF16)
    r1 = x - hi.astype(F32)
    mid = r1.astype(BF16)
    lo = (r1 - mid.astype(F32)).astype(BF16)
    return hi, mid, lo


def _dot_exact_rhs(a01, x):
    hi, mid, lo = _split3(x)
    d = lambda p: jnp.dot(a01, p, preferred_element_type=F32)
    return d(hi) + d(mid) + d(lo)


def _dot_exact_lhs(x, a01):
    hi, mid, lo = _split3(x)
    d = lambda p: jnp.dot(p, a01, preferred_element_type=F32)
    return d(hi) + d(mid) + d(lo)


def _sigmoid(x):
    return 0.5 * jnp.tanh(0.5 * x) + 0.5


def _rms(x, g):
    return x * lax.rsqrt(jnp.mean(x * x, axis=-1, keepdims=True) + EPS) * g


def _rms_head_pairs(x, g):
    lane = lax.broadcasted_iota(jnp.int32, x.shape, 1)
    left = lane < HEAD_DIM
    sq = x * x
    s0 = jnp.sum(jnp.where(left, sq, 0.0), axis=-1, keepdims=True)
    s1 = jnp.sum(jnp.where(left, 0.0, sq), axis=-1, keepdims=True)
    ms = jnp.where(left, s0, s1) * (1.0 / HEAD_DIM)
    return x * lax.rsqrt(ms + EPS) * g


MOD_TN = 1536
MOD_ROWS = 3


def _mod_kernel(ct_ref, w_ref, b_ref, o_ref):
    ct = ct_ref[...]
    st = ct * jax.nn.sigmoid(ct)
    w = w_ref[0]
    o_ref[...] = jnp.zeros(o_ref.shape, F32)
    for r in range(MOD_ROWS):
        o_ref[0, r:r + 1, :] = jnp.sum(w * st[:, r:r + 1], axis=0, keepdims=True) + b_ref[0]


def _mod_table(cvecs_t, w_ada, b_ada):
    n = 6 * D_MODEL
    return pl.pallas_call(
        _mod_kernel,
        grid=(DEPTH, n // MOD_TN),
        in_specs=[pl.BlockSpec((D_MODEL, 8), lambda l, j: (0, 0)),
                  pl.BlockSpec((1, D_MODEL, MOD_TN), lambda l, j: (l, 0, j)),
                  pl.BlockSpec((1, 1, MOD_TN), lambda l, j: (l, 0, j))],
        out_specs=pl.BlockSpec((1, 8, MOD_TN), lambda l, j: (l, 0, j)),
        out_shape=jax.ShapeDtypeStruct((DEPTH, 8, n), F32),
        compiler_params=_cparams(("arbitrary", "arbitrary")),
        name="adaln_mod",
    )(cvecs_t, w_ada, b_ada.reshape(DEPTH, 1, n))


IN_TM = 512


def _swap16(y):
    lane = lax.broadcasted_iota(jnp.int32, y.shape, 1)
    first = (lane % 32) < 16
    return jnp.where(first, pltpu.roll(y, LANES - 16, 1), pltpu.roll(y, 16, 1))


def _seg_pair_specs(tm, width, lat_row0=0):
    nt = SEG // tm
    lat_off = lat_row0 // tm
    return (pl.BlockSpec((tm, width), lambda s, i: (jnp.minimum(s * nt + i, nt - 1), 0)),
            pl.BlockSpec((tm, width), lambda s, i: (lat_off + jnp.maximum(s * nt + i - nt, 0), 0)))


TOK_TILE = D_MODEL // LANES


def _store_token_tiles(ref, x):
    for s in range(TOK_TILE):
        ref[pl.ds(s, x.shape[0], stride=TOK_TILE), :] = x[:, LANES * s:LANES * (s + 1)]


def _load_token_tiles(ref, n_tok):
    return jnp.concatenate([ref[pl.ds(s, n_tok, stride=TOK_TILE), :] for s in range(TOK_TILE)], axis=1)


def _pick(c_ref, l_ref):
    return jnp.where(pl.program_id(0) == 0, c_ref[...], l_ref[...])


def _x_pair(x):
    return (x, 0) if isinstance(x, tuple) else ((x, x), SEG)


def _inproj_kernel(xc_ref, xl_ref, mod_ref, g_ref, w_ref, qkg_ref, cos_ref, sin_ref, o_ref):
    seg = pl.program_id(0)
    mod = mod_ref[0]
    sh1 = mod[:, 0:D_MODEL]
    sc1 = mod[:, D_MODEL:2 * D_MODEL]
    h = _rms(_pick(xc_ref, xl_ref), g_ref[...]) * (1.0 + sc1) + sh1
    r = jnp.dot(h.astype(BF16), w_ref[...], preferred_element_type=F32)
    o_ref[:, 0:C_SQ] = r[:, 0:C_SQ]
    o_ref[:, C_RX:C_NQ] = r[:, C_RX:C_NQ]
    o_ref[:, C_NV:C_SK] = r[:, C_NV:C_SK]
    o_ref[:, C_SV:P_W] = r[:, C_SV:P_W]
    cos = cos_ref[...]
    sin = sin_ref[...]
    latent = seg > 0

    def rope(y):
        return jnp.where(latent, y * cos + _swap16(y) * sin, y)

    for p in range(2):
        a = C_SQ + LANES * p
        o_ref[:, a:a + LANES] = rope(_rms_head_pairs(r[:, a:a + LANES], qkg_ref[0:1, :]))
    o_ref[:, C_SK:C_SK + LANES] = rope(_rms_head_pairs(r[:, C_SK:C_SK + LANES], qkg_ref[1:2, :]))
    for p in range(2):
        a = C_NQ + LANES * p
        o_ref[:, a:a + LANES] = _rms_head_pairs(r[:, a:a + LANES], qkg_ref[2:3, :])
        a = C_NK + LANES * p
        o_ref[:, a:a + LANES] = _rms_head_pairs(r[:, a:a + LANES], qkg_ref[3:4, :])


def _inproj(x, mod_l, norm1, w_in_p, qk_gains, cos_t, sin_t):
    nt = SEG // IN_TM
    x_pair, lat_row0 = _x_pair(x)
    return pl.pallas_call(
        _inproj_kernel,
        grid=(N_SEG, nt),
        in_specs=[*_seg_pair_specs(IN_TM, D_MODEL, lat_row0),
                  pl.BlockSpec((1, 1, 6 * D_MODEL), lambda s, i: (s, 0, 0)),
                  pl.BlockSpec((1, D_MODEL), lambda s, i: (0, 0)),
                  pl.BlockSpec((D_MODEL, P_W), lambda s, i: (0, 0)),
                  pl.BlockSpec((4, LANES), lambda s, i: (0, 0)),
                  pl.BlockSpec((IN_TM, LANES), lambda s, i: (i, 0)),
                  pl.BlockSpec((IN_TM, LANES), lambda s, i: (i, 0))],
        out_specs=pl.BlockSpec((IN_TM, P_W), lambda s, i: (s * nt + i, 0)),
        out_shape=jax.ShapeDtypeStruct((N_TOK, P_W), F32),
        compiler_params=_cparams(("arbitrary", "arbitrary")),
        name="inproj",
    )(*x_pair, mod_l, norm1, w_in_p, qk_gains, cos_t, sin_t)


def _mlstm_direction(d, rows, q_ref, k_ref, v_ref, g_ref, bias, tri_ref, sel_ref, state):
    ch = ML_CHUNK
    r_io = lax.broadcasted_iota(jnp.int32, (ch, ch), 0)
    c_io = lax.broadcasted_iota(jnp.int32, (ch, ch), 1)
    lower = r_io >= c_io
    upper = r_io <= c_io
    mask = lower if d == 0 else upper
    tri = tri_ref[d]
    tri_t = tri_ref[1 - d]
    left = c_io < HEAD_DIM
    top = r_io < HEAD_DIM
    blockdiag = top == left
    cbs_in, nbs_in, m_old = state
    g = g_ref[rows, :] + bias[...]
    b_cols = _dot_exact_rhs(tri, jax.nn.log_sigmoid(g))
    b3 = jnp.concatenate(_split3(b_cols), axis=1)
    gt = g.T
    li_rows = gt[0:8, :]
    b_rows = _dot_exact_lhs(jax.nn.log_sigmoid(gt[8:16, :]), tri_t)
    a_rows = li_rows - b_rows
    bl = b_rows[:, ch - 1:ch] if d == 0 else b_rows[:, 0:1]
    g_rows = bl - b_rows + li_rows
    m_new = jnp.maximum(bl + m_old, jnp.max(g_rows, axis=1, keepdims=True))
    wk_rows = jnp.exp(g_rows - m_new)
    wp = jnp.exp(bl + m_old - m_new)
    ones_blk = jnp.ones((ch, LANES), BF16)
    left2 = lax.broadcasted_iota(jnp.int32, (ch, 2 * LANES), 1) % LANES < HEAD_DIM
    h_out, c_out, n_out = [], [], []
    for p in range(ML_HEADS // 2):
        lanes = slice(LANES * p, LANES * (p + 1))
        j0 = ML_HEADS * d + 2 * p
        q2 = q_ref[rows, lanes]
        k2t = (k_ref[rows, lanes] * SCALE).T.astype(BF16)
        v2e = jnp.concatenate([v_ref[rows, lanes].astype(BF16), ones_blk], axis=1)
        cb = cbs_in[p]
        nb = nbs_in[p]
        q2b = q2.astype(BF16)
        q_lo = (q2 - q2b.astype(F32)).astype(BF16)
        nb_hi = nb.astype(BF16)
        nb_lo = (nb - nb_hi.astype(F32)).astype(BF16)
        qc = jnp.dot(q2b, cb.astype(BF16), preferred_element_type=F32)
        qn = jnp.dot(jnp.concatenate([q2b, q_lo, q2b], axis=1), jnp.concatenate([nb_hi, nb_hi, nb_lo], axis=0),
                     preferred_element_type=F32)
        b_pair = jnp.dot(b3, sel_ref[2 * d + p], preferred_element_type=F32)
        cbs, sves = [], []
        for i in range(2):
            j = j0 + i
            half = left if i == 0 else jnp.logical_not(left)
            a_mat = jnp.where(mask, a_rows[j:j + 1, :], NEG)
            cvec = jnp.maximum(m_old[j:j + 1, :], jnp.max(a_mat, axis=1, keepdims=True))
            cbro = jnp.broadcast_to(cvec, (ch, ch))
            s = jnp.dot(jnp.where(half, q2b, 0), k2t, preferred_element_type=F32) * jnp.exp(a_mat - cbro)
            s_hi = s.astype(BF16)
            s_lo = (s - s_hi.astype(F32)).astype(BF16)
            sve = jnp.dot(s_hi, v2e, preferred_element_type=F32)
            rs_lo = jnp.dot(s_lo, ones_blk, preferred_element_type=F32)
            sves.append(jnp.concatenate([sve[:, 0:LANES], sve[:, LANES:2 * LANES] + rs_lo], axis=1))
            cbs.append(cbro)
        c_pair = jnp.where(left, cbs[0], cbs[1])
        w_prev = jnp.exp(jnp.where(left, m_old[j0:j0 + 1, :], m_old[j0 + 1:j0 + 2, :]) - c_pair)
        sve = jnp.where(left2, sves[0], sves[1])
        num = w_prev * qc + sve[:, 0:LANES]
        den = w_prev * qn + sve[:, LANES:2 * LANES]
        h_out.append(num / jnp.maximum(jnp.abs(den), jnp.exp(-(c_pair + b_pair))))
        kwt = k2t * jnp.where(top, wk_rows[j0:j0 + 1, :], wk_rows[j0 + 1:j0 + 2, :])
        kwt_hi = kwt.astype(BF16)
        kwt_lo = (kwt - kwt_hi.astype(F32)).astype(BF16)
        kve = jnp.dot(kwt_hi, v2e, preferred_element_type=F32)
        kn = kve[:, LANES:2 * LANES] + jnp.dot(kwt_lo, ones_blk, preferred_element_type=F32)
        wp_pair = jnp.where(top, wp[j0:j0 + 1, :], wp[j0 + 1:j0 + 2, :])
        c_out.append(wp_pair * cb + jnp.where(blockdiag, kve[:, 0:LANES], 0.0))
        n_out.append(wp_pair * nb + jnp.where(blockdiag, kn, 0.0))
    return h_out, (c_out, n_out, m_new)


def _mlstm_kernel(qf, kf, vf, gf, qb, kb, vb, gb, c0, n0, m0, bias, tri_ref, sel_ref,
                  hf, hb, co, no, mo, cbd, nbd, m_s, *, nc):
    c = pl.program_id(1)

    @pl.when(c == 0)
    def _():
        cbd[...] = c0[0]
        nbd[...] = n0[0]
        m_s[...] = m0[0]

    npair = ML_HEADS // 2
    for d, (refs, h_ref) in enumerate((((qf, kf, vf, gf), hf), ((qb, kb, vb, gb), hb))):
        state = ([cbd[npair * d + p] for p in range(npair)], [nbd[npair * d + p] for p in range(npair)],
                 m_s[:, 0:1])
        subs = range(ML_SUB) if d == 0 else reversed(range(ML_SUB))
        for sub in subs:
            rows = slice(sub * ML_CHUNK, (sub + 1) * ML_CHUNK)
            h_out, state = _mlstm_direction(d, rows, *refs, bias, tri_ref, sel_ref, state)
            for p in range(npair):
                h_ref[rows, LANES * p:LANES * (p + 1)] = h_out[p]
        c_out, n_out, m_new = state
        for p in range(npair):
            cbd[npair * d + p] = c_out[p]
            nbd[npair * d + p] = n_out[p]
        heads = slice(ML_HEADS * d, ML_HEADS * (d + 1))
        m_s[heads, :] = jnp.broadcast_to(m_new[heads, :], (ML_HEADS, LANES))

    @pl.when(c == nc - 1)
    def _():
        co[0] = cbd[...]
        no[0] = nbd[...]
        mo[0] = m_s[...]


def _mlstm_pack_state(c0, n0, m0):
    bsz = c0.shape[0]
    hd = HEAD_DIM
    eye = jnp.eye(2, dtype=F32)[None, None, :, None, :, None]
    cbd = c0.reshape(bsz, ML_HEADS, 2, hd, 1, hd) * eye
    nbd = jnp.broadcast_to(n0.reshape(bsz, ML_HEADS, 2, hd, 1, 1) * eye, cbd.shape)
    to_mat = lambda t: t.reshape(bsz, ML_HEADS, LANES, LANES)
    return to_mat(cbd), to_mat(nbd), jnp.broadcast_to(m0[..., None], m0.shape + (LANES,))


def _mlstm_unpack_state(cbd, nbd, mrow):
    hd = HEAD_DIM
    bsz = cbd.shape[0]
    c = jnp.stack([cbd[:, :, :hd, :hd], cbd[:, :, hd:, hd:]], axis=2).reshape(bsz, 2 * ML_HEADS, hd, hd)
    n = jnp.stack([nbd[:, :, :hd, 0], nbd[:, :, hd:, hd]], axis=2).reshape(bsz, 2 * ML_HEADS, hd)
    return c, n, mrow[:, :, 0]


def _mlstm_tables():
    r = np.arange(ML_CHUNK)
    lower = (r[:, None] >= r[None, :]).astype(np.float32)
    row = np.arange(3 * LANES)[:, None] % LANES
    lane_left = np.arange(LANES)[None, :] < HEAD_DIM
    sel = [row == np.where(lane_left, 8 + ML_HEADS * d + 2 * p, 9 + ML_HEADS * d + 2 * p)
           for d in range(2) for p in range(ML_HEADS // 2)]
    return (jnp.asarray(np.stack([lower, lower.T]), dtype=BF16),
            jnp.asarray(np.stack(sel).astype(np.float32), dtype=BF16))


def _mlstm(proj, gate_bias, c0, n0, m0, *, bsz, seq, row0):
    blk = ML_SUB * ML_CHUNK
    nc = seq // blk
    tri, sel = _mlstm_tables()
    base = row0 // blk
    nj = 2 * ML_HEADS
    fw = lambda col: (lambda b, c: (base + b * nc + c, col))
    bw = lambda col: (lambda b, c: (base + b * nc + nc - 1 - c, col))
    qkv = lambda f: [pl.BlockSpec((blk, 256), f(C_MQ // 256)),
                     pl.BlockSpec((blk, 256), f(C_MK // 256)),
                     pl.BlockSpec((blk, 256), f(C_MV // 256)),
                     pl.BlockSpec((blk, LANES), f(C_G // LANES))]
    st_specs = [pl.BlockSpec((1, nj // 2, LANES, LANES), lambda b, c: (b, 0, 0, 0)),
                pl.BlockSpec((1, nj // 2, LANES, LANES), lambda b, c: (b, 0, 0, 0)),
                pl.BlockSpec((1, nj, LANES), lambda b, c: (b, 0, 0))]
    st_shapes = [jax.ShapeDtypeStruct((bsz, nj // 2, LANES, LANES), F32),
                 jax.ShapeDtypeStruct((bsz, nj // 2, LANES, LANES), F32),
                 jax.ShapeDtypeStruct((bsz, nj, LANES), F32)]
    return pl.pallas_call(
        functools.partial(_mlstm_kernel, nc=nc),
        grid=(bsz, nc),
        in_specs=qkv(fw) + qkv(bw) + st_specs + [pl.BlockSpec((1, LANES), lambda b, c: (0, 0)),
                                                 pl.BlockSpec(tri.shape, lambda b, c: (0, 0, 0)),
                                                 pl.BlockSpec(sel.shape, lambda b, c: (0, 0, 0))],
        out_specs=[pl.BlockSpec((blk, 256), lambda b, c: (b * nc + c, 0)),
                   pl.BlockSpec((blk, 256), lambda b, c: (b * nc + nc - 1 - c, 0))] + st_specs,
        out_shape=[jax.ShapeDtypeStruct((bsz * seq, 256), F32),
                   jax.ShapeDtypeStruct((bsz * seq, 256), F32)] + st_shapes,
        scratch_shapes=[pltpu.VMEM((nj // 2, LANES, LANES), F32),
                        pltpu.VMEM((nj // 2, LANES, LANES), F32),
                        pltpu.VMEM((nj, LANES), F32)],
        compiler_params=_cparams(("arbitrary", "arbitrary")),
        name="mlstm",
    )(proj, proj, proj, proj, proj, proj, proj, proj, c0, n0, m0, gate_bias, tri, sel)


RG_TC = 256
RG_PAD = 8
RG_NSEG = 8
RG_SKEW = 4


def _rglru_kernel(rx, ry, cw, cb, wg, bg, lam, h0, oc, hl, xpad, af, ab, uf, ub, *, seg_len, chained):
    rows = RG_NSEG * seg_len
    seq_len = rows if chained else seg_len
    halves = RG_WIDTH // LANES
    pitch = seg_len + RG_SKEW
    buf_row = lambda t: (t // seg_len) * pitch + t % seg_len
    xpad[0:RG_PAD, :] = jnp.zeros((RG_PAD, RG_WIDTH), F32)
    xpad[rows + RG_PAD:rows + 2 * RG_PAD, :] = jnp.zeros((RG_PAD, RG_WIDTH), F32)
    xpad[RG_PAD:rows + RG_PAD, :] = rx[...]
    sp = jax.nn.softplus(-lam[...])
    left = (RG_CONV - 1) // 2
    for ci in range(rows // RG_TC):
        s0 = ci * RG_TC
        pos = (s0 + lax.broadcasted_iota(jnp.int32, (RG_TC, RG_WIDTH), 0)) % seq_len
        xc = None
        for j in range(RG_CONV):
            a = RG_PAD + s0 + j - left
            term = xpad[a:a + RG_TC, :] * cw[j:j + 1, :]
            if not chained and j != left:
                term = jnp.where((pos + (j - left) >= 0) & (pos + (j - left) < seq_len), term, 0.0)
            xc = term if xc is None else xc + term
        xc = xc + cb[...]
        pre = _dot(xc, wg[...]) + bg[...]
        for d, (a_ref, u_ref) in enumerate(((af, uf), (ab, ub))):
            o = 2 * RG_WIDTH * d
            r = _sigmoid(pre[:, o:o + RG_WIDTH])
            gi = _sigmoid(pre[:, o + RG_WIDTH:o + 2 * RG_WIDTH])
            log_a = -RG_C * r * sp[d:d + 1, :]
            a_val = jnp.exp(log_a)
            u_val = jnp.sqrt(-jnp.tanh(log_a) * (a_val * a_val + 1.0)) * (gi * xc)
            for hv in range(halves):
                dst = slice(buf_row(s0), buf_row(s0) + RG_TC)
                a_ref[hv, dst, :] = a_val[:, LANES * hv:LANES * (hv + 1)]
                u_ref[hv, dst, :] = u_val[:, LANES * hv:LANES * (hv + 1)]

    def body(s, carry):
        out = []
        for d, (a_ref, u_ref) in enumerate(((af, uf), (ab, ub))):
            step_rows = pl.ds(s if d == 0 else seg_len - 1 - s, RG_NSEG, stride=pitch)
            for hv in range(halves):
                h_loc, prod = carry[2 * (halves * d + hv)], carry[2 * (halves * d + hv) + 1]
                a = a_ref[hv, step_rows, :]
                h_loc = a * h_loc + u_ref[hv, step_rows, :]
                prod = a * prod
                u_ref[hv, step_rows, :] = h_loc
                a_ref[hv, step_rows, :] = prod
                out += [h_loc, prod]
        return tuple(out)

    zero = jnp.zeros((RG_NSEG, LANES), F32)
    one = jnp.ones((RG_NSEG, LANES), F32)
    ends = lax.fori_loop(0, seg_len, body, (zero, one) * (2 * halves), unroll=8)
    seg = lax.broadcasted_iota(jnp.int32, (RG_NSEG, LANES), 0)
    for d, (a_ref, u_ref) in enumerate(((af, uf), (ab, ub))):
        for hv in range(halves):
            lanes = slice(LANES * hv, LANES * (hv + 1))
            h_end, p_end = ends[2 * (halves * d + hv)], ends[2 * (halves * d + hv) + 1]
            h_in = h0[0, d][:, lanes]
            if chained:
                c = h_in[0:1, :]
                h_in = zero
                for k in (range(RG_NSEG) if d == 0 else reversed(range(RG_NSEG))):
                    h_in = jnp.where(seg == k, c, h_in)
                    c = h_end[k:k + 1, :] + p_end[k:k + 1, :] * c
            hl[0, d, :, lanes] = h_end + p_end * h_in
            for k in range(RG_NSEG):
                for ci in range(seg_len // RG_TC):
                    sl = slice(k * pitch + ci * RG_TC, k * pitch + (ci + 1) * RG_TC)
                    u_ref[hv, sl, :] = u_ref[hv, sl, :] + a_ref[hv, sl, :] * h_in[k:k + 1, :]
    for ci in range(rows // RG_TC):
        sl = slice(ci * RG_TC, (ci + 1) * RG_TC)
        for hv in range(halves):
            lanes = slice(LANES * hv, LANES * (hv + 1))
            src = slice(buf_row(ci * RG_TC), buf_row(ci * RG_TC) + RG_TC)
            oc[sl, lanes] = (uf[hv, src, :] + ub[hv, src, :]) * jax.nn.gelu(ry[sl, lanes])


def _rglru(proj, cw, cb, wg, bg, lam, h0, *, nblk, seg_len, chained, row0):
    rows = RG_NSEG * seg_len
    base = row0 // rows
    full = lambda shape: pl.BlockSpec(shape, lambda b: tuple(0 for _ in shape))
    st_spec = pl.BlockSpec((1, 2, RG_NSEG, RG_WIDTH), lambda b: (b, 0, 0, 0))
    return pl.pallas_call(
        functools.partial(_rglru_kernel, seg_len=seg_len, chained=chained),
        grid=(nblk,),
        in_specs=[pl.BlockSpec((rows, RG_WIDTH), lambda b: (base + b, C_RX // RG_WIDTH)),
                  pl.BlockSpec((rows, RG_WIDTH), lambda b: (base + b, C_RY // RG_WIDTH)),
                  full((RG_CONV, RG_WIDTH)), full((1, RG_WIDTH)),
                  full((RG_WIDTH, 4 * RG_WIDTH)), full((1, 4 * RG_WIDTH)), full((2, RG_WIDTH)),
                  st_spec],
        out_specs=[pl.BlockSpec((rows, RG_WIDTH), lambda b: (b, 0)), st_spec],
        out_shape=[jax.ShapeDtypeStruct((nblk * rows, RG_WIDTH), F32),
                   jax.ShapeDtypeStruct((nblk, 2, RG_NSEG, RG_WIDTH), F32)],
        scratch_shapes=[pltpu.VMEM((rows + 2 * RG_PAD, RG_WIDTH), F32)]
        + [pltpu.VMEM((RG_WIDTH // LANES, RG_NSEG * (seg_len + RG_SKEW), LANES), F32) for _ in range(4)],
        compiler_params=_cparams(("arbitrary",)),
        name="rglru",
    )(proj, proj, cw, cb, wg, bg, lam, h0)


def _softmax_pv(scores, values, sink):
    m = functools.reduce(jnp.maximum, [jnp.max(s, axis=-1, keepdims=True) for s in scores])
    if sink is not None:
        m = jnp.maximum(m, sink)
    ps = [jnp.exp(s - m) for s in scores]
    den = functools.reduce(jnp.add, [jnp.sum(p, axis=-1, keepdims=True) for p in ps])
    if sink is not None:
        den = den + jnp.exp(sink - m)
    num = functools.reduce(jnp.add, [_dot(p, v) for p, v in zip(ps, values)])
    return num / den


def _ctx_attn_kernel(sink, sq, sk, sv, nq, nk, nv, ob, od):
    assert SW_HEADS == 4 and SW_KV_HEADS == 2
    left = lax.broadcasted_iota(jnp.int32, (SEQ, LANES), 1) < HEAD_DIM
    first = lax.broadcasted_iota(jnp.int32, (2 * SEQ, 1), 0) < SEQ
    k2 = sk[...].astype(BF16)
    v2 = sv[...].astype(BF16)
    for p in range(SW_KV_HEADS):
        lanes = slice(LANES * p, LANES * (p + 1))
        q2 = sq[:, lanes]
        q_swapped = pltpu.roll(q2, HEAD_DIM, 1)
        kv_half = left if p == 0 else jnp.logical_not(left)
        qs = jnp.concatenate([jnp.where(kv_half, q2 if i == p else q_swapped, 0.0) for i in range(2)], axis=0)
        sink_col = jnp.where(first, sink[2 * p], sink[2 * p + 1])
        res = _softmax_pv([_dot_nt(qs, k2) * SCALE], [v2], sink_col)
        halves = [res[0:SEQ], res[SEQ:2 * SEQ]]
        placed = [halves[i] if i == p else pltpu.roll(halves[i], HEAD_DIM, 1) for i in range(2)]
        ob[:, lanes] = jnp.where(left, placed[0], placed[1])
    for p in range(NA_HEADS // 2):
        lanes = slice(LANES * p, LANES * (p + 1))
        q2 = nq[:, lanes]
        k2 = nk[:, lanes].astype(BF16)
        v2 = nv[:, lanes].astype(BF16)
        res = [_softmax_pv([_dot_nt(jnp.where(left if i == 0 else jnp.logical_not(left), q2, 0.0), k2) * SCALE],
                           [v2], None) for i in range(2)]
        od[:, lanes] = jnp.where(left, res[0], res[1])


def _ctx_attn(proj, sink):
    blk = lambda w, col: pl.BlockSpec((SEQ, w), lambda b: (b, col))
    return pl.pallas_call(
        _ctx_attn_kernel,
        grid=(BATCH,),
        in_specs=[pl.BlockSpec(memory_space=pltpu.SMEM),
                  blk(256, C_SQ // 256), blk(128, C_SK // 128), blk(128, C_SV // 128),
                  blk(256, C_NQ // 256), blk(256, C_NK // 256), blk(256, C_NV // 256)],
        out_specs=[pl.BlockSpec((SEQ, 256), lambda b: (b, 0)),
                   pl.BlockSpec((SEQ, 256), lambda b: (b, 0))],
        out_shape=[jax.ShapeDtypeStruct((SEG, 256), F32), jax.ShapeDtypeStruct((SEG, 256), F32)],
        compiler_params=_cparams(("arbitrary",)),
        name="ctx_attn",
    )(sink, proj, proj, proj, proj, proj, proj)


SW_QB = 128
SW_SPAN = SW_QB + 2 * SW_WINDOW


def _swa_kernel(sink, q, k, v, kc, vc, ob):
    assert SW_HEADS == 4 and SW_KV_HEADS == 2
    n = pl.program_id(1)
    ws = jnp.clip((n - 1) * SW_QB, 0, DEC_SEQ - SW_SPAN)
    ws = pl.multiple_of(ws, SW_QB)
    row = lax.broadcasted_iota(jnp.int32, (2 * SW_QB, SW_SPAN), 0)
    qpos = n * SW_QB + row % SW_QB
    kpos = ws + lax.broadcasted_iota(jnp.int32, (2 * SW_QB, SW_SPAN), 1)
    valid = jnp.abs(qpos - kpos) <= SW_WINDOW
    left = lax.broadcasted_iota(jnp.int32, (SW_QB, LANES), 1) < HEAD_DIM
    first = lax.broadcasted_iota(jnp.int32, (2 * SW_QB, 1), 0) < SW_QB
    k2 = k[pl.ds(ws, SW_SPAN), :].astype(BF16)
    v2 = v[pl.ds(ws, SW_SPAN), :].astype(BF16)
    kc2 = kc[0].astype(BF16)
    vc2 = vc[0].astype(BF16)
    for p in range(SW_KV_HEADS):
        lanes = slice(LANES * p, LANES * (p + 1))
        q2 = q[:, lanes]
        q_swapped = pltpu.roll(q2, HEAD_DIM, 1)
        kv_half = left if p == 0 else jnp.logical_not(left)
        qs = jnp.concatenate([jnp.where(kv_half, q2 if i == p else q_swapped, 0.0) for i in range(2)], axis=0)
        s_loc = jnp.where(valid, _dot_nt(qs, k2) * SCALE, NEG)
        s_ctx = _dot_nt(qs, kc2) * SCALE
        sink_col = jnp.where(first, sink[2 * p], sink[2 * p + 1])
        res = _softmax_pv([s_loc, s_ctx], [v2, vc2], sink_col)
        halves = [res[0:SW_QB], res[SW_QB:2 * SW_QB]]
        placed = [halves[i] if i == p else pltpu.roll(halves[i], HEAD_DIM, 1) for i in range(2)]
        ob[:, lanes] = jnp.where(left, placed[0], placed[1])


def _swa(proj, kc, vc, sink):
    nq = DEC_SEQ // SW_QB
    qbase = SEG // SW_QB
    return pl.pallas_call(
        _swa_kernel,
        grid=(DEC_BATCH, nq),
        in_specs=[pl.BlockSpec(memory_space=pltpu.SMEM),
                  pl.BlockSpec((SW_QB, 256), lambda b, n: (qbase + b * nq + n, C_SQ // 256)),
                  pl.BlockSpec((DEC_SEQ, 128), lambda b, n: (1 + b, C_SK // 128)),
                  pl.BlockSpec((DEC_SEQ, 128), lambda b, n: (1 + b, C_SV // 128)),
                  pl.BlockSpec((1, PAST_LEN, 128), lambda b, n: (b, 0, 0)),
                  pl.BlockSpec((1, PAST_LEN, 128), lambda b, n: (b, 0, 0))],
        out_specs=pl.BlockSpec((SW_QB, 256), lambda b, n: (b * nq + n, 0)),
        out_shape=jax.ShapeDtypeStruct((DEC_BATCH * DEC_SEQ, 256), F32),
        compiler_params=_cparams(("arbitrary", "arbitrary")),
        name="swa",
    )(sink, proj, proj, proj, kc, vc)


NA_RPB_R = 2 * NA_ROWS - 1
NA_RPB_C = 2 * NA_COLS - 1
GRID_ROWS = DEC_SEQ // GRID_W
NA_RB = 4
NA_UW = 12
NA_NQ = NA_RB * GRID_W
NA_NKEY = NA_UW * GRID_W
NA_NBLK = GRID_ROWS // NA_RB
NA_CASES = ((0, 0), (NA_RB, 0), (GRID_ROWS - NA_RB, GRID_ROWS - NA_UW))


def _na_row_start(qrow):
    return min(max(qrow - NA_ROWS // 2, 0), GRID_ROWS - NA_ROWS)


def _na_bias_kernel(rpb, out):
    h = pl.program_id(0)
    qc = lax.broadcasted_iota(jnp.int32, (GRID_W, GRID_W), 0)
    kc = lax.broadcasted_iota(jnp.int32, (GRID_W, GRID_W), 1)
    dc = jnp.clip(kc - qc, -(NA_COLS - 1), NA_COLS - 1) + NA_COLS - 1
    lo = jnp.clip(qc - NA_COLS // 2, 0, GRID_W - NA_COLS)
    valid = (kc >= lo) & (kc < lo + NA_COLS)
    tiles = []
    for dr in range(NA_RPB_R):
        t = jnp.zeros((GRID_W, GRID_W), F32)
        for j in range(NA_RPB_C):
            t = jnp.where(dc == j, rpb[(h * NA_RPB_R + dr) * NA_RPB_C + j], t)
        tiles.append(jnp.where(valid, t, NEG))
    outside = jnp.full((GRID_W, GRID_W), NEG, F32)
    for case, (q0, k0) in enumerate(NA_CASES):
        for a in range(NA_RB):
            r0 = _na_row_start(q0 + a)
            for i in range(NA_UW):
                inside = r0 <= k0 + i < r0 + NA_ROWS
                tile = tiles[k0 + i - (q0 + a) + NA_ROWS - 1] if inside else outside
                out[0, case, GRID_W * a:GRID_W * (a + 1), GRID_W * i:GRID_W * (i + 1)] = tile


def _na_bias(rpb_flat):
    shape = (NA_HEADS, len(NA_CASES), NA_NQ, NA_NKEY)
    return pl.pallas_call(
        _na_bias_kernel,
        grid=(NA_HEADS,),
        in_specs=[pl.BlockSpec(memory_space=pltpu.SMEM)],
        out_specs=pl.BlockSpec((1,) + shape[1:], lambda h: (h, 0, 0, 0)),
        out_shape=jax.ShapeDtypeStruct(shape, F32),
        compiler_params=_cparams(("arbitrary",)),
        name="na_bias",
    )(rpb_flat)


def _na_kernel(q, k, v, kc, vc, bias, od):
    blk = pl.program_id(1)
    case = jnp.where(blk == 0, 0, jnp.where(blk == NA_NBLK - 1, 2, 1))
    u0 = jnp.clip(blk * NA_RB - NA_ROWS // 2, 0, GRID_ROWS - NA_UW)
    k0 = pl.multiple_of(u0 * GRID_W, GRID_W)
    left = lax.broadcasted_iota(jnp.int32, (NA_NQ, LANES), 1) < HEAD_DIM
    for p in range(NA_HEADS // 2):
        lanes = slice(LANES * p, LANES * (p + 1))
        q2 = q[:, lanes]
        k2 = k[pl.ds(k0, NA_NKEY), lanes].astype(BF16)
        v2 = v[pl.ds(k0, NA_NKEY), lanes].astype(BF16)
        kc2 = kc[0, :, lanes].astype(BF16)
        vc2 = vc[0, :, lanes].astype(BF16)
        res = []
        for i in range(2):
            qm = jnp.where(left if i == 0 else jnp.logical_not(left), q2, 0.0)
            s_loc = _dot_nt(qm, k2) * SCALE + bias[2 * p + i, pl.ds(case, 1)][0]
            s_ctx = _dot_nt(qm, kc2) * SCALE
            res.append(_softmax_pv([s_loc, s_ctx], [v2, vc2], None))
        od[:, lanes] = jnp.where(left, res[0], res[1])


def _na(proj, kc, vc, bias):
    qbase = SEG // NA_NQ
    return pl.pallas_call(
        _na_kernel,
        grid=(DEC_BATCH, NA_NBLK),
        in_specs=[pl.BlockSpec((NA_NQ, 256), lambda b, r: (qbase + b * NA_NBLK + r, C_NQ // 256)),
                  pl.BlockSpec((DEC_SEQ, 256), lambda b, r: (1 + b, C_NK // 256)),
                  pl.BlockSpec((DEC_SEQ, 256), lambda b, r: (1 + b, C_NV // 256)),
                  pl.BlockSpec((1, PAST_LEN, 256), lambda b, r: (b, 0, 0)),
                  pl.BlockSpec((1, PAST_LEN, 256), lambda b, r: (b, 0, 0)),
                  pl.BlockSpec((NA_HEADS, len(NA_CASES), NA_NQ, NA_NKEY), lambda b, r: (0, 0, 0, 0))],
        out_specs=pl.BlockSpec((NA_NQ, 256), lambda b, r: (b * NA_NBLK + r, 0)),
        out_shape=jax.ShapeDtypeStruct((DEC_BATCH * DEC_SEQ, 256), F32),
        compiler_params=_cparams(("arbitrary", "arbitrary")),
        name="na",
    )(proj, proj, proj, kc, vc, bias)


MG_ROWS = 256
MG_SUB = 2
MG_TM = MG_SUB * MG_ROWS


def _merge_kernel(xc_ref, xl_ref, mod_ref, g1_ref, g2_ref, hf_c, hf_l, hb_c, hb_l, mo, ob_c, ob_l, oc_c, oc_l,
                  od_c, od_l, hn, wmg, bmg, wbr, wout, *rest, moe):
    if moe:
        wrt, br, x1_ref, h2_ref, route_ref = rest
    else:
        x1_ref, h2_ref = rest
    ctx = pl.program_id(0) == 0
    mod = mod_ref[0]
    chunk = lambda i: mod[:, i * D_MODEL:(i + 1) * D_MODEL]
    sh1, sc1, gate1, sh2, sc2 = chunk(0), chunk(1), chunk(2), chunk(3), chunk(4)

    def row_group(rows):
        pick = lambda c_ref, l_ref: jnp.where(ctx, c_ref[rows, :], l_ref[rows, :])
        x = pick(xc_ref, xl_ref)
        h = (_rms(x, g1_ref[...]) * (1.0 + sc1) + sh1).astype(BF16)
        hsum = pick(hf_c, hf_l) + pick(hb_c, hb_l)
        out_a = jnp.concatenate(
            [_rms_head_pairs(hsum[:, LANES * p:LANES * (p + 1)], hn[...]) for p in range(2)], axis=-1)
        out_a = out_a * jax.nn.sigmoid(mo[rows, :])
        acc = None
        for n, br_val in enumerate((out_a, pick(ob_c, ob_l), pick(oc_c, oc_l), pick(od_c, od_l))):
            gate = jax.nn.sigmoid(jnp.dot(h, wmg[:, n * D_MODEL:(n + 1) * D_MODEL], preferred_element_type=F32)
                                  + bmg[:, n * D_MODEL:(n + 1) * D_MODEL])
            term = gate * jnp.dot(br_val.astype(BF16), wbr[n], preferred_element_type=F32)
            acc = term if acc is None else acc + term
        y = jnp.dot(acc.astype(BF16), wout[...], preferred_element_type=F32)
        x1 = x + gate1 * y
        x1_ref[rows, :] = x1
        h2 = _rms(x1, g2_ref[...]) * (1.0 + sc2) + sh2
        if moe:
            _store_token_tiles(h2_ref.at[pl.ds(rows.start * TOK_TILE, MG_ROWS * TOK_TILE)], h2)
        else:
            h2_ref[rows, :] = h2.astype(BF16)
        if moe:
            logit = [jnp.sum(h2 * wrt[e:e + 1, :], axis=-1, keepdims=True) + br[e] for e in range(N_EXPERTS)]
            v1, i1 = logit[0], jnp.zeros(logit[0].shape, jnp.int32)
            for e in range(1, N_EXPERTS):
                better = logit[e] > v1
                v1 = jnp.where(better, logit[e], v1)
                i1 = jnp.where(better, e, i1)
            v2, i2 = jnp.full(v1.shape, -jnp.inf, F32), jnp.zeros(v1.shape, jnp.int32)
            for e in range(N_EXPERTS):
                better = (i1 != e) & (logit[e] > v2)
                v2 = jnp.where(better, logit[e], v2)
                i2 = jnp.where(better, e, i2)
            e2 = jnp.exp(v2 - v1)
            den = 1.0 + e2
            lane = lax.broadcasted_iota(jnp.int32, (MG_ROWS, LANES), 1)
            route = jnp.where(lane == 0, 1.0 / den, 0.0) + jnp.where(lane == 1, e2 / den, 0.0)
            route = route + jnp.where(lane == 2, i1.astype(F32), 0.0) + jnp.where(lane == 3, i2.astype(F32), 0.0)
            route_ref[rows, :] = route

    for sub in range(MG_SUB):
        row_group(slice(sub * MG_ROWS, (sub + 1) * MG_ROWS))


def _merge(x, mod_l, g1, g2, hf, hb, proj, ob, oc, od, hn, wmg, bmg, wbr, wout, router=None):
    nt = SEG // MG_TM
    moe = router is not None
    x_pair, lat_row0 = _x_pair(x)
    row = lambda w: pl.BlockSpec((MG_TM, w), lambda s, i: (s * nt + i, 0))
    ctx_blk, lat_blk = _seg_pair_specs(MG_TM, 256)
    full = lambda shape: pl.BlockSpec(shape, lambda s, i: tuple(0 for _ in shape), pipeline_mode=pl.Buffered(1))
    in_specs = [*_seg_pair_specs(MG_TM, D_MODEL, lat_row0),
                pl.BlockSpec((1, 1, 6 * D_MODEL), lambda s, i: (s, 0, 0)),
                full((1, D_MODEL)), full((1, D_MODEL)),
                ctx_blk, lat_blk, ctx_blk, lat_blk,
                pl.BlockSpec((MG_TM, 256), lambda s, i: (s * nt + i, C_MO // 256)),
                ctx_blk, lat_blk, ctx_blk, lat_blk, ctx_blk, lat_blk,
                full((1, LANES)), full((D_MODEL, N_BRANCH * D_MODEL)), full((1, N_BRANCH * D_MODEL)),
                full((N_BRANCH, 256, D_MODEL)), full((D_MODEL, D_MODEL))]
    args = [*x_pair, mod_l, g1, g2, *hf, *hb, proj, *ob, *oc, *od, hn, wmg, bmg, wbr, wout]
    if moe:
        h2_spec = pl.BlockSpec((MG_TM * TOK_TILE, LANES), lambda s, i: (s * nt + i, 0))
        h2_shape = jax.ShapeDtypeStruct((N_TOK * TOK_TILE, LANES), F32)
    else:
        h2_spec, h2_shape = row(D_MODEL), jax.ShapeDtypeStruct((N_TOK, D_MODEL), BF16)
    out_specs = [row(D_MODEL), h2_spec]
    out_shape = [jax.ShapeDtypeStruct((N_TOK, D_MODEL), F32), h2_shape]
    if moe:
        in_specs += [full((N_EXPERTS, D_MODEL)), pl.BlockSpec(memory_space=pltpu.SMEM)]
        args += list(router)
        out_specs.append(row(LANES))
        out_shape.append(jax.ShapeDtypeStruct((N_TOK, LANES), F32))
    return pl.pallas_call(
        functools.partial(_merge_kernel, moe=moe),
        grid=(N_SEG, nt),
        in_specs=in_specs,
        out_specs=out_specs,
        out_shape=out_shape,
        compiler_params=_cparams(("arbitrary", "arbitrary")),
        name="merge",
    )(*args)


FF_TM = 512


def _ffn_kernel(h2, w1, w3, w2, x1, mod_ref, out):
    h = h2[...]
    a = jnp.dot(h, w1[...], preferred_element_type=F32)
    b = jnp.dot(h, w3[...], preferred_element_type=F32)
    act = (jax.nn.silu(a) * b).astype(BF16)
    gate2 = mod_ref[0][:, 5 * D_MODEL:6 * D_MODEL]
    out[...] = x1[...] + gate2 * jnp.dot(act, w2[...], preferred_element_type=F32)


def _ffn(h2, x1, mod_l, w1, w3, w2):
    nt = N_TOK // FF_TM
    per_seg = SEG // FF_TM
    resident = lambda shape: pl.BlockSpec(shape, lambda i: (0, 0), pipeline_mode=pl.Buffered(1))
    return pl.pallas_call(
        _ffn_kernel,
        grid=(nt,),
        in_specs=[pl.BlockSpec((FF_TM, D_MODEL), lambda i: (i, 0)),
                  resident((D_MODEL, D_FF)), resident((D_MODEL, D_FF)), resident((D_FF, D_MODEL)),
                  pl.BlockSpec((FF_TM, D_MODEL), lambda i: (i, 0)),
                  pl.BlockSpec((1, 1, 6 * D_MODEL), lambda i: (i // per_seg, 0, 0))],
        out_specs=pl.BlockSpec((FF_TM, D_MODEL), lambda i: (i, 0)),
        out_shape=jax.ShapeDtypeStruct((N_TOK, D_MODEL), F32),
        compiler_params=_cparams(("arbitrary",)),
        name="ffn",
    )(h2, w1, w3, w2, x1, mod_l)


MOE_TM = 256
MOE_SLOTS = 2 * N_TOK
MOE_TILES = MOE_SLOTS // MOE_TM + N_EXPERTS
MOE_NBUF = 3
MOE_STEPS = MOE_TILES + MOE_NBUF
MOE_DUMP = MOE_NBUF * MOE_TM
MOE_LEAD = 1
MOE_PLAN_TILES = MOE_LEAD + MOE_TILES + 2
MOE_FCHUNKS = 1
MOE_UNROLL = 8


def _moe_group_kernel(texp, nused, src_tok, dst_row, h2_hbm, w1, w3, w2, y_hbm, xs, ys, sem_in, sem_out):
    del texp
    i = pl.program_id(0)
    n_used = nused[0]
    buf = i % MOE_NBUF
    buf_next = (i + 2) % MOE_NBUF

    def tile_rows(t):
        start = t * TOK_TILE
        return pl.ds(start if isinstance(start, int) else pl.multiple_of(start, TOK_TILE), TOK_TILE)

    def gather_copy(tile, b, r):
        tok = src_tok[(tile + MOE_LEAD) * MOE_TM + r]
        return pltpu.make_async_copy(h2_hbm.at[tile_rows(tok)], xs.at[b, tile_rows(r)], sem_in.at[b])

    def scatter_copy(tile, b, r):
        dst = dst_row[(tile + MOE_LEAD) * MOE_TM + r]
        return pltpu.make_async_copy(ys.at[b, tile_rows(r)], y_hbm.at[tile_rows(dst)], sem_out.at[b])

    def start_rows_loop(make_copy, tile, b):
        def body(r, carry):
            make_copy(tile, b, r).start()
            return carry
        lax.fori_loop(0, MOE_TM, body, 0, unroll=MOE_UNROLL)

    def wait_tile(b, gather):
        if gather:
            pltpu.make_async_copy(h2_hbm.at[pl.ds(0, MOE_TM * TOK_TILE)], xs.at[b], sem_in.at[b]).wait()
        else:
            pltpu.make_async_copy(ys.at[b], y_hbm.at[pl.ds(0, MOE_TM * TOK_TILE)], sem_out.at[b]).wait()

    @pl.when(i == 0)
    def _():
        xs[...] = jnp.zeros(xs.shape, F32)
        ys[...] = jnp.zeros(ys.shape, F32)
        for b in range(MOE_NBUF):
            fill = pltpu.make_async_copy(ys.at[b], y_hbm.at[pl.ds((MOE_SLOTS + b * MOE_TM) * TOK_TILE, MOE_TM * TOK_TILE)],
                                         sem_out.at[b])
            fill.start()
            fill.wait()
        start_rows_loop(gather_copy, 0, 0)
        start_rows_loop(gather_copy, 1, 1)

    @pl.when(i <= n_used + 1)
    def _():
        wait_tile(buf, True)

    @pl.when((i >= 2) & (i <= n_used + 2))
    def _():
        wait_tile(buf, False)

    @pl.when(i < n_used)
    def _():
        x = _load_token_tiles(xs.at[buf], MOE_TM).astype(BF16)
        fc = D_FF_EXPERT // MOE_FCHUNKS
        rc = MOE_TM // MOE_FCHUNKS
        y = None
        for c in range(MOE_FCHUNKS):
            for r in range(c * rc, (c + 1) * rc):
                gather_copy(i + 2, buf_next, r).start()
                scatter_copy(i - 1, buf_next, r).start()
            a = jnp.dot(x, w1[0, :, c * fc:(c + 1) * fc].astype(BF16), preferred_element_type=F32)
            b = jnp.dot(x, w3[0, :, c * fc:(c + 1) * fc].astype(BF16), preferred_element_type=F32)
            act = (jax.nn.silu(a) * b).astype(BF16)
            part = jnp.dot(act, w2[0, c * fc:(c + 1) * fc, :], preferred_element_type=F32)
            y = part if y is None else y + part
        _store_token_tiles(ys.at[buf], y)

    @pl.when(i == n_used)
    def _():
        start_rows_loop(scatter_copy, i - 1, buf_next)


def _moe_group(tile_expert, n_used, src_tok, dst_row, h2, w1, w3, w2):
    wspec = lambda shape: pl.BlockSpec((1,) + shape, lambda i, texp, *_: (texp[jnp.minimum(i, MOE_TILES - 1)], 0, 0))
    grid_spec = pltpu.PrefetchScalarGridSpec(
        num_scalar_prefetch=4,
        grid=(MOE_STEPS,),
        in_specs=[pl.BlockSpec(memory_space=pl.ANY),
                  wspec((D_MODEL, D_FF_EXPERT)), wspec((D_MODEL, D_FF_EXPERT)), wspec((D_FF_EXPERT, D_MODEL))],
        out_specs=pl.BlockSpec(memory_space=pl.ANY),
        scratch_shapes=[pltpu.VMEM((MOE_NBUF, MOE_TM * TOK_TILE, LANES), F32),
                        pltpu.VMEM((MOE_NBUF, MOE_TM * TOK_TILE, LANES), F32),
                        pltpu.SemaphoreType.DMA((MOE_NBUF,)), pltpu.SemaphoreType.DMA((MOE_NBUF,))])
    return pl.pallas_call(
        _moe_group_kernel,
        grid_spec=grid_spec,
        out_shape=jax.ShapeDtypeStruct(((MOE_SLOTS + MOE_DUMP) * TOK_TILE, LANES), F32),
        compiler_params=_cparams(("arbitrary",)),
        name="moe_group",
    )(tile_expert, n_used, src_tok, dst_row, h2, w1, w3, w2)


def _moe_plan(expert_ids):
    e_flat = expert_ids.T.reshape(-1)
    order = jnp.argsort(e_flat, stable=True).astype(jnp.int32)
    counts = jnp.sum((e_flat[:, None] == jnp.arange(N_EXPERTS)[None, :]).astype(jnp.int32), axis=0)
    padded = (counts + MOE_TM - 1) // MOE_TM * MOE_TM
    pend = jnp.cumsum(padded)
    pstart = pend - padded
    ustart = jnp.cumsum(counts) - counts
    n_used = pend[-1] // MOE_TM
    tiles = jnp.arange(MOE_TILES, dtype=jnp.int32)
    last_used = jnp.minimum(tiles, n_used - 1)
    tile_expert = jnp.sum((last_used[:, None] * MOE_TM >= pend[None, :]).astype(jnp.int32), axis=1)
    t = jnp.arange(-MOE_LEAD, MOE_PLAN_TILES - MOE_LEAD, dtype=jnp.int32)[:, None]
    r = jnp.arange(MOE_TM, dtype=jnp.int32)[None, :]
    e_t = tile_expert[jnp.clip(t, 0, MOE_TILES - 1)]
    off = t * MOE_TM + r - pstart[e_t]
    valid = (t >= 0) & (t < n_used) & (off < counts[e_t])
    slot = order[jnp.clip(ustart[e_t] + off, 0, MOE_SLOTS - 1)]
    src_tok = jnp.where(valid, slot % N_TOK, 0)
    dst_row = jnp.where(valid, slot, MOE_SLOTS + (t % MOE_NBUF) * MOE_TM + r)
    return (tile_expert.astype(jnp.int32), n_used.reshape(1).astype(jnp.int32),
            src_tok.reshape(-1).astype(jnp.int32), dst_row.reshape(-1).astype(jnp.int32))


def _moe_combine_kernel(x1, y0, y1, route, mod_ref, out_c, out_l):
    gate2 = mod_ref[0][:, 5 * D_MODEL:6 * D_MODEL]
    r = route[...]
    val = x1[...] + gate2 * (r[:, 0:1] * _load_token_tiles(y0, FF_TM) + r[:, 1:2] * _load_token_tiles(y1, FF_TM))
    is_ctx = pl.program_id(0) < SEG // FF_TM

    @pl.when(is_ctx)
    def _():
        out_c[...] = val

    @pl.when(jnp.logical_not(is_ctx))
    def _():
        out_l[...] = val


def _moe_combine(x1, y_slots, route, mod_l):
    nt = N_TOK // FF_TM
    per_seg = SEG // FF_TM
    return pl.pallas_call(
        _moe_combine_kernel,
        grid=(nt,),
        in_specs=[pl.BlockSpec((FF_TM, D_MODEL), lambda i: (i, 0)),
                  pl.BlockSpec((FF_TM * TOK_TILE, LANES), lambda i: (i, 0)),
                  pl.BlockSpec((FF_TM * TOK_TILE, LANES), lambda i: (nt + i, 0)),
                  pl.BlockSpec((FF_TM, LANES), lambda i: (i, 0)),
                  pl.BlockSpec((1, 1, 6 * D_MODEL), lambda i: (i // per_seg, 0, 0))],
        out_specs=[pl.BlockSpec((FF_TM, D_MODEL), lambda i: (jnp.minimum(i, per_seg - 1), 0)),
                   pl.BlockSpec((FF_TM, D_MODEL), lambda i: (jnp.maximum(i - per_seg, 0), 0))],
        out_shape=[jax.ShapeDtypeStruct((SEG, D_MODEL), F32), jax.ShapeDtypeStruct((N_TOK - SEG, D_MODEL), F32)],
        compiler_params=_cparams(("arbitrary",)),
        name="moe_combine",
    )(x1, y_slots, y_slots, route, mod_l)


def _rope_tables():
    t = np.arange(DEC_SEQ)
    row, col = (t // GRID_W).astype(np.float32), (t % GRID_W).astype(np.float32)
    nf = HEAD_DIM // 4
    freqs = np.float32(ROPE_BASE) ** (-np.arange(nf, dtype=np.float32) / np.float32(nf))
    lane = np.arange(LANES) % HEAD_DIM
    fidx = lane % nf
    use_col = (lane // (HEAD_DIM // 2)) == 1
    first = (lane % (HEAD_DIM // 2)) < nf
    pos = np.where(use_col[None, :], col[:, None], row[:, None])
    ang = (pos * freqs[fidx][None, :]).astype(np.float32).astype(np.float64)
    sin = np.sin(ang)
    return (jnp.asarray(np.cos(ang), dtype=F32), jnp.asarray(np.where(first[None, :], -sin, sin), dtype=F32))


def _permute_w_in(w):
    sizes = (256, 256, 256, 256, 8, 8, 256, 128, 128, 256, 256, 256, 256, 256)
    offs = np.concatenate([[0], np.cumsum(sizes)])
    part = lambda i: w[:, offs[i]:offs[i + 1]]
    mq, mk, mv, mo, mi, mf, sq, sk, sv, rx, ry, nq, nk, nv = (part(i) for i in range(14))
    pad = jnp.zeros((w.shape[0], LANES - 16), w.dtype)
    return jnp.concatenate([mq, mk, mv, mo, sq, rx, ry, nq, nk, nv, sk, sv, mi, mf, pad], axis=1)


def _block_diag(w):
    eye = jnp.eye(RG_BLOCKS, dtype=w.dtype)
    return (w[:, :, None, :] * eye[:, None, :, None]).reshape(RG_WIDTH, RG_WIDTH)


def _tile2(g):
    return jnp.concatenate([g, g]).reshape(1, LANES)


def kernel(x_prompt, x_sample, cache_swa_k, cache_swa_v, cache_na_k, cache_na_v, state_mlstm_C, state_mlstm_n, state_mlstm_m, state_rglru_h, c, c_ctx, norm1_g, norm2_g, w_ada, b_ada, w_in, ml_b_i, ml_b_f, ml_hn, sw_qn, sw_kn, sw_sink, rg_conv_w, rg_conv_b, rg_w_r, rg_b_r, rg_w_i, rg_b_i, rg_lam, na_qn, na_kn, na_rpb, w_br, w_mg, b_mg, w_out, ffn_w1, ffn_w3, ffn_w2, moe_wr, moe_br, moe_w1, moe_w3, moe_w2):
    assert DEPTH % 2 == 0
    x_all = (x_prompt.reshape(SEG, D_MODEL), x_sample.reshape(N_TOK - SEG, D_MODEL))
    cvecs = jnp.concatenate([c_ctx[None, :], c, jnp.zeros((8 - 1 - DEC_BATCH, D_MODEL), F32)], axis=0)
    mod = _mod_table(cvecs.T, w_ada, b_ada)
    cos_t, sin_t = _rope_tables()
    nj = 2 * ML_HEADS
    zeros_state = (jnp.zeros((BATCH, nj // 2, LANES, LANES), F32), jnp.zeros((BATCH, nj // 2, LANES, LANES), F32),
                   jnp.zeros((BATCH, nj, LANES), F32), jnp.zeros((BATCH // RG_NSEG, 2, RG_NSEG, RG_WIDTH), F32))
    ctx_out = []
    for l in range(DEPTH):
        mod_l = mod[l].reshape(8, 1, 6 * D_MODEL)
        qk_gains = jnp.stack([_tile2(sw_qn[l])[0], _tile2(sw_kn[l])[0], _tile2(na_qn[l])[0], _tile2(na_kn[l])[0]])
        proj = _inproj(x_all, mod_l, norm1_g[l].reshape(1, D_MODEL), _permute_w_in(w_in[l]).astype(BF16),
                       qk_gains, cos_t, sin_t)
        gate_bias = jnp.concatenate([ml_b_i[l].reshape(-1), ml_b_f[l].reshape(-1),
                                     jnp.zeros((LANES - 2 * nj,), F32)]).reshape(1, LANES)
        hf_c, hb_c, *st_new = _mlstm(proj, gate_bias, *zeros_state[:3], bsz=BATCH, seq=SEQ, row0=0)
        c_new, n_new, m_new = _mlstm_unpack_state(*st_new)
        st_lat = _mlstm_pack_state(state_mlstm_C[:, l].reshape(DEC_BATCH, nj, HEAD_DIM, HEAD_DIM),
                                   state_mlstm_n[:, l].reshape(DEC_BATCH, nj, HEAD_DIM),
                                   state_mlstm_m[:, l].reshape(DEC_BATCH, nj))
        hf_l, hb_l, _, _, _ = _mlstm(proj, gate_bias, *st_lat, bsz=DEC_BATCH, seq=DEC_SEQ, row0=SEG)
        wg = jnp.concatenate([_block_diag(rg_w_r[l, 0]), _block_diag(rg_w_i[l, 0]),
                              _block_diag(rg_w_r[l, 1]), _block_diag(rg_w_i[l, 1])], axis=1).astype(BF16)
        bg = jnp.concatenate([rg_b_r[l, 0], rg_b_i[l, 0], rg_b_r[l, 1], rg_b_i[l, 1]]).reshape(1, 4 * RG_WIDTH)
        rg_args = (rg_conv_w[l], rg_conv_b[l].reshape(1, RG_WIDTH), wg, bg, rg_lam[l])
        oc_c, hl_c = _rglru(proj, *rg_args, zeros_state[3], nblk=BATCH // RG_NSEG, seg_len=SEQ, chained=False, row0=0)
        hl_new = jnp.transpose(hl_c, (0, 2, 1, 3)).reshape(BATCH, 2, RG_WIDTH)
        h0_lat = jnp.broadcast_to(state_rglru_h[:, l][:, :, None, :], (DEC_BATCH, 2, RG_NSEG, RG_WIDTH))
        oc_l, _ = _rglru(proj, *rg_args, h0_lat, nblk=DEC_BATCH, seg_len=DEC_SEQ // RG_NSEG, chained=True, row0=SEG)
        ob_c, od_c = _ctx_attn(proj, sw_sink[l])
        ob_l = _swa(proj, cache_swa_k[:, l].reshape(DEC_BATCH, PAST_LEN, 128),
                    cache_swa_v[:, l].reshape(DEC_BATCH, PAST_LEN, 128), sw_sink[l])
        od_l = _na(proj, cache_na_k[:, l].reshape(DEC_BATCH, PAST_LEN, 256),
                   cache_na_v[:, l].reshape(DEC_BATCH, PAST_LEN, 256), _na_bias(na_rpb[l].reshape(-1)))
        moe_layer = l % 2 == 1
        j = l // 2
        router = (moe_wr[j].T, moe_br[j]) if moe_layer else None
        outs = _merge(x_all, mod_l, norm1_g[l].reshape(1, D_MODEL), norm2_g[l].reshape(1, D_MODEL),
                      (hf_c, hf_l), (hb_c, hb_l), proj, (ob_c, ob_l), (oc_c, oc_l), (od_c, od_l),
                      _tile2(ml_hn[l]), w_mg[l].astype(BF16), b_mg[l].reshape(1, -1), w_br[l].astype(BF16),
                      w_out[l].astype(BF16), router)
        if moe_layer:
            x1, h2, route = outs
            plan = _moe_plan(route[:, 2:4].astype(jnp.int32))
            y_slots = _moe_group(*plan, h2, moe_w1[j], moe_w3[j], moe_w2[j].astype(BF16))
            x_all = tuple(_moe_combine(x1, y_slots, route, mod_l))
        else:
            x1, h2 = outs
            x_all = _ffn(h2, x1, mod_l, ffn_w1[j].astype(BF16), ffn_w3[j].astype(BF16), ffn_w2[j].astype(BF16))
        pc = proj[:SEG]
        ctx_out.append(dict(
            sw_k=pc[:, C_SK:C_SK + 128].reshape(BATCH, SEQ, SW_KV_HEADS, HEAD_DIM),
            sw_v=pc[:, C_SV:C_SV + 128].reshape(BATCH, SEQ, SW_KV_HEADS, HEAD_DIM),
            na_k=pc[:, C_NK:C_NK + 256].reshape(BATCH, SEQ, NA_HEADS, HEAD_DIM),
            na_v=pc[:, C_NV:C_NV + 256].reshape(BATCH, SEQ, NA_HEADS, HEAD_DIM),
            ml_C=c_new.reshape(BATCH, 2, ML_HEADS, HEAD_DIM, HEAD_DIM),
            ml_n=n_new.reshape(BATCH, 2, ML_HEADS, HEAD_DIM),
            ml_m=m_new.reshape(BATCH, 2, ML_HEADS),
            rg_h=hl_new))
    stack = lambda name: jnp.stack([t[name] for t in ctx_out], axis=1)
    return (x_all[0].reshape(BATCH, SEQ, D_MODEL), x_all[1].reshape(DEC_BATCH, DEC_SEQ, D_MODEL),
            stack('sw_k'), stack('sw_v'), stack('na_k'), stack('na_v'),
            stack('ml_C'), stack('ml_n'), stack('ml_m'), stack('rg_h'))
```

```python
import functools

import numpy as np
import jax
import jax.numpy as jnp
from jax import lax
from jax.experimental import pallas as pl
from jax.experimental.pallas import tpu as pltpu

F32 = jnp.float32
BF16 = jnp.bfloat16

D_MODEL = 1024
BATCH = 16
SEQ = 256
DEPTH = 2
DEC_BATCH = 2
DEC_SEQ = 4096
PAST_LEN = 256
GRID_W = 64
HEAD_DIM = 64
ML_HEADS = 4
ML_CHUNK = 128
ML_SUB = 2
SW_HEADS = 4
SW_KV_HEADS = 2
SW_WINDOW = 128
RG_WIDTH = 256
RG_BLOCKS = 4
RG_CONV = 4
RG_C = 8.0
NA_HEADS = 4
NA_ROWS = 8
NA_COLS = 16
N_BRANCH = 4
ROPE_BASE = 10000.0
D_FF = 2816
N_EXPERTS = 8
D_FF_EXPERT = 2048
EPS = 1e-6
NEG = -1e30
SCALE = HEAD_DIM ** -0.5

SEG = 4096
N_SEG = 3
N_TOK = N_SEG * SEG
LANES = 128
VMEM_LIMIT = 60 * 1024 * 1024

C_MQ, C_MK, C_MV, C_MO = 0, 256, 512, 768
C_SQ, C_RX, C_RY, C_NQ, C_NK, C_NV = 1024, 1280, 1536, 1792, 2048, 2304
C_SK, C_SV, C_G = 2560, 2688, 2816
P_W = 2944


def _cparams(sem):
    return pltpu.CompilerParams(dimension_semantics=sem, vmem_limit_bytes=VMEM_LIMIT)


def _dot(a, b):
    return jnp.dot(a.astype(BF16), b.astype(BF16), preferred_element_type=F32)


def _dot_nt(a, b):
    return lax.dot_general(a.astype(BF16), b.astype(BF16), (((1,), (1,)), ((), ())),
                           preferred_element_type=F32)


def _dot_tn(a, b):
    return lax.dot_general(a.astype(BF16), b.astype(BF16), (((0,), (0,)), ((), ())),
                           preferred_element_type=F32)


def _split3(x):
    hi = x.astype(BF16)
    r1 = x - hi.astype(F32)
    mid = r1.astype(BF16)
    lo = (r1 - mid.astype(F32)).astype(BF16)
    return hi, mid, lo


def _dot_exact_rhs(a01, x):
    hi, mid, lo = _split3(x)
    d = lambda p: jnp.dot(a01, p, preferred_element_type=F32)
    return d(hi) + d(mid) + d(lo)


def _dot_exact_lhs(x, a01):
    hi, mid, lo = _split3(x)
    d = lambda p: jnp.dot(p, a01, preferred_element_type=F32)
    return d(hi) + d(mid) + d(lo)


def _sigmoid(x):
    return 0.5 * jnp.tanh(0.5 * x) + 0.5


def _rms(x, g):
    return x * lax.rsqrt(jnp.mean(x * x, axis=-1, keepdims=True) + EPS) * g


def _rms_head_pairs(x, g):
    lane = lax.broadcasted_iota(jnp.int32, x.shape, 1)
    left = lane < HEAD_DIM
    sq = x * x
    s0 = jnp.sum(jnp.where(left, sq, 0.0), axis=-1, keepdims=True)
    s1 = jnp.sum(jnp.where(left, 0.0, sq), axis=-1, keepdims=True)
    ms = jnp.where(left, s0, s1) * (1.0 / HEAD_DIM)
    return x * lax.rsqrt(ms + EPS) * g


MOD_TN = 1536
MOD_ROWS = 3


def _mod_kernel(ct_ref, w_ref, b_ref, o_ref):
    ct = ct_ref[...]
    st = ct * jax.nn.sigmoid(ct)
    w = w_ref[0]
    o_ref[...] = jnp.zeros(o_ref.shape, F32)
    for r in range(MOD_ROWS):
        o_ref[0, r:r + 1, :] = jnp.sum(w * st[:, r:r + 1], axis=0, keepdims=True) + b_ref[0]


def _mod_table(cvecs_t, w_ada, b_ada):
    n = 6 * D_MODEL
    return pl.pallas_call(
        _mod_kernel,
        grid=(DEPTH, n // MOD_TN),
        in_specs=[pl.BlockSpec((D_MODEL, 8), lambda l, j: (0, 0)),
                  pl.BlockSpec((1, D_MODEL, MOD_TN), lambda l, j: (l, 0, j)),
                  pl.BlockSpec((1, 1, MOD_TN), lambda l, j: (l, 0, j))],
        out_specs=pl.BlockSpec((1, 8, MOD_TN), lambda l, j: (l, 0, j)),
        out_shape=jax.ShapeDtypeStruct((DEPTH, 8, n), F32),
        compiler_params=_cparams(("arbitrary", "arbitrary")),
        name="adaln_mod",
    )(cvecs_t, w_ada, b_ada.reshape(DEPTH, 1, n))


IN_TM = 512


def _swap16(y):
    lane = lax.broadcasted_iota(jnp.int32, y.shape, 1)
    first = (lane % 32) < 16
    return jnp.where(first, pltpu.roll(y, LANES - 16, 1), pltpu.roll(y, 16, 1))


def _seg_pair_specs(tm, width, lat_row0=0):
    nt = SEG // tm
    lat_off = lat_row0 // tm
    return (pl.BlockSpec((tm, width), lambda s, i: (jnp.minimum(s * nt + i, nt - 1), 0)),
            pl.BlockSpec((tm, width), lambda s, i: (lat_off + jnp.maximum(s * nt + i - nt, 0), 0)))


TOK_TILE = D_MODEL // LANES


def _store_token_tiles(ref, x):
    for s in range(TOK_TILE):
        ref[pl.ds(s, x.shape[0], stride=TOK_TILE), :] = x[:, LANES * s:LANES * (s + 1)]


def _load_token_tiles(ref, n_tok):
    return jnp.concatenate([ref[pl.ds(s, n_tok, stride=TOK_TILE), :] for s in range(TOK_TILE)], axis=1)


def _pick(c_ref, l_ref):
    return jnp.where(pl.program_id(0) == 0, c_ref[...], l_ref[...])


def _x_pair(x):
    return (x, 0) if isinstance(x, tuple) else ((x, x), SEG)


def _inproj_kernel(xc_ref, xl_ref, mod_ref, g_ref, w_ref, qkg_ref, cos_ref, sin_ref, o_ref):
    seg = pl.program_id(0)
    mod = mod_ref[0]
    sh1 = mod[:, 0:D_MODEL]
    sc1 = mod[:, D_MODEL:2 * D_MODEL]
    h = _rms(_pick(xc_ref, xl_ref), g_ref[...]) * (1.0 + sc1) + sh1
    r = jnp.dot(h.astype(BF16), w_ref[...], preferred_element_type=F32)
    o_ref[:, 0:C_SQ] = r[:, 0:C_SQ]
    o_ref[:, C_RX:C_NQ] = r[:, C_RX:C_NQ]
    o_ref[:, C_NV:C_SK] = r[:, C_NV:C_SK]
    o_ref[:, C_SV:P_W] = r[:, C_SV:P_W]
    cos = cos_ref[...]
    sin = sin_ref[...]
    latent = seg > 0

    def rope(y):
        return jnp.where(latent, y * cos + _swap16(y) * sin, y)

    for p in range(2):
        a = C_SQ + LANES * p
        o_ref[:, a:a + LANES] = rope(_rms_head_pairs(r[:, a:a + LANES], qkg_ref[0:1, :]))
    o_ref[:, C_SK:C_SK + LANES] = rope(_rms_head_pairs(r[:, C_SK:C_SK + LANES], qkg_ref[1:2, :]))
    for p in range(2):
        a = C_NQ + LANES * p
        o_ref[:, a:a + LANES] = _rms_head_pairs(r[:, a:a + LANES], qkg_ref[2:3, :])
        a = C_NK + LANES * p
        o_ref[:, a:a + LANES] = _rms_head_pairs(r[:, a:a + LANES], qkg_ref[3:4, :])


def _inproj(x, mod_l, norm1, w_in_p, qk_gains, cos_t, sin_t):
    nt = SEG // IN_TM
    x_pair, lat_row0 = _x_pair(x)
    return pl.pallas_call(
        _inproj_kernel,
        grid=(N_SEG, nt),
        in_specs=[*_seg_pair_specs(IN_TM, D_MODEL, lat_row0),
                  pl.BlockSpec((1, 1, 6 * D_MODEL), lambda s, i: (s, 0, 0)),
                  pl.BlockSpec((1, D_MODEL), lambda s, i: (0, 0)),
                  pl.BlockSpec((D_MODEL, P_W), lambda s, i: (0, 0)),
                  pl.BlockSpec((4, LANES), lambda s, i: (0, 0)),
                  pl.BlockSpec((IN_TM, LANES), lambda s, i: (i, 0)),
                  pl.BlockSpec((IN_TM, LANES), lambda s, i: (i, 0))],
        out_specs=pl.BlockSpec((IN_TM, P_W), lambda s, i: (s * nt + i, 0)),
        out_shape=jax.ShapeDtypeStruct((N_TOK, P_W), F32),
        compiler_params=_cparams(("arbitrary", "arbitrary")),
        name="inproj",
    )(*x_pair, mod_l, norm1, w_in_p, qk_gains, cos_t, sin_t)


def _mlstm_direction(d, rows, q_ref, k_ref, v_ref, g_ref, bias, tri_ref, sel_ref, state):
    ch = ML_CHUNK
    r_io = lax.broadcasted_iota(jnp.int32, (ch, ch), 0)
    c_io = lax.broadcasted_iota(jnp.int32, (ch, ch), 1)
    lower = r_io >= c_io
    upper = r_io <= c_io
    mask = lower if d == 0 else upper
    tri = tri_ref[d]
    tri_t = tri_ref[1 - d]
    left = c_io < HEAD_DIM
    top = r_io < HEAD_DIM
    blockdiag = top == left
    cbs_in, nbs_in, m_old = state
    g = g_ref[rows, :] + bias[...]
    b_cols = _dot_exact_rhs(tri, jax.nn.log_sigmoid(g))
    b3 = jnp.concatenate(_split3(b_cols), axis=1)
    gt = g.T
    li_rows = gt[0:8, :]
    b_rows = _dot_exact_lhs(jax.nn.log_sigmoid(gt[8:16, :]), tri_t)
    a_rows = li_rows - b_rows
    bl = b_rows[:, ch - 1:ch] if d == 0 else b_rows[:, 0:1]
    g_rows = bl - b_rows + li_rows
    m_new = jnp.maximum(bl + m_old, jnp.max(g_rows, axis=1, keepdims=True))
    wk_rows = jnp.exp(g_rows - m_new)
    wp = jnp.exp(bl + m_old - m_new)
    ones_blk = jnp.ones((ch, LANES), BF16)
    left2 = lax.broadcasted_iota(jnp.int32, (ch, 2 * LANES), 1) % LANES < HEAD_DIM
    h_out, c_out, n_out = [], [], []
    for p in range(ML_HEADS // 2):
        lanes = slice(LANES * p, LANES * (p + 1))
        j0 = ML_HEADS * d + 2 * p
        q2 = q_ref[rows, lanes]
        k2t = (k_ref[rows, lanes] * SCALE).T.astype(BF16)
        v2e = jnp.concatenate([v_ref[rows, lanes].astype(BF16), ones_blk], axis=1)
        cb = cbs_in[p]
        nb = nbs_in[p]
        q2b = q2.astype(BF16)
        q_lo = (q2 - q2b.astype(F32)).astype(BF16)
        nb_hi = nb.astype(BF16)
        nb_lo = (nb - nb_hi.astype(F32)).astype(BF16)
        qc = jnp.dot(q2b, cb.astype(BF16), preferred_element_type=F32)
        qn = jnp.dot(jnp.concatenate([q2b, q_lo, q2b], axis=1), jnp.concatenate([nb_hi, nb_hi, nb_lo], axis=0),
                     preferred_element_type=F32)
        b_pair = jnp.dot(b3, sel_ref[2 * d + p], preferred_element_type=F32)
        cbs, sves = [], []
        for i in range(2):
            j = j0 + i
            half = left if i == 0 else jnp.logical_not(left)
            a_mat = jnp.where(mask, a_rows[j:j + 1, :], NEG)
            cvec = jnp.maximum(m_old[j:j + 1, :], jnp.max(a_mat, axis=1, keepdims=True))
            cbro = jnp.broadcast_to(cvec, (ch, ch))
            s = jnp.dot(jnp.where(half, q2b, 0), k2t, preferred_element_type=F32) * jnp.exp(a_mat - cbro)
            s_hi = s.astype(BF16)
            s_lo = (s - s_hi.astype(F32)).astype(BF16)
            sve = jnp.dot(s_hi, v2e, preferred_element_type=F32)
            rs_lo = jnp.dot(s_lo, ones_blk, preferred_element_type=F32)
            sves.append(jnp.concatenate([sve[:, 0:LANES], sve[:, LANES:2 * LANES] + rs_lo], axis=1))
            cbs.append(cbro)
        c_pair = jnp.where(left, cbs[0], cbs[1])
        w_prev = jnp.exp(jnp.where(left, m_old[j0:j0 + 1, :], m_old[j0 + 1:j0 + 2, :]) - c_pair)
        sve = jnp.where(left2, sves[0], sves[1])
        num = w_prev * qc + sve[:, 0:LANES]
        den = w_prev * qn + sve[:, LANES:2 * LANES]
        h_out.append(num / jnp.maximum(jnp.abs(den), jnp.exp(-(c_pair + b_pair))))
        kwt = k2t * jnp.where(top, wk_rows[j0:j0 + 1, :], wk_rows[j0 + 1:j0 + 2, :])
        kwt_hi = kwt.astype(BF16)
        kwt_lo = (kwt - kwt_hi.astype(F32)).astype(BF16)
        kve = jnp.dot(kwt_hi, v2e, preferred_element_type=F32)
        kn = kve[:, LANES:2 * LANES] + jnp.dot(kwt_lo, ones_blk, preferred_element_type=F32)
        wp_pair = jnp.where(top, wp[j0:j0 + 1, :], wp[j0 + 1:j0 + 2, :])
        c_out.append(wp_pair * cb + jnp.where(blockdiag, kve[:, 0:LANES], 0.0))
        n_out.append(wp_pair * nb + jnp.where(blockdiag, kn, 0.0))
    return h_out, (c_out, n_out, m_new)


def _mlstm_kernel(qf, kf, vf, gf, qb, kb, vb, gb, c0, n0, m0, bias, tri_ref, sel_ref,
                  hf, hb, co, no, mo, cbd, nbd, m_s, *, nc):
    c = pl.program_id(1)

    @pl.when(c == 0)
    def _():
        cbd[...] = c0[0]
        nbd[...] = n0[0]
        m_s[...] = m0[0]

    npair = ML_HEADS // 2
    for d, (refs, h_ref) in enumerate((((qf, kf, vf, gf), hf), ((qb, kb, vb, gb), hb))):
        state = ([cbd[npair * d + p] for p in range(npair)], [nbd[npair * d + p] for p in range(npair)],
                 m_s[:, 0:1])
        subs = range(ML_SUB) if d == 0 else reversed(range(ML_SUB))
        for sub in subs:
            rows = slice(sub * ML_CHUNK, (sub + 1) * ML_CHUNK)
            h_out, state = _mlstm_direction(d, rows, *refs, bias, tri_ref, sel_ref, state)
            for p in range(npair):
                h_ref[rows, LANES * p:LANES * (p + 1)] = h_out[p]
        c_out, n_out, m_new = state
        for p in range(npair):
            cbd[npair * d + p] = c_out[p]
            nbd[npair * d + p] = n_out[p]
        heads = slice(ML_HEADS * d, ML_HEADS * (d + 1))
        m_s[heads, :] = jnp.broadcast_to(m_new[heads, :], (ML_HEADS, LANES))

    @pl.when(c == nc - 1)
    def _():
        co[0] = cbd[...]
        no[0] = nbd[...]
        mo[0] = m_s[...]


def _mlstm_pack_state(c0, n0, m0):
    bsz = c0.shape[0]
    hd = HEAD_DIM
    eye = jnp.eye(2, dtype=F32)[None, None, :, None, :, None]
    cbd = c0.reshape(bsz, ML_HEADS, 2, hd, 1, hd) * eye
    nbd = jnp.broadcast_to(n0.reshape(bsz, ML_HEADS, 2, hd, 1, 1) * eye, cbd.shape)
    to_mat = lambda t: t.reshape(bsz, ML_HEADS, LANES, LANES)
    return to_mat(cbd), to_mat(nbd), jnp.broadcast_to(m0[..., None], m0.shape + (LANES,))


def _mlstm_unpack_state(cbd, nbd, mrow):
    hd = HEAD_DIM
    bsz = cbd.shape[0]
    c = jnp.stack([cbd[:, :, :hd, :hd], cbd[:, :, hd:, hd:]], axis=2).reshape(bsz, 2 * ML_HEADS, hd, hd)
    n = jnp.stack([nbd[:, :, :hd, 0], nbd[:, :, hd:, hd]], axis=2).reshape(bsz, 2 * ML_HEADS, hd)
    return c, n, mrow[:, :, 0]


def _mlstm_tables():
    r = np.arange(ML_CHUNK)
    lower = (r[:, None] >= r[None, :]).astype(np.float32)
    row = np.arange(3 * LANES)[:, None] % LANES
    lane_left = np.arange(LANES)[None, :] < HEAD_DIM
    sel = [row == np.where(lane_left, 8 + ML_HEADS * d + 2 * p, 9 + ML_HEADS * d + 2 * p)
           for d in range(2) for p in range(ML_HEADS // 2)]
    return (jnp.asarray(np.stack([lower, lower.T]), dtype=BF16),
            jnp.asarray(np.stack(sel).astype(np.float32), dtype=BF16))


def _mlstm(proj, gate_bias, c0, n0, m0, *, bsz, seq, row0):
    blk = ML_SUB * ML_CHUNK
    nc = seq // blk
    tri, sel = _mlstm_tables()
    base = row0 // blk
    nj = 2 * ML_HEADS
    fw = lambda col: (lambda b, c: (base + b * nc + c, col))
    bw = lambda col: (lambda b, c: (base + b * nc + nc - 1 - c, col))
    qkv = lambda f: [pl.BlockSpec((blk, 256), f(C_MQ // 256)),
                     pl.BlockSpec((blk, 256), f(C_MK // 256)),
                     pl.BlockSpec((blk, 256), f(C_MV // 256)),
                     pl.BlockSpec((blk, LANES), f(C_G // LANES))]
    st_specs = [pl.BlockSpec((1, nj // 2, LANES, LANES), lambda b, c: (b, 0, 0, 0)),
                pl.BlockSpec((1, nj // 2, LANES, LANES), lambda b, c: (b, 0, 0, 0)),
                pl.BlockSpec((1, nj, LANES), lambda b, c: (b, 0, 0))]
    st_shapes = [jax.ShapeDtypeStruct((bsz, nj // 2, LANES, LANES), F32),
                 jax.ShapeDtypeStruct((bsz, nj // 2, LANES, LANES), F32),
                 jax.ShapeDtypeStruct((bsz, nj, LANES), F32)]
    return pl.pallas_call(
        functools.partial(_mlstm_kernel, nc=nc),
        grid=(bsz, nc),
        in_specs=qkv(fw) + qkv(bw) + st_specs + [pl.BlockSpec((1, LANES), lambda b, c: (0, 0)),
                                                 pl.BlockSpec(tri.shape, lambda b, c: (0, 0, 0)),
                                                 pl.BlockSpec(sel.shape, lambda b, c: (0, 0, 0))],
        out_specs=[pl.BlockSpec((blk, 256), lambda b, c: (b * nc + c, 0)),
                   pl.BlockSpec((blk, 256), lambda b, c: (b * nc + nc - 1 - c, 0))] + st_specs,
        out_shape=[jax.ShapeDtypeStruct((bsz * seq, 256), F32),
                   jax.ShapeDtypeStruct((bsz * seq, 256), F32)] + st_shapes,
        scratch_shapes=[pltpu.VMEM((nj // 2, LANES, LANES), F32),
                        pltpu.VMEM((nj // 2, LANES, LANES), F32),
                        pltpu.VMEM((nj, LANES), F32)],
        compiler_params=_cparams(("arbitrary", "arbitrary")),
        name="mlstm",
    )(proj, proj, proj, proj, proj, proj, proj, proj, c0, n0, m0, gate_bias, tri, sel)


RG_TC = 256
RG_PAD = 8
RG_NSEG = 8
RG_SKEW = 4


def _rglru_kernel(rx, ry, cw, cb, wg, bg, lam, h0, oc, hl, xpad, af, ab, uf, ub, *, seg_len, chained):
    rows = RG_NSEG * seg_len
    seq_len = rows if chained else seg_len
    halves = RG_WIDTH // LANES
    pitch = seg_len + RG_SKEW
    buf_row = lambda t: (t // seg_len) * pitch + t % seg_len
    xpad[0:RG_PAD, :] = jnp.zeros((RG_PAD, RG_WIDTH), F32)
    xpad[rows + RG_PAD:rows + 2 * RG_PAD, :] = jnp.zeros((RG_PAD, RG_WIDTH), F32)
    xpad[RG_PAD:rows + RG_PAD, :] = rx[...]
    sp = jax.nn.softplus(-lam[...])
    left = (RG_CONV - 1) // 2
    for ci in range(rows // RG_TC):
        s0 = ci * RG_TC
        pos = (s0 + lax.broadcasted_iota(jnp.int32, (RG_TC, RG_WIDTH), 0)) % seq_len
        xc = None
        for j in range(RG_CONV):
            a = RG_PAD + s0 + j - left
            term = xpad[a:a + RG_TC, :] * cw[j:j + 1, :]
            if not chained and j != left:
                term = jnp.where((pos + (j - left) >= 0) & (pos + (j - left) < seq_len), term, 0.0)
            xc = term if xc is None else xc + term
        xc = xc + cb[...]
        pre = _dot(xc, wg[...]) + bg[...]
        for d, (a_ref, u_ref) in enumerate(((af, uf), (ab, ub))):
            o = 2 * RG_WIDTH * d
            r = _sigmoid(pre[:, o:o + RG_WIDTH])
            gi = _sigmoid(pre[:, o + RG_WIDTH:o + 2 * RG_WIDTH])
            log_a = -RG_C * r * sp[d:d + 1, :]
            a_val = jnp.exp(log_a)
            u_val = jnp.sqrt(-jnp.tanh(log_a) * (a_val * a_val + 1.0)) * (gi * xc)
            for hv in range(halves):
                dst = slice(buf_row(s0), buf_row(s0) + RG_TC)
                a_ref[hv, dst, :] = a_val[:, LANES * hv:LANES * (hv + 1)]
                u_ref[hv, dst, :] = u_val[:, LANES * hv:LANES * (hv + 1)]

    def body(s, carry):
        out = []
        for d, (a_ref, u_ref) in enumerate(((af, uf), (ab, ub))):
            step_rows = pl.ds(s if d == 0 else seg_len - 1 - s, RG_NSEG, stride=pitch)
            for hv in range(halves):
                h_loc, prod = carry[2 * (halves * d + hv)], carry[2 * (halves * d + hv) + 1]
                a = a_ref[hv, step_rows, :]
                h_loc = a * h_loc + u_ref[hv, step_rows, :]
                prod = a * prod
                u_ref[hv, step_rows, :] = h_loc
                a_ref[hv, step_rows, :] = prod
                out += [h_loc, prod]
        return tuple(out)

    zero = jnp.zeros((RG_NSEG, LANES), F32)
    one = jnp.ones((RG_NSEG, LANES), F32)
    ends = lax.fori_loop(0, seg_len, body, (zero, one) * (2 * halves), unroll=8)
    seg = lax.broadcasted_iota(jnp.int32, (RG_NSEG, LANES), 0)
    for d, (a_ref, u_ref) in enumerate(((af, uf), (ab, ub))):
        for hv in range(halves):
            lanes = slice(LANES * hv, LANES * (hv + 1))
            h_end, p_end = ends[2 * (halves * d + hv)], ends[2 * (halves * d + hv) + 1]
            h_in = h0[0, d][:, lanes]
            if chained:
                c = h_in[0:1, :]
                h_in = zero
                for k in (range(RG_NSEG) if d == 0 else reversed(range(RG_NSEG))):
                    h_in = jnp.where(seg == k, c, h_in)
                    c = h_end[k:k + 1, :] + p_end[k:k + 1, :] * c
            hl[0, d, :, lanes] = h_end + p_end * h_in
            for k in range(RG_NSEG):
                for ci in range(seg_len // RG_TC):
                    sl = slice(k * pitch + ci * RG_TC, k * pitch + (ci + 1) * RG_TC)
                    u_ref[hv, sl, :] = u_ref[hv, sl, :] + a_ref[hv, sl, :] * h_in[k:k + 1, :]
    for ci in range(rows // RG_TC):
        sl = slice(ci * RG_TC, (ci + 1) * RG_TC)
        for hv in range(halves):
            lanes = slice(LANES * hv, LANES * (hv + 1))
            src = slice(buf_row(ci * RG_TC), buf_row(ci * RG_TC) + RG_TC)
            oc[sl, lanes] = (uf[hv, src, :] + ub[hv, src, :]) * jax.nn.gelu(ry[sl, lanes])


def _rglru(proj, cw, cb, wg, bg, lam, h0, *, nblk, seg_len, chained, row0):
    rows = RG_NSEG * seg_len
    base = row0 // rows
    full = lambda shape: pl.BlockSpec(shape, lambda b: tuple(0 for _ in shape))
    st_spec = pl.BlockSpec((1, 2, RG_NSEG, RG_WIDTH), lambda b: (b, 0, 0, 0))
    return pl.pallas_call(
        functools.partial(_rglru_kernel, seg_len=seg_len, chained=chained),
        grid=(nblk,),
        in_specs=[pl.BlockSpec((rows, RG_WIDTH), lambda b: (base + b, C_RX // RG_WIDTH)),
                  pl.BlockSpec((rows, RG_WIDTH), lambda b: (base + b, C_RY // RG_WIDTH)),
                  full((RG_CONV, RG_WIDTH)), full((1, RG_WIDTH)),
                  full((RG_WIDTH, 4 * RG_WIDTH)), full((1, 4 * RG_WIDTH)), full((2, RG_WIDTH)),
                  st_spec],
        out_specs=[pl.BlockSpec((rows, RG_WIDTH), lambda b: (b, 0)), st_spec],
        out_shape=[jax.ShapeDtypeStruct((nblk * rows, RG_WIDTH), F32),
                   jax.ShapeDtypeStruct((nblk, 2, RG_NSEG, RG_WIDTH), F32)],
        scratch_shapes=[pltpu.VMEM((rows + 2 * RG_PAD, RG_WIDTH), F32)]
        + [pltpu.VMEM((RG_WIDTH // LANES, RG_NSEG * (seg_len + RG_SKEW), LANES), F32) for _ in range(4)],
        compiler_params=_cparams(("arbitrary",)),
        name="rglru",
    )(proj, proj, cw, cb, wg, bg, lam, h0)


def _softmax_pv(scores, values, sink):
    m = functools.reduce(jnp.maximum, [jnp.max(s, axis=-1, keepdims=True) for s in scores])
    if sink is not None:
        m = jnp.maximum(m, sink)
    ps = [jnp.exp(s - m) for s in scores]
    den = functools.reduce(jnp.add, [jnp.sum(p, axis=-1, keepdims=True) for p in ps])
    if sink is not None:
        den = den + jnp.exp(sink - m)
    num = functools.reduce(jnp.add, [_dot(p, v) for p, v in zip(ps, values)])
    return num / den


def _ctx_attn_kernel(sink, sq, sk, sv, nq, nk, nv, ob, od):
    assert SW_HEADS == 4 and SW_KV_HEADS == 2
    left = lax.broadcasted_iota(jnp.int32, (SEQ, LANES), 1) < HEAD_DIM
    first = lax.broadcasted_iota(jnp.int32, (2 * SEQ, 1), 0) < SEQ
    k2 = sk[...].astype(BF16)
    v2 = sv[...].astype(BF16)
    for p in range(SW_KV_HEADS):
        lanes = slice(LANES * p, LANES * (p + 1))
        q2 = sq[:, lanes]
        q_swapped = pltpu.roll(q2, HEAD_DIM, 1)
        kv_half = left if p == 0 else jnp.logical_not(left)
        qs = jnp.concatenate([jnp.where(kv_half, q2 if i == p else q_swapped, 0.0) for i in range(2)], axis=0)
        sink_col = jnp.where(first, sink[2 * p], sink[2 * p + 1])
        res = _softmax_pv([_dot_nt(qs, k2) * SCALE], [v2], sink_col)
        halves = [res[0:SEQ], res[SEQ:2 * SEQ]]
        placed = [halves[i] if i == p else pltpu.roll(halves[i], HEAD_DIM, 1) for i in range(2)]
        ob[:, lanes] = jnp.where(left, placed[0], placed[1])
    for p in range(NA_HEADS // 2):
        lanes = slice(LANES * p, LANES * (p + 1))
        q2 = nq[:, lanes]
        k2 = nk[:, lanes].astype(BF16)
        v2 = nv[:, lanes].astype(BF16)
        res = [_softmax_pv([_dot_nt(jnp.where(left if i == 0 else jnp.logical_not(left), q2, 0.0), k2) * SCALE],
                           [v2], None) for i in range(2)]
        od[:, lanes] = jnp.where(left, res[0], res[1])


def _ctx_attn(proj, sink):
    blk = lambda w, col: pl.BlockSpec((SEQ, w), lambda b: (b, col))
    return pl.pallas_call(
        _ctx_attn_kernel,
        grid=(BATCH,),
        in_specs=[pl.BlockSpec(memory_space=pltpu.SMEM),
                  blk(256, C_SQ // 256), blk(128, C_SK // 128), blk(128, C_SV // 128),
                  blk(256, C_NQ // 256), blk(256, C_NK // 256), blk(256, C_NV // 256)],
        out_specs=[pl.BlockSpec((SEQ, 256), lambda b: (b, 0)),
                   pl.BlockSpec((SEQ, 256), lambda b: (b, 0))],
        out_shape=[jax.ShapeDtypeStruct((SEG, 256), F32), jax.ShapeDtypeStruct((SEG, 256), F32)],
        compiler_params=_cparams(("arbitrary",)),
        name="ctx_attn",
    )(sink, proj, proj, proj, proj, proj, proj)


SW_QB = 128
SW_SPAN = SW_QB + 2 * SW_WINDOW


def _swa_kernel(sink, q, k, v, kc, vc, ob):
    assert SW_HEADS == 4 and SW_KV_HEADS == 2
    n = pl.program_id(1)
    ws = jnp.clip((n - 1) * SW_QB, 0, DEC_SEQ - SW_SPAN)
    ws = pl.multiple_of(ws, SW_QB)
    row = lax.broadcasted_iota(jnp.int32, (2 * SW_QB, SW_SPAN), 0)
    qpos = n * SW_QB + row % SW_QB
    kpos = ws + lax.broadcasted_iota(jnp.int32, (2 * SW_QB, SW_SPAN), 1)
    valid = jnp.abs(qpos - kpos) <= SW_WINDOW
    left = lax.broadcasted_iota(jnp.int32, (SW_QB, LANES), 1) < HEAD_DIM
    first = lax.broadcasted_iota(jnp.int32, (2 * SW_QB, 1), 0) < SW_QB
    k2 = k[pl.ds(ws, SW_SPAN), :].astype(BF16)
    v2 = v[pl.ds(ws, SW_SPAN), :].astype(BF16)
    kc2 = kc[0].astype(BF16)
    vc2 = vc[0].astype(BF16)
    for p in range(SW_KV_HEADS):
        lanes = slice(LANES * p, LANES * (p + 1))
        q2 = q[:, lanes]
        q_swapped = pltpu.roll(q2, HEAD_DIM, 1)
        kv_half = left if p == 0 else jnp.logical_not(left)
        qs = jnp.concatenate([jnp.where(kv_half, q2 if i == p else q_swapped, 0.0) for i in range(2)], axis=0)
        s_loc = jnp.where(valid, _dot_nt(qs, k2) * SCALE, NEG)
        s_ctx = _dot_nt(qs, kc2) * SCALE
        sink_col = jnp.where(first, sink[2 * p], sink[2 * p + 1])
        res = _softmax_pv([s_loc, s_ctx], [v2, vc2], sink_col)
        halves = [res[0:SW_QB], res[SW_QB:2 * SW_QB]]
        placed = [halves[i] if i == p else pltpu.roll(halves[i], HEAD_DIM, 1) for i in range(2)]
        ob[:, lanes] = jnp.where(left, placed[0], placed[1])


def _swa(proj, kc, vc, sink):
    nq = DEC_SEQ // SW_QB
    qbase = SEG // SW_QB
    return pl.pallas_call(
        _swa_kernel,
        grid=(DEC_BATCH, nq),
        in_specs=[pl.BlockSpec(memory_space=pltpu.SMEM),
                  pl.BlockSpec((SW_QB, 256), lambda b, n: (qbase + b * nq + n, C_SQ // 256)),
                  pl.BlockSpec((DEC_SEQ, 128), lambda b, n: (1 + b, C_SK // 128)),
                  pl.BlockSpec((DEC_SEQ, 128), lambda b, n: (1 + b, C_SV // 128)),
                  pl.BlockSpec((1, PAST_LEN, 128), lambda b, n: (b, 0, 0)),
                  pl.BlockSpec((1, PAST_LEN, 128), lambda b, n: (b, 0, 0))],
        out_specs=pl.BlockSpec((SW_QB, 256), lambda b, n: (b * nq + n, 0)),
        out_shape=jax.ShapeDtypeStruct((DEC_BATCH * DEC_SEQ, 256), F32),
        compiler_params=_cparams(("arbitrary", "arbitrary")),
        name="swa",
    )(sink, proj, proj, proj, kc, vc)


NA_RPB_R = 2 * NA_ROWS - 1
NA_RPB_C = 2 * NA_COLS - 1
GRID_ROWS = DEC_SEQ // GRID_W
NA_RB = 4
NA_UW = 12
NA_NQ = NA_RB * GRID_W
NA_NKEY = NA_UW * GRID_W
NA_NBLK = GRID_ROWS // NA_RB
NA_CASES = ((0, 0), (NA_RB, 0), (GRID_ROWS - NA_RB, GRID_ROWS - NA_UW))


def _na_row_start(qrow):
    return min(max(qrow - NA_ROWS // 2, 0), GRID_ROWS - NA_ROWS)


def _na_bias_kernel(rpb, out):
    h = pl.program_id(0)
    qc = lax.broadcasted_iota(jnp.int32, (GRID_W, GRID_W), 0)
    kc = lax.broadcasted_iota(jnp.int32, (GRID_W, GRID_W), 1)
    dc = jnp.clip(kc - qc, -(NA_COLS - 1), NA_COLS - 1) + NA_COLS - 1
    lo = jnp.clip(qc - NA_COLS // 2, 0, GRID_W - NA_COLS)
    valid = (kc >= lo) & (kc < lo + NA_COLS)
    tiles = []
    for dr in range(NA_RPB_R):
        t = jnp.zeros((GRID_W, GRID_W), F32)
        for j in range(NA_RPB_C):
            t = jnp.where(dc == j, rpb[(h * NA_RPB_R + dr) * NA_RPB_C + j], t)
        tiles.append(jnp.where(valid, t, NEG))
    outside = jnp.full((GRID_W, GRID_W), NEG, F32)
    for case, (q0, k0) in enumerate(NA_CASES):
        for a in range(NA_RB):
            r0 = _na_row_start(q0 + a)
            for i in range(NA_UW):
                inside = r0 <= k0 + i < r0 + NA_ROWS
                tile = tiles[k0 + i - (q0 + a) + NA_ROWS - 1] if inside else outside
                out[0, case, GRID_W * a:GRID_W * (a + 1), GRID_W * i:GRID_W * (i + 1)] = tile


def _na_bias(rpb_flat):
    shape = (NA_HEADS, len(NA_CASES), NA_NQ, NA_NKEY)
    return pl.pallas_call(
        _na_bias_kernel,
        grid=(NA_HEADS,),
        in_specs=[pl.BlockSpec(memory_space=pltpu.SMEM)],
        out_specs=pl.BlockSpec((1,) + shape[1:], lambda h: (h, 0, 0, 0)),
        out_shape=jax.ShapeDtypeStruct(shape, F32),
        compiler_params=_cparams(("arbitrary",)),
        name="na_bias",
    )(rpb_flat)


def _na_kernel(q, k, v, kc, vc, bias, od):
    blk = pl.program_id(1)
    case = jnp.where(blk == 0, 0, jnp.where(blk == NA_NBLK - 1, 2, 1))
    u0 = jnp.clip(blk * NA_RB - NA_ROWS // 2, 0, GRID_ROWS - NA_UW)
    k0 = pl.multiple_of(u0 * GRID_W, GRID_W)
    left = lax.broadcasted_iota(jnp.int32, (NA_NQ, LANES), 1) < HEAD_DIM
    for p in range(NA_HEADS // 2):
        lanes = slice(LANES * p, LANES * (p + 1))
        q2 = q[:, lanes]
        k2 = k[pl.ds(k0, NA_NKEY), lanes].astype(BF16)
        v2 = v[pl.ds(k0, NA_NKEY), lanes].astype(BF16)
        kc2 = kc[0, :, lanes].astype(BF16)
        vc2 = vc[0, :, lanes].astype(BF16)
        res = []
        for i in range(2):
            qm = jnp.where(left if i == 0 else jnp.logical_not(left), q2, 0.0)
            s_loc = _dot_nt(qm, k2) * SCALE + bias[2 * p + i, pl.ds(case, 1)][0]
            s_ctx = _dot_nt(qm, kc2) * SCALE
            res.append(_softmax_pv([s_loc, s_ctx], [v2, vc2], None))
        od[:, lanes] = jnp.where(left, res[0], res[1])


def _na(proj, kc, vc, bias):
    qbase = SEG // NA_NQ
    return pl.pallas_call(
        _na_kernel,
        grid=(DEC_BATCH, NA_NBLK),
        in_specs=[pl.BlockSpec((NA_NQ, 256), lambda b, r: (qbase + b * NA_NBLK + r, C_NQ // 256)),
                  pl.BlockSpec((DEC_SEQ, 256), lambda b, r: (1 + b, C_NK // 256)),
                  pl.BlockSpec((DEC_SEQ, 256), lambda b, r: (1 + b, C_NV // 256)),
                  pl.BlockSpec((1, PAST_LEN, 256), lambda b, r: (b, 0, 0)),
                  pl.BlockSpec((1, PAST_LEN, 256), lambda b, r: (b, 0, 0)),
                  pl.BlockSpec((NA_HEADS, len(NA_CASES), NA_NQ, NA_NKEY), lambda b, r: (0, 0, 0, 0))],
        out_specs=pl.BlockSpec((NA_NQ, 256), lambda b, r: (b * NA_NBLK + r, 0)),
        out_shape=jax.ShapeDtypeStruct((DEC_BATCH * DEC_SEQ, 256), F32),
        compiler_params=_cparams(("arbitrary", "arbitrary")),
        name="na",
    )(proj, proj, proj, kc, vc, bias)


MG_ROWS = 256
MG_SUB = 2
MG_TM = MG_SUB * MG_ROWS


def _merge_kernel(xc_ref, xl_ref, mod_ref, g1_ref, g2_ref, hf_c, hf_l, hb_c, hb_l, mo, ob_c, ob_l, oc_c, oc_l,
                  od_c, od_l, hn, wmg, bmg, wbr, wout, *rest, moe):
    if moe:
        wrt, br, x1_ref, h2_ref, route_ref = rest
    else:
        x1_ref, h2_ref = rest
    ctx = pl.program_id(0) == 0
    mod = mod_ref[0]
    chunk = lambda i: mod[:, i * D_MODEL:(i + 1) * D_MODEL]
    sh1, sc1, gate1, sh2, sc2 = chunk(0), chunk(1), chunk(2), chunk(3), chunk(4)

    def row_group(rows):
        pick = lambda c_ref, l_ref: jnp.where(ctx, c_ref[rows, :], l_ref[rows, :])
        x = pick(xc_ref, xl_ref)
        h = (_rms(x, g1_ref[...]) * (1.0 + sc1) + sh1).astype(BF16)
        hsum = pick(hf_c, hf_l) + pick(hb_c, hb_l)
        out_a = jnp.concatenate(
            [_rms_head_pairs(hsum[:, LANES * p:LANES * (p + 1)], hn[...]) for p in range(2)], axis=-1)
        out_a = out_a * jax.nn.sigmoid(mo[rows, :])
        acc = None
        for n, br_val in enumerate((out_a, pick(ob_c, ob_l), pick(oc_c, oc_l), pick(od_c, od_l))):
            gate = jax.nn.sigmoid(jnp.dot(h, wmg[:, n * D_MODEL:(n + 1) * D_MODEL], preferred_element_type=F32)
                                  + bmg[:, n * D_MODEL:(n + 1) * D_MODEL])
            term = gate * jnp.dot(br_val.astype(BF16), wbr[n], preferred_element_type=F32)
            acc = term if acc is None else acc + term
        y = jnp.dot(acc.astype(BF16), wout[...], preferred_element_type=F32)
        x1 = x + gate1 * y
        x1_ref[rows, :] = x1
        h2 = _rms(x1, g2_ref[...]) * (1.0 + sc2) + sh2
        if moe:
            _store_token_tiles(h2_ref.at[pl.ds(rows.start * TOK_TILE, MG_ROWS * TOK_TILE)], h2)
        else:
            h2_ref[rows, :] = h2.astype(BF16)
        if moe:
            logit = [jnp.sum(h2 * wrt[e:e + 1, :], axis=-1, keepdims=True) + br[e] for e in range(N_EXPERTS)]
            v1, i1 = logit[0], jnp.zeros(logit[0].shape, jnp.int32)
            for e in range(1, N_EXPERTS):
                better = logit[e] > v1
                v1 = jnp.where(better, logit[e], v1)
                i1 = jnp.where(better, e, i1)
            v2, i2 = jnp.full(v1.shape, -jnp.inf, F32), jnp.zeros(v1.shape, jnp.int32)
            for e in range(N_EXPERTS):
                better = (i1 != e) & (logit[e] > v2)
                v2 = jnp.where(better, logit[e], v2)
                i2 = jnp.where(better, e, i2)
            e2 = jnp.exp(v2 - v1)
            den = 1.0 + e2
            lane = lax.broadcasted_iota(jnp.int32, (MG_ROWS, LANES), 1)
            route = jnp.where(lane == 0, 1.0 / den, 0.0) + jnp.where(lane == 1, e2 / den, 0.0)
            route = route + jnp.where(lane == 2, i1.astype(F32), 0.0) + jnp.where(lane == 3, i2.astype(F32), 0.0)
            route_ref[rows, :] = route

    for sub in range(MG_SUB):
        row_group(slice(sub * MG_ROWS, (sub + 1) * MG_ROWS))


def _merge(x, mod_l, g1, g2, hf, hb, proj, ob, oc, od, hn, wmg, bmg, wbr, wout, router=None):
    nt = SEG // MG_TM
    moe = router is not None
    x_pair, lat_row0 = _x_pair(x)
    row = lambda w: pl.BlockSpec((MG_TM, w), lambda s, i: (s * nt + i, 0))
    ctx_blk, lat_blk = _seg_pair_specs(MG_TM, 256)
    full = lambda shape: pl.BlockSpec(shape, lambda s, i: tuple(0 for _ in shape), pipeline_mode=pl.Buffered(1))
    in_specs = [*_seg_pair_specs(MG_TM, D_MODEL, lat_row0),
                pl.BlockSpec((1, 1, 6 * D_MODEL), lambda s, i: (s, 0, 0)),
                full((1, D_MODEL)), full((1, D_MODEL)),
                ctx_blk, lat_blk, ctx_blk, lat_blk,
                pl.BlockSpec((MG_TM, 256), lambda s, i: (s * nt + i, C_MO // 256)),
                ctx_blk, lat_blk, ctx_blk, lat_blk, ctx_blk, lat_blk,
                full((1, LANES)), full((D_MODEL, N_BRANCH * D_MODEL)), full((1, N_BRANCH * D_MODEL)),
                full((N_BRANCH, 256, D_MODEL)), full((D_MODEL, D_MODEL))]
    args = [*x_pair, mod_l, g1, g2, *hf, *hb, proj, *ob, *oc, *od, hn, wmg, bmg, wbr, wout]
    if moe:
        h2_spec = pl.BlockSpec((MG_TM * TOK_TILE, LANES), lambda s, i: (s * nt + i, 0))
        h2_shape = jax.ShapeDtypeStruct((N_TOK * TOK_TILE, LANES), F32)
    else:
        h2_spec, h2_shape = row(D_MODEL), jax.ShapeDtypeStruct((N_TOK, D_MODEL), BF16)
    out_specs = [row(D_MODEL), h2_spec]
    out_shape = [jax.ShapeDtypeStruct((N_TOK, D_MODEL), F32), h2_shape]
    if moe:
        in_specs += [full((N_EXPERTS, D_MODEL)), pl.BlockSpec(memory_space=pltpu.SMEM)]
        args += list(router)
        out_specs.append(row(LANES))
        out_shape.append(jax.ShapeDtypeStruct((N_TOK, LANES), F32))
    return pl.pallas_call(
        functools.partial(_merge_kernel, moe=moe),
        grid=(N_SEG, nt),
        in_specs=in_specs,
        out_specs=out_specs,
        out_shape=out_shape,
        compiler_params=_cparams(("arbitrary", "arbitrary")),
        name="merge",
    )(*args)


FF_TM = 512


def _ffn_kernel(h2, w1, w3, w2, x1, mod_ref, out):
    h = h2[...]
    a = jnp.dot(h, w1[...], preferred_element_type=F32)
    b = jnp.dot(h, w3[...], preferred_element_type=F32)
    act = (jax.nn.silu(a) * b).astype(BF16)
    gate2 = mod_ref[0][:, 5 * D_MODEL:6 * D_MODEL]
    out[...] = x1[...] + gate2 * jnp.dot(act, w2[...], preferred_element_type=F32)


def _ffn(h2, x1, mod_l, w1, w3, w2):
    nt = N_TOK // FF_TM
    per_seg = SEG // FF_TM
    resident = lambda shape: pl.BlockSpec(shape, lambda i: (0, 0), pipeline_mode=pl.Buffered(1))
    return pl.pallas_call(
        _ffn_kernel,
        grid=(nt,),
        in_specs=[pl.BlockSpec((FF_TM, D_MODEL), lambda i: (i, 0)),
                  resident((D_MODEL, D_FF)), resident((D_MODEL, D_FF)), resident((D_FF, D_MODEL)),
                  pl.BlockSpec((FF_TM, D_MODEL), lambda i: (i, 0)),
                  pl.BlockSpec((1, 1, 6 * D_MODEL), lambda i: (i // per_seg, 0, 0))],
        out_specs=pl.BlockSpec((FF_TM, D_MODEL), lambda i: (i, 0)),
        out_shape=jax.ShapeDtypeStruct((N_TOK, D_MODEL), F32),
        compiler_params=_cparams(("arbitrary",)),
        name="ffn",
    )(h2, w1, w3, w2, x1, mod_l)


MOE_TM = 256
MOE_SLOTS = 2 * N_TOK
MOE_TILES = MOE_SLOTS // MOE_TM + N_EXPERTS
MOE_NBUF = 3
MOE_STEPS = MOE_TILES + MOE_NBUF
MOE_DUMP = MOE_NBUF * MOE_TM
MOE_LEAD = 1
MOE_PLAN_TILES = MOE_LEAD + MOE_TILES + 2
MOE_FCHUNKS = 1
MOE_UNROLL = 8


def _moe_group_kernel(texp, nused, src_tok, dst_row, h2_hbm, w1, w3, w2, y_hbm, xs, ys, sem_in, sem_out):
    del texp
    i = pl.program_id(0)
    n_used = nused[0]
    buf = i % MOE_NBUF
    buf_next = (i + 2) % MOE_NBUF

    def tile_rows(t):
        start = t * TOK_TILE
        return pl.ds(start if isinstance(start, int) else pl.multiple_of(start, TOK_TILE), TOK_TILE)

    def gather_copy(tile, b, r):
        tok = src_tok[(tile + MOE_LEAD) * MOE_TM + r]
        return pltpu.make_async_copy(h2_hbm.at[tile_rows(tok)], xs.at[b, tile_rows(r)], sem_in.at[b])

    def scatter_copy(tile, b, r):
        dst = dst_row[(tile + MOE_LEAD) * MOE_TM + r]
        return pltpu.make_async_copy(ys.at[b, tile_rows(r)], y_hbm.at[tile_rows(dst)], sem_out.at[b])

    def start_rows_loop(make_copy, tile, b):
        def body(r, carry):
            make_copy(tile, b, r).start()
            return carry
        lax.fori_loop(0, MOE_TM, body, 0, unroll=MOE_UNROLL)

    def wait_tile(b, gather):
        if gather:
            pltpu.make_async_copy(h2_hbm.at[pl.ds(0, MOE_TM * TOK_TILE)], xs.at[b], sem_in.at[b]).wait()
        else:
            pltpu.make_async_copy(ys.at[b], y_hbm.at[pl.ds(0, MOE_TM * TOK_TILE)], sem_out.at[b]).wait()

    @pl.when(i == 0)
    def _():
        xs[...] = jnp.zeros(xs.shape, F32)
        ys[...] = jnp.zeros(ys.shape, F32)
        for b in range(MOE_NBUF):
            fill = pltpu.make_async_copy(ys.at[b], y_hbm.at[pl.ds((MOE_SLOTS + b * MOE_TM) * TOK_TILE, MOE_TM * TOK_TILE)],
                                         sem_out.at[b])
            fill.start()
            fill.wait()
        start_rows_loop(gather_copy, 0, 0)
        start_rows_loop(gather_copy, 1, 1)

    @pl.when(i <= n_used + 1)
    def _():
        wait_tile(buf, True)

    @pl.when((i >= 2) & (i <= n_used + 2))
    def _():
        wait_tile(buf, False)

    @pl.when(i < n_used)
    def _():
        x = _load_token_tiles(xs.at[buf], MOE_TM).astype(BF16)
        fc = D_FF_EXPERT // MOE_FCHUNKS
        rc = MOE_TM // MOE_FCHUNKS
        y = None
        for c in range(MOE_FCHUNKS):
            for r in range(c * rc, (c + 1) * rc):
                gather_copy(i + 2, buf_next, r).start()
                scatter_copy(i - 1, buf_next, r).start()
            a = jnp.dot(x, w1[0, :, c * fc:(c + 1) * fc].astype(BF16), preferred_element_type=F32)
            b = jnp.dot(x, w3[0, :, c * fc:(c + 1) * fc].astype(BF16), preferred_element_type=F32)
            act = (jax.nn.silu(a) * b).astype(BF16)
            part = jnp.dot(act, w2[0, c * fc:(c + 1) * fc, :].astype(BF16), preferred_element_type=F32)
            y = part if y is None else y + part
        _store_token_tiles(ys.at[buf], y)

    @pl.when(i == n_used)
    def _():
        start_rows_loop(scatter_copy, i - 1, buf_next)


def _moe_group(tile_expert, n_used, src_tok, dst_row, h2, w1, w3, w2):
    wspec = lambda shape: pl.BlockSpec((1,) + shape, lambda i, texp, *_: (texp[jnp.minimum(i, MOE_TILES - 1)], 0, 0))
    grid_spec = pltpu.PrefetchScalarGridSpec(
        num_scalar_prefetch=4,
        grid=(MOE_STEPS,),
        in_specs=[pl.BlockSpec(memory_space=pl.ANY),
                  wspec((D_MODEL, D_FF_EXPERT)), wspec((D_MODEL, D_FF_EXPERT)), wspec((D_FF_EXPERT, D_MODEL))],
        out_specs=pl.BlockSpec(memory_space=pl.ANY),
        scratch_shapes=[pltpu.VMEM((MOE_NBUF, MOE_TM * TOK_TILE, LANES), F32),
                        pltpu.VMEM((MOE_NBUF, MOE_TM * TOK_TILE, LANES), F32),
                        pltpu.SemaphoreType.DMA((MOE_NBUF,)), pltpu.SemaphoreType.DMA((MOE_NBUF,))])
    return pl.pallas_call(
        _moe_group_kernel,
        grid_spec=grid_spec,
        out_shape=jax.ShapeDtypeStruct(((MOE_SLOTS + MOE_DUMP) * TOK_TILE, LANES), F32),
        compiler_params=_cparams(("arbitrary",)),
        name="moe_group",
    )(tile_expert, n_used, src_tok, dst_row, h2, w1, w3, w2)


def _moe_plan(expert_ids):
    e_flat = expert_ids.T.reshape(-1)
    order = jnp.argsort(e_flat, stable=True).astype(jnp.int32)
    counts = jnp.sum((e_flat[:, None] == jnp.arange(N_EXPERTS)[None, :]).astype(jnp.int32), axis=0)
    padded = (counts + MOE_TM - 1) // MOE_TM * MOE_TM
    pend = jnp.cumsum(padded)
    pstart = pend - padded
    ustart = jnp.cumsum(counts) - counts
    n_used = pend[-1] // MOE_TM
    tiles = jnp.arange(MOE_TILES, dtype=jnp.int32)
    last_used = jnp.minimum(tiles, n_used - 1)
    tile_expert = jnp.sum((last_used[:, None] * MOE_TM >= pend[None, :]).astype(jnp.int32), axis=1)
    t = jnp.arange(-MOE_LEAD, MOE_PLAN_TILES - MOE_LEAD, dtype=jnp.int32)[:, None]
    r = jnp.arange(MOE_TM, dtype=jnp.int32)[None, :]
    e_t = tile_expert[jnp.clip(t, 0, MOE_TILES - 1)]
    off = t * MOE_TM + r - pstart[e_t]
    valid = (t >= 0) & (t < n_used) & (off < counts[e_t])
    slot = order[jnp.clip(ustart[e_t] + off, 0, MOE_SLOTS - 1)]
    src_tok = jnp.where(valid, slot % N_TOK, 0)
    dst_row = jnp.where(valid, slot, MOE_SLOTS + (t % MOE_NBUF) * MOE_TM + r)
    return (tile_expert.astype(jnp.int32), n_used.reshape(1).astype(jnp.int32),
            src_tok.reshape(-1).astype(jnp.int32), dst_row.reshape(-1).astype(jnp.int32))


def _moe_combine_kernel(x1, y0, y1, route, mod_ref, out_c, out_l):
    gate2 = mod_ref[0][:, 5 * D_MODEL:6 * D_MODEL]
    r = route[...]
    val = x1[...] + gate2 * (r[:, 0:1] * _load_token_tiles(y0, FF_TM) + r[:, 1:2] * _load_token_tiles(y1, FF_TM))
    is_ctx = pl.program_id(0) < SEG // FF_TM

    @pl.when(is_ctx)
    def _():
        out_c[...] = val

    @pl.when(jnp.logical_not(is_ctx))
    def _():
        out_l[...] = val


def _moe_combine(x1, y_slots, route, mod_l):
    nt = N_TOK // FF_TM
    per_seg = SEG // FF_TM
    return pl.pallas_call(
        _moe_combine_kernel,
        grid=(nt,),
        in_specs=[pl.BlockSpec((FF_TM, D_MODEL), lambda i: (i, 0)),
                  pl.BlockSpec((FF_TM * TOK_TILE, LANES), lambda i: (i, 0)),
                  pl.BlockSpec((FF_TM * TOK_TILE, LANES), lambda i: (nt + i, 0)),
                  pl.BlockSpec((FF_TM, LANES), lambda i: (i, 0)),
                  pl.BlockSpec((1, 1, 6 * D_MODEL), lambda i: (i // per_seg, 0, 0))],
        out_specs=[pl.BlockSpec((FF_TM, D_MODEL), lambda i: (jnp.minimum(i, per_seg - 1), 0)),
                   pl.BlockSpec((FF_TM, D_MODEL), lambda i: (jnp.maximum(i - per_seg, 0), 0))],
        out_shape=[jax.ShapeDtypeStruct((SEG, D_MODEL), F32), jax.ShapeDtypeStruct((N_TOK - SEG, D_MODEL), F32)],
        compiler_params=_cparams(("arbitrary",)),
        name="moe_combine",
    )(x1, y_slots, y_slots, route, mod_l)


def _rope_tables():
    t = np.arange(DEC_SEQ)
    row, col = (t // GRID_W).astype(np.float32), (t % GRID_W).astype(np.float32)
    nf = HEAD_DIM // 4
    freqs = np.float32(ROPE_BASE) ** (-np.arange(nf, dtype=np.float32) / np.float32(nf))
    lane = np.arange(LANES) % HEAD_DIM
    fidx = lane % nf
    use_col = (lane // (HEAD_DIM // 2)) == 1
    first = (lane % (HEAD_DIM // 2)) < nf
    pos = np.where(use_col[None, :], col[:, None], row[:, None])
    ang = (pos * freqs[fidx][None, :]).astype(np.float32).astype(np.float64)
    sin = np.sin(ang)
    return (jnp.asarray(np.cos(ang), dtype=F32), jnp.asarray(np.where(first[None, :], -sin, sin), dtype=F32))


def _permute_w_in(w):
    sizes = (256, 256, 256, 256, 8, 8, 256, 128, 128, 256, 256, 256, 256, 256)
    offs = np.concatenate([[0], np.cumsum(sizes)])
    part = lambda i: w[:, offs[i]:offs[i + 1]]
    mq, mk, mv, mo, mi, mf, sq, sk, sv, rx, ry, nq, nk, nv = (part(i) for i in range(14))
    pad = jnp.zeros((w.shape[0], LANES - 16), w.dtype)
    return jnp.concatenate([mq, mk, mv, mo, sq, rx, ry, nq, nk, nv, sk, sv, mi, mf, pad], axis=1)


def _block_diag(w):
    eye = jnp.eye(RG_BLOCKS, dtype=w.dtype)
    return (w[:, :, None, :] * eye[:, None, :, None]).reshape(RG_WIDTH, RG_WIDTH)


def _tile2(g):
    return jnp.concatenate([g, g]).reshape(1, LANES)


def kernel(x_prompt, x_sample, cache_swa_k, cache_swa_v, cache_na_k, cache_na_v, state_mlstm_C, state_mlstm_n, state_mlstm_m, state_rglru_h, c, c_ctx, norm1_g, norm2_g, w_ada, b_ada, w_in, ml_b_i, ml_b_f, ml_hn, sw_qn, sw_kn, sw_sink, rg_conv_w, rg_conv_b, rg_w_r, rg_b_r, rg_w_i, rg_b_i, rg_lam, na_qn, na_kn, na_rpb, w_br, w_mg, b_mg, w_out, ffn_w1, ffn_w3, ffn_w2, moe_wr, moe_br, moe_w1, moe_w3, moe_w2):
    assert DEPTH % 2 == 0
    x_all = (x_prompt.reshape(SEG, D_MODEL), x_sample.reshape(N_TOK - SEG, D_MODEL))
    cvecs = jnp.concatenate([c_ctx[None, :], c, jnp.zeros((8 - 1 - DEC_BATCH, D_MODEL), F32)], axis=0)
    mod = _mod_table(cvecs.T, w_ada, b_ada)
    cos_t, sin_t = _rope_tables()
    nj = 2 * ML_HEADS
    zeros_state = (jnp.zeros((BATCH, nj // 2, LANES, LANES), F32), jnp.zeros((BATCH, nj // 2, LANES, LANES), F32),
                   jnp.zeros((BATCH, nj, LANES), F32), jnp.zeros((BATCH // RG_NSEG, 2, RG_NSEG, RG_WIDTH), F32))
    ctx_out = []
    for l in range(DEPTH):
        mod_l = mod[l].reshape(8, 1, 6 * D_MODEL)
        qk_gains = jnp.stack([_tile2(sw_qn[l])[0], _tile2(sw_kn[l])[0], _tile2(na_qn[l])[0], _tile2(na_kn[l])[0]])
        proj = _inproj(x_all, mod_l, norm1_g[l].reshape(1, D_MODEL), _permute_w_in(w_in[l]).astype(BF16),
                       qk_gains, cos_t, sin_t)
        gate_bias = jnp.concatenate([ml_b_i[l].reshape(-1), ml_b_f[l].reshape(-1),
                                     jnp.zeros((LANES - 2 * nj,), F32)]).reshape(1, LANES)
        hf_c, hb_c, *st_new = _mlstm(proj, gate_bias, *zeros_state[:3], bsz=BATCH, seq=SEQ, row0=0)
        c_new, n_new, m_new = _mlstm_unpack_state(*st_new)
        st_lat = _mlstm_pack_state(state_mlstm_C[:, l].reshape(DEC_BATCH, nj, HEAD_DIM, HEAD_DIM),
                                   state_mlstm_n[:, l].reshape(DEC_BATCH, nj, HEAD_DIM),
                                   state_mlstm_m[:, l].reshape(DEC_BATCH, nj))
        hf_l, hb_l, _, _, _ = _mlstm(proj, gate_bias, *st_lat, bsz=DEC_BATCH, seq=DEC_SEQ, row0=SEG)
        wg = jnp.concatenate([_block_diag(rg_w_r[l, 0]), _block_diag(rg_w_i[l, 0]),
                              _block_diag(rg_w_r[l, 1]), _block_diag(rg_w_i[l, 1])], axis=1).astype(BF16)
        bg = jnp.concatenate([rg_b_r[l, 0], rg_b_i[l, 0], rg_b_r[l, 1], rg_b_i[l, 1]]).reshape(1, 4 * RG_WIDTH)
        rg_args = (rg_conv_w[l], rg_conv_b[l].reshape(1, RG_WIDTH), wg, bg, rg_lam[l])
        oc_c, hl_c = _rglru(proj, *rg_args, zeros_state[3], nblk=BATCH // RG_NSEG, seg_len=SEQ, chained=False, row0=0)
        hl_new = jnp.transpose(hl_c, (0, 2, 1, 3)).reshape(BATCH, 2, RG_WIDTH)
        h0_lat = jnp.broadcast_to(state_rglru_h[:, l][:, :, None, :], (DEC_BATCH, 2, RG_NSEG, RG_WIDTH))
        oc_l, _ = _rglru(proj, *rg_args, h0_lat, nblk=DEC_BATCH, seg_len=DEC_SEQ // RG_NSEG, chained=True, row0=SEG)
        ob_c, od_c = _ctx_attn(proj, sw_sink[l])
        ob_l = _swa(proj, cache_swa_k[:, l].reshape(DEC_BATCH, PAST_LEN, 128),
                    cache_swa_v[:, l].reshape(DEC_BATCH, PAST_LEN, 128), sw_sink[l])
        od_l = _na(proj, cache_na_k[:, l].reshape(DEC_BATCH, PAST_LEN, 256),
                   cache_na_v[:, l].reshape(DEC_BATCH, PAST_LEN, 256), _na_bias(na_rpb[l].reshape(-1)))
        moe_layer = l % 2 == 1
        j = l // 2
        router = (moe_wr[j].T, moe_br[j]) if moe_layer else None
        outs = _merge(x_all, mod_l, norm1_g[l].reshape(1, D_MODEL), norm2_g[l].reshape(1, D_MODEL),
                      (hf_c, hf_l), (hb_c, hb_l), proj, (ob_c, ob_l), (oc_c, oc_l), (od_c, od_l),
                      _tile2(ml_hn[l]), w_mg[l].astype(BF16), b_mg[l].reshape(1, -1), w_br[l].astype(BF16),
                      w_out[l].astype(BF16), router)
        if moe_layer:
            x1, h2, route = outs
            plan = _moe_plan(route[:, 2:4].astype(jnp.int32))
            y_slots = _moe_group(*plan, h2, moe_w1[j], moe_w3[j], moe_w2[j])
            x_all = tuple(_moe_combine(x1, y_slots, route, mod_l))
        else:
            x1, h2 = outs
            x_all = _ffn(h2, x1, mod_l, ffn_w1[j].astype(BF16), ffn_w3[j].astype(BF16), ffn_w2[j].astype(BF16))
        pc = proj[:SEG]
        ctx_out.append(dict(
            sw_k=pc[:, C_SK:C_SK + 128].reshape(BATCH, SEQ, SW_KV_HEADS, HEAD_DIM),
            sw_v=pc[:, C_SV:C_SV + 128].reshape(BATCH, SEQ, SW_KV_HEADS, HEAD_DIM),
            na_k=pc[:, C_NK:C_NK + 256].reshape(BATCH, SEQ, NA_HEADS, HEAD_DIM),
            na_v=pc[:, C_NV:C_NV + 256].reshape(BATCH, SEQ, NA_HEADS, HEAD_DIM),
            ml_C=c_new.reshape(BATCH, 2, ML_HEADS, HEAD_DIM, HEAD_DIM),
            ml_n=n_new.reshape(BATCH, 2, ML_HEADS, HEAD_DIM),
            ml_m=m_new.reshape(BATCH, 2, ML_HEADS),
            rg_h=hl_new))
    stack = lambda name: jnp.stack([t[name] for t in ctx_out], axis=1)
    return (x_all[0].reshape(BATCH, SEQ, D_MODEL), x_all[1].reshape(DEC_BATCH, DEC_SEQ, D_MODEL),
            stack('sw_k'), stack('sw_v'), stack('na_k'), stack('na_v'),
            stack('ml_C'), stack('ml_n'), stack('ml_m'), stack('rg_h'))
```

```python
import functools

import numpy as np
import jax
import jax.numpy as jnp
from jax import lax
from jax.experimental import pallas as pl
from jax.experimental.pallas import tpu as pltpu

F32 = jnp.float32
BF16 = jnp.bfloat16

D_MODEL = 1024
BATCH = 16
SEQ = 256
DEPTH = 2
DEC_BATCH = 2
DEC_SEQ = 4096
PAST_LEN = 256
GRID_W = 64
HEAD_DIM = 64
ML_HEADS = 4
ML_CHUNK = 128
ML_SUB = 2
SW_HEADS = 4
SW_KV_HEADS = 2
SW_WINDOW = 128
RG_WIDTH = 256
RG_BLOCKS = 4
RG_CONV = 4
RG_C = 8.0
NA_HEADS = 4
NA_ROWS = 8
NA_COLS = 16
N_BRANCH = 4
ROPE_BASE = 10000.0
D_FF = 2816
N_EXPERTS = 8
D_FF_EXPERT = 2048
EPS = 1e-6
NEG = -1e30
SCALE = HEAD_DIM ** -0.5

SEG = 4096
N_SEG = 3
N_TOK = N_SEG * SEG
LANES = 128
VMEM_LIMIT = 60 * 1024 * 1024

C_MQ, C_MK, C_MV, C_MO = 0, 256, 512, 768
C_SQ, C_RX, C_RY, C_NQ, C_NK, C_NV = 1024, 1280, 1536, 1792, 2048, 2304
C_SK, C_SV, C_G = 2560, 2688, 2816
P_W = 2944


def _cparams(sem):
    return pltpu.CompilerParams(dimension_semantics=sem, vmem_limit_bytes=VMEM_LIMIT)


def _dot(a, b):
    return jnp.dot(a.astype(BF16), b.astype(BF16), preferred_element_type=F32)


def _dot_nt(a, b):
    return lax.dot_general(a.astype(BF16), b.astype(BF16), (((1,), (1,)), ((), ())),
                           preferred_element_type=F32)


def _dot_tn(a, b):
    return lax.dot_general(a.astype(BF16), b.astype(BF16), (((0,), (0,)), ((), ())),
                           preferred_element_type=F32)


def _split3(x):
    hi = x.astype(BF16)
    r1 = x - hi.astype(F32)
    mid = r1.astype(BF16)
    lo = (r1 - mid.astype(F32)).astype(BF16)
    return hi, mid, lo


def _dot_exact_rhs(a01, x):
    hi, mid, lo = _split3(x)
    d = lambda p: jnp.dot(a01, p, preferred_element_type=F32)
    return d(hi) + d(mid) + d(lo)


def _dot_exact_lhs(x, a01):
    hi, mid, lo = _split3(x)
    d = lambda p: jnp.dot(p, a01, preferred_element_type=F32)
    return d(hi) + d(mid) + d(lo)


def _sigmoid(x):
    return 0.5 * jnp.tanh(0.5 * x) + 0.5


def _rms(x, g):
    return x * lax.rsqrt(jnp.mean(x * x, axis=-1, keepdims=True) + EPS) * g


def _rms_head_pairs(x, g):
    lane = lax.broadcasted_iota(jnp.int32, x.shape, 1)
    left = lane < HEAD_DIM
    sq = x * x
    s0 = jnp.sum(jnp.where(left, sq, 0.0), axis=-1, keepdims=True)
    s1 = jnp.sum(jnp.where(left, 0.0, sq), axis=-1, keepdims=True)
    ms = jnp.where(left, s0, s1) * (1.0 / HEAD_DIM)
    return x * lax.rsqrt(ms + EPS) * g


MOD_TN = 1536
MOD_ROWS = 3


def _mod_kernel(ct_ref, w_ref, b_ref, o_ref):
    ct = ct_ref[...]
    st = ct * jax.nn.sigmoid(ct)
    w = w_ref[0]
    o_ref[...] = jnp.zeros(o_ref.shape, F32)
    for r in range(MOD_ROWS):
        o_ref[0, r:r + 1, :] = jnp.sum(w * st[:, r:r + 1], axis=0, keepdims=True) + b_ref[0]


def _mod_table(cvecs_t, w_ada, b_ada):
    n = 6 * D_MODEL
    return pl.pallas_call(
        _mod_kernel,
        grid=(DEPTH, n // MOD_TN),
        in_specs=[pl.BlockSpec((D_MODEL, 8), lambda l, j: (0, 0)),
                  pl.BlockSpec((1, D_MODEL, MOD_TN), lambda l, j: (l, 0, j)),
                  pl.BlockSpec((1, 1, MOD_TN), lambda l, j: (l, 0, j))],
        out_specs=pl.BlockSpec((1, 8, MOD_TN), lambda l, j: (l, 0, j)),
        out_shape=jax.ShapeDtypeStruct((DEPTH, 8, n), F32),
        compiler_params=_cparams(("arbitrary", "arbitrary")),
        name="adaln_mod",
    )(cvecs_t, w_ada, b_ada.reshape(DEPTH, 1, n))


IN_TM = 512


def _swap16(y):
    lane = lax.broadcasted_iota(jnp.int32, y.shape, 1)
    first = (lane % 32) < 16
    return jnp.where(first, pltpu.roll(y, LANES - 16, 1), pltpu.roll(y, 16, 1))


def _seg_pair_specs(tm, width, lat_row0=0):
    nt = SEG // tm
    lat_off = lat_row0 // tm
    return (pl.BlockSpec((tm, width), lambda s, i: (jnp.minimum(s * nt + i, nt - 1), 0)),
            pl.BlockSpec((tm, width), lambda s, i: (lat_off + jnp.maximum(s * nt + i - nt, 0), 0)))


TOK_TILE = D_MODEL // LANES


def _store_token_tiles(ref, x):
    for s in range(TOK_TILE):
        ref[pl.ds(s, x.shape[0], stride=TOK_TILE), :] = x[:, LANES * s:LANES * (s + 1)]


def _load_token_tiles(ref, n_tok):
    return jnp.concatenate([ref[pl.ds(s, n_tok, stride=TOK_TILE), :] for s in range(TOK_TILE)], axis=1)


def _pick(c_ref, l_ref):
    return jnp.where(pl.program_id(0) == 0, c_ref[...], l_ref[...])


def _x_pair(x):
    return (x, 0) if isinstance(x, tuple) else ((x, x), SEG)


def _inproj_kernel(xc_ref, xl_ref, mod_ref, g_ref, w_ref, qkg_ref, cos_ref, sin_ref, o_ref):
    seg = pl.program_id(0)
    mod = mod_ref[0]
    sh1 = mod[:, 0:D_MODEL]
    sc1 = mod[:, D_MODEL:2 * D_MODEL]
    h = _rms(_pick(xc_ref, xl_ref), g_ref[...]) * (1.0 + sc1) + sh1
    r = jnp.dot(h.astype(BF16), w_ref[...], preferred_element_type=F32)
    o_ref[:, 0:C_SQ] = r[:, 0:C_SQ]
    o_ref[:, C_RX:C_NQ] = r[:, C_RX:C_NQ]
    o_ref[:, C_NV:C_SK] = r[:, C_NV:C_SK]
    o_ref[:, C_SV:P_W] = r[:, C_SV:P_W]
    cos = cos_ref[...]
    sin = sin_ref[...]
    latent = seg > 0

    def rope(y):
        return jnp.where(latent, y * cos + _swap16(y) * sin, y)

    for p in range(2):
        a = C_SQ + LANES * p
        o_ref[:, a:a + LANES] = rope(_rms_head_pairs(r[:, a:a + LANES], qkg_ref[0:1, :]))
    o_ref[:, C_SK:C_SK + LANES] = rope(_rms_head_pairs(r[:, C_SK:C_SK + LANES], qkg_ref[1:2, :]))
    for p in range(2):
        a = C_NQ + LANES * p
        o_ref[:, a:a + LANES] = _rms_head_pairs(r[:, a:a + LANES], qkg_ref[2:3, :])
        a = C_NK + LANES * p
        o_ref[:, a:a + LANES] = _rms_head_pairs(r[:, a:a + LANES], qkg_ref[3:4, :])


def _inproj(x, mod_l, norm1, w_in_p, qk_gains, cos_t, sin_t):
    nt = SEG // IN_TM
    x_pair, lat_row0 = _x_pair(x)
    return pl.pallas_call(
        _inproj_kernel,
        grid=(N_SEG, nt),
        in_specs=[*_seg_pair_specs(IN_TM, D_MODEL, lat_row0),
                  pl.BlockSpec((1, 1, 6 * D_MODEL), lambda s, i: (s, 0, 0)),
                  pl.BlockSpec((1, D_MODEL), lambda s, i: (0, 0)),
                  pl.BlockSpec((D_MODEL, P_W), lambda s, i: (0, 0)),
                  pl.BlockSpec((4, LANES), lambda s, i: (0, 0)),
                  pl.BlockSpec((IN_TM, LANES), lambda s, i: (i, 0)),
                  pl.BlockSpec((IN_TM, LANES), lambda s, i: (i, 0))],
        out_specs=pl.BlockSpec((IN_TM, P_W), lambda s, i: (s * nt + i, 0)),
        out_shape=jax.ShapeDtypeStruct((N_TOK, P_W), F32),
        compiler_params=_cparams(("arbitrary", "arbitrary")),
        name="inproj",
    )(*x_pair, mod_l, norm1, w_in_p, qk_gains, cos_t, sin_t)


def _mlstm_direction(d, rows, q_ref, k_ref, v_ref, g_ref, bias, tri_ref, sel_ref, state):
    ch = ML_CHUNK
    r_io = lax.broadcasted_iota(jnp.int32, (ch, ch), 0)
    c_io = lax.broadcasted_iota(jnp.int32, (ch, ch), 1)
    lower = r_io >= c_io
    upper = r_io <= c_io
    mask = lower if d == 0 else upper
    tri = tri_ref[d]
    tri_t = tri_ref[1 - d]
    left = c_io < HEAD_DIM
    top = r_io < HEAD_DIM
    blockdiag = top == left
    cbs_in, nbs_in, m_old = state
    g = g_ref[rows, :] + bias[...]
    b_cols = _dot_exact_rhs(tri, jax.nn.log_sigmoid(g))
    b3 = jnp.concatenate(_split3(b_cols), axis=1)
    gt = g.T
    li_rows = gt[0:8, :]
    b_rows = _dot_exact_lhs(jax.nn.log_sigmoid(gt[8:16, :]), tri_t)
    a_rows = li_rows - b_rows
    bl = b_rows[:, ch - 1:ch] if d == 0 else b_rows[:, 0:1]
    g_rows = bl - b_rows + li_rows
    m_new = jnp.maximum(bl + m_old, jnp.max(g_rows, axis=1, keepdims=True))
    wk_rows = jnp.exp(g_rows - m_new)
    wp = jnp.exp(bl + m_old - m_new)
    ones_blk = jnp.ones((ch, LANES), BF16)
    left2 = lax.broadcasted_iota(jnp.int32, (ch, 2 * LANES), 1) % LANES < HEAD_DIM
    h_out, c_out, n_out = [], [], []
    for p in range(ML_HEADS // 2):
        lanes = slice(LANES * p, LANES * (p + 1))
        j0 = ML_HEADS * d + 2 * p
        q2 = q_ref[rows, lanes]
        k2t = (k_ref[rows, lanes] * SCALE).T.astype(BF16)
        v2e = jnp.concatenate([v_ref[rows, lanes].astype(BF16), ones_blk], axis=1)
        cb = cbs_in[p]
        nb = nbs_in[p]
        q2b = q2.astype(BF16)
        q_lo = (q2 - q2b.astype(F32)).astype(BF16)
        nb_hi = nb.astype(BF16)
        nb_lo = (nb - nb_hi.astype(F32)).astype(BF16)
        qc = jnp.dot(q2b, cb.astype(BF16), preferred_element_type=F32)
        qn = jnp.dot(jnp.concatenate([q2b, q_lo, q2b], axis=1), jnp.concatenate([nb_hi, nb_hi, nb_lo], axis=0),
                     preferred_element_type=F32)
        b_pair = jnp.dot(b3, sel_ref[2 * d + p], preferred_element_type=F32)
        cbs, sves = [], []
        for i in range(2):
            j = j0 + i
            half = left if i == 0 else jnp.logical_not(left)
            a_mat = jnp.where(mask, a_rows[j:j + 1, :], NEG)
            cvec = jnp.maximum(m_old[j:j + 1, :], jnp.max(a_mat, axis=1, keepdims=True))
            cbro = jnp.broadcast_to(cvec, (ch, ch))
            s = jnp.dot(jnp.where(half, q2b, 0), k2t, preferred_element_type=F32) * jnp.exp(a_mat - cbro)
            s_hi = s.astype(BF16)
            s_lo = (s - s_hi.astype(F32)).astype(BF16)
            sve = jnp.dot(s_hi, v2e, preferred_element_type=F32)
            rs_lo = jnp.dot(s_lo, ones_blk, preferred_element_type=F32)
            sves.append(jnp.concatenate([sve[:, 0:LANES], sve[:, LANES:2 * LANES] + rs_lo], axis=1))
            cbs.append(cbro)
        c_pair = jnp.where(left, cbs[0], cbs[1])
        w_prev = jnp.exp(jnp.where(left, m_old[j0:j0 + 1, :], m_old[j0 + 1:j0 + 2, :]) - c_pair)
        sve = jnp.where(left2, sves[0], sves[1])
        num = w_prev * qc + sve[:, 0:LANES]
        den = w_prev * qn + sve[:, LANES:2 * LANES]
        h_out.append(num / jnp.maximum(jnp.abs(den), jnp.exp(-(c_pair + b_pair))))
        kwt = k2t * jnp.where(top, wk_rows[j0:j0 + 1, :], wk_rows[j0 + 1:j0 + 2, :])
        kwt_hi = kwt.astype(BF16)
        kwt_lo = (kwt - kwt_hi.astype(F32)).astype(BF16)
        kve = jnp.dot(kwt_hi, v2e, preferred_element_type=F32)
        kn = kve[:, LANES:2 * LANES] + jnp.dot(kwt_lo, ones_blk, preferred_element_type=F32)
        wp_pair = jnp.where(top, wp[j0:j0 + 1, :], wp[j0 + 1:j0 + 2, :])
        c_out.append(wp_pair * cb + jnp.where(blockdiag, kve[:, 0:LANES], 0.0))
        n_out.append(wp_pair * nb + jnp.where(blockdiag, kn, 0.0))
    return h_out, (c_out, n_out, m_new)


def _mlstm_kernel(qf, kf, vf, gf, qb, kb, vb, gb, c0, n0, m0, bias, tri_ref, sel_ref,
                  hf, hb, co, no, mo, cbd, nbd, m_s, *, nc):
    c = pl.program_id(1)

    @pl.when(c == 0)
    def _():
        cbd[...] = c0[0]
        nbd[...] = n0[0]
        m_s[...] = m0[0]

    npair = ML_HEADS // 2
    for d, (refs, h_ref) in enumerate((((qf, kf, vf, gf), hf), ((qb, kb, vb, gb), hb))):
        state = ([cbd[npair * d + p] for p in range(npair)], [nbd[npair * d + p] for p in range(npair)],
                 m_s[:, 0:1])
        subs = range(ML_SUB) if d == 0 else reversed(range(ML_SUB))
        for sub in subs:
            rows = slice(sub * ML_CHUNK, (sub + 1) * ML_CHUNK)
            h_out, state = _mlstm_direction(d, rows, *refs, bias, tri_ref, sel_ref, state)
            for p in range(npair):
                h_ref[rows, LANES * p:LANES * (p + 1)] = h_out[p]
        c_out, n_out, m_new = state
        for p in range(npair):
            cbd[npair * d + p] = c_out[p]
            nbd[npair * d + p] = n_out[p]
        heads = slice(ML_HEADS * d, ML_HEADS * (d + 1))
        m_s[heads, :] = jnp.broadcast_to(m_new[heads, :], (ML_HEADS, LANES))

    @pl.when(c == nc - 1)
    def _():
        co[0] = cbd[...]
        no[0] = nbd[...]
        mo[0] = m_s[...]


def _mlstm_pack_state(c0, n0, m0):
    bsz = c0.shape[0]
    hd = HEAD_DIM
    eye = jnp.eye(2, dtype=F32)[None, None, :, None, :, None]
    cbd = c0.reshape(bsz, ML_HEADS, 2, hd, 1, hd) * eye
    nbd = jnp.broadcast_to(n0.reshape(bsz, ML_HEADS, 2, hd, 1, 1) * eye, cbd.shape)
    to_mat = lambda t: t.reshape(bsz, ML_HEADS, LANES, LANES)
    return to_mat(cbd), to_mat(nbd), jnp.broadcast_to(m0[..., None], m0.shape + (LANES,))


def _mlstm_unpack_state(cbd, nbd, mrow):
    hd = HEAD_DIM
    bsz = cbd.shape[0]
    c = jnp.stack([cbd[:, :, :hd, :hd], cbd[:, :, hd:, hd:]], axis=2).reshape(bsz, 2 * ML_HEADS, hd, hd)
    n = jnp.stack([nbd[:, :, :hd, 0], nbd[:, :, hd:, hd]], axis=2).reshape(bsz, 2 * ML_HEADS, hd)
    return c, n, mrow[:, :, 0]


def _mlstm_tables():
    r = np.arange(ML_CHUNK)
    lower = (r[:, None] >= r[None, :]).astype(np.float32)
    row = np.arange(3 * LANES)[:, None] % LANES
    lane_left = np.arange(LANES)[None, :] < HEAD_DIM
    sel = [row == np.where(lane_left, 8 + ML_HEADS * d + 2 * p, 9 + ML_HEADS * d + 2 * p)
           for d in range(2) for p in range(ML_HEADS // 2)]
    return (jnp.asarray(np.stack([lower, lower.T]), dtype=BF16),
            jnp.asarray(np.stack(sel).astype(np.float32), dtype=BF16))


def _mlstm(proj, gate_bias, c0, n0, m0, *, bsz, seq, row0):
    blk = ML_SUB * ML_CHUNK
    nc = seq // blk
    tri, sel = _mlstm_tables()
    base = row0 // blk
    nj = 2 * ML_HEADS
    fw = lambda col: (lambda b, c: (base + b * nc + c, col))
    bw = lambda col: (lambda b, c: (base + b * nc + nc - 1 - c, col))
    qkv = lambda f: [pl.BlockSpec((blk, 256), f(C_MQ // 256)),
                     pl.BlockSpec((blk, 256), f(C_MK // 256)),
                     pl.BlockSpec((blk, 256), f(C_MV // 256)),
                     pl.BlockSpec((blk, LANES), f(C_G // LANES))]
    st_specs = [pl.BlockSpec((1, nj // 2, LANES, LANES), lambda b, c: (b, 0, 0, 0)),
                pl.BlockSpec((1, nj // 2, LANES, LANES), lambda b, c: (b, 0, 0, 0)),
                pl.BlockSpec((1, nj, LANES), lambda b, c: (b, 0, 0))]
    st_shapes = [jax.ShapeDtypeStruct((bsz, nj // 2, LANES, LANES), F32),
                 jax.ShapeDtypeStruct((bsz, nj // 2, LANES, LANES), F32),
                 jax.ShapeDtypeStruct((bsz, nj, LANES), F32)]
    return pl.pallas_call(
        functools.partial(_mlstm_kernel, nc=nc),
        grid=(bsz, nc),
        in_specs=qkv(fw) + qkv(bw) + st_specs + [pl.BlockSpec((1, LANES), lambda b, c: (0, 0)),
                                                 pl.BlockSpec(tri.shape, lambda b, c: (0, 0, 0)),
                                                 pl.BlockSpec(sel.shape, lambda b, c: (0, 0, 0))],
        out_specs=[pl.BlockSpec((blk, 256), lambda b, c: (b * nc + c, 0)),
                   pl.BlockSpec((blk, 256), lambda b, c: (b * nc + nc - 1 - c, 0))] + st_specs,
        out_shape=[jax.ShapeDtypeStruct((bsz * seq, 256), F32),
                   jax.ShapeDtypeStruct((bsz * seq, 256), F32)] + st_shapes,
        scratch_shapes=[pltpu.VMEM((nj // 2, LANES, LANES), F32),
                        pltpu.VMEM((nj // 2, LANES, LANES), F32),
                        pltpu.VMEM((nj, LANES), F32)],
        compiler_params=_cparams(("arbitrary", "arbitrary")),
        name="mlstm",
    )(proj, proj, proj, proj, proj, proj, proj, proj, c0, n0, m0, gate_bias, tri, sel)


RG_TC = 256
RG_PAD = 8
RG_NSEG = 8
RG_SKEW = 4


def _rglru_kernel(rx, ry, cw, cb, wg, bg, lam, h0, oc, hl, xpad, af, ab, uf, ub, *, seg_len, chained):
    rows = RG_NSEG * seg_len
    seq_len = rows if chained else seg_len
    halves = RG_WIDTH // LANES
    pitch = seg_len + RG_SKEW
    buf_row = lambda t: (t // seg_len) * pitch + t % seg_len
    xpad[0:RG_PAD, :] = jnp.zeros((RG_PAD, RG_WIDTH), F32)
    xpad[rows + RG_PAD:rows + 2 * RG_PAD, :] = jnp.zeros((RG_PAD, RG_WIDTH), F32)
    xpad[RG_PAD:rows + RG_PAD, :] = rx[...]
    sp = jax.nn.softplus(-lam[...])
    left = (RG_CONV - 1) // 2
    for ci in range(rows // RG_TC):
        s0 = ci * RG_TC
        pos = (s0 + lax.broadcasted_iota(jnp.int32, (RG_TC, RG_WIDTH), 0)) % seq_len
        xc = None
        for j in range(RG_CONV):
            a = RG_PAD + s0 + j - left
            term = xpad[a:a + RG_TC, :] * cw[j:j + 1, :]
            if not chained and j != left:
                term = jnp.where((pos + (j - left) >= 0) & (pos + (j - left) < seq_len), term, 0.0)
            xc = term if xc is None else xc + term
        xc = xc + cb[...]
        pre = _dot(xc, wg[...]) + bg[...]
        for d, (a_ref, u_ref) in enumerate(((af, uf), (ab, ub))):
            o = 2 * RG_WIDTH * d
            r = _sigmoid(pre[:, o:o + RG_WIDTH])
            gi = _sigmoid(pre[:, o + RG_WIDTH:o + 2 * RG_WIDTH])
            log_a = -RG_C * r * sp[d:d + 1, :]
            a_val = jnp.exp(log_a)
            u_val = jnp.sqrt(-jnp.tanh(log_a) * (a_val * a_val + 1.0)) * (gi * xc)
            for hv in range(halves):
                dst = slice(buf_row(s0), buf_row(s0) + RG_TC)
                a_ref[hv, dst, :] = a_val[:, LANES * hv:LANES * (hv + 1)]
                u_ref[hv, dst, :] = u_val[:, LANES * hv:LANES * (hv + 1)]

    def body(s, carry):
        out = []
        for d, (a_ref, u_ref) in enumerate(((af, uf), (ab, ub))):
            step_rows = pl.ds(s if d == 0 else seg_len - 1 - s, RG_NSEG, stride=pitch)
            for hv in range(halves):
                h_loc, prod = carry[2 * (halves * d + hv)], carry[2 * (halves * d + hv) + 1]
                a = a_ref[hv, step_rows, :]
                h_loc = a * h_loc + u_ref[hv, step_rows, :]
                prod = a * prod
                u_ref[hv, step_rows, :] = h_loc
                a_ref[hv, step_rows, :] = prod
                out += [h_loc, prod]
        return tuple(out)

    zero = jnp.zeros((RG_NSEG, LANES), F32)
    one = jnp.ones((RG_NSEG, LANES), F32)
    ends = lax.fori_loop(0, seg_len, body, (zero, one) * (2 * halves), unroll=8)
    seg = lax.broadcasted_iota(jnp.int32, (RG_NSEG, LANES), 0)
    for d, (a_ref, u_ref) in enumerate(((af, uf), (ab, ub))):
        for hv in range(halves):
            lanes = slice(LANES * hv, LANES * (hv + 1))
            h_end, p_end = ends[2 * (halves * d + hv)], ends[2 * (halves * d + hv) + 1]
            h_in = h0[0, d][:, lanes]
            if chained:
                c = h_in[0:1, :]
                h_in = zero
                for k in (range(RG_NSEG) if d == 0 else reversed(range(RG_NSEG))):
                    h_in = jnp.where(seg == k, c, h_in)
                    c = h_end[k:k + 1, :] + p_end[k:k + 1, :] * c
            hl[0, d, :, lanes] = h_end + p_end * h_in
            for k in range(RG_NSEG):
                for ci in range(seg_len // RG_TC):
                    sl = slice(k * pitch + ci * RG_TC, k * pitch + (ci + 1) * RG_TC)
                    u_ref[hv, sl, :] = u_ref[hv, sl, :] + a_ref[hv, sl, :] * h_in[k:k + 1, :]
    for ci in range(rows // RG_TC):
        sl = slice(ci * RG_TC, (ci + 1) * RG_TC)
        for hv in range(halves):
            lanes = slice(LANES * hv, LANES * (hv + 1))
            src = slice(buf_row(ci * RG_TC), buf_row(ci * RG_TC) + RG_TC)
            oc[sl, lanes] = (uf[hv, src, :] + ub[hv, src, :]) * jax.nn.gelu(ry[sl, lanes])


def _rglru(proj, cw, cb, wg, bg, lam, h0, *, nblk, seg_len, chained, row0):
    rows = RG_NSEG * seg_len
    base = row0 // rows
    full = lambda shape: pl.BlockSpec(shape, lambda b: tuple(0 for _ in shape))
    st_spec = pl.BlockSpec((1, 2, RG_NSEG, RG_WIDTH), lambda b: (b, 0, 0, 0))
    return pl.pallas_call(
        functools.partial(_rglru_kernel, seg_len=seg_len, chained=chained),
        grid=(nblk,),
        in_specs=[pl.BlockSpec((rows, RG_WIDTH), lambda b: (base + b, C_RX // RG_WIDTH)),
                  pl.BlockSpec((rows, RG_WIDTH), lambda b: (base + b, C_RY // RG_WIDTH)),
                  full((RG_CONV, RG_WIDTH)), full((1, RG_WIDTH)),
                  full((RG_WIDTH, 4 * RG_WIDTH)), full((1, 4 * RG_WIDTH)), full((2, RG_WIDTH)),
                  st_spec],
        out_specs=[pl.BlockSpec((rows, RG_WIDTH), lambda b: (b, 0)), st_spec],
        out_shape=[jax.ShapeDtypeStruct((nblk * rows, RG_WIDTH), F32),
                   jax.ShapeDtypeStruct((nblk, 2, RG_NSEG, RG_WIDTH), F32)],
        scratch_shapes=[pltpu.VMEM((rows + 2 * RG_PAD, RG_WIDTH), F32)]
        + [pltpu.VMEM((RG_WIDTH // LANES, RG_NSEG * (seg_len + RG_SKEW), LANES), F32) for _ in range(4)],
        compiler_params=_cparams(("arbitrary",)),
        name="rglru",
    )(proj, proj, cw, cb, wg, bg, lam, h0)


def _softmax_pv(scores, values, sink):
    m = functools.reduce(jnp.maximum, [jnp.max(s, axis=-1, keepdims=True) for s in scores])
    if sink is not None:
        m = jnp.maximum(m, sink)
    ps = [jnp.exp(s - m) for s in scores]
    den = functools.reduce(jnp.add, [jnp.sum(p, axis=-1, keepdims=True) for p in ps])
    if sink is not None:
        den = den + jnp.exp(sink - m)
    num = functools.reduce(jnp.add, [_dot(p, v) for p, v in zip(ps, values)])
    return num / den


def _ctx_attn_kernel(sink, sq, sk, sv, nq, nk, nv, ob, od):
    assert SW_HEADS == 4 and SW_KV_HEADS == 2
    left = lax.broadcasted_iota(jnp.int32, (SEQ, LANES), 1) < HEAD_DIM
    first = lax.broadcasted_iota(jnp.int32, (2 * SEQ, 1), 0) < SEQ
    k2 = sk[...].astype(BF16)
    v2 = sv[...].astype(BF16)
    for p in range(SW_KV_HEADS):
        lanes = slice(LANES * p, LANES * (p + 1))
        q2 = sq[:, lanes]
        q_swapped = pltpu.roll(q2, HEAD_DIM, 1)
        kv_half = left if p == 0 else jnp.logical_not(left)
        qs = jnp.concatenate([jnp.where(kv_half, q2 if i == p else q_swapped, 0.0) for i in range(2)], axis=0)
        sink_col = jnp.where(first, sink[2 * p], sink[2 * p + 1])
        res = _softmax_pv([_dot_nt(qs, k2) * SCALE], [v2], sink_col)
        halves = [res[0:SEQ], res[SEQ:2 * SEQ]]
        placed = [halves[i] if i == p else pltpu.roll(halves[i], HEAD_DIM, 1) for i in range(2)]
        ob[:, lanes] = jnp.where(left, placed[0], placed[1])
    for p in range(NA_HEADS // 2):
        lanes = slice(LANES * p, LANES * (p + 1))
        q2 = nq[:, lanes]
        k2 = nk[:, lanes].astype(BF16)
        v2 = nv[:, lanes].astype(BF16)
        res = [_softmax_pv([_dot_nt(jnp.where(left if i == 0 else jnp.logical_not(left), q2, 0.0), k2) * SCALE],
                           [v2], None) for i in range(2)]
        od[:, lanes] = jnp.where(left, res[0], res[1])


def _ctx_attn(proj, sink):
    blk = lambda w, col: pl.BlockSpec((SEQ, w), lambda b: (b, col))
    return pl.pallas_call(
        _ctx_attn_kernel,
        grid=(BATCH,),
        in_specs=[pl.BlockSpec(memory_space=pltpu.SMEM),
                  blk(256, C_SQ // 256), blk(128, C_SK // 128), blk(128, C_SV // 128),
                  blk(256, C_NQ // 256), blk(256, C_NK // 256), blk(256, C_NV // 256)],
        out_specs=[pl.BlockSpec((SEQ, 256), lambda b: (b, 0)),
                   pl.BlockSpec((SEQ, 256), lambda b: (b, 0))],
        out_shape=[jax.ShapeDtypeStruct((SEG, 256), F32), jax.ShapeDtypeStruct((SEG, 256), F32)],
        compiler_params=_cparams(("arbitrary",)),
        name="ctx_attn",
    )(sink, proj, proj, proj, proj, proj, proj)


SW_QB = 128
SW_SPAN = SW_QB + 2 * SW_WINDOW


def _swa_kernel(sink, q, k, v, kc, vc, ob):
    assert SW_HEADS == 4 and SW_KV_HEADS == 2
    n = pl.program_id(1)
    ws = jnp.clip((n - 1) * SW_QB, 0, DEC_SEQ - SW_SPAN)
    ws = pl.multiple_of(ws, SW_QB)
    row = lax.broadcasted_iota(jnp.int32, (2 * SW_QB, SW_SPAN), 0)
    qpos = n * SW_QB + row % SW_QB
    kpos = ws + lax.broadcasted_iota(jnp.int32, (2 * SW_QB, SW_SPAN), 1)
    valid = jnp.abs(qpos - kpos) <= SW_WINDOW
    left = lax.broadcasted_iota(jnp.int32, (SW_QB, LANES), 1) < HEAD_DIM
    first = lax.broadcasted_iota(jnp.int32, (2 * SW_QB, 1), 0) < SW_QB
    k2 = k[pl.ds(ws, SW_SPAN), :].astype(BF16)
    v2 = v[pl.ds(ws, SW_SPAN), :].astype(BF16)
    kc2 = kc[0].astype(BF16)
    vc2 = vc[0].astype(BF16)
    for p in range(SW_KV_HEADS):
        lanes = slice(LANES * p, LANES * (p + 1))
        q2 = q[:, lanes]
        q_swapped = pltpu.roll(q2, HEAD_DIM, 1)
        kv_half = left if p == 0 else jnp.logical_not(left)
        qs = jnp.concatenate([jnp.where(kv_half, q2 if i == p else q_swapped, 0.0) for i in range(2)], axis=0)
        s_loc = jnp.where(valid, _dot_nt(qs, k2) * SCALE, NEG)
        s_ctx = _dot_nt(qs, kc2) * SCALE
        sink_col = jnp.where(first, sink[2 * p], sink[2 * p + 1])
        res = _softmax_pv([s_loc, s_ctx], [v2, vc2], sink_col)
        halves = [res[0:SW_QB], res[SW_QB:2 * SW_QB]]
        placed = [halves[i] if i == p else pltpu.roll(halves[i], HEAD_DIM, 1) for i in range(2)]
        ob[:, lanes] = jnp.where(left, placed[0], placed[1])


def _swa(proj, kc, vc, sink):
    nq = DEC_SEQ // SW_QB
    qbase = SEG // SW_QB
    return pl.pallas_call(
        _swa_kernel,
        grid=(DEC_BATCH, nq),
        in_specs=[pl.BlockSpec(memory_space=pltpu.SMEM),
                  pl.BlockSpec((SW_QB, 256), lambda b, n: (qbase + b * nq + n, C_SQ // 256)),
                  pl.BlockSpec((DEC_SEQ, 128), lambda b, n: (1 + b, C_SK // 128)),
                  pl.BlockSpec((DEC_SEQ, 128), lambda b, n: (1 + b, C_SV // 128)),
                  pl.BlockSpec((1, PAST_LEN, 128), lambda b, n: (b, 0, 0)),
                  pl.BlockSpec((1, PAST_LEN, 128), lambda b, n: (b, 0, 0))],
        out_specs=pl.BlockSpec((SW_QB, 256), lambda b, n: (b * nq + n, 0)),
        out_shape=jax.ShapeDtypeStruct((DEC_BATCH * DEC_SEQ, 256), F32),
        compiler_params=_cparams(("arbitrary", "arbitrary")),
        name="swa",
    )(sink, proj, proj, proj, kc, vc)


NA_RPB_R = 2 * NA_ROWS - 1
NA_RPB_C = 2 * NA_COLS - 1
GRID_ROWS = DEC_SEQ // GRID_W
NA_RB = 4
NA_UW = 12
NA_NQ = NA_RB * GRID_W
NA_NKEY = NA_UW * GRID_W
NA_NBLK = GRID_ROWS // NA_RB
NA_CASES = ((0, 0), (NA_RB, 0), (GRID_ROWS - NA_RB, GRID_ROWS - NA_UW))


def _na_row_start(qrow):
    return min(max(qrow - NA_ROWS // 2, 0), GRID_ROWS - NA_ROWS)


def _na_bias_kernel(rpb, out):
    h = pl.program_id(0)
    qc = lax.broadcasted_iota(jnp.int32, (GRID_W, GRID_W), 0)
    kc = lax.broadcasted_iota(jnp.int32, (GRID_W, GRID_W), 1)
    dc = jnp.clip(kc - qc, -(NA_COLS - 1), NA_COLS - 1) + NA_COLS - 1
    lo = jnp.clip(qc - NA_COLS // 2, 0, GRID_W - NA_COLS)
    valid = (kc >= lo) & (kc < lo + NA_COLS)
    tiles = []
    for dr in range(NA_RPB_R):
        t = jnp.zeros((GRID_W, GRID_W), F32)
        for j in range(NA_RPB_C):
            t = jnp.where(dc == j, rpb[(h * NA_RPB_R + dr) * NA_RPB_C + j], t)
        tiles.append(jnp.where(valid, t, NEG))
    outside = jnp.full((GRID_W, GRID_W), NEG, F32)
    for case, (q0, k0) in enumerate(NA_CASES):
        for a in range(NA_RB):
            r0 = _na_row_start(q0 + a)
            for i in range(NA_UW):
                inside = r0 <= k0 + i < r0 + NA_ROWS
                tile = tiles[k0 + i - (q0 + a) + NA_ROWS - 1] if inside else outside
                out[0, case, GRID_W * a:GRID_W * (a + 1), GRID_W * i:GRID_W * (i + 1)] = tile


def _na_bias(rpb_flat):
    shape = (NA_HEADS, len(NA_CASES), NA_NQ, NA_NKEY)
    return pl.pallas_call(
        _na_bias_kernel,
        grid=(NA_HEADS,),
        in_specs=[pl.BlockSpec(memory_space=pltpu.SMEM)],
        out_specs=pl.BlockSpec((1,) + shape[1:], lambda h: (h, 0, 0, 0)),
        out_shape=jax.ShapeDtypeStruct(shape, F32),
        compiler_params=_cparams(("arbitrary",)),
        name="na_bias",
    )(rpb_flat)


def _na_kernel(q, k, v, kc, vc, bias, od):
    blk = pl.program_id(1)
    case = jnp.where(blk == 0, 0, jnp.where(blk == NA_NBLK - 1, 2, 1))
    u0 = jnp.clip(blk * NA_RB - NA_ROWS // 2, 0, GRID_ROWS - NA_UW)
    k0 = pl.multiple_of(u0 * GRID_W, GRID_W)
    left = lax.broadcasted_iota(jnp.int32, (NA_NQ, LANES), 1) < HEAD_DIM
    for p in range(NA_HEADS // 2):
        lanes = slice(LANES * p, LANES * (p + 1))
        q2 = q[:, lanes]
        k2 = k[pl.ds(k0, NA_NKEY), lanes].astype(BF16)
        v2 = v[pl.ds(k0, NA_NKEY), lanes].astype(BF16)
        kc2 = kc[0, :, lanes].astype(BF16)
        vc2 = vc[0, :, lanes].astype(BF16)
        res = []
        for i in range(2):
            qm = jnp.where(left if i == 0 else jnp.logical_not(left), q2, 0.0)
            s_loc = _dot_nt(qm, k2) * SCALE + bias[2 * p + i, pl.ds(case, 1)][0]
            s_ctx = _dot_nt(qm, kc2) * SCALE
            res.append(_softmax_pv([s_loc, s_ctx], [v2, vc2], None))
        od[:, lanes] = jnp.where(left, res[0], res[1])


def _na(proj, kc, vc, bias):
    qbase = SEG // NA_NQ
    return pl.pallas_call(
        _na_kernel,
        grid=(DEC_BATCH, NA_NBLK),
        in_specs=[pl.BlockSpec((NA_NQ, 256), lambda b, r: (qbase + b * NA_NBLK + r, C_NQ // 256)),
                  pl.BlockSpec((DEC_SEQ, 256), lambda b, r: (1 + b, C_NK // 256)),
                  pl.BlockSpec((DEC_SEQ, 256), lambda b, r: (1 + b, C_NV // 256)),
                  pl.BlockSpec((1, PAST_LEN, 256), lambda b, r: (b, 0, 0)),
                  pl.BlockSpec((1, PAST_LEN, 256), lambda b, r: (b, 0, 0)),
                  pl.BlockSpec((NA_HEADS, len(NA_CASES), NA_NQ, NA_NKEY), lambda b, r: (0, 0, 0, 0))],
        out_specs=pl.BlockSpec((NA_NQ, 256), lambda b, r: (b * NA_NBLK + r, 0)),
        out_shape=jax.ShapeDtypeStruct((DEC_BATCH * DEC_SEQ, 256), F32),
        compiler_params=_cparams(("arbitrary", "arbitrary")),
        name="na",
    )(proj, proj, proj, kc, vc, bias)


MG_ROWS = 256
MG_SUB = 2
MG_TM = MG_SUB * MG_ROWS


def _merge_kernel(xc_ref, xl_ref, mod_ref, g1_ref, g2_ref, hf_c, hf_l, hb_c, hb_l, mo, ob_c, ob_l, oc_c, oc_l,
                  od_c, od_l, hn, wmg, bmg, wbr, wout, *rest, moe):
    if moe:
        wrt, br, x1_ref, h2_ref, route_ref = rest
    else:
        x1_ref, h2_ref = rest
    ctx = pl.program_id(0) == 0
    mod = mod_ref[0]
    chunk = lambda i: mod[:, i * D_MODEL:(i + 1) * D_MODEL]
    sh1, sc1, gate1, sh2, sc2 = chunk(0), chunk(1), chunk(2), chunk(3), chunk(4)

    def row_group(rows):
        pick = lambda c_ref, l_ref: jnp.where(ctx, c_ref[rows, :], l_ref[rows, :])
        x = pick(xc_ref, xl_ref)
        h = (_rms(x, g1_ref[...]) * (1.0 + sc1) + sh1).astype(BF16)
        hsum = pick(hf_c, hf_l) + pick(hb_c, hb_l)
        out_a = jnp.concatenate(
            [_rms_head_pairs(hsum[:, LANES * p:LANES * (p + 1)], hn[...]) for p in range(2)], axis=-1)
        out_a = out_a * jax.nn.sigmoid(mo[rows, :])
        acc = None
        for n, br_val in enumerate((out_a, pick(ob_c, ob_l), pick(oc_c, oc_l), pick(od_c, od_l))):
            gate = jax.nn.sigmoid(jnp.dot(h, wmg[:, n * D_MODEL:(n + 1) * D_MODEL].astype(BF16),
                                          preferred_element_type=F32)
                                  + bmg[:, n * D_MODEL:(n + 1) * D_MODEL])
            term = gate * jnp.dot(br_val.astype(BF16), wbr[n].astype(BF16), preferred_element_type=F32)
            acc = term if acc is None else acc + term
        y = jnp.dot(acc.astype(BF16), wout[...].astype(BF16), preferred_element_type=F32)
        x1 = x + gate1 * y
        x1_ref[rows, :] = x1
        h2 = _rms(x1, g2_ref[...]) * (1.0 + sc2) + sh2
        if moe:
            _store_token_tiles(h2_ref.at[pl.ds(rows.start * TOK_TILE, MG_ROWS * TOK_TILE)], h2)
        else:
            h2_ref[rows, :] = h2.astype(BF16)
        if moe:
            logit = [jnp.sum(h2 * wrt[e:e + 1, :], axis=-1, keepdims=True) + br[e] for e in range(N_EXPERTS)]
            v1, i1 = logit[0], jnp.zeros(logit[0].shape, jnp.int32)
            for e in range(1, N_EXPERTS):
                better = logit[e] > v1
                v1 = jnp.where(better, logit[e], v1)
                i1 = jnp.where(better, e, i1)
            v2, i2 = jnp.full(v1.shape, -jnp.inf, F32), jnp.zeros(v1.shape, jnp.int32)
            for e in range(N_EXPERTS):
                better = (i1 != e) & (logit[e] > v2)
                v2 = jnp.where(better, logit[e], v2)
                i2 = jnp.where(better, e, i2)
            e2 = jnp.exp(v2 - v1)
            den = 1.0 + e2
            lane = lax.broadcasted_iota(jnp.int32, (MG_ROWS, LANES), 1)
            route = jnp.where(lane == 0, 1.0 / den, 0.0) + jnp.where(lane == 1, e2 / den, 0.0)
            route = route + jnp.where(lane == 2, i1.astype(F32), 0.0) + jnp.where(lane == 3, i2.astype(F32), 0.0)
            route_ref[rows, :] = route

    for sub in range(MG_SUB):
        row_group(slice(sub * MG_ROWS, (sub + 1) * MG_ROWS))


def _merge(x, mod_l, g1, g2, hf, hb, proj, ob, oc, od, hn, wmg, bmg, wbr, wout, router=None):
    nt = SEG // MG_TM
    moe = router is not None
    x_pair, lat_row0 = _x_pair(x)
    row = lambda w: pl.BlockSpec((MG_TM, w), lambda s, i: (s * nt + i, 0))
    ctx_blk, lat_blk = _seg_pair_specs(MG_TM, 256)
    full = lambda shape: pl.BlockSpec(shape, lambda s, i: tuple(0 for _ in shape), pipeline_mode=pl.Buffered(1))
    in_specs = [*_seg_pair_specs(MG_TM, D_MODEL, lat_row0),
                pl.BlockSpec((1, 1, 6 * D_MODEL), lambda s, i: (s, 0, 0)),
                full((1, D_MODEL)), full((1, D_MODEL)),
                ctx_blk, lat_blk, ctx_blk, lat_blk,
                pl.BlockSpec((MG_TM, 256), lambda s, i: (s * nt + i, C_MO // 256)),
                ctx_blk, lat_blk, ctx_blk, lat_blk, ctx_blk, lat_blk,
                full((1, LANES)), full((D_MODEL, N_BRANCH * D_MODEL)), full((1, N_BRANCH * D_MODEL)),
                full((N_BRANCH, 256, D_MODEL)), full((D_MODEL, D_MODEL))]
    args = [*x_pair, mod_l, g1, g2, *hf, *hb, proj, *ob, *oc, *od, hn, wmg, bmg, wbr, wout]
    if moe:
        h2_spec = pl.BlockSpec((MG_TM * TOK_TILE, LANES), lambda s, i: (s * nt + i, 0))
        h2_shape = jax.ShapeDtypeStruct((N_TOK * TOK_TILE, LANES), F32)
    else:
        h2_spec, h2_shape = row(D_MODEL), jax.ShapeDtypeStruct((N_TOK, D_MODEL), BF16)
    out_specs = [row(D_MODEL), h2_spec]
    out_shape = [jax.ShapeDtypeStruct((N_TOK, D_MODEL), F32), h2_shape]
    if moe:
        in_specs += [full((N_EXPERTS, D_MODEL)), pl.BlockSpec(memory_space=pltpu.SMEM)]
        args += list(router)
        out_specs.append(row(LANES))
        out_shape.append(jax.ShapeDtypeStruct((N_TOK, LANES), F32))
    return pl.pallas_call(
        functools.partial(_merge_kernel, moe=moe),
        grid=(N_SEG, nt),
        in_specs=in_specs,
        out_specs=out_specs,
        out_shape=out_shape,
        compiler_params=_cparams(("arbitrary", "arbitrary")),
        name="merge",
    )(*args)


FF_TM = 512


def _ffn_kernel(h2, w1, w3, w2, x1, mod_ref, out):
    h = h2[...]
    a = jnp.dot(h, w1[...].astype(BF16), preferred_element_type=F32)
    b = jnp.dot(h, w3[...].astype(BF16), preferred_element_type=F32)
    act = (jax.nn.silu(a) * b).astype(BF16)
    gate2 = mod_ref[0][:, 5 * D_MODEL:6 * D_MODEL]
    out[...] = x1[...] + gate2 * jnp.dot(act, w2[...].astype(BF16), preferred_element_type=F32)


def _ffn(h2, x1, mod_l, w1, w3, w2):
    nt = N_TOK // FF_TM
    per_seg = SEG // FF_TM
    resident = lambda shape: pl.BlockSpec(shape, lambda i: (0, 0), pipeline_mode=pl.Buffered(1))
    return pl.pallas_call(
        _ffn_kernel,
        grid=(nt,),
        in_specs=[pl.BlockSpec((FF_TM, D_MODEL), lambda i: (i, 0)),
                  resident((D_MODEL, D_FF)), resident((D_MODEL, D_FF)), resident((D_FF, D_MODEL)),
                  pl.BlockSpec((FF_TM, D_MODEL), lambda i: (i, 0)),
                  pl.BlockSpec((1, 1, 6 * D_MODEL), lambda i: (i // per_seg, 0, 0))],
        out_specs=pl.BlockSpec((FF_TM, D_MODEL), lambda i: (i, 0)),
        out_shape=jax.ShapeDtypeStruct((N_TOK, D_MODEL), F32),
        compiler_params=_cparams(("arbitrary",)),
        name="ffn",
    )(h2, w1, w3, w2, x1, mod_l)


MOE_TM = 256
MOE_SLOTS = 2 * N_TOK
MOE_TILES = MOE_SLOTS // MOE_TM + N_EXPERTS
MOE_NBUF = 3
MOE_STEPS = MOE_TILES + MOE_NBUF
MOE_DUMP = MOE_NBUF * MOE_TM
MOE_LEAD = 1
MOE_PLAN_TILES = MOE_LEAD + MOE_TILES + 2
MOE_FCHUNKS = 1
MOE_UNROLL = 8


def _moe_group_kernel(texp, nused, src_tok, dst_row, h2_hbm, w1, w3, w2, y_hbm, xs, ys, sem_in, sem_out):
    del texp
    i = pl.program_id(0)
    n_used = nused[0]
    buf = i % MOE_NBUF
    buf_next = (i + 2) % MOE_NBUF

    def tile_rows(t):
        start = t * TOK_TILE
        return pl.ds(start if isinstance(start, int) else pl.multiple_of(start, TOK_TILE), TOK_TILE)

    def gather_copy(tile, b, r):
        tok = src_tok[(tile + MOE_LEAD) * MOE_TM + r]
        return pltpu.make_async_copy(h2_hbm.at[tile_rows(tok)], xs.at[b, tile_rows(r)], sem_in.at[b])

    def scatter_copy(tile, b, r):
        dst = dst_row[(tile + MOE_LEAD) * MOE_TM + r]
        return pltpu.make_async_copy(ys.at[b, tile_rows(r)], y_hbm.at[tile_rows(dst)], sem_out.at[b])

    def start_rows_loop(make_copy, tile, b):
        def body(r, carry):
            make_copy(tile, b, r).start()
            return carry
        lax.fori_loop(0, MOE_TM, body, 0, unroll=MOE_UNROLL)

    def wait_tile(b, gather):
        if gather:
            pltpu.make_async_copy(h2_hbm.at[pl.ds(0, MOE_TM * TOK_TILE)], xs.at[b], sem_in.at[b]).wait()
        else:
            pltpu.make_async_copy(ys.at[b], y_hbm.at[pl.ds(0, MOE_TM * TOK_TILE)], sem_out.at[b]).wait()

    @pl.when(i == 0)
    def _():
        xs[...] = jnp.zeros(xs.shape, F32)
        ys[...] = jnp.zeros(ys.shape, F32)
        for b in range(MOE_NBUF):
            fill = pltpu.make_async_copy(ys.at[b], y_hbm.at[pl.ds((MOE_SLOTS + b * MOE_TM) * TOK_TILE, MOE_TM * TOK_TILE)],
                                         sem_out.at[b])
            fill.start()
            fill.wait()
        start_rows_loop(gather_copy, 0, 0)
        start_rows_loop(gather_copy, 1, 1)

    @pl.when(i <= n_used + 1)
    def _():
        wait_tile(buf, True)

    @pl.when((i >= 2) & (i <= n_used + 2))
    def _():
        wait_tile(buf, False)

    @pl.when(i < n_used)
    def _():
        x = _load_token_tiles(xs.at[buf], MOE_TM).astype(BF16)
        fc = D_FF_EXPERT // MOE_FCHUNKS
        rc = MOE_TM // MOE_FCHUNKS
        y = None
        for c in range(MOE_FCHUNKS):
            for r in range(c * rc, (c + 1) * rc):
                gather_copy(i + 2, buf_next, r).start()
                scatter_copy(i - 1, buf_next, r).start()
            a = jnp.dot(x, w1[0, :, c * fc:(c + 1) * fc].astype(BF16), preferred_element_type=F32)
            b = jnp.dot(x, w3[0, :, c * fc:(c + 1) * fc].astype(BF16), preferred_element_type=F32)
            act = (jax.nn.silu(a) * b).astype(BF16)
            part = jnp.dot(act, w2[0, c * fc:(c + 1) * fc, :].astype(BF16), preferred_element_type=F32)
            y = part if y is None else y + part
        _store_token_tiles(ys.at[buf], y)

    @pl.when(i == n_used)
    def _():
        start_rows_loop(scatter_copy, i - 1, buf_next)


def _moe_group(tile_expert, n_used, src_tok, dst_row, h2, w1, w3, w2):
    wspec = lambda shape: pl.BlockSpec((1,) + shape, lambda i, texp, *_: (texp[jnp.minimum(i, MOE_TILES - 1)], 0, 0))
    grid_spec = pltpu.PrefetchScalarGridSpec(
        num_scalar_prefetch=4,
        grid=(MOE_STEPS,),
        in_specs=[pl.BlockSpec(memory_space=pl.ANY),
                  wspec((D_MODEL, D_FF_EXPERT)), wspec((D_MODEL, D_FF_EXPERT)), wspec((D_FF_EXPERT, D_MODEL))],
        out_specs=pl.BlockSpec(memory_space=pl.ANY),
        scratch_shapes=[pltpu.VMEM((MOE_NBUF, MOE_TM * TOK_TILE, LANES), F32),
                        pltpu.VMEM((MOE_NBUF, MOE_TM * TOK_TILE, LANES), F32),
                        pltpu.SemaphoreType.DMA((MOE_NBUF,)), pltpu.SemaphoreType.DMA((MOE_NBUF,))])
    return pl.pallas_call(
        _moe_group_kernel,
        grid_spec=grid_spec,
        out_shape=jax.ShapeDtypeStruct(((MOE_SLOTS + MOE_DUMP) * TOK_TILE, LANES), F32),
        compiler_params=_cparams(("arbitrary",)),
        name="moe_group",
    )(tile_expert, n_used, src_tok, dst_row, h2, w1, w3, w2)


def _moe_plan(expert_ids):
    e_flat = expert_ids.T.reshape(-1)
    order = jnp.argsort(e_flat, stable=True).astype(jnp.int32)
    counts = jnp.sum((e_flat[:, None] == jnp.arange(N_EXPERTS)[None, :]).astype(jnp.int32), axis=0)
    padded = (counts + MOE_TM - 1) // MOE_TM * MOE_TM
    pend = jnp.cumsum(padded)
    pstart = pend - padded
    ustart = jnp.cumsum(counts) - counts
    n_used = pend[-1] // MOE_TM
    tiles = jnp.arange(MOE_TILES, dtype=jnp.int32)
    last_used = jnp.minimum(tiles, n_used - 1)
    tile_expert = jnp.sum((last_used[:, None] * MOE_TM >= pend[None, :]).astype(jnp.int32), axis=1)
    t = jnp.arange(-MOE_LEAD, MOE_PLAN_TILES - MOE_LEAD, dtype=jnp.int32)[:, None]
    r = jnp.arange(MOE_TM, dtype=jnp.int32)[None, :]
    e_t = tile_expert[jnp.clip(t, 0, MOE_TILES - 1)]
    off = t * MOE_TM + r - pstart[e_t]
    valid = (t >= 0) & (t < n_used) & (off < counts[e_t])
    slot = order[jnp.clip(ustart[e_t] + off, 0, MOE_SLOTS - 1)]
    src_tok = jnp.where(valid, slot % N_TOK, 0)
    dst_row = jnp.where(valid, slot, MOE_SLOTS + (t % MOE_NBUF) * MOE_TM + r)
    return (tile_expert.astype(jnp.int32), n_used.reshape(1).astype(jnp.int32),
            src_tok.reshape(-1).astype(jnp.int32), dst_row.reshape(-1).astype(jnp.int32))


def _moe_combine_kernel(x1, y0, y1, route, mod_ref, out_c, out_l):
    gate2 = mod_ref[0][:, 5 * D_MODEL:6 * D_MODEL]
    r = route[...]
    val = x1[...] + gate2 * (r[:, 0:1] * _load_token_tiles(y0, FF_TM) + r[:, 1:2] * _load_token_tiles(y1, FF_TM))
    is_ctx = pl.program_id(0) < SEG // FF_TM

    @pl.when(is_ctx)
    def _():
        out_c[...] = val

    @pl.when(jnp.logical_not(is_ctx))
    def _():
        out_l[...] = val


def _moe_combine(x1, y_slots, route, mod_l):
    nt = N_TOK // FF_TM
    per_seg = SEG // FF_TM
    return pl.pallas_call(
        _moe_combine_kernel,
        grid=(nt,),
        in_specs=[pl.BlockSpec((FF_TM, D_MODEL), lambda i: (i, 0)),
                  pl.BlockSpec((FF_TM * TOK_TILE, LANES), lambda i: (i, 0)),
                  pl.BlockSpec((FF_TM * TOK_TILE, LANES), lambda i: (nt + i, 0)),
                  pl.BlockSpec((FF_TM, LANES), lambda i: (i, 0)),
                  pl.BlockSpec((1, 1, 6 * D_MODEL), lambda i: (i // per_seg, 0, 0))],
        out_specs=[pl.BlockSpec((FF_TM, D_MODEL), lambda i: (jnp.minimum(i, per_seg - 1), 0)),
                   pl.BlockSpec((FF_TM, D_MODEL), lambda i: (jnp.maximum(i - per_seg, 0), 0))],
        out_shape=[jax.ShapeDtypeStruct((SEG, D_MODEL), F32), jax.ShapeDtypeStruct((N_TOK - SEG, D_MODEL), F32)],
        compiler_params=_cparams(("arbitrary",)),
        name="moe_combine",
    )(x1, y_slots, y_slots, route, mod_l)


def _rope_tables():
    t = np.arange(DEC_SEQ)
    row, col = (t // GRID_W).astype(np.float32), (t % GRID_W).astype(np.float32)
    nf = HEAD_DIM // 4
    freqs = np.float32(ROPE_BASE) ** (-np.arange(nf, dtype=np.float32) / np.float32(nf))
    lane = np.arange(LANES) % HEAD_DIM
    fidx = lane % nf
    use_col = (lane // (HEAD_DIM // 2)) == 1
    first = (lane % (HEAD_DIM // 2)) < nf
    pos = np.where(use_col[None, :], col[:, None], row[:, None])
    ang = (pos * freqs[fidx][None, :]).astype(np.float32).astype(np.float64)
    sin = np.sin(ang)
    return (jnp.asarray(np.cos(ang), dtype=F32), jnp.asarray(np.where(first[None, :], -sin, sin), dtype=F32))


def _permute_w_in(w):
    sizes = (256, 256, 256, 256, 8, 8, 256, 128, 128, 256, 256, 256, 256, 256)
    offs = np.concatenate([[0], np.cumsum(sizes)])
    part = lambda i: w[:, offs[i]:offs[i + 1]]
    mq, mk, mv, mo, mi, mf, sq, sk, sv, rx, ry, nq, nk, nv = (part(i) for i in range(14))
    pad = jnp.zeros((w.shape[0], LANES - 16), w.dtype)
    return jnp.concatenate([mq, mk, mv, mo, sq, rx, ry, nq, nk, nv, sk, sv, mi, mf, pad], axis=1)


def _block_diag(w):
    eye = jnp.eye(RG_BLOCKS, dtype=w.dtype)
    return (w[:, :, None, :] * eye[:, None, :, None]).reshape(RG_WIDTH, RG_WIDTH)


def _tile2(g):
    return jnp.concatenate([g, g]).reshape(1, LANES)


def kernel(x_prompt, x_sample, cache_swa_k, cache_swa_v, cache_na_k, cache_na_v, state_mlstm_C, state_mlstm_n, state_mlstm_m, state_rglru_h, c, c_ctx, norm1_g, norm2_g, w_ada, b_ada, w_in, ml_b_i, ml_b_f, ml_hn, sw_qn, sw_kn, sw_sink, rg_conv_w, rg_conv_b, rg_w_r, rg_b_r, rg_w_i, rg_b_i, rg_lam, na_qn, na_kn, na_rpb, w_br, w_mg, b_mg, w_out, ffn_w1, ffn_w3, ffn_w2, moe_wr, moe_br, moe_w1, moe_w3, moe_w2):
    assert DEPTH % 2 == 0
    x_all = (x_prompt.reshape(SEG, D_MODEL), x_sample.reshape(N_TOK - SEG, D_MODEL))
    cvecs = jnp.concatenate([c_ctx[None, :], c, jnp.zeros((8 - 1 - DEC_BATCH, D_MODEL), F32)], axis=0)
    mod = _mod_table(cvecs.T, w_ada, b_ada)
    cos_t, sin_t = _rope_tables()
    nj = 2 * ML_HEADS
    zeros_state = (jnp.zeros((BATCH, nj // 2, LANES, LANES), F32), jnp.zeros((BATCH, nj // 2, LANES, LANES), F32),
                   jnp.zeros((BATCH, nj, LANES), F32), jnp.zeros((BATCH // RG_NSEG, 2, RG_NSEG, RG_WIDTH), F32))
    ctx_out = []
    for l in range(DEPTH):
        mod_l = mod[l].reshape(8, 1, 6 * D_MODEL)
        qk_gains = jnp.stack([_tile2(sw_qn[l])[0], _tile2(sw_kn[l])[0], _tile2(na_qn[l])[0], _tile2(na_kn[l])[0]])
        proj = _inproj(x_all, mod_l, norm1_g[l].reshape(1, D_MODEL), _permute_w_in(w_in[l]).astype(BF16),
                       qk_gains, cos_t, sin_t)
        gate_bias = jnp.concatenate([ml_b_i[l].reshape(-1), ml_b_f[l].reshape(-1),
                                     jnp.zeros((LANES - 2 * nj,), F32)]).reshape(1, LANES)
        hf_c, hb_c, *st_new = _mlstm(proj, gate_bias, *zeros_state[:3], bsz=BATCH, seq=SEQ, row0=0)
        c_new, n_new, m_new = _mlstm_unpack_state(*st_new)
        st_lat = _mlstm_pack_state(state_mlstm_C[:, l].reshape(DEC_BATCH, nj, HEAD_DIM, HEAD_DIM),
                                   state_mlstm_n[:, l].reshape(DEC_BATCH, nj, HEAD_DIM),
                                   state_mlstm_m[:, l].reshape(DEC_BATCH, nj))
        hf_l, hb_l, _, _, _ = _mlstm(proj, gate_bias, *st_lat, bsz=DEC_BATCH, seq=DEC_SEQ, row0=SEG)
        wg = jnp.concatenate([_block_diag(rg_w_r[l, 0]), _block_diag(rg_w_i[l, 0]),
                              _block_diag(rg_w_r[l, 1]), _block_diag(rg_w_i[l, 1])], axis=1).astype(BF16)
        bg = jnp.concatenate([rg_b_r[l, 0], rg_b_i[l, 0], rg_b_r[l, 1], rg_b_i[l, 1]]).reshape(1, 4 * RG_WIDTH)
        rg_args = (rg_conv_w[l], rg_conv_b[l].reshape(1, RG_WIDTH), wg, bg, rg_lam[l])
        oc_c, hl_c = _rglru(proj, *rg_args, zeros_state[3], nblk=BATCH // RG_NSEG, seg_len=SEQ, chained=False, row0=0)
        hl_new = jnp.transpose(hl_c, (0, 2, 1, 3)).reshape(BATCH, 2, RG_WIDTH)
        h0_lat = jnp.broadcast_to(state_rglru_h[:, l][:, :, None, :], (DEC_BATCH, 2, RG_NSEG, RG_WIDTH))
        oc_l, _ = _rglru(proj, *rg_args, h0_lat, nblk=DEC_BATCH, seg_len=DEC_SEQ // RG_NSEG, chained=True, row0=SEG)
        ob_c, od_c = _ctx_attn(proj, sw_sink[l])
        ob_l = _swa(proj, cache_swa_k[:, l].reshape(DEC_BATCH, PAST_LEN, 128),
                    cache_swa_v[:, l].reshape(DEC_BATCH, PAST_LEN, 128), sw_sink[l])
        od_l = _na(proj, cache_na_k[:, l].reshape(DEC_BATCH, PAST_LEN, 256),
                   cache_na_v[:, l].reshape(DEC_BATCH, PAST_LEN, 256), _na_bias(na_rpb[l].reshape(-1)))
        moe_layer = l % 2 == 1
        j = l // 2
        router = (moe_wr[j].T, moe_br[j]) if moe_layer else None
        outs = _merge(x_all, mod_l, norm1_g[l].reshape(1, D_MODEL), norm2_g[l].reshape(1, D_MODEL),
                      (hf_c, hf_l), (hb_c, hb_l), proj, (ob_c, ob_l), (oc_c, oc_l), (od_c, od_l),
                      _tile2(ml_hn[l]), w_mg[l], b_mg[l].reshape(1, -1), w_br[l], w_out[l], router)
        if moe_layer:
            x1, h2, route = outs
            plan = _moe_plan(route[:, 2:4].astype(jnp.int32))
            y_slots = _moe_group(*plan, h2, moe_w1[j], moe_w3[j], moe_w2[j])
            x_all = tuple(_moe_combine(x1, y_slots, route, mod_l))
        else:
            x1, h2 = outs
            x_all = _ffn(h2, x1, mod_l, ffn_w1[j], ffn_w3[j], ffn_w2[j])
        pc = proj[:SEG]
        ctx_out.append(dict(
            sw_k=pc[:, C_SK:C_SK + 128].reshape(BATCH, SEQ, SW_KV_HEADS, HEAD_DIM),
            sw_v=pc[:, C_SV:C_SV + 128].reshape(BATCH, SEQ, SW_KV_HEADS, HEAD_DIM),
            na_k=pc[:, C_NK:C_NK + 256].reshape(BATCH, SEQ, NA_HEADS, HEAD_DIM),
            na_v=pc[:, C_NV:C_NV + 256].reshape(BATCH, SEQ, NA_HEADS, HEAD_DIM),
            ml_C=c_new.reshape(BATCH, 2, ML_HEADS, HEAD_DIM, HEAD_DIM),
            ml_n=n_new.reshape(BATCH, 2, ML_HEADS, HEAD_DIM),
            ml_m=m_new.reshape(BATCH, 2, ML_HEADS),
            rg_h=hl_new))
    stack = lambda name: jnp.stack([t[name] for t in ctx_out], axis=1)
    return (x_all[0].reshape(BATCH, SEQ, D_MODEL), x_all[1].reshape(DEC_BATCH, DEC_SEQ, D_MODEL),
            stack('sw_k'), stack('sw_v'), stack('na_k'), stack('na_v'),
            stack('ml_C'), stack('ml_n'), stack('ml_m'), stack('rg_h'))
```

```python
import functools

import numpy as np
import jax
import jax.numpy as jnp
from jax import lax
from jax.experimental import pallas as pl
from jax.experimental.pallas import tpu as pltpu

F32 = jnp.float32
BF16 = jnp.bfloat16

D_MODEL = 1024
BATCH = 16
SEQ = 256
DEPTH = 2
DEC_BATCH = 2
DEC_SEQ = 4096
PAST_LEN = 256
GRID_W = 64
HEAD_DIM = 64
ML_HEADS = 4
ML_CHUNK = 128
ML_SUB = 4
SW_HEADS = 4
SW_KV_HEADS = 2
SW_WINDOW = 128
RG_WIDTH = 256
RG_BLOCKS = 4
RG_CONV = 4
RG_C = 8.0
NA_HEADS = 4
NA_ROWS = 8
NA_COLS = 16
N_BRANCH = 4
ROPE_BASE = 10000.0
D_FF = 2816
N_EXPERTS = 8
D_FF_EXPERT = 2048
EPS = 1e-6
NEG = -1e30
SCALE = HEAD_DIM ** -0.5

SEG = 4096
N_SEG = 3
N_TOK = N_SEG * SEG
LANES = 128
VMEM_LIMIT = 60 * 1024 * 1024

C_MQ, C_MK, C_MV, C_MO = 0, 256, 512, 768
C_SQ, C_RX, C_RY, C_NQ, C_NK, C_NV = 1024, 1280, 1536, 1792, 2048, 2304
C_SK, C_SV, C_G = 2560, 2688, 2816
P_W = 2944


def _cparams(sem):
    return pltpu.CompilerParams(dimension_semantics=sem, vmem_limit_bytes=VMEM_LIMIT)


def _dot(a, b):
    return jnp.dot(a.astype(BF16), b.astype(BF16), preferred_element_type=F32)


def _dot_nt(a, b):
    return lax.dot_general(a.astype(BF16), b.astype(BF16), (((1,), (1,)), ((), ())),
                           preferred_element_type=F32)


def _dot_tn(a, b):
    return lax.dot_general(a.astype(BF16), b.astype(BF16), (((0,), (0,)), ((), ())),
                           preferred_element_type=F32)


def _split3(x):
    hi = x.astype(BF16)
    r1 = x - hi.astype(F32)
    mid = r1.astype(BF16)
    lo = (r1 - mid.astype(F32)).astype(BF16)
    return hi, mid, lo


def _dot_exact_rhs(a01, x):
    hi, mid, lo = _split3(x)
    d = lambda p: jnp.dot(a01, p, preferred_element_type=F32)
    return d(hi) + d(mid) + d(lo)


def _dot_exact_lhs(x, a01):
    hi, mid, lo = _split3(x)
    d = lambda p: jnp.dot(p, a01, preferred_element_type=F32)
    return d(hi) + d(mid) + d(lo)


def _sigmoid(x):
    return 0.5 * jnp.tanh(0.5 * x) + 0.5


def _rms(x, g):
    return x * lax.rsqrt(jnp.mean(x * x, axis=-1, keepdims=True) + EPS) * g


def _rms_head_pairs(x, g):
    lane = lax.broadcasted_iota(jnp.int32, x.shape, 1)
    left = lane < HEAD_DIM
    sq = x * x
    s0 = jnp.sum(jnp.where(left, sq, 0.0), axis=-1, keepdims=True)
    s1 = jnp.sum(jnp.where(left, 0.0, sq), axis=-1, keepdims=True)
    ms = jnp.where(left, s0, s1) * (1.0 / HEAD_DIM)
    return x * lax.rsqrt(ms + EPS) * g


MOD_TN = 1536
MOD_ROWS = 3


def _mod_kernel(ct_ref, w_ref, b_ref, o_ref):
    ct = ct_ref[...]
    st = ct * jax.nn.sigmoid(ct)
    w = w_ref[0]
    o_ref[...] = jnp.zeros(o_ref.shape, F32)
    for r in range(MOD_ROWS):
        o_ref[0, r:r + 1, :] = jnp.sum(w * st[:, r:r + 1], axis=0, keepdims=True) + b_ref[0]


def _mod_table(cvecs_t, w_ada, b_ada):
    n = 6 * D_MODEL
    return pl.pallas_call(
        _mod_kernel,
        grid=(DEPTH, n // MOD_TN),
        in_specs=[pl.BlockSpec((D_MODEL, 8), lambda l, j: (0, 0)),
                  pl.BlockSpec((1, D_MODEL, MOD_TN), lambda l, j: (l, 0, j)),
                  pl.BlockSpec((1, 1, MOD_TN), lambda l, j: (l, 0, j))],
        out_specs=pl.BlockSpec((1, 8, MOD_TN), lambda l, j: (l, 0, j)),
        out_shape=jax.ShapeDtypeStruct((DEPTH, 8, n), F32),
        compiler_params=_cparams(("arbitrary", "arbitrary")),
        name="adaln_mod",
    )(cvecs_t, w_ada, b_ada.reshape(DEPTH, 1, n))


IN_TM = 512


def _swap16(y):
    lane = lax.broadcasted_iota(jnp.int32, y.shape, 1)
    first = (lane % 32) < 16
    return jnp.where(first, pltpu.roll(y, LANES - 16, 1), pltpu.roll(y, 16, 1))


def _seg_pair_specs(tm, width, lat_row0=0):
    nt = SEG // tm
    lat_off = lat_row0 // tm
    return (pl.BlockSpec((tm, width), lambda s, i: (jnp.minimum(s * nt + i, nt - 1), 0)),
            pl.BlockSpec((tm, width), lambda s, i: (lat_off + jnp.maximum(s * nt + i - nt, 0), 0)))


TOK_TILE = D_MODEL // LANES


def _store_token_tiles(ref, x):
    for s in range(TOK_TILE):
        ref[pl.ds(s, x.shape[0], stride=TOK_TILE), :] = x[:, LANES * s:LANES * (s + 1)]


def _load_token_tiles(ref, n_tok):
    return jnp.concatenate([ref[pl.ds(s, n_tok, stride=TOK_TILE), :] for s in range(TOK_TILE)], axis=1)


def _pick(c_ref, l_ref):
    return jnp.where(pl.program_id(0) == 0, c_ref[...], l_ref[...])


def _x_pair(x):
    return (x, 0) if isinstance(x, tuple) else ((x, x), SEG)


def _inproj_kernel(xc_ref, xl_ref, mod_ref, g_ref, w_ref, qkg_ref, cos_ref, sin_ref, o_ref):
    seg = pl.program_id(0)
    mod = mod_ref[0]
    sh1 = mod[:, 0:D_MODEL]
    sc1 = mod[:, D_MODEL:2 * D_MODEL]
    h = _rms(_pick(xc_ref, xl_ref), g_ref[...]) * (1.0 + sc1) + sh1
    r = jnp.dot(h.astype(BF16), w_ref[...], preferred_element_type=F32)
    o_ref[:, 0:C_SQ] = r[:, 0:C_SQ]
    o_ref[:, C_RX:C_NQ] = r[:, C_RX:C_NQ]
    o_ref[:, C_NV:C_SK] = r[:, C_NV:C_SK]
    o_ref[:, C_SV:P_W] = r[:, C_SV:P_W]
    cos = cos_ref[...]
    sin = sin_ref[...]
    latent = seg > 0

    def rope(y):
        return jnp.where(latent, y * cos + _swap16(y) * sin, y)

    for p in range(2):
        a = C_SQ + LANES * p
        o_ref[:, a:a + LANES] = rope(_rms_head_pairs(r[:, a:a + LANES], qkg_ref[0:1, :]))
    o_ref[:, C_SK:C_SK + LANES] = rope(_rms_head_pairs(r[:, C_SK:C_SK + LANES], qkg_ref[1:2, :]))
    for p in range(2):
        a = C_NQ + LANES * p
        o_ref[:, a:a + LANES] = _rms_head_pairs(r[:, a:a + LANES], qkg_ref[2:3, :])
        a = C_NK + LANES * p
        o_ref[:, a:a + LANES] = _rms_head_pairs(r[:, a:a + LANES], qkg_ref[3:4, :])


def _inproj(x, mod_l, norm1, w_in_p, qk_gains, cos_t, sin_t):
    nt = SEG // IN_TM
    x_pair, lat_row0 = _x_pair(x)
    return pl.pallas_call(
        _inproj_kernel,
        grid=(N_SEG, nt),
        in_specs=[*_seg_pair_specs(IN_TM, D_MODEL, lat_row0),
                  pl.BlockSpec((1, 1, 6 * D_MODEL), lambda s, i: (s, 0, 0)),
                  pl.BlockSpec((1, D_MODEL), lambda s, i: (0, 0)),
                  pl.BlockSpec((D_MODEL, P_W), lambda s, i: (0, 0)),
                  pl.BlockSpec((4, LANES), lambda s, i: (0, 0)),
                  pl.BlockSpec((IN_TM, LANES), lambda s, i: (i, 0)),
                  pl.BlockSpec((IN_TM, LANES), lambda s, i: (i, 0))],
        out_specs=pl.BlockSpec((IN_TM, P_W), lambda s, i: (s * nt + i, 0)),
        out_shape=jax.ShapeDtypeStruct((N_TOK, P_W), F32),
        compiler_params=_cparams(("arbitrary", "arbitrary")),
        name="inproj",
    )(*x_pair, mod_l, norm1, w_in_p, qk_gains, cos_t, sin_t)


def _mlstm_direction(d, rows, q_ref, k_ref, v_ref, g_ref, bias, tri_ref, sel_ref, state):
    ch = ML_CHUNK
    r_io = lax.broadcasted_iota(jnp.int32, (ch, ch), 0)
    c_io = lax.broadcasted_iota(jnp.int32, (ch, ch), 1)
    lower = r_io >= c_io
    upper = r_io <= c_io
    mask = lower if d == 0 else upper
    tri = tri_ref[d]
    tri_t = tri_ref[1 - d]
    left = c_io < HEAD_DIM
    top = r_io < HEAD_DIM
    blockdiag = top == left
    cbs_in, nbs_in, m_old = state
    g = g_ref[rows, :] + bias[...]
    b_cols = _dot_exact_rhs(tri, jax.nn.log_sigmoid(g))
    b3 = jnp.concatenate(_split3(b_cols), axis=1)
    gt = g.T
    li_rows = gt[0:8, :]
    b_rows = _dot_exact_lhs(jax.nn.log_sigmoid(gt[8:16, :]), tri_t)
    a_rows = li_rows - b_rows
    bl = b_rows[:, ch - 1:ch] if d == 0 else b_rows[:, 0:1]
    g_rows = bl - b_rows + li_rows
    m_new = jnp.maximum(bl + m_old, jnp.max(g_rows, axis=1, keepdims=True))
    wk_rows = jnp.exp(g_rows - m_new)
    wp = jnp.exp(bl + m_old - m_new)
    ones_blk = jnp.ones((ch, LANES), BF16)
    left2 = lax.broadcasted_iota(jnp.int32, (ch, 2 * LANES), 1) % LANES < HEAD_DIM
    h_out, c_out, n_out = [], [], []
    for p in range(ML_HEADS // 2):
        lanes = slice(LANES * p, LANES * (p + 1))
        j0 = ML_HEADS * d + 2 * p
        q2 = q_ref[rows, lanes]
        k2t = (k_ref[rows, lanes] * SCALE).T.astype(BF16)
        v2e = jnp.concatenate([v_ref[rows, lanes].astype(BF16), ones_blk], axis=1)
        cb = cbs_in[p]
        nb = nbs_in[p]
        q2b = q2.astype(BF16)
        q_lo = (q2 - q2b.astype(F32)).astype(BF16)
        nb_hi = nb.astype(BF16)
        nb_lo = (nb - nb_hi.astype(F32)).astype(BF16)
        qc = jnp.dot(q2b, cb.astype(BF16), preferred_element_type=F32)
        qn = jnp.dot(jnp.concatenate([q2b, q_lo, q2b], axis=1), jnp.concatenate([nb_hi, nb_hi, nb_lo], axis=0),
                     preferred_element_type=F32)
        b_pair = jnp.dot(b3, sel_ref[2 * d + p], preferred_element_type=F32)
        cbs, sves = [], []
        for i in range(2):
            j = j0 + i
            half = left if i == 0 else jnp.logical_not(left)
            a_mat = jnp.where(mask, a_rows[j:j + 1, :], NEG)
            cvec = jnp.maximum(m_old[j:j + 1, :], jnp.max(a_mat, axis=1, keepdims=True))
            cbro = jnp.broadcast_to(cvec, (ch, ch))
            s = jnp.dot(jnp.where(half, q2b, 0), k2t, preferred_element_type=F32) * jnp.exp(a_mat - cbro)
            s_hi = s.astype(BF16)
            s_lo = (s - s_hi.astype(F32)).astype(BF16)
            sve = jnp.dot(s_hi, v2e, preferred_element_type=F32)
            rs_lo = jnp.dot(s_lo, ones_blk, preferred_element_type=F32)
            sves.append(jnp.concatenate([sve[:, 0:LANES], sve[:, LANES:2 * LANES] + rs_lo], axis=1))
            cbs.append(cbro)
        c_pair = jnp.where(left, cbs[0], cbs[1])
        w_prev = jnp.exp(jnp.where(left, m_old[j0:j0 + 1, :], m_old[j0 + 1:j0 + 2, :]) - c_pair)
        sve = jnp.where(left2, sves[0], sves[1])
        num = w_prev * qc + sve[:, 0:LANES]
        den = w_prev * qn + sve[:, LANES:2 * LANES]
        h_out.append(num / jnp.maximum(jnp.abs(den), jnp.exp(-(c_pair + b_pair))))
        kwt = k2t * jnp.where(top, wk_rows[j0:j0 + 1, :], wk_rows[j0 + 1:j0 + 2, :])
        kwt_hi = kwt.astype(BF16)
        kwt_lo = (kwt - kwt_hi.astype(F32)).astype(BF16)
        kve = jnp.dot(kwt_hi, v2e, preferred_element_type=F32)
        kn = kve[:, LANES:2 * LANES] + jnp.dot(kwt_lo, ones_blk, preferred_element_type=F32)
        wp_pair = jnp.where(top, wp[j0:j0 + 1, :], wp[j0 + 1:j0 + 2, :])
        c_out.append(wp_pair * cb + jnp.where(blockdiag, kve[:, 0:LANES], 0.0))
        n_out.append(wp_pair * nb + jnp.where(blockdiag, kn, 0.0))
    return h_out, (c_out, n_out, m_new)


def _mlstm_kernel(qf, kf, vf, gf, qb, kb, vb, gb, c0, n0, m0, bias, tri_ref, sel_ref,
                  hf, hb, co, no, mo, cbd, nbd, m_s, *, nc, nsub):
    c = pl.program_id(1)

    @pl.when(c == 0)
    def _():
        cbd[...] = c0[0]
        nbd[...] = n0[0]
        m_s[...] = m0[0]

    npair = ML_HEADS // 2
    for d, (refs, h_ref) in enumerate((((qf, kf, vf, gf), hf), ((qb, kb, vb, gb), hb))):
        state = ([cbd[npair * d + p] for p in range(npair)], [nbd[npair * d + p] for p in range(npair)],
                 m_s[:, 0:1])
        subs = range(nsub) if d == 0 else reversed(range(nsub))
        for sub in subs:
            rows = slice(sub * ML_CHUNK, (sub + 1) * ML_CHUNK)
            h_out, state = _mlstm_direction(d, rows, *refs, bias, tri_ref, sel_ref, state)
            for p in range(npair):
                h_ref[rows, LANES * p:LANES * (p + 1)] = h_out[p]
        c_out, n_out, m_new = state
        for p in range(npair):
            cbd[npair * d + p] = c_out[p]
            nbd[npair * d + p] = n_out[p]
        heads = slice(ML_HEADS * d, ML_HEADS * (d + 1))
        m_s[heads, :] = jnp.broadcast_to(m_new[heads, :], (ML_HEADS, LANES))

    @pl.when(c == nc - 1)
    def _():
        co[0] = cbd[...]
        no[0] = nbd[...]
        mo[0] = m_s[...]


def _mlstm_pack_state(c0, n0, m0):
    bsz = c0.shape[0]
    hd = HEAD_DIM
    eye = jnp.eye(2, dtype=F32)[None, None, :, None, :, None]
    cbd = c0.reshape(bsz, ML_HEADS, 2, hd, 1, hd) * eye
    nbd = jnp.broadcast_to(n0.reshape(bsz, ML_HEADS, 2, hd, 1, 1) * eye, cbd.shape)
    to_mat = lambda t: t.reshape(bsz, ML_HEADS, LANES, LANES)
    return to_mat(cbd), to_mat(nbd), jnp.broadcast_to(m0[..., None], m0.shape + (LANES,))


def _mlstm_unpack_state(cbd, nbd, mrow):
    hd = HEAD_DIM
    bsz = cbd.shape[0]
    c = jnp.stack([cbd[:, :, :hd, :hd], cbd[:, :, hd:, hd:]], axis=2).reshape(bsz, 2 * ML_HEADS, hd, hd)
    n = jnp.stack([nbd[:, :, :hd, 0], nbd[:, :, hd:, hd]], axis=2).reshape(bsz, 2 * ML_HEADS, hd)
    return c, n, mrow[:, :, 0]


def _mlstm_tables():
    r = np.arange(ML_CHUNK)
    lower = (r[:, None] >= r[None, :]).astype(np.float32)
    row = np.arange(3 * LANES)[:, None] % LANES
    lane_left = np.arange(LANES)[None, :] < HEAD_DIM
    sel = [row == np.where(lane_left, 8 + ML_HEADS * d + 2 * p, 9 + ML_HEADS * d + 2 * p)
           for d in range(2) for p in range(ML_HEADS // 2)]
    return (jnp.asarray(np.stack([lower, lower.T]), dtype=BF16),
            jnp.asarray(np.stack(sel).astype(np.float32), dtype=BF16))


def _mlstm(proj, gate_bias, c0, n0, m0, *, bsz, seq, row0):
    nsub = min(ML_SUB, seq // ML_CHUNK)
    blk = nsub * ML_CHUNK
    assert seq % blk == 0 and row0 % blk == 0
    nc = seq // blk
    tri, sel = _mlstm_tables()
    base = row0 // blk
    nj = 2 * ML_HEADS
    fw = lambda col: (lambda b, c: (base + b * nc + c, col))
    bw = lambda col: (lambda b, c: (base + b * nc + nc - 1 - c, col))
    qkv = lambda f: [pl.BlockSpec((blk, 256), f(C_MQ // 256)),
                     pl.BlockSpec((blk, 256), f(C_MK // 256)),
                     pl.BlockSpec((blk, 256), f(C_MV // 256)),
                     pl.BlockSpec((blk, LANES), f(C_G // LANES))]
    st_specs = [pl.BlockSpec((1, nj // 2, LANES, LANES), lambda b, c: (b, 0, 0, 0)),
                pl.BlockSpec((1, nj // 2, LANES, LANES), lambda b, c: (b, 0, 0, 0)),
                pl.BlockSpec((1, nj, LANES), lambda b, c: (b, 0, 0))]
    st_shapes = [jax.ShapeDtypeStruct((bsz, nj // 2, LANES, LANES), F32),
                 jax.ShapeDtypeStruct((bsz, nj // 2, LANES, LANES), F32),
                 jax.ShapeDtypeStruct((bsz, nj, LANES), F32)]
    return pl.pallas_call(
        functools.partial(_mlstm_kernel, nc=nc, nsub=nsub),
        grid=(bsz, nc),
        in_specs=qkv(fw) + qkv(bw) + st_specs + [pl.BlockSpec((1, LANES), lambda b, c: (0, 0)),
                                                 pl.BlockSpec(tri.shape, lambda b, c: (0, 0, 0)),
                                                 pl.BlockSpec(sel.shape, lambda b, c: (0, 0, 0))],
        out_specs=[pl.BlockSpec((blk, 256), lambda b, c: (b * nc + c, 0)),
                   pl.BlockSpec((blk, 256), lambda b, c: (b * nc + nc - 1 - c, 0))] + st_specs,
        out_shape=[jax.ShapeDtypeStruct((bsz * seq, 256), F32),
                   jax.ShapeDtypeStruct((bsz * seq, 256), F32)] + st_shapes,
        scratch_shapes=[pltpu.VMEM((nj // 2, LANES, LANES), F32),
                        pltpu.VMEM((nj // 2, LANES, LANES), F32),
                        pltpu.VMEM((nj, LANES), F32)],
        compiler_params=_cparams(("arbitrary", "arbitrary")),
        name="mlstm",
    )(proj, proj, proj, proj, proj, proj, proj, proj, c0, n0, m0, gate_bias, tri, sel)


RG_TC = 256
RG_PAD = 8
RG_NSEG = 8
RG_SKEW = 4


def _rglru_kernel(rx, ry, cw, cb, wg, bg, lam, h0, oc, hl, xpad, af, ab, uf, ub, *, seg_len, chained):
    rows = RG_NSEG * seg_len
    seq_len = rows if chained else seg_len
    halves = RG_WIDTH // LANES
    pitch = seg_len + RG_SKEW
    buf_row = lambda t: (t // seg_len) * pitch + t % seg_len
    xpad[0:RG_PAD, :] = jnp.zeros((RG_PAD, RG_WIDTH), F32)
    xpad[rows + RG_PAD:rows + 2 * RG_PAD, :] = jnp.zeros((RG_PAD, RG_WIDTH), F32)
    xpad[RG_PAD:rows + RG_PAD, :] = rx[...]
    sp = jax.nn.softplus(-lam[...])
    left = (RG_CONV - 1) // 2
    for ci in range(rows // RG_TC):
        s0 = ci * RG_TC
        pos = (s0 + lax.broadcasted_iota(jnp.int32, (RG_TC, RG_WIDTH), 0)) % seq_len
        xc = None
        for j in range(RG_CONV):
            a = RG_PAD + s0 + j - left
            term = xpad[a:a + RG_TC, :] * cw[j:j + 1, :]
            if not chained and j != left:
                term = jnp.where((pos + (j - left) >= 0) & (pos + (j - left) < seq_len), term, 0.0)
            xc = term if xc is None else xc + term
        xc = xc + cb[...]
        pre = _dot(xc, wg[...]) + bg[...]
        for d, (a_ref, u_ref) in enumerate(((af, uf), (ab, ub))):
            o = 2 * RG_WIDTH * d
            r = _sigmoid(pre[:, o:o + RG_WIDTH])
            gi = _sigmoid(pre[:, o + RG_WIDTH:o + 2 * RG_WIDTH])
            log_a = -RG_C * r * sp[d:d + 1, :]
            a_val = jnp.exp(log_a)
            u_val = jnp.sqrt(-jnp.tanh(log_a) * (a_val * a_val + 1.0)) * (gi * xc)
            for hv in range(halves):
                dst = slice(buf_row(s0), buf_row(s0) + RG_TC)
                a_ref[hv, dst, :] = a_val[:, LANES * hv:LANES * (hv + 1)]
                u_ref[hv, dst, :] = u_val[:, LANES * hv:LANES * (hv + 1)]

    def body(s, carry):
        out = []
        for d, (a_ref, u_ref) in enumerate(((af, uf), (ab, ub))):
            step_rows = pl.ds(s if d == 0 else seg_len - 1 - s, RG_NSEG, stride=pitch)
            for hv in range(halves):
                h_loc, prod = carry[2 * (halves * d + hv)], carry[2 * (halves * d + hv) + 1]
                a = a_ref[hv, step_rows, :]
                h_loc = a * h_loc + u_ref[hv, step_rows, :]
                prod = a * prod
                u_ref[hv, step_rows, :] = h_loc
                a_ref[hv, step_rows, :] = prod
                out += [h_loc, prod]
        return tuple(out)

    zero = jnp.zeros((RG_NSEG, LANES), F32)
    one = jnp.ones((RG_NSEG, LANES), F32)
    ends = lax.fori_loop(0, seg_len, body, (zero, one) * (2 * halves), unroll=8)
    seg = lax.broadcasted_iota(jnp.int32, (RG_NSEG, LANES), 0)
    for d, (a_ref, u_ref) in enumerate(((af, uf), (ab, ub))):
        for hv in range(halves):
            lanes = slice(LANES * hv, LANES * (hv + 1))
            h_end, p_end = ends[2 * (halves * d + hv)], ends[2 * (halves * d + hv) + 1]
            h_in = h0[0, d][:, lanes]
            if chained:
                c = h_in[0:1, :]
                h_in = zero
                for k in (range(RG_NSEG) if d == 0 else reversed(range(RG_NSEG))):
                    h_in = jnp.where(seg == k, c, h_in)
                    c = h_end[k:k + 1, :] + p_end[k:k + 1, :] * c
            hl[0, d, :, lanes] = h_end + p_end * h_in
            for k in range(RG_NSEG):
                for ci in range(seg_len // RG_TC):
                    sl = slice(k * pitch + ci * RG_TC, k * pitch + (ci + 1) * RG_TC)
                    u_ref[hv, sl, :] = u_ref[hv, sl, :] + a_ref[hv, sl, :] * h_in[k:k + 1, :]
    for ci in range(rows // RG_TC):
        sl = slice(ci * RG_TC, (ci + 1) * RG_TC)
        for hv in range(halves):
            lanes = slice(LANES * hv, LANES * (hv + 1))
            src = slice(buf_row(ci * RG_TC), buf_row(ci * RG_TC) + RG_TC)
            oc[sl, lanes] = (uf[hv, src, :] + ub[hv, src, :]) * jax.nn.gelu(ry[sl, lanes])


def _rglru(proj, cw, cb, wg, bg, lam, h0, *, nblk, seg_len, chained, row0):
    rows = RG_NSEG * seg_len
    base = row0 // rows
    full = lambda shape: pl.BlockSpec(shape, lambda b: tuple(0 for _ in shape))
    st_spec = pl.BlockSpec((1, 2, RG_NSEG, RG_WIDTH), lambda b: (b, 0, 0, 0))
    return pl.pallas_call(
        functools.partial(_rglru_kernel, seg_len=seg_len, chained=chained),
        grid=(nblk,),
        in_specs=[pl.BlockSpec((rows, RG_WIDTH), lambda b: (base + b, C_RX // RG_WIDTH)),
                  pl.BlockSpec((rows, RG_WIDTH), lambda b: (base + b, C_RY // RG_WIDTH)),
                  full((RG_CONV, RG_WIDTH)), full((1, RG_WIDTH)),
                  full((RG_WIDTH, 4 * RG_WIDTH)), full((1, 4 * RG_WIDTH)), full((2, RG_WIDTH)),
                  st_spec],
        out_specs=[pl.BlockSpec((rows, RG_WIDTH), lambda b: (b, 0)), st_spec],
        out_shape=[jax.ShapeDtypeStruct((nblk * rows, RG_WIDTH), F32),
                   jax.ShapeDtypeStruct((nblk, 2, RG_NSEG, RG_WIDTH), F32)],
        scratch_shapes=[pltpu.VMEM((rows + 2 * RG_PAD, RG_WIDTH), F32)]
        + [pltpu.VMEM((RG_WIDTH // LANES, RG_NSEG * (seg_len + RG_SKEW), LANES), F32) for _ in range(4)],
        compiler_params=_cparams(("arbitrary",)),
        name="rglru",
    )(proj, proj, cw, cb, wg, bg, lam, h0)


def _softmax_pv(scores, values, sink):
    m = functools.reduce(jnp.maximum, [jnp.max(s, axis=-1, keepdims=True) for s in scores])
    if sink is not None:
        m = jnp.maximum(m, sink)
    ps = [jnp.exp(s - m) for s in scores]
    den = functools.reduce(jnp.add, [jnp.sum(p, axis=-1, keepdims=True) for p in ps])
    if sink is not None:
        den = den + jnp.exp(sink - m)
    num = functools.reduce(jnp.add, [_dot(p, v) for p, v in zip(ps, values)])
    return num / den


def _ctx_attn_kernel(sink, sq, sk, sv, nq, nk, nv, ob, od):
    assert SW_HEADS == 4 and SW_KV_HEADS == 2
    left = lax.broadcasted_iota(jnp.int32, (SEQ, LANES), 1) < HEAD_DIM
    first = lax.broadcasted_iota(jnp.int32, (2 * SEQ, 1), 0) < SEQ
    k2 = sk[...].astype(BF16)
    v2 = sv[...].astype(BF16)
    for p in range(SW_KV_HEADS):
        lanes = slice(LANES * p, LANES * (p + 1))
        q2 = sq[:, lanes]
        q_swapped = pltpu.roll(q2, HEAD_DIM, 1)
        kv_half = left if p == 0 else jnp.logical_not(left)
        qs = jnp.concatenate([jnp.where(kv_half, q2 if i == p else q_swapped, 0.0) for i in range(2)], axis=0)
        sink_col = jnp.where(first, sink[2 * p], sink[2 * p + 1])
        res = _softmax_pv([_dot_nt(qs, k2) * SCALE], [v2], sink_col)
        halves = [res[0:SEQ], res[SEQ:2 * SEQ]]
        placed = [halves[i] if i == p else pltpu.roll(halves[i], HEAD_DIM, 1) for i in range(2)]
        ob[:, lanes] = jnp.where(left, placed[0], placed[1])
    for p in range(NA_HEADS // 2):
        lanes = slice(LANES * p, LANES * (p + 1))
        q2 = nq[:, lanes]
        k2 = nk[:, lanes].astype(BF16)
        v2 = nv[:, lanes].astype(BF16)
        res = [_softmax_pv([_dot_nt(jnp.where(left if i == 0 else jnp.logical_not(left), q2, 0.0), k2) * SCALE],
                           [v2], None) for i in range(2)]
        od[:, lanes] = jnp.where(left, res[0], res[1])


def _ctx_attn(proj, sink):
    blk = lambda w, col: pl.BlockSpec((SEQ, w), lambda b: (b, col))
    return pl.pallas_call(
        _ctx_attn_kernel,
        grid=(BATCH,),
        in_specs=[pl.BlockSpec(memory_space=pltpu.SMEM),
                  blk(256, C_SQ // 256), blk(128, C_SK // 128), blk(128, C_SV // 128),
                  blk(256, C_NQ // 256), blk(256, C_NK // 256), blk(256, C_NV // 256)],
        out_specs=[pl.BlockSpec((SEQ, 256), lambda b: (b, 0)),
                   pl.BlockSpec((SEQ, 256), lambda b: (b, 0))],
        out_shape=[jax.ShapeDtypeStruct((SEG, 256), F32), jax.ShapeDtypeStruct((SEG, 256), F32)],
        compiler_params=_cparams(("arbitrary",)),
        name="ctx_attn",
    )(sink, proj, proj, proj, proj, proj, proj)


SW_QB = 128
SW_SPAN = SW_QB + 2 * SW_WINDOW


def _swa_kernel(sink, q, k, v, kc, vc, ob):
    assert SW_HEADS == 4 and SW_KV_HEADS == 2
    n = pl.program_id(1)
    ws = jnp.clip((n - 1) * SW_QB, 0, DEC_SEQ - SW_SPAN)
    ws = pl.multiple_of(ws, SW_QB)
    row = lax.broadcasted_iota(jnp.int32, (2 * SW_QB, SW_SPAN), 0)
    qpos = n * SW_QB + row % SW_QB
    kpos = ws + lax.broadcasted_iota(jnp.int32, (2 * SW_QB, SW_SPAN), 1)
    valid = jnp.abs(qpos - kpos) <= SW_WINDOW
    left = lax.broadcasted_iota(jnp.int32, (SW_QB, LANES), 1) < HEAD_DIM
    first = lax.broadcasted_iota(jnp.int32, (2 * SW_QB, 1), 0) < SW_QB
    k2 = k[pl.ds(ws, SW_SPAN), :].astype(BF16)
    v2 = v[pl.ds(ws, SW_SPAN), :].astype(BF16)
    kc2 = kc[0].astype(BF16)
    vc2 = vc[0].astype(BF16)
    for p in range(SW_KV_HEADS):
        lanes = slice(LANES * p, LANES * (p + 1))
        q2 = q[:, lanes]
        q_swapped = pltpu.roll(q2, HEAD_DIM, 1)
        kv_half = left if p == 0 else jnp.logical_not(left)
        qs = jnp.concatenate([jnp.where(kv_half, q2 if i == p else q_swapped, 0.0) for i in range(2)], axis=0)
        s_loc = jnp.where(valid, _dot_nt(qs, k2) * SCALE, NEG)
        s_ctx = _dot_nt(qs, kc2) * SCALE
        sink_col = jnp.where(first, sink[2 * p], sink[2 * p + 1])
        res = _softmax_pv([s_loc, s_ctx], [v2, vc2], sink_col)
        halves = [res[0:SW_QB], res[SW_QB:2 * SW_QB]]
        placed = [halves[i] if i == p else pltpu.roll(halves[i], HEAD_DIM, 1) for i in range(2)]
        ob[:, lanes] = jnp.where(left, placed[0], placed[1])


def _swa(proj, kc, vc, sink):
    nq = DEC_SEQ // SW_QB
    qbase = SEG // SW_QB
    return pl.pallas_call(
        _swa_kernel,
        grid=(DEC_BATCH, nq),
        in_specs=[pl.BlockSpec(memory_space=pltpu.SMEM),
                  pl.BlockSpec((SW_QB, 256), lambda b, n: (qbase + b * nq + n, C_SQ // 256)),
                  pl.BlockSpec((DEC_SEQ, 128), lambda b, n: (1 + b, C_SK // 128)),
                  pl.BlockSpec((DEC_SEQ, 128), lambda b, n: (1 + b, C_SV // 128)),
                  pl.BlockSpec((1, PAST_LEN, 128), lambda b, n: (b, 0, 0)),
                  pl.BlockSpec((1, PAST_LEN, 128), lambda b, n: (b, 0, 0))],
        out_specs=pl.BlockSpec((SW_QB, 256), lambda b, n: (b * nq + n, 0)),
        out_shape=jax.ShapeDtypeStruct((DEC_BATCH * DEC_SEQ, 256), F32),
        compiler_params=_cparams(("arbitrary", "arbitrary")),
        name="swa",
    )(sink, proj, proj, proj, kc, vc)


NA_RPB_R = 2 * NA_ROWS - 1
NA_RPB_C = 2 * NA_COLS - 1
GRID_ROWS = DEC_SEQ // GRID_W
NA_RB = 4
NA_UW = 12
NA_NQ = NA_RB * GRID_W
NA_NKEY = NA_UW * GRID_W
NA_NBLK = GRID_ROWS // NA_RB
NA_CASES = ((0, 0), (NA_RB, 0), (GRID_ROWS - NA_RB, GRID_ROWS - NA_UW))


def _na_row_start(qrow):
    return min(max(qrow - NA_ROWS // 2, 0), GRID_ROWS - NA_ROWS)


def _na_bias_kernel(rpb, out):
    h = pl.program_id(0)
    qc = lax.broadcasted_iota(jnp.int32, (GRID_W, GRID_W), 0)
    kc = lax.broadcasted_iota(jnp.int32, (GRID_W, GRID_W), 1)
    dc = jnp.clip(kc - qc, -(NA_COLS - 1), NA_COLS - 1) + NA_COLS - 1
    lo = jnp.clip(qc - NA_COLS // 2, 0, GRID_W - NA_COLS)
    valid = (kc >= lo) & (kc < lo + NA_COLS)
    tiles = []
    for dr in range(NA_RPB_R):
        t = jnp.zeros((GRID_W, GRID_W), F32)
        for j in range(NA_RPB_C):
            t = jnp.where(dc == j, rpb[(h * NA_RPB_R + dr) * NA_RPB_C + j], t)
        tiles.append(jnp.where(valid, t, NEG))
    outside = jnp.full((GRID_W, GRID_W), NEG, F32)
    for case, (q0, k0) in enumerate(NA_CASES):
        for a in range(NA_RB):
            r0 = _na_row_start(q0 + a)
            for i in range(NA_UW):
                inside = r0 <= k0 + i < r0 + NA_ROWS
                tile = tiles[k0 + i - (q0 + a) + NA_ROWS - 1] if inside else outside
                out[0, case, GRID_W * a:GRID_W * (a + 1), GRID_W * i:GRID_W * (i + 1)] = tile


def _na_bias(rpb_flat):
    shape = (NA_HEADS, len(NA_CASES), NA_NQ, NA_NKEY)
    return pl.pallas_call(
        _na_bias_kernel,
        grid=(NA_HEADS,),
        in_specs=[pl.BlockSpec(memory_space=pltpu.SMEM)],
        out_specs=pl.BlockSpec((1,) + shape[1:], lambda h: (h, 0, 0, 0)),
        out_shape=jax.ShapeDtypeStruct(shape, F32),
        compiler_params=_cparams(("arbitrary",)),
        name="na_bias",
    )(rpb_flat)


def _na_kernel(q, k, v, kc, vc, bias, od):
    blk = pl.program_id(1)
    case = jnp.where(blk == 0, 0, jnp.where(blk == NA_NBLK - 1, 2, 1))
    u0 = jnp.clip(blk * NA_RB - NA_ROWS // 2, 0, GRID_ROWS - NA_UW)
    k0 = pl.multiple_of(u0 * GRID_W, GRID_W)
    left = lax.broadcasted_iota(jnp.int32, (NA_NQ, LANES), 1) < HEAD_DIM
    for p in range(NA_HEADS // 2):
        lanes = slice(LANES * p, LANES * (p + 1))
        q2 = q[:, lanes]
        k2 = k[pl.ds(k0, NA_NKEY), lanes].astype(BF16)
        v2 = v[pl.ds(k0, NA_NKEY), lanes].astype(BF16)
        kc2 = kc[0, :, lanes].astype(BF16)
        vc2 = vc[0, :, lanes].astype(BF16)
        res = []
        for i in range(2):
            qm = jnp.where(left if i == 0 else jnp.logical_not(left), q2, 0.0)
            s_loc = _dot_nt(qm, k2) * SCALE + bias[2 * p + i, pl.ds(case, 1)][0]
            s_ctx = _dot_nt(qm, kc2) * SCALE
            res.append(_softmax_pv([s_loc, s_ctx], [v2, vc2], None))
        od[:, lanes] = jnp.where(left, res[0], res[1])


def _na(proj, kc, vc, bias):
    qbase = SEG // NA_NQ
    return pl.pallas_call(
        _na_kernel,
        grid=(DEC_BATCH, NA_NBLK),
        in_specs=[pl.BlockSpec((NA_NQ, 256), lambda b, r: (qbase + b * NA_NBLK + r, C_NQ // 256)),
                  pl.BlockSpec((DEC_SEQ, 256), lambda b, r: (1 + b, C_NK // 256)),
                  pl.BlockSpec((DEC_SEQ, 256), lambda b, r: (1 + b, C_NV // 256)),
                  pl.BlockSpec((1, PAST_LEN, 256), lambda b, r: (b, 0, 0)),
                  pl.BlockSpec((1, PAST_LEN, 256), lambda b, r: (b, 0, 0)),
                  pl.BlockSpec((NA_HEADS, len(NA_CASES), NA_NQ, NA_NKEY), lambda b, r: (0, 0, 0, 0))],
        out_specs=pl.BlockSpec((NA_NQ, 256), lambda b, r: (b * NA_NBLK + r, 0)),
        out_shape=jax.ShapeDtypeStruct((DEC_BATCH * DEC_SEQ, 256), F32),
        compiler_params=_cparams(("arbitrary", "arbitrary")),
        name="na",
    )(proj, proj, proj, kc, vc, bias)


MG_ROWS = 256
MG_SUB = 2
MG_TM = MG_SUB * MG_ROWS


def _merge_kernel(xc_ref, xl_ref, mod_ref, g1_ref, g2_ref, hf_c, hf_l, hb_c, hb_l, mo, ob_c, ob_l, oc_c, oc_l,
                  od_c, od_l, hn, wmg, bmg, wbr, wout, *rest, moe):
    if moe:
        wrt, br, x1_ref, h2_ref, route_ref = rest
    else:
        x1_ref, h2_ref = rest
    ctx = pl.program_id(0) == 0
    mod = mod_ref[0]
    chunk = lambda i: mod[:, i * D_MODEL:(i + 1) * D_MODEL]
    sh1, sc1, gate1, sh2, sc2 = chunk(0), chunk(1), chunk(2), chunk(3), chunk(4)

    def row_group(rows):
        pick = lambda c_ref, l_ref: jnp.where(ctx, c_ref[rows, :], l_ref[rows, :])
        x = pick(xc_ref, xl_ref)
        h = (_rms(x, g1_ref[...]) * (1.0 + sc1) + sh1).astype(BF16)
        hsum = pick(hf_c, hf_l) + pick(hb_c, hb_l)
        out_a = jnp.concatenate(
            [_rms_head_pairs(hsum[:, LANES * p:LANES * (p + 1)], hn[...]) for p in range(2)], axis=-1)
        out_a = out_a * jax.nn.sigmoid(mo[rows, :])
        acc = None
        for n, br_val in enumerate((out_a, pick(ob_c, ob_l), pick(oc_c, oc_l), pick(od_c, od_l))):
            gate = jax.nn.sigmoid(jnp.dot(h, wmg[:, n * D_MODEL:(n + 1) * D_MODEL].astype(BF16),
                                          preferred_element_type=F32)
                                  + bmg[:, n * D_MODEL:(n + 1) * D_MODEL])
            term = gate * jnp.dot(br_val.astype(BF16), wbr[n].astype(BF16), preferred_element_type=F32)
            acc = term if acc is None else acc + term
        y = jnp.dot(acc.astype(BF16), wout[...].astype(BF16), preferred_element_type=F32)
        x1 = x + gate1 * y
        x1_ref[rows, :] = x1
        h2 = _rms(x1, g2_ref[...]) * (1.0 + sc2) + sh2
        if moe:
            _store_token_tiles(h2_ref.at[pl.ds(rows.start * TOK_TILE, MG_ROWS * TOK_TILE)], h2)
        else:
            h2_ref[rows, :] = h2.astype(BF16)
        if moe:
            logit = [jnp.sum(h2 * wrt[e:e + 1, :], axis=-1, keepdims=True) + br[e] for e in range(N_EXPERTS)]
            v1, i1 = logit[0], jnp.zeros(logit[0].shape, jnp.int32)
            for e in range(1, N_EXPERTS):
                better = logit[e] > v1
                v1 = jnp.where(better, logit[e], v1)
                i1 = jnp.where(better, e, i1)
            v2, i2 = jnp.full(v1.shape, -jnp.inf, F32), jnp.zeros(v1.shape, jnp.int32)
            for e in range(N_EXPERTS):
                better = (i1 != e) & (logit[e] > v2)
                v2 = jnp.where(better, logit[e], v2)
                i2 = jnp.where(better, e, i2)
            e2 = jnp.exp(v2 - v1)
            den = 1.0 + e2
            lane = lax.broadcasted_iota(jnp.int32, (MG_ROWS, LANES), 1)
            route = jnp.where(lane == 0, 1.0 / den, 0.0) + jnp.where(lane == 1, e2 / den, 0.0)
            route = route + jnp.where(lane == 2, i1.astype(F32), 0.0) + jnp.where(lane == 3, i2.astype(F32), 0.0)
            route_ref[rows, :] = route

    for sub in range(MG_SUB):
        row_group(slice(sub * MG_ROWS, (sub + 1) * MG_ROWS))


def _merge(x, mod_l, g1, g2, hf, hb, proj, ob, oc, od, hn, wmg, bmg, wbr, wout, router=None):
    nt = SEG // MG_TM
    moe = router is not None
    x_pair, lat_row0 = _x_pair(x)
    row = lambda w: pl.BlockSpec((MG_TM, w), lambda s, i: (s * nt + i, 0))
    ctx_blk, lat_blk = _seg_pair_specs(MG_TM, 256)
    full = lambda shape: pl.BlockSpec(shape, lambda s, i: tuple(0 for _ in shape), pipeline_mode=pl.Buffered(1))
    in_specs = [*_seg_pair_specs(MG_TM, D_MODEL, lat_row0),
                pl.BlockSpec((1, 1, 6 * D_MODEL), lambda s, i: (s, 0, 0)),
                full((1, D_MODEL)), full((1, D_MODEL)),
                ctx_blk, lat_blk, ctx_blk, lat_blk,
                pl.BlockSpec((MG_TM, 256), lambda s, i: (s * nt + i, C_MO // 256)),
                ctx_blk, lat_blk, ctx_blk, lat_blk, ctx_blk, lat_blk,
                full((1, LANES)), full((D_MODEL, N_BRANCH * D_MODEL)), full((1, N_BRANCH * D_MODEL)),
                full((N_BRANCH, 256, D_MODEL)), full((D_MODEL, D_MODEL))]
    args = [*x_pair, mod_l, g1, g2, *hf, *hb, proj, *ob, *oc, *od, hn, wmg, bmg, wbr, wout]
    if moe:
        h2_spec = pl.BlockSpec((MG_TM * TOK_TILE, LANES), lambda s, i: (s * nt + i, 0))
        h2_shape = jax.ShapeDtypeStruct((N_TOK * TOK_TILE, LANES), F32)
    else:
        h2_spec, h2_shape = row(D_MODEL), jax.ShapeDtypeStruct((N_TOK, D_MODEL), BF16)
    out_specs = [row(D_MODEL), h2_spec]
    out_shape = [jax.ShapeDtypeStruct((N_TOK, D_MODEL), F32), h2_shape]
    if moe:
        in_specs += [full((N_EXPERTS, D_MODEL)), pl.BlockSpec(memory_space=pltpu.SMEM)]
        args += list(router)
        out_specs.append(row(LANES))
        out_shape.append(jax.ShapeDtypeStruct((N_TOK, LANES), F32))
    return pl.pallas_call(
        functools.partial(_merge_kernel, moe=moe),
        grid=(N_SEG, nt),
        in_specs=in_specs,
        out_specs=out_specs,
        out_shape=out_shape,
        compiler_params=_cparams(("arbitrary", "arbitrary")),
        name="merge",
    )(*args)


FF_TM = 512


def _ffn_kernel(h2, w1, w3, w2, x1, mod_ref, out):
    h = h2[...]
    a = jnp.dot(h, w1[...].astype(BF16), preferred_element_type=F32)
    b = jnp.dot(h, w3[...].astype(BF16), preferred_element_type=F32)
    act = (jax.nn.silu(a) * b).astype(BF16)
    gate2 = mod_ref[0][:, 5 * D_MODEL:6 * D_MODEL]
    out[...] = x1[...] + gate2 * jnp.dot(act, w2[...].astype(BF16), preferred_element_type=F32)


def _ffn(h2, x1, mod_l, w1, w3, w2):
    nt = N_TOK // FF_TM
    per_seg = SEG // FF_TM
    resident = lambda shape: pl.BlockSpec(shape, lambda i: (0, 0), pipeline_mode=pl.Buffered(1))
    return pl.pallas_call(
        _ffn_kernel,
        grid=(nt,),
        in_specs=[pl.BlockSpec((FF_TM, D_MODEL), lambda i: (i, 0)),
                  resident((D_MODEL, D_FF)), resident((D_MODEL, D_FF)), resident((D_FF, D_MODEL)),
                  pl.BlockSpec((FF_TM, D_MODEL), lambda i: (i, 0)),
                  pl.BlockSpec((1, 1, 6 * D_MODEL), lambda i: (i // per_seg, 0, 0))],
        out_specs=pl.BlockSpec((FF_TM, D_MODEL), lambda i: (i, 0)),
        out_shape=jax.ShapeDtypeStruct((N_TOK, D_MODEL), F32),
        compiler_params=_cparams(("arbitrary",)),
        name="ffn",
    )(h2, w1, w3, w2, x1, mod_l)


MOE_TM = 256
MOE_SLOTS = 2 * N_TOK
MOE_TILES = MOE_SLOTS // MOE_TM + N_EXPERTS
MOE_NBUF = 3
MOE_STEPS = MOE_TILES + MOE_NBUF
MOE_DUMP = MOE_NBUF * MOE_TM
MOE_LEAD = 1
MOE_PLAN_TILES = MOE_LEAD + MOE_TILES + 2
MOE_FCHUNKS = 1
MOE_UNROLL = 8


def _moe_group_kernel(texp, nused, src_tok, dst_row, h2_hbm, w1, w3, w2, y_hbm, xs, ys, sem_in, sem_out):
    del texp
    i = pl.program_id(0)
    n_used = nused[0]
    buf = i % MOE_NBUF
    buf_next = (i + 2) % MOE_NBUF

    def tile_rows(t):
        start = t * TOK_TILE
        return pl.ds(start if isinstance(start, int) else pl.multiple_of(start, TOK_TILE), TOK_TILE)

    def gather_copy(tile, b, r):
        tok = src_tok[(tile + MOE_LEAD) * MOE_TM + r]
        return pltpu.make_async_copy(h2_hbm.at[tile_rows(tok)], xs.at[b, tile_rows(r)], sem_in.at[b])

    def scatter_copy(tile, b, r):
        dst = dst_row[(tile + MOE_LEAD) * MOE_TM + r]
        return pltpu.make_async_copy(ys.at[b, tile_rows(r)], y_hbm.at[tile_rows(dst)], sem_out.at[b])

    def start_rows_loop(make_copy, tile, b):
        def body(r, carry):
            make_copy(tile, b, r).start()
            return carry
        lax.fori_loop(0, MOE_TM, body, 0, unroll=MOE_UNROLL)

    def wait_tile(b, gather):
        if gather:
            pltpu.make_async_copy(h2_hbm.at[pl.ds(0, MOE_TM * TOK_TILE)], xs.at[b], sem_in.at[b]).wait()
        else:
            pltpu.make_async_copy(ys.at[b], y_hbm.at[pl.ds(0, MOE_TM * TOK_TILE)], sem_out.at[b]).wait()

    @pl.when(i == 0)
    def _():
        xs[...] = jnp.zeros(xs.shape, F32)
        ys[...] = jnp.zeros(ys.shape, F32)
        for b in range(MOE_NBUF):
            fill = pltpu.make_async_copy(ys.at[b], y_hbm.at[pl.ds((MOE_SLOTS + b * MOE_TM) * TOK_TILE, MOE_TM * TOK_TILE)],
                                         sem_out.at[b])
            fill.start()
            fill.wait()
        start_rows_loop(gather_copy, 0, 0)
        start_rows_loop(gather_copy, 1, 1)

    @pl.when(i <= n_used + 1)
    def _():
        wait_tile(buf, True)

    @pl.when((i >= 2) & (i <= n_used + 2))
    def _():
        wait_tile(buf, False)

    @pl.when(i < n_used)
    def _():
        x = _load_token_tiles(xs.at[buf], MOE_TM).astype(BF16)
        fc = D_FF_EXPERT // MOE_FCHUNKS
        rc = MOE_TM // MOE_FCHUNKS
        y = None
        for c in range(MOE_FCHUNKS):
            for r in range(c * rc, (c + 1) * rc):
                gather_copy(i + 2, buf_next, r).start()
                scatter_copy(i - 1, buf_next, r).start()
            a = jnp.dot(x, w1[0, :, c * fc:(c + 1) * fc].astype(BF16), preferred_element_type=F32)
            b = jnp.dot(x, w3[0, :, c * fc:(c + 1) * fc].astype(BF16), preferred_element_type=F32)
            act = (jax.nn.silu(a) * b).astype(BF16)
            part = jnp.dot(act, w2[0, c * fc:(c + 1) * fc, :].astype(BF16), preferred_element_type=F32)
            y = part if y is None else y + part
        _store_token_tiles(ys.at[buf], y)

    @pl.when(i == n_used)
    def _():
        start_rows_loop(scatter_copy, i - 1, buf_next)


def _moe_group(tile_expert, n_used, src_tok, dst_row, h2, w1, w3, w2):
    wspec = lambda shape: pl.BlockSpec((1,) + shape, lambda i, texp, *_: (texp[jnp.minimum(i, MOE_TILES - 1)], 0, 0))
    grid_spec = pltpu.PrefetchScalarGridSpec(
        num_scalar_prefetch=4,
        grid=(MOE_STEPS,),
        in_specs=[pl.BlockSpec(memory_space=pl.ANY),
                  wspec((D_MODEL, D_FF_EXPERT)), wspec((D_MODEL, D_FF_EXPERT)), wspec((D_FF_EXPERT, D_MODEL))],
        out_specs=pl.BlockSpec(memory_space=pl.ANY),
        scratch_shapes=[pltpu.VMEM((MOE_NBUF, MOE_TM * TOK_TILE, LANES), F32),
                        pltpu.VMEM((MOE_NBUF, MOE_TM * TOK_TILE, LANES), F32),
                        pltpu.SemaphoreType.DMA((MOE_NBUF,)), pltpu.SemaphoreType.DMA((MOE_NBUF,))])
    return pl.pallas_call(
        _moe_group_kernel,
        grid_spec=grid_spec,
        out_shape=jax.ShapeDtypeStruct(((MOE_SLOTS + MOE_DUMP) * TOK_TILE, LANES), F32),
        compiler_params=_cparams(("arbitrary",)),
        name="moe_group",
    )(tile_expert, n_used, src_tok, dst_row, h2, w1, w3, w2)


def _moe_plan(expert_ids):
    e_flat = expert_ids.T.reshape(-1)
    order = jnp.argsort(e_flat, stable=True).astype(jnp.int32)
    counts = jnp.sum((e_flat[:, None] == jnp.arange(N_EXPERTS)[None, :]).astype(jnp.int32), axis=0)
    padded = (counts + MOE_TM - 1) // MOE_TM * MOE_TM
    pend = jnp.cumsum(padded)
    pstart = pend - padded
    ustart = jnp.cumsum(counts) - counts
    n_used = pend[-1] // MOE_TM
    tiles = jnp.arange(MOE_TILES, dtype=jnp.int32)
    last_used = jnp.minimum(tiles, n_used - 1)
    tile_expert = jnp.sum((last_used[:, None] * MOE_TM >= pend[None, :]).astype(jnp.int32), axis=1)
    t = jnp.arange(-MOE_LEAD, MOE_PLAN_TILES - MOE_LEAD, dtype=jnp.int32)[:, None]
    r = jnp.arange(MOE_TM, dtype=jnp.int32)[None, :]
    e_t = tile_expert[jnp.clip(t, 0, MOE_TILES - 1)]
    off = t * MOE_TM + r - pstart[e_t]
    valid = (t >= 0) & (t < n_used) & (off < counts[e_t])
    slot = order[jnp.clip(ustart[e_t] + off, 0, MOE_SLOTS - 1)]
    src_tok = jnp.where(valid, slot % N_TOK, 0)
    dst_row = jnp.where(valid, slot, MOE_SLOTS + (t % MOE_NBUF) * MOE_TM + r)
    return (tile_expert.astype(jnp.int32), n_used.reshape(1).astype(jnp.int32),
            src_tok.reshape(-1).astype(jnp.int32), dst_row.reshape(-1).astype(jnp.int32))


def _moe_combine_kernel(x1, y0, y1, route, mod_ref, out_c, out_l):
    gate2 = mod_ref[0][:, 5 * D_MODEL:6 * D_MODEL]
    r = route[...]
    val = x1[...] + gate2 * (r[:, 0:1] * _load_token_tiles(y0, FF_TM) + r[:, 1:2] * _load_token_tiles(y1, FF_TM))
    is_ctx = pl.program_id(0) < SEG // FF_TM

    @pl.when(is_ctx)
    def _():
        out_c[...] = val

    @pl.when(jnp.logical_not(is_ctx))
    def _():
        out_l[...] = val


def _moe_combine(x1, y_slots, route, mod_l):
    nt = N_TOK // FF_TM
    per_seg = SEG // FF_TM
    return pl.pallas_call(
        _moe_combine_kernel,
        grid=(nt,),
        in_specs=[pl.BlockSpec((FF_TM, D_MODEL), lambda i: (i, 0)),
                  pl.BlockSpec((FF_TM * TOK_TILE, LANES), lambda i: (i, 0)),
                  pl.BlockSpec((FF_TM * TOK_TILE, LANES), lambda i: (nt + i, 0)),
                  pl.BlockSpec((FF_TM, LANES), lambda i: (i, 0)),
                  pl.BlockSpec((1, 1, 6 * D_MODEL), lambda i: (i // per_seg, 0, 0))],
        out_specs=[pl.BlockSpec((FF_TM, D_MODEL), lambda i: (jnp.minimum(i, per_seg - 1), 0)),
                   pl.BlockSpec((FF_TM, D_MODEL), lambda i: (jnp.maximum(i - per_seg, 0), 0))],
        out_shape=[jax.ShapeDtypeStruct((SEG, D_MODEL), F32), jax.ShapeDtypeStruct((N_TOK - SEG, D_MODEL), F32)],
        compiler_params=_cparams(("arbitrary",)),
        name="moe_combine",
    )(x1, y_slots, y_slots, route, mod_l)


def _rope_tables():
    t = np.arange(DEC_SEQ)
    row, col = (t // GRID_W).astype(np.float32), (t % GRID_W).astype(np.float32)
    nf = HEAD_DIM // 4
    freqs = np.float32(ROPE_BASE) ** (-np.arange(nf, dtype=np.float32) / np.float32(nf))
    lane = np.arange(LANES) % HEAD_DIM
    fidx = lane % nf
    use_col = (lane // (HEAD_DIM // 2)) == 1
    first = (lane % (HEAD_DIM // 2)) < nf
    pos = np.where(use_col[None, :], col[:, None], row[:, None])
    ang = (pos * freqs[fidx][None, :]).astype(np.float32).astype(np.float64)
    sin = np.sin(ang)
    return (jnp.asarray(np.cos(ang), dtype=F32), jnp.asarray(np.where(first[None, :], -sin, sin), dtype=F32))


def _permute_w_in(w):
    sizes = (256, 256, 256, 256, 8, 8, 256, 128, 128, 256, 256, 256, 256, 256)
    offs = np.concatenate([[0], np.cumsum(sizes)])
    part = lambda i: w[:, offs[i]:offs[i + 1]]
    mq, mk, mv, mo, mi, mf, sq, sk, sv, rx, ry, nq, nk, nv = (part(i) for i in range(14))
    pad = jnp.zeros((w.shape[0], LANES - 16), w.dtype)
    return jnp.concatenate([mq, mk, mv, mo, sq, rx, ry, nq, nk, nv, sk, sv, mi, mf, pad], axis=1)


def _block_diag(w):
    eye = jnp.eye(RG_BLOCKS, dtype=w.dtype)
    return (w[:, :, None, :] * eye[:, None, :, None]).reshape(RG_WIDTH, RG_WIDTH)


def _tile2(g):
    return jnp.concatenate([g, g]).reshape(1, LANES)


def kernel(x_prompt, x_sample, cache_swa_k, cache_swa_v, cache_na_k, cache_na_v, state_mlstm_C, state_mlstm_n, state_mlstm_m, state_rglru_h, c, c_ctx, norm1_g, norm2_g, w_ada, b_ada, w_in, ml_b_i, ml_b_f, ml_hn, sw_qn, sw_kn, sw_sink, rg_conv_w, rg_conv_b, rg_w_r, rg_b_r, rg_w_i, rg_b_i, rg_lam, na_qn, na_kn, na_rpb, w_br, w_mg, b_mg, w_out, ffn_w1, ffn_w3, ffn_w2, moe_wr, moe_br, moe_w1, moe_w3, moe_w2):
    assert DEPTH % 2 == 0
    x_all = (x_prompt.reshape(SEG, D_MODEL), x_sample.reshape(N_TOK - SEG, D_MODEL))
    cvecs = jnp.concatenate([c_ctx[None, :], c, jnp.zeros((8 - 1 - DEC_BATCH, D_MODEL), F32)], axis=0)
    mod = _mod_table(cvecs.T, w_ada, b_ada)
    cos_t, sin_t = _rope_tables()
    nj = 2 * ML_HEADS
    zeros_state = (jnp.zeros((BATCH, nj // 2, LANES, LANES), F32), jnp.zeros((BATCH, nj // 2, LANES, LANES), F32),
                   jnp.zeros((BATCH, nj, LANES), F32), jnp.zeros((BATCH // RG_NSEG, 2, RG_NSEG, RG_WIDTH), F32))
    ctx_out = []
    for l in range(DEPTH):
        mod_l = mod[l].reshape(8, 1, 6 * D_MODEL)
        qk_gains = jnp.stack([_tile2(sw_qn[l])[0], _tile2(sw_kn[l])[0], _tile2(na_qn[l])[0], _tile2(na_kn[l])[0]])
        proj = _inproj(x_all, mod_l, norm1_g[l].reshape(1, D_MODEL), _permute_w_in(w_in[l]).astype(BF16),
                       qk_gains, cos_t, sin_t)
        gate_bias = jnp.concatenate([ml_b_i[l].reshape(-1), ml_b_f[l].reshape(-1),
                                     jnp.zeros((LANES - 2 * nj,), F32)]).reshape(1, LANES)
        hf_c, hb_c, *st_new = _mlstm(proj, gate_bias, *zeros_state[:3], bsz=BATCH, seq=SEQ, row0=0)
        c_new, n_new, m_new = _mlstm_unpack_state(*st_new)
        st_lat = _mlstm_pack_state(state_mlstm_C[:, l].reshape(DEC_BATCH, nj, HEAD_DIM, HEAD_DIM),
                                   state_mlstm_n[:, l].reshape(DEC_BATCH, nj, HEAD_DIM),
                                   state_mlstm_m[:, l].reshape(DEC_BATCH, nj))
        hf_l, hb_l, _, _, _ = _mlstm(proj, gate_bias, *st_lat, bsz=DEC_BATCH, seq=DEC_SEQ, row0=SEG)
        wg = jnp.concatenate([_block_diag(rg_w_r[l, 0]), _block_diag(rg_w_i[l, 0]),
                              _block_diag(rg_w_r[l, 1]), _block_diag(rg_w_i[l, 1])], axis=1).astype(BF16)
        bg = jnp.concatenate([rg_b_r[l, 0], rg_b_i[l, 0], rg_b_r[l, 1], rg_b_i[l, 1]]).reshape(1, 4 * RG_WIDTH)
        rg_args = (rg_conv_w[l], rg_conv_b[l].reshape(1, RG_WIDTH), wg, bg, rg_lam[l])
        oc_c, hl_c = _rglru(proj, *rg_args, zeros_state[3], nblk=BATCH // RG_NSEG, seg_len=SEQ, chained=False, row0=0)
        hl_new = jnp.transpose(hl_c, (0, 2, 1, 3)).reshape(BATCH, 2, RG_WIDTH)
        h0_lat = jnp.broadcast_to(state_rglru_h[:, l][:, :, None, :], (DEC_BATCH, 2, RG_NSEG, RG_WIDTH))
        oc_l, _ = _rglru(proj, *rg_args, h0_lat, nblk=DEC_BATCH, seg_len=DEC_SEQ // RG_NSEG, chained=True, row0=SEG)
        ob_c, od_c = _ctx_attn(proj, sw_sink[l])
        ob_l = _swa(proj, cache_swa_k[:, l].reshape(DEC_BATCH, PAST_LEN, 128),
                    cache_swa_v[:, l].reshape(DEC_BATCH, PAST_LEN, 128), sw_sink[l])
        od_l = _na(proj, cache_na_k[:, l].reshape(DEC_BATCH, PAST_LEN, 256),
                   cache_na_v[:, l].reshape(DEC_BATCH, PAST_LEN, 256), _na_bias(na_rpb[l].reshape(-1)))
        moe_layer = l % 2 == 1
        j = l // 2
        router = (moe_wr[j].T, moe_br[j]) if moe_layer else None
        outs = _merge(x_all, mod_l, norm1_g[l].reshape(1, D_MODEL), norm2_g[l].reshape(1, D_MODEL),
                      (hf_c, hf_l), (hb_c, hb_l), proj, (ob_c, ob_l), (oc_c, oc_l), (od_c, od_l),
                      _tile2(ml_hn[l]), w_mg[l], b_mg[l].reshape(1, -1), w_br[l], w_out[l], router)
        if moe_layer:
            x1, h2, route = outs
            plan = _moe_plan(route[:, 2:4].astype(jnp.int32))
            y_slots = _moe_group(*plan, h2, moe_w1[j], moe_w3[j], moe_w2[j])
            x_all = tuple(_moe_combine(x1, y_slots, route, mod_l))
        else:
            x1, h2 = outs
            x_all = _ffn(h2, x1, mod_l, ffn_w1[j], ffn_w3[j], ffn_w2[j])
        pc = proj[:SEG]
        ctx_out.append(dict(
            sw_k=pc[:, C_SK:C_SK + 128].reshape(BATCH, SEQ, SW_KV_HEADS, HEAD_DIM),
            sw_v=pc[:, C_SV:C_SV + 128].reshape(BATCH, SEQ, SW_KV_HEADS, HEAD_DIM),
            na_k=pc[:, C_NK:C_NK + 256].reshape(BATCH, SEQ, NA_HEADS, HEAD_DIM),
            na_v=pc[:, C_NV:C_NV + 256].reshape(BATCH, SEQ, NA_HEADS, HEAD_DIM),
            ml_C=c_new.reshape(BATCH, 2, ML_HEADS, HEAD_DIM, HEAD_DIM),
            ml_n=n_new.reshape(BATCH, 2, ML_HEADS, HEAD_DIM),
            ml_m=m_new.reshape(BATCH, 2, ML_HEADS),
            rg_h=hl_new))
    stack = lambda name: jnp.stack([t[name] for t in ctx_out], axis=1)
    return (x_all[0].reshape(BATCH, SEQ, D_MODEL), x_all[1].reshape(DEC_BATCH, DEC_SEQ, D_MODEL),
            stack('sw_k'), stack('sw_v'), stack('na_k'), stack('na_v'),
            stack('ml_C'), stack('ml_n'), stack('ml_m'), stack('rg_h'))
```

```python
import functools

import numpy as np
import jax
import jax.numpy as jnp
from jax import lax
from jax.experimental import pallas as pl
from jax.experimental.pallas import tpu as pltpu

F32 = jnp.float32
BF16 = jnp.bfloat16

D_MODEL = 1024
BATCH = 16
SEQ = 256
DEPTH = 2
DEC_BATCH = 2
DEC_SEQ = 4096
PAST_LEN = 256
GRID_W = 64
HEAD_DIM = 64
ML_HEADS = 4
ML_CHUNK = 128
ML_SUB = 4
SW_HEADS = 4
SW_KV_HEADS = 2
SW_WINDOW = 128
RG_WIDTH = 256
RG_BLOCKS = 4
RG_CONV = 4
RG_C = 8.0
NA_HEADS = 4
NA_ROWS = 8
NA_COLS = 16
N_BRANCH = 4
ROPE_BASE = 10000.0
D_FF = 2816
N_EXPERTS = 8
D_FF_EXPERT = 2048
EPS = 1e-6
NEG = -1e30
SCALE = HEAD_DIM ** -0.5

SEG = 4096
N_SEG = 3
N_TOK = N_SEG * SEG
LANES = 128
VMEM_LIMIT = 60 * 1024 * 1024

C_MQ, C_MK, C_MV, C_MO = 0, 256, 512, 768
C_SQ, C_RX, C_RY, C_NQ, C_NK, C_NV = 1024, 1280, 1536, 1792, 2048, 2304
C_SK, C_SV, C_G = 2560, 2688, 2816
P_W = 2944


def _cparams(sem):
    return pltpu.CompilerParams(dimension_semantics=sem, vmem_limit_bytes=VMEM_LIMIT)


def _dot(a, b):
    return jnp.dot(a.astype(BF16), b.astype(BF16), preferred_element_type=F32)


def _dot_nt(a, b):
    return lax.dot_general(a.astype(BF16), b.astype(BF16), (((1,), (1,)), ((), ())),
                           preferred_element_type=F32)


def _dot_tn(a, b):
    return lax.dot_general(a.astype(BF16), b.astype(BF16), (((0,), (0,)), ((), ())),
                           preferred_element_type=F32)


def _split3(x):
    hi = x.astype(BF16)
    r1 = x - hi.astype(F32)
    mid = r1.astype(BF16)
    lo = (r1 - mid.astype(F32)).astype(BF16)
    return hi, mid, lo


def _dot_exact_rhs(a01, x):
    hi, mid, lo = _split3(x)
    d = lambda p: jnp.dot(a01, p, preferred_element_type=F32)
    return d(hi) + d(mid) + d(lo)


def _dot_exact_lhs(x, a01):
    hi, mid, lo = _split3(x)
    d = lambda p: jnp.dot(p, a01, preferred_element_type=F32)
    return d(hi) + d(mid) + d(lo)


def _sigmoid(x):
    return 0.5 * jnp.tanh(0.5 * x) + 0.5


def _rms(x, g):
    return x * lax.rsqrt(jnp.mean(x * x, axis=-1, keepdims=True) + EPS) * g


def _rms_head_pairs(x, g):
    lane = lax.broadcasted_iota(jnp.int32, x.shape, 1)
    left = lane < HEAD_DIM
    sq = x * x
    s0 = jnp.sum(jnp.where(left, sq, 0.0), axis=-1, keepdims=True)
    s1 = jnp.sum(jnp.where(left, 0.0, sq), axis=-1, keepdims=True)
    ms = jnp.where(left, s0, s1) * (1.0 / HEAD_DIM)
    return x * lax.rsqrt(ms + EPS) * g


MOD_TN = 1536
MOD_ROWS = 3


def _mod_kernel(ct_ref, w_ref, b_ref, o_ref):
    ct = ct_ref[...]
    st = ct * jax.nn.sigmoid(ct)
    w = w_ref[0]
    o_ref[...] = jnp.zeros(o_ref.shape, F32)
    for r in range(MOD_ROWS):
        o_ref[0, r:r + 1, :] = jnp.sum(w * st[:, r:r + 1], axis=0, keepdims=True) + b_ref[0]


def _mod_table(cvecs_t, w_ada, b_ada):
    n = 6 * D_MODEL
    return pl.pallas_call(
        _mod_kernel,
        grid=(DEPTH, n // MOD_TN),
        in_specs=[pl.BlockSpec((D_MODEL, 8), lambda l, j: (0, 0)),
                  pl.BlockSpec((1, D_MODEL, MOD_TN), lambda l, j: (l, 0, j)),
                  pl.BlockSpec((1, 1, MOD_TN), lambda l, j: (l, 0, j))],
        out_specs=pl.BlockSpec((1, 8, MOD_TN), lambda l, j: (l, 0, j)),
        out_shape=jax.ShapeDtypeStruct((DEPTH, 8, n), F32),
        compiler_params=_cparams(("arbitrary", "arbitrary")),
        name="adaln_mod",
    )(cvecs_t, w_ada, b_ada.reshape(DEPTH, 1, n))


IN_TM = 512


def _swap16(y):
    lane = lax.broadcasted_iota(jnp.int32, y.shape, 1)
    first = (lane % 32) < 16
    return jnp.where(first, pltpu.roll(y, LANES - 16, 1), pltpu.roll(y, 16, 1))


def _seg_pair_specs(tm, width, lat_row0=0):
    nt = SEG // tm
    lat_off = lat_row0 // tm
    return (pl.BlockSpec((tm, width), lambda s, i: (jnp.minimum(s * nt + i, nt - 1), 0)),
            pl.BlockSpec((tm, width), lambda s, i: (lat_off + jnp.maximum(s * nt + i - nt, 0), 0)))


TOK_TILE = D_MODEL // LANES


def _store_token_tiles(ref, x):
    for s in range(TOK_TILE):
        ref[pl.ds(s, x.shape[0], stride=TOK_TILE), :] = x[:, LANES * s:LANES * (s + 1)]


def _load_token_tiles(ref, n_tok):
    return jnp.concatenate([ref[pl.ds(s, n_tok, stride=TOK_TILE), :] for s in range(TOK_TILE)], axis=1)


def _pick(c_ref, l_ref):
    return jnp.where(pl.program_id(0) == 0, c_ref[...], l_ref[...])


def _x_pair(x):
    return (x, 0) if isinstance(x, tuple) else ((x, x), SEG)


def _inproj_kernel(xc_ref, xl_ref, mod_ref, g_ref, w_ref, qkg_ref, cos_ref, sin_ref, o_ref):
    seg = pl.program_id(0)
    mod = mod_ref[0]
    sh1 = mod[:, 0:D_MODEL]
    sc1 = mod[:, D_MODEL:2 * D_MODEL]
    h = _rms(_pick(xc_ref, xl_ref), g_ref[...]) * (1.0 + sc1) + sh1
    r = jnp.dot(h.astype(BF16), w_ref[...], preferred_element_type=F32)
    o_ref[:, 0:C_SQ] = r[:, 0:C_SQ]
    o_ref[:, C_RX:C_NQ] = r[:, C_RX:C_NQ]
    o_ref[:, C_NV:C_SK] = r[:, C_NV:C_SK]
    o_ref[:, C_SV:P_W] = r[:, C_SV:P_W]
    cos = cos_ref[...]
    sin = sin_ref[...]
    latent = seg > 0

    def rope(y):
        return jnp.where(latent, y * cos + _swap16(y) * sin, y)

    for p in range(2):
        a = C_SQ + LANES * p
        o_ref[:, a:a + LANES] = rope(_rms_head_pairs(r[:, a:a + LANES], qkg_ref[0:1, :]))
    o_ref[:, C_SK:C_SK + LANES] = rope(_rms_head_pairs(r[:, C_SK:C_SK + LANES], qkg_ref[1:2, :]))
    for p in range(2):
        a = C_NQ + LANES * p
        o_ref[:, a:a + LANES] = _rms_head_pairs(r[:, a:a + LANES], qkg_ref[2:3, :])
        a = C_NK + LANES * p
        o_ref[:, a:a + LANES] = _rms_head_pairs(r[:, a:a + LANES], qkg_ref[3:4, :])


def _inproj(x, mod_l, norm1, w_in_p, qk_gains, cos_t, sin_t):
    nt = SEG // IN_TM
    x_pair, lat_row0 = _x_pair(x)
    return pl.pallas_call(
        _inproj_kernel,
        grid=(N_SEG, nt),
        in_specs=[*_seg_pair_specs(IN_TM, D_MODEL, lat_row0),
                  pl.BlockSpec((1, 1, 6 * D_MODEL), lambda s, i: (s, 0, 0)),
                  pl.BlockSpec((1, D_MODEL), lambda s, i: (0, 0)),
                  pl.BlockSpec((D_MODEL, P_W), lambda s, i: (0, 0)),
                  pl.BlockSpec((4, LANES), lambda s, i: (0, 0)),
                  pl.BlockSpec((IN_TM, LANES), lambda s, i: (i, 0)),
                  pl.BlockSpec((IN_TM, LANES), lambda s, i: (i, 0))],
        out_specs=pl.BlockSpec((IN_TM, P_W), lambda s, i: (s * nt + i, 0)),
        out_shape=jax.ShapeDtypeStruct((N_TOK, P_W), F32),
        compiler_params=_cparams(("arbitrary", "arbitrary")),
        name="inproj",
    )(*x_pair, mod_l, norm1, w_in_p, qk_gains, cos_t, sin_t)


def _mlstm_direction(d, rows, q_ref, k_ref, v_ref, g_ref, bias, tri_ref, sel_ref, state):
    ch = ML_CHUNK
    r_io = lax.broadcasted_iota(jnp.int32, (ch, ch), 0)
    c_io = lax.broadcasted_iota(jnp.int32, (ch, ch), 1)
    lower = r_io >= c_io
    upper = r_io <= c_io
    mask = lower if d == 0 else upper
    tri = tri_ref[d]
    tri_t = tri_ref[1 - d]
    left = c_io < HEAD_DIM
    top = r_io < HEAD_DIM
    blockdiag = top == left
    cbs_in, nbs_in, m_old = state
    g = g_ref[rows, :] + bias[...]
    b_cols = _dot_exact_rhs(tri, jax.nn.log_sigmoid(g))
    b3 = jnp.concatenate(_split3(b_cols), axis=1)
    gt = g.T
    li_rows = gt[0:8, :]
    b_rows = _dot_exact_lhs(jax.nn.log_sigmoid(gt[8:16, :]), tri_t)
    a_rows = li_rows - b_rows
    bl = b_rows[:, ch - 1:ch] if d == 0 else b_rows[:, 0:1]
    g_rows = bl - b_rows + li_rows
    m_new = jnp.maximum(bl + m_old, jnp.max(g_rows, axis=1, keepdims=True))
    wk_rows = jnp.exp(g_rows - m_new)
    wp = jnp.exp(bl + m_old - m_new)
    ones_blk = jnp.ones((ch, LANES), BF16)
    left2 = lax.broadcasted_iota(jnp.int32, (ch, 2 * LANES), 1) % LANES < HEAD_DIM
    h_out, c_out, n_out = [], [], []
    for p in range(ML_HEADS // 2):
        lanes = slice(LANES * p, LANES * (p + 1))
        j0 = ML_HEADS * d + 2 * p
        q2 = q_ref[rows, lanes]
        k2t = (k_ref[rows, lanes] * SCALE).T.astype(BF16)
        v2e = jnp.concatenate([v_ref[rows, lanes].astype(BF16), ones_blk], axis=1)
        cb = cbs_in[p]
        nb = nbs_in[p]
        q2b = q2.astype(BF16)
        q_lo = (q2 - q2b.astype(F32)).astype(BF16)
        nb_hi = nb.astype(BF16)
        nb_lo = (nb - nb_hi.astype(F32)).astype(BF16)
        qc = jnp.dot(q2b, cb.astype(BF16), preferred_element_type=F32)
        qn = jnp.dot(jnp.concatenate([q2b, q_lo, q2b], axis=1), jnp.concatenate([nb_hi, nb_hi, nb_lo], axis=0),
                     preferred_element_type=F32)
        b_pair = jnp.dot(b3, sel_ref[2 * d + p], preferred_element_type=F32)
        cbs, sves = [], []
        for i in range(2):
            j = j0 + i
            half = left if i == 0 else jnp.logical_not(left)
            a_mat = jnp.where(mask, a_rows[j:j + 1, :], NEG)
            cvec = jnp.maximum(m_old[j:j + 1, :], jnp.max(a_mat, axis=1, keepdims=True))
            cbro = jnp.broadcast_to(cvec, (ch, ch))
            s = jnp.dot(jnp.where(half, q2b, 0), k2t, preferred_element_type=F32) * jnp.exp(a_mat - cbro)
            s_hi = s.astype(BF16)
            s_lo = (s - s_hi.astype(F32)).astype(BF16)
            sve = jnp.dot(s_hi, v2e, preferred_element_type=F32)
            rs_lo = jnp.dot(s_lo, ones_blk, preferred_element_type=F32)
            sves.append(jnp.concatenate([sve[:, 0:LANES], sve[:, LANES:2 * LANES] + rs_lo], axis=1))
            cbs.append(cbro)
        c_pair = jnp.where(left, cbs[0], cbs[1])
        w_prev = jnp.exp(jnp.where(left, m_old[j0:j0 + 1, :], m_old[j0 + 1:j0 + 2, :]) - c_pair)
        sve = jnp.where(left2, sves[0], sves[1])
        num = w_prev * qc + sve[:, 0:LANES]
        den = w_prev * qn + sve[:, LANES:2 * LANES]
        h_out.append(num / jnp.maximum(jnp.abs(den), jnp.exp(-(c_pair + b_pair))))
        kwt = k2t * jnp.where(top, wk_rows[j0:j0 + 1, :], wk_rows[j0 + 1:j0 + 2, :])
        kwt_hi = kwt.astype(BF16)
        kwt_lo = (kwt - kwt_hi.astype(F32)).astype(BF16)
        kve = jnp.dot(kwt_hi, v2e, preferred_element_type=F32)
        kn = kve[:, LANES:2 * LANES] + jnp.dot(kwt_lo, ones_blk, preferred_element_type=F32)
        wp_pair = jnp.where(top, wp[j0:j0 + 1, :], wp[j0 + 1:j0 + 2, :])
        c_out.append(wp_pair * cb + jnp.where(blockdiag, kve[:, 0:LANES], 0.0))
        n_out.append(wp_pair * nb + jnp.where(blockdiag, kn, 0.0))
    return h_out, (c_out, n_out, m_new)


def _mlstm_kernel(qf, kf, vf, gf, qb, kb, vb, gb, c0, n0, m0, bias, tri_ref, sel_ref,
                  hf, hb, co, no, mo, cbd, nbd, m_s, *, nc, nsub):
    c = pl.program_id(1)

    @pl.when(c == 0)
    def _():
        cbd[...] = c0[0]
        nbd[...] = n0[0]
        m_s[...] = m0[0]

    npair = ML_HEADS // 2
    for d, (refs, h_ref) in enumerate((((qf, kf, vf, gf), hf), ((qb, kb, vb, gb), hb))):
        state = ([cbd[npair * d + p] for p in range(npair)], [nbd[npair * d + p] for p in range(npair)],
                 m_s[:, 0:1])
        subs = range(nsub) if d == 0 else reversed(range(nsub))
        for sub in subs:
            rows = slice(sub * ML_CHUNK, (sub + 1) * ML_CHUNK)
            h_out, state = _mlstm_direction(d, rows, *refs, bias, tri_ref, sel_ref, state)
            for p in range(npair):
                h_ref[rows, LANES * p:LANES * (p + 1)] = h_out[p]
        c_out, n_out, m_new = state
        for p in range(npair):
            cbd[npair * d + p] = c_out[p]
            nbd[npair * d + p] = n_out[p]
        heads = slice(ML_HEADS * d, ML_HEADS * (d + 1))
        m_s[heads, :] = jnp.broadcast_to(m_new[heads, :], (ML_HEADS, LANES))

    @pl.when(c == nc - 1)
    def _():
        co[0] = cbd[...]
        no[0] = nbd[...]
        mo[0] = m_s[...]


def _mlstm_pack_state(c0, n0, m0):
    bsz = c0.shape[0]
    hd = HEAD_DIM
    eye = jnp.eye(2, dtype=F32)[None, None, :, None, :, None]
    cbd = c0.reshape(bsz, ML_HEADS, 2, hd, 1, hd) * eye
    nbd = jnp.broadcast_to(n0.reshape(bsz, ML_HEADS, 2, hd, 1, 1) * eye, cbd.shape)
    to_mat = lambda t: t.reshape(bsz, ML_HEADS, LANES, LANES)
    return to_mat(cbd), to_mat(nbd), jnp.broadcast_to(m0[..., None], m0.shape + (LANES,))


def _mlstm_unpack_state(cbd, nbd, mrow):
    hd = HEAD_DIM
    bsz = cbd.shape[0]
    c = jnp.stack([cbd[:, :, :hd, :hd], cbd[:, :, hd:, hd:]], axis=2).reshape(bsz, 2 * ML_HEADS, hd, hd)
    n = jnp.stack([nbd[:, :, :hd, 0], nbd[:, :, hd:, hd]], axis=2).reshape(bsz, 2 * ML_HEADS, hd)
    return c, n, mrow[:, :, 0]


def _mlstm_tables():
    r = np.arange(ML_CHUNK)
    lower = (r[:, None] >= r[None, :]).astype(np.float32)
    row = np.arange(3 * LANES)[:, None] % LANES
    lane_left = np.arange(LANES)[None, :] < HEAD_DIM
    sel = [row == np.where(lane_left, 8 + ML_HEADS * d + 2 * p, 9 + ML_HEADS * d + 2 * p)
           for d in range(2) for p in range(ML_HEADS // 2)]
    return (jnp.asarray(np.stack([lower, lower.T]), dtype=BF16),
            jnp.asarray(np.stack(sel).astype(np.float32), dtype=BF16))


def _mlstm(proj, gate_bias, c0, n0, m0, *, bsz, seq, row0):
    nsub = min(ML_SUB, seq // ML_CHUNK)
    blk = nsub * ML_CHUNK
    assert seq % blk == 0 and row0 % blk == 0
    nc = seq // blk
    tri, sel = _mlstm_tables()
    base = row0 // blk
    nj = 2 * ML_HEADS
    fw = lambda col: (lambda b, c: (base + b * nc + c, col))
    bw = lambda col: (lambda b, c: (base + b * nc + nc - 1 - c, col))
    qkv = lambda f: [pl.BlockSpec((blk, 256), f(C_MQ // 256)),
                     pl.BlockSpec((blk, 256), f(C_MK // 256)),
                     pl.BlockSpec((blk, 256), f(C_MV // 256)),
                     pl.BlockSpec((blk, LANES), f(C_G // LANES))]
    st_specs = [pl.BlockSpec((1, nj // 2, LANES, LANES), lambda b, c: (b, 0, 0, 0)),
                pl.BlockSpec((1, nj // 2, LANES, LANES), lambda b, c: (b, 0, 0, 0)),
                pl.BlockSpec((1, nj, LANES), lambda b, c: (b, 0, 0))]
    st_shapes = [jax.ShapeDtypeStruct((bsz, nj // 2, LANES, LANES), F32),
                 jax.ShapeDtypeStruct((bsz, nj // 2, LANES, LANES), F32),
                 jax.ShapeDtypeStruct((bsz, nj, LANES), F32)]
    return pl.pallas_call(
        functools.partial(_mlstm_kernel, nc=nc, nsub=nsub),
        grid=(bsz, nc),
        in_specs=qkv(fw) + qkv(bw) + st_specs + [pl.BlockSpec((1, LANES), lambda b, c: (0, 0)),
                                                 pl.BlockSpec(tri.shape, lambda b, c: (0, 0, 0)),
                                                 pl.BlockSpec(sel.shape, lambda b, c: (0, 0, 0))],
        out_specs=[pl.BlockSpec((blk, 256), lambda b, c: (b * nc + c, 0)),
                   pl.BlockSpec((blk, 256), lambda b, c: (b * nc + nc - 1 - c, 0))] + st_specs,
        out_shape=[jax.ShapeDtypeStruct((bsz * seq, 256), F32),
                   jax.ShapeDtypeStruct((bsz * seq, 256), F32)] + st_shapes,
        scratch_shapes=[pltpu.VMEM((nj // 2, LANES, LANES), F32),
                        pltpu.VMEM((nj // 2, LANES, LANES), F32),
                        pltpu.VMEM((nj, LANES), F32)],
        compiler_params=_cparams(("arbitrary", "arbitrary")),
        name="mlstm",
    )(proj, proj, proj, proj, proj, proj, proj, proj, c0, n0, m0, gate_bias, tri, sel)


RG_TC = 256
RG_PAD = 8
RG_NSEG = 8
RG_SKEW = 4


def _rglru_kernel(rx, ry, cw, cb, wg, bg, lam, h0, oc, hl, xpad, af, ab, uf, ub, *, seg_len, chained):
    rows = RG_NSEG * seg_len
    seq_len = rows if chained else seg_len
    halves = RG_WIDTH // LANES
    pitch = seg_len + RG_SKEW
    buf_row = lambda t: (t // seg_len) * pitch + t % seg_len
    xpad[0:RG_PAD, :] = jnp.zeros((RG_PAD, RG_WIDTH), F32)
    xpad[rows + RG_PAD:rows + 2 * RG_PAD, :] = jnp.zeros((RG_PAD, RG_WIDTH), F32)
    xpad[RG_PAD:rows + RG_PAD, :] = rx[...]
    sp = jax.nn.softplus(-lam[...])
    left = (RG_CONV - 1) // 2
    for ci in range(rows // RG_TC):
        s0 = ci * RG_TC
        pos = (s0 + lax.broadcasted_iota(jnp.int32, (RG_TC, RG_WIDTH), 0)) % seq_len
        xc = None
        for j in range(RG_CONV):
            a = RG_PAD + s0 + j - left
            term = xpad[a:a + RG_TC, :] * cw[j:j + 1, :]
            if not chained and j != left:
                term = jnp.where((pos + (j - left) >= 0) & (pos + (j - left) < seq_len), term, 0.0)
            xc = term if xc is None else xc + term
        xc = xc + cb[...]
        pre = _dot(xc, wg[...]) + bg[...]
        for d, (a_ref, u_ref) in enumerate(((af, uf), (ab, ub))):
            o = 2 * RG_WIDTH * d
            r = _sigmoid(pre[:, o:o + RG_WIDTH])
            gi = _sigmoid(pre[:, o + RG_WIDTH:o + 2 * RG_WIDTH])
            log_a = -RG_C * r * sp[d:d + 1, :]
            a_val = jnp.exp(log_a)
            u_val = jnp.sqrt(-jnp.tanh(log_a) * (a_val * a_val + 1.0)) * (gi * xc)
            for hv in range(halves):
                dst = slice(buf_row(s0), buf_row(s0) + RG_TC)
                a_ref[hv, dst, :] = a_val[:, LANES * hv:LANES * (hv + 1)]
                u_ref[hv, dst, :] = u_val[:, LANES * hv:LANES * (hv + 1)]

    def body(s, carry):
        out = []
        for d, (a_ref, u_ref) in enumerate(((af, uf), (ab, ub))):
            step_rows = pl.ds(s if d == 0 else seg_len - 1 - s, RG_NSEG, stride=pitch)
            for hv in range(halves):
                h_loc, prod = carry[2 * (halves * d + hv)], carry[2 * (halves * d + hv) + 1]
                a = a_ref[hv, step_rows, :]
                h_loc = a * h_loc + u_ref[hv, step_rows, :]
                prod = a * prod
                u_ref[hv, step_rows, :] = h_loc
                a_ref[hv, step_rows, :] = prod
                out += [h_loc, prod]
        return tuple(out)

    zero = jnp.zeros((RG_NSEG, LANES), F32)
    one = jnp.ones((RG_NSEG, LANES), F32)
    ends = lax.fori_loop(0, seg_len, body, (zero, one) * (2 * halves), unroll=8)
    seg = lax.broadcasted_iota(jnp.int32, (RG_NSEG, LANES), 0)
    for d, (a_ref, u_ref) in enumerate(((af, uf), (ab, ub))):
        for hv in range(halves):
            lanes = slice(LANES * hv, LANES * (hv + 1))
            h_end, p_end = ends[2 * (halves * d + hv)], ends[2 * (halves * d + hv) + 1]
            h_in = h0[0, d][:, lanes]
            if chained:
                c = h_in[0:1, :]
                h_in = zero
                for k in (range(RG_NSEG) if d == 0 else reversed(range(RG_NSEG))):
                    h_in = jnp.where(seg == k, c, h_in)
                    c = h_end[k:k + 1, :] + p_end[k:k + 1, :] * c
            hl[0, d, :, lanes] = h_end + p_end * h_in
            for k in range(RG_NSEG):
                for ci in range(seg_len // RG_TC):
                    sl = slice(k * pitch + ci * RG_TC, k * pitch + (ci + 1) * RG_TC)
                    u_ref[hv, sl, :] = u_ref[hv, sl, :] + a_ref[hv, sl, :] * h_in[k:k + 1, :]
    for ci in range(rows // RG_TC):
        sl = slice(ci * RG_TC, (ci + 1) * RG_TC)
        for hv in range(halves):
            lanes = slice(LANES * hv, LANES * (hv + 1))
            src = slice(buf_row(ci * RG_TC), buf_row(ci * RG_TC) + RG_TC)
            oc[sl, lanes] = (uf[hv, src, :] + ub[hv, src, :]) * jax.nn.gelu(ry[sl, lanes])


def _rglru(proj, cw, cb, wg, bg, lam, h0, *, nblk, seg_len, chained, row0):
    rows = RG_NSEG * seg_len
    base = row0 // rows
    full = lambda shape: pl.BlockSpec(shape, lambda b: tuple(0 for _ in shape))
    st_spec = pl.BlockSpec((1, 2, RG_NSEG, RG_WIDTH), lambda b: (b, 0, 0, 0))
    return pl.pallas_call(
        functools.partial(_rglru_kernel, seg_len=seg_len, chained=chained),
        grid=(nblk,),
        in_specs=[pl.BlockSpec((rows, RG_WIDTH), lambda b: (base + b, C_RX // RG_WIDTH)),
                  pl.BlockSpec((rows, RG_WIDTH), lambda b: (base + b, C_RY // RG_WIDTH)),
                  full((RG_CONV, RG_WIDTH)), full((1, RG_WIDTH)),
                  full((RG_WIDTH, 4 * RG_WIDTH)), full((1, 4 * RG_WIDTH)), full((2, RG_WIDTH)),
                  st_spec],
        out_specs=[pl.BlockSpec((rows, RG_WIDTH), lambda b: (b, 0)), st_spec],
        out_shape=[jax.ShapeDtypeStruct((nblk * rows, RG_WIDTH), F32),
                   jax.ShapeDtypeStruct((nblk, 2, RG_NSEG, RG_WIDTH), F32)],
        scratch_shapes=[pltpu.VMEM((rows + 2 * RG_PAD, RG_WIDTH), F32)]
        + [pltpu.VMEM((RG_WIDTH // LANES, RG_NSEG * (seg_len + RG_SKEW), LANES), F32) for _ in range(4)],
        compiler_params=_cparams(("arbitrary",)),
        name="rglru",
    )(proj, proj, cw, cb, wg, bg, lam, h0)


def _softmax_pv(scores, values, sink):
    m = functools.reduce(jnp.maximum, [jnp.max(s, axis=-1, keepdims=True) for s in scores])
    if sink is not None:
        m = jnp.maximum(m, sink)
    ps = [jnp.exp(s - m) for s in scores]
    den = functools.reduce(jnp.add, [jnp.sum(p, axis=-1, keepdims=True) for p in ps])
    if sink is not None:
        den = den + jnp.exp(sink - m)
    num = functools.reduce(jnp.add, [_dot(p, v) for p, v in zip(ps, values)])
    return num / den


def _ctx_attn_kernel(sink, sq, sk, sv, nq, nk, nv, ob, od):
    assert SW_HEADS == 4 and SW_KV_HEADS == 2
    left = lax.broadcasted_iota(jnp.int32, (SEQ, LANES), 1) < HEAD_DIM
    first = lax.broadcasted_iota(jnp.int32, (2 * SEQ, 1), 0) < SEQ
    k2 = sk[...].astype(BF16)
    v2 = sv[...].astype(BF16)
    for p in range(SW_KV_HEADS):
        lanes = slice(LANES * p, LANES * (p + 1))
        q2 = sq[:, lanes]
        q_swapped = pltpu.roll(q2, HEAD_DIM, 1)
        kv_half = left if p == 0 else jnp.logical_not(left)
        qs = jnp.concatenate([jnp.where(kv_half, q2 if i == p else q_swapped, 0.0) for i in range(2)], axis=0)
        sink_col = jnp.where(first, sink[2 * p], sink[2 * p + 1])
        res = _softmax_pv([_dot_nt(qs, k2) * SCALE], [v2], sink_col)
        halves = [res[0:SEQ], res[SEQ:2 * SEQ]]
        placed = [halves[i] if i == p else pltpu.roll(halves[i], HEAD_DIM, 1) for i in range(2)]
        ob[:, lanes] = jnp.where(left, placed[0], placed[1])
    for p in range(NA_HEADS // 2):
        lanes = slice(LANES * p, LANES * (p + 1))
        q2 = nq[:, lanes]
        k2 = nk[:, lanes].astype(BF16)
        v2 = nv[:, lanes].astype(BF16)
        res = [_softmax_pv([_dot_nt(jnp.where(left if i == 0 else jnp.logical_not(left), q2, 0.0), k2) * SCALE],
                           [v2], None) for i in range(2)]
        od[:, lanes] = jnp.where(left, res[0], res[1])


def _ctx_attn(proj, sink):
    blk = lambda w, col: pl.BlockSpec((SEQ, w), lambda b: (b, col))
    return pl.pallas_call(
        _ctx_attn_kernel,
        grid=(BATCH,),
        in_specs=[pl.BlockSpec(memory_space=pltpu.SMEM),
                  blk(256, C_SQ // 256), blk(128, C_SK // 128), blk(128, C_SV // 128),
                  blk(256, C_NQ // 256), blk(256, C_NK // 256), blk(256, C_NV // 256)],
        out_specs=[pl.BlockSpec((SEQ, 256), lambda b: (b, 0)),
                   pl.BlockSpec((SEQ, 256), lambda b: (b, 0))],
        out_shape=[jax.ShapeDtypeStruct((SEG, 256), F32), jax.ShapeDtypeStruct((SEG, 256), F32)],
        compiler_params=_cparams(("arbitrary",)),
        name="ctx_attn",
    )(sink, proj, proj, proj, proj, proj, proj)


SW_QB = 128
SW_SPAN = SW_QB + 2 * SW_WINDOW


def _swa_kernel(sink, q, k, v, kc, vc, ob):
    assert SW_HEADS == 4 and SW_KV_HEADS == 2
    n = pl.program_id(1)
    ws = jnp.clip((n - 1) * SW_QB, 0, DEC_SEQ - SW_SPAN)
    ws = pl.multiple_of(ws, SW_QB)
    row = lax.broadcasted_iota(jnp.int32, (2 * SW_QB, SW_SPAN), 0)
    qpos = n * SW_QB + row % SW_QB
    kpos = ws + lax.broadcasted_iota(jnp.int32, (2 * SW_QB, SW_SPAN), 1)
    valid = jnp.abs(qpos - kpos) <= SW_WINDOW
    left = lax.broadcasted_iota(jnp.int32, (SW_QB, LANES), 1) < HEAD_DIM
    first = lax.broadcasted_iota(jnp.int32, (2 * SW_QB, 1), 0) < SW_QB
    k2 = k[pl.ds(ws, SW_SPAN), :].astype(BF16)
    v2 = v[pl.ds(ws, SW_SPAN), :].astype(BF16)
    kc2 = kc[0].astype(BF16)
    vc2 = vc[0].astype(BF16)
    for p in range(SW_KV_HEADS):
        lanes = slice(LANES * p, LANES * (p + 1))
        q2 = q[:, lanes]
        q_swapped = pltpu.roll(q2, HEAD_DIM, 1)
        kv_half = left if p == 0 else jnp.logical_not(left)
        qs = jnp.concatenate([jnp.where(kv_half, q2 if i == p else q_swapped, 0.0) for i in range(2)], axis=0)
        s_loc = jnp.where(valid, _dot_nt(qs, k2) * SCALE, NEG)
        s_ctx = _dot_nt(qs, kc2) * SCALE
        sink_col = jnp.where(first, sink[2 * p], sink[2 * p + 1])
        res = _softmax_pv([s_loc, s_ctx], [v2, vc2], sink_col)
        halves = [res[0:SW_QB], res[SW_QB:2 * SW_QB]]
        placed = [halves[i] if i == p else pltpu.roll(halves[i], HEAD_DIM, 1) for i in range(2)]
        ob[:, lanes] = jnp.where(left, placed[0], placed[1])


def _swa(proj, kc, vc, sink):
    nq = DEC_SEQ // SW_QB
    qbase = SEG // SW_QB
    return pl.pallas_call(
        _swa_kernel,
        grid=(DEC_BATCH, nq),
        in_specs=[pl.BlockSpec(memory_space=pltpu.SMEM),
                  pl.BlockSpec((SW_QB, 256), lambda b, n: (qbase + b * nq + n, C_SQ // 256)),
                  pl.BlockSpec((DEC_SEQ, 128), lambda b, n: (1 + b, C_SK // 128)),
                  pl.BlockSpec((DEC_SEQ, 128), lambda b, n: (1 + b, C_SV // 128)),
                  pl.BlockSpec((1, PAST_LEN, 128), lambda b, n: (b, 0, 0)),
                  pl.BlockSpec((1, PAST_LEN, 128), lambda b, n: (b, 0, 0))],
        out_specs=pl.BlockSpec((SW_QB, 256), lambda b, n: (b * nq + n, 0)),
        out_shape=jax.ShapeDtypeStruct((DEC_BATCH * DEC_SEQ, 256), F32),
        compiler_params=_cparams(("arbitrary", "arbitrary")),
        name="swa",
    )(sink, proj, proj, proj, kc, vc)


NA_RPB_R = 2 * NA_ROWS - 1
NA_RPB_C = 2 * NA_COLS - 1
GRID_ROWS = DEC_SEQ // GRID_W
NA_RB = 4
NA_UW = 12
NA_NQ = NA_RB * GRID_W
NA_NKEY = NA_UW * GRID_W
NA_NBLK = GRID_ROWS // NA_RB
NA_CASES = ((0, 0), (NA_RB, 0), (GRID_ROWS - NA_RB, GRID_ROWS - NA_UW))


def _na_row_start(qrow):
    return min(max(qrow - NA_ROWS // 2, 0), GRID_ROWS - NA_ROWS)


def _na_bias_kernel(rpb, out):
    h = pl.program_id(0)
    qc = lax.broadcasted_iota(jnp.int32, (GRID_W, GRID_W), 0)
    kc = lax.broadcasted_iota(jnp.int32, (GRID_W, GRID_W), 1)
    dc = jnp.clip(kc - qc, -(NA_COLS - 1), NA_COLS - 1) + NA_COLS - 1
    lo = jnp.clip(qc - NA_COLS // 2, 0, GRID_W - NA_COLS)
    valid = (kc >= lo) & (kc < lo + NA_COLS)
    tiles = []
    for dr in range(NA_RPB_R):
        t = jnp.zeros((GRID_W, GRID_W), F32)
        for j in range(NA_RPB_C):
            t = jnp.where(dc == j, rpb[(h * NA_RPB_R + dr) * NA_RPB_C + j], t)
        tiles.append(jnp.where(valid, t, NEG))
    outside = jnp.full((GRID_W, GRID_W), NEG, F32)
    for case, (q0, k0) in enumerate(NA_CASES):
        for a in range(NA_RB):
            r0 = _na_row_start(q0 + a)
            for i in range(NA_UW):
                inside = r0 <= k0 + i < r0 + NA_ROWS
                tile = tiles[k0 + i - (q0 + a) + NA_ROWS - 1] if inside else outside
                out[0, case, GRID_W * a:GRID_W * (a + 1), GRID_W * i:GRID_W * (i + 1)] = tile


def _na_bias(rpb_flat):
    shape = (NA_HEADS, len(NA_CASES), NA_NQ, NA_NKEY)
    return pl.pallas_call(
        _na_bias_kernel,
        grid=(NA_HEADS,),
        in_specs=[pl.BlockSpec(memory_space=pltpu.SMEM)],
        out_specs=pl.BlockSpec((1,) + shape[1:], lambda h: (h, 0, 0, 0)),
        out_shape=jax.ShapeDtypeStruct(shape, F32),
        compiler_params=_cparams(("arbitrary",)),
        name="na_bias",
    )(rpb_flat)


def _na_kernel(q, k, v, kc, vc, bias, od):
    blk = pl.program_id(1)
    case = jnp.where(blk == 0, 0, jnp.where(blk == NA_NBLK - 1, 2, 1))
    u0 = jnp.clip(blk * NA_RB - NA_ROWS // 2, 0, GRID_ROWS - NA_UW)
    k0 = pl.multiple_of(u0 * GRID_W, GRID_W)
    left = lax.broadcasted_iota(jnp.int32, (NA_NQ, LANES), 1) < HEAD_DIM
    for p in range(NA_HEADS // 2):
        lanes = slice(LANES * p, LANES * (p + 1))
        q2 = q[:, lanes]
        k2 = k[pl.ds(k0, NA_NKEY), lanes].astype(BF16)
        v2 = v[pl.ds(k0, NA_NKEY), lanes].astype(BF16)
        kc2 = kc[0, :, lanes].astype(BF16)
        vc2 = vc[0, :, lanes].astype(BF16)
        res = []
        for i in range(2):
            qm = jnp.where(left if i == 0 else jnp.logical_not(left), q2, 0.0)
            s_loc = _dot_nt(qm, k2) * SCALE + bias[2 * p + i, pl.ds(case, 1)][0]
            s_ctx = _dot_nt(qm, kc2) * SCALE
            res.append(_softmax_pv([s_loc, s_ctx], [v2, vc2], None))
        od[:, lanes] = jnp.where(left, res[0], res[1])


def _na(proj, kc, vc, bias):
    qbase = SEG // NA_NQ
    return pl.pallas_call(
        _na_kernel,
        grid=(DEC_BATCH, NA_NBLK),
        in_specs=[pl.BlockSpec((NA_NQ, 256), lambda b, r: (qbase + b * NA_NBLK + r, C_NQ // 256)),
                  pl.BlockSpec((DEC_SEQ, 256), lambda b, r: (1 + b, C_NK // 256)),
                  pl.BlockSpec((DEC_SEQ, 256), lambda b, r: (1 + b, C_NV // 256)),
                  pl.BlockSpec((1, PAST_LEN, 256), lambda b, r: (b, 0, 0)),
                  pl.BlockSpec((1, PAST_LEN, 256), lambda b, r: (b, 0, 0)),
                  pl.BlockSpec((NA_HEADS, len(NA_CASES), NA_NQ, NA_NKEY), lambda b, r: (0, 0, 0, 0))],
        out_specs=pl.BlockSpec((NA_NQ, 256), lambda b, r: (b * NA_NBLK + r, 0)),
        out_shape=jax.ShapeDtypeStruct((DEC_BATCH * DEC_SEQ, 256), F32),
        compiler_params=_cparams(("arbitrary", "arbitrary")),
        name="na",
    )(proj, proj, proj, kc, vc, bias)


MG_ROWS = 256
MG_SUB = 2
MG_TM = MG_SUB * MG_ROWS


def _merge_kernel(xc_ref, xl_ref, mod_ref, g1_ref, g2_ref, hf_c, hf_l, hb_c, hb_l, mo, ob_c, ob_l, oc_c, oc_l,
                  od_c, od_l, hn, wmg, bmg, wbr, wout, *rest, moe):
    if moe:
        wrt, br, x1_ref, h2_ref, route_ref = rest
    else:
        x1_ref, h2_ref = rest
    ctx = pl.program_id(0) == 0
    mod = mod_ref[0]
    chunk = lambda i: mod[:, i * D_MODEL:(i + 1) * D_MODEL]
    sh1, sc1, gate1, sh2, sc2 = chunk(0), chunk(1), chunk(2), chunk(3), chunk(4)

    def row_group(rows):
        pick = lambda c_ref, l_ref: jnp.where(ctx, c_ref[rows, :], l_ref[rows, :])
        x = pick(xc_ref, xl_ref)
        h = (_rms(x, g1_ref[...]) * (1.0 + sc1) + sh1).astype(BF16)
        hsum = pick(hf_c, hf_l) + pick(hb_c, hb_l)
        out_a = jnp.concatenate(
            [_rms_head_pairs(hsum[:, LANES * p:LANES * (p + 1)], hn[...]) for p in range(2)], axis=-1)
        out_a = out_a * jax.nn.sigmoid(mo[rows, :])
        acc = None
        for n, br_val in enumerate((out_a, pick(ob_c, ob_l), pick(oc_c, oc_l), pick(od_c, od_l))):
            gate = jax.nn.sigmoid(jnp.dot(h, wmg[:, n * D_MODEL:(n + 1) * D_MODEL].astype(BF16),
                                          preferred_element_type=F32)
                                  + bmg[:, n * D_MODEL:(n + 1) * D_MODEL])
            term = gate * jnp.dot(br_val.astype(BF16), wbr[n].astype(BF16), preferred_element_type=F32)
            acc = term if acc is None else acc + term
        y = jnp.dot(acc.astype(BF16), wout[...].astype(BF16), preferred_element_type=F32)
        x1 = x + gate1 * y
        x1_ref[rows, :] = x1
        h2 = _rms(x1, g2_ref[...]) * (1.0 + sc2) + sh2
        if moe:
            _store_token_tiles(h2_ref.at[pl.ds(rows.start * TOK_TILE, MG_ROWS * TOK_TILE)], h2)
        else:
            h2_ref[rows, :] = h2.astype(BF16)
        if moe:
            logit = [jnp.sum(h2 * wrt[e:e + 1, :], axis=-1, keepdims=True) + br[e] for e in range(N_EXPERTS)]
            v1, i1 = logit[0], jnp.zeros(logit[0].shape, jnp.int32)
            for e in range(1, N_EXPERTS):
                better = logit[e] > v1
                v1 = jnp.where(better, logit[e], v1)
                i1 = jnp.where(better, e, i1)
            v2, i2 = jnp.full(v1.shape, -jnp.inf, F32), jnp.zeros(v1.shape, jnp.int32)
            for e in range(N_EXPERTS):
                better = (i1 != e) & (logit[e] > v2)
                v2 = jnp.where(better, logit[e], v2)
                i2 = jnp.where(better, e, i2)
            e2 = jnp.exp(v2 - v1)
            den = 1.0 + e2
            lane = lax.broadcasted_iota(jnp.int32, (MG_ROWS, LANES), 1)
            route = jnp.where(lane == 0, 1.0 / den, 0.0) + jnp.where(lane == 1, e2 / den, 0.0)
            route = route + jnp.where(lane == 2, i1.astype(F32), 0.0) + jnp.where(lane == 3, i2.astype(F32), 0.0)
            route_ref[rows, :] = route

    for sub in range(MG_SUB):
        row_group(slice(sub * MG_ROWS, (sub + 1) * MG_ROWS))


def _merge(x, mod_l, g1, g2, hf, hb, proj, ob, oc, od, hn, wmg, bmg, wbr, wout, router=None):
    nt = SEG // MG_TM
    moe = router is not None
    x_pair, lat_row0 = _x_pair(x)
    row = lambda w: pl.BlockSpec((MG_TM, w), lambda s, i: (s * nt + i, 0))
    ctx_blk, lat_blk = _seg_pair_specs(MG_TM, 256)
    full = lambda shape: pl.BlockSpec(shape, lambda s, i: tuple(0 for _ in shape), pipeline_mode=pl.Buffered(1))
    in_specs = [*_seg_pair_specs(MG_TM, D_MODEL, lat_row0),
                pl.BlockSpec((1, 1, 6 * D_MODEL), lambda s, i: (s, 0, 0)),
                full((1, D_MODEL)), full((1, D_MODEL)),
                ctx_blk, lat_blk, ctx_blk, lat_blk,
                pl.BlockSpec((MG_TM, 256), lambda s, i: (s * nt + i, C_MO // 256)),
                ctx_blk, lat_blk, ctx_blk, lat_blk, ctx_blk, lat_blk,
                full((1, LANES)), full((D_MODEL, N_BRANCH * D_MODEL)), full((1, N_BRANCH * D_MODEL)),
                full((N_BRANCH, 256, D_MODEL)), full((D_MODEL, D_MODEL))]
    args = [*x_pair, mod_l, g1, g2, *hf, *hb, proj, *ob, *oc, *od, hn, wmg, bmg, wbr, wout]
    if moe:
        h2_spec = pl.BlockSpec((MG_TM * TOK_TILE, LANES), lambda s, i: (s * nt + i, 0))
        h2_shape = jax.ShapeDtypeStruct((N_TOK * TOK_TILE, LANES), F32)
    else:
        h2_spec, h2_shape = row(D_MODEL), jax.ShapeDtypeStruct((N_TOK, D_MODEL), BF16)
    out_specs = [row(D_MODEL), h2_spec]
    out_shape = [jax.ShapeDtypeStruct((N_TOK, D_MODEL), F32), h2_shape]
    if moe:
        in_specs += [full((N_EXPERTS, D_MODEL)), pl.BlockSpec(memory_space=pltpu.SMEM)]
        args += list(router)
        out_specs.append(row(LANES))
        out_shape.append(jax.ShapeDtypeStruct((N_TOK, LANES), F32))
    return pl.pallas_call(
        functools.partial(_merge_kernel, moe=moe),
        grid=(N_SEG, nt),
        in_specs=in_specs,
        out_specs=out_specs,
        out_shape=out_shape,
        compiler_params=_cparams(("arbitrary", "arbitrary")),
        name="merge",
    )(*args)


FF_TM = 512


def _ffn_kernel(h2, w1, w3, w2, x1, mod_ref, out):
    h = h2[...]
    a = jnp.dot(h, w1[...].astype(BF16), preferred_element_type=F32)
    b = jnp.dot(h, w3[...].astype(BF16), preferred_element_type=F32)
    act = (jax.nn.silu(a) * b).astype(BF16)
    gate2 = mod_ref[0][:, 5 * D_MODEL:6 * D_MODEL]
    out[...] = x1[...] + gate2 * jnp.dot(act, w2[...].astype(BF16), preferred_element_type=F32)


def _ffn(h2, x1, mod_l, w1, w3, w2):
    nt = N_TOK // FF_TM
    per_seg = SEG // FF_TM
    resident = lambda shape: pl.BlockSpec(shape, lambda i: (0, 0), pipeline_mode=pl.Buffered(1))
    return pl.pallas_call(
        _ffn_kernel,
        grid=(nt,),
        in_specs=[pl.BlockSpec((FF_TM, D_MODEL), lambda i: (i, 0)),
                  resident((D_MODEL, D_FF)), resident((D_MODEL, D_FF)), resident((D_FF, D_MODEL)),
                  pl.BlockSpec((FF_TM, D_MODEL), lambda i: (i, 0)),
                  pl.BlockSpec((1, 1, 6 * D_MODEL), lambda i: (i // per_seg, 0, 0))],
        out_specs=pl.BlockSpec((FF_TM, D_MODEL), lambda i: (i, 0)),
        out_shape=jax.ShapeDtypeStruct((N_TOK, D_MODEL), F32),
        compiler_params=_cparams(("arbitrary",)),
        name="ffn",
    )(h2, w1, w3, w2, x1, mod_l)


MOE_TM = 256
MOE_SLOTS = 2 * N_TOK
MOE_TILES = MOE_SLOTS // MOE_TM + N_EXPERTS
MOE_NBUF = 3
MOE_STEPS = MOE_TILES + MOE_NBUF
MOE_DUMP = MOE_NBUF * MOE_TM
MOE_LEAD = 1
MOE_PLAN_TILES = MOE_LEAD + MOE_TILES + 2
MOE_FCHUNKS = 1
MOE_UNROLL = 8


def _moe_group_kernel(texp, nused, src_tok, dst_row, h2_hbm, w1, w3, w2, y_hbm, xs, ys, sem_in, sem_out):
    del texp
    i = pl.program_id(0)
    n_used = nused[0]
    buf = i % MOE_NBUF
    buf_next = (i + 2) % MOE_NBUF

    def tile_rows(t):
        start = t * TOK_TILE
        return pl.ds(start if isinstance(start, int) else pl.multiple_of(start, TOK_TILE), TOK_TILE)

    def gather_copy(tile, b, r):
        tok = src_tok[(tile + MOE_LEAD) * MOE_TM + r]
        return pltpu.make_async_copy(h2_hbm.at[tile_rows(tok)], xs.at[b, tile_rows(r)], sem_in.at[b])

    def scatter_copy(tile, b, r):
        dst = dst_row[(tile + MOE_LEAD) * MOE_TM + r]
        return pltpu.make_async_copy(ys.at[b, tile_rows(r)], y_hbm.at[tile_rows(dst)], sem_out.at[b])

    def start_rows_loop(make_copy, tile, b):
        def body(r, carry):
            make_copy(tile, b, r).start()
            return carry
        lax.fori_loop(0, MOE_TM, body, 0, unroll=MOE_UNROLL)

    def wait_tile(b, gather):
        if gather:
            pltpu.make_async_copy(h2_hbm.at[pl.ds(0, MOE_TM * TOK_TILE)], xs.at[b], sem_in.at[b]).wait()
        else:
            pltpu.make_async_copy(ys.at[b], y_hbm.at[pl.ds(0, MOE_TM * TOK_TILE)], sem_out.at[b]).wait()

    @pl.when(i == 0)
    def _():
        xs[...] = jnp.zeros(xs.shape, F32)
        ys[...] = jnp.zeros(ys.shape, F32)
        for b in range(MOE_NBUF):
            fill = pltpu.make_async_copy(ys.at[b], y_hbm.at[pl.ds((MOE_SLOTS + b * MOE_TM) * TOK_TILE, MOE_TM * TOK_TILE)],
                                         sem_out.at[b])
            fill.start()
            fill.wait()
        start_rows_loop(gather_copy, 0, 0)
        start_rows_loop(gather_copy, 1, 1)

    @pl.when(i <= n_used + 1)
    def _():
        wait_tile(buf, True)

    @pl.when((i >= 2) & (i <= n_used + 2))
    def _():
        wait_tile(buf, False)

    @pl.when(i < n_used)
    def _():
        x = _load_token_tiles(xs.at[buf], MOE_TM).astype(BF16)
        fc = D_FF_EXPERT // MOE_FCHUNKS
        rc = MOE_TM // MOE_FCHUNKS
        y = None
        for c in range(MOE_FCHUNKS):
            for r in range(c * rc, (c + 1) * rc):
                gather_copy(i + 2, buf_next, r).start(priority=r % 2)
                scatter_copy(i - 1, buf_next, r).start(priority=r % 2)
            a = jnp.dot(x, w1[0, :, c * fc:(c + 1) * fc].astype(BF16), preferred_element_type=F32)
            b = jnp.dot(x, w3[0, :, c * fc:(c + 1) * fc].astype(BF16), preferred_element_type=F32)
            act = (jax.nn.silu(a) * b).astype(BF16)
            part = jnp.dot(act, w2[0, c * fc:(c + 1) * fc, :].astype(BF16), preferred_element_type=F32)
            y = part if y is None else y + part
        _store_token_tiles(ys.at[buf], y)

    @pl.when(i == n_used)
    def _():
        start_rows_loop(scatter_copy, i - 1, buf_next)


def _moe_group(tile_expert, n_used, src_tok, dst_row, h2, w1, w3, w2):
    wspec = lambda shape: pl.BlockSpec((1,) + shape, lambda i, texp, *_: (texp[jnp.minimum(i, MOE_TILES - 1)], 0, 0))
    grid_spec = pltpu.PrefetchScalarGridSpec(
        num_scalar_prefetch=4,
        grid=(MOE_STEPS,),
        in_specs=[pl.BlockSpec(memory_space=pl.ANY),
                  wspec((D_MODEL, D_FF_EXPERT)), wspec((D_MODEL, D_FF_EXPERT)), wspec((D_FF_EXPERT, D_MODEL))],
        out_specs=pl.BlockSpec(memory_space=pl.ANY),
        scratch_shapes=[pltpu.VMEM((MOE_NBUF, MOE_TM * TOK_TILE, LANES), F32),
                        pltpu.VMEM((MOE_NBUF, MOE_TM * TOK_TILE, LANES), F32),
                        pltpu.SemaphoreType.DMA((MOE_NBUF,)), pltpu.SemaphoreType.DMA((MOE_NBUF,))])
    return pl.pallas_call(
        _moe_group_kernel,
        grid_spec=grid_spec,
        out_shape=jax.ShapeDtypeStruct(((MOE_SLOTS + MOE_DUMP) * TOK_TILE, LANES), F32),
        compiler_params=_cparams(("arbitrary",)),
        name="moe_group",
    )(tile_expert, n_used, src_tok, dst_row, h2, w1, w3, w2)


def _moe_plan(expert_ids):
    e_flat = expert_ids.T.reshape(-1)
    order = jnp.argsort(e_flat, stable=True).astype(jnp.int32)
    counts = jnp.sum((e_flat[:, None] == jnp.arange(N_EXPERTS)[None, :]).astype(jnp.int32), axis=0)
    padded = (counts + MOE_TM - 1) // MOE_TM * MOE_TM
    pend = jnp.cumsum(padded)
    pstart = pend - padded
    ustart = jnp.cumsum(counts) - counts
    n_used = pend[-1] // MOE_TM
    tiles = jnp.arange(MOE_TILES, dtype=jnp.int32)
    last_used = jnp.minimum(tiles, n_used - 1)
    tile_expert = jnp.sum((last_used[:, None] * MOE_TM >= pend[None, :]).astype(jnp.int32), axis=1)
    t = jnp.arange(-MOE_LEAD, MOE_PLAN_TILES - MOE_LEAD, dtype=jnp.int32)[:, None]
    r = jnp.arange(MOE_TM, dtype=jnp.int32)[None, :]
    e_t = tile_expert[jnp.clip(t, 0, MOE_TILES - 1)]
    off = t * MOE_TM + r - pstart[e_t]
    valid = (t >= 0) & (t < n_used) & (off < counts[e_t])
    slot = order[jnp.clip(ustart[e_t] + off, 0, MOE_SLOTS - 1)]
    src_tok = jnp.where(valid, slot % N_TOK, 0)
    dst_row = jnp.where(valid, slot, MOE_SLOTS + (t % MOE_NBUF) * MOE_TM + r)
    return (tile_expert.astype(jnp.int32), n_used.reshape(1).astype(jnp.int32),
            src_tok.reshape(-1).astype(jnp.int32), dst_row.reshape(-1).astype(jnp.int32))


def _moe_combine_kernel(x1, y0, y1, route, mod_ref, out_c, out_l):
    gate2 = mod_ref[0][:, 5 * D_MODEL:6 * D_MODEL]
    r = route[...]
    val = x1[...] + gate2 * (r[:, 0:1] * _load_token_tiles(y0, FF_TM) + r[:, 1:2] * _load_token_tiles(y1, FF_TM))
    is_ctx = pl.program_id(0) < SEG // FF_TM

    @pl.when(is_ctx)
    def _():
        out_c[...] = val

    @pl.when(jnp.logical_not(is_ctx))
    def _():
        out_l[...] = val


def _moe_combine(x1, y_slots, route, mod_l):
    nt = N_TOK // FF_TM
    per_seg = SEG // FF_TM
    return pl.pallas_call(
        _moe_combine_kernel,
        grid=(nt,),
        in_specs=[pl.BlockSpec((FF_TM, D_MODEL), lambda i: (i, 0)),
                  pl.BlockSpec((FF_TM * TOK_TILE, LANES), lambda i: (i, 0)),
                  pl.BlockSpec((FF_TM * TOK_TILE, LANES), lambda i: (nt + i, 0)),
                  pl.BlockSpec((FF_TM, LANES), lambda i: (i, 0)),
                  pl.BlockSpec((1, 1, 6 * D_MODEL), lambda i: (i // per_seg, 0, 0))],
        out_specs=[pl.BlockSpec((FF_TM, D_MODEL), lambda i: (jnp.minimum(i, per_seg - 1), 0)),
                   pl.BlockSpec((FF_TM, D_MODEL), lambda i: (jnp.maximum(i - per_seg, 0), 0))],
        out_shape=[jax.ShapeDtypeStruct((SEG, D_MODEL), F32), jax.ShapeDtypeStruct((N_TOK - SEG, D_MODEL), F32)],
        compiler_params=_cparams(("arbitrary",)),
        name="moe_combine",
    )(x1, y_slots, y_slots, route, mod_l)


def _rope_tables():
    t = np.arange(DEC_SEQ)
    row, col = (t // GRID_W).astype(np.float32), (t % GRID_W).astype(np.float32)
    nf = HEAD_DIM // 4
    freqs = np.float32(ROPE_BASE) ** (-np.arange(nf, dtype=np.float32) / np.float32(nf))
    lane = np.arange(LANES) % HEAD_DIM
    fidx = lane % nf
    use_col = (lane // (HEAD_DIM // 2)) == 1
    first = (lane % (HEAD_DIM // 2)) < nf
    pos = np.where(use_col[None, :], col[:, None], row[:, None])
    ang = (pos * freqs[fidx][None, :]).astype(np.float32).astype(np.float64)
    sin = np.sin(ang)
    return (jnp.asarray(np.cos(ang), dtype=F32), jnp.asarray(np.where(first[None, :], -sin, sin), dtype=F32))


def _permute_w_in(w):
    sizes = (256, 256, 256, 256, 8, 8, 256, 128, 128, 256, 256, 256, 256, 256)
    offs = np.concatenate([[0], np.cumsum(sizes)])
    part = lambda i: w[:, offs[i]:offs[i + 1]]
    mq, mk, mv, mo, mi, mf, sq, sk, sv, rx, ry, nq, nk, nv = (part(i) for i in range(14))
    pad = jnp.zeros((w.shape[0], LANES - 16), w.dtype)
    return jnp.concatenate([mq, mk, mv, mo, sq, rx, ry, nq, nk, nv, sk, sv, mi, mf, pad], axis=1)


def _block_diag(w):
    eye = jnp.eye(RG_BLOCKS, dtype=w.dtype)
    return (w[:, :, None, :] * eye[:, None, :, None]).reshape(RG_WIDTH, RG_WIDTH)


def _tile2(g):
    return jnp.concatenate([g, g]).reshape(1, LANES)


def kernel(x_prompt, x_sample, cache_swa_k, cache_swa_v, cache_na_k, cache_na_v, state_mlstm_C, state_mlstm_n, state_mlstm_m, state_rglru_h, c, c_ctx, norm1_g, norm2_g, w_ada, b_ada, w_in, ml_b_i, ml_b_f, ml_hn, sw_qn, sw_kn, sw_sink, rg_conv_w, rg_conv_b, rg_w_r, rg_b_r, rg_w_i, rg_b_i, rg_lam, na_qn, na_kn, na_rpb, w_br, w_mg, b_mg, w_out, ffn_w1, ffn_w3, ffn_w2, moe_wr, moe_br, moe_w1, moe_w3, moe_w2):
    assert DEPTH % 2 == 0
    x_all = (x_prompt.reshape(SEG, D_MODEL), x_sample.reshape(N_TOK - SEG, D_MODEL))
    cvecs = jnp.concatenate([c_ctx[None, :], c, jnp.zeros((8 - 1 - DEC_BATCH, D_MODEL), F32)], axis=0)
    mod = _mod_table(cvecs.T, w_ada, b_ada)
    cos_t, sin_t = _rope_tables()
    nj = 2 * ML_HEADS
    zeros_state = (jnp.zeros((BATCH, nj // 2, LANES, LANES), F32), jnp.zeros((BATCH, nj // 2, LANES, LANES), F32),
                   jnp.zeros((BATCH, nj, LANES), F32), jnp.zeros((BATCH // RG_NSEG, 2, RG_NSEG, RG_WIDTH), F32))
    ctx_out = []
    for l in range(DEPTH):
        mod_l = mod[l].reshape(8, 1, 6 * D_MODEL)
        qk_gains = jnp.stack([_tile2(sw_qn[l])[0], _tile2(sw_kn[l])[0], _tile2(na_qn[l])[0], _tile2(na_kn[l])[0]])
        proj = _inproj(x_all, mod_l, norm1_g[l].reshape(1, D_MODEL), _permute_w_in(w_in[l]).astype(BF16),
                       qk_gains, cos_t, sin_t)
        gate_bias = jnp.concatenate([ml_b_i[l].reshape(-1), ml_b_f[l].reshape(-1),
                                     jnp.zeros((LANES - 2 * nj,), F32)]).reshape(1, LANES)
        hf_c, hb_c, *st_new = _mlstm(proj, gate_bias, *zeros_state[:3], bsz=BATCH, seq=SEQ, row0=0)
        c_new, n_new, m_new = _mlstm_unpack_state(*st_new)
        st_lat = _mlstm_pack_state(state_mlstm_C[:, l].reshape(DEC_BATCH, nj, HEAD_DIM, HEAD_DIM),
                                   state_mlstm_n[:, l].reshape(DEC_BATCH, nj, HEAD_DIM),
                                   state_mlstm_m[:, l].reshape(DEC_BATCH, nj))
        hf_l, hb_l, _, _, _ = _mlstm(proj, gate_bias, *st_lat, bsz=DEC_BATCH, seq=DEC_SEQ, row0=SEG)
        wg = jnp.concatenate([_block_diag(rg_w_r[l, 0]), _block_diag(rg_w_i[l, 0]),
                              _block_diag(rg_w_r[l, 1]), _block_diag(rg_w_i[l, 1])], axis=1).astype(BF16)
        bg = jnp.concatenate([rg_b_r[l, 0], rg_b_i[l, 0], rg_b_r[l, 1], rg_b_i[l, 1]]).reshape(1, 4 * RG_WIDTH)
        rg_args = (rg_conv_w[l], rg_conv_b[l].reshape(1, RG_WIDTH), wg, bg, rg_lam[l])
        oc_c, hl_c = _rglru(proj, *rg_args, zeros_state[3], nblk=BATCH // RG_NSEG, seg_len=SEQ, chained=False, row0=0)
        hl_new = jnp.transpose(hl_c, (0, 2, 1, 3)).reshape(BATCH, 2, RG_WIDTH)
        h0_lat = jnp.broadcast_to(state_rglru_h[:, l][:, :, None, :], (DEC_BATCH, 2, RG_NSEG, RG_WIDTH))
        oc_l, _ = _rglru(proj, *rg_args, h0_lat, nblk=DEC_BATCH, seg_len=DEC_SEQ // RG_NSEG, chained=True, row0=SEG)
        ob_c, od_c = _ctx_attn(proj, sw_sink[l])
        ob_l = _swa(proj, cache_swa_k[:, l].reshape(DEC_BATCH, PAST_LEN, 128),
                    cache_swa_v[:, l].reshape(DEC_BATCH, PAST_LEN, 128), sw_sink[l])
        od_l = _na(proj, cache_na_k[:, l].reshape(DEC_BATCH, PAST_LEN, 256),
                   cache_na_v[:, l].reshape(DEC_BATCH, PAST_LEN, 256), _na_bias(na_rpb[l].reshape(-1)))
        moe_layer = l % 2 == 1
        j = l // 2
        router = (moe_wr[j].T, moe_br[j]) if moe_layer else None
        outs = _merge(x_all, mod_l, norm1_g[l].reshape(1, D_MODEL), norm2_g[l].reshape(1, D_MODEL),
                      (hf_c, hf_l), (hb_c, hb_l), proj, (ob_c, ob_l), (oc_c, oc_l), (od_c, od_l),
                      _tile2(ml_hn[l]), w_mg[l], b_mg[l].reshape(1, -1), w_br[l], w_out[l], router)
        if moe_layer:
            x1, h2, route = outs
            plan = _moe_plan(route[:, 2:4].astype(jnp.int32))
            y_slots = _moe_group(*plan, h2, moe_w1[j], moe_w3[j], moe_w2[j])
            x_all = tuple(_moe_combine(x1, y_slots, route, mod_l))
        else:
            x1, h2 = outs
            x_all = _ffn(h2, x1, mod_l, ffn_w1[j], ffn_w3[j], ffn_w2[j])
        pc = proj[:SEG]
        ctx_out.append(dict(
            sw_k=pc[:, C_SK:C_SK + 128].reshape(BATCH, SEQ, SW_KV_HEADS, HEAD_DIM),
            sw_v=pc[:, C_SV:C_SV + 128].reshape(BATCH, SEQ, SW_KV_HEADS, HEAD_DIM),
            na_k=pc[:, C_NK:C_NK + 256].reshape(BATCH, SEQ, NA_HEADS, HEAD_DIM),
            na_v=pc[:, C_NV:C_NV + 256].reshape(BATCH, SEQ, NA_HEADS, HEAD_DIM),
            ml_C=c_new.reshape(BATCH, 2, ML_HEADS, HEAD_DIM, HEAD_DIM),
            ml_n=n_new.reshape(BATCH, 2, ML_HEADS, HEAD_DIM),
            ml_m=m_new.reshape(BATCH, 2, ML_HEADS),
            rg_h=hl_new))
    stack = lambda name: jnp.stack([t[name] for t in ctx_out], axis=1)
    return (x_all[0].reshape(BATCH, SEQ, D_MODEL), x_all[1].reshape(DEC_BATCH, DEC_SEQ, D_MODEL),
            stack('sw_k'), stack('sw_v'), stack('na_k'), stack('na_v'),
            stack('ml_C'), stack('ml_n'), stack('ml_m'), stack('rg_h'))
```
